```python
import numpy as np
import jax
import jax.numpy as jnp
from jax import lax

D_MODEL = 1024
BATCH = 8
SEQ = 16384
DEPTH = 4

HEAD_DIM = 64
N_MIXERS = 2
A_Q_HEADS = 16
A_KV_HEADS = 4
A_HALF_WINDOW = 128
B_GROUPS = ((128, 1), (512, 4), (2048, 16))
B_Q_HEADS = 8
B_KV_HEADS = 2
D_FF = -(-8 * D_MODEL // (3 * 256)) * 256
A_QKV = (A_Q_HEADS + 2 * A_KV_HEADS) * HEAD_DIM
B_QKV = len(B_GROUPS) * (B_Q_HEADS + 2 * B_KV_HEADS) * HEAD_DIM
A_OUT = A_Q_HEADS * HEAD_DIM
B_OUT = B_Q_HEADS * HEAD_DIM
N_A_LAYERS = (DEPTH + 1) // 2
N_B_LAYERS = DEPTH // 2
RMS_EPS = 1e-6
NEG = -1e30

kernel_name = "hybrid_window_dilated_alibi_encoder"


def rmsnorm(x, g):
    xf = x.astype(jnp.float32)
    y = xf * lax.rsqrt(jnp.mean(xf * xf, axis=-1, keepdims=True) + RMS_EPS)
    return (y * g.astype(jnp.float32)).astype(x.dtype)


def alibi_slopes(n):
    return jnp.asarray(2.0 ** (-8.0 * np.arange(1, n + 1) / n), dtype=jnp.float32)


def banded_attention(q, k, v, half_window, slopes, dist_scale, sink=None):
    n, L, hkv, g, dh = q.shape
    blk = half_window
    nb = -(-L // blk)
    lp = nb * blk
    qb = jnp.pad(q, ((0, 0), (0, lp - L), (0, 0), (0, 0), (0, 0))).reshape(n, nb, blk, hkv, g, dh)
    pad_kv = ((0, 0), (blk, lp - L + blk), (0, 0), (0, 0))
    kb = jnp.pad(k, pad_kv).reshape(n, nb + 2, blk, hkv, dh)
    vb = jnp.pad(v, pad_kv).reshape(n, nb + 2, blk, hkv, dh)
    kw = jnp.concatenate([kb[:, :-2], kb[:, 1:-1], kb[:, 2:]], axis=2)
    vw = jnp.concatenate([vb[:, :-2], vb[:, 1:-1], vb[:, 2:]], axis=2)
    rel = jnp.arange(3 * blk)[None, :] - blk - jnp.arange(blk)[:, None]
    kpos = jnp.arange(nb)[:, None] * blk - blk + jnp.arange(3 * blk)[None, :]
    mask = (jnp.abs(rel) <= half_window)[None] & ((kpos >= 0) & (kpos < L))[:, None, :]
    dist = (dist_scale * jnp.abs(rel)).astype(jnp.float32)
    bias = -slopes.astype(jnp.float32)[:, :, None, None] * dist
    s = jnp.einsum('nbqhgd,nbkhd->nbhgqk', qb, kw,
                   preferred_element_type=jnp.float32) * (dh ** -0.5) + bias
    s = jnp.where(mask[None, :, None, None], s, NEG)
    m = s.max(axis=-1)
    if sink is not None:
        sk = sink.astype(jnp.float32)[:, :, None]
        m = jnp.maximum(m, sk)
    p = jnp.exp(s - m[..., None])
    l = p.sum(axis=-1)
    if sink is not None:
        l = l + jnp.exp(sk - m)
    o = jnp.einsum('nbhgqk,nbkhd->nbqhgd', p, vw.astype(jnp.float32))
    o = o / jnp.moveaxis(l, -1, 2)[..., None]
    lse = jnp.moveaxis(m + jnp.log(l), -1, 2)
    o = o.reshape(n, lp, hkv, g, dh)[:, :L].astype(q.dtype)
    lse = lse.reshape(n, lp, hkv, g)[:, :L]
    return o, lse


def to_strided(t, dil):
    b, s = t.shape[:2]
    rest = t.shape[2:]
    t = t.reshape(b, s // dil, dil, *rest)
    return jnp.moveaxis(t, 2, 1).reshape(b * dil, s // dil, *rest)


def from_strided(t, b, dil):
    l = t.shape[1]
    rest = t.shape[2:]
    t = t.reshape(b, dil, l, *rest)
    return jnp.moveaxis(t, 1, 2).reshape(b, l * dil, *rest)


def windowed_gqa(h, w_in, w_out, sink):
    b, s, _ = h.shape
    gq = A_Q_HEADS // A_KV_HEADS
    dq, dk = A_Q_HEADS * HEAD_DIM, A_KV_HEADS * HEAD_DIM
    qkv = h @ w_in
    q = qkv[..., :dq].reshape(b, s, A_KV_HEADS, gq, HEAD_DIM)
    k = qkv[..., dq:dq + dk].reshape(b, s, A_KV_HEADS, HEAD_DIM)
    v = qkv[..., dq + dk:].reshape(b, s, A_KV_HEADS, HEAD_DIM)
    slopes = alibi_slopes(A_Q_HEADS).reshape(A_KV_HEADS, gq)
    o, _ = banded_attention(q, k, v, A_HALF_WINDOW, slopes, 1, sink.reshape(A_KV_HEADS, gq))
    return o.reshape(b, s, A_OUT) @ w_out


def dilated_attention(h, w_in, w_out):
    b, s, _ = h.shape
    ng = len(B_GROUPS)
    gq = B_Q_HEADS // B_KV_HEADS
    dq, dk = ng * B_Q_HEADS * HEAD_DIM, ng * B_KV_HEADS * HEAD_DIM
    qkv = h @ w_in
    q = qkv[..., :dq].reshape(b, s, ng, B_KV_HEADS, gq, HEAD_DIM)
    k = qkv[..., dq:dq + dk].reshape(b, s, ng, B_KV_HEADS, HEAD_DIM)
    v = qkv[..., dq + dk:].reshape(b, s, ng, B_KV_HEADS, HEAD_DIM)
    slopes = alibi_slopes(ng * B_Q_HEADS).reshape(ng, B_KV_HEADS, gq)
    outs, lses = [], []
    for gi, (window, dil) in enumerate(B_GROUPS):
        half = window // (2 * dil)
        o, lse = banded_attention(to_strided(q[:, :, gi], dil), to_strided(k[:, :, gi], dil),
                                  to_strided(v[:, :, gi], dil), half, slopes[gi], dil)
        outs.append(from_strided(o, b, dil))
        lses.append(from_strided(lse, b, dil))
    alpha = jax.nn.softmax(jnp.stack(lses), axis=0)
    o = jnp.einsum('nbshg,nbshgd->bshgd', alpha,
                   jnp.stack(outs).astype(jnp.float32)).astype(h.dtype)
    return o.reshape(b, s, B_OUT) @ w_out


def swiglu(h, w_in, w_out):
    gu = h @ w_in
    gate, up = jnp.split(gu, 2, axis=-1)
    return (jax.nn.silu(gate) * up) @ w_out


def _fwd_setup_inputs(seed: int = 0) -> dict:
    key = jax.random.key(seed)
    ks = jax.random.split(key, 16)
    D = D_MODEL
    nrm = lambda k, shape, scale: jax.random.normal(k, shape, jnp.float32) * scale
    return {
        "x": nrm(ks[0], (BATCH, SEQ, D), 1.0),
        "c": nrm(ks[1], (BATCH, D), 1.0),
        "ada_w": nrm(ks[2], (DEPTH, D, 6 * D), 0.5 * D ** -0.5),
        "ada_b": nrm(ks[3], (DEPTH, 6 * D), 0.1),
        "norm_mix": 1.0 + nrm(ks[4], (DEPTH, D), 0.05),
        "norm_ffn": 1.0 + nrm(ks[5], (DEPTH, D), 0.05),
        "ffn_w_in": nrm(ks[6], (DEPTH, D, 2 * D_FF), D ** -0.5),
        "ffn_w_out": nrm(ks[7], (DEPTH, D_FF, D), D_FF ** -0.5),
        "a_w_in": nrm(ks[8], (N_A_LAYERS, D, A_QKV), D ** -0.5),
        "a_w_out": nrm(ks[9], (N_A_LAYERS, A_OUT, D), A_OUT ** -0.5),
        "a_sink": nrm(ks[10], (N_A_LAYERS, A_Q_HEADS), 0.5),
        "b_w_in": nrm(ks[11], (N_B_LAYERS, D, B_QKV), D ** -0.5),
        "b_w_out": nrm(ks[12], (N_B_LAYERS, B_OUT, D), B_OUT ** -0.5),
        "final_norm": 1.0 + nrm(ks[13], (D,), 0.05),
    }


def _fwd_reference(x, c, ada_w, ada_b, norm_mix, norm_ffn, ffn_w_in, ffn_w_out,
              a_w_in, a_w_out, a_sink, b_w_in, b_w_out, final_norm):
    cond = jax.nn.silu(c)
    for i in range(DEPTH):
        mod = (cond @ ada_w[i] + ada_b[i])[:, None, :]
        sh1, sc1, g1, sh2, sc2, g2 = jnp.split(mod, 6, axis=-1)
        h = rmsnorm(x, norm_mix[i]) * (1 + sc1) + sh1
        j = i // N_MIXERS
        if i % N_MIXERS == 0:
            y = windowed_gqa(h, a_w_in[j], a_w_out[j], a_sink[j])
        else:
            y = dilated_attention(h, b_w_in[j], b_w_out[j])
        x = x + g1 * y
        h = rmsnorm(x, norm_ffn[i]) * (1 + sc2) + sh2
        x = x + g2 * swiglu(h, ffn_w_in[i], ffn_w_out[i])
    return rmsnorm(x, final_norm)


import jax as _jax
import jax.numpy as _jnp

TWIN_FORMAT = 'train_step'
FWD_PARAMS = ['x', 'c', 'ada_w', 'ada_b', 'norm_mix', 'norm_ffn', 'ffn_w_in', 'ffn_w_out', 'a_w_in', 'a_w_out', 'a_sink', 'b_w_in', 'b_w_out', 'final_norm']
TWIN_WEIGHTS = ['ada_w', 'ada_b', 'norm_mix', 'norm_ffn', 'ffn_w_in', 'ffn_w_out', 'a_w_in', 'a_w_out', 'a_sink', 'b_w_in', 'b_w_out', 'final_norm']
TWIN_DIFF_INPUT = 'x'
TWIN_INPUTS = ['x', 'c', 'ada_w', 'ada_b', 'norm_mix', 'norm_ffn', 'ffn_w_in', 'ffn_w_out', 'a_w_in', 'a_w_out', 'a_sink', 'b_w_in', 'b_w_out', 'final_norm', 'loss_target', 'm_ada_w', 'm_ada_b', 'm_norm_mix', 'm_norm_ffn', 'm_ffn_w_in', 'm_ffn_w_out', 'm_a_w_in', 'm_a_w_out', 'm_a_sink', 'm_b_w_in', 'm_b_w_out', 'm_final_norm', 'v_ada_w', 'v_ada_b', 'v_norm_mix', 'v_norm_ffn', 'v_ffn_w_in', 'v_ffn_w_out', 'v_a_w_in', 'v_a_w_out', 'v_a_sink', 'v_b_w_in', 'v_b_w_out', 'v_final_norm']
TWIN_OUTPUTS = ['loss', 'grad_x', 'grad_ada_w', 'grad_ada_b', 'grad_norm_mix', 'grad_norm_ffn', 'grad_ffn_w_in', 'grad_ffn_w_out', 'grad_a_w_in', 'grad_a_w_out', 'grad_a_sink', 'grad_b_w_in', 'grad_b_w_out', 'grad_final_norm', 'delta_ada_w', 'delta_ada_b', 'delta_norm_mix', 'delta_norm_ffn', 'delta_ffn_w_in', 'delta_ffn_w_out', 'delta_a_w_in', 'delta_a_w_out', 'delta_a_sink', 'delta_b_w_in', 'delta_b_w_out', 'delta_final_norm', 'new_m_ada_w', 'new_m_ada_b', 'new_m_norm_mix', 'new_m_norm_ffn', 'new_m_ffn_w_in', 'new_m_ffn_w_out', 'new_m_a_w_in', 'new_m_a_w_out', 'new_m_a_sink', 'new_m_b_w_in', 'new_m_b_w_out', 'new_m_final_norm', 'new_v_ada_w', 'new_v_ada_b', 'new_v_norm_mix', 'new_v_norm_ffn', 'new_v_ffn_w_in', 'new_v_ffn_w_out', 'new_v_a_w_in', 'new_v_a_w_out', 'new_v_a_sink', 'new_v_b_w_in', 'new_v_b_w_out', 'new_v_final_norm']
TWIN_LEAF_KINDS = {'loss': 'loss', 'grad_x': 'grad_x', 'grad_ada_w': 'grad_w', 'grad_ada_b': 'grad_w', 'grad_norm_mix': 'grad_w', 'grad_norm_ffn': 'grad_w', 'grad_ffn_w_in': 'grad_w', 'grad_ffn_w_out': 'grad_w', 'grad_a_w_in': 'grad_w', 'grad_a_w_out': 'grad_w', 'grad_a_sink': 'grad_w', 'grad_b_w_in': 'grad_w', 'grad_b_w_out': 'grad_w', 'grad_final_norm': 'grad_w', 'delta_ada_w': 'delta_w', 'delta_ada_b': 'delta_w', 'delta_norm_mix': 'delta_w', 'delta_norm_ffn': 'delta_w', 'delta_ffn_w_in': 'delta_w', 'delta_ffn_w_out': 'delta_w', 'delta_a_w_in': 'delta_w', 'delta_a_w_out': 'delta_w', 'delta_a_sink': 'delta_w', 'delta_b_w_in': 'delta_w', 'delta_b_w_out': 'delta_w', 'delta_final_norm': 'delta_w', 'new_m_ada_w': 'new_m', 'new_m_ada_b': 'new_m', 'new_m_norm_mix': 'new_m', 'new_m_norm_ffn': 'new_m', 'new_m_ffn_w_in': 'new_m', 'new_m_ffn_w_out': 'new_m', 'new_m_a_w_in': 'new_m', 'new_m_a_w_out': 'new_m', 'new_m_a_sink': 'new_m', 'new_m_b_w_in': 'new_m', 'new_m_b_w_out': 'new_m', 'new_m_final_norm': 'new_m', 'new_v_ada_w': 'new_v', 'new_v_ada_b': 'new_v', 'new_v_norm_mix': 'new_v', 'new_v_norm_ffn': 'new_v', 'new_v_ffn_w_in': 'new_v', 'new_v_ffn_w_out': 'new_v', 'new_v_a_w_in': 'new_v', 'new_v_a_w_out': 'new_v', 'new_v_a_sink': 'new_v', 'new_v_b_w_in': 'new_v', 'new_v_b_w_out': 'new_v', 'new_v_final_norm': 'new_v'}


def _forward(args):
    return _fwd_reference(*[args[k] for k in FWD_PARAMS])


def _output_shape():
    def fwd():
        inp = _fwd_setup_inputs(0)
        return _fwd_reference(*[inp[k] for k in FWD_PARAMS])
    out = _jax.eval_shape(fwd)
    return out.shape, out.dtype

N_MICROBATCH = 1
ADAM_LR = 0.001
ADAM_B1 = 0.9
ADAM_B2 = 0.999
ADAM_EPS = 1e-08
ADAM_WD = 0.01
ADAM_STEP = 10
PER_EXAMPLE_BATCH_AXIS = {'x': 0, 'c': 0, 'loss_target': 0}
SHARED_INPUTS = []
_WEIGHT_DTYPES = {'ada_w': _jnp.float32, 'ada_b': _jnp.float32, 'norm_mix': _jnp.float32, 'norm_ffn': _jnp.float32, 'ffn_w_in': _jnp.float32, 'ffn_w_out': _jnp.float32, 'a_w_in': _jnp.float32, 'a_w_out': _jnp.float32, 'a_sink': _jnp.float32, 'b_w_in': _jnp.float32, 'b_w_out': _jnp.float32, 'final_norm': _jnp.float32}
MOMENT_SCALE = {'ada_w': 1.190406e-01, 'ada_b': 2.518889e-01, 'norm_mix': 5.598853e-02, 'norm_ffn': 1.120181e-01, 'ffn_w_in': 4.958305e-02, 'ffn_w_out': 8.126633e-02, 'a_w_in': 6.002471e-02, 'a_w_out': 6.379596e-02, 'a_sink': 4.169011e-02, 'b_w_in': 4.297705e-02, 'b_w_out': 4.159735e-02, 'final_norm': 1.281568e+02}


def _to_microbatches(a, axis):
    t = _jnp.moveaxis(a, axis, 0)
    t = t.reshape((N_MICROBATCH, t.shape[0] // N_MICROBATCH) + t.shape[1:])
    return _jnp.moveaxis(t, 1, axis + 1)


def setup_inputs(seed: int = 0) -> dict:
    inp = _fwd_setup_inputs(seed)
    key = _jax.random.fold_in(_jax.random.key(seed), 7919)
    shape, _ = _output_shape()
    out = dict(inp)
    out["loss_target"] = _jax.random.normal(_jax.random.fold_in(key, 0), shape, _jnp.float32)
    for i, name in enumerate(TWIN_WEIGHTS):
        w = inp[name].astype(_jnp.float32)
        if MOMENT_SCALE is None:
            s = _jnp.sqrt(_jnp.mean(_jnp.square(w)) + 1e-30)
        else:
            s = MOMENT_SCALE[name]
        km, kv = _jax.random.split(_jax.random.fold_in(key, i + 1))
        out[name] = w
        out["m_" + name] = s * _jax.random.normal(km, w.shape, _jnp.float32)
        out["v_" + name] = (s * s) * _jax.random.uniform(kv, w.shape, _jnp.float32, 0.5, 1.5)
    if N_MICROBATCH > 1:
        for name, axis in PER_EXAMPLE_BATCH_AXIS.items():
            out[name] = _to_microbatches(out[name], axis)
    return {'x': out['x'], 'c': out['c'], 'ada_w': out['ada_w'], 'ada_b': out['ada_b'], 'norm_mix': out['norm_mix'], 'norm_ffn': out['norm_ffn'], 'ffn_w_in': out['ffn_w_in'], 'ffn_w_out': out['ffn_w_out'], 'a_w_in': out['a_w_in'], 'a_w_out': out['a_w_out'], 'a_sink': out['a_sink'], 'b_w_in': out['b_w_in'], 'b_w_out': out['b_w_out'], 'final_norm': out['final_norm'], 'loss_target': out['loss_target'], 'm_ada_w': out['m_ada_w'], 'm_ada_b': out['m_ada_b'], 'm_norm_mix': out['m_norm_mix'], 'm_norm_ffn': out['m_norm_ffn'], 'm_ffn_w_in': out['m_ffn_w_in'], 'm_ffn_w_out': out['m_ffn_w_out'], 'm_a_w_in': out['m_a_w_in'], 'm_a_w_out': out['m_a_w_out'], 'm_a_sink': out['m_a_sink'], 'm_b_w_in': out['m_b_w_in'], 'm_b_w_out': out['m_b_w_out'], 'm_final_norm': out['m_final_norm'], 'v_ada_w': out['v_ada_w'], 'v_ada_b': out['v_ada_b'], 'v_norm_mix': out['v_norm_mix'], 'v_norm_ffn': out['v_norm_ffn'], 'v_ffn_w_in': out['v_ffn_w_in'], 'v_ffn_w_out': out['v_ffn_w_out'], 'v_a_w_in': out['v_a_w_in'], 'v_a_w_out': out['v_a_w_out'], 'v_a_sink': out['v_a_sink'], 'v_b_w_in': out['v_b_w_in'], 'v_b_w_out': out['v_b_w_out'], 'v_final_norm': out['v_final_norm']}


def _loss(weights, diff, rest, loss_target):
    with _jax.named_scope("forward"):
        args = {**rest, TWIN_DIFF_INPUT: diff, **{k: w.astype(_WEIGHT_DTYPES[k]) for k, w in weights.items()}}
        y = _forward(args)
    with _jax.named_scope("loss_head"):
        err = _jnp.square(y.astype(_jnp.float32) - loss_target)
        return 0.5 * _jnp.sum(_jnp.mean(err, axis=-1)) if err.ndim else 0.5 * err


def _adamw(w, g, m, v):
    m = ADAM_B1 * m + (1.0 - ADAM_B1) * g
    v = ADAM_B2 * v + (1.0 - ADAM_B2) * _jnp.square(g)
    m_hat = m / (1.0 - ADAM_B1 ** ADAM_STEP)
    v_hat = v / (1.0 - ADAM_B2 ** ADAM_STEP)
    delta = -ADAM_LR * (m_hat / (_jnp.sqrt(v_hat) + ADAM_EPS) + ADAM_WD * w)
    return delta, m, v


def reference(x, c, ada_w, ada_b, norm_mix, norm_ffn, ffn_w_in, ffn_w_out, a_w_in, a_w_out, a_sink, b_w_in, b_w_out, final_norm, loss_target, m_ada_w, m_ada_b, m_norm_mix, m_norm_ffn, m_ffn_w_in, m_ffn_w_out, m_a_w_in, m_a_w_out, m_a_sink, m_b_w_in, m_b_w_out, m_final_norm, v_ada_w, v_ada_b, v_norm_mix, v_norm_ffn, v_ffn_w_in, v_ffn_w_out, v_a_w_in, v_a_w_out, v_a_sink, v_b_w_in, v_b_w_out, v_final_norm):
    given = dict(x=x, c=c, ada_w=ada_w, ada_b=ada_b, norm_mix=norm_mix, norm_ffn=norm_ffn, ffn_w_in=ffn_w_in, ffn_w_out=ffn_w_out, a_w_in=a_w_in, a_w_out=a_w_out, a_sink=a_sink, b_w_in=b_w_in, b_w_out=b_w_out, final_norm=final_norm, loss_target=loss_target, m_ada_w=m_ada_w, m_ada_b=m_ada_b, m_norm_mix=m_norm_mix, m_norm_ffn=m_norm_ffn, m_ffn_w_in=m_ffn_w_in, m_ffn_w_out=m_ffn_w_out, m_a_w_in=m_a_w_in, m_a_w_out=m_a_w_out, m_a_sink=m_a_sink, m_b_w_in=m_b_w_in, m_b_w_out=m_b_w_out, m_final_norm=m_final_norm, v_ada_w=v_ada_w, v_ada_b=v_ada_b, v_norm_mix=v_norm_mix, v_norm_ffn=v_norm_ffn, v_ffn_w_in=v_ffn_w_in, v_ffn_w_out=v_ffn_w_out, v_a_w_in=v_a_w_in, v_a_w_out=v_a_w_out, v_a_sink=v_a_sink, v_b_w_in=v_b_w_in, v_b_w_out=v_b_w_out, v_final_norm=v_final_norm)
    weights = {n: given[n] for n in TWIN_WEIGHTS}
    shared = {n: given[n] for n in SHARED_INPUTS}
    per_example = {n: given[n] for n in ['x', 'c']}
    grad_fn = _jax.value_and_grad(_loss, argnums=(0, 1))

    def one_microbatch(ex, loss_target):
        ex = dict(ex)
        diff = ex.pop(TWIN_DIFF_INPUT)
        return grad_fn(weights, diff, {**shared, **ex}, loss_target)

    if N_MICROBATCH == 1:
        loss, (grad_w, grad_x) = one_microbatch(per_example, given["loss_target"])
    else:
        def body(carry, xs):
            loss_sum, grad_sum = carry
            l_k, (gw_k, gx_k) = one_microbatch(xs[0], xs[1])
            with _jax.named_scope("update"):
                return (loss_sum + l_k, _jax.tree.map(_jnp.add, grad_sum, gw_k)), gx_k

        init = (_jnp.zeros((), _jnp.float32), _jax.tree.map(_jnp.zeros_like, weights))
        (loss, grad_w), grad_x = _jax.lax.scan(body, init, (per_example, given["loss_target"]))
    with _jax.named_scope("update"):
        delta_w, new_m, new_v = {}, {}, {}
        for n in TWIN_WEIGHTS:
            delta_w[n], new_m[n], new_v[n] = _adamw(weights[n], grad_w[n], given["m_" + n], given["v_" + n])
    return (loss, grad_x, *[grad_w[n] for n in TWIN_WEIGHTS], *[delta_w[n] for n in TWIN_WEIGHTS],
            *[new_m[n] for n in TWIN_WEIGHTS], *[new_v[n] for n in TWIN_WEIGHTS])
```

```python
import functools
import math

import numpy as np
import jax
import jax.numpy as jnp
from jax import lax
from jax.experimental import pallas as pl
from jax.experimental.pallas import tpu as pltpu

f32 = jnp.float32
bf16 = jnp.bfloat16

D = 1024
DH = 64
GQ = 4
DEPTH = 4
F = 2816
FT = 256
A_QKV, A_OUT = 1536, 1024
B_QKV, B_OUT = 2304, 512
B_GROUPS = ((128, 1), (512, 4), (2048, 16))
RMS_EPS = 1e-6
NEG = -1e30
LR, B1, B2, ADAM_EPS, WD, STEP = 0.001, 0.9, 0.999, 1e-08, 0.01, 10
N_DEV = 8
STAT_ROWS = 40
MESH = pl.DeviceIdType.MESH
ANY = pl.BlockSpec(memory_space=pl.ANY)


def _pcall(body, **kw):
    return pl.pallas_call(body, **kw)


def _params(*sem):
    return pltpu.CompilerParams(dimension_semantics=sem, vmem_limit_bytes=56 * 1024 * 1024)


def _row_tile(s, want=1024):
    return want if s % want == 0 else s


def mm_norm(x, nw, sc, sh, w, *, swiglu, name):
    s, d = x.shape
    n = w.shape[1]
    tm = _row_tile(s)
    tn = 2 * FT if swiglu else n // 3

    def body(x_ref, nw_ref, sc_ref, sh_ref, w_ref, h_ref, y_ref, *rest):
        @pl.when(pl.program_id(1) == 0)
        def _():
            xv = x_ref[...]
            r = lax.rsqrt(jnp.mean(xv * xv, axis=-1, keepdims=True) + RMS_EPS)
            h_ref[...] = ((xv * r * nw_ref[...]) * (1.0 + sc_ref[...]) + sh_ref[...]).astype(bf16)

        res = jnp.dot(h_ref[...], w_ref[...], preferred_element_type=f32)
        y_ref[...] = res.astype(bf16)
        if swiglu:
            g, u = res[:, :FT], res[:, FT:]
            rest[0][...] = (g * jax.nn.sigmoid(g) * u).astype(bf16)

    vec = pl.BlockSpec((1, d), lambda i, j: (0, 0))
    out_shape = [jax.ShapeDtypeStruct((s, d), bf16), jax.ShapeDtypeStruct((s, n), bf16)]
    out_specs = [pl.BlockSpec((tm, d), lambda i, j: (i, 0)), pl.BlockSpec((tm, tn), lambda i, j: (i, j))]
    if swiglu:
        out_shape.append(jax.ShapeDtypeStruct((s, n // 2), bf16))
        out_specs.append(pl.BlockSpec((tm, FT), lambda i, j: (i, j)))
    return _pcall(
        body, name=name, grid=(s // tm, n // tn),
        in_specs=[pl.BlockSpec((tm, d), lambda i, j: (i, 0)), vec, vec, vec, pl.BlockSpec((d, tn), lambda i, j: (0, j))],
        out_specs=out_specs, out_shape=out_shape, compiler_params=_params("parallel", "arbitrary"),
    )(x, nw, sc, sh, w)


def mm_resid(a, w, xres, g, *, name):
    s, k = a.shape
    n = w.shape[1]
    tm, tn = _row_tile(s), 512

    def body(a_ref, w_ref, x_ref, g_ref, o_ref):
        o_ref[...] = x_ref[...] + g_ref[...] * jnp.dot(a_ref[...], w_ref[...], preferred_element_type=f32)

    return _pcall(
        body, name=name, grid=(s // tm, n // tn),
        in_specs=[pl.BlockSpec((tm, k), lambda i, j: (i, 0)), pl.BlockSpec((k, tn), lambda i, j: (0, j)),
                  pl.BlockSpec((tm, tn), lambda i, j: (i, j)), pl.BlockSpec((1, tn), lambda i, j: (0, j))],
        out_specs=pl.BlockSpec((tm, tn), lambda i, j: (i, j)), out_shape=jax.ShapeDtypeStruct((s, n), f32),
        compiler_params=_params("parallel", "arbitrary"),
    )(a, w, xres, g)


def mm_nt_scaled(dx, g, w, gu=None, *, name):
    s, d = dx.shape
    n = w.shape[0]
    tm = _row_tile(s)
    tn = FT if gu is not None else 512
    wo = 2 * FT if gu is not None else tn

    def body(dx_ref, g_ref, w_ref, *rest):
        o_ref, a_ref = rest[-2], rest[-1]

        @pl.when(pl.program_id(1) == 0)
        def _():
            a_ref[...] = (dx_ref[...] * g_ref[...]).astype(bf16)

        da = lax.dot_general(a_ref[...], w_ref[...], (((1,), (1,)), ((), ())), preferred_element_type=f32)
        if gu is None:
            o_ref[...] = da.astype(bf16)
        else:
            guv = rest[0][...].astype(f32)
            gt, up = guv[:, :FT], guv[:, FT:]
            sg = jax.nn.sigmoid(gt)
            dgate = da * up * (sg * (1.0 + gt * (1.0 - sg)))
            dup = da * (gt * sg)
            o_ref[...] = jnp.concatenate([dgate, dup], axis=1).astype(bf16)

    in_specs = [pl.BlockSpec((tm, d), lambda i, j: (i, 0)), pl.BlockSpec((1, d), lambda i, j: (0, 0)),
                pl.BlockSpec((tn, d), lambda i, j: (j, 0))]
    args = [dx, g, w]
    if gu is not None:
        in_specs.append(pl.BlockSpec((tm, wo), lambda i, j: (i, j)))
        args.append(gu)
    return _pcall(
        body, name=name, grid=(s // tm, n // tn), in_specs=in_specs,
        out_specs=pl.BlockSpec((tm, wo), lambda i, j: (i, j)),
        out_shape=jax.ShapeDtypeStruct((s, (n // tn) * wo), bf16),
        scratch_shapes=[pltpu.VMEM((tm, d), bf16)], compiler_params=_params("parallel", "arbitrary"),
    )(*args)


def mm_nt_acc(a, w, *, name):
    s, k = a.shape
    n = w.shape[0]
    tm = _row_tile(s)
    tk = 768 if k % 768 == 0 and k % 512 != 0 else 512

    def body(a_ref, w_ref, o_ref):
        part = lax.dot_general(a_ref[...], w_ref[...], (((1,), (1,)), ((), ())), preferred_element_type=f32)

        @pl.when(pl.program_id(1) == 0)
        def _():
            o_ref[...] = part

        @pl.when(pl.program_id(1) != 0)
        def _():
            o_ref[...] += part

    return _pcall(
        body, name=name, grid=(s // tm, k // tk),
        in_specs=[pl.BlockSpec((tm, tk), lambda i, kk: (i, kk)), pl.BlockSpec((n, tk), lambda i, kk: (0, kk))],
        out_specs=pl.BlockSpec((tm, n), lambda i, kk: (i, 0)), out_shape=jax.ShapeDtypeStruct((s, n), f32),
        compiler_params=_params("parallel", "arbitrary"),
    )(a, w)


def mm_tn(a, b, scale=None, *, name):
    s, ka = a.shape
    nb = b.shape[1]
    ts = _row_tile(s)
    tn = 768 if nb % 768 == 0 and nb % 512 != 0 else 512
    tka = 1408 if ka % 1408 == 0 else min(ka, 1024)
    ns = s // ts

    def body(a_ref, b_ref, *rest):
        o_ref = rest[2] if scale is not None else rest[0]
        si = pl.program_id(2)
        part = lax.dot_general(a_ref[...], b_ref[...].astype(bf16), (((0,), (0,)), ((), ())), preferred_element_type=f32)

        @pl.when(si == 0)
        def _():
            o_ref[...] = part

        @pl.when(si != 0)
        def _():
            o_ref[...] += part

        if scale is not None:
            g_ref, wb_ref, dg_ref = rest[0], rest[1], rest[3]

            @pl.when(si == ns - 1)
            def _():
                gm = o_ref[...]
                dgp = jnp.sum(wb_ref[...].astype(f32) * gm, axis=0, keepdims=True)

                @pl.when(pl.program_id(1) == 0)
                def _():
                    dg_ref[...] = dgp

                @pl.when(pl.program_id(1) != 0)
                def _():
                    dg_ref[...] += dgp

                o_ref[...] = gm * g_ref[...]

    in_specs = [pl.BlockSpec((ts, tka), lambda j, i, k: (k, i)), pl.BlockSpec((ts, tn), lambda j, i, k: (k, j))]
    args = [a, b]
    out_specs = [pl.BlockSpec((tka, tn), lambda j, i, k: (i, j))]
    out_shape = [jax.ShapeDtypeStruct((ka, nb), f32)]
    if scale is not None:
        in_specs += [pl.BlockSpec((1, tn), lambda j, i, k: (0, j)), pl.BlockSpec((tka, tn), lambda j, i, k: (i, j))]
        args += list(scale)
        out_specs.append(pl.BlockSpec((1, tn), lambda j, i, k: (0, j)))
        out_shape.append(jax.ShapeDtypeStruct((1, nb), f32))
    res = _pcall(
        body, name=name, grid=(nb // tn, ka // tka, ns), in_specs=in_specs, out_specs=out_specs, out_shape=out_shape,
        compiler_params=_params("arbitrary", "arbitrary", "arbitrary"),
    )(*args)
    return res if scale is not None else res[0]


def norm_bwd(x, dh, dres, nw, sc, *, name):
    s, d = x.shape
    tm = _row_tile(s, 512)

    def body(x_ref, dh_ref, dr_ref, nw_ref, sc_ref, dx_ref, st_ref):
        xv, dhv = x_ref[...], dh_ref[...]
        r = lax.rsqrt(jnp.mean(xv * xv, axis=-1, keepdims=True) + RMS_EPS)
        xh = xv * r
        dn = dhv * (1.0 + sc_ref[...])
        dxh = dn * nw_ref[...]
        dx_ref[...] = dr_ref[...] + r * (dxh - xh * jnp.mean(dxh * xh, axis=-1, keepdims=True))
        rows = jnp.concatenate([
            jnp.sum(dn * xh, axis=0, keepdims=True),
            jnp.sum(dhv * (xh * nw_ref[...]), axis=0, keepdims=True),
            jnp.sum(dhv, axis=0, keepdims=True),
            jnp.zeros((5, d), f32)], axis=0)

        @pl.when(pl.program_id(0) == 0)
        def _():
            st_ref[...] = rows

        @pl.when(pl.program_id(0) != 0)
        def _():
            st_ref[...] += rows

    big = pl.BlockSpec((tm, d), lambda i: (i, 0))
    vec = pl.BlockSpec((1, d), lambda i: (0, 0))
    return _pcall(
        body, name=name, grid=(s // tm,), in_specs=[big, big, big, vec, vec],
        out_specs=[big, pl.BlockSpec((8, d), lambda i: (0, 0))],
        out_shape=[jax.ShapeDtypeStruct((s, d), f32), jax.ShapeDtypeStruct((8, d), f32)],
        compiler_params=_params("arbitrary"),
    )(x, dh, dres, nw, sc)


def loss_head(x, fn, tgt, *, name):
    s, d = x.shape
    tm = _row_tile(s, 512)

    def body(x_ref, fn_ref, t_ref, dx_ref, st_ref):
        xv = x_ref[...]
        r = lax.rsqrt(jnp.mean(xv * xv, axis=-1, keepdims=True) + RMS_EPS)
        xh = xv * r
        err = xh * fn_ref[...] - t_ref[...]
        dy = err / float(d)
        dxh = dy * fn_ref[...]
        dx_ref[...] = r * (dxh - xh * jnp.mean(dxh * xh, axis=-1, keepdims=True))
        rows = jnp.concatenate([
            jnp.sum(dy * xh, axis=0, keepdims=True),
            jnp.sum(err * err, axis=0, keepdims=True),
            jnp.zeros((6, d), f32)], axis=0)

        @pl.when(pl.program_id(0) == 0)
        def _():
            st_ref[...] = rows

        @pl.when(pl.program_id(0) != 0)
        def _():
            st_ref[...] += rows

    big = pl.BlockSpec((tm, d), lambda i: (i, 0))
    return _pcall(
        body, name=name, grid=(s // tm,), in_specs=[big, pl.BlockSpec((1, d), lambda i: (0, 0)), big],
        out_specs=[big, pl.BlockSpec((8, d), lambda i: (0, 0))],
        out_shape=[jax.ShapeDtypeStruct((s, d), f32), jax.ShapeDtypeStruct((8, d), f32)],
        compiler_params=_params("arbitrary"),
    )(x, fn, tgt)


def _alibi(n):
    return np.asarray(2.0 ** (-8.0 * np.arange(1, n + 1) / n), dtype=np.float32)


class _Attn:
    def __init__(self, s, *, mixer, group=0):
        if mixer == "a":
            self.blk, self.dil, self.c, self.npairs = 128, 1, A_QKV, 2
            self.qb0, self.kb0, self.vb0 = 0, 8, 10
            slopes = _alibi(16).reshape(2, 2, GQ)
        else:
            window, dil = B_GROUPS[group]
            self.blk, self.dil, self.c, self.npairs = window // (2 * dil), dil, B_QKV, 1
            self.qb0, self.kb0, self.vb0 = 2 * group, 12 + group, 15 + group
            slopes = _alibi(24).reshape(3, 1, 2, GQ)[group]
        self.l = s // self.dil
        self.t = min(512, self.l)
        self.nt = self.l // self.t
        self.nb = self.t // self.blk
        blk = self.blk
        qi = np.arange(blk)[:, None]
        rel = np.arange(3 * blk)[None, :] - blk - qi
        dist = (self.dil * np.abs(rel)).astype(np.float32)
        bias = -slopes[:, :, :, None, None] * dist[None, None, None]
        bias = np.where(np.abs(rel) <= blk, bias, np.float32(NEG)).astype(np.float32)
        self.bias = bias.reshape(self.npairs, 2, GQ * blk, 3 * blk)

    def grid(self):
        return (self.dil, self.npairs, self.nt)

    def halo(self, width, col):
        t, blk, nbl = self.t, self.blk, self.l // self.blk
        per = t // blk
        return [
            pl.BlockSpec((blk, width), lambda r, hp, i: (jnp.maximum(i * per - 1, 0), col(r, hp))),
            pl.BlockSpec((t, width), lambda r, hp, i: (i, col(r, hp))),
            pl.BlockSpec((blk, width), lambda r, hp, i: (jnp.minimum((i + 1) * per, nbl - 1), col(r, hp))),
        ]

    def qcol(self, e):
        return lambda r, hp: r * (self.c // 256) + self.qb0 + 2 * hp + e

    def kcol(self, r, hp):
        return r * (self.c // 128) + self.kb0 + hp

    def vcol(self, r, hp):
        return r * (self.c // 128) + self.vb0 + hp

    def pcol(self, r, hp):
        return r * self.npairs + hp


def _stack_heads(x):
    return jnp.concatenate([x[:, g * DH:(g + 1) * DH] for g in range(GQ)], axis=0)


def _unstack_heads(x, rows):
    return jnp.concatenate([x[g * rows:(g + 1) * rows] for g in range(GQ)], axis=1)


def _head_cols(tile, hh, rows):
    return jnp.concatenate([tile[:, hh * GQ + g:hh * GQ + g + 1] for g in range(GQ)], axis=0)


def attn_fwd(qkv, sinkcol, cfg, *, out_dtype, name):
    s = qkv.shape[0]
    blk, t, nb, nt, dil = cfg.blk, cfg.t, cfg.nb, cfg.nt, cfg.dil
    view = qkv.reshape(cfg.l, dil * cfg.c)
    scale = DH ** -0.5

    def body(q0, q1, kp, km, kn, vp, vm, vn, bias_ref, sink_ref, o_ref, lse_ref, kx, vx):
        ti = pl.program_id(2)
        first, last = ti == 0, ti == nt - 1
        for hh in range(2):
            sl = slice(hh * DH, (hh + 1) * DH)
            for dst, (p_, m_, n_) in ((kx, (kp, km, kn)), (vx, (vp, vm, vn))):
                dst[hh, 0:blk] = p_[:, sl]
                dst[hh, blk:blk + t] = m_[:, sl]
                dst[hh, blk + t:] = n_[:, sl]
        col = lax.broadcasted_iota(jnp.int32, (GQ * blk, 3 * blk), 1)
        lane = lax.broadcasted_iota(jnp.int32, (blk, 128), 1)
        for b in range(nb):
            lse_tile = jnp.zeros((blk, 128), f32)
            for hh in range(2):
                qs = _stack_heads((q0, q1)[hh][b * blk:(b + 1) * blk, :])
                sc = lax.dot_general(qs, kx[hh, b * blk:(b + 3) * blk, :], (((1,), (1,)), ((), ())),
                                     preferred_element_type=f32) * scale + bias_ref[0, hh]
                if b == 0:
                    sc = jnp.where(jnp.logical_and(first, col < blk), NEG, sc)
                if b == nb - 1:
                    sc = jnp.where(jnp.logical_and(last, col >= 2 * blk), NEG, sc)
                sk = sink_ref[0, hh]
                m = jnp.maximum(jnp.max(sc, axis=-1, keepdims=True), sk)
                p = jnp.exp(sc - m)
                l = jnp.sum(p, axis=-1, keepdims=True) + jnp.exp(sk - m)
                o = jnp.dot(p.astype(bf16), vx[hh, b * blk:(b + 3) * blk, :], preferred_element_type=f32) / l
                lse = m + jnp.log(l)
                o_ref[b * blk:(b + 1) * blk, hh * 256:(hh + 1) * 256] = _unstack_heads(o, blk).astype(out_dtype)
                for g in range(GQ):
                    lse_tile = jnp.where(lane == hh * GQ + g, lse[g * blk:(g + 1) * blk], lse_tile)
            lse_ref[b * blk:(b + 1) * blk, :] = lse_tile

    in_specs = [pl.BlockSpec((t, 256), lambda r, hp, i, e=e: (i, cfg.qcol(e)(r, hp))) for e in range(2)]
    in_specs += cfg.halo(128, cfg.kcol) + cfg.halo(128, cfg.vcol)
    in_specs += [pl.BlockSpec((1, 2, GQ * blk, 3 * blk), lambda r, hp, i: (hp, 0, 0, 0)),
                 pl.BlockSpec((1, 2, GQ * blk, 1), lambda r, hp, i: (hp, 0, 0, 0))]
    o, lse = _pcall(
        body, name=name, grid=cfg.grid(), in_specs=in_specs,
        out_specs=[pl.BlockSpec((t, 512), lambda r, hp, i: (i, cfg.pcol(r, hp))),
                   pl.BlockSpec((t, 128), lambda r, hp, i: (i, cfg.pcol(r, hp)))],
        out_shape=[jax.ShapeDtypeStruct((cfg.l, dil * cfg.npairs * 512), out_dtype),
                   jax.ShapeDtypeStruct((cfg.l, dil * cfg.npairs * 128), f32)],
        scratch_shapes=[pltpu.VMEM((2, t + 2 * blk, DH), bf16), pltpu.VMEM((2, t + 2 * blk, DH), bf16)],
        compiler_params=_params("arbitrary", "arbitrary", "arbitrary"),
    )(*([view] * 8), jnp.asarray(cfg.bias), sinkcol)
    return o.reshape(s, cfg.npairs * 512), lse.reshape(s, cfg.npairs * 128)


def attn_bwd(qkv, do, lse, delta, cfg, *, name):
    s = qkv.shape[0]
    blk, t, nb, nt, dil, npairs = cfg.blk, cfg.t, cfg.nb, cfg.nt, cfg.dil, cfg.npairs
    view = qkv.reshape(cfg.l, dil * cfg.c)
    dov = do.reshape(cfg.l, dil * npairs * 512)
    lsev = lse.reshape(cfg.l, dil * npairs * 128)
    dlv = delta.reshape(cfg.l, dil * npairs * 128)
    scale = DH ** -0.5
    nt_dims = (((1,), (1,)), ((), ()))
    tn_dims = (((0,), (0,)), ((), ()))

    def body(q0p, q0m, q0n, q1p, q1m, q1n, kp, km, kn, vp, vm, vn, dop, dom, don, lp, lm, ln, dp_, dm_, dn_,
             bias_ref, dq_ref, dk_ref, dv_ref, kx, vx, dkx, dvx):
        ti = pl.program_id(2)
        first, last = ti == 0, ti == nt - 1
        for hh in range(2):
            sl = slice(hh * DH, (hh + 1) * DH)
            for dst, (p_, m_, n_) in ((kx, (kp, km, kn)), (vx, (vp, vm, vn))):
                dst[hh, 0:blk] = p_[:, sl]
                dst[hh, blk:blk + t] = m_[:, sl]
                dst[hh, blk + t:] = n_[:, sl]
        dkx[...] = jnp.zeros_like(dkx)
        dvx[...] = jnp.zeros_like(dvx)

        def slab(prev, main, nxt, e):
            if e == 0:
                return prev[...]
            if e == nb + 1:
                return nxt[...]
            return main[(e - 1) * blk:e * blk, :]

        col3 = lax.broadcasted_iota(jnp.int32, (GQ * blk, 3 * blk), 1)
        for e in range(nb + 2):
            lse_t = slab(lp, lm, ln, e)
            dl_t = slab(dp_, dm_, dn_, e)
            do_t = slab(dop, dom, don, e)
            for hh in range(2):
                qs = _stack_heads(slab(*((q0p, q0m, q0n), (q1p, q1m, q1n))[hh], e))
                dos = _stack_heads(do_t[:, hh * 256:(hh + 1) * 256])
                lse_c = _head_cols(lse_t, hh, blk)
                dl_c = _head_cols(dl_t, hh, blk)
                if e == 0:
                    k0, k1, bias = 1, 2, bias_ref[0, hh, :, 2 * blk:3 * blk]
                elif e == nb + 1:
                    k0, k1, bias = nb, nb + 1, bias_ref[0, hh, :, 0:blk]
                else:
                    k0, k1, bias = e - 1, e + 2, bias_ref[0, hh]
                kw = kx[hh, k0 * blk:k1 * blk, :]
                vw = vx[hh, k0 * blk:k1 * blk, :]
                sc = lax.dot_general(qs, kw, nt_dims, preferred_element_type=f32) * scale + bias
                if e == 0:
                    sc = jnp.where(first, NEG, sc)
                elif e == nb + 1:
                    sc = jnp.where(last, NEG, sc)
                else:
                    if e == 1:
                        sc = jnp.where(jnp.logical_and(first, col3 < blk), NEG, sc)
                    if e == nb:
                        sc = jnp.where(jnp.logical_and(last, col3 >= 2 * blk), NEG, sc)
                p = jnp.exp(sc - lse_c)
                dp = lax.dot_general(dos, vw, nt_dims, preferred_element_type=f32)
                ds = (p * (dp - dl_c) * scale).astype(bf16)
                dkx[hh, k0 * blk:k1 * blk, :] += lax.dot_general(ds, qs, tn_dims, preferred_element_type=f32)
                dvx[hh, k0 * blk:k1 * blk, :] += lax.dot_general(p.astype(bf16), dos, tn_dims, preferred_element_type=f32)
                if 1 <= e <= nb:
                    dq = jnp.dot(ds, kw, preferred_element_type=f32)
                    dq_ref[(e - 1) * blk:e * blk, hh * 256:(hh + 1) * 256] = _unstack_heads(dq, blk).astype(bf16)
        for hh in range(2):
            dk_ref[:, hh * DH:(hh + 1) * DH] = dkx[hh, blk:blk + t, :].astype(bf16)
            dv_ref[:, hh * DH:(hh + 1) * DH] = dvx[hh, blk:blk + t, :].astype(bf16)

    in_specs = cfg.halo(256, cfg.qcol(0)) + cfg.halo(256, cfg.qcol(1))
    in_specs += cfg.halo(128, cfg.kcol) + cfg.halo(128, cfg.vcol)
    in_specs += cfg.halo(512, cfg.pcol) + cfg.halo(128, cfg.pcol) + cfg.halo(128, cfg.pcol)
    in_specs += [pl.BlockSpec((1, 2, GQ * blk, 3 * blk), lambda r, hp, i: (hp, 0, 0, 0))]
    dq, dk, dv = _pcall(
        body, name=name, grid=cfg.grid(), in_specs=in_specs,
        out_specs=[pl.BlockSpec((t, 512), lambda r, hp, i: (i, cfg.pcol(r, hp))),
                   pl.BlockSpec((t, 128), lambda r, hp, i: (i, cfg.pcol(r, hp))),
                   pl.BlockSpec((t, 128), lambda r, hp, i: (i, cfg.pcol(r, hp)))],
        out_shape=[jax.ShapeDtypeStruct((cfg.l, dil * npairs * 512), bf16),
                   jax.ShapeDtypeStruct((cfg.l, dil * npairs * 128), bf16),
                   jax.ShapeDtypeStruct((cfg.l, dil * npairs * 128), bf16)],
        scratch_shapes=[pltpu.VMEM((2, t + 2 * blk, DH), bf16), pltpu.VMEM((2, t + 2 * blk, DH), bf16),
                        pltpu.VMEM((2, t + 2 * blk, DH), f32), pltpu.VMEM((2, t + 2 * blk, DH), f32)],
        compiler_params=_params("arbitrary", "arbitrary", "arbitrary"),
    )(*([view] * 12), dov, dov, dov, lsev, lsev, lsev, dlv, dlv, dlv, jnp.asarray(cfg.bias))
    return dq.reshape(s, npairs * 512), dk.reshape(s, npairs * 128), dv.reshape(s, npairs * 128)


def _head_indicator(nheads):
    e = np.zeros((nheads * DH, (nheads // 8) * 128), np.float32)
    for c in range(nheads * DH):
        h = c // DH
        e[c, (h // 8) * 128 + h % 8] = 1.0
    return e


def _dot_split(x, e):
    hi = x.astype(bf16)
    lo = (x - hi.astype(f32)).astype(bf16)
    return jnp.dot(hi, e, preferred_element_type=f32) + jnp.dot(lo, e, preferred_element_type=f32)


def attn_delta(do, o, lse, sinkrow, *, name):
    s, co = do.shape
    w = lse.shape[1]
    tm = _row_tile(s)
    ind = jnp.asarray(_head_indicator(co // DH), dtype=bf16)

    def body(do_ref, o_ref, lse_ref, sink_ref, e_ref, dl_ref, ds_ref):
        dl = _dot_split(do_ref[...].astype(f32) * o_ref[...].astype(f32), e_ref[...])
        dl_ref[...] = dl
        part = -jnp.sum(jnp.exp(sink_ref[...] - lse_ref[...]) * dl, axis=0, keepdims=True)
        part = jnp.concatenate([part, jnp.zeros((7, w), f32)], axis=0)

        @pl.when(pl.program_id(0) == 0)
        def _():
            ds_ref[...] = part

        @pl.when(pl.program_id(0) != 0)
        def _():
            ds_ref[...] += part

    return _pcall(
        body, name=name, grid=(s // tm,),
        in_specs=[pl.BlockSpec((tm, co), lambda i: (i, 0)), pl.BlockSpec((tm, co), lambda i: (i, 0)),
                  pl.BlockSpec((tm, w), lambda i: (i, 0)), pl.BlockSpec((1, w), lambda i: (0, 0)),
                  pl.BlockSpec((co, w), lambda i: (0, 0))],
        out_specs=[pl.BlockSpec((tm, w), lambda i: (i, 0)), pl.BlockSpec((8, w), lambda i: (0, 0))],
        out_shape=[jax.ShapeDtypeStruct((s, w), f32), jax.ShapeDtypeStruct((8, w), f32)],
        compiler_params=_params("arbitrary"),
    )(do, o, lse, sinkrow, ind)


def attn_merge(os_, lses, *, name):
    s = os_[0].shape[0]
    tm = _row_tile(s)
    ind_t = jnp.asarray(_head_indicator(8).T, dtype=bf16)

    def body(o0, o1, o2, l0, l1, l2, e_ref, o_ref, lse_ref):
        ls = [l0[...], l1[...], l2[...]]
        m = jnp.maximum(jnp.maximum(ls[0], ls[1]), ls[2])
        tot = m + jnp.log(jnp.exp(ls[0] - m) + jnp.exp(ls[1] - m) + jnp.exp(ls[2] - m))
        lse_ref[...] = tot
        acc = jnp.zeros((tm, B_OUT), f32)
        for og, lg in zip((o0, o1, o2), ls):
            acc = acc + _dot_split(jnp.exp(lg - tot), e_ref[...]) * og[...]
        o_ref[...] = acc.astype(bf16)

    big = pl.BlockSpec((tm, B_OUT), lambda i: (i, 0))
    sm = pl.BlockSpec((tm, 128), lambda i: (i, 0))
    return _pcall(
        body, name=name, grid=(s // tm,), in_specs=[big, big, big, sm, sm, sm, pl.BlockSpec((128, B_OUT), lambda i: (0, 0))],
        out_specs=[big, sm], out_shape=[jax.ShapeDtypeStruct((s, B_OUT), bf16), jax.ShapeDtypeStruct((s, 128), f32)],
        compiler_params=_params("parallel"),
    )(*os_, *lses, ind_t)


def ada_mod(c_all, w, b, *, name):
    n = w.shape[2]

    def body(c_ref, w_ref, b_ref, o_ref):
        cv = c_ref[...]
        cond = cv * jax.nn.sigmoid(cv)
        o_ref[0] = jnp.dot(cond, w_ref[0], preferred_element_type=f32, precision=lax.Precision.HIGHEST) + b_ref[0]

    return _pcall(
        body, name=name, grid=(DEPTH,),
        in_specs=[pl.BlockSpec((N_DEV, D), lambda i: (0, 0)), pl.BlockSpec((1, D, n), lambda i: (i, 0, 0)),
                  pl.BlockSpec((1, 1, n), lambda i: (i, 0, 0))],
        out_specs=pl.BlockSpec((1, N_DEV, n), lambda i: (i, 0, 0)),
        out_shape=jax.ShapeDtypeStruct((DEPTH, N_DEV, n), f32), compiler_params=_params("arbitrary"),
    )(c_all, w, b)


def ada_grad(c_t, dm, *, name):
    n = dm.shape[2]

    def body(c_ref, dm_ref, o_ref):
        cv = c_ref[...]
        cond = cv * jax.nn.sigmoid(cv)
        acc = cond[:, 0:1] * dm_ref[0, 0:1, :]
        for b in range(1, N_DEV):
            acc = acc + cond[:, b:b + 1] * dm_ref[0, b:b + 1, :]
        o_ref[0] = acc

    return _pcall(
        body, name=name, grid=(DEPTH,),
        in_specs=[pl.BlockSpec((D, N_DEV), lambda i: (0, 0)), pl.BlockSpec((1, N_DEV, n), lambda i: (i, 0, 0))],
        out_specs=pl.BlockSpec((1, D, n), lambda i: (i, 0, 0)),
        out_shape=jax.ShapeDtypeStruct((DEPTH, D, n), f32), compiler_params=_params("arbitrary"),
    )(c_t, dm)


def _adam_math(w, g, m, v):
    m2 = B1 * m + (1.0 - B1) * g
    v2 = B2 * v + (1.0 - B2) * (g * g)
    mh = m2 / (1.0 - B1 ** STEP)
    vh = v2 / (1.0 - B2 ** STEP)
    return -LR * (mh / (jnp.sqrt(vh) + ADAM_EPS) + WD * w), m2, v2


def adamw(w, m, v, g, *, name):
    r, c = w.shape
    tr = 256 if r % 256 == 0 else r

    def body(w_ref, m_ref, v_ref, g_ref, d_ref, m2_ref, v2_ref):
        d_ref[...], m2_ref[...], v2_ref[...] = _adam_math(w_ref[...], g_ref[...], m_ref[...], v_ref[...])

    spec = pl.BlockSpec((tr, c), lambda i: (i, 0))
    return _pcall(
        body, name=name, grid=(r // tr,), in_specs=[spec] * 4, out_specs=[spec] * 3,
        out_shape=[jax.ShapeDtypeStruct((r, c), f32)] * 3, compiler_params=_params("parallel"),
    )(w, m, v, g)


def adamw_parts(w, m, v, own, sib, *, name):
    r, c = w.shape
    tr = 256 if r % 256 == 0 else r

    def body(w_ref, m_ref, v_ref, own_ref, sib_ref, g_ref, d_ref, m2_ref, v2_ref):
        def total(ref):
            return ((ref[0].astype(f32) + ref[1].astype(f32)) + ref[2].astype(f32)) + ref[3].astype(f32)

        g = total(own_ref) + total(sib_ref)
        g_ref[...] = g
        d_ref[...], m2_ref[...], v2_ref[...] = _adam_math(w_ref[...], g, m_ref[...], v_ref[...])

    spec = pl.BlockSpec((tr, c), lambda i: (i, 0))
    pspec = pl.BlockSpec((4, tr, c), lambda i: (0, i, 0))
    return _pcall(
        body, name=name, grid=(r // tr,), in_specs=[spec] * 3 + [pspec] * 2, out_specs=[spec] * 4,
        out_shape=[jax.ShapeDtypeStruct((r, c), f32)] * 4, compiler_params=_params("parallel"),
    )(w, m, v, own, sib)


def sum_devices(g, *, name):
    _, r, c = g.shape

    def body(g_ref, o_ref):
        acc = g_ref[0]
        for k in range(1, N_DEV):
            acc = acc + g_ref[k]
        o_ref[...] = acc

    return _pcall(body, name=name, out_shape=jax.ShapeDtypeStruct((r, c), f32))(g)


def _place():
    x, y, c = lax.axis_index("x"), lax.axis_index("y"), lax.axis_index("c")
    chips = [(1 - x, y), (x, 1 - y), (1 - x, 1 - y)]
    return x, y, c, chips


def allgather8(v, *, name):
    r, c_ = v.shape

    def body(v_ref, o_ref, send_sems, recv_sems, local_sem):
        x, y, c, _ = _place()
        me = 4 * x + 2 * y + c
        mine = pltpu.make_async_copy(v_ref, o_ref.at[me], local_sem)
        mine.start()
        flips = [(fx, fy, fc) for fx in (0, 1) for fy in (0, 1) for fc in (0, 1)][1:]

        def peer(f):
            return (x ^ f[0], y ^ f[1], c ^ f[2])

        def copy(k, slot, to):
            return pltpu.make_async_remote_copy(
                src_ref=v_ref, dst_ref=o_ref.at[slot], send_sem=send_sems.at[k], recv_sem=recv_sems.at[k],
                device_id=to, device_id_type=MESH)

        sends = [copy(k, me, peer(f)) for k, f in enumerate(flips)]
        for cp in sends:
            cp.start()
        for k, f in enumerate(flips):
            px, py, pc = peer(f)
            copy(k, 4 * px + 2 * py + pc, (x, y, c)).wait_recv()
        for cp in sends:
            cp.wait_send()
        mine.wait()

    return _pcall(
        body, name=name, in_specs=[ANY], out_specs=ANY, out_shape=jax.ShapeDtypeStruct((N_DEV, r, c_), v.dtype),
        scratch_shapes=[pltpu.SemaphoreType.DMA((7,)), pltpu.SemaphoreType.DMA((7,)), pltpu.SemaphoreType.DMA],
    )(v)


def gather_weights(shards, *, name):
    n = len(shards)

    def body(*refs):
        src, out = refs[:n], refs[n:2 * n]
        send_a, recv_a, send_f, recv_f, local_sems = refs[2 * n:]
        x, y, c, chips = _place()
        sib = (x, y, 1 - c)
        me = 2 * x + y
        locals_ = [pltpu.make_async_copy(src[a], out[a].at[me], local_sems.at[a]) for a in range(n)]
        for cp in locals_:
            cp.start()

        def half(a, which):
            rh = src[a].shape[0] // 2
            return pl.ds(which * rh, rh)

        def first(a, k, chip_from, to):
            slot = 2 * chip_from[0] + chip_from[1]
            s_ref = src[a].at[half(a, c)]
            return pltpu.make_async_remote_copy(
                src_ref=s_ref, dst_ref=out[a].at[slot, half(a, c)], send_sem=send_a.at[3 * a + k],
                recv_sem=recv_a.at[3 * a + k], device_id=to, device_id_type=MESH)

        def passed(a, k, chip_from, which, to):
            slot = 2 * chip_from[0] + chip_from[1]
            ref = out[a].at[slot, half(a, which)]
            return pltpu.make_async_remote_copy(
                src_ref=ref, dst_ref=ref, send_sem=send_f.at[3 * a + k], recv_sem=recv_f.at[3 * a + k],
                device_id=to, device_id_type=MESH)

        sends = [first(a, k, (x, y), (*chip, c)) for a in range(n) for k, chip in enumerate(chips)]
        for cp in sends:
            cp.start()
        fwd = []
        for a in range(n):
            for k, chip in enumerate(chips):
                first(a, k, chip, (x, y, c)).wait_recv()
                cp = passed(a, k, chip, c, sib)
                cp.start()
                fwd.append(cp)
        for a in range(n):
            for k, chip in enumerate(chips):
                passed(a, k, chip, 1 - c, (x, y, c)).wait_recv()
        for cp in sends + fwd:
            cp.wait_send()
        for cp in locals_:
            cp.wait()

    return _pcall(
        body, name=name, in_specs=[ANY] * n, out_specs=[ANY] * n,
        out_shape=[jax.ShapeDtypeStruct((4,) + tuple(sh.shape), sh.dtype) for sh in shards],
        scratch_shapes=[pltpu.SemaphoreType.DMA((3 * n,)) for _ in range(4)] + [pltpu.SemaphoreType.DMA((n,))],
    )(*shards)


def exchange_grads(parts, *, name):
    n = len(parts)

    def body(*refs):
        src, own, sibo = refs[:n], refs[n:2 * n], refs[2 * n:3 * n]
        send_sems, recv_sems, local_sems = refs[3 * n:]
        x, y, c, chips = _place()
        sib = (x, y, 1 - c)
        me = 2 * x + y

        def slot(chip):
            return 2 * chip[0] + chip[1]

        def copy(a, k, s_ref, d_ref, to):
            return pltpu.make_async_remote_copy(
                src_ref=s_ref, dst_ref=d_ref, send_sem=send_sems.at[7 * a + k], recv_sem=recv_sems.at[7 * a + k],
                device_id=to, device_id_type=MESH)

        locals_ = [pltpu.make_async_copy(src[a].at[me], own[a].at[me], local_sems.at[a]) for a in range(n)]
        for cp in locals_:
            cp.start()
        sends = []
        for a in range(n):
            sends.append(copy(a, 0, src[a].at[me], sibo[a].at[me], sib))
            for k, chip in enumerate(chips):
                sends.append(copy(a, 1 + k, src[a].at[slot(chip)], own[a].at[me], (*chip, c)))
        for cp in sends:
            cp.start()
        fwd = []
        for a in range(n):
            for k, chip in enumerate(chips):
                copy(a, 1 + k, src[a].at[me], own[a].at[slot(chip)], (x, y, c)).wait_recv()
                cp = copy(a, 4 + k, own[a].at[slot(chip)], sibo[a].at[slot(chip)], sib)
                cp.start()
                fwd.append(cp)
        for a in range(n):
            copy(a, 0, src[a].at[me], sibo[a].at[me], (x, y, c)).wait_recv()
            for k, chip in enumerate(chips):
                copy(a, 4 + k, src[a].at[me], sibo[a].at[slot(chip)], (x, y, c)).wait_recv()
        for cp in sends + fwd:
            cp.wait_send()
        for cp in locals_:
            cp.wait()

    shapes = [jax.ShapeDtypeStruct(p.shape, p.dtype) for p in parts]
    res = _pcall(
        body, name=name, in_specs=[ANY] * n, out_specs=[ANY] * (2 * n), out_shape=shapes + shapes,
        scratch_shapes=[pltpu.SemaphoreType.DMA((7 * n,)), pltpu.SemaphoreType.DMA((7 * n,)), pltpu.SemaphoreType.DMA((n,))],
    )(*parts)
    return res[:n], res[n:]


def _natural(g, how):
    nl = g.shape[1]
    if how == "col":
        return jnp.moveaxis(g, 0, 2).reshape(nl, g.shape[2], 4 * g.shape[3])
    return jnp.moveaxis(g, 0, 1).reshape(nl, 4 * g.shape[2], g.shape[3])


def _chunks(gw, how):
    k, n = gw.shape
    if how == "col":
        return jnp.moveaxis(gw.reshape(k, 4, n // 4), 1, 0).astype(bf16)
    return gw.reshape(4, k // 4, n).astype(bf16)


def _interleave(w):
    lead = w.shape[:-1]
    return jnp.swapaxes(w.reshape(*lead, 2, F // FT, FT), -3, -2).reshape(*lead, 2 * F)


def _deinterleave(w):
    lead = w.shape[:-1]
    return jnp.swapaxes(w.reshape(*lead, F // FT, 2, FT), -3, -2).reshape(*lead, 2 * F)


def kernel(x, c, ada_w, ada_b, norm_mix, norm_ffn, ffn_w_in, ffn_w_out, a_w_in, a_w_out, a_sink, b_w_in, b_w_out, final_norm, loss_target, m_ada_w, m_ada_b, m_norm_mix, m_norm_ffn, m_ffn_w_in, m_ffn_w_out, m_a_w_in, m_a_w_out, m_a_sink, m_b_w_in, m_b_w_out, m_final_norm, v_ada_w, v_ada_b, v_norm_mix, v_norm_ffn, v_ffn_w_in, v_ffn_w_out, v_a_w_in, v_a_w_out, v_a_sink, v_b_w_in, v_b_w_out, v_final_norm):
    s = x.shape[1]
    xi, yi, ci = lax.axis_index("x"), lax.axis_index("y"), lax.axis_index("c")
    chip = 2 * xi + yi
    dev = 2 * chip + ci
    x0 = x[0]
    tgt = loss_target[0]

    big = {"ffn_w_in": (ffn_w_in, "col"), "ffn_w_out": (ffn_w_out, "row"), "a_w_in": (a_w_in, "col"),
           "a_w_out": (a_w_out, "row"), "b_w_in": (b_w_in, "col"), "b_w_out": (b_w_out, "col")}
    names = list(big)
    flat = [big[k][0].astype(bf16).reshape(-1, big[k][0].shape[-1]) for k in names]
    gathered = gather_weights(flat, name="gather_weights")
    wb = {k: _natural(g.reshape((4,) + big[k][0].shape), big[k][1]) for k, g in zip(names, gathered)}
    wb["ffn_w_in"] = _interleave(wb["ffn_w_in"])

    c_all = allgather8(jnp.broadcast_to(c, (8, D)), name="gather_c")[:, 0, :]
    nsh = ada_w.shape[2]
    ada_b_sh = lax.dynamic_slice_in_dim(ada_b, chip * nsh, nsh, axis=1)[:, None, :]
    mod_part = ada_mod(c_all, ada_w, ada_b_sh, name="ada_mod")
    mod_all = allgather8(mod_part.reshape(DEPTH * N_DEV, nsh), name="gather_mod")
    mod_all = mod_all.reshape(4, 2, DEPTH, N_DEV, nsh)[:, 0]
    mod = lax.dynamic_index_in_dim(mod_all, dev, axis=2, keepdims=False)
    mod = jnp.moveaxis(mod, 0, 1).reshape(DEPTH, 6, 1, D)

    cfg_a = _Attn(s, mixer="a")
    cfg_b = [_Attn(s, mixer="b", group=g) for g in range(3)]
    no_sink = jnp.full((1, 2, GQ * 64, 1), NEG, f32)

    saved = []
    xc = x0
    for i in range(DEPTH):
        j = i // 2
        sh1, sc1, g1, sh2, sc2, g2 = (mod[i, q] for q in range(6))
        nmix, nffn = norm_mix[i][None, :], norm_ffn[i][None, :]
        if i % 2 == 0:
            h, qkv = mm_norm(xc, nmix, sc1, sh1, wb["a_w_in"][j], swiglu=False, name="a_qkv")
            sinkcol = jnp.repeat(a_sink[j].reshape(2, 2, GQ), 128, axis=2)[..., None]
            o, lse = attn_fwd(qkv, sinkcol, cfg_a, out_dtype=bf16, name="a_attn_fwd")
            w_o = wb["a_w_out"][j]
        else:
            h, qkv = mm_norm(xc, nmix, sc1, sh1, wb["b_w_in"][j], swiglu=False, name="b_qkv")
            outs = [attn_fwd(qkv, no_sink, cfg_b[g], out_dtype=f32, name=f"b_attn_fwd{g}") for g in range(3)]
            o, lse = attn_merge([t[0] for t in outs], [t[1] for t in outs], name="b_merge")
            w_o = wb["b_w_out"][j]
        x1 = mm_resid(o, w_o, xc, g1, name="a_out" if i % 2 == 0 else "b_out")
        h2, gu, act = mm_norm(x1, nffn, sc2, sh2, wb["ffn_w_in"][i], swiglu=True, name="ffn_in")
        x2 = mm_resid(act, wb["ffn_w_out"][i], x1, g2, name="ffn_out")
        saved.append((xc, h, qkv, o, lse, x1, h2, gu, act))
        xc = x2

    dx, st_final = loss_head(xc, final_norm[None, :], tgt, name="loss_head")

    zero_row = jnp.zeros((1, D), f32)
    dmod_rows = [None] * DEPTH
    d_nmix, d_nffn = [None] * DEPTH, [None] * DEPTH
    d_sink = [None] * 2
    gw = {k: [None] * big[k][0].shape[0] for k in names}
    for i in reversed(range(DEPTH)):
        j = i // 2
        xin, h, qkv, o, lse, x1, h2, gu, act = saved[i]
        sh1, sc1, g1, sh2, sc2, g2 = (mod[i, q] for q in range(6))
        nmix, nffn = norm_mix[i][None, :], norm_ffn[i][None, :]
        w_fo = wb["ffn_w_out"][i]
        dgu = mm_nt_scaled(dx, g2, w_fo, gu, name="ffn_dact")
        gw["ffn_w_out"][i], dg2 = mm_tn(act, dx, (g2, w_fo), name="ffn_dw_out")
        gw["ffn_w_in"][i] = _deinterleave(mm_tn(h2, dgu, name="ffn_dw_in"))
        dh2 = mm_nt_acc(dgu, wb["ffn_w_in"][i], name="ffn_dh")
        dx1, st2 = norm_bwd(x1, dh2, dx, nffn, sc2, name="norm_bwd")
        if i % 2 == 0:
            w_o, w_i, kin, kout = wb["a_w_out"][j], wb["a_w_in"][j], "a_w_in", "a_w_out"
        else:
            w_o, w_i, kin, kout = wb["b_w_out"][j], wb["b_w_in"][j], "b_w_in", "b_w_out"
        do = mm_nt_scaled(dx1, g1, w_o, name="a_do" if i % 2 == 0 else "b_do")
        gw[kout][j], dg1 = mm_tn(o, dx1, (g1, w_o), name="a_dw_out" if i % 2 == 0 else "b_dw_out")
        if i % 2 == 0:
            sinkrow = jnp.pad(a_sink[j].reshape(2, 8), ((0, 0), (0, 120))).reshape(1, 256)
            delta, dsk = attn_delta(do, o, lse, sinkrow, name="a_delta")
            d_sink[j] = dsk[0].reshape(2, 128)[:, :8].reshape(16)
            dq, dk, dv = attn_bwd(qkv, do, lse, delta, cfg_a, name="a_attn_bwd")
            dqkv = jnp.concatenate([dq, dk, dv], axis=1)
        else:
            delta, _ = attn_delta(do, o, lse, jnp.zeros((1, 128), f32), name="b_delta")
            gr = [attn_bwd(qkv, do, lse, delta, cfg_b[g], name=f"b_attn_bwd{g}") for g in range(3)]
            dqkv = jnp.concatenate([gr[g][q] for q in range(3) for g in range(3)], axis=1)
        gw[kin][j] = mm_tn(h, dqkv, name="a_dw_in" if i % 2 == 0 else "b_dw_in")
        dh = mm_nt_acc(dqkv, w_i, name="a_dh" if i % 2 == 0 else "b_dh")
        dx, st1 = norm_bwd(xin, dh, dx1, nmix, sc1, name="norm_bwd")
        dmod_rows[i] = jnp.concatenate([st1[2:3], st1[1:2], dg1, st2[2:3], st2[1:2], dg2], axis=0)
        d_nmix[i], d_nffn[i] = st1[0:1], st2[0:1]

    sink_row = jnp.pad(jnp.concatenate(d_sink), (0, D - 32))[None, :]
    stats = jnp.concatenate(dmod_rows + d_nmix + d_nffn + [sink_row, st_final[0:1], st_final[1:2]]
                            + [zero_row] * (STAT_ROWS - 35), axis=0)
    stats_all = allgather8(stats, name="gather_stats")
    tot = sum_devices(stats_all, name="sum_stats")
    loss = 0.5 * jnp.sum(tot[34]) / float(D)

    def pack(ab, nm, nf, sk, fnm, fill):
        return jnp.concatenate([ab.reshape(24, D), nm, nf, jnp.pad(sk.reshape(1, 32), ((0, 0), (0, D - 32)), constant_values=fill),
                                fnm[None, :], jnp.full((STAT_ROWS - 34, D), fill, f32)], axis=0)

    sd, sm, sv = adamw(pack(ada_b, norm_mix, norm_ffn, a_sink, final_norm, 0.0),
                       pack(m_ada_b, m_norm_mix, m_norm_ffn, m_a_sink, m_final_norm, 0.0),
                       pack(v_ada_b, v_norm_mix, v_norm_ffn, v_a_sink, v_final_norm, 1.0), tot, name="adamw_small")

    def unpack(p):
        return p[0:24].reshape(DEPTH, 6 * D), p[24:28], p[28:32], p[32, :32].reshape(2, 16), p[33]

    small = {"grad": unpack(tot), "delta": unpack(sd), "m": unpack(sm), "v": unpack(sv)}

    dmod_all = stats_all[:, 0:24, :].reshape(N_DEV, DEPTH, 6 * D)
    dm_sh = jnp.moveaxis(lax.dynamic_slice_in_dim(dmod_all, chip * nsh, nsh, axis=2), 0, 1)
    g_ada = ada_grad(c_all.T, dm_sh, name="ada_grad")
    r_ada = (DEPTH * D, nsh)
    ada_res = adamw(ada_w.reshape(r_ada), m_ada_w.reshape(r_ada), v_ada_w.reshape(r_ada), g_ada.reshape(r_ada), name="adamw_ada")
    ada_out = [g_ada] + [t.reshape(ada_w.shape) for t in ada_res]

    parts = [jnp.stack([_chunks(gl, big[k][1]) for gl in gw[k]], axis=1) for k in names]
    parts = [p.reshape(4, -1, p.shape[-1]) for p in parts]
    own, sibs = exchange_grads(parts, name="exchange_grads")
    big_out = {}
    mom = {"ffn_w_in": (m_ffn_w_in, v_ffn_w_in), "ffn_w_out": (m_ffn_w_out, v_ffn_w_out), "a_w_in": (m_a_w_in, v_a_w_in),
           "a_w_out": (m_a_w_out, v_a_w_out), "b_w_in": (m_b_w_in, v_b_w_in), "b_w_out": (m_b_w_out, v_b_w_out)}
    for k, o_, s_ in zip(names, own, sibs):
        w = big[k][0]
        r2 = (-1, w.shape[-1])
        res = adamw_parts(w.reshape(r2), mom[k][0].reshape(r2), mom[k][1].reshape(r2), o_, s_, name="adamw_" + k)
        big_out[k] = [t.reshape(w.shape) for t in res]

    def leaves(q):
        sm_ = small[("grad", "delta", "m", "v")[q]]
        return (ada_out[q], sm_[0], sm_[1], sm_[2], big_out["ffn_w_in"][q], big_out["ffn_w_out"][q], big_out["a_w_in"][q],
                big_out["a_w_out"][q], sm_[3], big_out["b_w_in"][q], big_out["b_w_out"][q], sm_[4])

    return (loss, dx[None], *leaves(0), *leaves(1), *leaves(2), *leaves(3))
```

```python
import functools
import math

import numpy as np
import jax
import jax.numpy as jnp
from jax import lax
from jax.experimental import pallas as pl
from jax.experimental.pallas import tpu as pltpu

f32 = jnp.float32
bf16 = jnp.bfloat16

D = 1024
DH = 64
GQ = 4
DEPTH = 4
F = 2816
FT = 256
A_QKV, A_OUT = 1536, 1024
B_QKV, B_OUT = 2304, 512
B_GROUPS = ((128, 1), (512, 4), (2048, 16))
RMS_EPS = 1e-6
NEG = -1e30
LR, B1, B2, ADAM_EPS, WD, STEP = 0.001, 0.9, 0.999, 1e-08, 0.01, 10
N_DEV = 8
STAT_ROWS = 40
MESH = pl.DeviceIdType.MESH
ANY = pl.BlockSpec(memory_space=pl.ANY)


def _pcall(body, **kw):
    return pl.pallas_call(body, **kw)


def _params(*sem):
    return pltpu.CompilerParams(dimension_semantics=sem, vmem_limit_bytes=56 * 1024 * 1024)


def _row_tile(s, want=1024):
    return want if s % want == 0 else s


def mm_norm(x, nw, sc, sh, w, *, swiglu, name):
    s, d = x.shape
    n = w.shape[1]
    tm = _row_tile(s)
    tn = 2 * FT if swiglu else n // 3

    def body(x_ref, nw_ref, sc_ref, sh_ref, w_ref, h_ref, y_ref, *rest):
        @pl.when(pl.program_id(1) == 0)
        def _():
            xv = x_ref[...]
            r = lax.rsqrt(jnp.mean(xv * xv, axis=-1, keepdims=True) + RMS_EPS)
            h_ref[...] = ((xv * r * nw_ref[...]) * (1.0 + sc_ref[...]) + sh_ref[...]).astype(bf16)

        res = jnp.dot(h_ref[...], w_ref[...], preferred_element_type=f32)
        y_ref[...] = res.astype(bf16)
        if swiglu:
            g, u = res[:, :FT], res[:, FT:]
            rest[0][...] = (g * jax.nn.sigmoid(g) * u).astype(bf16)

    vec = pl.BlockSpec((1, d), lambda i, j: (0, 0))
    out_shape = [jax.ShapeDtypeStruct((s, d), bf16), jax.ShapeDtypeStruct((s, n), bf16)]
    out_specs = [pl.BlockSpec((tm, d), lambda i, j: (i, 0)), pl.BlockSpec((tm, tn), lambda i, j: (i, j))]
    if swiglu:
        out_shape.append(jax.ShapeDtypeStruct((s, n // 2), bf16))
        out_specs.append(pl.BlockSpec((tm, FT), lambda i, j: (i, j)))
    return _pcall(
        body, name=name, grid=(s // tm, n // tn),
        in_specs=[pl.BlockSpec((tm, d), lambda i, j: (i, 0)), vec, vec, vec, pl.BlockSpec((d, tn), lambda i, j: (0, j))],
        out_specs=out_specs, out_shape=out_shape, compiler_params=_params("parallel", "arbitrary"),
    )(x, nw, sc, sh, w)


def mm_resid(a, w, xres, g, *, name):
    s, k = a.shape
    n = w.shape[1]
    tm, tn = _row_tile(s), 512

    def body(a_ref, w_ref, x_ref, g_ref, o_ref):
        o_ref[...] = x_ref[...] + g_ref[...] * jnp.dot(a_ref[...], w_ref[...], preferred_element_type=f32)

    return _pcall(
        body, name=name, grid=(s // tm, n // tn),
        in_specs=[pl.BlockSpec((tm, k), lambda i, j: (i, 0)), pl.BlockSpec((k, tn), lambda i, j: (0, j)),
                  pl.BlockSpec((tm, tn), lambda i, j: (i, j)), pl.BlockSpec((1, tn), lambda i, j: (0, j))],
        out_specs=pl.BlockSpec((tm, tn), lambda i, j: (i, j)), out_shape=jax.ShapeDtypeStruct((s, n), f32),
        compiler_params=_params("parallel", "arbitrary"),
    )(a, w, xres, g)


def mm_nt_scaled(dx, g, w, gu=None, *, name):
    s, d = dx.shape
    n = w.shape[0]
    tm = _row_tile(s)
    tn = FT if gu is not None else 512
    wo = 2 * FT if gu is not None else tn

    def body(dx_ref, g_ref, w_ref, *rest):
        o_ref, a_ref = rest[-2], rest[-1]

        @pl.when(pl.program_id(1) == 0)
        def _():
            a_ref[...] = (dx_ref[...] * g_ref[...]).astype(bf16)

        da = lax.dot_general(a_ref[...], w_ref[...], (((1,), (1,)), ((), ())), preferred_element_type=f32)
        if gu is None:
            o_ref[...] = da.astype(bf16)
        else:
            guv = rest[0][...].astype(f32)
            gt, up = guv[:, :FT], guv[:, FT:]
            sg = jax.nn.sigmoid(gt)
            dgate = da * up * (sg * (1.0 + gt * (1.0 - sg)))
            dup = da * (gt * sg)
            o_ref[...] = jnp.concatenate([dgate, dup], axis=1).astype(bf16)

    in_specs = [pl.BlockSpec((tm, d), lambda i, j: (i, 0)), pl.BlockSpec((1, d), lambda i, j: (0, 0)),
                pl.BlockSpec((tn, d), lambda i, j: (j, 0))]
    args = [dx, g, w]
    if gu is not None:
        in_specs.append(pl.BlockSpec((tm, wo), lambda i, j: (i, j)))
        args.append(gu)
    return _pcall(
        body, name=name, grid=(s // tm, n // tn), in_specs=in_specs,
        out_specs=pl.BlockSpec((tm, wo), lambda i, j: (i, j)),
        out_shape=jax.ShapeDtypeStruct((s, (n // tn) * wo), bf16),
        scratch_shapes=[pltpu.VMEM((tm, d), bf16)], compiler_params=_params("parallel", "arbitrary"),
    )(*args)


def mm_nt_acc(a, w, *, name):
    s, k = a.shape
    n = w.shape[0]
    tm = _row_tile(s)
    tk = 768 if k % 768 == 0 and k % 512 != 0 else 512

    def body(a_ref, w_ref, o_ref):
        part = lax.dot_general(a_ref[...], w_ref[...], (((1,), (1,)), ((), ())), preferred_element_type=f32)

        @pl.when(pl.program_id(1) == 0)
        def _():
            o_ref[...] = part

        @pl.when(pl.program_id(1) != 0)
        def _():
            o_ref[...] += part

    return _pcall(
        body, name=name, grid=(s // tm, k // tk),
        in_specs=[pl.BlockSpec((tm, tk), lambda i, kk: (i, kk)), pl.BlockSpec((n, tk), lambda i, kk: (0, kk))],
        out_specs=pl.BlockSpec((tm, n), lambda i, kk: (i, 0)), out_shape=jax.ShapeDtypeStruct((s, n), f32),
        compiler_params=_params("parallel", "arbitrary"),
    )(a, w)


def mm_tn(a, b, scale=None, *, name):
    s, ka = a.shape
    nb = b.shape[1]
    ts = _row_tile(s)
    tn = 768 if nb % 768 == 0 and nb % 512 != 0 else 512
    tka = 1408 if ka % 1408 == 0 else min(ka, 1024)
    ns = s // ts

    def body(a_ref, b_ref, *rest):
        o_ref = rest[2] if scale is not None else rest[0]
        si = pl.program_id(2)
        part = lax.dot_general(a_ref[...], b_ref[...].astype(bf16), (((0,), (0,)), ((), ())), preferred_element_type=f32)

        @pl.when(si == 0)
        def _():
            o_ref[...] = part

        @pl.when(si != 0)
        def _():
            o_ref[...] += part

        if scale is not None:
            g_ref, wb_ref, dg_ref = rest[0], rest[1], rest[3]

            @pl.when(si == ns - 1)
            def _():
                gm = o_ref[...]
                dgp = jnp.sum(wb_ref[...].astype(f32) * gm, axis=0, keepdims=True)

                @pl.when(pl.program_id(1) == 0)
                def _():
                    dg_ref[...] = dgp

                @pl.when(pl.program_id(1) != 0)
                def _():
                    dg_ref[...] += dgp

                o_ref[...] = gm * g_ref[...]

    in_specs = [pl.BlockSpec((ts, tka), lambda j, i, k: (k, i)), pl.BlockSpec((ts, tn), lambda j, i, k: (k, j))]
    args = [a, b]
    out_specs = [pl.BlockSpec((tka, tn), lambda j, i, k: (i, j))]
    out_shape = [jax.ShapeDtypeStruct((ka, nb), f32)]
    if scale is not None:
        in_specs += [pl.BlockSpec((1, tn), lambda j, i, k: (0, j)), pl.BlockSpec((tka, tn), lambda j, i, k: (i, j))]
        args += list(scale)
        out_specs.append(pl.BlockSpec((1, tn), lambda j, i, k: (0, j)))
        out_shape.append(jax.ShapeDtypeStruct((1, nb), f32))
    res = _pcall(
        body, name=name, grid=(nb // tn, ka // tka, ns), in_specs=in_specs, out_specs=out_specs, out_shape=out_shape,
        compiler_params=_params("arbitrary", "arbitrary", "arbitrary"),
    )(*args)
    return res if scale is not None else res[0]


def norm_bwd(x, dh, dres, nw, sc, *, name):
    s, d = x.shape
    tm = _row_tile(s, 512)

    def body(x_ref, dh_ref, dr_ref, nw_ref, sc_ref, dx_ref, st_ref):
        xv, dhv = x_ref[...], dh_ref[...]
        r = lax.rsqrt(jnp.mean(xv * xv, axis=-1, keepdims=True) + RMS_EPS)
        xh = xv * r
        dn = dhv * (1.0 + sc_ref[...])
        dxh = dn * nw_ref[...]
        dx_ref[...] = dr_ref[...] + r * (dxh - xh * jnp.mean(dxh * xh, axis=-1, keepdims=True))
        rows = jnp.concatenate([
            jnp.sum(dn * xh, axis=0, keepdims=True),
            jnp.sum(dhv * (xh * nw_ref[...]), axis=0, keepdims=True),
            jnp.sum(dhv, axis=0, keepdims=True),
            jnp.zeros((5, d), f32)], axis=0)

        @pl.when(pl.program_id(0) == 0)
        def _():
            st_ref[...] = rows

        @pl.when(pl.program_id(0) != 0)
        def _():
            st_ref[...] += rows

    big = pl.BlockSpec((tm, d), lambda i: (i, 0))
    vec = pl.BlockSpec((1, d), lambda i: (0, 0))
    return _pcall(
        body, name=name, grid=(s // tm,), in_specs=[big, big, big, vec, vec],
        out_specs=[big, pl.BlockSpec((8, d), lambda i: (0, 0))],
        out_shape=[jax.ShapeDtypeStruct((s, d), f32), jax.ShapeDtypeStruct((8, d), f32)],
        compiler_params=_params("arbitrary"),
    )(x, dh, dres, nw, sc)


def loss_head(x, fn, tgt, *, name):
    s, d = x.shape
    tm = _row_tile(s, 512)

    def body(x_ref, fn_ref, t_ref, dx_ref, st_ref):
        xv = x_ref[...]
        r = lax.rsqrt(jnp.mean(xv * xv, axis=-1, keepdims=True) + RMS_EPS)
        xh = xv * r
        err = xh * fn_ref[...] - t_ref[...]
        dy = err / float(d)
        dxh = dy * fn_ref[...]
        dx_ref[...] = r * (dxh - xh * jnp.mean(dxh * xh, axis=-1, keepdims=True))
        rows = jnp.concatenate([
            jnp.sum(dy * xh, axis=0, keepdims=True),
            jnp.sum(err * err, axis=0, keepdims=True),
            jnp.zeros((6, d), f32)], axis=0)

        @pl.when(pl.program_id(0) == 0)
        def _():
            st_ref[...] = rows

        @pl.when(pl.program_id(0) != 0)
        def _():
            st_ref[...] += rows

    big = pl.BlockSpec((tm, d), lambda i: (i, 0))
    return _pcall(
        body, name=name, grid=(s // tm,), in_specs=[big, pl.BlockSpec((1, d), lambda i: (0, 0)), big],
        out_specs=[big, pl.BlockSpec((8, d), lambda i: (0, 0))],
        out_shape=[jax.ShapeDtypeStruct((s, d), f32), jax.ShapeDtypeStruct((8, d), f32)],
        compiler_params=_params("arbitrary"),
    )(x, fn, tgt)


def _alibi(n):
    return np.asarray(2.0 ** (-8.0 * np.arange(1, n + 1) / n), dtype=np.float32)


class _Attn:
    def __init__(self, s, *, mixer, group=0):
        if mixer == "a":
            self.blk, self.dil, self.c, self.npairs = 128, 1, A_QKV, 2
            self.qb0, self.kb0, self.vb0 = 0, 8, 10
            slopes = _alibi(16).reshape(2, 2, GQ)
        else:
            window, dil = B_GROUPS[group]
            self.blk, self.dil, self.c, self.npairs = window // (2 * dil), dil, B_QKV, 1
            self.qb0, self.kb0, self.vb0 = 2 * group, 12 + group, 15 + group
            slopes = _alibi(24).reshape(3, 1, 2, GQ)[group]
        self.l = s // self.dil
        self.t = min(512, self.l)
        self.nt = self.l // self.t
        self.nb = self.t // self.blk
        blk = self.blk
        qi = np.arange(blk)[:, None]
        rel = np.arange(3 * blk)[None, :] - blk - qi
        dist = (self.dil * np.abs(rel)).astype(np.float32)
        bias = -slopes[:, :, :, None, None] * dist[None, None, None]
        bias = np.where(np.abs(rel) <= blk, bias, np.float32(NEG)).astype(np.float32)
        self.bias = bias.reshape(self.npairs, 2, GQ * blk, 3 * blk)

    def grid(self):
        return (self.dil, self.npairs, self.nt)

    def halo(self, width, col):
        t, blk, nbl = self.t, self.blk, self.l // self.blk
        per = t // blk
        return [
            pl.BlockSpec((blk, width), lambda r, hp, i: (jnp.maximum(i * per - 1, 0), col(r, hp))),
            pl.BlockSpec((t, width), lambda r, hp, i: (i, col(r, hp))),
            pl.BlockSpec((blk, width), lambda r, hp, i: (jnp.minimum((i + 1) * per, nbl - 1), col(r, hp))),
        ]

    def qcol(self, e):
        return lambda r, hp: r * (self.c // 256) + self.qb0 + 2 * hp + e

    def kcol(self, r, hp):
        return r * (self.c // 128) + self.kb0 + hp

    def vcol(self, r, hp):
        return r * (self.c // 128) + self.vb0 + hp

    def pcol(self, r, hp):
        return r * self.npairs + hp


def _stack_heads(x):
    return jnp.concatenate([x[:, g * DH:(g + 1) * DH] for g in range(GQ)], axis=0)


def _unstack_heads(x, rows):
    return jnp.concatenate([x[g * rows:(g + 1) * rows] for g in range(GQ)], axis=1)


def _head_cols(tile, hh, rows):
    return jnp.concatenate([tile[:, hh * GQ + g:hh * GQ + g + 1] for g in range(GQ)], axis=0)


def attn_fwd(qkv, sinkcol, cfg, *, out_dtype, name):
    s = qkv.shape[0]
    blk, t, nb, nt, dil = cfg.blk, cfg.t, cfg.nb, cfg.nt, cfg.dil
    view = qkv.reshape(cfg.l, dil * cfg.c)
    scale = DH ** -0.5

    def body(q0, q1, kp, km, kn, vp, vm, vn, bias_ref, sink_ref, o_ref, lse_ref, kx, vx):
        ti = pl.program_id(2)
        first, last = ti == 0, ti == nt - 1
        for hh in range(2):
            sl = slice(hh * DH, (hh + 1) * DH)
            for dst, (p_, m_, n_) in ((kx, (kp, km, kn)), (vx, (vp, vm, vn))):
                dst[hh, 0:blk] = p_[:, sl]
                dst[hh, blk:blk + t] = m_[:, sl]
                dst[hh, blk + t:] = n_[:, sl]
        col = lax.broadcasted_iota(jnp.int32, (GQ * blk, 3 * blk), 1)
        lane = lax.broadcasted_iota(jnp.int32, (blk, 128), 1)
        pairs = [(b, hh) for b in range(nb) for hh in range(2)]
        qs = [_stack_heads((q0, q1)[hh][b * blk:(b + 1) * blk, :]) for b, hh in pairs]
        sc = [lax.dot_general(q_, kx[hh, b * blk:(b + 3) * blk, :], (((1,), (1,)), ((), ())), preferred_element_type=f32)
              for q_, (b, hh) in zip(qs, pairs)]
        sc = [s_ * scale + bias_ref[0, hh] for s_, (b, hh) in zip(sc, pairs)]
        sc = [jnp.where(jnp.logical_and(first, col < blk), NEG, s_) if b == 0 else s_ for s_, (b, hh) in zip(sc, pairs)]
        sc = [jnp.where(jnp.logical_and(last, col >= 2 * blk), NEG, s_) if b == nb - 1 else s_ for s_, (b, hh) in zip(sc, pairs)]
        ms = [jnp.maximum(jnp.max(s_, axis=-1, keepdims=True), sink_ref[0, hh]) for s_, (b, hh) in zip(sc, pairs)]
        ps = [jnp.exp(s_ - m_) for s_, m_ in zip(sc, ms)]
        ls = [jnp.sum(p_, axis=-1, keepdims=True) + jnp.exp(sink_ref[0, hh] - m_) for p_, m_, (b, hh) in zip(ps, ms, pairs)]
        os_ = [jnp.dot(p_.astype(bf16), vx[hh, b * blk:(b + 3) * blk, :], preferred_element_type=f32)
               for p_, (b, hh) in zip(ps, pairs)]
        os_ = [o_ / l_ for o_, l_ in zip(os_, ls)]
        lses = [m_ + jnp.log(l_) for m_, l_ in zip(ms, ls)]
        for o_, (b, hh) in zip(os_, pairs):
            o_ref[b * blk:(b + 1) * blk, hh * 256:(hh + 1) * 256] = _unstack_heads(o_, blk).astype(out_dtype)
        for b in range(nb):
            lse_tile = jnp.zeros((blk, 128), f32)
            for hh in range(2):
                lse = lses[2 * b + hh]
                for g in range(GQ):
                    lse_tile = jnp.where(lane == hh * GQ + g, lse[g * blk:(g + 1) * blk], lse_tile)
            lse_ref[b * blk:(b + 1) * blk, :] = lse_tile

    in_specs = [pl.BlockSpec((t, 256), lambda r, hp, i, e=e: (i, cfg.qcol(e)(r, hp))) for e in range(2)]
    in_specs += cfg.halo(128, cfg.kcol) + cfg.halo(128, cfg.vcol)
    in_specs += [pl.BlockSpec((1, 2, GQ * blk, 3 * blk), lambda r, hp, i: (hp, 0, 0, 0)),
                 pl.BlockSpec((1, 2, GQ * blk, 1), lambda r, hp, i: (hp, 0, 0, 0))]
    o, lse = _pcall(
        body, name=name, grid=cfg.grid(), in_specs=in_specs,
        out_specs=[pl.BlockSpec((t, 512), lambda r, hp, i: (i, cfg.pcol(r, hp))),
                   pl.BlockSpec((t, 128), lambda r, hp, i: (i, cfg.pcol(r, hp)))],
        out_shape=[jax.ShapeDtypeStruct((cfg.l, dil * cfg.npairs * 512), out_dtype),
                   jax.ShapeDtypeStruct((cfg.l, dil * cfg.npairs * 128), f32)],
        scratch_shapes=[pltpu.VMEM((2, t + 2 * blk, DH), bf16), pltpu.VMEM((2, t + 2 * blk, DH), bf16)],
        compiler_params=_params("arbitrary", "arbitrary", "arbitrary"),
    )(*([view] * 8), jnp.asarray(cfg.bias), sinkcol)
    return o.reshape(s, cfg.npairs * 512), lse.reshape(s, cfg.npairs * 128)


def attn_bwd(qkv, do, lse, delta, cfg, *, name):
    s = qkv.shape[0]
    blk, t, nb, nt, dil, npairs = cfg.blk, cfg.t, cfg.nb, cfg.nt, cfg.dil, cfg.npairs
    view = qkv.reshape(cfg.l, dil * cfg.c)
    dov = do.reshape(cfg.l, dil * npairs * 512)
    lsev = lse.reshape(cfg.l, dil * npairs * 128)
    dlv = delta.reshape(cfg.l, dil * npairs * 128)
    scale = DH ** -0.5
    nt_dims = (((1,), (1,)), ((), ()))
    tn_dims = (((0,), (0,)), ((), ()))

    def body(q0p, q0m, q0n, q1p, q1m, q1n, kp, km, kn, vp, vm, vn, dop, dom, don, lp, lm, ln, dp_, dm_, dn_,
             bias_ref, dq_ref, dk_ref, dv_ref, kx, vx, dkx, dvx):
        ti = pl.program_id(2)
        first, last = ti == 0, ti == nt - 1
        for hh in range(2):
            sl = slice(hh * DH, (hh + 1) * DH)
            for dst, (p_, m_, n_) in ((kx, (kp, km, kn)), (vx, (vp, vm, vn))):
                dst[hh, 0:blk] = p_[:, sl]
                dst[hh, blk:blk + t] = m_[:, sl]
                dst[hh, blk + t:] = n_[:, sl]
        dkx[...] = jnp.zeros_like(dkx)
        dvx[...] = jnp.zeros_like(dvx)

        def slab(prev, main, nxt, e):
            if e == 0:
                return prev[...]
            if e == nb + 1:
                return nxt[...]
            return main[(e - 1) * blk:e * blk, :]

        col3 = lax.broadcasted_iota(jnp.int32, (GQ * blk, 3 * blk), 1)

        def keys(e):
            if e == 0:
                return 1, 2, slice(2 * blk, 3 * blk)
            if e == nb + 1:
                return nb, nb + 1, slice(0, blk)
            return e - 1, e + 2, slice(0, 3 * blk)

        def edge(sc, e):
            if e == 0:
                return jnp.where(first, NEG, sc)
            if e == nb + 1:
                return jnp.where(last, NEG, sc)
            if e == 1:
                sc = jnp.where(jnp.logical_and(first, col3 < blk), NEG, sc)
            if e == nb:
                sc = jnp.where(jnp.logical_and(last, col3 >= 2 * blk), NEG, sc)
            return sc

        pairs = [(e, hh) for e in range(nb + 2) for hh in range(2)]
        qs = [_stack_heads(slab(*((q0p, q0m, q0n), (q1p, q1m, q1n))[hh], e)) for e, hh in pairs]
        dos = [_stack_heads(slab(dop, dom, don, e)[:, hh * 256:(hh + 1) * 256]) for e, hh in pairs]
        lse_c = [_head_cols(slab(lp, lm, ln, e), hh, blk) for e, hh in pairs]
        dl_c = [_head_cols(slab(dp_, dm_, dn_, e), hh, blk) for e, hh in pairs]
        kw = [kx[hh, keys(e)[0] * blk:keys(e)[1] * blk, :] for e, hh in pairs]
        vw = [vx[hh, keys(e)[0] * blk:keys(e)[1] * blk, :] for e, hh in pairs]
        sc = [lax.dot_general(q_, k_, nt_dims, preferred_element_type=f32) for q_, k_ in zip(qs, kw)]
        dp = [lax.dot_general(d_, v_, nt_dims, preferred_element_type=f32) for d_, v_ in zip(dos, vw)]
        sc = [edge(s_ * scale + bias_ref[0, hh, :, keys(e)[2]], e) for s_, (e, hh) in zip(sc, pairs)]
        ps = [jnp.exp(s_ - l_) for s_, l_ in zip(sc, lse_c)]
        ds = [(p_ * (d_ - c_) * scale).astype(bf16) for p_, d_, c_ in zip(ps, dp, dl_c)]
        pb = [p_.astype(bf16) for p_ in ps]
        dks = [lax.dot_general(s_, q_, tn_dims, preferred_element_type=f32) for s_, q_ in zip(ds, qs)]
        dvs = [lax.dot_general(p_, d_, tn_dims, preferred_element_type=f32) for p_, d_ in zip(pb, dos)]
        dqs = [jnp.dot(s_, k_, preferred_element_type=f32) if 1 <= e <= nb else None for s_, k_, (e, hh) in zip(ds, kw, pairs)]
        for dk_, dv_, dq_, (e, hh) in zip(dks, dvs, dqs, pairs):
            k0, k1, _ = keys(e)
            dkx[hh, k0 * blk:k1 * blk, :] += dk_
            dvx[hh, k0 * blk:k1 * blk, :] += dv_
            if dq_ is not None:
                dq_ref[(e - 1) * blk:e * blk, hh * 256:(hh + 1) * 256] = _unstack_heads(dq_, blk).astype(bf16)
        for hh in range(2):
            dk_ref[:, hh * DH:(hh + 1) * DH] = dkx[hh, blk:blk + t, :].astype(bf16)
            dv_ref[:, hh * DH:(hh + 1) * DH] = dvx[hh, blk:blk + t, :].astype(bf16)

    in_specs = cfg.halo(256, cfg.qcol(0)) + cfg.halo(256, cfg.qcol(1))
    in_specs += cfg.halo(128, cfg.kcol) + cfg.halo(128, cfg.vcol)
    in_specs += cfg.halo(512, cfg.pcol) + cfg.halo(128, cfg.pcol) + cfg.halo(128, cfg.pcol)
    in_specs += [pl.BlockSpec((1, 2, GQ * blk, 3 * blk), lambda r, hp, i: (hp, 0, 0, 0))]
    dq, dk, dv = _pcall(
        body, name=name, grid=cfg.grid(), in_specs=in_specs,
        out_specs=[pl.BlockSpec((t, 512), lambda r, hp, i: (i, cfg.pcol(r, hp))),
                   pl.BlockSpec((t, 128), lambda r, hp, i: (i, cfg.pcol(r, hp))),
                   pl.BlockSpec((t, 128), lambda r, hp, i: (i, cfg.pcol(r, hp)))],
        out_shape=[jax.ShapeDtypeStruct((cfg.l, dil * npairs * 512), bf16),
                   jax.ShapeDtypeStruct((cfg.l, dil * npairs * 128), bf16),
                   jax.ShapeDtypeStruct((cfg.l, dil * npairs * 128), bf16)],
        scratch_shapes=[pltpu.VMEM((2, t + 2 * blk, DH), bf16), pltpu.VMEM((2, t + 2 * blk, DH), bf16),
                        pltpu.VMEM((2, t + 2 * blk, DH), f32), pltpu.VMEM((2, t + 2 * blk, DH), f32)],
        compiler_params=_params("arbitrary", "arbitrary", "arbitrary"),
    )(*([view] * 12), dov, dov, dov, lsev, lsev, lsev, dlv, dlv, dlv, jnp.asarray(cfg.bias))
    return dq.reshape(s, npairs * 512), dk.reshape(s, npairs * 128), dv.reshape(s, npairs * 128)


def _head_indicator(nheads):
    e = np.zeros((nheads * DH, (nheads // 8) * 128), np.float32)
    for c in range(nheads * DH):
        h = c // DH
        e[c, (h // 8) * 128 + h % 8] = 1.0
    return e


def _dot_split(x, e):
    hi = x.astype(bf16)
    lo = (x - hi.astype(f32)).astype(bf16)
    return jnp.dot(hi, e, preferred_element_type=f32) + jnp.dot(lo, e, preferred_element_type=f32)


def attn_delta(do, o, lse, sinkrow, *, name):
    s, co = do.shape
    w = lse.shape[1]
    tm = _row_tile(s)
    ind = jnp.asarray(_head_indicator(co // DH), dtype=bf16)

    def body(do_ref, o_ref, lse_ref, sink_ref, e_ref, dl_ref, ds_ref):
        dl = _dot_split(do_ref[...].astype(f32) * o_ref[...].astype(f32), e_ref[...])
        dl_ref[...] = dl
        part = -jnp.sum(jnp.exp(sink_ref[...] - lse_ref[...]) * dl, axis=0, keepdims=True)
        part = jnp.concatenate([part, jnp.zeros((7, w), f32)], axis=0)

        @pl.when(pl.program_id(0) == 0)
        def _():
            ds_ref[...] = part

        @pl.when(pl.program_id(0) != 0)
        def _():
            ds_ref[...] += part

    return _pcall(
        body, name=name, grid=(s // tm,),
        in_specs=[pl.BlockSpec((tm, co), lambda i: (i, 0)), pl.BlockSpec((tm, co), lambda i: (i, 0)),
                  pl.BlockSpec((tm, w), lambda i: (i, 0)), pl.BlockSpec((1, w), lambda i: (0, 0)),
                  pl.BlockSpec((co, w), lambda i: (0, 0))],
        out_specs=[pl.BlockSpec((tm, w), lambda i: (i, 0)), pl.BlockSpec((8, w), lambda i: (0, 0))],
        out_shape=[jax.ShapeDtypeStruct((s, w), f32), jax.ShapeDtypeStruct((8, w), f32)],
        compiler_params=_params("arbitrary"),
    )(do, o, lse, sinkrow, ind)


def attn_merge(os_, lses, *, name):
    s = os_[0].shape[0]
    tm = _row_tile(s)
    ind_t = jnp.asarray(_head_indicator(8).T, dtype=bf16)

    def body(o0, o1, o2, l0, l1, l2, e_ref, o_ref, lse_ref):
        ls = [l0[...], l1[...], l2[...]]
        m = jnp.maximum(jnp.maximum(ls[0], ls[1]), ls[2])
        tot = m + jnp.log(jnp.exp(ls[0] - m) + jnp.exp(ls[1] - m) + jnp.exp(ls[2] - m))
        lse_ref[...] = tot
        acc = jnp.zeros((tm, B_OUT), f32)
        for og, lg in zip((o0, o1, o2), ls):
            acc = acc + _dot_split(jnp.exp(lg - tot), e_ref[...]) * og[...]
        o_ref[...] = acc.astype(bf16)

    big = pl.BlockSpec((tm, B_OUT), lambda i: (i, 0))
    sm = pl.BlockSpec((tm, 128), lambda i: (i, 0))
    return _pcall(
        body, name=name, grid=(s // tm,), in_specs=[big, big, big, sm, sm, sm, pl.BlockSpec((128, B_OUT), lambda i: (0, 0))],
        out_specs=[big, sm], out_shape=[jax.ShapeDtypeStruct((s, B_OUT), bf16), jax.ShapeDtypeStruct((s, 128), f32)],
        compiler_params=_params("parallel"),
    )(*os_, *lses, ind_t)


def ada_mod(c_all, w, b, *, name):
    n = w.shape[2]

    def body(c_ref, w_ref, b_ref, o_ref):
        cv = c_ref[...]
        cond = cv * jax.nn.sigmoid(cv)
        o_ref[0] = jnp.dot(cond, w_ref[0], preferred_element_type=f32, precision=lax.Precision.HIGHEST) + b_ref[0]

    return _pcall(
        body, name=name, grid=(DEPTH,),
        in_specs=[pl.BlockSpec((N_DEV, D), lambda i: (0, 0)), pl.BlockSpec((1, D, n), lambda i: (i, 0, 0)),
                  pl.BlockSpec((1, 1, n), lambda i: (i, 0, 0))],
        out_specs=pl.BlockSpec((1, N_DEV, n), lambda i: (i, 0, 0)),
        out_shape=jax.ShapeDtypeStruct((DEPTH, N_DEV, n), f32), compiler_params=_params("arbitrary"),
    )(c_all, w, b)


def ada_grad(c_t, dm, *, name):
    n = dm.shape[2]

    def body(c_ref, dm_ref, o_ref):
        cv = c_ref[...]
        cond = cv * jax.nn.sigmoid(cv)
        acc = cond[:, 0:1] * dm_ref[0, 0:1, :]
        for b in range(1, N_DEV):
            acc = acc + cond[:, b:b + 1] * dm_ref[0, b:b + 1, :]
        o_ref[0] = acc

    return _pcall(
        body, name=name, grid=(DEPTH,),
        in_specs=[pl.BlockSpec((D, N_DEV), lambda i: (0, 0)), pl.BlockSpec((1, N_DEV, n), lambda i: (i, 0, 0))],
        out_specs=pl.BlockSpec((1, D, n), lambda i: (i, 0, 0)),
        out_shape=jax.ShapeDtypeStruct((DEPTH, D, n), f32), compiler_params=_params("arbitrary"),
    )(c_t, dm)


def _adam_math(w, g, m, v):
    m2 = B1 * m + (1.0 - B1) * g
    v2 = B2 * v + (1.0 - B2) * (g * g)
    mh = m2 / (1.0 - B1 ** STEP)
    vh = v2 / (1.0 - B2 ** STEP)
    return -LR * (mh / (jnp.sqrt(vh) + ADAM_EPS) + WD * w), m2, v2


def adamw(w, m, v, g, *, name):
    r, c = w.shape
    tr = 256 if r % 256 == 0 else r

    def body(w_ref, m_ref, v_ref, g_ref, d_ref, m2_ref, v2_ref):
        d_ref[...], m2_ref[...], v2_ref[...] = _adam_math(w_ref[...], g_ref[...], m_ref[...], v_ref[...])

    spec = pl.BlockSpec((tr, c), lambda i: (i, 0))
    return _pcall(
        body, name=name, grid=(r // tr,), in_specs=[spec] * 4, out_specs=[spec] * 3,
        out_shape=[jax.ShapeDtypeStruct((r, c), f32)] * 3, compiler_params=_params("parallel"),
    )(w, m, v, g)


def adamw_parts(w, m, v, own, sib, *, name):
    r, c = w.shape
    tr = 256 if r % 256 == 0 else r

    def body(w_ref, m_ref, v_ref, own_ref, sib_ref, g_ref, d_ref, m2_ref, v2_ref):
        def total(ref):
            return ((ref[0].astype(f32) + ref[1].astype(f32)) + ref[2].astype(f32)) + ref[3].astype(f32)

        g = total(own_ref) + total(sib_ref)
        g_ref[...] = g
        d_ref[...], m2_ref[...], v2_ref[...] = _adam_math(w_ref[...], g, m_ref[...], v_ref[...])

    spec = pl.BlockSpec((tr, c), lambda i: (i, 0))
    pspec = pl.BlockSpec((4, tr, c), lambda i: (0, i, 0))
    return _pcall(
        body, name=name, grid=(r // tr,), in_specs=[spec] * 3 + [pspec] * 2, out_specs=[spec] * 4,
        out_shape=[jax.ShapeDtypeStruct((r, c), f32)] * 4, compiler_params=_params("parallel"),
    )(w, m, v, own, sib)


def sum_devices(g, *, name):
    _, r, c = g.shape

    def body(g_ref, o_ref):
        acc = g_ref[0]
        for k in range(1, N_DEV):
            acc = acc + g_ref[k]
        o_ref[...] = acc

    return _pcall(body, name=name, out_shape=jax.ShapeDtypeStruct((r, c), f32))(g)


def _place():
    x, y, c = lax.axis_index("x"), lax.axis_index("y"), lax.axis_index("c")
    chips = [(1 - x, y), (x, 1 - y), (1 - x, 1 - y)]
    return x, y, c, chips


def allgather8(v, *, name):
    r, c_ = v.shape

    def body(v_ref, o_ref, send_sems, recv_sems, local_sem):
        x, y, c, _ = _place()
        me = 4 * x + 2 * y + c
        mine = pltpu.make_async_copy(v_ref, o_ref.at[me], local_sem)
        mine.start()
        flips = [(fx, fy, fc) for fx in (0, 1) for fy in (0, 1) for fc in (0, 1)][1:]

        def peer(f):
            return (x ^ f[0], y ^ f[1], c ^ f[2])

        def copy(k, slot, to):
            return pltpu.make_async_remote_copy(
                src_ref=v_ref, dst_ref=o_ref.at[slot], send_sem=send_sems.at[k], recv_sem=recv_sems.at[k],
                device_id=to, device_id_type=MESH)

        sends = [copy(k, me, peer(f)) for k, f in enumerate(flips)]
        for cp in sends:
            cp.start()
        for k, f in enumerate(flips):
            px, py, pc = peer(f)
            copy(k, 4 * px + 2 * py + pc, (x, y, c)).wait_recv()
        for cp in sends:
            cp.wait_send()
        mine.wait()

    return _pcall(
        body, name=name, in_specs=[ANY], out_specs=ANY, out_shape=jax.ShapeDtypeStruct((N_DEV, r, c_), v.dtype),
        scratch_shapes=[pltpu.SemaphoreType.DMA((7,)), pltpu.SemaphoreType.DMA((7,)), pltpu.SemaphoreType.DMA],
    )(v)


def gather_weights(shards, *, name):
    n = len(shards)

    def body(*refs):
        src, out = refs[:n], refs[n:2 * n]
        send_a, recv_a, send_f, recv_f, local_sems = refs[2 * n:]
        x, y, c, chips = _place()
        sib = (x, y, 1 - c)
        me = 2 * x + y
        locals_ = [pltpu.make_async_copy(src[a], out[a].at[me], local_sems.at[a]) for a in range(n)]
        for cp in locals_:
            cp.start()

        def half(a, which):
            rh = src[a].shape[0] // 2
            return pl.ds(which * rh, rh)

        def first(a, k, chip_from, to):
            slot = 2 * chip_from[0] + chip_from[1]
            s_ref = src[a].at[half(a, c)]
            return pltpu.make_async_remote_copy(
                src_ref=s_ref, dst_ref=out[a].at[slot, half(a, c)], send_sem=send_a.at[3 * a + k],
                recv_sem=recv_a.at[3 * a + k], device_id=to, device_id_type=MESH)

        def passed(a, k, chip_from, which, to):
            slot = 2 * chip_from[0] + chip_from[1]
            ref = out[a].at[slot, half(a, which)]
            return pltpu.make_async_remote_copy(
                src_ref=ref, dst_ref=ref, send_sem=send_f.at[3 * a + k], recv_sem=recv_f.at[3 * a + k],
                device_id=to, device_id_type=MESH)

        sends = [first(a, k, (x, y), (*chip, c)) for a in range(n) for k, chip in enumerate(chips)]
        for cp in sends:
            cp.start()
        fwd = []
        for a in range(n):
            for k, chip in enumerate(chips):
                first(a, k, chip, (x, y, c)).wait_recv()
                cp = passed(a, k, chip, c, sib)
                cp.start()
                fwd.append(cp)
        for a in range(n):
            for k, chip in enumerate(chips):
                passed(a, k, chip, 1 - c, (x, y, c)).wait_recv()
        for cp in sends + fwd:
            cp.wait_send()
        for cp in locals_:
            cp.wait()

    return _pcall(
        body, name=name, in_specs=[ANY] * n, out_specs=[ANY] * n,
        out_shape=[jax.ShapeDtypeStruct((4,) + tuple(sh.shape), sh.dtype) for sh in shards],
        scratch_shapes=[pltpu.SemaphoreType.DMA((3 * n,)) for _ in range(4)] + [pltpu.SemaphoreType.DMA((n,))],
    )(*shards)


def exchange_grads(parts, *, name):
    n = len(parts)

    def body(*refs):
        src, own, sibo = refs[:n], refs[n:2 * n], refs[2 * n:3 * n]
        send_sems, recv_sems, local_sems = refs[3 * n:]
        x, y, c, chips = _place()
        sib = (x, y, 1 - c)
        me = 2 * x + y

        def slot(chip):
            return 2 * chip[0] + chip[1]

        def copy(a, k, s_ref, d_ref, to):
            return pltpu.make_async_remote_copy(
                src_ref=s_ref, dst_ref=d_ref, send_sem=send_sems.at[7 * a + k], recv_sem=recv_sems.at[7 * a + k],
                device_id=to, device_id_type=MESH)

        locals_ = [pltpu.make_async_copy(src[a].at[me], own[a].at[me], local_sems.at[a]) for a in range(n)]
        for cp in locals_:
            cp.start()
        sends = []
        for a in range(n):
            sends.append(copy(a, 0, src[a].at[me], sibo[a].at[me], sib))
            for k, chip in enumerate(chips):
                sends.append(copy(a, 1 + k, src[a].at[slot(chip)], own[a].at[me], (*chip, c)))
        for cp in sends:
            cp.start()
        fwd = []
        for a in range(n):
            for k, chip in enumerate(chips):
                copy(a, 1 + k, src[a].at[me], own[a].at[slot(chip)], (x, y, c)).wait_recv()
                cp = copy(a, 4 + k, own[a].at[slot(chip)], sibo[a].at[slot(chip)], sib)
                cp.start()
                fwd.append(cp)
        for a in range(n):
            copy(a, 0, src[a].at[me], sibo[a].at[me], (x, y, c)).wait_recv()
            for k, chip in enumerate(chips):
                copy(a, 4 + k, src[a].at[me], sibo[a].at[slot(chip)], (x, y, c)).wait_recv()
        for cp in sends + fwd:
            cp.wait_send()
        for cp in locals_:
            cp.wait()

    shapes = [jax.ShapeDtypeStruct(p.shape, p.dtype) for p in parts]
    res = _pcall(
        body, name=name, in_specs=[ANY] * n, out_specs=[ANY] * (2 * n), out_shape=shapes + shapes,
        scratch_shapes=[pltpu.SemaphoreType.DMA((7 * n,)), pltpu.SemaphoreType.DMA((7 * n,)), pltpu.SemaphoreType.DMA((n,))],
    )(*parts)
    return res[:n], res[n:]


def _natural(g, how):
    nl = g.shape[1]
    if how == "col":
        return jnp.moveaxis(g, 0, 2).reshape(nl, g.shape[2], 4 * g.shape[3])
    return jnp.moveaxis(g, 0, 1).reshape(nl, 4 * g.shape[2], g.shape[3])


def _chunks(gw, how):
    k, n = gw.shape
    if how == "col":
        return jnp.moveaxis(gw.reshape(k, 4, n // 4), 1, 0).astype(bf16)
    return gw.reshape(4, k // 4, n).astype(bf16)


def _interleave(w):
    lead = w.shape[:-1]
    return jnp.swapaxes(w.reshape(*lead, 2, F // FT, FT), -3, -2).reshape(*lead, 2 * F)


def _deinterleave(w):
    lead = w.shape[:-1]
    return jnp.swapaxes(w.reshape(*lead, F // FT, 2, FT), -3, -2).reshape(*lead, 2 * F)


def kernel(x, c, ada_w, ada_b, norm_mix, norm_ffn, ffn_w_in, ffn_w_out, a_w_in, a_w_out, a_sink, b_w_in, b_w_out, final_norm, loss_target, m_ada_w, m_ada_b, m_norm_mix, m_norm_ffn, m_ffn_w_in, m_ffn_w_out, m_a_w_in, m_a_w_out, m_a_sink, m_b_w_in, m_b_w_out, m_final_norm, v_ada_w, v_ada_b, v_norm_mix, v_norm_ffn, v_ffn_w_in, v_ffn_w_out, v_a_w_in, v_a_w_out, v_a_sink, v_b_w_in, v_b_w_out, v_final_norm):
    s = x.shape[1]
    xi, yi, ci = lax.axis_index("x"), lax.axis_index("y"), lax.axis_index("c")
    chip = 2 * xi + yi
    dev = 2 * chip + ci
    x0 = x[0]
    tgt = loss_target[0]

    big = {"ffn_w_in": (ffn_w_in, "col"), "ffn_w_out": (ffn_w_out, "row"), "a_w_in": (a_w_in, "col"),
           "a_w_out": (a_w_out, "row"), "b_w_in": (b_w_in, "col"), "b_w_out": (b_w_out, "col")}
    names = list(big)
    flat = [big[k][0].astype(bf16).reshape(-1, big[k][0].shape[-1]) for k in names]
    gathered = gather_weights(flat, name="gather_weights")
    wb = {k: _natural(g.reshape((4,) + big[k][0].shape), big[k][1]) for k, g in zip(names, gathered)}
    wb["ffn_w_in"] = _interleave(wb["ffn_w_in"])

    c_all = allgather8(jnp.broadcast_to(c, (8, D)), name="gather_c")[:, 0, :]
    nsh = ada_w.shape[2]
    ada_b_sh = lax.dynamic_slice_in_dim(ada_b, chip * nsh, nsh, axis=1)[:, None, :]
    mod_part = ada_mod(c_all, ada_w, ada_b_sh, name="ada_mod")
    mod_all = allgather8(mod_part.reshape(DEPTH * N_DEV, nsh), name="gather_mod")
    mod_all = mod_all.reshape(4, 2, DEPTH, N_DEV, nsh)[:, 0]
    mod = lax.dynamic_index_in_dim(mod_all, dev, axis=2, keepdims=False)
    mod = jnp.moveaxis(mod, 0, 1).reshape(DEPTH, 6, 1, D)

    cfg_a = _Attn(s, mixer="a")
    cfg_b = [_Attn(s, mixer="b", group=g) for g in range(3)]
    no_sink = jnp.full((1, 2, GQ * 64, 1), NEG, f32)

    saved = []
    xc = x0
    for i in range(DEPTH):
        j = i // 2
        sh1, sc1, g1, sh2, sc2, g2 = (mod[i, q] for q in range(6))
        nmix, nffn = norm_mix[i][None, :], norm_ffn[i][None, :]
        if i % 2 == 0:
            h, qkv = mm_norm(xc, nmix, sc1, sh1, wb["a_w_in"][j], swiglu=False, name="a_qkv")
            sinkcol = jnp.repeat(a_sink[j].reshape(2, 2, GQ), 128, axis=2)[..., None]
            o, lse = attn_fwd(qkv, sinkcol, cfg_a, out_dtype=bf16, name="a_attn_fwd")
            w_o = wb["a_w_out"][j]
        else:
            h, qkv = mm_norm(xc, nmix, sc1, sh1, wb["b_w_in"][j], swiglu=False, name="b_qkv")
            outs = [attn_fwd(qkv, no_sink, cfg_b[g], out_dtype=f32, name=f"b_attn_fwd{g}") for g in range(3)]
            o, lse = attn_merge([t[0] for t in outs], [t[1] for t in outs], name="b_merge")
            w_o = wb["b_w_out"][j]
        x1 = mm_resid(o, w_o, xc, g1, name="a_out" if i % 2 == 0 else "b_out")
        h2, gu, act = mm_norm(x1, nffn, sc2, sh2, wb["ffn_w_in"][i], swiglu=True, name="ffn_in")
        x2 = mm_resid(act, wb["ffn_w_out"][i], x1, g2, name="ffn_out")
        saved.append((xc, h, qkv, o, lse, x1, h2, gu, act))
        xc = x2

    dx, st_final = loss_head(xc, final_norm[None, :], tgt, name="loss_head")

    zero_row = jnp.zeros((1, D), f32)
    dmod_rows = [None] * DEPTH
    d_nmix, d_nffn = [None] * DEPTH, [None] * DEPTH
    d_sink = [None] * 2
    gw = {k: [None] * big[k][0].shape[0] for k in names}
    for i in reversed(range(DEPTH)):
        j = i // 2
        xin, h, qkv, o, lse, x1, h2, gu, act = saved[i]
        sh1, sc1, g1, sh2, sc2, g2 = (mod[i, q] for q in range(6))
        nmix, nffn = norm_mix[i][None, :], norm_ffn[i][None, :]
        w_fo = wb["ffn_w_out"][i]
        dgu = mm_nt_scaled(dx, g2, w_fo, gu, name="ffn_dact")
        gw["ffn_w_out"][i], dg2 = mm_tn(act, dx, (g2, w_fo), name="ffn_dw_out")
        gw["ffn_w_in"][i] = _deinterleave(mm_tn(h2, dgu, name="ffn_dw_in"))
        dh2 = mm_nt_acc(dgu, wb["ffn_w_in"][i], name="ffn_dh")
        dx1, st2 = norm_bwd(x1, dh2, dx, nffn, sc2, name="norm_bwd")
        if i % 2 == 0:
            w_o, w_i, kin, kout = wb["a_w_out"][j], wb["a_w_in"][j], "a_w_in", "a_w_out"
        else:
            w_o, w_i, kin, kout = wb["b_w_out"][j], wb["b_w_in"][j], "b_w_in", "b_w_out"
        do = mm_nt_scaled(dx1, g1, w_o, name="a_do" if i % 2 == 0 else "b_do")
        gw[kout][j], dg1 = mm_tn(o, dx1, (g1, w_o), name="a_dw_out" if i % 2 == 0 else "b_dw_out")
        if i % 2 == 0:
            sinkrow = jnp.pad(a_sink[j].reshape(2, 8), ((0, 0), (0, 120))).reshape(1, 256)
            delta, dsk = attn_delta(do, o, lse, sinkrow, name="a_delta")
            d_sink[j] = dsk[0].reshape(2, 128)[:, :8].reshape(16)
            dq, dk, dv = attn_bwd(qkv, do, lse, delta, cfg_a, name="a_attn_bwd")
            dqkv = jnp.concatenate([dq, dk, dv], axis=1)
        else:
            delta, _ = attn_delta(do, o, lse, jnp.zeros((1, 128), f32), name="b_delta")
            gr = [attn_bwd(qkv, do, lse, delta, cfg_b[g], name=f"b_attn_bwd{g}") for g in range(3)]
            dqkv = jnp.concatenate([gr[g][q] for q in range(3) for g in range(3)], axis=1)
        gw[kin][j] = mm_tn(h, dqkv, name="a_dw_in" if i % 2 == 0 else "b_dw_in")
        dh = mm_nt_acc(dqkv, w_i, name="a_dh" if i % 2 == 0 else "b_dh")
        dx, st1 = norm_bwd(xin, dh, dx1, nmix, sc1, name="norm_bwd")
        dmod_rows[i] = jnp.concatenate([st1[2:3], st1[1:2], dg1, st2[2:3], st2[1:2], dg2], axis=0)
        d_nmix[i], d_nffn[i] = st1[0:1], st2[0:1]

    sink_row = jnp.pad(jnp.concatenate(d_sink), (0, D - 32))[None, :]
    stats = jnp.concatenate(dmod_rows + d_nmix + d_nffn + [sink_row, st_final[0:1], st_final[1:2]]
                            + [zero_row] * (STAT_ROWS - 35), axis=0)
    stats_all = allgather8(stats, name="gather_stats")
    tot = sum_devices(stats_all, name="sum_stats")
    loss = 0.5 * jnp.sum(tot[34]) / float(D)

    def pack(ab, nm, nf, sk, fnm, fill):
        return jnp.concatenate([ab.reshape(24, D), nm, nf, jnp.pad(sk.reshape(1, 32), ((0, 0), (0, D - 32)), constant_values=fill),
                                fnm[None, :], jnp.full((STAT_ROWS - 34, D), fill, f32)], axis=0)

    sd, sm, sv = adamw(pack(ada_b, norm_mix, norm_ffn, a_sink, final_norm, 0.0),
                       pack(m_ada_b, m_norm_mix, m_norm_ffn, m_a_sink, m_final_norm, 0.0),
                       pack(v_ada_b, v_norm_mix, v_norm_ffn, v_a_sink, v_final_norm, 1.0), tot, name="adamw_small")

    def unpack(p):
        return p[0:24].reshape(DEPTH, 6 * D), p[24:28], p[28:32], p[32, :32].reshape(2, 16), p[33]

    small = {"grad": unpack(tot), "delta": unpack(sd), "m": unpack(sm), "v": unpack(sv)}

    dmod_all = stats_all[:, 0:24, :].reshape(N_DEV, DEPTH, 6 * D)
    dm_sh = jnp.moveaxis(lax.dynamic_slice_in_dim(dmod_all, chip * nsh, nsh, axis=2), 0, 1)
    g_ada = ada_grad(c_all.T, dm_sh, name="ada_grad")
    r_ada = (DEPTH * D, nsh)
    ada_res = adamw(ada_w.reshape(r_ada), m_ada_w.reshape(r_ada), v_ada_w.reshape(r_ada), g_ada.reshape(r_ada), name="adamw_ada")
    ada_out = [g_ada] + [t.reshape(ada_w.shape) for t in ada_res]

    parts = [jnp.stack([_chunks(gl, big[k][1]) for gl in gw[k]], axis=1) for k in names]
    parts = [p.reshape(4, -1, p.shape[-1]) for p in parts]
    own, sibs = exchange_grads(parts, name="exchange_grads")
    big_out = {}
    mom = {"ffn_w_in": (m_ffn_w_in, v_ffn_w_in), "ffn_w_out": (m_ffn_w_out, v_ffn_w_out), "a_w_in": (m_a_w_in, v_a_w_in),
           "a_w_out": (m_a_w_out, v_a_w_out), "b_w_in": (m_b_w_in, v_b_w_in), "b_w_out": (m_b_w_out, v_b_w_out)}
    for k, o_, s_ in zip(names, own, sibs):
        w = big[k][0]
        r2 = (-1, w.shape[-1])
        res = adamw_parts(w.reshape(r2), mom[k][0].reshape(r2), mom[k][1].reshape(r2), o_, s_, name="adamw_" + k)
        big_out[k] = [t.reshape(w.shape) for t in res]

    def leaves(q):
        sm_ = small[("grad", "delta", "m", "v")[q]]
        return (ada_out[q], sm_[0], sm_[1], sm_[2], big_out["ffn_w_in"][q], big_out["ffn_w_out"][q], big_out["a_w_in"][q],
                big_out["a_w_out"][q], sm_[3], big_out["b_w_in"][q], big_out["b_w_out"][q], sm_[4])

    return (loss, dx[None], *leaves(0), *leaves(1), *leaves(2), *leaves(3))
```

```python
import functools
import math

import numpy as np
import jax
import jax.numpy as jnp
from jax import lax
from jax.experimental import pallas as pl
from jax.experimental.pallas import tpu as pltpu

f32 = jnp.float32
bf16 = jnp.bfloat16

D = 1024
DH = 64
GQ = 4
DEPTH = 4
F = 2816
A_QKV, A_OUT = 1536, 1024
B_QKV, B_OUT = 2304, 512
B_GROUPS = ((128, 1), (512, 4), (2048, 16))
RMS_EPS = 1e-6
NEG = -1e30
LR, B1, B2, ADAM_EPS, WD, STEP = 0.001, 0.9, 0.999, 1e-08, 0.01, 10
N_DEV = 8
STAT_ROWS = 40
MESH = pl.DeviceIdType.MESH
ANY = pl.BlockSpec(memory_space=pl.ANY)


def _pcall(body, **kw):
    return pl.pallas_call(body, **kw)


def _params(*sem):
    return pltpu.CompilerParams(dimension_semantics=sem, vmem_limit_bytes=56 * 1024 * 1024)


def _row_tile(s, want=1024):
    return want if s % want == 0 else s


def mm_norm(x, nw, sc, sh, w, *, name):
    s, d = x.shape
    n = w.shape[1]
    tm = _row_tile(s)
    tn = n // 3

    def body(x_ref, nw_ref, sc_ref, sh_ref, w_ref, h_ref, y_ref):
        @pl.when(pl.program_id(1) == 0)
        def _():
            xv = x_ref[...]
            r = lax.rsqrt(jnp.mean(xv * xv, axis=-1, keepdims=True) + RMS_EPS)
            h_ref[...] = ((xv * r * nw_ref[...]) * (1.0 + sc_ref[...]) + sh_ref[...]).astype(bf16)

        y_ref[...] = jnp.dot(h_ref[...], w_ref[...], preferred_element_type=f32).astype(bf16)

    vec = pl.BlockSpec((1, d), lambda i, j: (0, 0))
    return _pcall(
        body, name=name, grid=(s // tm, n // tn),
        in_specs=[pl.BlockSpec((tm, d), lambda i, j: (i, 0)), vec, vec, vec, pl.BlockSpec((d, tn), lambda i, j: (0, j))],
        out_specs=[pl.BlockSpec((tm, d), lambda i, j: (i, 0)), pl.BlockSpec((tm, tn), lambda i, j: (i, j))],
        out_shape=[jax.ShapeDtypeStruct((s, d), bf16), jax.ShapeDtypeStruct((s, n), bf16)],
        compiler_params=_params("parallel", "arbitrary"),
    )(x, nw, sc, sh, w)


def mm_resid(a, w, xres, g, *, name):
    s, k = a.shape
    n = w.shape[1]
    tm, tn = _row_tile(s), 512

    def body(a_ref, w_ref, x_ref, g_ref, o_ref):
        o_ref[...] = x_ref[...] + g_ref[...] * jnp.dot(a_ref[...], w_ref[...], preferred_element_type=f32)

    return _pcall(
        body, name=name, grid=(s // tm, n // tn),
        in_specs=[pl.BlockSpec((tm, k), lambda i, j: (i, 0)), pl.BlockSpec((k, tn), lambda i, j: (0, j)),
                  pl.BlockSpec((tm, tn), lambda i, j: (i, j)), pl.BlockSpec((1, tn), lambda i, j: (0, j))],
        out_specs=pl.BlockSpec((tm, tn), lambda i, j: (i, j)), out_shape=jax.ShapeDtypeStruct((s, n), f32),
        compiler_params=_params("parallel", "arbitrary"),
    )(a, w, xres, g)


def mm_nt_scaled(dx, g, w, *, name):
    s, d = dx.shape
    n = w.shape[0]
    tm, tn = _row_tile(s), 512

    def body(dx_ref, g_ref, w_ref, o_ref, a_ref):
        @pl.when(pl.program_id(1) == 0)
        def _():
            a_ref[...] = (dx_ref[...] * g_ref[...]).astype(bf16)

        o_ref[...] = lax.dot_general(a_ref[...], w_ref[...], (((1,), (1,)), ((), ())), preferred_element_type=f32).astype(bf16)

    return _pcall(
        body, name=name, grid=(s // tm, n // tn),
        in_specs=[pl.BlockSpec((tm, d), lambda i, j: (i, 0)), pl.BlockSpec((1, d), lambda i, j: (0, 0)),
                  pl.BlockSpec((tn, d), lambda i, j: (j, 0))],
        out_specs=pl.BlockSpec((tm, tn), lambda i, j: (i, j)), out_shape=jax.ShapeDtypeStruct((s, n), bf16),
        scratch_shapes=[pltpu.VMEM((tm, d), bf16)], compiler_params=_params("parallel", "arbitrary"),
    )(dx, g, w)


def mm_nt_norm_bwd(a, w, x, dres, nw, sc, *, name):
    s, k = a.shape
    d = w.shape[0]
    tm = _row_tile(s, 512)
    tk = 768 if k % 768 == 0 and k % 512 != 0 else 512
    nk = k // tk

    def body(a_ref, w_ref, x_ref, dr_ref, nw_ref, sc_ref, o_ref, st_ref, acc):
        kk = pl.program_id(1)
        part = lax.dot_general(a_ref[...], w_ref[...], (((1,), (1,)), ((), ())), preferred_element_type=f32)

        @pl.when(kk == 0)
        def _():
            acc[...] = part

        @pl.when(kk != 0)
        def _():
            acc[...] += part

        @pl.when(kk == nk - 1)
        def _():
            dh = acc[...]
            xv = x_ref[...]
            r = lax.rsqrt(jnp.mean(xv * xv, axis=-1, keepdims=True) + RMS_EPS)
            xh = xv * r
            dn = dh * (1.0 + sc_ref[...])
            dxh = dn * nw_ref[...]
            o_ref[...] = dr_ref[...] + r * (dxh - xh * jnp.mean(dxh * xh, axis=-1, keepdims=True))
            rows = jnp.concatenate([
                jnp.sum(dn * xh, axis=0, keepdims=True),
                jnp.sum(dh * (xh * nw_ref[...]), axis=0, keepdims=True),
                jnp.sum(dh, axis=0, keepdims=True),
                jnp.zeros((5, d), f32)], axis=0)

            @pl.when(pl.program_id(0) == 0)
            def _():
                st_ref[...] = rows

            @pl.when(pl.program_id(0) != 0)
            def _():
                st_ref[...] += rows

    big = pl.BlockSpec((tm, d), lambda i, kk: (i, 0))
    vec = pl.BlockSpec((1, d), lambda i, kk: (0, 0))
    return _pcall(
        body, name=name, grid=(s // tm, nk),
        in_specs=[pl.BlockSpec((tm, tk), lambda i, kk: (i, kk)), pl.BlockSpec((d, tk), lambda i, kk: (0, kk)), big, big, vec, vec],
        out_specs=[big, pl.BlockSpec((8, d), lambda i, kk: (0, 0))],
        out_shape=[jax.ShapeDtypeStruct((s, d), f32), jax.ShapeDtypeStruct((8, d), f32)],
        scratch_shapes=[pltpu.VMEM((tm, d), f32)], compiler_params=_params("arbitrary", "arbitrary"),
    )(a, w, x, dres, nw, sc)


def mm_tn(a, b, scale=None, *, name):
    s, ka = a.shape
    nb = b.shape[1]
    ts = _row_tile(s)
    tn = 768 if nb % 768 == 0 and nb % 512 != 0 else 512
    tka = 1408 if ka % 1408 == 0 else min(ka, 1024)
    ns = s // ts

    def body(a_ref, b_ref, *rest):
        o_ref = rest[2] if scale is not None else rest[0]
        si = pl.program_id(2)
        part = lax.dot_general(a_ref[...], b_ref[...].astype(bf16), (((0,), (0,)), ((), ())), preferred_element_type=f32)

        @pl.when(si == 0)
        def _():
            o_ref[...] = part

        @pl.when(si != 0)
        def _():
            o_ref[...] += part

        if scale is not None:
            g_ref, wb_ref, dg_ref = rest[0], rest[1], rest[3]

            @pl.when(si == ns - 1)
            def _():
                gm = o_ref[...]
                dgp = jnp.sum(wb_ref[...].astype(f32) * gm, axis=0, keepdims=True)

                @pl.when(pl.program_id(1) == 0)
                def _():
                    dg_ref[...] = dgp

                @pl.when(pl.program_id(1) != 0)
                def _():
                    dg_ref[...] += dgp

                o_ref[...] = gm * g_ref[...]

    in_specs = [pl.BlockSpec((ts, tka), lambda j, i, k: (k, i)), pl.BlockSpec((ts, tn), lambda j, i, k: (k, j))]
    args = [a, b]
    out_specs = [pl.BlockSpec((tka, tn), lambda j, i, k: (i, j))]
    out_shape = [jax.ShapeDtypeStruct((ka, nb), f32)]
    if scale is not None:
        in_specs += [pl.BlockSpec((1, tn), lambda j, i, k: (0, j)), pl.BlockSpec((tka, tn), lambda j, i, k: (i, j))]
        args += list(scale)
        out_specs.append(pl.BlockSpec((1, tn), lambda j, i, k: (0, j)))
        out_shape.append(jax.ShapeDtypeStruct((1, nb), f32))
    res = _pcall(
        body, name=name, grid=(nb // tn, ka // tka, ns), in_specs=in_specs, out_specs=out_specs, out_shape=out_shape,
        compiler_params=_params("arbitrary", "arbitrary", "arbitrary"),
    )(*args)
    return res if scale is not None else res[0]


def loss_head(x, fn, tgt, *, name):
    s, d = x.shape
    tm = _row_tile(s, 512)

    def body(x_ref, fn_ref, t_ref, dx_ref, st_ref):
        xv = x_ref[...]
        r = lax.rsqrt(jnp.mean(xv * xv, axis=-1, keepdims=True) + RMS_EPS)
        xh = xv * r
        err = xh * fn_ref[...] - t_ref[...]
        dy = err / float(d)
        dxh = dy * fn_ref[...]
        dx_ref[...] = r * (dxh - xh * jnp.mean(dxh * xh, axis=-1, keepdims=True))
        rows = jnp.concatenate([
            jnp.sum(dy * xh, axis=0, keepdims=True),
            jnp.sum(err * err, axis=0, keepdims=True),
            jnp.zeros((6, d), f32)], axis=0)

        @pl.when(pl.program_id(0) == 0)
        def _():
            st_ref[...] = rows

        @pl.when(pl.program_id(0) != 0)
        def _():
            st_ref[...] += rows

    big = pl.BlockSpec((tm, d), lambda i: (i, 0))
    return _pcall(
        body, name=name, grid=(s // tm,), in_specs=[big, pl.BlockSpec((1, d), lambda i: (0, 0)), big],
        out_specs=[big, pl.BlockSpec((8, d), lambda i: (0, 0))],
        out_shape=[jax.ShapeDtypeStruct((s, d), f32), jax.ShapeDtypeStruct((8, d), f32)],
        compiler_params=_params("arbitrary"),
    )(x, fn, tgt)


FC = 2 * F // 4
FFN_ROWS = 256


def _resident(pairs, sems):
    @pl.when(pl.program_id(0) == 0)
    def _():
        cps = [pltpu.make_async_copy(h, v, sems.at[i]) for i, (h, v) in enumerate(pairs)]
        for cp in cps:
            cp.start()
        for cp in cps:
            cp.wait()


def ffn_fwd(x, nw, sc, sh, g, w_in, w_out, *, name):
    s, d = x.shape
    tm = _row_tile(s, FFN_ROWS)

    def body(x_ref, nw_ref, sc_ref, sh_ref, g_ref, win_hbm, wout_hbm, h_ref, gu_ref, a_ref, o_ref, win_v, wout_v, sems):
        _resident([(win_hbm, win_v), (wout_hbm, wout_v)], sems)
        xv = x_ref[...]
        r = lax.rsqrt(jnp.mean(xv * xv, axis=-1, keepdims=True) + RMS_EPS)
        h = ((xv * r * nw_ref[...]) * (1.0 + sc_ref[...]) + sh_ref[...]).astype(bf16)
        h_ref[...] = h
        y = None
        for c in range(2):
            cs = slice(c * FC, (c + 1) * FC)
            gt = jnp.dot(h, win_v[c], preferred_element_type=f32)
            up = jnp.dot(h, win_v[c + 2], preferred_element_type=f32)
            gu_ref[0, :, cs] = gt.astype(bf16)
            gu_ref[1, :, cs] = up.astype(bf16)
            act = (gt * jax.nn.sigmoid(gt) * up).astype(bf16)
            a_ref[:, cs] = act
            part = jnp.dot(act, wout_v[cs, :], preferred_element_type=f32)
            y = part if y is None else y + part
        o_ref[...] = xv + g_ref[...] * y

    big = pl.BlockSpec((tm, d), lambda i: (i, 0))
    vec = pl.BlockSpec((1, d), lambda i: (0, 0))
    return _pcall(
        body, name=name, grid=(s // tm,), in_specs=[big, vec, vec, vec, vec, ANY, ANY],
        out_specs=[big, pl.BlockSpec((2, tm, F), lambda i: (0, i, 0)), pl.BlockSpec((tm, F), lambda i: (i, 0)), big],
        out_shape=[jax.ShapeDtypeStruct((s, d), bf16), jax.ShapeDtypeStruct((2, s, F), bf16),
                   jax.ShapeDtypeStruct((s, F), bf16), jax.ShapeDtypeStruct((s, d), f32)],
        scratch_shapes=[pltpu.VMEM((4, d, FC), bf16), pltpu.VMEM((F, d), bf16), pltpu.SemaphoreType.DMA((2,))],
        compiler_params=_params("arbitrary"),
    )(x, nw, sc, sh, g, w_in, w_out)


def ffn_bwd_rows(dx, x, gu, g, nw, sc, w_in, w_out, *, name):
    s, d = x.shape
    tm = _row_tile(s, FFN_ROWS)
    nt_dims = (((1,), (1,)), ((), ()))

    def body(dx_ref, x_ref, gu_ref, g_ref, nw_ref, sc_ref, win_hbm, wout_hbm, dgu_ref, o_ref, st_ref, win_v, wout_v, sems):
        _resident([(win_hbm, win_v), (wout_hbm, wout_v)], sems)
        dxv = dx_ref[...]
        a = (dxv * g_ref[...]).astype(bf16)
        dh = None
        for c in range(2):
            cs = slice(c * FC, (c + 1) * FC)
            da = lax.dot_general(a, wout_v[cs, :], nt_dims, preferred_element_type=f32)
            gt = gu_ref[0, :, cs].astype(f32)
            up = gu_ref[1, :, cs].astype(f32)
            sg = jax.nn.sigmoid(gt)
            dgate = (da * up * (sg * (1.0 + gt * (1.0 - sg)))).astype(bf16)
            dup = (da * (gt * sg)).astype(bf16)
            dgu_ref[0, :, cs] = dgate
            dgu_ref[1, :, cs] = dup
            part = (lax.dot_general(dgate, win_v[c], nt_dims, preferred_element_type=f32)
                    + lax.dot_general(dup, win_v[c + 2], nt_dims, preferred_element_type=f32))
            dh = part if dh is None else dh + part
        xv = x_ref[...]
        r = lax.rsqrt(jnp.mean(xv * xv, axis=-1, keepdims=True) + RMS_EPS)
        xh = xv * r
        dn = dh * (1.0 + sc_ref[...])
        dxh = dn * nw_ref[...]
        o_ref[...] = dxv + r * (dxh - xh * jnp.mean(dxh * xh, axis=-1, keepdims=True))
        rows = jnp.concatenate([
            jnp.sum(dn * xh, axis=0, keepdims=True),
            jnp.sum(dh * (xh * nw_ref[...]), axis=0, keepdims=True),
            jnp.sum(dh, axis=0, keepdims=True),
            jnp.zeros((5, d), f32)], axis=0)

        @pl.when(pl.program_id(0) == 0)
        def _():
            st_ref[...] = rows

        @pl.when(pl.program_id(0) != 0)
        def _():
            st_ref[...] += rows

    big = pl.BlockSpec((tm, d), lambda i: (i, 0))
    vec = pl.BlockSpec((1, d), lambda i: (0, 0))
    gus = pl.BlockSpec((2, tm, F), lambda i: (0, i, 0))
    return _pcall(
        body, name=name, grid=(s // tm,), in_specs=[big, big, gus, vec, vec, vec, ANY, ANY],
        out_specs=[gus, big, pl.BlockSpec((8, d), lambda i: (0, 0))],
        out_shape=[jax.ShapeDtypeStruct((2, s, F), bf16), jax.ShapeDtypeStruct((s, d), f32), jax.ShapeDtypeStruct((8, d), f32)],
        scratch_shapes=[pltpu.VMEM((4, d, FC), bf16), pltpu.VMEM((F, d), bf16), pltpu.SemaphoreType.DMA((2,))],
        compiler_params=_params("arbitrary"),
    )(dx, x, gu, g, nw, sc, w_in, w_out)


def ffn_dw_in(h, dgu, *, name):
    s, d = h.shape
    ts = _row_tile(s)
    ns = s // ts
    tn_dims = (((0,), (0,)), ((), ()))

    def body(h_ref, dgu_ref, o_ref, acc):
        k = pl.program_id(1)

        @pl.when(k == 0)
        def _():
            acc[...] = jnp.zeros_like(acc)

        hv = h_ref[...]
        for c in range(2):
            acc[c] += lax.dot_general(hv, dgu_ref[:, c * FC:(c + 1) * FC], tn_dims, preferred_element_type=f32)

        @pl.when(k == ns - 1)
        def _():
            o_ref[...] = acc[...].astype(bf16)

    return _pcall(
        body, name=name, grid=(2, ns),
        in_specs=[pl.BlockSpec((ts, d), lambda hf, k: (k, 0)), pl.BlockSpec((None, ts, F), lambda hf, k: (hf, k, 0))],
        out_specs=pl.BlockSpec((2, d, FC), lambda hf, k: (hf, 0, 0)),
        out_shape=jax.ShapeDtypeStruct((4, d, FC), bf16), scratch_shapes=[pltpu.VMEM((2, d, FC), f32)],
        compiler_params=_params("arbitrary", "arbitrary"),
    )(h, dgu)


def ffn_dw_out(a, dx, g, wb, *, name):
    s, fdim = a.shape
    d = dx.shape[1]
    ts = _row_tile(s)
    ns = s // ts
    tn = d // 2
    tn_dims = (((0,), (0,)), ((), ()))

    def body(a_ref, dx_ref, g_ref, wb_ref, o_ref, dg_ref, acc):
        k = pl.program_id(1)

        @pl.when(k == 0)
        def _():
            acc[...] = jnp.zeros_like(acc)

        acc[...] += lax.dot_general(a_ref[...], dx_ref[...].astype(bf16), tn_dims, preferred_element_type=f32)

        @pl.when(k == ns - 1)
        def _():
            gm = acc[...]
            dg_ref[...] = jnp.concatenate([jnp.sum(wb_ref[...].astype(f32) * gm, axis=0, keepdims=True),
                                           jnp.zeros((7, tn), f32)], axis=0)
            o_ref[...] = (gm * g_ref[...]).astype(bf16)

    return _pcall(
        body, name=name, grid=(2, ns),
        in_specs=[pl.BlockSpec((ts, fdim), lambda j, k: (k, 0)), pl.BlockSpec((ts, tn), lambda j, k: (k, j)),
                  pl.BlockSpec((1, tn), lambda j, k: (0, j)), pl.BlockSpec((fdim, tn), lambda j, k: (0, j))],
        out_specs=[pl.BlockSpec((fdim, tn), lambda j, k: (0, j)), pl.BlockSpec((8, tn), lambda j, k: (0, j))],
        out_shape=[jax.ShapeDtypeStruct((fdim, d), bf16), jax.ShapeDtypeStruct((8, d), f32)],
        scratch_shapes=[pltpu.VMEM((fdim, tn), f32)], compiler_params=_params("arbitrary", "arbitrary"),
    )(a, dx, g, wb)


def _alibi(n):
    return np.asarray(2.0 ** (-8.0 * np.arange(1, n + 1) / n), dtype=np.float32)


class _Attn:
    def __init__(self, s, *, mixer, group=0):
        if mixer == "a":
            self.blk, self.dil, self.c, self.npairs = 128, 1, A_QKV, 2
            self.qb0, self.kb0, self.vb0 = 0, 8, 10
            slopes = _alibi(16).reshape(2, 2, GQ)
        else:
            window, dil = B_GROUPS[group]
            self.blk, self.dil, self.c, self.npairs = window // (2 * dil), dil, B_QKV, 1
            self.qb0, self.kb0, self.vb0 = 2 * group, 12 + group, 15 + group
            slopes = _alibi(24).reshape(3, 1, 2, GQ)[group]
        self.l = s // self.dil
        self.t = min(512, self.l)
        self.nt = self.l // self.t
        self.nb = self.t // self.blk
        blk = self.blk
        qi = np.arange(blk)[:, None]
        rel = np.arange(3 * blk)[None, :] - blk - qi
        dist = (self.dil * np.abs(rel)).astype(np.float32)
        bias = -slopes[:, :, :, None, None] * dist[None, None, None]
        bias = np.where(np.abs(rel) <= blk, bias, np.float32(NEG)).astype(np.float32)
        self.bias = bias.reshape(self.npairs, 2, GQ * blk, 3 * blk)

    def grid(self):
        return (self.dil, self.npairs, self.nt)

    def halo(self, width, col):
        t, blk, nbl = self.t, self.blk, self.l // self.blk
        per = t // blk
        return [
            pl.BlockSpec((blk, width), lambda r, hp, i: (jnp.maximum(i * per - 1, 0), col(r, hp))),
            pl.BlockSpec((t, width), lambda r, hp, i: (i, col(r, hp))),
            pl.BlockSpec((blk, width), lambda r, hp, i: (jnp.minimum((i + 1) * per, nbl - 1), col(r, hp))),
        ]

    def qcol(self, e):
        return lambda r, hp: r * (self.c // 256) + self.qb0 + 2 * hp + e

    def kcol(self, r, hp):
        return r * (self.c // 128) + self.kb0 + hp

    def vcol(self, r, hp):
        return r * (self.c // 128) + self.vb0 + hp

    def pcol(self, r, hp):
        return r * self.npairs + hp


def _stack_heads(x):
    return jnp.concatenate([x[:, g * DH:(g + 1) * DH] for g in range(GQ)], axis=0)


def _unstack_heads(x, rows):
    return jnp.concatenate([x[g * rows:(g + 1) * rows] for g in range(GQ)], axis=1)


def _head_cols(tile, hh, rows):
    return jnp.concatenate([tile[:, hh * GQ + g:hh * GQ + g + 1] for g in range(GQ)], axis=0)


def attn_fwd(qkv, sinkcol, cfg, *, out_dtype, name):
    s = qkv.shape[0]
    blk, t, nb, nt, dil = cfg.blk, cfg.t, cfg.nb, cfg.nt, cfg.dil
    view = qkv.reshape(cfg.l, dil * cfg.c)
    scale = DH ** -0.5

    def body(q0, q1, kp, km, kn, vp, vm, vn, bias_ref, sink_ref, o_ref, lse_ref, kx, vx):
        ti = pl.program_id(2)
        first, last = ti == 0, ti == nt - 1
        for hh in range(2):
            sl = slice(hh * DH, (hh + 1) * DH)
            for dst, (p_, m_, n_) in ((kx, (kp, km, kn)), (vx, (vp, vm, vn))):
                dst[hh, 0:blk] = p_[:, sl]
                dst[hh, blk:blk + t] = m_[:, sl]
                dst[hh, blk + t:] = n_[:, sl]
        col = lax.broadcasted_iota(jnp.int32, (GQ * blk, 3 * blk), 1)
        lane = lax.broadcasted_iota(jnp.int32, (blk, 128), 1)
        pairs = [(b, hh) for b in range(nb) for hh in range(2)]
        qs = [_stack_heads((q0, q1)[hh][b * blk:(b + 1) * blk, :]) for b, hh in pairs]
        sc = [lax.dot_general(q_, kx[hh, b * blk:(b + 3) * blk, :], (((1,), (1,)), ((), ())), preferred_element_type=f32)
              for q_, (b, hh) in zip(qs, pairs)]
        sc = [s_ * scale + bias_ref[0, hh] for s_, (b, hh) in zip(sc, pairs)]
        sc = [jnp.where(jnp.logical_and(first, col < blk), NEG, s_) if b == 0 else s_ for s_, (b, hh) in zip(sc, pairs)]
        sc = [jnp.where(jnp.logical_and(last, col >= 2 * blk), NEG, s_) if b == nb - 1 else s_ for s_, (b, hh) in zip(sc, pairs)]
        ms = [jnp.maximum(jnp.max(s_, axis=-1, keepdims=True), sink_ref[0, hh]) for s_, (b, hh) in zip(sc, pairs)]
        ps = [jnp.exp(s_ - m_) for s_, m_ in zip(sc, ms)]
        ls = [jnp.sum(p_, axis=-1, keepdims=True) + jnp.exp(sink_ref[0, hh] - m_) for p_, m_, (b, hh) in zip(ps, ms, pairs)]
        os_ = [jnp.dot(p_.astype(bf16), vx[hh, b * blk:(b + 3) * blk, :], preferred_element_type=f32)
               for p_, (b, hh) in zip(ps, pairs)]
        os_ = [o_ / l_ for o_, l_ in zip(os_, ls)]
        lses = [m_ + jnp.log(l_) for m_, l_ in zip(ms, ls)]
        for o_, (b, hh) in zip(os_, pairs):
            o_ref[b * blk:(b + 1) * blk, hh * 256:(hh + 1) * 256] = _unstack_heads(o_, blk).astype(out_dtype)
        for b in range(nb):
            lse_tile = jnp.zeros((blk, 128), f32)
            for hh in range(2):
                lse = lses[2 * b + hh]
                for g in range(GQ):
                    lse_tile = jnp.where(lane == hh * GQ + g, lse[g * blk:(g + 1) * blk], lse_tile)
            lse_ref[b * blk:(b + 1) * blk, :] = lse_tile

    in_specs = [pl.BlockSpec((t, 256), lambda r, hp, i, e=e: (i, cfg.qcol(e)(r, hp))) for e in range(2)]
    in_specs += cfg.halo(128, cfg.kcol) + cfg.halo(128, cfg.vcol)
    in_specs += [pl.BlockSpec((1, 2, GQ * blk, 3 * blk), lambda r, hp, i: (hp, 0, 0, 0)),
                 pl.BlockSpec((1, 2, GQ * blk, 1), lambda r, hp, i: (hp, 0, 0, 0))]
    o, lse = _pcall(
        body, name=name, grid=cfg.grid(), in_specs=in_specs,
        out_specs=[pl.BlockSpec((t, 512), lambda r, hp, i: (i, cfg.pcol(r, hp))),
                   pl.BlockSpec((t, 128), lambda r, hp, i: (i, cfg.pcol(r, hp)))],
        out_shape=[jax.ShapeDtypeStruct((cfg.l, dil * cfg.npairs * 512), out_dtype),
                   jax.ShapeDtypeStruct((cfg.l, dil * cfg.npairs * 128), f32)],
        scratch_shapes=[pltpu.VMEM((2, t + 2 * blk, DH), bf16), pltpu.VMEM((2, t + 2 * blk, DH), bf16)],
        compiler_params=_params("arbitrary", "arbitrary", "arbitrary"),
    )(*([view] * 8), jnp.asarray(cfg.bias), sinkcol)
    return o.reshape(s, cfg.npairs * 512), lse.reshape(s, cfg.npairs * 128)


def attn_bwd(qkv, do, lse, delta, cfg, *, name):
    s = qkv.shape[0]
    blk, t, nb, nt, dil, npairs = cfg.blk, cfg.t, cfg.nb, cfg.nt, cfg.dil, cfg.npairs
    view = qkv.reshape(cfg.l, dil * cfg.c)
    dov = do.reshape(cfg.l, dil * npairs * 512)
    lsev = lse.reshape(cfg.l, dil * npairs * 128)
    dlv = delta.reshape(cfg.l, dil * npairs * 128)
    scale = DH ** -0.5
    nt_dims = (((1,), (1,)), ((), ()))
    tn_dims = (((0,), (0,)), ((), ()))

    def body(q0p, q0m, q0n, q1p, q1m, q1n, kp, km, kn, vp, vm, vn, dop, dom, don, lp, lm, ln, dp_, dm_, dn_,
             bias_ref, dq_ref, dk_ref, dv_ref, kx, vx, dkx, dvx):
        ti = pl.program_id(2)
        first, last = ti == 0, ti == nt - 1
        for hh in range(2):
            sl = slice(hh * DH, (hh + 1) * DH)
            for dst, (p_, m_, n_) in ((kx, (kp, km, kn)), (vx, (vp, vm, vn))):
                dst[hh, 0:blk] = p_[:, sl]
                dst[hh, blk:blk + t] = m_[:, sl]
                dst[hh, blk + t:] = n_[:, sl]
        dkx[...] = jnp.zeros_like(dkx)
        dvx[...] = jnp.zeros_like(dvx)

        def slab(prev, main, nxt, e):
            if e == 0:
                return prev[...]
            if e == nb + 1:
                return nxt[...]
            return main[(e - 1) * blk:e * blk, :]

        col3 = lax.broadcasted_iota(jnp.int32, (GQ * blk, 3 * blk), 1)

        def keys(e):
            if e == 0:
                return 1, 2, slice(2 * blk, 3 * blk)
            if e == nb + 1:
                return nb, nb + 1, slice(0, blk)
            return e - 1, e + 2, slice(0, 3 * blk)

        def edge(sc, e):
            if e == 0:
                return jnp.where(first, NEG, sc)
            if e == nb + 1:
                return jnp.where(last, NEG, sc)
            if e == 1:
                sc = jnp.where(jnp.logical_and(first, col3 < blk), NEG, sc)
            if e == nb:
                sc = jnp.where(jnp.logical_and(last, col3 >= 2 * blk), NEG, sc)
            return sc

        pairs = [(e, hh) for e in range(nb + 2) for hh in range(2)]
        qs = [_stack_heads(slab(*((q0p, q0m, q0n), (q1p, q1m, q1n))[hh], e)) for e, hh in pairs]
        dos = [_stack_heads(slab(dop, dom, don, e)[:, hh * 256:(hh + 1) * 256]) for e, hh in pairs]
        lse_c = [_head_cols(slab(lp, lm, ln, e), hh, blk) for e, hh in pairs]
        dl_c = [_head_cols(slab(dp_, dm_, dn_, e), hh, blk) for e, hh in pairs]
        kw = [kx[hh, keys(e)[0] * blk:keys(e)[1] * blk, :] for e, hh in pairs]
        vw = [vx[hh, keys(e)[0] * blk:keys(e)[1] * blk, :] for e, hh in pairs]
        sc = [lax.dot_general(q_, k_, nt_dims, preferred_element_type=f32) for q_, k_ in zip(qs, kw)]
        dp = [lax.dot_general(d_, v_, nt_dims, preferred_element_type=f32) for d_, v_ in zip(dos, vw)]
        sc = [edge(s_ * scale + bias_ref[0, hh, :, keys(e)[2]], e) for s_, (e, hh) in zip(sc, pairs)]
        ps = [jnp.exp(s_ - l_) for s_, l_ in zip(sc, lse_c)]
        ds = [(p_ * (d_ - c_) * scale).astype(bf16) for p_, d_, c_ in zip(ps, dp, dl_c)]
        pb = [p_.astype(bf16) for p_ in ps]
        dks = [lax.dot_general(s_, q_, tn_dims, preferred_element_type=f32) for s_, q_ in zip(ds, qs)]
        dvs = [lax.dot_general(p_, d_, tn_dims, preferred_element_type=f32) for p_, d_ in zip(pb, dos)]
        dqs = [jnp.dot(s_, k_, preferred_element_type=f32) if 1 <= e <= nb else None for s_, k_, (e, hh) in zip(ds, kw, pairs)]
        for dk_, dv_, dq_, (e, hh) in zip(dks, dvs, dqs, pairs):
            k0, k1, _ = keys(e)
            dkx[hh, k0 * blk:k1 * blk, :] += dk_
            dvx[hh, k0 * blk:k1 * blk, :] += dv_
            if dq_ is not None:
                dq_ref[(e - 1) * blk:e * blk, hh * 256:(hh + 1) * 256] = _unstack_heads(dq_, blk).astype(bf16)
        for hh in range(2):
            dk_ref[:, hh * DH:(hh + 1) * DH] = dkx[hh, blk:blk + t, :].astype(bf16)
            dv_ref[:, hh * DH:(hh + 1) * DH] = dvx[hh, blk:blk + t, :].astype(bf16)

    in_specs = cfg.halo(256, cfg.qcol(0)) + cfg.halo(256, cfg.qcol(1))
    in_specs += cfg.halo(128, cfg.kcol) + cfg.halo(128, cfg.vcol)
    in_specs += cfg.halo(512, cfg.pcol) + cfg.halo(128, cfg.pcol) + cfg.halo(128, cfg.pcol)
    in_specs += [pl.BlockSpec((1, 2, GQ * blk, 3 * blk), lambda r, hp, i: (hp, 0, 0, 0))]
    dq, dk, dv = _pcall(
        body, name=name, grid=cfg.grid(), in_specs=in_specs,
        out_specs=[pl.BlockSpec((t, 512), lambda r, hp, i: (i, cfg.pcol(r, hp))),
                   pl.BlockSpec((t, 128), lambda r, hp, i: (i, cfg.pcol(r, hp))),
                   pl.BlockSpec((t, 128), lambda r, hp, i: (i, cfg.pcol(r, hp)))],
        out_shape=[jax.ShapeDtypeStruct((cfg.l, dil * npairs * 512), bf16),
                   jax.ShapeDtypeStruct((cfg.l, dil * npairs * 128), bf16),
                   jax.ShapeDtypeStruct((cfg.l, dil * npairs * 128), bf16)],
        scratch_shapes=[pltpu.VMEM((2, t + 2 * blk, DH), bf16), pltpu.VMEM((2, t + 2 * blk, DH), bf16),
                        pltpu.VMEM((2, t + 2 * blk, DH), f32), pltpu.VMEM((2, t + 2 * blk, DH), f32)],
        compiler_params=_params("arbitrary", "arbitrary", "arbitrary"),
    )(*([view] * 12), dov, dov, dov, lsev, lsev, lsev, dlv, dlv, dlv, jnp.asarray(cfg.bias))
    return dq.reshape(s, npairs * 512), dk.reshape(s, npairs * 128), dv.reshape(s, npairs * 128)


def _head_indicator(nheads):
    e = np.zeros((nheads * DH, (nheads // 8) * 128), np.float32)
    for c in range(nheads * DH):
        h = c // DH
        e[c, (h // 8) * 128 + h % 8] = 1.0
    return e


def _dot_split(x, e):
    hi = x.astype(bf16)
    lo = (x - hi.astype(f32)).astype(bf16)
    return jnp.dot(hi, e, preferred_element_type=f32) + jnp.dot(lo, e, preferred_element_type=f32)


def attn_delta(do, o, lse, sinkrow, *, name):
    s, co = do.shape
    w = lse.shape[1]
    tm = _row_tile(s)
    ind = jnp.asarray(_head_indicator(co // DH), dtype=bf16)

    def body(do_ref, o_ref, lse_ref, sink_ref, e_ref, dl_ref, ds_ref):
        dl = _dot_split(do_ref[...].astype(f32) * o_ref[...].astype(f32), e_ref[...])
        dl_ref[...] = dl
        part = -jnp.sum(jnp.exp(sink_ref[...] - lse_ref[...]) * dl, axis=0, keepdims=True)
        part = jnp.concatenate([part, jnp.zeros((7, w), f32)], axis=0)

        @pl.when(pl.program_id(0) == 0)
        def _():
            ds_ref[...] = part

        @pl.when(pl.program_id(0) != 0)
        def _():
            ds_ref[...] += part

    return _pcall(
        body, name=name, grid=(s // tm,),
        in_specs=[pl.BlockSpec((tm, co), lambda i: (i, 0)), pl.BlockSpec((tm, co), lambda i: (i, 0)),
                  pl.BlockSpec((tm, w), lambda i: (i, 0)), pl.BlockSpec((1, w), lambda i: (0, 0)),
                  pl.BlockSpec((co, w), lambda i: (0, 0))],
        out_specs=[pl.BlockSpec((tm, w), lambda i: (i, 0)), pl.BlockSpec((8, w), lambda i: (0, 0))],
        out_shape=[jax.ShapeDtypeStruct((s, w), f32), jax.ShapeDtypeStruct((8, w), f32)],
        compiler_params=_params("arbitrary"),
    )(do, o, lse, sinkrow, ind)


def attn_merge(os_, lses, *, name):
    s = os_[0].shape[0]
    tm = _row_tile(s)
    ind_t = jnp.asarray(_head_indicator(8).T, dtype=bf16)

    def body(o0, o1, o2, l0, l1, l2, e_ref, o_ref, lse_ref):
        ls = [l0[...], l1[...], l2[...]]
        m = jnp.maximum(jnp.maximum(ls[0], ls[1]), ls[2])
        tot = m + jnp.log(jnp.exp(ls[0] - m) + jnp.exp(ls[1] - m) + jnp.exp(ls[2] - m))
        lse_ref[...] = tot
        acc = jnp.zeros((tm, B_OUT), f32)
        for og, lg in zip((o0, o1, o2), ls):
            acc = acc + _dot_split(jnp.exp(lg - tot), e_ref[...]) * og[...]
        o_ref[...] = acc.astype(bf16)

    big = pl.BlockSpec((tm, B_OUT), lambda i: (i, 0))
    sm = pl.BlockSpec((tm, 128), lambda i: (i, 0))
    return _pcall(
        body, name=name, grid=(s // tm,), in_specs=[big, big, big, sm, sm, sm, pl.BlockSpec((128, B_OUT), lambda i: (0, 0))],
        out_specs=[big, sm], out_shape=[jax.ShapeDtypeStruct((s, B_OUT), bf16), jax.ShapeDtypeStruct((s, 128), f32)],
        compiler_params=_params("parallel"),
    )(*os_, *lses, ind_t)


def ada_mod(c_all, w, b, *, name):
    n = w.shape[2]

    def body(c_ref, w_ref, b_ref, o_ref):
        cv = c_ref[...]
        cond = cv * jax.nn.sigmoid(cv)
        o_ref[0] = jnp.dot(cond, w_ref[0], preferred_element_type=f32, precision=lax.Precision.HIGHEST) + b_ref[0]

    return _pcall(
        body, name=name, grid=(DEPTH,),
        in_specs=[pl.BlockSpec((N_DEV, D), lambda i: (0, 0)), pl.BlockSpec((1, D, n), lambda i: (i, 0, 0)),
                  pl.BlockSpec((1, 1, n), lambda i: (i, 0, 0))],
        out_specs=pl.BlockSpec((1, N_DEV, n), lambda i: (i, 0, 0)),
        out_shape=jax.ShapeDtypeStruct((DEPTH, N_DEV, n), f32), compiler_params=_params("arbitrary"),
    )(c_all, w, b)


def ada_grad(c_t, dm, *, name):
    n = dm.shape[2]

    def body(c_ref, dm_ref, o_ref):
        cv = c_ref[...]
        cond = cv * jax.nn.sigmoid(cv)
        acc = cond[:, 0:1] * dm_ref[0, 0:1, :]
        for b in range(1, N_DEV):
            acc = acc + cond[:, b:b + 1] * dm_ref[0, b:b + 1, :]
        o_ref[0] = acc

    return _pcall(
        body, name=name, grid=(DEPTH,),
        in_specs=[pl.BlockSpec((D, N_DEV), lambda i: (0, 0)), pl.BlockSpec((1, N_DEV, n), lambda i: (i, 0, 0))],
        out_specs=pl.BlockSpec((1, D, n), lambda i: (i, 0, 0)),
        out_shape=jax.ShapeDtypeStruct((DEPTH, D, n), f32), compiler_params=_params("arbitrary"),
    )(c_t, dm)


def _adam_math(w, g, m, v):
    m2 = B1 * m + (1.0 - B1) * g
    v2 = B2 * v + (1.0 - B2) * (g * g)
    mh = m2 / (1.0 - B1 ** STEP)
    vh = v2 / (1.0 - B2 ** STEP)
    return -LR * (mh / (jnp.sqrt(vh) + ADAM_EPS) + WD * w), m2, v2


def adamw(w, m, v, g, *, name):
    r, c = w.shape
    tr = 256 if r % 256 == 0 else r

    def body(w_ref, m_ref, v_ref, g_ref, d_ref, m2_ref, v2_ref):
        d_ref[...], m2_ref[...], v2_ref[...] = _adam_math(w_ref[...], g_ref[...], m_ref[...], v_ref[...])

    spec = pl.BlockSpec((tr, c), lambda i: (i, 0))
    return _pcall(
        body, name=name, grid=(r // tr,), in_specs=[spec] * 4, out_specs=[spec] * 3,
        out_shape=[jax.ShapeDtypeStruct((r, c), f32)] * 3, compiler_params=_params("parallel"),
    )(w, m, v, g)


def adamw_parts(w, m, v, own, sib, *, name):
    r, c = w.shape
    tr = 256 if r % 256 == 0 else r

    def body(w_ref, m_ref, v_ref, own_ref, sib_ref, g_ref, d_ref, m2_ref, v2_ref):
        def total(ref):
            return ((ref[0].astype(f32) + ref[1].astype(f32)) + ref[2].astype(f32)) + ref[3].astype(f32)

        g = total(own_ref) + total(sib_ref)
        g_ref[...] = g
        d_ref[...], m2_ref[...], v2_ref[...] = _adam_math(w_ref[...], g, m_ref[...], v_ref[...])

    spec = pl.BlockSpec((tr, c), lambda i: (i, 0))
    pspec = pl.BlockSpec((4, tr, c), lambda i: (0, i, 0))
    return _pcall(
        body, name=name, grid=(r // tr,), in_specs=[spec] * 3 + [pspec] * 2, out_specs=[spec] * 4,
        out_shape=[jax.ShapeDtypeStruct((r, c), f32)] * 4, compiler_params=_params("parallel"),
    )(w, m, v, own, sib)


def sum_devices(g, *, name):
    _, r, c = g.shape

    def body(g_ref, o_ref):
        acc = g_ref[0]
        for k in range(1, N_DEV):
            acc = acc + g_ref[k]
        o_ref[...] = acc

    return _pcall(body, name=name, out_shape=jax.ShapeDtypeStruct((r, c), f32))(g)


def _place():
    x, y, c = lax.axis_index("x"), lax.axis_index("y"), lax.axis_index("c")
    chips = [(1 - x, y), (x, 1 - y), (1 - x, 1 - y)]
    return x, y, c, chips


def allgather8(v, *, name):
    r, c_ = v.shape

    def body(v_ref, o_ref, send_sems, recv_sems, local_sem):
        x, y, c, _ = _place()
        me = 4 * x + 2 * y + c
        mine = pltpu.make_async_copy(v_ref, o_ref.at[me], local_sem)
        mine.start()
        flips = [(fx, fy, fc) for fx in (0, 1) for fy in (0, 1) for fc in (0, 1)][1:]

        def peer(f):
            return (x ^ f[0], y ^ f[1], c ^ f[2])

        def copy(k, slot, to):
            return pltpu.make_async_remote_copy(
                src_ref=v_ref, dst_ref=o_ref.at[slot], send_sem=send_sems.at[k], recv_sem=recv_sems.at[k],
                device_id=to, device_id_type=MESH)

        sends = [copy(k, me, peer(f)) for k, f in enumerate(flips)]
        for cp in sends:
            cp.start()
        for k, f in enumerate(flips):
            px, py, pc = peer(f)
            copy(k, 4 * px + 2 * py + pc, (x, y, c)).wait_recv()
        for cp in sends:
            cp.wait_send()
        mine.wait()

    return _pcall(
        body, name=name, in_specs=[ANY], out_specs=ANY, out_shape=jax.ShapeDtypeStruct((N_DEV, r, c_), v.dtype),
        scratch_shapes=[pltpu.SemaphoreType.DMA((7,)), pltpu.SemaphoreType.DMA((7,)), pltpu.SemaphoreType.DMA],
    )(v)


def gather_weights(shards, *, name):
    n = len(shards)

    def body(*refs):
        src, out = refs[:n], refs[n:2 * n]
        send_a, recv_a, send_f, recv_f, local_sems = refs[2 * n:]
        x, y, c, chips = _place()
        sib = (x, y, 1 - c)
        me = 2 * x + y
        locals_ = [pltpu.make_async_copy(src[a], out[a].at[me], local_sems.at[a]) for a in range(n)]
        for cp in locals_:
            cp.start()

        def half(a, which):
            rh = src[a].shape[0] // 2
            return pl.ds(which * rh, rh)

        def first(a, k, chip_from, to):
            slot = 2 * chip_from[0] + chip_from[1]
            s_ref = src[a].at[half(a, c)]
            return pltpu.make_async_remote_copy(
                src_ref=s_ref, dst_ref=out[a].at[slot, half(a, c)], send_sem=send_a.at[3 * a + k],
                recv_sem=recv_a.at[3 * a + k], device_id=to, device_id_type=MESH)

        def passed(a, k, chip_from, which, to):
            slot = 2 * chip_from[0] + chip_from[1]
            ref = out[a].at[slot, half(a, which)]
            return pltpu.make_async_remote_copy(
                src_ref=ref, dst_ref=ref, send_sem=send_f.at[3 * a + k], recv_sem=recv_f.at[3 * a + k],
                device_id=to, device_id_type=MESH)

        sends = [first(a, k, (x, y), (*chip, c)) for a in range(n) for k, chip in enumerate(chips)]
        for cp in sends:
            cp.start()
        fwd = []
        for a in range(n):
            for k, chip in enumerate(chips):
                first(a, k, chip, (x, y, c)).wait_recv()
                cp = passed(a, k, chip, c, sib)
                cp.start()
                fwd.append(cp)
        for a in range(n):
            for k, chip in enumerate(chips):
                passed(a, k, chip, 1 - c, (x, y, c)).wait_recv()
        for cp in sends + fwd:
            cp.wait_send()
        for cp in locals_:
            cp.wait()

    return _pcall(
        body, name=name, in_specs=[ANY] * n, out_specs=[ANY] * n,
        out_shape=[jax.ShapeDtypeStruct((4,) + tuple(sh.shape), sh.dtype) for sh in shards],
        scratch_shapes=[pltpu.SemaphoreType.DMA((3 * n,)) for _ in range(4)] + [pltpu.SemaphoreType.DMA((n,))],
    )(*shards)


def exchange_grads(parts, *, name):
    n = len(parts)
    flat = [p for group in parts for p in group]
    first_of = np.cumsum([0] + [len(g) for g in parts])

    def body(*refs):
        nf = len(flat)
        src_flat, own, sibo = refs[:nf], refs[nf:nf + n], refs[nf + n:nf + 2 * n]
        send_sems, recv_sems, local_sems = refs[nf + 2 * n:]
        src = [src_flat[first_of[a]:first_of[a + 1]] for a in range(n)]
        x, y, c, chips = _place()
        sib = (x, y, 1 - c)
        me = 2 * x + y

        def slot(chip):
            return 2 * chip[0] + chip[1]

        def rows(a, l):
            r = src[a][0].shape[1]
            return pl.ds(l * r, r)

        def copy(a, k, s_ref, d_ref, to):
            return pltpu.make_async_remote_copy(
                src_ref=s_ref, dst_ref=d_ref, send_sem=send_sems.at[7 * a + k], recv_sem=recv_sems.at[7 * a + k],
                device_id=to, device_id_type=MESH)

        def whole(a, k, ref_):
            return copy(a, k, ref_, ref_, (x, y, c))

        for a in range(n):
            for l in range(len(src[a])):
                pltpu.make_async_copy(src[a][l].at[me], own[a].at[me, rows(a, l)], local_sems.at[a]).start()
        for a in range(n):
            for l in range(len(src[a])):
                copy(a, 0, src[a][l].at[me], sibo[a].at[me, rows(a, l)], sib).start()
                for k, chip in enumerate(chips):
                    copy(a, 1 + k, src[a][l].at[slot(chip)], own[a].at[me, rows(a, l)], (*chip, c)).start()
        for a in range(n):
            for k, chip in enumerate(chips):
                whole(a, 1 + k, own[a].at[slot(chip)]).wait_recv()
                copy(a, 4 + k, own[a].at[slot(chip)], sibo[a].at[slot(chip)], sib).start()
        for a in range(n):
            whole(a, 0, sibo[a].at[me]).wait_recv()
            for k, chip in enumerate(chips):
                whole(a, 4 + k, sibo[a].at[slot(chip)]).wait_recv()
        for a in range(n):
            for k in range(7):
                whole(a, k, own[a].at[me]).wait_send()
            pltpu.make_async_copy(sibo[a].at[me], own[a].at[me], local_sems.at[a]).wait()

    shapes = [jax.ShapeDtypeStruct((4, len(g) * g[0].shape[1], g[0].shape[2]), g[0].dtype) for g in parts]
    res = _pcall(
        body, name=name, in_specs=[ANY] * len(flat), out_specs=[ANY] * (2 * n), out_shape=shapes + shapes,
        scratch_shapes=[pltpu.SemaphoreType.DMA((7 * n,)), pltpu.SemaphoreType.DMA((7 * n,)), pltpu.SemaphoreType.DMA((n,))],
    )(*flat)
    return res[:n], res[n:]


def _natural(g, how):
    if how == "col":
        return jnp.moveaxis(g, 0, 1).reshape(g.shape[1], 4 * g.shape[2])
    return g.reshape(4 * g.shape[1], g.shape[2])


def _chunks(gw, how):
    k, n = gw.shape
    if how == "col":
        return jnp.moveaxis(gw.reshape(k, 4, n // 4), 1, 0).astype(bf16)
    return gw.reshape(4, k // 4, n).astype(bf16)


def kernel(x, c, ada_w, ada_b, norm_mix, norm_ffn, ffn_w_in, ffn_w_out, a_w_in, a_w_out, a_sink, b_w_in, b_w_out, final_norm, loss_target, m_ada_w, m_ada_b, m_norm_mix, m_norm_ffn, m_ffn_w_in, m_ffn_w_out, m_a_w_in, m_a_w_out, m_a_sink, m_b_w_in, m_b_w_out, m_final_norm, v_ada_w, v_ada_b, v_norm_mix, v_norm_ffn, v_ffn_w_in, v_ffn_w_out, v_a_w_in, v_a_w_out, v_a_sink, v_b_w_in, v_b_w_out, v_final_norm):
    s = x.shape[1]
    xi, yi, ci = lax.axis_index("x"), lax.axis_index("y"), lax.axis_index("c")
    chip = 2 * xi + yi
    dev = 2 * chip + ci
    x0 = x[0]
    tgt = loss_target[0]

    big = {"ffn_w_in": (ffn_w_in, "col"), "ffn_w_out": (ffn_w_out, "row"), "a_w_in": (a_w_in, "col"),
           "a_w_out": (a_w_out, "row"), "b_w_in": (b_w_in, "col"), "b_w_out": (b_w_out, "col")}
    names = list(big)
    shards = [big[k][0][l].astype(bf16) for k in names for l in range(big[k][0].shape[0])]
    gathered = iter(gather_weights(shards, name="gather_weights"))
    wc = {k: [next(gathered) for _ in range(big[k][0].shape[0])] for k in names}
    wb = {k: [_natural(g, big[k][1]) for g in wc[k]] for k in names if k != "ffn_w_in"}

    c_all = allgather8(jnp.broadcast_to(c, (8, D)), name="gather_c")[:, 0, :]
    nsh = ada_w.shape[2]
    ada_b_sh = lax.dynamic_slice_in_dim(ada_b, chip * nsh, nsh, axis=1)[:, None, :]
    mod_part = ada_mod(c_all, ada_w, ada_b_sh, name="ada_mod")
    mod_all = allgather8(mod_part.reshape(DEPTH * N_DEV, nsh), name="gather_mod")
    mod_all = mod_all.reshape(4, 2, DEPTH, N_DEV, nsh)[:, 0]
    mod = lax.dynamic_index_in_dim(mod_all, dev, axis=2, keepdims=False)
    mod = jnp.moveaxis(mod, 0, 1).reshape(DEPTH, 6, 1, D)

    cfg_a = _Attn(s, mixer="a")
    cfg_b = [_Attn(s, mixer="b", group=g) for g in range(3)]
    no_sink = jnp.full((1, 2, GQ * 64, 1), NEG, f32)

    saved = []
    xc = x0
    for i in range(DEPTH):
        j = i // 2
        sh1, sc1, g1, sh2, sc2, g2 = (mod[i, q] for q in range(6))
        nmix, nffn = norm_mix[i][None, :], norm_ffn[i][None, :]
        if i % 2 == 0:
            h, qkv = mm_norm(xc, nmix, sc1, sh1, wb["a_w_in"][j], name="a_qkv")
            sinkcol = jnp.repeat(a_sink[j].reshape(2, 2, GQ), 128, axis=2)[..., None]
            o, lse = attn_fwd(qkv, sinkcol, cfg_a, out_dtype=bf16, name="a_attn_fwd")
            w_o = wb["a_w_out"][j]
        else:
            h, qkv = mm_norm(xc, nmix, sc1, sh1, wb["b_w_in"][j], name="b_qkv")
            outs = [attn_fwd(qkv, no_sink, cfg_b[g], out_dtype=f32, name=f"b_attn_fwd{g}") for g in range(3)]
            o, lse = attn_merge([t[0] for t in outs], [t[1] for t in outs], name="b_merge")
            w_o = wb["b_w_out"][j]
        x1 = mm_resid(o, w_o, xc, g1, name="a_out" if i % 2 == 0 else "b_out")
        h2, gu, act, x2 = ffn_fwd(x1, nffn, sc2, sh2, g2, wc["ffn_w_in"][i], wb["ffn_w_out"][i], name="ffn_fwd")
        saved.append((xc, h, qkv, o, lse, x1, h2, gu, act))
        xc = x2

    dx, st_final = loss_head(xc, final_norm[None, :], tgt, name="loss_head")

    zero_row = jnp.zeros((1, D), f32)
    dmod_rows = [None] * DEPTH
    d_nmix, d_nffn = [None] * DEPTH, [None] * DEPTH
    d_sink = [None] * 2
    gw = {k: [None] * big[k][0].shape[0] for k in names}
    for i in reversed(range(DEPTH)):
        j = i // 2
        xin, h, qkv, o, lse, x1, h2, gu, act = saved[i]
        sh1, sc1, g1, sh2, sc2, g2 = (mod[i, q] for q in range(6))
        nmix, nffn = norm_mix[i][None, :], norm_ffn[i][None, :]
        w_fo = wb["ffn_w_out"][i]
        dgu, dx1, st2 = ffn_bwd_rows(dx, x1, gu, g2, nffn, sc2, wc["ffn_w_in"][i], w_fo, name="ffn_bwd_rows")
        gwo, dg2 = ffn_dw_out(act, dx, g2, w_fo, name="ffn_dw_out")
        gw["ffn_w_out"][i], dg2 = gwo.reshape(4, F // 4, D), dg2[0:1]
        gw["ffn_w_in"][i] = ffn_dw_in(h2, dgu, name="ffn_dw_in")
        if i % 2 == 0:
            w_o, w_i, kin, kout = wb["a_w_out"][j], wb["a_w_in"][j], "a_w_in", "a_w_out"
        else:
            w_o, w_i, kin, kout = wb["b_w_out"][j], wb["b_w_in"][j], "b_w_in", "b_w_out"
        do = mm_nt_scaled(dx1, g1, w_o, name="a_do" if i % 2 == 0 else "b_do")
        gw[kout][j], dg1 = mm_tn(o, dx1, (g1, w_o), name="a_dw_out" if i % 2 == 0 else "b_dw_out")
        if i % 2 == 0:
            sinkrow = jnp.pad(a_sink[j].reshape(2, 8), ((0, 0), (0, 120))).reshape(1, 256)
            delta, dsk = attn_delta(do, o, lse, sinkrow, name="a_delta")
            d_sink[j] = dsk[0].reshape(2, 128)[:, :8].reshape(16)
            dq, dk, dv = attn_bwd(qkv, do, lse, delta, cfg_a, name="a_attn_bwd")
            dqkv = jnp.concatenate([dq, dk, dv], axis=1)
        else:
            delta, _ = attn_delta(do, o, lse, jnp.zeros((1, 128), f32), name="b_delta")
            gr = [attn_bwd(qkv, do, lse, delta, cfg_b[g], name=f"b_attn_bwd{g}") for g in range(3)]
            dqkv = jnp.concatenate([gr[g][q] for q in range(3) for g in range(3)], axis=1)
        gw[kin][j] = mm_tn(h, dqkv, name="a_dw_in" if i % 2 == 0 else "b_dw_in")
        dx, st1 = mm_nt_norm_bwd(dqkv, w_i, xin, dx1, nmix, sc1, name="a_dh" if i % 2 == 0 else "b_dh")
        dmod_rows[i] = jnp.concatenate([st1[2:3], st1[1:2], dg1, st2[2:3], st2[1:2], dg2], axis=0)
        d_nmix[i], d_nffn[i] = st1[0:1], st2[0:1]

    sink_row = jnp.pad(jnp.concatenate(d_sink), (0, D - 32))[None, :]
    stats = jnp.concatenate(dmod_rows + d_nmix + d_nffn + [sink_row, st_final[0:1], st_final[1:2]]
                            + [zero_row] * (STAT_ROWS - 35), axis=0)
    stats_all = allgather8(stats, name="gather_stats")
    tot = sum_devices(stats_all, name="sum_stats")
    loss = 0.5 * jnp.sum(tot[34]) / float(D)

    def pack(ab, nm, nf, sk, fnm, fill):
        return jnp.concatenate([ab.reshape(24, D), nm, nf, jnp.pad(sk.reshape(1, 32), ((0, 0), (0, D - 32)), constant_values=fill),
                                fnm[None, :], jnp.full((STAT_ROWS - 34, D), fill, f32)], axis=0)

    sd, sm, sv = adamw(pack(ada_b, norm_mix, norm_ffn, a_sink, final_norm, 0.0),
                       pack(m_ada_b, m_norm_mix, m_norm_ffn, m_a_sink, m_final_norm, 0.0),
                       pack(v_ada_b, v_norm_mix, v_norm_ffn, v_a_sink, v_final_norm, 1.0), tot, name="adamw_small")

    def unpack(p):
        return p[0:24].reshape(DEPTH, 6 * D), p[24:28], p[28:32], p[32, :32].reshape(2, 16), p[33]

    small = {"grad": unpack(tot), "delta": unpack(sd), "m": unpack(sm), "v": unpack(sv)}

    dmod_all = stats_all[:, 0:24, :].reshape(N_DEV, DEPTH, 6 * D)
    dm_sh = jnp.moveaxis(lax.dynamic_slice_in_dim(dmod_all, chip * nsh, nsh, axis=2), 0, 1)
    g_ada = ada_grad(c_all.T, dm_sh, name="ada_grad")
    r_ada = (DEPTH * D, nsh)
    ada_res = adamw(ada_w.reshape(r_ada), m_ada_w.reshape(r_ada), v_ada_w.reshape(r_ada), g_ada.reshape(r_ada), name="adamw_ada")
    ada_out = [g_ada] + [t.reshape(ada_w.shape) for t in ada_res]

    parts = [gw[k] if k.startswith("ffn") else [_chunks(gl, big[k][1]) for gl in gw[k]] for k in names]
    own, sibs = exchange_grads(parts, name="exchange_grads")
    big_out = {}
    mom = {"ffn_w_in": (m_ffn_w_in, v_ffn_w_in), "ffn_w_out": (m_ffn_w_out, v_ffn_w_out), "a_w_in": (m_a_w_in, v_a_w_in),
           "a_w_out": (m_a_w_out, v_a_w_out), "b_w_in": (m_b_w_in, v_b_w_in), "b_w_out": (m_b_w_out, v_b_w_out)}
    for k, o_, s_ in zip(names, own, sibs):
        w = big[k][0]
        r2 = (-1, w.shape[-1])
        res = adamw_parts(w.reshape(r2), mom[k][0].reshape(r2), mom[k][1].reshape(r2), o_, s_, name="adamw_" + k)
        big_out[k] = [t.reshape(w.shape) for t in res]

    def leaves(q):
        sm_ = small[("grad", "delta", "m", "v")[q]]
        return (ada_out[q], sm_[0], sm_[1], sm_[2], big_out["ffn_w_in"][q], big_out["ffn_w_out"][q], big_out["a_w_in"][q],
                big_out["a_w_out"][q], sm_[3], big_out["b_w_in"][q], big_out["b_w_out"][q], sm_[4])

    return (loss, dx[None], *leaves(0), *leaves(1), *leaves(2), *leaves(3))
```

```python
import functools
import math

import numpy as np
import jax
import jax.numpy as jnp
from jax import lax
from jax.experimental import pallas as pl
from jax.experimental.pallas import tpu as pltpu

f32 = jnp.float32
bf16 = jnp.bfloat16

D = 1024
DH = 64
GQ = 4
DEPTH = 4
F = 2816
A_QKV, A_OUT = 1536, 1024
B_QKV, B_OUT = 2304, 512
B_GROUPS = ((128, 1), (512, 4), (2048, 16))
RMS_EPS = 1e-6
NEG = -1e30
LR, B1, B2, ADAM_EPS, WD, STEP = 0.001, 0.9, 0.999, 1e-08, 0.01, 10
N_DEV = 8
STAT_ROWS = 40
MESH = pl.DeviceIdType.MESH
ANY = pl.BlockSpec(memory_space=pl.ANY)


def _pcall(body, **kw):
    return pl.pallas_call(body, **kw)


def _params(*sem):
    return pltpu.CompilerParams(dimension_semantics=sem, vmem_limit_bytes=56 * 1024 * 1024)


def _row_tile(s, want=1024):
    return want if s % want == 0 else s


def mm_norm(x, nw, sc, sh, w, *, name):
    s, d = x.shape
    n = w.shape[1]
    tm = _row_tile(s)
    tn = n // 3

    def body(x_ref, nw_ref, sc_ref, sh_ref, w_ref, h_ref, y_ref):
        @pl.when(pl.program_id(1) == 0)
        def _():
            xv = x_ref[...]
            r = lax.rsqrt(jnp.mean(xv * xv, axis=-1, keepdims=True) + RMS_EPS)
            h_ref[...] = ((xv * r * nw_ref[...]) * (1.0 + sc_ref[...]) + sh_ref[...]).astype(bf16)

        y_ref[...] = jnp.dot(h_ref[...], w_ref[...], preferred_element_type=f32).astype(bf16)

    vec = pl.BlockSpec((1, d), lambda i, j: (0, 0))
    return _pcall(
        body, name=name, grid=(s // tm, n // tn),
        in_specs=[pl.BlockSpec((tm, d), lambda i, j: (i, 0)), vec, vec, vec, pl.BlockSpec((d, tn), lambda i, j: (0, j))],
        out_specs=[pl.BlockSpec((tm, d), lambda i, j: (i, 0)), pl.BlockSpec((tm, tn), lambda i, j: (i, j))],
        out_shape=[jax.ShapeDtypeStruct((s, d), bf16), jax.ShapeDtypeStruct((s, n), bf16)],
        compiler_params=_params("parallel", "arbitrary"),
    )(x, nw, sc, sh, w)


def mm_resid(a, w, xres, g, *, name):
    s, k = a.shape
    n = w.shape[1]
    tm, tn = _row_tile(s), 512

    def body(a_ref, w_ref, x_ref, g_ref, o_ref):
        o_ref[...] = x_ref[...] + g_ref[...] * jnp.dot(a_ref[...], w_ref[...], preferred_element_type=f32)

    return _pcall(
        body, name=name, grid=(s // tm, n // tn),
        in_specs=[pl.BlockSpec((tm, k), lambda i, j: (i, 0)), pl.BlockSpec((k, tn), lambda i, j: (0, j)),
                  pl.BlockSpec((tm, tn), lambda i, j: (i, j)), pl.BlockSpec((1, tn), lambda i, j: (0, j))],
        out_specs=pl.BlockSpec((tm, tn), lambda i, j: (i, j)), out_shape=jax.ShapeDtypeStruct((s, n), f32),
        compiler_params=_params("parallel", "arbitrary"),
    )(a, w, xres, g)


def mm_nt_scaled(dx, g, w, *, name):
    s, d = dx.shape
    n = w.shape[0]
    tm, tn = _row_tile(s), 512

    def body(dx_ref, g_ref, w_ref, o_ref, a_ref):
        @pl.when(pl.program_id(1) == 0)
        def _():
            a_ref[...] = (dx_ref[...] * g_ref[...]).astype(bf16)

        o_ref[...] = lax.dot_general(a_ref[...], w_ref[...], (((1,), (1,)), ((), ())), preferred_element_type=f32).astype(bf16)

    return _pcall(
        body, name=name, grid=(s // tm, n // tn),
        in_specs=[pl.BlockSpec((tm, d), lambda i, j: (i, 0)), pl.BlockSpec((1, d), lambda i, j: (0, 0)),
                  pl.BlockSpec((tn, d), lambda i, j: (j, 0))],
        out_specs=pl.BlockSpec((tm, tn), lambda i, j: (i, j)), out_shape=jax.ShapeDtypeStruct((s, n), bf16),
        scratch_shapes=[pltpu.VMEM((tm, d), bf16)], compiler_params=_params("parallel", "arbitrary"),
    )(dx, g, w)


def mm_nt_norm_bwd(a, w, x, dres, nw, sc, *, name):
    s, k = a.shape
    d = w.shape[0]
    tm = _row_tile(s, 512)
    tk = 768 if k % 768 == 0 and k % 512 != 0 else 512
    nk = k // tk

    def body(a_ref, w_ref, x_ref, dr_ref, nw_ref, sc_ref, o_ref, st_ref, acc):
        kk = pl.program_id(1)
        part = lax.dot_general(a_ref[...], w_ref[...], (((1,), (1,)), ((), ())), preferred_element_type=f32)

        @pl.when(kk == 0)
        def _():
            acc[...] = part

        @pl.when(kk != 0)
        def _():
            acc[...] += part

        @pl.when(kk == nk - 1)
        def _():
            dh = acc[...]
            xv = x_ref[...]
            r = lax.rsqrt(jnp.mean(xv * xv, axis=-1, keepdims=True) + RMS_EPS)
            xh = xv * r
            dn = dh * (1.0 + sc_ref[...])
            dxh = dn * nw_ref[...]
            o_ref[...] = dr_ref[...] + r * (dxh - xh * jnp.mean(dxh * xh, axis=-1, keepdims=True))
            rows = jnp.concatenate([
                jnp.sum(dn * xh, axis=0, keepdims=True),
                jnp.sum(dh * (xh * nw_ref[...]), axis=0, keepdims=True),
                jnp.sum(dh, axis=0, keepdims=True),
                jnp.zeros((5, d), f32)], axis=0)

            @pl.when(pl.program_id(0) == 0)
            def _():
                st_ref[...] = rows

            @pl.when(pl.program_id(0) != 0)
            def _():
                st_ref[...] += rows

    big = pl.BlockSpec((tm, d), lambda i, kk: (i, 0))
    vec = pl.BlockSpec((1, d), lambda i, kk: (0, 0))
    return _pcall(
        body, name=name, grid=(s // tm, nk),
        in_specs=[pl.BlockSpec((tm, tk), lambda i, kk: (i, kk)), pl.BlockSpec((d, tk), lambda i, kk: (0, kk)), big, big, vec, vec],
        out_specs=[big, pl.BlockSpec((8, d), lambda i, kk: (0, 0))],
        out_shape=[jax.ShapeDtypeStruct((s, d), f32), jax.ShapeDtypeStruct((8, d), f32)],
        scratch_shapes=[pltpu.VMEM((tm, d), f32)], compiler_params=_params("arbitrary", "arbitrary"),
    )(a, w, x, dres, nw, sc)


def mm_tn(a, b, scale=None, *, name):
    s, ka = a.shape
    nb = b.shape[1]
    ts = _row_tile(s)
    tn = 768 if nb % 768 == 0 and nb % 512 != 0 else 512
    tka = 1408 if ka % 1408 == 0 else min(ka, 1024)
    ns = s // ts

    def body(a_ref, b_ref, *rest):
        o_ref = rest[2] if scale is not None else rest[0]
        si = pl.program_id(2)
        part = lax.dot_general(a_ref[...], b_ref[...].astype(bf16), (((0,), (0,)), ((), ())), preferred_element_type=f32)

        @pl.when(si == 0)
        def _():
            o_ref[...] = part

        @pl.when(si != 0)
        def _():
            o_ref[...] += part

        if scale is not None:
            g_ref, wb_ref, dg_ref = rest[0], rest[1], rest[3]

            @pl.when(si == ns - 1)
            def _():
                gm = o_ref[...]
                dgp = jnp.sum(wb_ref[...].astype(f32) * gm, axis=0, keepdims=True)

                @pl.when(pl.program_id(1) == 0)
                def _():
                    dg_ref[...] = dgp

                @pl.when(pl.program_id(1) != 0)
                def _():
                    dg_ref[...] += dgp

                o_ref[...] = gm * g_ref[...]

    in_specs = [pl.BlockSpec((ts, tka), lambda j, i, k: (k, i)), pl.BlockSpec((ts, tn), lambda j, i, k: (k, j))]
    args = [a, b]
    out_specs = [pl.BlockSpec((tka, tn), lambda j, i, k: (i, j))]
    out_shape = [jax.ShapeDtypeStruct((ka, nb), f32)]
    if scale is not None:
        in_specs += [pl.BlockSpec((1, tn), lambda j, i, k: (0, j)), pl.BlockSpec((tka, tn), lambda j, i, k: (i, j))]
        args += list(scale)
        out_specs.append(pl.BlockSpec((1, tn), lambda j, i, k: (0, j)))
        out_shape.append(jax.ShapeDtypeStruct((1, nb), f32))
    res = _pcall(
        body, name=name, grid=(nb // tn, ka // tka, ns), in_specs=in_specs, out_specs=out_specs, out_shape=out_shape,
        compiler_params=_params("arbitrary", "arbitrary", "arbitrary"),
    )(*args)
    return res if scale is not None else res[0]


def loss_head(x, fn, tgt, *, name):
    s, d = x.shape
    tm = _row_tile(s, 512)

    def body(x_ref, fn_ref, t_ref, dx_ref, st_ref):
        xv = x_ref[...]
        r = lax.rsqrt(jnp.mean(xv * xv, axis=-1, keepdims=True) + RMS_EPS)
        xh = xv * r
        err = xh * fn_ref[...] - t_ref[...]
        dy = err / float(d)
        dxh = dy * fn_ref[...]
        dx_ref[...] = r * (dxh - xh * jnp.mean(dxh * xh, axis=-1, keepdims=True))
        rows = jnp.concatenate([
            jnp.sum(dy * xh, axis=0, keepdims=True),
            jnp.sum(err * err, axis=0, keepdims=True),
            jnp.zeros((6, d), f32)], axis=0)

        @pl.when(pl.program_id(0) == 0)
        def _():
            st_ref[...] = rows

        @pl.when(pl.program_id(0) != 0)
        def _():
            st_ref[...] += rows

    big = pl.BlockSpec((tm, d), lambda i: (i, 0))
    return _pcall(
        body, name=name, grid=(s // tm,), in_specs=[big, pl.BlockSpec((1, d), lambda i: (0, 0)), big],
        out_specs=[big, pl.BlockSpec((8, d), lambda i: (0, 0))],
        out_shape=[jax.ShapeDtypeStruct((s, d), f32), jax.ShapeDtypeStruct((8, d), f32)],
        compiler_params=_params("arbitrary"),
    )(x, fn, tgt)


FC = 2 * F // 4
FFN_ROWS = 256


def _resident(pairs, sems):
    @pl.when(pl.program_id(0) == 0)
    def _():
        cps = [pltpu.make_async_copy(h, v, sems.at[i]) for i, (h, v) in enumerate(pairs)]
        for cp in cps:
            cp.start()
        for cp in cps:
            cp.wait()


def ffn_fwd(x, nw, sc, sh, g, w_in, w_out, *, name):
    s, d = x.shape
    tm = _row_tile(s, FFN_ROWS)

    def body(x_ref, nw_ref, sc_ref, sh_ref, g_ref, win_hbm, wout_hbm, h_ref, gu_ref, a_ref, o_ref, win_v, wout_v, sems):
        _resident([(win_hbm, win_v), (wout_hbm, wout_v)], sems)
        xv = x_ref[...]
        r = lax.rsqrt(jnp.mean(xv * xv, axis=-1, keepdims=True) + RMS_EPS)
        h = ((xv * r * nw_ref[...]) * (1.0 + sc_ref[...]) + sh_ref[...]).astype(bf16)
        h_ref[...] = h
        y = None
        for c in range(2):
            cs = slice(c * FC, (c + 1) * FC)
            gt = jnp.dot(h, win_v[c], preferred_element_type=f32)
            up = jnp.dot(h, win_v[c + 2], preferred_element_type=f32)
            gu_ref[0, :, cs] = gt.astype(bf16)
            gu_ref[1, :, cs] = up.astype(bf16)
            act = (gt * jax.nn.sigmoid(gt) * up).astype(bf16)
            a_ref[:, cs] = act
            part = jnp.dot(act, wout_v[cs, :], preferred_element_type=f32)
            y = part if y is None else y + part
        o_ref[...] = xv + g_ref[...] * y

    big = pl.BlockSpec((tm, d), lambda i: (i, 0))
    vec = pl.BlockSpec((1, d), lambda i: (0, 0))
    return _pcall(
        body, name=name, grid=(s // tm,), in_specs=[big, vec, vec, vec, vec, ANY, ANY],
        out_specs=[big, pl.BlockSpec((2, tm, F), lambda i: (0, i, 0)), pl.BlockSpec((tm, F), lambda i: (i, 0)), big],
        out_shape=[jax.ShapeDtypeStruct((s, d), bf16), jax.ShapeDtypeStruct((2, s, F), bf16),
                   jax.ShapeDtypeStruct((s, F), bf16), jax.ShapeDtypeStruct((s, d), f32)],
        scratch_shapes=[pltpu.VMEM((4, d, FC), bf16), pltpu.VMEM((F, d), bf16), pltpu.SemaphoreType.DMA((2,))],
        compiler_params=_params("arbitrary"),
    )(x, nw, sc, sh, g, w_in, w_out)


def ffn_bwd_rows(dx, x, gu, g, nw, sc, w_in, w_out, *, name):
    s, d = x.shape
    tm = _row_tile(s, FFN_ROWS)
    nt_dims = (((1,), (1,)), ((), ()))

    def body(dx_ref, x_ref, gu_ref, g_ref, nw_ref, sc_ref, win_hbm, wout_hbm, dgu_ref, o_ref, st_ref, win_v, wout_v, sems):
        _resident([(win_hbm, win_v), (wout_hbm, wout_v)], sems)
        dxv = dx_ref[...]
        a = (dxv * g_ref[...]).astype(bf16)
        dh = None
        for c in range(2):
            cs = slice(c * FC, (c + 1) * FC)
            da = lax.dot_general(a, wout_v[cs, :], nt_dims, preferred_element_type=f32)
            gt = gu_ref[0, :, cs].astype(f32)
            up = gu_ref[1, :, cs].astype(f32)
            sg = jax.nn.sigmoid(gt)
            dgate = (da * up * (sg * (1.0 + gt * (1.0 - sg)))).astype(bf16)
            dup = (da * (gt * sg)).astype(bf16)
            dgu_ref[0, :, cs] = dgate
            dgu_ref[1, :, cs] = dup
            part = (lax.dot_general(dgate, win_v[c], nt_dims, preferred_element_type=f32)
                    + lax.dot_general(dup, win_v[c + 2], nt_dims, preferred_element_type=f32))
            dh = part if dh is None else dh + part
        xv = x_ref[...]
        r = lax.rsqrt(jnp.mean(xv * xv, axis=-1, keepdims=True) + RMS_EPS)
        xh = xv * r
        dn = dh * (1.0 + sc_ref[...])
        dxh = dn * nw_ref[...]
        o_ref[...] = dxv + r * (dxh - xh * jnp.mean(dxh * xh, axis=-1, keepdims=True))
        rows = jnp.concatenate([
            jnp.sum(dn * xh, axis=0, keepdims=True),
            jnp.sum(dh * (xh * nw_ref[...]), axis=0, keepdims=True),
            jnp.sum(dh, axis=0, keepdims=True),
            jnp.zeros((5, d), f32)], axis=0)

        @pl.when(pl.program_id(0) == 0)
        def _():
            st_ref[...] = rows

        @pl.when(pl.program_id(0) != 0)
        def _():
            st_ref[...] += rows

    big = pl.BlockSpec((tm, d), lambda i: (i, 0))
    vec = pl.BlockSpec((1, d), lambda i: (0, 0))
    gus = pl.BlockSpec((2, tm, F), lambda i: (0, i, 0))
    return _pcall(
        body, name=name, grid=(s // tm,), in_specs=[big, big, gus, vec, vec, vec, ANY, ANY],
        out_specs=[gus, big, pl.BlockSpec((8, d), lambda i: (0, 0))],
        out_shape=[jax.ShapeDtypeStruct((2, s, F), bf16), jax.ShapeDtypeStruct((s, d), f32), jax.ShapeDtypeStruct((8, d), f32)],
        scratch_shapes=[pltpu.VMEM((4, d, FC), bf16), pltpu.VMEM((F, d), bf16), pltpu.SemaphoreType.DMA((2,))],
        compiler_params=_params("arbitrary"),
    )(dx, x, gu, g, nw, sc, w_in, w_out)


def ffn_dw_in(h, dgu, *, name):
    s, d = h.shape
    ts = _row_tile(s)
    ns = s // ts
    tn_dims = (((0,), (0,)), ((), ()))

    def body(h_ref, dgu_ref, o_ref, acc):
        k = pl.program_id(1)

        @pl.when(k == 0)
        def _():
            acc[...] = jnp.zeros_like(acc)

        hv = h_ref[...]
        for c in range(2):
            acc[c] += lax.dot_general(hv, dgu_ref[:, c * FC:(c + 1) * FC], tn_dims, preferred_element_type=f32)

        @pl.when(k == ns - 1)
        def _():
            o_ref[...] = acc[...].astype(bf16)

    return _pcall(
        body, name=name, grid=(2, ns),
        in_specs=[pl.BlockSpec((ts, d), lambda hf, k: (k, 0)), pl.BlockSpec((None, ts, F), lambda hf, k: (hf, k, 0))],
        out_specs=pl.BlockSpec((2, d, FC), lambda hf, k: (hf, 0, 0)),
        out_shape=jax.ShapeDtypeStruct((4, d, FC), bf16), scratch_shapes=[pltpu.VMEM((2, d, FC), f32)],
        compiler_params=_params("arbitrary", "arbitrary"),
    )(h, dgu)


def ffn_dw_out(a, dx, g, wb, *, name):
    s, fdim = a.shape
    d = dx.shape[1]
    ts = _row_tile(s)
    ns = s // ts
    tn = d // 2
    tn_dims = (((0,), (0,)), ((), ()))

    def body(a_ref, dx_ref, g_ref, wb_ref, o_ref, dg_ref, acc):
        k = pl.program_id(1)

        @pl.when(k == 0)
        def _():
            acc[...] = jnp.zeros_like(acc)

        acc[...] += lax.dot_general(a_ref[...], dx_ref[...].astype(bf16), tn_dims, preferred_element_type=f32)

        @pl.when(k == ns - 1)
        def _():
            gm = acc[...]
            dg_ref[...] = jnp.concatenate([jnp.sum(wb_ref[...].astype(f32) * gm, axis=0, keepdims=True),
                                           jnp.zeros((7, tn), f32)], axis=0)
            o_ref[...] = (gm * g_ref[...]).astype(bf16)

    return _pcall(
        body, name=name, grid=(2, ns),
        in_specs=[pl.BlockSpec((ts, fdim), lambda j, k: (k, 0)), pl.BlockSpec((ts, tn), lambda j, k: (k, j)),
                  pl.BlockSpec((1, tn), lambda j, k: (0, j)), pl.BlockSpec((fdim, tn), lambda j, k: (0, j))],
        out_specs=[pl.BlockSpec((fdim, tn), lambda j, k: (0, j)), pl.BlockSpec((8, tn), lambda j, k: (0, j))],
        out_shape=[jax.ShapeDtypeStruct((fdim, d), bf16), jax.ShapeDtypeStruct((8, d), f32)],
        scratch_shapes=[pltpu.VMEM((fdim, tn), f32)], compiler_params=_params("arbitrary", "arbitrary"),
    )(a, dx, g, wb)


def _alibi(n):
    return np.asarray(2.0 ** (-8.0 * np.arange(1, n + 1) / n), dtype=np.float32)


class _Attn:
    def __init__(self, s, *, mixer, group=0):
        if mixer == "a":
            self.blk, self.dil, self.npairs = 128, 1, 2
            self.qb0, self.kb0, self.vb0 = 0, 8, 10
            slopes = _alibi(16).reshape(2, 2, GQ)
        else:
            window, dil = B_GROUPS[group]
            self.blk, self.dil, self.npairs = window // (2 * dil), dil, 1
            self.qb0, self.kb0, self.vb0 = (0, 12, 15) if dil == 1 else (0, 4, 5)
            slopes = _alibi(24).reshape(3, 1, 2, GQ)[group]
        self.l = s // self.dil
        self.t = min(512, self.l)
        self.nt = self.l // self.t
        self.nb = self.t // self.blk
        blk = self.blk
        qi = np.arange(blk)[:, None]
        rel = np.arange(3 * blk)[None, :] - blk - qi
        dist = (self.dil * np.abs(rel)).astype(np.float32)
        bias = -slopes[:, :, :, None, None] * dist[None, None, None]
        bias = np.where(np.abs(rel) <= blk, bias, np.float32(NEG)).astype(np.float32)
        self.bias = bias.reshape(self.npairs, 2, GQ * blk, 3 * blk)

    def grid(self):
        return (self.dil, self.npairs, self.nt)

    def tile(self, width, col):
        return pl.BlockSpec((None, self.t, width), lambda r, hp, i: (r, i, col(hp)))

    def halo(self, width, col):
        t, blk, nbl = self.t, self.blk, self.l // self.blk
        per = t // blk
        return [
            pl.BlockSpec((None, blk, width), lambda r, hp, i: (r, jnp.maximum(i * per - 1, 0), col(hp))),
            self.tile(width, col),
            pl.BlockSpec((None, blk, width), lambda r, hp, i: (r, jnp.minimum((i + 1) * per, nbl - 1), col(hp))),
        ]

    def qcol(self, e):
        return lambda hp: self.qb0 + 2 * hp + e

    def kcol(self, hp):
        return self.kb0 + hp

    def vcol(self, hp):
        return self.vb0 + hp

    def pcol(self, hp):
        return hp


def _stack_heads(x):
    return jnp.concatenate([x[:, g * DH:(g + 1) * DH] for g in range(GQ)], axis=0)


def _unstack_heads(x, rows):
    return jnp.concatenate([x[g * rows:(g + 1) * rows] for g in range(GQ)], axis=1)


def _head_cols(tile, hh, rows):
    return jnp.concatenate([tile[:, hh * GQ + g:hh * GQ + g + 1] for g in range(GQ)], axis=0)


def attn_fwd(qkv, sinkcol, cfg, *, out_dtype, name):
    blk, t, nb, nt, dil = cfg.blk, cfg.t, cfg.nb, cfg.nt, cfg.dil
    scale = DH ** -0.5

    def body(q0, q1, kp, km, kn, vp, vm, vn, bias_ref, sink_ref, o_ref, lse_ref, kx, vx):
        ti = pl.program_id(2)
        first, last = ti == 0, ti == nt - 1
        for hh in range(2):
            sl = slice(hh * DH, (hh + 1) * DH)
            for dst, (p_, m_, n_) in ((kx, (kp, km, kn)), (vx, (vp, vm, vn))):
                dst[hh, 0:blk] = p_[:, sl]
                dst[hh, blk:blk + t] = m_[:, sl]
                dst[hh, blk + t:] = n_[:, sl]
        col = lax.broadcasted_iota(jnp.int32, (GQ * blk, 3 * blk), 1)
        lane = lax.broadcasted_iota(jnp.int32, (blk, 128), 1)
        pairs = [(b, hh) for b in range(nb) for hh in range(2)]
        qs = [_stack_heads((q0, q1)[hh][b * blk:(b + 1) * blk, :]) for b, hh in pairs]
        sc = [lax.dot_general(q_, kx[hh, b * blk:(b + 3) * blk, :], (((1,), (1,)), ((), ())), preferred_element_type=f32)
              for q_, (b, hh) in zip(qs, pairs)]
        sc = [s_ * scale + bias_ref[0, hh] for s_, (b, hh) in zip(sc, pairs)]
        sc = [jnp.where(jnp.logical_and(first, col < blk), NEG, s_) if b == 0 else s_ for s_, (b, hh) in zip(sc, pairs)]
        sc = [jnp.where(jnp.logical_and(last, col >= 2 * blk), NEG, s_) if b == nb - 1 else s_ for s_, (b, hh) in zip(sc, pairs)]
        ms = [jnp.maximum(jnp.max(s_, axis=-1, keepdims=True), sink_ref[0, hh]) for s_, (b, hh) in zip(sc, pairs)]
        ps = [jnp.exp(s_ - m_) for s_, m_ in zip(sc, ms)]
        ls = [jnp.sum(p_, axis=-1, keepdims=True) + jnp.exp(sink_ref[0, hh] - m_) for p_, m_, (b, hh) in zip(ps, ms, pairs)]
        os_ = [jnp.dot(p_.astype(bf16), vx[hh, b * blk:(b + 3) * blk, :], preferred_element_type=f32)
               for p_, (b, hh) in zip(ps, pairs)]
        os_ = [o_ / l_ for o_, l_ in zip(os_, ls)]
        lses = [m_ + jnp.log(l_) for m_, l_ in zip(ms, ls)]
        for o_, (b, hh) in zip(os_, pairs):
            o_ref[b * blk:(b + 1) * blk, hh * 256:(hh + 1) * 256] = _unstack_heads(o_, blk).astype(out_dtype)
        for b in range(nb):
            lse_tile = jnp.zeros((blk, 128), f32)
            for hh in range(2):
                lse = lses[2 * b + hh]
                for g in range(GQ):
                    lse_tile = jnp.where(lane == hh * GQ + g, lse[g * blk:(g + 1) * blk], lse_tile)
            lse_ref[b * blk:(b + 1) * blk, :] = lse_tile

    in_specs = [cfg.tile(256, cfg.qcol(e)) for e in range(2)]
    in_specs += cfg.halo(128, cfg.kcol) + cfg.halo(128, cfg.vcol)
    in_specs += [pl.BlockSpec((1, 2, GQ * blk, 3 * blk), lambda r, hp, i: (hp, 0, 0, 0)),
                 pl.BlockSpec((1, 2, GQ * blk, 1), lambda r, hp, i: (hp, 0, 0, 0))]
    return _pcall(
        body, name=name, grid=cfg.grid(), in_specs=in_specs,
        out_specs=[cfg.tile(512, cfg.pcol), cfg.tile(128, cfg.pcol)],
        out_shape=[jax.ShapeDtypeStruct((dil, cfg.l, cfg.npairs * 512), out_dtype),
                   jax.ShapeDtypeStruct((dil, cfg.l, cfg.npairs * 128), f32)],
        scratch_shapes=[pltpu.VMEM((2, t + 2 * blk, DH), bf16), pltpu.VMEM((2, t + 2 * blk, DH), bf16)],
        compiler_params=_params("arbitrary", "arbitrary", "arbitrary"),
    )(*([qkv] * 8), jnp.asarray(cfg.bias), sinkcol)


def attn_bwd(qkv, do, lse, delta, cfg, *, name):
    blk, t, nb, nt, dil, npairs = cfg.blk, cfg.t, cfg.nb, cfg.nt, cfg.dil, cfg.npairs
    scale = DH ** -0.5
    nt_dims = (((1,), (1,)), ((), ()))
    tn_dims = (((0,), (0,)), ((), ()))

    def body(q0p, q0m, q0n, q1p, q1m, q1n, kp, km, kn, vp, vm, vn, dop, dom, don, lp, lm, ln, dp_, dm_, dn_,
             bias_ref, dq_ref, dk_ref, dv_ref, kx, vx, dkx, dvx):
        ti = pl.program_id(2)
        first, last = ti == 0, ti == nt - 1
        for hh in range(2):
            sl = slice(hh * DH, (hh + 1) * DH)
            for dst, (p_, m_, n_) in ((kx, (kp, km, kn)), (vx, (vp, vm, vn))):
                dst[hh, 0:blk] = p_[:, sl]
                dst[hh, blk:blk + t] = m_[:, sl]
                dst[hh, blk + t:] = n_[:, sl]
        dkx[...] = jnp.zeros_like(dkx)
        dvx[...] = jnp.zeros_like(dvx)

        def slab(prev, main, nxt, e):
            if e == 0:
                return prev[...]
            if e == nb + 1:
                return nxt[...]
            return main[(e - 1) * blk:e * blk, :]

        col3 = lax.broadcasted_iota(jnp.int32, (GQ * blk, 3 * blk), 1)

        def keys(e):
            if e == 0:
                return 1, 2, slice(2 * blk, 3 * blk)
            if e == nb + 1:
                return nb, nb + 1, slice(0, blk)
            return e - 1, e + 2, slice(0, 3 * blk)

        def edge(sc, e):
            if e == 0:
                return jnp.where(first, NEG, sc)
            if e == nb + 1:
                return jnp.where(last, NEG, sc)
            if e == 1:
                sc = jnp.where(jnp.logical_and(first, col3 < blk), NEG, sc)
            if e == nb:
                sc = jnp.where(jnp.logical_and(last, col3 >= 2 * blk), NEG, sc)
            return sc

        pairs = [(e, hh) for e in range(nb + 2) for hh in range(2)]
        qs = [_stack_heads(slab(*((q0p, q0m, q0n), (q1p, q1m, q1n))[hh], e)) for e, hh in pairs]
        dos = [_stack_heads(slab(dop, dom, don, e)[:, hh * 256:(hh + 1) * 256]) for e, hh in pairs]
        lse_c = [_head_cols(slab(lp, lm, ln, e), hh, blk) for e, hh in pairs]
        dl_c = [_head_cols(slab(dp_, dm_, dn_, e), hh, blk) for e, hh in pairs]
        kw = [kx[hh, keys(e)[0] * blk:keys(e)[1] * blk, :] for e, hh in pairs]
        vw = [vx[hh, keys(e)[0] * blk:keys(e)[1] * blk, :] for e, hh in pairs]
        sc = [lax.dot_general(q_, k_, nt_dims, preferred_element_type=f32) for q_, k_ in zip(qs, kw)]
        dp = [lax.dot_general(d_, v_, nt_dims, preferred_element_type=f32) for d_, v_ in zip(dos, vw)]
        sc = [edge(s_ * scale + bias_ref[0, hh, :, keys(e)[2]], e) for s_, (e, hh) in zip(sc, pairs)]
        ps = [jnp.exp(s_ - l_) for s_, l_ in zip(sc, lse_c)]
        ds = [(p_ * (d_ - c_) * scale).astype(bf16) for p_, d_, c_ in zip(ps, dp, dl_c)]
        pb = [p_.astype(bf16) for p_ in ps]
        dks = [lax.dot_general(s_, q_, tn_dims, preferred_element_type=f32) for s_, q_ in zip(ds, qs)]
        dvs = [lax.dot_general(p_, d_, tn_dims, preferred_element_type=f32) for p_, d_ in zip(pb, dos)]
        dqs = [jnp.dot(s_, k_, preferred_element_type=f32) if 1 <= e <= nb else None for s_, k_, (e, hh) in zip(ds, kw, pairs)]
        for dk_, dv_, dq_, (e, hh) in zip(dks, dvs, dqs, pairs):
            k0, k1, _ = keys(e)
            dkx[hh, k0 * blk:k1 * blk, :] += dk_
            dvx[hh, k0 * blk:k1 * blk, :] += dv_
            if dq_ is not None:
                dq_ref[(e - 1) * blk:e * blk, hh * 256:(hh + 1) * 256] = _unstack_heads(dq_, blk).astype(bf16)
        for hh in range(2):
            dk_ref[:, hh * DH:(hh + 1) * DH] = dkx[hh, blk:blk + t, :].astype(bf16)
            dv_ref[:, hh * DH:(hh + 1) * DH] = dvx[hh, blk:blk + t, :].astype(bf16)

    in_specs = cfg.halo(256, cfg.qcol(0)) + cfg.halo(256, cfg.qcol(1))
    in_specs += cfg.halo(128, cfg.kcol) + cfg.halo(128, cfg.vcol)
    in_specs += cfg.halo(512, cfg.pcol) + cfg.halo(128, cfg.pcol) + cfg.halo(128, cfg.pcol)
    in_specs += [pl.BlockSpec((1, 2, GQ * blk, 3 * blk), lambda r, hp, i: (hp, 0, 0, 0))]
    return _pcall(
        body, name=name, grid=cfg.grid(), in_specs=in_specs,
        out_specs=[cfg.tile(512, cfg.pcol), cfg.tile(128, cfg.pcol), cfg.tile(128, cfg.pcol)],
        out_shape=[jax.ShapeDtypeStruct((dil, cfg.l, npairs * 512), bf16),
                   jax.ShapeDtypeStruct((dil, cfg.l, npairs * 128), bf16),
                   jax.ShapeDtypeStruct((dil, cfg.l, npairs * 128), bf16)],
        scratch_shapes=[pltpu.VMEM((2, t + 2 * blk, DH), bf16), pltpu.VMEM((2, t + 2 * blk, DH), bf16),
                        pltpu.VMEM((2, t + 2 * blk, DH), f32), pltpu.VMEM((2, t + 2 * blk, DH), f32)],
        compiler_params=_params("arbitrary", "arbitrary", "arbitrary"),
    )(*([qkv] * 12), do, do, do, lse, lse, lse, delta, delta, delta, jnp.asarray(cfg.bias))


def _head_indicator(nheads):
    e = np.zeros((nheads * DH, (nheads // 8) * 128), np.float32)
    for c in range(nheads * DH):
        h = c // DH
        e[c, (h // 8) * 128 + h % 8] = 1.0
    return e


def _dot_split(x, e):
    hi = x.astype(bf16)
    lo = (x - hi.astype(f32)).astype(bf16)
    return jnp.dot(hi, e, preferred_element_type=f32) + jnp.dot(lo, e, preferred_element_type=f32)


def attn_delta(do, o, lse, sinkrow, *, name):
    s, co = do.shape
    w = lse.shape[1]
    tm = _row_tile(s)
    ind = jnp.asarray(_head_indicator(co // DH), dtype=bf16)

    def body(do_ref, o_ref, lse_ref, sink_ref, e_ref, dl_ref, ds_ref):
        dl = _dot_split(do_ref[...].astype(f32) * o_ref[...].astype(f32), e_ref[...])
        dl_ref[...] = dl
        part = -jnp.sum(jnp.exp(sink_ref[...] - lse_ref[...]) * dl, axis=0, keepdims=True)
        part = jnp.concatenate([part, jnp.zeros((7, w), f32)], axis=0)

        @pl.when(pl.program_id(0) == 0)
        def _():
            ds_ref[...] = part

        @pl.when(pl.program_id(0) != 0)
        def _():
            ds_ref[...] += part

    return _pcall(
        body, name=name, grid=(s // tm,),
        in_specs=[pl.BlockSpec((tm, co), lambda i: (i, 0)), pl.BlockSpec((tm, co), lambda i: (i, 0)),
                  pl.BlockSpec((tm, w), lambda i: (i, 0)), pl.BlockSpec((1, w), lambda i: (0, 0)),
                  pl.BlockSpec((co, w), lambda i: (0, 0))],
        out_specs=[pl.BlockSpec((tm, w), lambda i: (i, 0)), pl.BlockSpec((8, w), lambda i: (0, 0))],
        out_shape=[jax.ShapeDtypeStruct((s, w), f32), jax.ShapeDtypeStruct((8, w), f32)],
        compiler_params=_params("arbitrary"),
    )(do, o, lse, sinkrow, ind)


def _spread(scr, x, d):
    tm, w = x.shape
    for j in range(w // 128):
        scr[j] = x[:, j * 128:(j + 1) * 128]
    return [jnp.concatenate([scr[j, pl.ds(r, tm // d, stride=d), :] for j in range(w // 128)], axis=1) for r in range(d)]


def _weave(scr, blocks, d):
    n, w = blocks[0].shape
    for r in range(d):
        for j in range(w // 128):
            scr[j, pl.ds(r, n, stride=d), :] = blocks[r][:, j * 128:(j + 1) * 128]
    return jnp.concatenate([scr[j] for j in range(w // 128)], axis=1)


def _res_spec(d, tm, w):
    return pl.BlockSpec((d, tm // d, w), lambda i: (0, i, 0))


DILATED = tuple(dil for _, dil in B_GROUPS[1:])


def b_to_strided(qkv, *, name):
    s = qkv.shape[0]
    tm = _row_tile(s)

    def body(x_ref, *rest):
        outs, scr = rest[:-1], rest[-1]
        for gi, (o_ref, d) in enumerate(zip(outs, DILATED), start=1):
            cols = jnp.concatenate([x_ref[:, gi * 512:(gi + 1) * 512], x_ref[:, 1536 + gi * 128:1536 + (gi + 1) * 128],
                                    x_ref[:, 1920 + gi * 128:1920 + (gi + 1) * 128]], axis=1).astype(f32)
            for r, blk_ in enumerate(_spread(scr, cols, d)):
                o_ref[r] = blk_.astype(bf16)

    return _pcall(
        body, name=name, grid=(s // tm,), in_specs=[pl.BlockSpec((tm, B_QKV), lambda i: (i, 0))],
        out_specs=[_res_spec(d, tm, 768) for d in DILATED],
        out_shape=[jax.ShapeDtypeStruct((d, s // d, 768), bf16) for d in DILATED],
        scratch_shapes=[pltpu.VMEM((6, tm, 128), f32)], compiler_params=_params("parallel"),
    )(qkv)


def b_bwd_to_strided(do, lse, delta, *, name):
    s = do.shape[0]
    tm = _row_tile(s)

    def body(do_ref, lse_ref, dl_ref, *rest):
        outs, scr = rest[:-1], rest[-1]
        allc = jnp.concatenate([do_ref[...].astype(f32), lse_ref[...], dl_ref[...]], axis=1)
        for gi, d in enumerate(DILATED):
            o_do, o_lse, o_dl = outs[3 * gi:3 * gi + 3]
            for r, blk_ in enumerate(_spread(scr, allc, d)):
                o_do[r] = blk_[:, :512].astype(bf16)
                o_lse[r] = blk_[:, 512:640]
                o_dl[r] = blk_[:, 640:768]

    out_specs, out_shape = [], []
    for d in DILATED:
        out_specs += [_res_spec(d, tm, 512), _res_spec(d, tm, 128), _res_spec(d, tm, 128)]
        out_shape += [jax.ShapeDtypeStruct((d, s // d, 512), bf16), jax.ShapeDtypeStruct((d, s // d, 128), f32),
                      jax.ShapeDtypeStruct((d, s // d, 128), f32)]
    return _pcall(
        body, name=name, grid=(s // tm,),
        in_specs=[pl.BlockSpec((tm, 512), lambda i: (i, 0)), pl.BlockSpec((tm, 128), lambda i: (i, 0)),
                  pl.BlockSpec((tm, 128), lambda i: (i, 0))],
        out_specs=out_specs, out_shape=out_shape, scratch_shapes=[pltpu.VMEM((6, tm, 128), f32)],
        compiler_params=_params("parallel"),
    )(do, lse, delta)


def b_from_strided(grads, *, name):
    s = grads[0][0].shape[1]
    tm = _row_tile(s)

    def body(*refs):
        ins, o_ref, scr = refs[:9], refs[9], refs[10]
        nat = [jnp.concatenate([ins[q][0].astype(f32) for q in range(3)], axis=1)]
        for gi, d in enumerate(DILATED, start=1):
            blocks = [jnp.concatenate([ins[3 * gi + q][r].astype(f32) for q in range(3)], axis=1) for r in range(d)]
            nat.append(_weave(scr, blocks, d))
        pieces = [nat[g][:, lo:hi] for lo, hi in ((0, 512), (512, 640), (640, 768)) for g in range(3)]
        o_ref[...] = jnp.concatenate(pieces, axis=1).astype(bf16)

    dils = (1,) + DILATED
    in_specs = [_res_spec(d, tm, w) for d in dils for w in (512, 128, 128)]
    return _pcall(
        body, name=name, grid=(s * 1 // tm,), in_specs=in_specs, out_specs=pl.BlockSpec((tm, B_QKV), lambda i: (i, 0)),
        out_shape=jax.ShapeDtypeStruct((s, B_QKV), bf16), scratch_shapes=[pltpu.VMEM((6, tm, 128), f32)],
        compiler_params=_params("parallel"),
    )(*[a for g in grads for a in g])


def attn_merge(os_, lses, *, name):
    s = os_[0].shape[1]
    tm = _row_tile(s)
    ind_t = jnp.asarray(_head_indicator(8).T, dtype=bf16)
    dils = (1,) + DILATED

    def body(o0, o1, o2, l0, l1, l2, e_ref, o_ref, lse_ref, scr):
        both = [jnp.concatenate([o0[0], l0[0]], axis=1)]
        for og, lg, d in ((o1, l1, dils[1]), (o2, l2, dils[2])):
            both.append(_weave(scr, [jnp.concatenate([og[r], lg[r]], axis=1) for r in range(d)], d))
        ls = [b[:, 512:640] for b in both]
        m = jnp.maximum(jnp.maximum(ls[0], ls[1]), ls[2])
        tot = m + jnp.log(jnp.exp(ls[0] - m) + jnp.exp(ls[1] - m) + jnp.exp(ls[2] - m))
        lse_ref[...] = tot
        acc = jnp.zeros((tm, B_OUT), f32)
        for b, lg in zip(both, ls):
            acc = acc + _dot_split(jnp.exp(lg - tot), e_ref[...]) * b[:, :512]
        o_ref[...] = acc.astype(bf16)

    return _pcall(
        body, name=name, grid=(s * 1 // tm,),
        in_specs=[_res_spec(d, tm, 512) for d in dils] + [_res_spec(d, tm, 128) for d in dils]
        + [pl.BlockSpec((128, B_OUT), lambda i: (0, 0))],
        out_specs=[pl.BlockSpec((tm, B_OUT), lambda i: (i, 0)), pl.BlockSpec((tm, 128), lambda i: (i, 0))],
        out_shape=[jax.ShapeDtypeStruct((s, B_OUT), bf16), jax.ShapeDtypeStruct((s, 128), f32)],
        scratch_shapes=[pltpu.VMEM((5, tm, 128), f32)], compiler_params=_params("parallel"),
    )(*os_, *lses, ind_t)


def ada_mod(c_all, w, b, *, name):
    n = w.shape[2]

    def body(c_ref, w_ref, b_ref, o_ref):
        cv = c_ref[...]
        cond = cv * jax.nn.sigmoid(cv)
        o_ref[0] = jnp.dot(cond, w_ref[0], preferred_element_type=f32, precision=lax.Precision.HIGHEST) + b_ref[0]

    return _pcall(
        body, name=name, grid=(DEPTH,),
        in_specs=[pl.BlockSpec((N_DEV, D), lambda i: (0, 0)), pl.BlockSpec((1, D, n), lambda i: (i, 0, 0)),
                  pl.BlockSpec((1, 1, n), lambda i: (i, 0, 0))],
        out_specs=pl.BlockSpec((1, N_DEV, n), lambda i: (i, 0, 0)),
        out_shape=jax.ShapeDtypeStruct((DEPTH, N_DEV, n), f32), compiler_params=_params("arbitrary"),
    )(c_all, w, b)


def ada_grad(c_t, dm, *, name):
    n = dm.shape[2]

    def body(c_ref, dm_ref, o_ref):
        cv = c_ref[...]
        cond = cv * jax.nn.sigmoid(cv)
        acc = cond[:, 0:1] * dm_ref[0, 0:1, :]
        for b in range(1, N_DEV):
            acc = acc + cond[:, b:b + 1] * dm_ref[0, b:b + 1, :]
        o_ref[0] = acc

    return _pcall(
        body, name=name, grid=(DEPTH,),
        in_specs=[pl.BlockSpec((D, N_DEV), lambda i: (0, 0)), pl.BlockSpec((1, N_DEV, n), lambda i: (i, 0, 0))],
        out_specs=pl.BlockSpec((1, D, n), lambda i: (i, 0, 0)),
        out_shape=jax.ShapeDtypeStruct((DEPTH, D, n), f32), compiler_params=_params("arbitrary"),
    )(c_t, dm)


def _adam_math(w, g, m, v):
    m2 = B1 * m + (1.0 - B1) * g
    v2 = B2 * v + (1.0 - B2) * (g * g)
    mh = m2 / (1.0 - B1 ** STEP)
    vh = v2 / (1.0 - B2 ** STEP)
    return -LR * (mh / (jnp.sqrt(vh) + ADAM_EPS) + WD * w), m2, v2


def adamw(w, m, v, g, *, name):
    r, c = w.shape
    tr = 256 if r % 256 == 0 else r

    def body(w_ref, m_ref, v_ref, g_ref, d_ref, m2_ref, v2_ref):
        d_ref[...], m2_ref[...], v2_ref[...] = _adam_math(w_ref[...], g_ref[...], m_ref[...], v_ref[...])

    spec = pl.BlockSpec((tr, c), lambda i: (i, 0))
    return _pcall(
        body, name=name, grid=(r // tr,), in_specs=[spec] * 4, out_specs=[spec] * 3,
        out_shape=[jax.ShapeDtypeStruct((r, c), f32)] * 3, compiler_params=_params("parallel"),
    )(w, m, v, g)


def adamw_parts(w, m, v, own, sib, *, name):
    r, c = w.shape
    tr = 256 if r % 256 == 0 else r

    def body(w_ref, m_ref, v_ref, own_ref, sib_ref, g_ref, d_ref, m2_ref, v2_ref):
        def total(ref):
            return ((ref[0].astype(f32) + ref[1].astype(f32)) + ref[2].astype(f32)) + ref[3].astype(f32)

        g = total(own_ref) + total(sib_ref)
        g_ref[...] = g
        d_ref[...], m2_ref[...], v2_ref[...] = _adam_math(w_ref[...], g, m_ref[...], v_ref[...])

    spec = pl.BlockSpec((tr, c), lambda i: (i, 0))
    pspec = pl.BlockSpec((4, tr, c), lambda i: (0, i, 0))
    return _pcall(
        body, name=name, grid=(r // tr,), in_specs=[spec] * 3 + [pspec] * 2, out_specs=[spec] * 4,
        out_shape=[jax.ShapeDtypeStruct((r, c), f32)] * 4, compiler_params=_params("parallel"),
    )(w, m, v, own, sib)


def sum_devices(g, *, name):
    _, r, c = g.shape

    def body(g_ref, o_ref):
        acc = g_ref[0]
        for k in range(1, N_DEV):
            acc = acc + g_ref[k]
        o_ref[...] = acc

    return _pcall(body, name=name, out_shape=jax.ShapeDtypeStruct((r, c), f32))(g)


def _place():
    x, y, c = lax.axis_index("x"), lax.axis_index("y"), lax.axis_index("c")
    chips = [(1 - x, y), (x, 1 - y), (1 - x, 1 - y)]
    return x, y, c, chips


def allgather8(v, *, name):
    r, c_ = v.shape

    def body(v_ref, o_ref, send_sems, recv_sems, local_sem):
        x, y, c, _ = _place()
        me = 4 * x + 2 * y + c
        mine = pltpu.make_async_copy(v_ref, o_ref.at[me], local_sem)
        mine.start()
        flips = [(fx, fy, fc) for fx in (0, 1) for fy in (0, 1) for fc in (0, 1)][1:]

        def peer(f):
            return (x ^ f[0], y ^ f[1], c ^ f[2])

        def copy(k, slot, to):
            return pltpu.make_async_remote_copy(
                src_ref=v_ref, dst_ref=o_ref.at[slot], send_sem=send_sems.at[k], recv_sem=recv_sems.at[k],
                device_id=to, device_id_type=MESH)

        sends = [copy(k, me, peer(f)) for k, f in enumerate(flips)]
        for cp in sends:
            cp.start()
        for k, f in enumerate(flips):
            px, py, pc = peer(f)
            copy(k, 4 * px + 2 * py + pc, (x, y, c)).wait_recv()
        for cp in sends:
            cp.wait_send()
        mine.wait()

    return _pcall(
        body, name=name, in_specs=[ANY], out_specs=ANY, out_shape=jax.ShapeDtypeStruct((N_DEV, r, c_), v.dtype),
        scratch_shapes=[pltpu.SemaphoreType.DMA((7,)), pltpu.SemaphoreType.DMA((7,)), pltpu.SemaphoreType.DMA],
    )(v)


def gather_weights(shards, *, name):
    n = len(shards)

    def body(*refs):
        src, out = refs[:n], refs[n:2 * n]
        send_a, recv_a, send_f, recv_f, local_sems = refs[2 * n:]
        x, y, c, chips = _place()
        sib = (x, y, 1 - c)
        me = 2 * x + y
        locals_ = [pltpu.make_async_copy(src[a], out[a].at[me], local_sems.at[a]) for a in range(n)]
        for cp in locals_:
            cp.start()

        def half(a, which):
            rh = src[a].shape[0] // 2
            return pl.ds(which * rh, rh)

        def first(a, k, chip_from, to):
            slot = 2 * chip_from[0] + chip_from[1]
            s_ref = src[a].at[half(a, c)]
            return pltpu.make_async_remote_copy(
                src_ref=s_ref, dst_ref=out[a].at[slot, half(a, c)], send_sem=send_a.at[3 * a + k],
                recv_sem=recv_a.at[3 * a + k], device_id=to, device_id_type=MESH)

        def passed(a, k, chip_from, which, to):
            slot = 2 * chip_from[0] + chip_from[1]
            ref = out[a].at[slot, half(a, which)]
            return pltpu.make_async_remote_copy(
                src_ref=ref, dst_ref=ref, send_sem=send_f.at[3 * a + k], recv_sem=recv_f.at[3 * a + k],
                device_id=to, device_id_type=MESH)

        sends = [first(a, k, (x, y), (*chip, c)) for a in range(n) for k, chip in enumerate(chips)]
        for cp in sends:
            cp.start()
        fwd = []
        for a in range(n):
            for k, chip in enumerate(chips):
                first(a, k, chip, (x, y, c)).wait_recv()
                cp = passed(a, k, chip, c, sib)
                cp.start()
                fwd.append(cp)
        for a in range(n):
            for k, chip in enumerate(chips):
                passed(a, k, chip, 1 - c, (x, y, c)).wait_recv()
        for cp in sends + fwd:
            cp.wait_send()
        for cp in locals_:
            cp.wait()

    return _pcall(
        body, name=name, in_specs=[ANY] * n, out_specs=[ANY] * n,
        out_shape=[jax.ShapeDtypeStruct((4,) + tuple(sh.shape), sh.dtype) for sh in shards],
        scratch_shapes=[pltpu.SemaphoreType.DMA((3 * n,)) for _ in range(4)] + [pltpu.SemaphoreType.DMA((n,))],
    )(*shards)


def exchange_grads(parts, *, name):
    n = len(parts)
    flat = [p for group in parts for p in group]
    first_of = np.cumsum([0] + [len(g) for g in parts])

    def body(*refs):
        nf = len(flat)
        src_flat, own, sibo = refs[:nf], refs[nf:nf + n], refs[nf + n:nf + 2 * n]
        send_sems, recv_sems, local_sems = refs[nf + 2 * n:]
        src = [src_flat[first_of[a]:first_of[a + 1]] for a in range(n)]
        x, y, c, chips = _place()
        sib = (x, y, 1 - c)
        me = 2 * x + y

        def slot(chip):
            return 2 * chip[0] + chip[1]

        def rows(a, l):
            r = src[a][0].shape[1]
            return pl.ds(l * r, r)

        def copy(a, k, s_ref, d_ref, to):
            return pltpu.make_async_remote_copy(
                src_ref=s_ref, dst_ref=d_ref, send_sem=send_sems.at[7 * a + k], recv_sem=recv_sems.at[7 * a + k],
                device_id=to, device_id_type=MESH)

        def whole(a, k, ref_):
            return copy(a, k, ref_, ref_, (x, y, c))

        for a in range(n):
            for l in range(len(src[a])):
                pltpu.make_async_copy(src[a][l].at[me], own[a].at[me, rows(a, l)], local_sems.at[a]).start()
        for a in range(n):
            for l in range(len(src[a])):
                copy(a, 0, src[a][l].at[me], sibo[a].at[me, rows(a, l)], sib).start()
                for k, chip in enumerate(chips):
                    copy(a, 1 + k, src[a][l].at[slot(chip)], own[a].at[me, rows(a, l)], (*chip, c)).start()
        for a in range(n):
            for k, chip in enumerate(chips):
                whole(a, 1 + k, own[a].at[slot(chip)]).wait_recv()
                copy(a, 4 + k, own[a].at[slot(chip)], sibo[a].at[slot(chip)], sib).start()
        for a in range(n):
            whole(a, 0, sibo[a].at[me]).wait_recv()
            for k, chip in enumerate(chips):
                whole(a, 4 + k, sibo[a].at[slot(chip)]).wait_recv()
        for a in range(n):
            for k in range(7):
                whole(a, k, own[a].at[me]).wait_send()
            pltpu.make_async_copy(sibo[a].at[me], own[a].at[me], local_sems.at[a]).wait()

    shapes = [jax.ShapeDtypeStruct((4, len(g) * g[0].shape[1], g[0].shape[2]), g[0].dtype) for g in parts]
    res = _pcall(
        body, name=name, in_specs=[ANY] * len(flat), out_specs=[ANY] * (2 * n), out_shape=shapes + shapes,
        scratch_shapes=[pltpu.SemaphoreType.DMA((7 * n,)), pltpu.SemaphoreType.DMA((7 * n,)), pltpu.SemaphoreType.DMA((n,))],
    )(*flat)
    return res[:n], res[n:]


def _natural(g, how):
    if how == "col":
        return jnp.moveaxis(g, 0, 1).reshape(g.shape[1], 4 * g.shape[2])
    return g.reshape(4 * g.shape[1], g.shape[2])


def _chunks(gw, how):
    k, n = gw.shape
    if how == "col":
        return jnp.moveaxis(gw.reshape(k, 4, n // 4), 1, 0).astype(bf16)
    return gw.reshape(4, k // 4, n).astype(bf16)


def kernel(x, c, ada_w, ada_b, norm_mix, norm_ffn, ffn_w_in, ffn_w_out, a_w_in, a_w_out, a_sink, b_w_in, b_w_out, final_norm, loss_target, m_ada_w, m_ada_b, m_norm_mix, m_norm_ffn, m_ffn_w_in, m_ffn_w_out, m_a_w_in, m_a_w_out, m_a_sink, m_b_w_in, m_b_w_out, m_final_norm, v_ada_w, v_ada_b, v_norm_mix, v_norm_ffn, v_ffn_w_in, v_ffn_w_out, v_a_w_in, v_a_w_out, v_a_sink, v_b_w_in, v_b_w_out, v_final_norm):
    s = x.shape[1]
    xi, yi, ci = lax.axis_index("x"), lax.axis_index("y"), lax.axis_index("c")
    chip = 2 * xi + yi
    dev = 2 * chip + ci
    x0 = x[0]
    tgt = loss_target[0]

    big = {"ffn_w_in": (ffn_w_in, "col"), "ffn_w_out": (ffn_w_out, "row"), "a_w_in": (a_w_in, "col"),
           "a_w_out": (a_w_out, "row"), "b_w_in": (b_w_in, "col"), "b_w_out": (b_w_out, "col")}
    names = list(big)
    shards = [big[k][0][l].astype(bf16) for k in names for l in range(big[k][0].shape[0])]
    gathered = iter(gather_weights(shards, name="gather_weights"))
    wc = {k: [next(gathered) for _ in range(big[k][0].shape[0])] for k in names}
    wb = {k: [_natural(g, big[k][1]) for g in wc[k]] for k in names if k != "ffn_w_in"}

    c_all = allgather8(jnp.broadcast_to(c, (8, D)), name="gather_c")[:, 0, :]
    nsh = ada_w.shape[2]
    ada_b_sh = lax.dynamic_slice_in_dim(ada_b, chip * nsh, nsh, axis=1)[:, None, :]
    mod_part = ada_mod(c_all, ada_w, ada_b_sh, name="ada_mod")
    mod_all = allgather8(mod_part.reshape(DEPTH * N_DEV, nsh), name="gather_mod")
    mod_all = mod_all.reshape(4, 2, DEPTH, N_DEV, nsh)[:, 0]
    mod = lax.dynamic_index_in_dim(mod_all, dev, axis=2, keepdims=False)
    mod = jnp.moveaxis(mod, 0, 1).reshape(DEPTH, 6, 1, D)

    cfg_a = _Attn(s, mixer="a")
    cfg_b = [_Attn(s, mixer="b", group=g) for g in range(3)]
    no_sink = jnp.full((1, 2, GQ * 64, 1), NEG, f32)

    saved = []
    xc = x0
    for i in range(DEPTH):
        j = i // 2
        sh1, sc1, g1, sh2, sc2, g2 = (mod[i, q] for q in range(6))
        nmix, nffn = norm_mix[i][None, :], norm_ffn[i][None, :]
        if i % 2 == 0:
            h, qkv = mm_norm(xc, nmix, sc1, sh1, wb["a_w_in"][j], name="a_qkv")
            sinkcol = jnp.repeat(a_sink[j].reshape(2, 2, GQ), 128, axis=2)[..., None]
            o, lse = (t[0] for t in attn_fwd(qkv[None], sinkcol, cfg_a, out_dtype=bf16, name="a_attn_fwd"))
            w_o = wb["a_w_out"][j]
        else:
            h, qkv = mm_norm(xc, nmix, sc1, sh1, wb["b_w_in"][j], name="b_qkv")
            qkv = [qkv[None]] + list(b_to_strided(qkv, name="b_to_strided"))
            outs = [attn_fwd(qkv[g], no_sink, cfg_b[g], out_dtype=f32, name=f"b_attn_fwd{g}") for g in range(3)]
            o, lse = attn_merge([t[0] for t in outs], [t[1] for t in outs], name="b_merge")
            w_o = wb["b_w_out"][j]
        x1 = mm_resid(o, w_o, xc, g1, name="a_out" if i % 2 == 0 else "b_out")
        h2, gu, act, x2 = ffn_fwd(x1, nffn, sc2, sh2, g2, wc["ffn_w_in"][i], wb["ffn_w_out"][i], name="ffn_fwd")
        saved.append((xc, h, qkv, o, lse, x1, h2, gu, act))
        xc = x2

    dx, st_final = loss_head(xc, final_norm[None, :], tgt, name="loss_head")

    zero_row = jnp.zeros((1, D), f32)
    dmod_rows = [None] * DEPTH
    d_nmix, d_nffn = [None] * DEPTH, [None] * DEPTH
    d_sink = [None] * 2
    gw = {k: [None] * big[k][0].shape[0] for k in names}
    for i in reversed(range(DEPTH)):
        j = i // 2
        xin, h, qkv, o, lse, x1, h2, gu, act = saved[i]
        sh1, sc1, g1, sh2, sc2, g2 = (mod[i, q] for q in range(6))
        nmix, nffn = norm_mix[i][None, :], norm_ffn[i][None, :]
        w_fo = wb["ffn_w_out"][i]
        dgu, dx1, st2 = ffn_bwd_rows(dx, x1, gu, g2, nffn, sc2, wc["ffn_w_in"][i], w_fo, name="ffn_bwd_rows")
        gwo, dg2 = ffn_dw_out(act, dx, g2, w_fo, name="ffn_dw_out")
        gw["ffn_w_out"][i], dg2 = gwo.reshape(4, F // 4, D), dg2[0:1]
        gw["ffn_w_in"][i] = ffn_dw_in(h2, dgu, name="ffn_dw_in")
        if i % 2 == 0:
            w_o, w_i, kin, kout = wb["a_w_out"][j], wb["a_w_in"][j], "a_w_in", "a_w_out"
        else:
            w_o, w_i, kin, kout = wb["b_w_out"][j], wb["b_w_in"][j], "b_w_in", "b_w_out"
        do = mm_nt_scaled(dx1, g1, w_o, name="a_do" if i % 2 == 0 else "b_do")
        gw[kout][j], dg1 = mm_tn(o, dx1, (g1, w_o), name="a_dw_out" if i % 2 == 0 else "b_dw_out")
        if i % 2 == 0:
            sinkrow = jnp.pad(a_sink[j].reshape(2, 8), ((0, 0), (0, 120))).reshape(1, 256)
            delta, dsk = attn_delta(do, o, lse, sinkrow, name="a_delta")
            d_sink[j] = dsk[0].reshape(2, 128)[:, :8].reshape(16)
            dq, dk, dv = attn_bwd(qkv[None], do[None], lse[None], delta[None], cfg_a, name="a_attn_bwd")
            dqkv = jnp.concatenate([dq[0], dk[0], dv[0]], axis=1)
        else:
            delta, _ = attn_delta(do, o, lse, jnp.zeros((1, 128), f32), name="b_delta")
            st = [do[None], lse[None], delta[None]] + list(b_bwd_to_strided(do, lse, delta, name="b_bwd_to_strided"))
            gr = [attn_bwd(qkv[g], *st[3 * g:3 * g + 3], cfg_b[g], name=f"b_attn_bwd{g}") for g in range(3)]
            dqkv = b_from_strided(gr, name="b_from_strided")
        gw[kin][j] = mm_tn(h, dqkv, name="a_dw_in" if i % 2 == 0 else "b_dw_in")
        dx, st1 = mm_nt_norm_bwd(dqkv, w_i, xin, dx1, nmix, sc1, name="a_dh" if i % 2 == 0 else "b_dh")
        dmod_rows[i] = jnp.concatenate([st1[2:3], st1[1:2], dg1, st2[2:3], st2[1:2], dg2], axis=0)
        d_nmix[i], d_nffn[i] = st1[0:1], st2[0:1]

    sink_row = jnp.pad(jnp.concatenate(d_sink), (0, D - 32))[None, :]
    stats = jnp.concatenate(dmod_rows + d_nmix + d_nffn + [sink_row, st_final[0:1], st_final[1:2]]
                            + [zero_row] * (STAT_ROWS - 35), axis=0)
    stats_all = allgather8(stats, name="gather_stats")
    tot = sum_devices(stats_all, name="sum_stats")
    loss = 0.5 * jnp.sum(tot[34]) / float(D)

    def pack(ab, nm, nf, sk, fnm, fill):
        return jnp.concatenate([ab.reshape(24, D), nm, nf, jnp.pad(sk.reshape(1, 32), ((0, 0), (0, D - 32)), constant_values=fill),
                                fnm[None, :], jnp.full((STAT_ROWS - 34, D), fill, f32)], axis=0)

    sd, sm, sv = adamw(pack(ada_b, norm_mix, norm_ffn, a_sink, final_norm, 0.0),
                       pack(m_ada_b, m_norm_mix, m_norm_ffn, m_a_sink, m_final_norm, 0.0),
                       pack(v_ada_b, v_norm_mix, v_norm_ffn, v_a_sink, v_final_norm, 1.0), tot, name="adamw_small")

    def unpack(p):
        return p[0:24].reshape(DEPTH, 6 * D), p[24:28], p[28:32], p[32, :32].reshape(2, 16), p[33]

    small = {"grad": unpack(tot), "delta": unpack(sd), "m": unpack(sm), "v": unpack(sv)}

    dmod_all = stats_all[:, 0:24, :].reshape(N_DEV, DEPTH, 6 * D)
    dm_sh = jnp.moveaxis(lax.dynamic_slice_in_dim(dmod_all, chip * nsh, nsh, axis=2), 0, 1)
    g_ada = ada_grad(c_all.T, dm_sh, name="ada_grad")
    r_ada = (DEPTH * D, nsh)
    ada_res = adamw(ada_w.reshape(r_ada), m_ada_w.reshape(r_ada), v_ada_w.reshape(r_ada), g_ada.reshape(r_ada), name="adamw_ada")
    ada_out = [g_ada] + [t.reshape(ada_w.shape) for t in ada_res]

    parts = [gw[k] if k.startswith("ffn") else [_chunks(gl, big[k][1]) for gl in gw[k]] for k in names]
    own, sibs = exchange_grads(parts, name="exchange_grads")
    big_out = {}
    mom = {"ffn_w_in": (m_ffn_w_in, v_ffn_w_in), "ffn_w_out": (m_ffn_w_out, v_ffn_w_out), "a_w_in": (m_a_w_in, v_a_w_in),
           "a_w_out": (m_a_w_out, v_a_w_out), "b_w_in": (m_b_w_in, v_b_w_in), "b_w_out": (m_b_w_out, v_b_w_out)}
    for k, o_, s_ in zip(names, own, sibs):
        w = big[k][0]
        r2 = (-1, w.shape[-1])
        res = adamw_parts(w.reshape(r2), mom[k][0].reshape(r2), mom[k][1].reshape(r2), o_, s_, name="adamw_" + k)
        big_out[k] = [t.reshape(w.shape) for t in res]

    def leaves(q):
        sm_ = small[("grad", "delta", "m", "v")[q]]
        return (ada_out[q], sm_[0], sm_[1], sm_[2], big_out["ffn_w_in"][q], big_out["ffn_w_out"][q], big_out["a_w_in"][q],
                big_out["a_w_out"][q], sm_[3], big_out["b_w_in"][q], big_out["b_w_out"][q], sm_[4])

    return (loss, dx[None], *leaves(0), *leaves(1), *leaves(2), *leaves(3))
```

```python
import functools
import math

import numpy as np
import jax
import jax.numpy as jnp
from jax import lax
from jax.experimental import pallas as pl
from jax.experimental.pallas import tpu as pltpu

f32 = jnp.float32
bf16 = jnp.bfloat16

D = 1024
DH = 64
GQ = 4
DEPTH = 4
F = 2816
A_QKV, A_OUT = 1536, 1024
B_QKV, B_OUT = 2304, 512
B_GROUPS = ((128, 1), (512, 4), (2048, 16))
RMS_EPS = 1e-6
NEG = -1e30
LR, B1, B2, ADAM_EPS, WD, STEP = 0.001, 0.9, 0.999, 1e-08, 0.01, 10
N_DEV = 8
STAT_ROWS = 40
MESH = pl.DeviceIdType.MESH
ANY = pl.BlockSpec(memory_space=pl.ANY)


def _pcall(body, **kw):
    return pl.pallas_call(body, **kw)


def _params(*sem):
    return pltpu.CompilerParams(dimension_semantics=sem, vmem_limit_bytes=56 * 1024 * 1024)


def _row_tile(s, want=1024):
    return want if s % want == 0 else s


def mm_norm(x, nw, sc, sh, w, *, name):
    s, d = x.shape
    n = w.shape[1]
    tm = _row_tile(s)
    tn = n // 3

    def body(x_ref, nw_ref, sc_ref, sh_ref, w_ref, h_ref, y_ref):
        @pl.when(pl.program_id(1) == 0)
        def _():
            xv = x_ref[...]
            r = lax.rsqrt(jnp.mean(xv * xv, axis=-1, keepdims=True) + RMS_EPS)
            h_ref[...] = ((xv * r * nw_ref[...]) * (1.0 + sc_ref[...]) + sh_ref[...]).astype(bf16)

        y_ref[...] = jnp.dot(h_ref[...], w_ref[...], preferred_element_type=f32).astype(bf16)

    vec = pl.BlockSpec((1, d), lambda i, j: (0, 0))
    return _pcall(
        body, name=name, grid=(s // tm, n // tn),
        in_specs=[pl.BlockSpec((tm, d), lambda i, j: (i, 0)), vec, vec, vec, pl.BlockSpec((d, tn), lambda i, j: (0, j))],
        out_specs=[pl.BlockSpec((tm, d), lambda i, j: (i, 0)), pl.BlockSpec((tm, tn), lambda i, j: (i, j))],
        out_shape=[jax.ShapeDtypeStruct((s, d), bf16), jax.ShapeDtypeStruct((s, n), bf16)],
        compiler_params=_params("parallel", "arbitrary"),
    )(x, nw, sc, sh, w)


def mm_resid(a, w, xres, g, *, name):
    s, k = a.shape
    n = w.shape[1]
    tm, tn = _row_tile(s), 512

    def body(a_ref, w_ref, x_ref, g_ref, o_ref):
        o_ref[...] = x_ref[...] + g_ref[...] * jnp.dot(a_ref[...], w_ref[...], preferred_element_type=f32)

    return _pcall(
        body, name=name, grid=(s // tm, n // tn),
        in_specs=[pl.BlockSpec((tm, k), lambda i, j: (i, 0)), pl.BlockSpec((k, tn), lambda i, j: (0, j)),
                  pl.BlockSpec((tm, tn), lambda i, j: (i, j)), pl.BlockSpec((1, tn), lambda i, j: (0, j))],
        out_specs=pl.BlockSpec((tm, tn), lambda i, j: (i, j)), out_shape=jax.ShapeDtypeStruct((s, n), f32),
        compiler_params=_params("parallel", "arbitrary"),
    )(a, w, xres, g)


def mm_nt_scaled(dx, g, w, *, name):
    s, d = dx.shape
    n = w.shape[0]
    tm, tn = _row_tile(s), 512

    def body(dx_ref, g_ref, w_ref, o_ref, a_ref):
        @pl.when(pl.program_id(1) == 0)
        def _():
            a_ref[...] = (dx_ref[...] * g_ref[...]).astype(bf16)

        o_ref[...] = lax.dot_general(a_ref[...], w_ref[...], (((1,), (1,)), ((), ())), preferred_element_type=f32).astype(bf16)

    return _pcall(
        body, name=name, grid=(s // tm, n // tn),
        in_specs=[pl.BlockSpec((tm, d), lambda i, j: (i, 0)), pl.BlockSpec((1, d), lambda i, j: (0, 0)),
                  pl.BlockSpec((tn, d), lambda i, j: (j, 0))],
        out_specs=pl.BlockSpec((tm, tn), lambda i, j: (i, j)), out_shape=jax.ShapeDtypeStruct((s, n), bf16),
        scratch_shapes=[pltpu.VMEM((tm, d), bf16)], compiler_params=_params("parallel", "arbitrary"),
    )(dx, g, w)


def mm_nt_norm_bwd(a, w, x, dres, nw, sc, *, name):
    s, k = a.shape
    d = w.shape[0]
    tm = _row_tile(s, 512)
    tk = 768 if k % 768 == 0 and k % 512 != 0 else 512
    nk = k // tk

    def body(a_ref, w_ref, x_ref, dr_ref, nw_ref, sc_ref, o_ref, st_ref, acc):
        kk = pl.program_id(1)
        part = lax.dot_general(a_ref[...], w_ref[...], (((1,), (1,)), ((), ())), preferred_element_type=f32)

        @pl.when(kk == 0)
        def _():
            acc[...] = part

        @pl.when(kk != 0)
        def _():
            acc[...] += part

        @pl.when(kk == nk - 1)
        def _():
            dh = acc[...]
            xv = x_ref[...]
            r = lax.rsqrt(jnp.mean(xv * xv, axis=-1, keepdims=True) + RMS_EPS)
            xh = xv * r
            dn = dh * (1.0 + sc_ref[...])
            dxh = dn * nw_ref[...]
            o_ref[...] = dr_ref[...] + r * (dxh - xh * jnp.mean(dxh * xh, axis=-1, keepdims=True))
            rows = jnp.concatenate([
                jnp.sum(dn * xh, axis=0, keepdims=True),
                jnp.sum(dh * (xh * nw_ref[...]), axis=0, keepdims=True),
                jnp.sum(dh, axis=0, keepdims=True),
                jnp.zeros((5, d), f32)], axis=0)

            @pl.when(pl.program_id(0) == 0)
            def _():
                st_ref[...] = rows

            @pl.when(pl.program_id(0) != 0)
            def _():
                st_ref[...] += rows

    big = pl.BlockSpec((tm, d), lambda i, kk: (i, 0))
    vec = pl.BlockSpec((1, d), lambda i, kk: (0, 0))
    return _pcall(
        body, name=name, grid=(s // tm, nk),
        in_specs=[pl.BlockSpec((tm, tk), lambda i, kk: (i, kk)), pl.BlockSpec((d, tk), lambda i, kk: (0, kk)), big, big, vec, vec],
        out_specs=[big, pl.BlockSpec((8, d), lambda i, kk: (0, 0))],
        out_shape=[jax.ShapeDtypeStruct((s, d), f32), jax.ShapeDtypeStruct((8, d), f32)],
        scratch_shapes=[pltpu.VMEM((tm, d), f32)], compiler_params=_params("arbitrary", "arbitrary"),
    )(a, w, x, dres, nw, sc)


def mm_tn(a, b, scale=None, *, name):
    s, ka = a.shape
    nb = b.shape[1]
    ts = _row_tile(s)
    tn = 768 if nb % 768 == 0 and nb % 512 != 0 else 512
    tka = 1408 if ka % 1408 == 0 else min(ka, 1024)
    ns = s // ts

    def body(a_ref, b_ref, *rest):
        o_ref = rest[2] if scale is not None else rest[0]
        si = pl.program_id(2)
        part = lax.dot_general(a_ref[...], b_ref[...].astype(bf16), (((0,), (0,)), ((), ())), preferred_element_type=f32)

        @pl.when(si == 0)
        def _():
            o_ref[...] = part

        @pl.when(si != 0)
        def _():
            o_ref[...] += part

        if scale is not None:
            g_ref, wb_ref, dg_ref = rest[0], rest[1], rest[3]

            @pl.when(si == ns - 1)
            def _():
                gm = o_ref[...]
                dgp = jnp.sum(wb_ref[...].astype(f32) * gm, axis=0, keepdims=True)

                @pl.when(pl.program_id(1) == 0)
                def _():
                    dg_ref[...] = dgp

                @pl.when(pl.program_id(1) != 0)
                def _():
                    dg_ref[...] += dgp

                o_ref[...] = gm * g_ref[...]

    in_specs = [pl.BlockSpec((ts, tka), lambda j, i, k: (k, i)), pl.BlockSpec((ts, tn), lambda j, i, k: (k, j))]
    args = [a, b]
    out_specs = [pl.BlockSpec((tka, tn), lambda j, i, k: (i, j))]
    out_shape = [jax.ShapeDtypeStruct((ka, nb), f32)]
    if scale is not None:
        in_specs += [pl.BlockSpec((1, tn), lambda j, i, k: (0, j)), pl.BlockSpec((tka, tn), lambda j, i, k: (i, j))]
        args += list(scale)
        out_specs.append(pl.BlockSpec((1, tn), lambda j, i, k: (0, j)))
        out_shape.append(jax.ShapeDtypeStruct((1, nb), f32))
    res = _pcall(
        body, name=name, grid=(nb // tn, ka // tka, ns), in_specs=in_specs, out_specs=out_specs, out_shape=out_shape,
        compiler_params=_params("arbitrary", "arbitrary", "arbitrary"),
    )(*args)
    return res if scale is not None else res[0]


def loss_head(x, fn, tgt, *, name):
    s, d = x.shape
    tm = _row_tile(s, 512)

    def body(x_ref, fn_ref, t_ref, dx_ref, st_ref):
        xv = x_ref[...]
        r = lax.rsqrt(jnp.mean(xv * xv, axis=-1, keepdims=True) + RMS_EPS)
        xh = xv * r
        err = xh * fn_ref[...] - t_ref[...]
        dy = err / float(d)
        dxh = dy * fn_ref[...]
        dx_ref[...] = r * (dxh - xh * jnp.mean(dxh * xh, axis=-1, keepdims=True))
        rows = jnp.concatenate([
            jnp.sum(dy * xh, axis=0, keepdims=True),
            jnp.sum(err * err, axis=0, keepdims=True),
            jnp.zeros((6, d), f32)], axis=0)

        @pl.when(pl.program_id(0) == 0)
        def _():
            st_ref[...] = rows

        @pl.when(pl.program_id(0) != 0)
        def _():
            st_ref[...] += rows

    big = pl.BlockSpec((tm, d), lambda i: (i, 0))
    return _pcall(
        body, name=name, grid=(s // tm,), in_specs=[big, pl.BlockSpec((1, d), lambda i: (0, 0)), big],
        out_specs=[big, pl.BlockSpec((8, d), lambda i: (0, 0))],
        out_shape=[jax.ShapeDtypeStruct((s, d), f32), jax.ShapeDtypeStruct((8, d), f32)],
        compiler_params=_params("arbitrary"),
    )(x, fn, tgt)


FC = 2 * F // 4
FFN_ROWS = 256


def _resident(pairs, sems):
    @pl.when(pl.program_id(0) == 0)
    def _():
        cps = [pltpu.make_async_copy(h, v, sems.at[i]) for i, (h, v) in enumerate(pairs)]
        for cp in cps:
            cp.start()
        for cp in cps:
            cp.wait()


def ffn_fwd(x, nw, sc, sh, g, w_in, w_out, carry=(), *, name):
    s, d = x.shape
    tm = _row_tile(s, FFN_ROWS)
    nsteps = s // tm
    nc = len(carry)

    def body(*refs):
        x_ref, nw_ref, sc_ref, sh_ref, g_ref, win_hbm, wout_hbm = refs[:7]
        h_ref, gu_ref, a_ref, o_ref = refs[7 + nc:11 + nc]
        win_v, wout_v, sems = refs[11 + 2 * nc:14 + 2 * nc]
        if nc:
            start, finish = _gather_direct(refs[7:7 + nc], refs[11 + nc:11 + 2 * nc], *refs[14 + 2 * nc:])
            pl.when(pl.program_id(0) == 0)(start)
        _resident([(win_hbm, win_v), (wout_hbm, wout_v)], sems)
        xv = x_ref[...]
        r = lax.rsqrt(jnp.mean(xv * xv, axis=-1, keepdims=True) + RMS_EPS)
        h = ((xv * r * nw_ref[...]) * (1.0 + sc_ref[...]) + sh_ref[...]).astype(bf16)
        h_ref[...] = h
        y = None
        for c in range(2):
            cs = slice(c * FC, (c + 1) * FC)
            gt = jnp.dot(h, win_v[c], preferred_element_type=f32)
            up = jnp.dot(h, win_v[c + 2], preferred_element_type=f32)
            gu_ref[0, :, cs] = gt.astype(bf16)
            gu_ref[1, :, cs] = up.astype(bf16)
            act = (gt * jax.nn.sigmoid(gt) * up).astype(bf16)
            a_ref[:, cs] = act
            part = jnp.dot(act, wout_v[cs, :], preferred_element_type=f32)
            y = part if y is None else y + part
        o_ref[...] = xv + g_ref[...] * y
        if nc:
            pl.when(pl.program_id(0) == nsteps - 1)(finish)

    big = pl.BlockSpec((tm, d), lambda i: (i, 0))
    vec = pl.BlockSpec((1, d), lambda i: (0, 0))
    return _pcall(
        body, name=name, grid=(nsteps,), in_specs=[big, vec, vec, vec, vec, ANY, ANY] + [ANY] * nc,
        out_specs=[big, pl.BlockSpec((2, tm, F), lambda i: (0, i, 0)), pl.BlockSpec((tm, F), lambda i: (i, 0)), big] + [ANY] * nc,
        out_shape=[jax.ShapeDtypeStruct((s, d), bf16), jax.ShapeDtypeStruct((2, s, F), bf16),
                   jax.ShapeDtypeStruct((s, F), bf16), jax.ShapeDtypeStruct((s, d), f32)]
        + [jax.ShapeDtypeStruct((4,) + tuple(sh_.shape), sh_.dtype) for sh_ in carry],
        scratch_shapes=[pltpu.VMEM((4, d, FC), bf16), pltpu.VMEM((F, d), bf16), pltpu.SemaphoreType.DMA((2,))]
        + (_gather_scratch(nc) if nc else []),
        compiler_params=_params("arbitrary"),
    )(x, nw, sc, sh, g, w_in, w_out, *carry)


def ffn_bwd_rows(dx, x, gu, g, nw, sc, w_in, w_out, carry=(), *, name):
    s, d = x.shape
    tm = _row_tile(s, FFN_ROWS)
    nsteps = s // tm
    nc = len(carry)
    nt_dims = (((1,), (1,)), ((), ()))

    def body(*refs):
        dx_ref, x_ref, gu_ref, g_ref, nw_ref, sc_ref, win_hbm, wout_hbm = refs[:8]
        dgu_ref, o_ref, st_ref = refs[8 + nc:11 + nc]
        win_v, wout_v, sems = refs[11 + 3 * nc:14 + 3 * nc]
        if nc:
            start, finish = _exchange(refs[8:8 + nc], refs[11 + nc:11 + 2 * nc], refs[11 + 2 * nc:11 + 3 * nc],
                                      *refs[14 + 3 * nc:])
            pl.when(pl.program_id(0) == 0)(start)
        _resident([(win_hbm, win_v), (wout_hbm, wout_v)], sems)
        dxv = dx_ref[...]
        a = (dxv * g_ref[...]).astype(bf16)
        dh = None
        for c in range(2):
            cs = slice(c * FC, (c + 1) * FC)
            da = lax.dot_general(a, wout_v[cs, :], nt_dims, preferred_element_type=f32)
            gt = gu_ref[0, :, cs].astype(f32)
            up = gu_ref[1, :, cs].astype(f32)
            sg = jax.nn.sigmoid(gt)
            dgate = (da * up * (sg * (1.0 + gt * (1.0 - sg)))).astype(bf16)
            dup = (da * (gt * sg)).astype(bf16)
            dgu_ref[0, :, cs] = dgate
            dgu_ref[1, :, cs] = dup
            part = (lax.dot_general(dgate, win_v[c], nt_dims, preferred_element_type=f32)
                    + lax.dot_general(dup, win_v[c + 2], nt_dims, preferred_element_type=f32))
            dh = part if dh is None else dh + part
        xv = x_ref[...]
        r = lax.rsqrt(jnp.mean(xv * xv, axis=-1, keepdims=True) + RMS_EPS)
        xh = xv * r
        dn = dh * (1.0 + sc_ref[...])
        dxh = dn * nw_ref[...]
        o_ref[...] = dxv + r * (dxh - xh * jnp.mean(dxh * xh, axis=-1, keepdims=True))
        rows = jnp.concatenate([
            jnp.sum(dn * xh, axis=0, keepdims=True),
            jnp.sum(dh * (xh * nw_ref[...]), axis=0, keepdims=True),
            jnp.sum(dh, axis=0, keepdims=True),
            jnp.zeros((5, d), f32)], axis=0)

        @pl.when(pl.program_id(0) == 0)
        def _():
            st_ref[...] = rows

        @pl.when(pl.program_id(0) != 0)
        def _():
            st_ref[...] += rows

        if nc:
            pl.when(pl.program_id(0) == nsteps - 1)(finish)

    big = pl.BlockSpec((tm, d), lambda i: (i, 0))
    vec = pl.BlockSpec((1, d), lambda i: (0, 0))
    gus = pl.BlockSpec((2, tm, F), lambda i: (0, i, 0))
    cshapes = [jax.ShapeDtypeStruct(p.shape, p.dtype) for p in carry]
    res = _pcall(
        body, name=name, grid=(nsteps,), in_specs=[big, big, gus, vec, vec, vec, ANY, ANY] + [ANY] * nc,
        out_specs=[gus, big, pl.BlockSpec((8, d), lambda i: (0, 0))] + [ANY] * (2 * nc),
        out_shape=[jax.ShapeDtypeStruct((2, s, F), bf16), jax.ShapeDtypeStruct((s, d), f32), jax.ShapeDtypeStruct((8, d), f32)]
        + cshapes + cshapes,
        scratch_shapes=[pltpu.VMEM((4, d, FC), bf16), pltpu.VMEM((F, d), bf16), pltpu.SemaphoreType.DMA((2,))]
        + (_exchange_scratch(nc) if nc else []),
        compiler_params=_params("arbitrary"),
    )(dx, x, gu, g, nw, sc, w_in, w_out, *carry)
    return res[0], res[1], res[2], res[3:3 + nc], res[3 + nc:]


def ffn_dw_in(h, dgu, *, name):
    s, d = h.shape
    ts = _row_tile(s)
    ns = s // ts
    tn_dims = (((0,), (0,)), ((), ()))

    def body(h_ref, dgu_ref, o_ref, acc):
        k = pl.program_id(1)

        @pl.when(k == 0)
        def _():
            acc[...] = jnp.zeros_like(acc)

        hv = h_ref[...]
        for c in range(2):
            acc[c] += lax.dot_general(hv, dgu_ref[:, c * FC:(c + 1) * FC], tn_dims, preferred_element_type=f32)

        @pl.when(k == ns - 1)
        def _():
            o_ref[...] = acc[...].astype(bf16)

    return _pcall(
        body, name=name, grid=(2, ns),
        in_specs=[pl.BlockSpec((ts, d), lambda hf, k: (k, 0)), pl.BlockSpec((None, ts, F), lambda hf, k: (hf, k, 0))],
        out_specs=pl.BlockSpec((2, d, FC), lambda hf, k: (hf, 0, 0)),
        out_shape=jax.ShapeDtypeStruct((4, d, FC), bf16), scratch_shapes=[pltpu.VMEM((2, d, FC), f32)],
        compiler_params=_params("arbitrary", "arbitrary"),
    )(h, dgu)


def ffn_dw_out(a, dx, g, wb, *, name):
    s, fdim = a.shape
    d = dx.shape[1]
    ts = _row_tile(s)
    ns = s // ts
    tn = d // 2
    tn_dims = (((0,), (0,)), ((), ()))

    def body(a_ref, dx_ref, g_ref, wb_ref, o_ref, dg_ref, acc):
        k = pl.program_id(1)

        @pl.when(k == 0)
        def _():
            acc[...] = jnp.zeros_like(acc)

        acc[...] += lax.dot_general(a_ref[...], dx_ref[...].astype(bf16), tn_dims, preferred_element_type=f32)

        @pl.when(k == ns - 1)
        def _():
            gm = acc[...]
            dg_ref[...] = jnp.concatenate([jnp.sum(wb_ref[...].astype(f32) * gm, axis=0, keepdims=True),
                                           jnp.zeros((7, tn), f32)], axis=0)
            o_ref[...] = (gm * g_ref[...]).astype(bf16)

    return _pcall(
        body, name=name, grid=(2, ns),
        in_specs=[pl.BlockSpec((ts, fdim), lambda j, k: (k, 0)), pl.BlockSpec((ts, tn), lambda j, k: (k, j)),
                  pl.BlockSpec((1, tn), lambda j, k: (0, j)), pl.BlockSpec((fdim, tn), lambda j, k: (0, j))],
        out_specs=[pl.BlockSpec((fdim, tn), lambda j, k: (0, j)), pl.BlockSpec((8, tn), lambda j, k: (0, j))],
        out_shape=[jax.ShapeDtypeStruct((fdim, d), bf16), jax.ShapeDtypeStruct((8, d), f32)],
        scratch_shapes=[pltpu.VMEM((fdim, tn), f32)], compiler_params=_params("arbitrary", "arbitrary"),
    )(a, dx, g, wb)


def _alibi(n):
    return np.asarray(2.0 ** (-8.0 * np.arange(1, n + 1) / n), dtype=np.float32)


class _Attn:
    def __init__(self, s, *, mixer, group=0):
        if mixer == "a":
            self.blk, self.dil, self.npairs = 128, 1, 2
            self.qb0, self.kb0, self.vb0 = 0, 8, 10
            slopes = _alibi(16).reshape(2, 2, GQ)
        else:
            window, dil = B_GROUPS[group]
            self.blk, self.dil, self.npairs = window // (2 * dil), dil, 1
            self.qb0, self.kb0, self.vb0 = (0, 12, 15) if dil == 1 else (0, 4, 5)
            slopes = _alibi(24).reshape(3, 1, 2, GQ)[group]
        self.l = s // self.dil
        self.t = min(512, self.l)
        self.nt = self.l // self.t
        self.nb = self.t // self.blk
        blk = self.blk
        qi = np.arange(blk)[:, None]
        rel = np.arange(3 * blk)[None, :] - blk - qi
        dist = (self.dil * np.abs(rel)).astype(np.float32)
        bias = -slopes[:, :, :, None, None] * dist[None, None, None]
        bias = np.where(np.abs(rel) <= blk, bias, np.float32(NEG)).astype(np.float32)
        self.bias = bias.reshape(self.npairs, 2, GQ * blk, 3 * blk)

    def grid(self):
        return (self.dil, self.npairs, self.nt)

    def tile(self, width, col):
        return pl.BlockSpec((None, self.t, width), lambda r, hp, i: (r, i, col(hp)))

    def halo(self, width, col):
        t, blk, nbl = self.t, self.blk, self.l // self.blk
        per = t // blk
        return [
            pl.BlockSpec((None, blk, width), lambda r, hp, i: (r, jnp.maximum(i * per - 1, 0), col(hp))),
            self.tile(width, col),
            pl.BlockSpec((None, blk, width), lambda r, hp, i: (r, jnp.minimum((i + 1) * per, nbl - 1), col(hp))),
        ]

    def qcol(self, e):
        return lambda hp: self.qb0 + 2 * hp + e

    def kcol(self, hp):
        return self.kb0 + hp

    def vcol(self, hp):
        return self.vb0 + hp

    def pcol(self, hp):
        return hp


def _stack_heads(x):
    return jnp.concatenate([x[:, g * DH:(g + 1) * DH] for g in range(GQ)], axis=0)


def _unstack_heads(x, rows):
    return jnp.concatenate([x[g * rows:(g + 1) * rows] for g in range(GQ)], axis=1)


def _head_cols(tile, hh, rows):
    return jnp.concatenate([tile[:, hh * GQ + g:hh * GQ + g + 1] for g in range(GQ)], axis=0)


def attn_fwd(qkv, sinkcol, cfg, *, out_dtype, name):
    blk, t, nb, nt, dil = cfg.blk, cfg.t, cfg.nb, cfg.nt, cfg.dil
    scale = DH ** -0.5

    def body(q0, q1, kp, km, kn, vp, vm, vn, bias_ref, sink_ref, o_ref, lse_ref, kx, vx):
        ti = pl.program_id(2)
        first, last = ti == 0, ti == nt - 1
        for hh in range(2):
            sl = slice(hh * DH, (hh + 1) * DH)
            for dst, (p_, m_, n_) in ((kx, (kp, km, kn)), (vx, (vp, vm, vn))):
                dst[hh, 0:blk] = p_[:, sl]
                dst[hh, blk:blk + t] = m_[:, sl]
                dst[hh, blk + t:] = n_[:, sl]
        col = lax.broadcasted_iota(jnp.int32, (GQ * blk, 3 * blk), 1)
        lane = lax.broadcasted_iota(jnp.int32, (blk, 128), 1)
        pairs = [(b, hh) for b in range(nb) for hh in range(2)]
        qs = [_stack_heads((q0, q1)[hh][b * blk:(b + 1) * blk, :]) for b, hh in pairs]
        sc = [lax.dot_general(q_, kx[hh, b * blk:(b + 3) * blk, :], (((1,), (1,)), ((), ())), preferred_element_type=f32)
              for q_, (b, hh) in zip(qs, pairs)]
        sc = [s_ * scale + bias_ref[0, hh] for s_, (b, hh) in zip(sc, pairs)]
        sc = [jnp.where(jnp.logical_and(first, col < blk), NEG, s_) if b == 0 else s_ for s_, (b, hh) in zip(sc, pairs)]
        sc = [jnp.where(jnp.logical_and(last, col >= 2 * blk), NEG, s_) if b == nb - 1 else s_ for s_, (b, hh) in zip(sc, pairs)]
        ms = [jnp.maximum(jnp.max(s_, axis=-1, keepdims=True), sink_ref[0, hh]) for s_, (b, hh) in zip(sc, pairs)]
        ps = [jnp.exp(s_ - m_) for s_, m_ in zip(sc, ms)]
        ls = [jnp.sum(p_, axis=-1, keepdims=True) + jnp.exp(sink_ref[0, hh] - m_) for p_, m_, (b, hh) in zip(ps, ms, pairs)]
        os_ = [jnp.dot(p_.astype(bf16), vx[hh, b * blk:(b + 3) * blk, :], preferred_element_type=f32)
               for p_, (b, hh) in zip(ps, pairs)]
        os_ = [o_ / l_ for o_, l_ in zip(os_, ls)]
        lses = [m_ + jnp.log(l_) for m_, l_ in zip(ms, ls)]
        for o_, (b, hh) in zip(os_, pairs):
            o_ref[b * blk:(b + 1) * blk, hh * 256:(hh + 1) * 256] = _unstack_heads(o_, blk).astype(out_dtype)
        for b in range(nb):
            lse_tile = jnp.zeros((blk, 128), f32)
            for hh in range(2):
                lse = lses[2 * b + hh]
                for g in range(GQ):
                    lse_tile = jnp.where(lane == hh * GQ + g, lse[g * blk:(g + 1) * blk], lse_tile)
            lse_ref[b * blk:(b + 1) * blk, :] = lse_tile

    in_specs = [cfg.tile(256, cfg.qcol(e)) for e in range(2)]
    in_specs += cfg.halo(128, cfg.kcol) + cfg.halo(128, cfg.vcol)
    in_specs += [pl.BlockSpec((1, 2, GQ * blk, 3 * blk), lambda r, hp, i: (hp, 0, 0, 0)),
                 pl.BlockSpec((1, 2, GQ * blk, 1), lambda r, hp, i: (hp, 0, 0, 0))]
    return _pcall(
        body, name=name, grid=cfg.grid(), in_specs=in_specs,
        out_specs=[cfg.tile(512, cfg.pcol), cfg.tile(128, cfg.pcol)],
        out_shape=[jax.ShapeDtypeStruct((dil, cfg.l, cfg.npairs * 512), out_dtype),
                   jax.ShapeDtypeStruct((dil, cfg.l, cfg.npairs * 128), f32)],
        scratch_shapes=[pltpu.VMEM((2, t + 2 * blk, DH), bf16), pltpu.VMEM((2, t + 2 * blk, DH), bf16)],
        compiler_params=_params("arbitrary", "arbitrary", "arbitrary"),
    )(*([qkv] * 8), jnp.asarray(cfg.bias), sinkcol)


def attn_bwd(qkv, do, lse, delta, cfg, *, name):
    blk, t, nb, nt, dil, npairs = cfg.blk, cfg.t, cfg.nb, cfg.nt, cfg.dil, cfg.npairs
    scale = DH ** -0.5
    nt_dims = (((1,), (1,)), ((), ()))
    tn_dims = (((0,), (0,)), ((), ()))

    def body(q0p, q0m, q0n, q1p, q1m, q1n, kp, km, kn, vp, vm, vn, dop, dom, don, lp, lm, ln, dp_, dm_, dn_,
             bias_ref, dq_ref, dk_ref, dv_ref, kx, vx, dkx, dvx):
        ti = pl.program_id(2)
        first, last = ti == 0, ti == nt - 1
        for hh in range(2):
            sl = slice(hh * DH, (hh + 1) * DH)
            for dst, (p_, m_, n_) in ((kx, (kp, km, kn)), (vx, (vp, vm, vn))):
                dst[hh, 0:blk] = p_[:, sl]
                dst[hh, blk:blk + t] = m_[:, sl]
                dst[hh, blk + t:] = n_[:, sl]
        dkx[...] = jnp.zeros_like(dkx)
        dvx[...] = jnp.zeros_like(dvx)

        def slab(prev, main, nxt, e):
            if e == 0:
                return prev[...]
            if e == nb + 1:
                return nxt[...]
            return main[(e - 1) * blk:e * blk, :]

        col3 = lax.broadcasted_iota(jnp.int32, (GQ * blk, 3 * blk), 1)

        def keys(e):
            if e == 0:
                return 1, 2, slice(2 * blk, 3 * blk)
            if e == nb + 1:
                return nb, nb + 1, slice(0, blk)
            return e - 1, e + 2, slice(0, 3 * blk)

        def edge(sc, e):
            if e == 0:
                return jnp.where(first, NEG, sc)
            if e == nb + 1:
                return jnp.where(last, NEG, sc)
            if e == 1:
                sc = jnp.where(jnp.logical_and(first, col3 < blk), NEG, sc)
            if e == nb:
                sc = jnp.where(jnp.logical_and(last, col3 >= 2 * blk), NEG, sc)
            return sc

        pairs = [(e, hh) for e in range(nb + 2) for hh in range(2)]
        qs = [_stack_heads(slab(*((q0p, q0m, q0n), (q1p, q1m, q1n))[hh], e)) for e, hh in pairs]
        dos = [_stack_heads(slab(dop, dom, don, e)[:, hh * 256:(hh + 1) * 256]) for e, hh in pairs]
        lse_c = [_head_cols(slab(lp, lm, ln, e), hh, blk) for e, hh in pairs]
        dl_c = [_head_cols(slab(dp_, dm_, dn_, e), hh, blk) for e, hh in pairs]
        kw = [kx[hh, keys(e)[0] * blk:keys(e)[1] * blk, :] for e, hh in pairs]
        vw = [vx[hh, keys(e)[0] * blk:keys(e)[1] * blk, :] for e, hh in pairs]
        sc = [lax.dot_general(q_, k_, nt_dims, preferred_element_type=f32) for q_, k_ in zip(qs, kw)]
        dp = [lax.dot_general(d_, v_, nt_dims, preferred_element_type=f32) for d_, v_ in zip(dos, vw)]
        sc = [edge(s_ * scale + bias_ref[0, hh, :, keys(e)[2]], e) for s_, (e, hh) in zip(sc, pairs)]
        ps = [jnp.exp(s_ - l_) for s_, l_ in zip(sc, lse_c)]
        ds = [(p_ * (d_ - c_) * scale).astype(bf16) for p_, d_, c_ in zip(ps, dp, dl_c)]
        pb = [p_.astype(bf16) for p_ in ps]
        dks = [lax.dot_general(s_, q_, tn_dims, preferred_element_type=f32) for s_, q_ in zip(ds, qs)]
        dvs = [lax.dot_general(p_, d_, tn_dims, preferred_element_type=f32) for p_, d_ in zip(pb, dos)]
        dqs = [jnp.dot(s_, k_, preferred_element_type=f32) if 1 <= e <= nb else None for s_, k_, (e, hh) in zip(ds, kw, pairs)]
        for dk_, dv_, dq_, (e, hh) in zip(dks, dvs, dqs, pairs):
            k0, k1, _ = keys(e)
            dkx[hh, k0 * blk:k1 * blk, :] += dk_
            dvx[hh, k0 * blk:k1 * blk, :] += dv_
            if dq_ is not None:
                dq_ref[(e - 1) * blk:e * blk, hh * 256:(hh + 1) * 256] = _unstack_heads(dq_, blk).astype(bf16)
        for hh in range(2):
            dk_ref[:, hh * DH:(hh + 1) * DH] = dkx[hh, blk:blk + t, :].astype(bf16)
            dv_ref[:, hh * DH:(hh + 1) * DH] = dvx[hh, blk:blk + t, :].astype(bf16)

    in_specs = cfg.halo(256, cfg.qcol(0)) + cfg.halo(256, cfg.qcol(1))
    in_specs += cfg.halo(128, cfg.kcol) + cfg.halo(128, cfg.vcol)
    in_specs += cfg.halo(512, cfg.pcol) + cfg.halo(128, cfg.pcol) + cfg.halo(128, cfg.pcol)
    in_specs += [pl.BlockSpec((1, 2, GQ * blk, 3 * blk), lambda r, hp, i: (hp, 0, 0, 0))]
    return _pcall(
        body, name=name, grid=cfg.grid(), in_specs=in_specs,
        out_specs=[cfg.tile(512, cfg.pcol), cfg.tile(128, cfg.pcol), cfg.tile(128, cfg.pcol)],
        out_shape=[jax.ShapeDtypeStruct((dil, cfg.l, npairs * 512), bf16),
                   jax.ShapeDtypeStruct((dil, cfg.l, npairs * 128), bf16),
                   jax.ShapeDtypeStruct((dil, cfg.l, npairs * 128), bf16)],
        scratch_shapes=[pltpu.VMEM((2, t + 2 * blk, DH), bf16), pltpu.VMEM((2, t + 2 * blk, DH), bf16),
                        pltpu.VMEM((2, t + 2 * blk, DH), f32), pltpu.VMEM((2, t + 2 * blk, DH), f32)],
        compiler_params=_params("arbitrary", "arbitrary", "arbitrary"),
    )(*([qkv] * 12), do, do, do, lse, lse, lse, delta, delta, delta, jnp.asarray(cfg.bias))


def _head_indicator(nheads):
    e = np.zeros((nheads * DH, (nheads // 8) * 128), np.float32)
    for c in range(nheads * DH):
        h = c // DH
        e[c, (h // 8) * 128 + h % 8] = 1.0
    return e


def _dot_split(x, e):
    hi = x.astype(bf16)
    lo = (x - hi.astype(f32)).astype(bf16)
    return jnp.dot(hi, e, preferred_element_type=f32) + jnp.dot(lo, e, preferred_element_type=f32)


def attn_delta(do, o, lse, sinkrow, *, name):
    s, co = do.shape
    w = lse.shape[1]
    tm = _row_tile(s)
    ind = jnp.asarray(_head_indicator(co // DH), dtype=bf16)

    def body(do_ref, o_ref, lse_ref, sink_ref, e_ref, dl_ref, ds_ref):
        dl = _dot_split(do_ref[...].astype(f32) * o_ref[...].astype(f32), e_ref[...])
        dl_ref[...] = dl
        part = -jnp.sum(jnp.exp(sink_ref[...] - lse_ref[...]) * dl, axis=0, keepdims=True)
        part = jnp.concatenate([part, jnp.zeros((7, w), f32)], axis=0)

        @pl.when(pl.program_id(0) == 0)
        def _():
            ds_ref[...] = part

        @pl.when(pl.program_id(0) != 0)
        def _():
            ds_ref[...] += part

    return _pcall(
        body, name=name, grid=(s // tm,),
        in_specs=[pl.BlockSpec((tm, co), lambda i: (i, 0)), pl.BlockSpec((tm, co), lambda i: (i, 0)),
                  pl.BlockSpec((tm, w), lambda i: (i, 0)), pl.BlockSpec((1, w), lambda i: (0, 0)),
                  pl.BlockSpec((co, w), lambda i: (0, 0))],
        out_specs=[pl.BlockSpec((tm, w), lambda i: (i, 0)), pl.BlockSpec((8, w), lambda i: (0, 0))],
        out_shape=[jax.ShapeDtypeStruct((s, w), f32), jax.ShapeDtypeStruct((8, w), f32)],
        compiler_params=_params("arbitrary"),
    )(do, o, lse, sinkrow, ind)


def _spread(scr, x, d):
    tm, w = x.shape
    for j in range(w // 128):
        scr[j] = x[:, j * 128:(j + 1) * 128]
    return [jnp.concatenate([scr[j, pl.ds(r, tm // d, stride=d), :] for j in range(w // 128)], axis=1) for r in range(d)]


def _weave(scr, blocks, d):
    n, w = blocks[0].shape
    for r in range(d):
        for j in range(w // 128):
            scr[j, pl.ds(r, n, stride=d), :] = blocks[r][:, j * 128:(j + 1) * 128]
    return jnp.concatenate([scr[j] for j in range(w // 128)], axis=1)


def _res_spec(d, tm, w):
    return pl.BlockSpec((d, tm // d, w), lambda i: (0, i, 0))


DILATED = tuple(dil for _, dil in B_GROUPS[1:])


def b_to_strided(qkv, *, name):
    s = qkv.shape[0]
    tm = _row_tile(s)

    def body(x_ref, *rest):
        outs, scr = rest[:-1], rest[-1]
        for gi, (o_ref, d) in enumerate(zip(outs, DILATED), start=1):
            cols = jnp.concatenate([x_ref[:, gi * 512:(gi + 1) * 512], x_ref[:, 1536 + gi * 128:1536 + (gi + 1) * 128],
                                    x_ref[:, 1920 + gi * 128:1920 + (gi + 1) * 128]], axis=1).astype(f32)
            for r, blk_ in enumerate(_spread(scr, cols, d)):
                o_ref[r] = blk_.astype(bf16)

    return _pcall(
        body, name=name, grid=(s // tm,), in_specs=[pl.BlockSpec((tm, B_QKV), lambda i: (i, 0))],
        out_specs=[_res_spec(d, tm, 768) for d in DILATED],
        out_shape=[jax.ShapeDtypeStruct((d, s // d, 768), bf16) for d in DILATED],
        scratch_shapes=[pltpu.VMEM((6, tm, 128), f32)], compiler_params=_params("parallel"),
    )(qkv)


def b_bwd_to_strided(do, lse, delta, *, name):
    s = do.shape[0]
    tm = _row_tile(s)

    def body(do_ref, lse_ref, dl_ref, *rest):
        outs, scr = rest[:-1], rest[-1]
        allc = jnp.concatenate([do_ref[...].astype(f32), lse_ref[...], dl_ref[...]], axis=1)
        for gi, d in enumerate(DILATED):
            o_do, o_lse, o_dl = outs[3 * gi:3 * gi + 3]
            for r, blk_ in enumerate(_spread(scr, allc, d)):
                o_do[r] = blk_[:, :512].astype(bf16)
                o_lse[r] = blk_[:, 512:640]
                o_dl[r] = blk_[:, 640:768]

    out_specs, out_shape = [], []
    for d in DILATED:
        out_specs += [_res_spec(d, tm, 512), _res_spec(d, tm, 128), _res_spec(d, tm, 128)]
        out_shape += [jax.ShapeDtypeStruct((d, s // d, 512), bf16), jax.ShapeDtypeStruct((d, s // d, 128), f32),
                      jax.ShapeDtypeStruct((d, s // d, 128), f32)]
    return _pcall(
        body, name=name, grid=(s // tm,),
        in_specs=[pl.BlockSpec((tm, 512), lambda i: (i, 0)), pl.BlockSpec((tm, 128), lambda i: (i, 0)),
                  pl.BlockSpec((tm, 128), lambda i: (i, 0))],
        out_specs=out_specs, out_shape=out_shape, scratch_shapes=[pltpu.VMEM((6, tm, 128), f32)],
        compiler_params=_params("parallel"),
    )(do, lse, delta)


def b_from_strided(grads, *, name):
    s = grads[0][0].shape[1]
    tm = _row_tile(s)

    def body(*refs):
        ins, o_ref, scr = refs[:9], refs[9], refs[10]
        nat = [jnp.concatenate([ins[q][0].astype(f32) for q in range(3)], axis=1)]
        for gi, d in enumerate(DILATED, start=1):
            blocks = [jnp.concatenate([ins[3 * gi + q][r].astype(f32) for q in range(3)], axis=1) for r in range(d)]
            nat.append(_weave(scr, blocks, d))
        pieces = [nat[g][:, lo:hi] for lo, hi in ((0, 512), (512, 640), (640, 768)) for g in range(3)]
        o_ref[...] = jnp.concatenate(pieces, axis=1).astype(bf16)

    dils = (1,) + DILATED
    in_specs = [_res_spec(d, tm, w) for d in dils for w in (512, 128, 128)]
    return _pcall(
        body, name=name, grid=(s * 1 // tm,), in_specs=in_specs, out_specs=pl.BlockSpec((tm, B_QKV), lambda i: (i, 0)),
        out_shape=jax.ShapeDtypeStruct((s, B_QKV), bf16), scratch_shapes=[pltpu.VMEM((6, tm, 128), f32)],
        compiler_params=_params("parallel"),
    )(*[a for g in grads for a in g])


def attn_merge(os_, lses, *, name):
    s = os_[0].shape[1]
    tm = _row_tile(s)
    ind_t = jnp.asarray(_head_indicator(8).T, dtype=bf16)
    dils = (1,) + DILATED

    def body(o0, o1, o2, l0, l1, l2, e_ref, o_ref, lse_ref, scr):
        both = [jnp.concatenate([o0[0], l0[0]], axis=1)]
        for og, lg, d in ((o1, l1, dils[1]), (o2, l2, dils[2])):
            both.append(_weave(scr, [jnp.concatenate([og[r], lg[r]], axis=1) for r in range(d)], d))
        ls = [b[:, 512:640] for b in both]
        m = jnp.maximum(jnp.maximum(ls[0], ls[1]), ls[2])
        tot = m + jnp.log(jnp.exp(ls[0] - m) + jnp.exp(ls[1] - m) + jnp.exp(ls[2] - m))
        lse_ref[...] = tot
        acc = jnp.zeros((tm, B_OUT), f32)
        for b, lg in zip(both, ls):
            acc = acc + _dot_split(jnp.exp(lg - tot), e_ref[...]) * b[:, :512]
        o_ref[...] = acc.astype(bf16)

    return _pcall(
        body, name=name, grid=(s * 1 // tm,),
        in_specs=[_res_spec(d, tm, 512) for d in dils] + [_res_spec(d, tm, 128) for d in dils]
        + [pl.BlockSpec((128, B_OUT), lambda i: (0, 0))],
        out_specs=[pl.BlockSpec((tm, B_OUT), lambda i: (i, 0)), pl.BlockSpec((tm, 128), lambda i: (i, 0))],
        out_shape=[jax.ShapeDtypeStruct((s, B_OUT), bf16), jax.ShapeDtypeStruct((s, 128), f32)],
        scratch_shapes=[pltpu.VMEM((5, tm, 128), f32)], compiler_params=_params("parallel"),
    )(*os_, *lses, ind_t)


def ada_mod(c_all, w, b, *, name):
    n = w.shape[2]

    def body(c_ref, w_ref, b_ref, o_ref):
        cv = c_ref[...]
        cond = cv * jax.nn.sigmoid(cv)
        o_ref[0] = jnp.dot(cond, w_ref[0], preferred_element_type=f32, precision=lax.Precision.HIGHEST) + b_ref[0]

    return _pcall(
        body, name=name, grid=(DEPTH,),
        in_specs=[pl.BlockSpec((N_DEV, D), lambda i: (0, 0)), pl.BlockSpec((1, D, n), lambda i: (i, 0, 0)),
                  pl.BlockSpec((1, 1, n), lambda i: (i, 0, 0))],
        out_specs=pl.BlockSpec((1, N_DEV, n), lambda i: (i, 0, 0)),
        out_shape=jax.ShapeDtypeStruct((DEPTH, N_DEV, n), f32), compiler_params=_params("arbitrary"),
    )(c_all, w, b)


def ada_grad(c_t, dm, *, name):
    n = dm.shape[2]

    def body(c_ref, dm_ref, o_ref):
        cv = c_ref[...]
        cond = cv * jax.nn.sigmoid(cv)
        acc = cond[:, 0:1] * dm_ref[0, 0:1, :]
        for b in range(1, N_DEV):
            acc = acc + cond[:, b:b + 1] * dm_ref[0, b:b + 1, :]
        o_ref[0] = acc

    return _pcall(
        body, name=name, grid=(DEPTH,),
        in_specs=[pl.BlockSpec((D, N_DEV), lambda i: (0, 0)), pl.BlockSpec((1, N_DEV, n), lambda i: (i, 0, 0))],
        out_specs=pl.BlockSpec((1, D, n), lambda i: (i, 0, 0)),
        out_shape=jax.ShapeDtypeStruct((DEPTH, D, n), f32), compiler_params=_params("arbitrary"),
    )(c_t, dm)


def _adam_math(w, g, m, v):
    m2 = B1 * m + (1.0 - B1) * g
    v2 = B2 * v + (1.0 - B2) * (g * g)
    mh = m2 / (1.0 - B1 ** STEP)
    vh = v2 / (1.0 - B2 ** STEP)
    return -LR * (mh / (jnp.sqrt(vh) + ADAM_EPS) + WD * w), m2, v2


def adamw(w, m, v, g, *, name):
    r, c = w.shape
    tr = 256 if r % 256 == 0 else r

    def body(w_ref, m_ref, v_ref, g_ref, d_ref, m2_ref, v2_ref):
        d_ref[...], m2_ref[...], v2_ref[...] = _adam_math(w_ref[...], g_ref[...], m_ref[...], v_ref[...])

    spec = pl.BlockSpec((tr, c), lambda i: (i, 0))
    return _pcall(
        body, name=name, grid=(r // tr,), in_specs=[spec] * 4, out_specs=[spec] * 3,
        out_shape=[jax.ShapeDtypeStruct((r, c), f32)] * 3, compiler_params=_params("parallel"),
    )(w, m, v, g)


def adamw_parts(w, m, v, own, sib, layer, prev=None, *, name):
    c = w.shape[1]
    r = own.shape[1]
    tr = 256 if r % 256 == 0 else r // 2
    off = layer * (r // tr)

    def body(w_ref, m_ref, v_ref, own_ref, sib_ref, *rest):
        g_ref, d_ref, m2_ref, v2_ref = rest[-4:]

        def total(ref):
            return ((ref[0].astype(f32) + ref[1].astype(f32)) + ref[2].astype(f32)) + ref[3].astype(f32)

        g = total(own_ref) + total(sib_ref)
        g_ref[...] = g
        d_ref[...], m2_ref[...], v2_ref[...] = _adam_math(w_ref[...], g, m_ref[...], v_ref[...])

    spec = pl.BlockSpec((tr, c), lambda i: (off + i, 0))
    pspec = pl.BlockSpec((4, tr, c), lambda i: (0, i, 0))
    prev = () if prev is None else tuple(prev)
    return _pcall(
        body, name=name, grid=(r // tr,), in_specs=[spec] * 3 + [pspec] * 2 + [ANY] * len(prev), out_specs=[spec] * 4,
        out_shape=[jax.ShapeDtypeStruct(w.shape, f32)] * 4,
        input_output_aliases={5 + q: q for q in range(len(prev))}, compiler_params=_params("parallel"),
    )(w, m, v, own, sib, *prev)


def sum_devices(g, *, name):
    _, r, c = g.shape

    def body(g_ref, o_ref):
        acc = g_ref[0]
        for k in range(1, N_DEV):
            acc = acc + g_ref[k]
        o_ref[...] = acc

    return _pcall(body, name=name, out_shape=jax.ShapeDtypeStruct((r, c), f32))(g)


def _place():
    x, y, c = lax.axis_index("x"), lax.axis_index("y"), lax.axis_index("c")
    chips = [(1 - x, y), (x, 1 - y), (1 - x, 1 - y)]
    return x, y, c, chips


def allgather8(v, *, name):
    r, c_ = v.shape

    def body(v_ref, o_ref, send_sems, recv_sems, local_sem):
        x, y, c, _ = _place()
        me = 4 * x + 2 * y + c
        mine = pltpu.make_async_copy(v_ref, o_ref.at[me], local_sem)
        mine.start()
        flips = [(fx, fy, fc) for fx in (0, 1) for fy in (0, 1) for fc in (0, 1)][1:]

        def peer(f):
            return (x ^ f[0], y ^ f[1], c ^ f[2])

        def copy(k, slot, to):
            return pltpu.make_async_remote_copy(
                src_ref=v_ref, dst_ref=o_ref.at[slot], send_sem=send_sems.at[k], recv_sem=recv_sems.at[k],
                device_id=to, device_id_type=MESH)

        sends = [copy(k, me, peer(f)) for k, f in enumerate(flips)]
        for cp in sends:
            cp.start()
        for k, f in enumerate(flips):
            px, py, pc = peer(f)
            copy(k, 4 * px + 2 * py + pc, (x, y, c)).wait_recv()
        for cp in sends:
            cp.wait_send()
        mine.wait()

    return _pcall(
        body, name=name, in_specs=[ANY], out_specs=ANY, out_shape=jax.ShapeDtypeStruct((N_DEV, r, c_), v.dtype),
        scratch_shapes=[pltpu.SemaphoreType.DMA((7,)), pltpu.SemaphoreType.DMA((7,)), pltpu.SemaphoreType.DMA],
    )(v)


def gather_weights(shards, *, name):
    n = len(shards)

    def body(*refs):
        src, out = refs[:n], refs[n:2 * n]
        send_a, recv_a, send_f, recv_f, local_sems = refs[2 * n:]
        x, y, c, chips = _place()
        sib = (x, y, 1 - c)
        me = 2 * x + y
        locals_ = [pltpu.make_async_copy(src[a], out[a].at[me], local_sems.at[a]) for a in range(n)]
        for cp in locals_:
            cp.start()

        def half(a, which):
            rh = src[a].shape[0] // 2
            return pl.ds(which * rh, rh)

        def first(a, k, chip_from, to):
            slot = 2 * chip_from[0] + chip_from[1]
            s_ref = src[a].at[half(a, c)]
            return pltpu.make_async_remote_copy(
                src_ref=s_ref, dst_ref=out[a].at[slot, half(a, c)], send_sem=send_a.at[3 * a + k],
                recv_sem=recv_a.at[3 * a + k], device_id=to, device_id_type=MESH)

        def passed(a, k, chip_from, which, to):
            slot = 2 * chip_from[0] + chip_from[1]
            ref = out[a].at[slot, half(a, which)]
            return pltpu.make_async_remote_copy(
                src_ref=ref, dst_ref=ref, send_sem=send_f.at[3 * a + k], recv_sem=recv_f.at[3 * a + k],
                device_id=to, device_id_type=MESH)

        sends = [first(a, k, (x, y), (*chip, c)) for a in range(n) for k, chip in enumerate(chips)]
        for cp in sends:
            cp.start()
        fwd = []
        for a in range(n):
            for k, chip in enumerate(chips):
                first(a, k, chip, (x, y, c)).wait_recv()
                cp = passed(a, k, chip, c, sib)
                cp.start()
                fwd.append(cp)
        for a in range(n):
            for k, chip in enumerate(chips):
                passed(a, k, chip, 1 - c, (x, y, c)).wait_recv()
        for cp in sends + fwd:
            cp.wait_send()
        for cp in locals_:
            cp.wait()

    return _pcall(
        body, name=name, in_specs=[ANY] * n, out_specs=[ANY] * n,
        out_shape=[jax.ShapeDtypeStruct((4,) + tuple(sh.shape), sh.dtype) for sh in shards],
        scratch_shapes=[pltpu.SemaphoreType.DMA((3 * n,)) for _ in range(4)] + [pltpu.SemaphoreType.DMA((n,))],
    )(*shards)


def _gather_direct(src, out, send_sems, recv_sems, local_sems):
    n = len(src)
    x, y, c, chips = _place()
    me = 2 * x + y

    def copy(a, k, slot, to):
        return pltpu.make_async_remote_copy(
            src_ref=src[a], dst_ref=out[a].at[slot], send_sem=send_sems.at[3 * a + k], recv_sem=recv_sems.at[3 * a + k],
            device_id=to, device_id_type=MESH)

    def start():
        for a in range(n):
            pltpu.make_async_copy(src[a], out[a].at[me], local_sems.at[a]).start()
            for k, chip in enumerate(chips):
                copy(a, k, me, (*chip, c)).start()

    def finish():
        for a in range(n):
            for k, chip in enumerate(chips):
                copy(a, k, 2 * chip[0] + chip[1], (x, y, c)).wait_recv()
        for a in range(n):
            for k in range(3):
                copy(a, k, me, (x, y, c)).wait_send()
            pltpu.make_async_copy(src[a], out[a].at[me], local_sems.at[a]).wait()

    return start, finish


def _gather_scratch(n):
    return [pltpu.SemaphoreType.DMA((3 * n,)), pltpu.SemaphoreType.DMA((3 * n,)), pltpu.SemaphoreType.DMA((n,))]


def _exchange(src, own, sibo, send_sems, recv_sems, local_sems):
    n = len(src)
    x, y, c, chips = _place()
    sib = (x, y, 1 - c)
    me = 2 * x + y

    def slot(chip):
        return 2 * chip[0] + chip[1]

    def copy(a, k, s_ref, d_ref, to):
        return pltpu.make_async_remote_copy(
            src_ref=s_ref, dst_ref=d_ref, send_sem=send_sems.at[7 * a + k], recv_sem=recv_sems.at[7 * a + k],
            device_id=to, device_id_type=MESH)

    def start():
        for a in range(n):
            pltpu.make_async_copy(src[a].at[me], own[a].at[me], local_sems.at[a]).start()
            copy(a, 0, src[a].at[me], sibo[a].at[me], sib).start()
            for k, chip in enumerate(chips):
                copy(a, 1 + k, src[a].at[slot(chip)], own[a].at[me], (*chip, c)).start()

    def finish():
        for a in range(n):
            for k, chip in enumerate(chips):
                copy(a, 1 + k, src[a].at[me], own[a].at[slot(chip)], (x, y, c)).wait_recv()
                copy(a, 4 + k, own[a].at[slot(chip)], sibo[a].at[slot(chip)], sib).start()
        for a in range(n):
            copy(a, 0, src[a].at[me], sibo[a].at[me], (x, y, c)).wait_recv()
            for k, chip in enumerate(chips):
                copy(a, 4 + k, src[a].at[me], sibo[a].at[slot(chip)], (x, y, c)).wait_recv()
        for a in range(n):
            for k in range(7):
                copy(a, k, src[a].at[me], own[a].at[me], (x, y, c)).wait_send()
            pltpu.make_async_copy(src[a].at[me], own[a].at[me], local_sems.at[a]).wait()

    return start, finish


def _exchange_scratch(n):
    return [pltpu.SemaphoreType.DMA((7 * n,)), pltpu.SemaphoreType.DMA((7 * n,)), pltpu.SemaphoreType.DMA((n,))]


def exchange_grads(parts, *, name):
    n = len(parts)

    def body(*refs):
        start, finish = _exchange(refs[:n], refs[n:2 * n], refs[2 * n:3 * n], *refs[3 * n:])
        start()
        finish()

    shapes = [jax.ShapeDtypeStruct(p.shape, p.dtype) for p in parts]
    res = _pcall(body, name=name, in_specs=[ANY] * n, out_specs=[ANY] * (2 * n), out_shape=shapes + shapes,
                 scratch_shapes=_exchange_scratch(n))(*parts)
    return res[:n], res[n:]


def _natural(g, how):
    if how == "col":
        return jnp.moveaxis(g, 0, 1).reshape(g.shape[1], 4 * g.shape[2])
    return g.reshape(4 * g.shape[1], g.shape[2])


def _chunks(gw, how):
    k, n = gw.shape
    if how == "col":
        return jnp.moveaxis(gw.reshape(k, 4, n // 4), 1, 0).astype(bf16)
    return gw.reshape(4, k // 4, n).astype(bf16)


def kernel(x, c, ada_w, ada_b, norm_mix, norm_ffn, ffn_w_in, ffn_w_out, a_w_in, a_w_out, a_sink, b_w_in, b_w_out, final_norm, loss_target, m_ada_w, m_ada_b, m_norm_mix, m_norm_ffn, m_ffn_w_in, m_ffn_w_out, m_a_w_in, m_a_w_out, m_a_sink, m_b_w_in, m_b_w_out, m_final_norm, v_ada_w, v_ada_b, v_norm_mix, v_norm_ffn, v_ffn_w_in, v_ffn_w_out, v_a_w_in, v_a_w_out, v_a_sink, v_b_w_in, v_b_w_out, v_final_norm):
    s = x.shape[1]
    xi, yi, ci = lax.axis_index("x"), lax.axis_index("y"), lax.axis_index("c")
    chip = 2 * xi + yi
    dev = 2 * chip + ci
    x0 = x[0]
    tgt = loss_target[0]

    big = {"ffn_w_in": (ffn_w_in, "col"), "ffn_w_out": (ffn_w_out, "row"), "a_w_in": (a_w_in, "col"),
           "a_w_out": (a_w_out, "row"), "b_w_in": (b_w_in, "col"), "b_w_out": (b_w_out, "col")}
    names = list(big)

    def layer_keys(i):
        mix = "a" if i % 2 == 0 else "b"
        return [("ffn_w_in", i), ("ffn_w_out", i), (mix + "_w_in", i // 2), (mix + "_w_out", i // 2)]

    def shards_of(i):
        return [big[k][0][l].astype(bf16) for k, l in layer_keys(i)]

    def weights_of(i, gathered):
        return {k: (g if k == "ffn_w_in" else _natural(g, big[k][1])) for (k, _), g in zip(layer_keys(i), gathered)}

    wl = [weights_of(0, gather_weights(shards_of(0), name="gather_weights"))]

    c_all = allgather8(jnp.broadcast_to(c, (8, D)), name="gather_c")[:, 0, :]
    nsh = ada_w.shape[2]
    ada_b_sh = lax.dynamic_slice_in_dim(ada_b, chip * nsh, nsh, axis=1)[:, None, :]
    mod_part = ada_mod(c_all, ada_w, ada_b_sh, name="ada_mod")
    mod_all = allgather8(mod_part.reshape(DEPTH * N_DEV, nsh), name="gather_mod")
    mod_all = mod_all.reshape(4, 2, DEPTH, N_DEV, nsh)[:, 0]
    mod = lax.dynamic_index_in_dim(mod_all, dev, axis=2, keepdims=False)
    mod = jnp.moveaxis(mod, 0, 1).reshape(DEPTH, 6, 1, D)

    cfg_a = _Attn(s, mixer="a")
    cfg_b = [_Attn(s, mixer="b", group=g) for g in range(3)]
    no_sink = jnp.full((1, 2, GQ * 64, 1), NEG, f32)

    saved = []
    xc = x0
    for i in range(DEPTH):
        j = i // 2
        sh1, sc1, g1, sh2, sc2, g2 = (mod[i, q] for q in range(6))
        nmix, nffn = norm_mix[i][None, :], norm_ffn[i][None, :]
        mix = "a" if i % 2 == 0 else "b"
        if i % 2 == 0:
            h, qkv = mm_norm(xc, nmix, sc1, sh1, wl[i]["a_w_in"], name="a_qkv")
            sinkcol = jnp.repeat(a_sink[j].reshape(2, 2, GQ), 128, axis=2)[..., None]
            o, lse = (t[0] for t in attn_fwd(qkv[None], sinkcol, cfg_a, out_dtype=bf16, name="a_attn_fwd"))
        else:
            h, qkv = mm_norm(xc, nmix, sc1, sh1, wl[i]["b_w_in"], name="b_qkv")
            qkv = [qkv[None]] + list(b_to_strided(qkv, name="b_to_strided"))
            outs = [attn_fwd(qkv[g], no_sink, cfg_b[g], out_dtype=f32, name=f"b_attn_fwd{g}") for g in range(3)]
            o, lse = attn_merge([t[0] for t in outs], [t[1] for t in outs], name="b_merge")
        x1 = mm_resid(o, wl[i][mix + "_w_out"], xc, g1, name=mix + "_out")
        nxt = shards_of(i + 1) if i + 1 < DEPTH else []
        h2, gu, act, x2, *got = ffn_fwd(x1, nffn, sc2, sh2, g2, wl[i]["ffn_w_in"], wl[i]["ffn_w_out"], nxt,
                                        name="ffn_fwd_gather" if nxt else "ffn_fwd")
        if nxt:
            wl.append(weights_of(i + 1, got))
        saved.append((xc, h, qkv, o, lse, x1, h2, gu, act))
        xc = x2

    dx, st_final = loss_head(xc, final_norm[None, :], tgt, name="loss_head")

    zero_row = jnp.zeros((1, D), f32)
    dmod_rows = [None] * DEPTH
    d_nmix, d_nffn = [None] * DEPTH, [None] * DEPTH
    d_sink = [None] * 2
    parts, exchanged = None, {}
    for i in reversed(range(DEPTH)):
        j = i // 2
        xin, h, qkv, o, lse, x1, h2, gu, act = saved[i]
        sh1, sc1, g1, sh2, sc2, g2 = (mod[i, q] for q in range(6))
        nmix, nffn = norm_mix[i][None, :], norm_ffn[i][None, :]
        mix = "a" if i % 2 == 0 else "b"
        w_fo, w_o, w_i = wl[i]["ffn_w_out"], wl[i][mix + "_w_out"], wl[i][mix + "_w_in"]
        dgu, dx1, st2, own_, sib_ = ffn_bwd_rows(dx, x1, gu, g2, nffn, sc2, wl[i]["ffn_w_in"], w_fo, parts or [],
                                                 name="ffn_bwd_rows_exchange" if parts else "ffn_bwd_rows")
        if parts:
            exchanged[i + 1] = (own_, sib_)
        gwo, dg2 = ffn_dw_out(act, dx, g2, w_fo, name="ffn_dw_out")
        dg2 = dg2[0:1]
        gwi = ffn_dw_in(h2, dgu, name="ffn_dw_in")
        do = mm_nt_scaled(dx1, g1, w_o, name=mix + "_do")
        gmo, dg1 = mm_tn(o, dx1, (g1, w_o), name=mix + "_dw_out")
        if i % 2 == 0:
            sinkrow = jnp.pad(a_sink[j].reshape(2, 8), ((0, 0), (0, 120))).reshape(1, 256)
            delta, dsk = attn_delta(do, o, lse, sinkrow, name="a_delta")
            d_sink[j] = dsk[0].reshape(2, 128)[:, :8].reshape(16)
            dq, dk, dv = attn_bwd(qkv[None], do[None], lse[None], delta[None], cfg_a, name="a_attn_bwd")
            dqkv = jnp.concatenate([dq[0], dk[0], dv[0]], axis=1)
        else:
            delta, _ = attn_delta(do, o, lse, jnp.zeros((1, 128), f32), name="b_delta")
            st = [do[None], lse[None], delta[None]] + list(b_bwd_to_strided(do, lse, delta, name="b_bwd_to_strided"))
            gr = [attn_bwd(qkv[g], *st[3 * g:3 * g + 3], cfg_b[g], name=f"b_attn_bwd{g}") for g in range(3)]
            dqkv = b_from_strided(gr, name="b_from_strided")
        gmi = mm_tn(h, dqkv, name=mix + "_dw_in")
        dx, st1 = mm_nt_norm_bwd(dqkv, w_i, xin, dx1, nmix, sc1, name=mix + "_dh")
        dmod_rows[i] = jnp.concatenate([st1[2:3], st1[1:2], dg1, st2[2:3], st2[1:2], dg2], axis=0)
        d_nmix[i], d_nffn[i] = st1[0:1], st2[0:1]
        parts = [gwi, gwo.reshape(4, F // 4, D), _chunks(gmi, big[mix + "_w_in"][1]), _chunks(gmo, big[mix + "_w_out"][1])]
    exchanged[0] = exchange_grads(parts, name="exchange_grads")

    sink_row = jnp.pad(jnp.concatenate(d_sink), (0, D - 32))[None, :]
    stats = jnp.concatenate(dmod_rows + d_nmix + d_nffn + [sink_row, st_final[0:1], st_final[1:2]]
                            + [zero_row] * (STAT_ROWS - 35), axis=0)
    stats_all = allgather8(stats, name="gather_stats")
    tot = sum_devices(stats_all, name="sum_stats")
    loss = 0.5 * jnp.sum(tot[34]) / float(D)

    def pack(ab, nm, nf, sk, fnm, fill):
        return jnp.concatenate([ab.reshape(24, D), nm, nf, jnp.pad(sk.reshape(1, 32), ((0, 0), (0, D - 32)), constant_values=fill),
                                fnm[None, :], jnp.full((STAT_ROWS - 34, D), fill, f32)], axis=0)

    sd, sm, sv = adamw(pack(ada_b, norm_mix, norm_ffn, a_sink, final_norm, 0.0),
                       pack(m_ada_b, m_norm_mix, m_norm_ffn, m_a_sink, m_final_norm, 0.0),
                       pack(v_ada_b, v_norm_mix, v_norm_ffn, v_a_sink, v_final_norm, 1.0), tot, name="adamw_small")

    def unpack(p):
        return p[0:24].reshape(DEPTH, 6 * D), p[24:28], p[28:32], p[32, :32].reshape(2, 16), p[33]

    small = {"grad": unpack(tot), "delta": unpack(sd), "m": unpack(sm), "v": unpack(sv)}

    dmod_all = stats_all[:, 0:24, :].reshape(N_DEV, DEPTH, 6 * D)
    dm_sh = jnp.moveaxis(lax.dynamic_slice_in_dim(dmod_all, chip * nsh, nsh, axis=2), 0, 1)
    g_ada = ada_grad(c_all.T, dm_sh, name="ada_grad")
    r_ada = (DEPTH * D, nsh)
    ada_res = adamw(ada_w.reshape(r_ada), m_ada_w.reshape(r_ada), v_ada_w.reshape(r_ada), g_ada.reshape(r_ada), name="adamw_ada")
    ada_out = [g_ada] + [t.reshape(ada_w.shape) for t in ada_res]

    mom = {"ffn_w_in": (m_ffn_w_in, v_ffn_w_in), "ffn_w_out": (m_ffn_w_out, v_ffn_w_out), "a_w_in": (m_a_w_in, v_a_w_in),
           "a_w_out": (m_a_w_out, v_a_w_out), "b_w_in": (m_b_w_in, v_b_w_in), "b_w_out": (m_b_w_out, v_b_w_out)}
    big_out = {k: None for k in names}
    for i in reversed(range(DEPTH)):
        own_, sib_ = exchanged[i]
        for (k, l), o_, s_ in zip(layer_keys(i), own_, sib_):
            w = big[k][0]
            r2 = (-1, w.shape[-1])
            big_out[k] = adamw_parts(w.reshape(r2), mom[k][0].reshape(r2), mom[k][1].reshape(r2), o_, s_, l, big_out[k],
                                     name=f"adamw_{k}{l}")
    big_out = {k: [t.reshape(big[k][0].shape) for t in big_out[k]] for k in names}

    def leaves(q):
        sm_ = small[("grad", "delta", "m", "v")[q]]
        return (ada_out[q], sm_[0], sm_[1], sm_[2], big_out["ffn_w_in"][q], big_out["ffn_w_out"][q], big_out["a_w_in"][q],
                big_out["a_w_out"][q], sm_[3], big_out["b_w_in"][q], big_out["b_w_out"][q], sm_[4])

    return (loss, dx[None], *leaves(0), *leaves(1), *leaves(2), *leaves(3))
```

```python
import functools
import math

import numpy as np
import jax
import jax.numpy as jnp
from jax import lax
from jax.experimental import pallas as pl
from jax.experimental.pallas import tpu as pltpu

f32 = jnp.float32
bf16 = jnp.bfloat16

D = 1024
DH = 64
GQ = 4
DEPTH = 4
F = 2816
A_QKV, A_OUT = 1536, 1024
B_QKV, B_OUT = 2304, 512
B_GROUPS = ((128, 1), (512, 4), (2048, 16))
RMS_EPS = 1e-6
NEG = -1e30
LR, B1, B2, ADAM_EPS, WD, STEP = 0.001, 0.9, 0.999, 1e-08, 0.01, 10
N_DEV = 8
STAT_ROWS = 40
MESH = pl.DeviceIdType.MESH
ANY = pl.BlockSpec(memory_space=pl.ANY)


def _pcall(body, **kw):
    return pl.pallas_call(body, **kw)


def _params(*sem):
    return pltpu.CompilerParams(dimension_semantics=sem, vmem_limit_bytes=56 * 1024 * 1024)


def _row_tile(s, want=1024):
    return want if s % want == 0 else s


ROW_CHUNKS = 4


def mm_norm(x, nw, sc, sh, w, *, name):
    s, d = x.shape
    n = w.shape[1]
    tm = _row_tile(s)
    rc = tm // ROW_CHUNKS

    def body(x_ref, nw_ref, sc_ref, sh_ref, w_ref, h_ref, y_ref):
        hs = []
        for c in range(ROW_CHUNKS):
            xv = x_ref[c * rc:(c + 1) * rc, :]
            r = lax.rsqrt(jnp.mean(xv * xv, axis=-1, keepdims=True) + RMS_EPS)
            hs.append(((xv * r * nw_ref[...]) * (1.0 + sc_ref[...]) + sh_ref[...]).astype(bf16))
        ys = [jnp.dot(h, w_ref[...], preferred_element_type=f32) for h in hs]
        for c in range(ROW_CHUNKS):
            h_ref[c * rc:(c + 1) * rc, :] = hs[c]
            y_ref[c * rc:(c + 1) * rc, :] = ys[c].astype(bf16)

    vec = pl.BlockSpec((1, d), lambda i: (0, 0))
    return _pcall(
        body, name=name, grid=(s // tm,),
        in_specs=[pl.BlockSpec((tm, d), lambda i: (i, 0)), vec, vec, vec, pl.BlockSpec((d, n), lambda i: (0, 0))],
        out_specs=[pl.BlockSpec((tm, d), lambda i: (i, 0)), pl.BlockSpec((tm, n), lambda i: (i, 0))],
        out_shape=[jax.ShapeDtypeStruct((s, d), bf16), jax.ShapeDtypeStruct((s, n), bf16)],
        compiler_params=_params("parallel"),
    )(x, nw, sc, sh, w)


def mm_resid(a, w, xres, g, *, name):
    s, k = a.shape
    n = w.shape[1]
    tm, tn = _row_tile(s), 512

    def body(a_ref, w_ref, x_ref, g_ref, o_ref):
        o_ref[...] = x_ref[...] + g_ref[...] * jnp.dot(a_ref[...], w_ref[...], preferred_element_type=f32)

    return _pcall(
        body, name=name, grid=(s // tm, n // tn),
        in_specs=[pl.BlockSpec((tm, k), lambda i, j: (i, 0)), pl.BlockSpec((k, tn), lambda i, j: (0, j)),
                  pl.BlockSpec((tm, tn), lambda i, j: (i, j)), pl.BlockSpec((1, tn), lambda i, j: (0, j))],
        out_specs=pl.BlockSpec((tm, tn), lambda i, j: (i, j)), out_shape=jax.ShapeDtypeStruct((s, n), f32),
        compiler_params=_params("parallel", "arbitrary"),
    )(a, w, xres, g)


def mm_nt_scaled(dx, g, w, *, name):
    s, d = dx.shape
    n = w.shape[0]
    tm, tn = _row_tile(s), 512

    def body(dx_ref, g_ref, w_ref, o_ref, a_ref):
        @pl.when(pl.program_id(1) == 0)
        def _():
            a_ref[...] = (dx_ref[...] * g_ref[...]).astype(bf16)

        o_ref[...] = lax.dot_general(a_ref[...], w_ref[...], (((1,), (1,)), ((), ())), preferred_element_type=f32).astype(bf16)

    return _pcall(
        body, name=name, grid=(s // tm, n // tn),
        in_specs=[pl.BlockSpec((tm, d), lambda i, j: (i, 0)), pl.BlockSpec((1, d), lambda i, j: (0, 0)),
                  pl.BlockSpec((tn, d), lambda i, j: (j, 0))],
        out_specs=pl.BlockSpec((tm, tn), lambda i, j: (i, j)), out_shape=jax.ShapeDtypeStruct((s, n), bf16),
        scratch_shapes=[pltpu.VMEM((tm, d), bf16)], compiler_params=_params("parallel", "arbitrary"),
    )(dx, g, w)


def mm_nt_norm_bwd(a, w, x, dres, nw, sc, *, name):
    s, k = a.shape
    d = w.shape[0]
    tm = _row_tile(s)
    rc = tm // ROW_CHUNKS

    def body(a_ref, w_ref, x_ref, dr_ref, nw_ref, sc_ref, o_ref, st_ref):
        dhs = [lax.dot_general(a_ref[c * rc:(c + 1) * rc, :], w_ref[...], (((1,), (1,)), ((), ())), preferred_element_type=f32)
               for c in range(ROW_CHUNKS)]
        rows = None
        for c, dh in enumerate(dhs):
            xv = x_ref[c * rc:(c + 1) * rc, :]
            r = lax.rsqrt(jnp.mean(xv * xv, axis=-1, keepdims=True) + RMS_EPS)
            xh = xv * r
            dn = dh * (1.0 + sc_ref[...])
            dxh = dn * nw_ref[...]
            o_ref[c * rc:(c + 1) * rc, :] = dr_ref[c * rc:(c + 1) * rc, :] + r * (dxh - xh * jnp.mean(dxh * xh, axis=-1, keepdims=True))
            part = jnp.concatenate([
                jnp.sum(dn * xh, axis=0, keepdims=True),
                jnp.sum(dh * (xh * nw_ref[...]), axis=0, keepdims=True),
                jnp.sum(dh, axis=0, keepdims=True),
                jnp.zeros((5, d), f32)], axis=0)
            rows = part if rows is None else rows + part

        @pl.when(pl.program_id(0) == 0)
        def _():
            st_ref[...] = rows

        @pl.when(pl.program_id(0) != 0)
        def _():
            st_ref[...] += rows

    big = pl.BlockSpec((tm, d), lambda i: (i, 0))
    vec = pl.BlockSpec((1, d), lambda i: (0, 0))
    return _pcall(
        body, name=name, grid=(s // tm,),
        in_specs=[pl.BlockSpec((tm, k), lambda i: (i, 0)), pl.BlockSpec((d, k), lambda i: (0, 0)), big, big, vec, vec],
        out_specs=[big, pl.BlockSpec((8, d), lambda i: (0, 0))],
        out_shape=[jax.ShapeDtypeStruct((s, d), f32), jax.ShapeDtypeStruct((8, d), f32)],
        compiler_params=_params("arbitrary"),
    )(a, w, x, dres, nw, sc)


def mm_tn(a, b, scale=None, *, name):
    s, ka = a.shape
    nb = b.shape[1]
    ts = _row_tile(s)
    tn = 768 if nb % 768 == 0 and nb % 512 != 0 else 512
    tka = 1408 if ka % 1408 == 0 else min(ka, 1024)
    ns = s // ts

    def body(a_ref, b_ref, *rest):
        o_ref = rest[2] if scale is not None else rest[0]
        si = pl.program_id(2)
        part = lax.dot_general(a_ref[...], b_ref[...].astype(bf16), (((0,), (0,)), ((), ())), preferred_element_type=f32)

        @pl.when(si == 0)
        def _():
            o_ref[...] = part

        @pl.when(si != 0)
        def _():
            o_ref[...] += part

        if scale is not None:
            g_ref, wb_ref, dg_ref = rest[0], rest[1], rest[3]

            @pl.when(si == ns - 1)
            def _():
                gm = o_ref[...]
                dgp = jnp.sum(wb_ref[...].astype(f32) * gm, axis=0, keepdims=True)

                @pl.when(pl.program_id(1) == 0)
                def _():
                    dg_ref[...] = dgp

                @pl.when(pl.program_id(1) != 0)
                def _():
                    dg_ref[...] += dgp

                o_ref[...] = gm * g_ref[...]

    in_specs = [pl.BlockSpec((ts, tka), lambda j, i, k: (k, i)), pl.BlockSpec((ts, tn), lambda j, i, k: (k, j))]
    args = [a, b]
    out_specs = [pl.BlockSpec((tka, tn), lambda j, i, k: (i, j))]
    out_shape = [jax.ShapeDtypeStruct((ka, nb), f32)]
    if scale is not None:
        in_specs += [pl.BlockSpec((1, tn), lambda j, i, k: (0, j)), pl.BlockSpec((tka, tn), lambda j, i, k: (i, j))]
        args += list(scale)
        out_specs.append(pl.BlockSpec((1, tn), lambda j, i, k: (0, j)))
        out_shape.append(jax.ShapeDtypeStruct((1, nb), f32))
    res = _pcall(
        body, name=name, grid=(nb // tn, ka // tka, ns), in_specs=in_specs, out_specs=out_specs, out_shape=out_shape,
        compiler_params=_params("arbitrary", "arbitrary", "arbitrary"),
    )(*args)
    return res if scale is not None else res[0]


def loss_head(x, fn, tgt, *, name):
    s, d = x.shape
    tm = _row_tile(s, 512)

    def body(x_ref, fn_ref, t_ref, dx_ref, st_ref):
        xv = x_ref[...]
        r = lax.rsqrt(jnp.mean(xv * xv, axis=-1, keepdims=True) + RMS_EPS)
        xh = xv * r
        err = xh * fn_ref[...] - t_ref[...]
        dy = err / float(d)
        dxh = dy * fn_ref[...]
        dx_ref[...] = r * (dxh - xh * jnp.mean(dxh * xh, axis=-1, keepdims=True))
        rows = jnp.concatenate([
            jnp.sum(dy * xh, axis=0, keepdims=True),
            jnp.sum(err * err, axis=0, keepdims=True),
            jnp.zeros((6, d), f32)], axis=0)

        @pl.when(pl.program_id(0) == 0)
        def _():
            st_ref[...] = rows

        @pl.when(pl.program_id(0) != 0)
        def _():
            st_ref[...] += rows

    big = pl.BlockSpec((tm, d), lambda i: (i, 0))
    return _pcall(
        body, name=name, grid=(s // tm,), in_specs=[big, pl.BlockSpec((1, d), lambda i: (0, 0)), big],
        out_specs=[big, pl.BlockSpec((8, d), lambda i: (0, 0))],
        out_shape=[jax.ShapeDtypeStruct((s, d), f32), jax.ShapeDtypeStruct((8, d), f32)],
        compiler_params=_params("arbitrary"),
    )(x, fn, tgt)


FC = 2 * F // 4
FFN_ROWS = 256


def _resident(pairs, sems):
    @pl.when(pl.program_id(0) == 0)
    def _():
        cps = [pltpu.make_async_copy(h, v, sems.at[i]) for i, (h, v) in enumerate(pairs)]
        for cp in cps:
            cp.start()
        for cp in cps:
            cp.wait()


def ffn_fwd(x, nw, sc, sh, g, w_in, w_out, carry=(), *, name):
    s, d = x.shape
    tm = _row_tile(s, FFN_ROWS)
    nsteps = s // tm
    nc = len(carry)

    def body(*refs):
        x_ref, nw_ref, sc_ref, sh_ref, g_ref, win_hbm, wout_hbm = refs[:7]
        h_ref, gu_ref, a_ref, o_ref = refs[7 + nc:11 + nc]
        win_v, wout_v, sems = refs[11 + 2 * nc:14 + 2 * nc]
        if nc:
            start, finish = _gather_direct(refs[7:7 + nc], refs[11 + nc:11 + 2 * nc], *refs[14 + 2 * nc:])
            pl.when(pl.program_id(0) == 0)(start)
        _resident([(win_hbm, win_v), (wout_hbm, wout_v)], sems)
        xv = x_ref[...]
        r = lax.rsqrt(jnp.mean(xv * xv, axis=-1, keepdims=True) + RMS_EPS)
        h = ((xv * r * nw_ref[...]) * (1.0 + sc_ref[...]) + sh_ref[...]).astype(bf16)
        h_ref[...] = h
        y = None
        for c in range(2):
            cs = slice(c * FC, (c + 1) * FC)
            gt = jnp.dot(h, win_v[c], preferred_element_type=f32)
            up = jnp.dot(h, win_v[c + 2], preferred_element_type=f32)
            gu_ref[0, :, cs] = gt.astype(bf16)
            gu_ref[1, :, cs] = up.astype(bf16)
            act = (gt * jax.nn.sigmoid(gt) * up).astype(bf16)
            a_ref[:, cs] = act
            part = jnp.dot(act, wout_v[cs, :], preferred_element_type=f32)
            y = part if y is None else y + part
        o_ref[...] = xv + g_ref[...] * y
        if nc:
            pl.when(pl.program_id(0) == nsteps - 1)(finish)

    big = pl.BlockSpec((tm, d), lambda i: (i, 0))
    vec = pl.BlockSpec((1, d), lambda i: (0, 0))
    return _pcall(
        body, name=name, grid=(nsteps,), in_specs=[big, vec, vec, vec, vec, ANY, ANY] + [ANY] * nc,
        out_specs=[big, pl.BlockSpec((2, tm, F), lambda i: (0, i, 0)), pl.BlockSpec((tm, F), lambda i: (i, 0)), big] + [ANY] * nc,
        out_shape=[jax.ShapeDtypeStruct((s, d), bf16), jax.ShapeDtypeStruct((2, s, F), bf16),
                   jax.ShapeDtypeStruct((s, F), bf16), jax.ShapeDtypeStruct((s, d), f32)]
        + [jax.ShapeDtypeStruct((4,) + tuple(sh_.shape), sh_.dtype) for sh_ in carry],
        scratch_shapes=[pltpu.VMEM((4, d, FC), bf16), pltpu.VMEM((F, d), bf16), pltpu.SemaphoreType.DMA((2,))]
        + (_gather_scratch(nc) if nc else []),
        compiler_params=_params("arbitrary"),
    )(x, nw, sc, sh, g, w_in, w_out, *carry)


def ffn_bwd_rows(dx, x, gu, g, nw, sc, w_in, w_out, carry=(), *, name):
    s, d = x.shape
    tm = _row_tile(s, FFN_ROWS)
    nsteps = s // tm
    nc = len(carry)
    nt_dims = (((1,), (1,)), ((), ()))

    def body(*refs):
        dx_ref, x_ref, gu_ref, g_ref, nw_ref, sc_ref, win_hbm, wout_hbm = refs[:8]
        dgu_ref, o_ref, st_ref = refs[8 + nc:11 + nc]
        win_v, wout_v, sems = refs[11 + 3 * nc:14 + 3 * nc]
        if nc:
            start, finish = _exchange(refs[8:8 + nc], refs[11 + nc:11 + 2 * nc], refs[11 + 2 * nc:11 + 3 * nc],
                                      *refs[14 + 3 * nc:])
            pl.when(pl.program_id(0) == 0)(start)
        _resident([(win_hbm, win_v), (wout_hbm, wout_v)], sems)
        dxv = dx_ref[...]
        a = (dxv * g_ref[...]).astype(bf16)
        dh = None
        for c in range(2):
            cs = slice(c * FC, (c + 1) * FC)
            da = lax.dot_general(a, wout_v[cs, :], nt_dims, preferred_element_type=f32)
            gt = gu_ref[0, :, cs].astype(f32)
            up = gu_ref[1, :, cs].astype(f32)
            sg = jax.nn.sigmoid(gt)
            dgate = (da * up * (sg * (1.0 + gt * (1.0 - sg)))).astype(bf16)
            dup = (da * (gt * sg)).astype(bf16)
            dgu_ref[0, :, cs] = dgate
            dgu_ref[1, :, cs] = dup
            part = (lax.dot_general(dgate, win_v[c], nt_dims, preferred_element_type=f32)
                    + lax.dot_general(dup, win_v[c + 2], nt_dims, preferred_element_type=f32))
            dh = part if dh is None else dh + part
        xv = x_ref[...]
        r = lax.rsqrt(jnp.mean(xv * xv, axis=-1, keepdims=True) + RMS_EPS)
        xh = xv * r
        dn = dh * (1.0 + sc_ref[...])
        dxh = dn * nw_ref[...]
        o_ref[...] = dxv + r * (dxh - xh * jnp.mean(dxh * xh, axis=-1, keepdims=True))
        rows = jnp.concatenate([
            jnp.sum(dn * xh, axis=0, keepdims=True),
            jnp.sum(dh * (xh * nw_ref[...]), axis=0, keepdims=True),
            jnp.sum(dh, axis=0, keepdims=True),
            jnp.zeros((5, d), f32)], axis=0)

        @pl.when(pl.program_id(0) == 0)
        def _():
            st_ref[...] = rows

        @pl.when(pl.program_id(0) != 0)
        def _():
            st_ref[...] += rows

        if nc:
            pl.when(pl.program_id(0) == nsteps - 1)(finish)

    big = pl.BlockSpec((tm, d), lambda i: (i, 0))
    vec = pl.BlockSpec((1, d), lambda i: (0, 0))
    gus = pl.BlockSpec((2, tm, F), lambda i: (0, i, 0))
    cshapes = [jax.ShapeDtypeStruct(p.shape, p.dtype) for p in carry]
    res = _pcall(
        body, name=name, grid=(nsteps,), in_specs=[big, big, gus, vec, vec, vec, ANY, ANY] + [ANY] * nc,
        out_specs=[gus, big, pl.BlockSpec((8, d), lambda i: (0, 0))] + [ANY] * (2 * nc),
        out_shape=[jax.ShapeDtypeStruct((2, s, F), bf16), jax.ShapeDtypeStruct((s, d), f32), jax.ShapeDtypeStruct((8, d), f32)]
        + cshapes + cshapes,
        scratch_shapes=[pltpu.VMEM((4, d, FC), bf16), pltpu.VMEM((F, d), bf16), pltpu.SemaphoreType.DMA((2,))]
        + (_exchange_scratch(nc) if nc else []),
        compiler_params=_params("arbitrary"),
    )(dx, x, gu, g, nw, sc, w_in, w_out, *carry)
    return res[0], res[1], res[2], res[3:3 + nc], res[3 + nc:]


def ffn_dw_in(h, dgu, *, name):
    s, d = h.shape
    ts = _row_tile(s)
    ns = s // ts
    tn_dims = (((0,), (0,)), ((), ()))

    def body(h_ref, dgu_ref, o_ref, acc):
        k = pl.program_id(1)

        @pl.when(k == 0)
        def _():
            acc[...] = jnp.zeros_like(acc)

        hv = h_ref[...]
        for c in range(2):
            acc[c] += lax.dot_general(hv, dgu_ref[:, c * FC:(c + 1) * FC], tn_dims, preferred_element_type=f32)

        @pl.when(k == ns - 1)
        def _():
            o_ref[...] = acc[...].astype(bf16)

    return _pcall(
        body, name=name, grid=(2, ns),
        in_specs=[pl.BlockSpec((ts, d), lambda hf, k: (k, 0)), pl.BlockSpec((None, ts, F), lambda hf, k: (hf, k, 0))],
        out_specs=pl.BlockSpec((2, d, FC), lambda hf, k: (hf, 0, 0)),
        out_shape=jax.ShapeDtypeStruct((4, d, FC), bf16), scratch_shapes=[pltpu.VMEM((2, d, FC), f32)],
        compiler_params=_params("arbitrary", "arbitrary"),
    )(h, dgu)


def ffn_dw_out(a, dx, g, wb, *, name):
    s, fdim = a.shape
    d = dx.shape[1]
    ts = _row_tile(s)
    ns = s // ts
    tn = d // 2
    tn_dims = (((0,), (0,)), ((), ()))

    def body(a_ref, dx_ref, g_ref, wb_ref, o_ref, dg_ref, acc):
        k = pl.program_id(1)

        @pl.when(k == 0)
        def _():
            acc[...] = jnp.zeros_like(acc)

        acc[...] += lax.dot_general(a_ref[...], dx_ref[...].astype(bf16), tn_dims, preferred_element_type=f32)

        @pl.when(k == ns - 1)
        def _():
            gm = acc[...]
            dg_ref[...] = jnp.concatenate([jnp.sum(wb_ref[...].astype(f32) * gm, axis=0, keepdims=True),
                                           jnp.zeros((7, tn), f32)], axis=0)
            o_ref[...] = (gm * g_ref[...]).astype(bf16)

    return _pcall(
        body, name=name, grid=(2, ns),
        in_specs=[pl.BlockSpec((ts, fdim), lambda j, k: (k, 0)), pl.BlockSpec((ts, tn), lambda j, k: (k, j)),
                  pl.BlockSpec((1, tn), lambda j, k: (0, j)), pl.BlockSpec((fdim, tn), lambda j, k: (0, j))],
        out_specs=[pl.BlockSpec((fdim, tn), lambda j, k: (0, j)), pl.BlockSpec((8, tn), lambda j, k: (0, j))],
        out_shape=[jax.ShapeDtypeStruct((fdim, d), bf16), jax.ShapeDtypeStruct((8, d), f32)],
        scratch_shapes=[pltpu.VMEM((fdim, tn), f32)], compiler_params=_params("arbitrary", "arbitrary"),
    )(a, dx, g, wb)


def _alibi(n):
    return np.asarray(2.0 ** (-8.0 * np.arange(1, n + 1) / n), dtype=np.float32)


class _Attn:
    def __init__(self, s, *, mixer, group=0):
        if mixer == "a":
            self.blk, self.dil, self.npairs = 128, 1, 2
            self.qb0, self.kb0, self.vb0 = 0, 8, 10
            slopes = _alibi(16).reshape(2, 2, GQ)
        else:
            window, dil = B_GROUPS[group]
            self.blk, self.dil, self.npairs = window // (2 * dil), dil, 1
            self.qb0, self.kb0, self.vb0 = (0, 12, 15) if dil == 1 else (0, 4, 5)
            slopes = _alibi(24).reshape(3, 1, 2, GQ)[group]
        self.l = s // self.dil
        self.t = min(512, self.l)
        self.nt = self.l // self.t
        self.nb = self.t // self.blk
        blk = self.blk
        qi = np.arange(blk)[:, None]
        rel = np.arange(3 * blk)[None, :] - blk - qi
        dist = (self.dil * np.abs(rel)).astype(np.float32)
        bias = -slopes[:, :, :, None, None] * dist[None, None, None]
        bias = np.where(np.abs(rel) <= blk, bias, np.float32(NEG)).astype(np.float32)
        self.bias = bias.reshape(self.npairs, 2, GQ * blk, 3 * blk)

    def grid(self):
        return (self.dil, self.npairs, self.nt)

    def tile(self, width, col):
        return pl.BlockSpec((None, self.t, width), lambda r, hp, i: (r, i, col(hp)))

    def halo(self, width, col):
        t, blk, nbl = self.t, self.blk, self.l // self.blk
        per = t // blk
        return [
            pl.BlockSpec((None, blk, width), lambda r, hp, i: (r, jnp.maximum(i * per - 1, 0), col(hp))),
            self.tile(width, col),
            pl.BlockSpec((None, blk, width), lambda r, hp, i: (r, jnp.minimum((i + 1) * per, nbl - 1), col(hp))),
        ]

    def qcol(self, e):
        return lambda hp: self.qb0 + 2 * hp + e

    def kcol(self, hp):
        return self.kb0 + hp

    def vcol(self, hp):
        return self.vb0 + hp

    def pcol(self, hp):
        return hp


def _stack_heads(x):
    return jnp.concatenate([x[:, g * DH:(g + 1) * DH] for g in range(GQ)], axis=0)


def _unstack_heads(x, rows):
    return jnp.concatenate([x[g * rows:(g + 1) * rows] for g in range(GQ)], axis=1)


def _head_cols(tile, hh, rows):
    return jnp.concatenate([tile[:, hh * GQ + g:hh * GQ + g + 1] for g in range(GQ)], axis=0)


def attn_fwd(qkv, sinkcol, cfg, *, out_dtype, name):
    blk, t, nb, nt, dil = cfg.blk, cfg.t, cfg.nb, cfg.nt, cfg.dil
    scale = DH ** -0.5

    def body(q0, q1, kp, km, kn, vp, vm, vn, bias_ref, sink_ref, o_ref, lse_ref, kx, vx):
        ti = pl.program_id(2)
        first, last = ti == 0, ti == nt - 1
        for hh in range(2):
            sl = slice(hh * DH, (hh + 1) * DH)
            for dst, (p_, m_, n_) in ((kx, (kp, km, kn)), (vx, (vp, vm, vn))):
                dst[hh, 0:blk] = p_[:, sl]
                dst[hh, blk:blk + t] = m_[:, sl]
                dst[hh, blk + t:] = n_[:, sl]
        col = lax.broadcasted_iota(jnp.int32, (GQ * blk, 3 * blk), 1)
        lane = lax.broadcasted_iota(jnp.int32, (blk, 128), 1)
        pairs = [(b, hh) for b in range(nb) for hh in range(2)]
        qs = [_stack_heads((q0, q1)[hh][b * blk:(b + 1) * blk, :]) * scale for b, hh in pairs]
        sc = [lax.dot_general(q_, kx[hh, b * blk:(b + 3) * blk, :], (((1,), (1,)), ((), ())), preferred_element_type=f32)
              for q_, (b, hh) in zip(qs, pairs)]
        sc = [s_ + bias_ref[0, hh] for s_, (b, hh) in zip(sc, pairs)]
        sc = [jnp.where(jnp.logical_and(first, col < blk), NEG, s_) if b == 0 else s_ for s_, (b, hh) in zip(sc, pairs)]
        sc = [jnp.where(jnp.logical_and(last, col >= 2 * blk), NEG, s_) if b == nb - 1 else s_ for s_, (b, hh) in zip(sc, pairs)]
        ms = [jnp.maximum(jnp.max(s_, axis=-1, keepdims=True), sink_ref[0, hh]) for s_, (b, hh) in zip(sc, pairs)]
        ps = [jnp.exp(s_ - m_) for s_, m_ in zip(sc, ms)]
        ls = [jnp.sum(p_, axis=-1, keepdims=True) + jnp.exp(sink_ref[0, hh] - m_) for p_, m_, (b, hh) in zip(ps, ms, pairs)]
        os_ = [jnp.dot(p_.astype(bf16), vx[hh, b * blk:(b + 3) * blk, :], preferred_element_type=f32)
               for p_, (b, hh) in zip(ps, pairs)]
        os_ = [o_ / l_ for o_, l_ in zip(os_, ls)]
        lses = [m_ + jnp.log(l_) for m_, l_ in zip(ms, ls)]
        for o_, (b, hh) in zip(os_, pairs):
            o_ref[b * blk:(b + 1) * blk, hh * 256:(hh + 1) * 256] = _unstack_heads(o_, blk).astype(out_dtype)
        for b in range(nb):
            lse_tile = jnp.zeros((blk, 128), f32)
            for hh in range(2):
                lse = lses[2 * b + hh]
                for g in range(GQ):
                    lse_tile = jnp.where(lane == hh * GQ + g, lse[g * blk:(g + 1) * blk], lse_tile)
            lse_ref[b * blk:(b + 1) * blk, :] = lse_tile

    in_specs = [cfg.tile(256, cfg.qcol(e)) for e in range(2)]
    in_specs += cfg.halo(128, cfg.kcol) + cfg.halo(128, cfg.vcol)
    in_specs += [pl.BlockSpec((1, 2, GQ * blk, 3 * blk), lambda r, hp, i: (hp, 0, 0, 0)),
                 pl.BlockSpec((1, 2, GQ * blk, 1), lambda r, hp, i: (hp, 0, 0, 0))]
    return _pcall(
        body, name=name, grid=cfg.grid(), in_specs=in_specs,
        out_specs=[cfg.tile(512, cfg.pcol), cfg.tile(128, cfg.pcol)],
        out_shape=[jax.ShapeDtypeStruct((dil, cfg.l, cfg.npairs * 512), out_dtype),
                   jax.ShapeDtypeStruct((dil, cfg.l, cfg.npairs * 128), f32)],
        scratch_shapes=[pltpu.VMEM((2, t + 2 * blk, DH), bf16), pltpu.VMEM((2, t + 2 * blk, DH), bf16)],
        compiler_params=_params("arbitrary", "arbitrary", "arbitrary"),
    )(*([qkv] * 8), jnp.asarray(cfg.bias), sinkcol)


def attn_bwd(qkv, do, lse, delta, cfg, *, name):
    blk, t, nb, nt, dil, npairs = cfg.blk, cfg.t, cfg.nb, cfg.nt, cfg.dil, cfg.npairs
    scale = DH ** -0.5
    nt_dims = (((1,), (1,)), ((), ()))
    tn_dims = (((0,), (0,)), ((), ()))

    def body(q0p, q0m, q0n, q1p, q1m, q1n, kp, km, kn, vp, vm, vn, dop, dom, don, lp, lm, ln, dp_, dm_, dn_,
             bias_ref, dq_ref, dk_ref, dv_ref, kx, vx, dkx, dvx):
        ti = pl.program_id(2)
        first, last = ti == 0, ti == nt - 1
        for hh in range(2):
            sl = slice(hh * DH, (hh + 1) * DH)
            for dst, (p_, m_, n_) in ((kx, (kp, km, kn)), (vx, (vp, vm, vn))):
                dst[hh, 0:blk] = p_[:, sl]
                dst[hh, blk:blk + t] = m_[:, sl]
                dst[hh, blk + t:] = n_[:, sl]
        dkx[...] = jnp.zeros_like(dkx)
        dvx[...] = jnp.zeros_like(dvx)

        def slab(prev, main, nxt, e):
            if e == 0:
                return prev[...]
            if e == nb + 1:
                return nxt[...]
            return main[(e - 1) * blk:e * blk, :]

        col3 = lax.broadcasted_iota(jnp.int32, (GQ * blk, 3 * blk), 1)

        def keys(e):
            if e == 0:
                return 1, 2, slice(2 * blk, 3 * blk)
            if e == nb + 1:
                return nb, nb + 1, slice(0, blk)
            return e - 1, e + 2, slice(0, 3 * blk)

        def edge(sc, e):
            if e == 0:
                return jnp.where(first, NEG, sc)
            if e == nb + 1:
                return jnp.where(last, NEG, sc)
            if e == 1:
                sc = jnp.where(jnp.logical_and(first, col3 < blk), NEG, sc)
            if e == nb:
                sc = jnp.where(jnp.logical_and(last, col3 >= 2 * blk), NEG, sc)
            return sc

        pairs = [(e, hh) for e in range(nb + 2) for hh in range(2)]
        qs = [_stack_heads(slab(*((q0p, q0m, q0n), (q1p, q1m, q1n))[hh], e)) * scale for e, hh in pairs]
        dos = [_stack_heads(slab(dop, dom, don, e)[:, hh * 256:(hh + 1) * 256]) for e, hh in pairs]
        lse_c = [_head_cols(slab(lp, lm, ln, e), hh, blk) for e, hh in pairs]
        dl_c = [_head_cols(slab(dp_, dm_, dn_, e), hh, blk) for e, hh in pairs]
        kw = [kx[hh, keys(e)[0] * blk:keys(e)[1] * blk, :] for e, hh in pairs]
        vw = [vx[hh, keys(e)[0] * blk:keys(e)[1] * blk, :] for e, hh in pairs]
        sc = [lax.dot_general(q_, k_, nt_dims, preferred_element_type=f32) for q_, k_ in zip(qs, kw)]
        dp = [lax.dot_general(d_, v_, nt_dims, preferred_element_type=f32) for d_, v_ in zip(dos, vw)]
        sc = [edge(s_ + bias_ref[0, hh, :, keys(e)[2]], e) for s_, (e, hh) in zip(sc, pairs)]
        ps = [jnp.exp(s_ - l_) for s_, l_ in zip(sc, lse_c)]
        ds = [(p_ * (d_ - c_)).astype(bf16) for p_, d_, c_ in zip(ps, dp, dl_c)]
        pb = [p_.astype(bf16) for p_ in ps]
        dks = [lax.dot_general(s_, q_, tn_dims, preferred_element_type=f32) for s_, q_ in zip(ds, qs)]
        dvs = [lax.dot_general(p_, d_, tn_dims, preferred_element_type=f32) for p_, d_ in zip(pb, dos)]
        dqs = [jnp.dot(s_, k_, preferred_element_type=f32) if 1 <= e <= nb else None for s_, k_, (e, hh) in zip(ds, kw, pairs)]
        for dk_, dv_, dq_, (e, hh) in zip(dks, dvs, dqs, pairs):
            k0, k1, _ = keys(e)
            dkx[hh, k0 * blk:k1 * blk, :] += dk_
            dvx[hh, k0 * blk:k1 * blk, :] += dv_
            if dq_ is not None:
                dq_ref[(e - 1) * blk:e * blk, hh * 256:(hh + 1) * 256] = (_unstack_heads(dq_, blk) * scale).astype(bf16)
        for hh in range(2):
            dk_ref[:, hh * DH:(hh + 1) * DH] = dkx[hh, blk:blk + t, :].astype(bf16)
            dv_ref[:, hh * DH:(hh + 1) * DH] = dvx[hh, blk:blk + t, :].astype(bf16)

    in_specs = cfg.halo(256, cfg.qcol(0)) + cfg.halo(256, cfg.qcol(1))
    in_specs += cfg.halo(128, cfg.kcol) + cfg.halo(128, cfg.vcol)
    in_specs += cfg.halo(512, cfg.pcol) + cfg.halo(128, cfg.pcol) + cfg.halo(128, cfg.pcol)
    in_specs += [pl.BlockSpec((1, 2, GQ * blk, 3 * blk), lambda r, hp, i: (hp, 0, 0, 0))]
    return _pcall(
        body, name=name, grid=cfg.grid(), in_specs=in_specs,
        out_specs=[cfg.tile(512, cfg.pcol), cfg.tile(128, cfg.pcol), cfg.tile(128, cfg.pcol)],
        out_shape=[jax.ShapeDtypeStruct((dil, cfg.l, npairs * 512), bf16),
                   jax.ShapeDtypeStruct((dil, cfg.l, npairs * 128), bf16),
                   jax.ShapeDtypeStruct((dil, cfg.l, npairs * 128), bf16)],
        scratch_shapes=[pltpu.VMEM((2, t + 2 * blk, DH), bf16), pltpu.VMEM((2, t + 2 * blk, DH), bf16),
                        pltpu.VMEM((2, t + 2 * blk, DH), f32), pltpu.VMEM((2, t + 2 * blk, DH), f32)],
        compiler_params=_params("arbitrary", "arbitrary", "arbitrary"),
    )(*([qkv] * 12), do, do, do, lse, lse, lse, delta, delta, delta, jnp.asarray(cfg.bias))


def _head_indicator(nheads):
    e = np.zeros((nheads * DH, (nheads // 8) * 128), np.float32)
    for c in range(nheads * DH):
        h = c // DH
        e[c, (h // 8) * 128 + h % 8] = 1.0
    return e


def _dot_split(x, e):
    hi = x.astype(bf16)
    lo = (x - hi.astype(f32)).astype(bf16)
    return jnp.dot(hi, e, preferred_element_type=f32) + jnp.dot(lo, e, preferred_element_type=f32)


def attn_delta(do, o, lse, sinkrow, *, name):
    s, co = do.shape
    w = lse.shape[1]
    tm = _row_tile(s)
    ind = jnp.asarray(_head_indicator(co // DH), dtype=bf16)

    def body(do_ref, o_ref, lse_ref, sink_ref, e_ref, dl_ref, ds_ref):
        dl = _dot_split(do_ref[...].astype(f32) * o_ref[...].astype(f32), e_ref[...])
        dl_ref[...] = dl
        part = -jnp.sum(jnp.exp(sink_ref[...] - lse_ref[...]) * dl, axis=0, keepdims=True)
        part = jnp.concatenate([part, jnp.zeros((7, w), f32)], axis=0)

        @pl.when(pl.program_id(0) == 0)
        def _():
            ds_ref[...] = part

        @pl.when(pl.program_id(0) != 0)
        def _():
            ds_ref[...] += part

    return _pcall(
        body, name=name, grid=(s // tm,),
        in_specs=[pl.BlockSpec((tm, co), lambda i: (i, 0)), pl.BlockSpec((tm, co), lambda i: (i, 0)),
                  pl.BlockSpec((tm, w), lambda i: (i, 0)), pl.BlockSpec((1, w), lambda i: (0, 0)),
                  pl.BlockSpec((co, w), lambda i: (0, 0))],
        out_specs=[pl.BlockSpec((tm, w), lambda i: (i, 0)), pl.BlockSpec((8, w), lambda i: (0, 0))],
        out_shape=[jax.ShapeDtypeStruct((s, w), f32), jax.ShapeDtypeStruct((8, w), f32)],
        compiler_params=_params("arbitrary"),
    )(do, o, lse, sinkrow, ind)


def _spread(scr, x, d):
    tm, w = x.shape
    for j in range(w // 128):
        scr[j] = x[:, j * 128:(j + 1) * 128]
    return [jnp.concatenate([scr[j, pl.ds(r, tm // d, stride=d), :] for j in range(w // 128)], axis=1) for r in range(d)]


def _weave(scr, blocks, d):
    n, w = blocks[0].shape
    for r in range(d):
        for j in range(w // 128):
            scr[j, pl.ds(r, n, stride=d), :] = blocks[r][:, j * 128:(j + 1) * 128]
    return jnp.concatenate([scr[j] for j in range(w // 128)], axis=1)


def _res_spec(d, tm, w):
    return pl.BlockSpec((d, tm // d, w), lambda i: (0, i, 0))


DILATED = tuple(dil for _, dil in B_GROUPS[1:])


def b_to_strided(qkv, *, name):
    s = qkv.shape[0]
    tm = _row_tile(s)

    def body(x_ref, *rest):
        outs, scr = rest[:-1], rest[-1]
        for gi, (o_ref, d) in enumerate(zip(outs, DILATED), start=1):
            cols = jnp.concatenate([x_ref[:, gi * 512:(gi + 1) * 512], x_ref[:, 1536 + gi * 128:1536 + (gi + 1) * 128],
                                    x_ref[:, 1920 + gi * 128:1920 + (gi + 1) * 128]], axis=1).astype(f32)
            for r, blk_ in enumerate(_spread(scr, cols, d)):
                o_ref[r] = blk_.astype(bf16)

    return _pcall(
        body, name=name, grid=(s // tm,), in_specs=[pl.BlockSpec((tm, B_QKV), lambda i: (i, 0))],
        out_specs=[_res_spec(d, tm, 768) for d in DILATED],
        out_shape=[jax.ShapeDtypeStruct((d, s // d, 768), bf16) for d in DILATED],
        scratch_shapes=[pltpu.VMEM((6, tm, 128), f32)], compiler_params=_params("parallel"),
    )(qkv)


def b_bwd_to_strided(do, lse, delta, *, name):
    s = do.shape[0]
    tm = _row_tile(s)

    def body(do_ref, lse_ref, dl_ref, *rest):
        outs, scr = rest[:-1], rest[-1]
        allc = jnp.concatenate([do_ref[...].astype(f32), lse_ref[...], dl_ref[...]], axis=1)
        for gi, d in enumerate(DILATED):
            o_do, o_lse, o_dl = outs[3 * gi:3 * gi + 3]
            for r, blk_ in enumerate(_spread(scr, allc, d)):
                o_do[r] = blk_[:, :512].astype(bf16)
                o_lse[r] = blk_[:, 512:640]
                o_dl[r] = blk_[:, 640:768]

    out_specs, out_shape = [], []
    for d in DILATED:
        out_specs += [_res_spec(d, tm, 512), _res_spec(d, tm, 128), _res_spec(d, tm, 128)]
        out_shape += [jax.ShapeDtypeStruct((d, s // d, 512), bf16), jax.ShapeDtypeStruct((d, s // d, 128), f32),
                      jax.ShapeDtypeStruct((d, s // d, 128), f32)]
    return _pcall(
        body, name=name, grid=(s // tm,),
        in_specs=[pl.BlockSpec((tm, 512), lambda i: (i, 0)), pl.BlockSpec((tm, 128), lambda i: (i, 0)),
                  pl.BlockSpec((tm, 128), lambda i: (i, 0))],
        out_specs=out_specs, out_shape=out_shape, scratch_shapes=[pltpu.VMEM((6, tm, 128), f32)],
        compiler_params=_params("parallel"),
    )(do, lse, delta)


def b_from_strided(grads, *, name):
    s = grads[0][0].shape[1]
    tm = _row_tile(s)

    def body(*refs):
        ins, o_ref, scr = refs[:9], refs[9], refs[10]
        nat = [jnp.concatenate([ins[q][0].astype(f32) for q in range(3)], axis=1)]
        for gi, d in enumerate(DILATED, start=1):
            blocks = [jnp.concatenate([ins[3 * gi + q][r].astype(f32) for q in range(3)], axis=1) for r in range(d)]
            nat.append(_weave(scr, blocks, d))
        pieces = [nat[g][:, lo:hi] for lo, hi in ((0, 512), (512, 640), (640, 768)) for g in range(3)]
        o_ref[...] = jnp.concatenate(pieces, axis=1).astype(bf16)

    dils = (1,) + DILATED
    in_specs = [_res_spec(d, tm, w) for d in dils for w in (512, 128, 128)]
    return _pcall(
        body, name=name, grid=(s * 1 // tm,), in_specs=in_specs, out_specs=pl.BlockSpec((tm, B_QKV), lambda i: (i, 0)),
        out_shape=jax.ShapeDtypeStruct((s, B_QKV), bf16), scratch_shapes=[pltpu.VMEM((6, tm, 128), f32)],
        compiler_params=_params("parallel"),
    )(*[a for g in grads for a in g])


def attn_merge(os_, lses, *, name):
    s = os_[0].shape[1]
    tm = _row_tile(s)
    ind_t = jnp.asarray(_head_indicator(8).T, dtype=bf16)
    dils = (1,) + DILATED

    def body(o0, o1, o2, l0, l1, l2, e_ref, o_ref, lse_ref, scr):
        both = [jnp.concatenate([o0[0], l0[0]], axis=1)]
        for og, lg, d in ((o1, l1, dils[1]), (o2, l2, dils[2])):
            both.append(_weave(scr, [jnp.concatenate([og[r], lg[r]], axis=1) for r in range(d)], d))
        ls = [b[:, 512:640] for b in both]
        m = jnp.maximum(jnp.maximum(ls[0], ls[1]), ls[2])
        tot = m + jnp.log(jnp.exp(ls[0] - m) + jnp.exp(ls[1] - m) + jnp.exp(ls[2] - m))
        lse_ref[...] = tot
        acc = jnp.zeros((tm, B_OUT), f32)
        for b, lg in zip(both, ls):
            acc = acc + _dot_split(jnp.exp(lg - tot), e_ref[...]) * b[:, :512]
        o_ref[...] = acc.astype(bf16)

    return _pcall(
        body, name=name, grid=(s * 1 // tm,),
        in_specs=[_res_spec(d, tm, 512) for d in dils] + [_res_spec(d, tm, 128) for d in dils]
        + [pl.BlockSpec((128, B_OUT), lambda i: (0, 0))],
        out_specs=[pl.BlockSpec((tm, B_OUT), lambda i: (i, 0)), pl.BlockSpec((tm, 128), lambda i: (i, 0))],
        out_shape=[jax.ShapeDtypeStruct((s, B_OUT), bf16), jax.ShapeDtypeStruct((s, 128), f32)],
        scratch_shapes=[pltpu.VMEM((5, tm, 128), f32)], compiler_params=_params("parallel"),
    )(*os_, *lses, ind_t)


def ada_mod(c_all, w, b, *, name):
    n = w.shape[2]

    def body(c_ref, w_ref, b_ref, o_ref):
        cv = c_ref[...]
        cond = cv * jax.nn.sigmoid(cv)
        o_ref[0] = jnp.dot(cond, w_ref[0], preferred_element_type=f32, precision=lax.Precision.HIGHEST) + b_ref[0]

    return _pcall(
        body, name=name, grid=(DEPTH,),
        in_specs=[pl.BlockSpec((N_DEV, D), lambda i: (0, 0)), pl.BlockSpec((1, D, n), lambda i: (i, 0, 0)),
                  pl.BlockSpec((1, 1, n), lambda i: (i, 0, 0))],
        out_specs=pl.BlockSpec((1, N_DEV, n), lambda i: (i, 0, 0)),
        out_shape=jax.ShapeDtypeStruct((DEPTH, N_DEV, n), f32), compiler_params=_params("arbitrary"),
    )(c_all, w, b)


def ada_grad(c_t, dm, *, name):
    n = dm.shape[2]

    def body(c_ref, dm_ref, o_ref):
        cv = c_ref[...]
        cond = cv * jax.nn.sigmoid(cv)
        acc = cond[:, 0:1] * dm_ref[0, 0:1, :]
        for b in range(1, N_DEV):
            acc = acc + cond[:, b:b + 1] * dm_ref[0, b:b + 1, :]
        o_ref[0] = acc

    return _pcall(
        body, name=name, grid=(DEPTH,),
        in_specs=[pl.BlockSpec((D, N_DEV), lambda i: (0, 0)), pl.BlockSpec((1, N_DEV, n), lambda i: (i, 0, 0))],
        out_specs=pl.BlockSpec((1, D, n), lambda i: (i, 0, 0)),
        out_shape=jax.ShapeDtypeStruct((DEPTH, D, n), f32), compiler_params=_params("arbitrary"),
    )(c_t, dm)


def _adam_math(w, g, m, v):
    m2 = B1 * m + (1.0 - B1) * g
    v2 = B2 * v + (1.0 - B2) * (g * g)
    mh = m2 / (1.0 - B1 ** STEP)
    vh = v2 / (1.0 - B2 ** STEP)
    return -LR * (mh / (jnp.sqrt(vh) + ADAM_EPS) + WD * w), m2, v2


def adamw(w, m, v, g, *, name):
    r, c = w.shape
    tr = 256 if r % 256 == 0 else r

    def body(w_ref, m_ref, v_ref, g_ref, d_ref, m2_ref, v2_ref):
        d_ref[...], m2_ref[...], v2_ref[...] = _adam_math(w_ref[...], g_ref[...], m_ref[...], v_ref[...])

    spec = pl.BlockSpec((tr, c), lambda i: (i, 0))
    return _pcall(
        body, name=name, grid=(r // tr,), in_specs=[spec] * 4, out_specs=[spec] * 3,
        out_shape=[jax.ShapeDtypeStruct((r, c), f32)] * 3, compiler_params=_params("parallel"),
    )(w, m, v, g)


def adamw_parts(w, m, v, own, sib, layer, prev=None, *, name):
    c = w.shape[1]
    r = own.shape[1]
    tr = 256 if r % 256 == 0 else r // 2
    off = layer * (r // tr)

    def body(w_ref, m_ref, v_ref, own_ref, sib_ref, *rest):
        g_ref, d_ref, m2_ref, v2_ref = rest[-4:]

        def total(ref):
            return ((ref[0].astype(f32) + ref[1].astype(f32)) + ref[2].astype(f32)) + ref[3].astype(f32)

        g = total(own_ref) + total(sib_ref)
        g_ref[...] = g
        d_ref[...], m2_ref[...], v2_ref[...] = _adam_math(w_ref[...], g, m_ref[...], v_ref[...])

    spec = pl.BlockSpec((tr, c), lambda i: (off + i, 0))
    pspec = pl.BlockSpec((4, tr, c), lambda i: (0, i, 0))
    prev = () if prev is None else tuple(prev)
    return _pcall(
        body, name=name, grid=(r // tr,), in_specs=[spec] * 3 + [pspec] * 2 + [ANY] * len(prev), out_specs=[spec] * 4,
        out_shape=[jax.ShapeDtypeStruct(w.shape, f32)] * 4,
        input_output_aliases={5 + q: q for q in range(len(prev))}, compiler_params=_params("parallel"),
    )(w, m, v, own, sib, *prev)


def sum_devices(g, *, name):
    _, r, c = g.shape

    def body(g_ref, o_ref):
        acc = g_ref[0]
        for k in range(1, N_DEV):
            acc = acc + g_ref[k]
        o_ref[...] = acc

    return _pcall(body, name=name, out_shape=jax.ShapeDtypeStruct((r, c), f32))(g)


def _place():
    x, y, c = lax.axis_index("x"), lax.axis_index("y"), lax.axis_index("c")
    chips = [(1 - x, y), (x, 1 - y), (1 - x, 1 - y)]
    return x, y, c, chips


def allgather8(v, *, name):
    r, c_ = v.shape

    def body(v_ref, o_ref, send_sems, recv_sems, local_sem):
        x, y, c, _ = _place()
        me = 4 * x + 2 * y + c
        mine = pltpu.make_async_copy(v_ref, o_ref.at[me], local_sem)
        mine.start()
        flips = [(fx, fy, fc) for fx in (0, 1) for fy in (0, 1) for fc in (0, 1)][1:]

        def peer(f):
            return (x ^ f[0], y ^ f[1], c ^ f[2])

        def copy(k, slot, to):
            return pltpu.make_async_remote_copy(
                src_ref=v_ref, dst_ref=o_ref.at[slot], send_sem=send_sems.at[k], recv_sem=recv_sems.at[k],
                device_id=to, device_id_type=MESH)

        sends = [copy(k, me, peer(f)) for k, f in enumerate(flips)]
        for cp in sends:
            cp.start()
        for k, f in enumerate(flips):
            px, py, pc = peer(f)
            copy(k, 4 * px + 2 * py + pc, (x, y, c)).wait_recv()
        for cp in sends:
            cp.wait_send()
        mine.wait()

    return _pcall(
        body, name=name, in_specs=[ANY], out_specs=ANY, out_shape=jax.ShapeDtypeStruct((N_DEV, r, c_), v.dtype),
        scratch_shapes=[pltpu.SemaphoreType.DMA((7,)), pltpu.SemaphoreType.DMA((7,)), pltpu.SemaphoreType.DMA],
    )(v)


def gather_weights(shards, *, name):
    n = len(shards)

    def body(*refs):
        src, out = refs[:n], refs[n:2 * n]
        send_a, recv_a, send_f, recv_f, local_sems = refs[2 * n:]
        x, y, c, chips = _place()
        sib = (x, y, 1 - c)
        me = 2 * x + y
        locals_ = [pltpu.make_async_copy(src[a], out[a].at[me], local_sems.at[a]) for a in range(n)]
        for cp in locals_:
            cp.start()

        def half(a, which):
            rh = src[a].shape[0] // 2
            return pl.ds(which * rh, rh)

        def first(a, k, chip_from, to):
            slot = 2 * chip_from[0] + chip_from[1]
            s_ref = src[a].at[half(a, c)]
            return pltpu.make_async_remote_copy(
                src_ref=s_ref, dst_ref=out[a].at[slot, half(a, c)], send_sem=send_a.at[3 * a + k],
                recv_sem=recv_a.at[3 * a + k], device_id=to, device_id_type=MESH)

        def passed(a, k, chip_from, which, to):
            slot = 2 * chip_from[0] + chip_from[1]
            ref = out[a].at[slot, half(a, which)]
            return pltpu.make_async_remote_copy(
                src_ref=ref, dst_ref=ref, send_sem=send_f.at[3 * a + k], recv_sem=recv_f.at[3 * a + k],
                device_id=to, device_id_type=MESH)

        sends = [first(a, k, (x, y), (*chip, c)) for a in range(n) for k, chip in enumerate(chips)]
        for cp in sends:
            cp.start()
        fwd = []
        for a in range(n):
            for k, chip in enumerate(chips):
                first(a, k, chip, (x, y, c)).wait_recv()
                cp = passed(a, k, chip, c, sib)
                cp.start()
                fwd.append(cp)
        for a in range(n):
            for k, chip in enumerate(chips):
                passed(a, k, chip, 1 - c, (x, y, c)).wait_recv()
        for cp in sends + fwd:
            cp.wait_send()
        for cp in locals_:
            cp.wait()

    return _pcall(
        body, name=name, in_specs=[ANY] * n, out_specs=[ANY] * n,
        out_shape=[jax.ShapeDtypeStruct((4,) + tuple(sh.shape), sh.dtype) for sh in shards],
        scratch_shapes=[pltpu.SemaphoreType.DMA((3 * n,)) for _ in range(4)] + [pltpu.SemaphoreType.DMA((n,))],
    )(*shards)


def _gather_direct(src, out, send_sems, recv_sems, local_sems):
    n = len(src)
    x, y, c, chips = _place()
    me = 2 * x + y

    def copy(a, k, slot, to):
        return pltpu.make_async_remote_copy(
            src_ref=src[a], dst_ref=out[a].at[slot], send_sem=send_sems.at[3 * a + k], recv_sem=recv_sems.at[3 * a + k],
            device_id=to, device_id_type=MESH)

    def start():
        for a in range(n):
            pltpu.make_async_copy(src[a], out[a].at[me], local_sems.at[a]).start()
            for k, chip in enumerate(chips):
                copy(a, k, me, (*chip, c)).start()

    def finish():
        for a in range(n):
            for k, chip in enumerate(chips):
                copy(a, k, 2 * chip[0] + chip[1], (x, y, c)).wait_recv()
        for a in range(n):
            for k in range(3):
                copy(a, k, me, (x, y, c)).wait_send()
            pltpu.make_async_copy(src[a], out[a].at[me], local_sems.at[a]).wait()

    return start, finish


def _gather_scratch(n):
    return [pltpu.SemaphoreType.DMA((3 * n,)), pltpu.SemaphoreType.DMA((3 * n,)), pltpu.SemaphoreType.DMA((n,))]


def _exchange(src, own, sibo, send_sems, recv_sems, local_sems):
    n = len(src)
    x, y, c, chips = _place()
    sib = (x, y, 1 - c)
    me = 2 * x + y

    def slot(chip):
        return 2 * chip[0] + chip[1]

    def copy(a, k, s_ref, d_ref, to):
        return pltpu.make_async_remote_copy(
            src_ref=s_ref, dst_ref=d_ref, send_sem=send_sems.at[7 * a + k], recv_sem=recv_sems.at[7 * a + k],
            device_id=to, device_id_type=MESH)

    def start():
        for a in range(n):
            pltpu.make_async_copy(src[a].at[me], own[a].at[me], local_sems.at[a]).start()
            copy(a, 0, src[a].at[me], sibo[a].at[me], sib).start()
            for k, chip in enumerate(chips):
                copy(a, 1 + k, src[a].at[slot(chip)], own[a].at[me], (*chip, c)).start()

    def finish():
        for a in range(n):
            for k, chip in enumerate(chips):
                copy(a, 1 + k, src[a].at[me], own[a].at[slot(chip)], (x, y, c)).wait_recv()
                copy(a, 4 + k, own[a].at[slot(chip)], sibo[a].at[slot(chip)], sib).start()
        for a in range(n):
            copy(a, 0, src[a].at[me], sibo[a].at[me], (x, y, c)).wait_recv()
            for k, chip in enumerate(chips):
                copy(a, 4 + k, src[a].at[me], sibo[a].at[slot(chip)], (x, y, c)).wait_recv()
        for a in range(n):
            for k in range(7):
                copy(a, k, src[a].at[me], own[a].at[me], (x, y, c)).wait_send()
            pltpu.make_async_copy(src[a].at[me], own[a].at[me], local_sems.at[a]).wait()

    return start, finish


def _exchange_scratch(n):
    return [pltpu.SemaphoreType.DMA((7 * n,)), pltpu.SemaphoreType.DMA((7 * n,)), pltpu.SemaphoreType.DMA((n,))]


def exchange_grads(parts, *, name):
    n = len(parts)

    def body(*refs):
        start, finish = _exchange(refs[:n], refs[n:2 * n], refs[2 * n:3 * n], *refs[3 * n:])
        start()
        finish()

    shapes = [jax.ShapeDtypeStruct(p.shape, p.dtype) for p in parts]
    res = _pcall(body, name=name, in_specs=[ANY] * n, out_specs=[ANY] * (2 * n), out_shape=shapes + shapes,
                 scratch_shapes=_exchange_scratch(n))(*parts)
    return res[:n], res[n:]


def _natural(g, how):
    if how == "col":
        return jnp.moveaxis(g, 0, 1).reshape(g.shape[1], 4 * g.shape[2])
    return g.reshape(4 * g.shape[1], g.shape[2])


def _chunks(gw, how):
    k, n = gw.shape
    if how == "col":
        return jnp.moveaxis(gw.reshape(k, 4, n // 4), 1, 0).astype(bf16)
    return gw.reshape(4, k // 4, n).astype(bf16)


def kernel(x, c, ada_w, ada_b, norm_mix, norm_ffn, ffn_w_in, ffn_w_out, a_w_in, a_w_out, a_sink, b_w_in, b_w_out, final_norm, loss_target, m_ada_w, m_ada_b, m_norm_mix, m_norm_ffn, m_ffn_w_in, m_ffn_w_out, m_a_w_in, m_a_w_out, m_a_sink, m_b_w_in, m_b_w_out, m_final_norm, v_ada_w, v_ada_b, v_norm_mix, v_norm_ffn, v_ffn_w_in, v_ffn_w_out, v_a_w_in, v_a_w_out, v_a_sink, v_b_w_in, v_b_w_out, v_final_norm):
    s = x.shape[1]
    xi, yi, ci = lax.axis_index("x"), lax.axis_index("y"), lax.axis_index("c")
    chip = 2 * xi + yi
    dev = 2 * chip + ci
    x0 = x[0]
    tgt = loss_target[0]

    big = {"ffn_w_in": (ffn_w_in, "col"), "ffn_w_out": (ffn_w_out, "row"), "a_w_in": (a_w_in, "col"),
           "a_w_out": (a_w_out, "row"), "b_w_in": (b_w_in, "col"), "b_w_out": (b_w_out, "col")}
    names = list(big)

    def layer_keys(i):
        mix = "a" if i % 2 == 0 else "b"
        return [("ffn_w_in", i), ("ffn_w_out", i), (mix + "_w_in", i // 2), (mix + "_w_out", i // 2)]

    def shards_of(i):
        return [big[k][0][l].astype(bf16) for k, l in layer_keys(i)]

    def weights_of(i, gathered):
        return {k: (g if k == "ffn_w_in" else _natural(g, big[k][1])) for (k, _), g in zip(layer_keys(i), gathered)}

    wl = [weights_of(0, gather_weights(shards_of(0), name="gather_weights"))]

    c_all = allgather8(jnp.broadcast_to(c, (8, D)), name="gather_c")[:, 0, :]
    nsh = ada_w.shape[2]
    ada_b_sh = lax.dynamic_slice_in_dim(ada_b, chip * nsh, nsh, axis=1)[:, None, :]
    mod_part = ada_mod(c_all, ada_w, ada_b_sh, name="ada_mod")
    mod_all = allgather8(mod_part.reshape(DEPTH * N_DEV, nsh), name="gather_mod")
    mod_all = mod_all.reshape(4, 2, DEPTH, N_DEV, nsh)[:, 0]
    mod = lax.dynamic_index_in_dim(mod_all, dev, axis=2, keepdims=False)
    mod = jnp.moveaxis(mod, 0, 1).reshape(DEPTH, 6, 1, D)

    cfg_a = _Attn(s, mixer="a")
    cfg_b = [_Attn(s, mixer="b", group=g) for g in range(3)]
    no_sink = jnp.full((1, 2, GQ * 64, 1), NEG, f32)

    saved = []
    xc = x0
    for i in range(DEPTH):
        j = i // 2
        sh1, sc1, g1, sh2, sc2, g2 = (mod[i, q] for q in range(6))
        nmix, nffn = norm_mix[i][None, :], norm_ffn[i][None, :]
        mix = "a" if i % 2 == 0 else "b"
        if i % 2 == 0:
            h, qkv = mm_norm(xc, nmix, sc1, sh1, wl[i]["a_w_in"], name="a_qkv")
            sinkcol = jnp.repeat(a_sink[j].reshape(2, 2, GQ), 128, axis=2)[..., None]
            o, lse = (t[0] for t in attn_fwd(qkv[None], sinkcol, cfg_a, out_dtype=bf16, name="a_attn_fwd"))
        else:
            h, qkv = mm_norm(xc, nmix, sc1, sh1, wl[i]["b_w_in"], name="b_qkv")
            qkv = [qkv[None]] + list(b_to_strided(qkv, name="b_to_strided"))
            outs = [attn_fwd(qkv[g], no_sink, cfg_b[g], out_dtype=f32, name=f"b_attn_fwd{g}") for g in range(3)]
            o, lse = attn_merge([t[0] for t in outs], [t[1] for t in outs], name="b_merge")
        x1 = mm_resid(o, wl[i][mix + "_w_out"], xc, g1, name=mix + "_out")
        nxt = shards_of(i + 1) if i + 1 < DEPTH else []
        h2, gu, act, x2, *got = ffn_fwd(x1, nffn, sc2, sh2, g2, wl[i]["ffn_w_in"], wl[i]["ffn_w_out"], nxt,
                                        name="ffn_fwd_gather" if nxt else "ffn_fwd")
        if nxt:
            wl.append(weights_of(i + 1, got))
        saved.append((xc, h, qkv, o, lse, x1, h2, gu, act))
        xc = x2

    dx, st_final = loss_head(xc, final_norm[None, :], tgt, name="loss_head")

    zero_row = jnp.zeros((1, D), f32)
    dmod_rows = [None] * DEPTH
    d_nmix, d_nffn = [None] * DEPTH, [None] * DEPTH
    d_sink = [None] * 2
    parts, exchanged = None, {}
    for i in reversed(range(DEPTH)):
        j = i // 2
        xin, h, qkv, o, lse, x1, h2, gu, act = saved[i]
        sh1, sc1, g1, sh2, sc2, g2 = (mod[i, q] for q in range(6))
        nmix, nffn = norm_mix[i][None, :], norm_ffn[i][None, :]
        mix = "a" if i % 2 == 0 else "b"
        w_fo, w_o, w_i = wl[i]["ffn_w_out"], wl[i][mix + "_w_out"], wl[i][mix + "_w_in"]
        dgu, dx1, st2, own_, sib_ = ffn_bwd_rows(dx, x1, gu, g2, nffn, sc2, wl[i]["ffn_w_in"], w_fo, parts or [],
                                                 name="ffn_bwd_rows_exchange" if parts else "ffn_bwd_rows")
        if parts:
            exchanged[i + 1] = (own_, sib_)
        gwo, dg2 = ffn_dw_out(act, dx, g2, w_fo, name="ffn_dw_out")
        dg2 = dg2[0:1]
        gwi = ffn_dw_in(h2, dgu, name="ffn_dw_in")
        do = mm_nt_scaled(dx1, g1, w_o, name=mix + "_do")
        gmo, dg1 = mm_tn(o, dx1, (g1, w_o), name=mix + "_dw_out")
        if i % 2 == 0:
            sinkrow = jnp.pad(a_sink[j].reshape(2, 8), ((0, 0), (0, 120))).reshape(1, 256)
            delta, dsk = attn_delta(do, o, lse, sinkrow, name="a_delta")
            d_sink[j] = dsk[0].reshape(2, 128)[:, :8].reshape(16)
            dq, dk, dv = attn_bwd(qkv[None], do[None], lse[None], delta[None], cfg_a, name="a_attn_bwd")
            dqkv = jnp.concatenate([dq[0], dk[0], dv[0]], axis=1)
        else:
            delta, _ = attn_delta(do, o, lse, jnp.zeros((1, 128), f32), name="b_delta")
            st = [do[None], lse[None], delta[None]] + list(b_bwd_to_strided(do, lse, delta, name="b_bwd_to_strided"))
            gr = [attn_bwd(qkv[g], *st[3 * g:3 * g + 3], cfg_b[g], name=f"b_attn_bwd{g}") for g in range(3)]
            dqkv = b_from_strided(gr, name="b_from_strided")
        gmi = mm_tn(h, dqkv, name=mix + "_dw_in")
        dx, st1 = mm_nt_norm_bwd(dqkv, w_i, xin, dx1, nmix, sc1, name=mix + "_dh")
        dmod_rows[i] = jnp.concatenate([st1[2:3], st1[1:2], dg1, st2[2:3], st2[1:2], dg2], axis=0)
        d_nmix[i], d_nffn[i] = st1[0:1], st2[0:1]
        parts = [gwi, gwo.reshape(4, F // 4, D), _chunks(gmi, big[mix + "_w_in"][1]), _chunks(gmo, big[mix + "_w_out"][1])]
    exchanged[0] = exchange_grads(parts, name="exchange_grads")

    sink_row = jnp.pad(jnp.concatenate(d_sink), (0, D - 32))[None, :]
    stats = jnp.concatenate(dmod_rows + d_nmix + d_nffn + [sink_row, st_final[0:1], st_final[1:2]]
                            + [zero_row] * (STAT_ROWS - 35), axis=0)
    stats_all = allgather8(stats, name="gather_stats")
    tot = sum_devices(stats_all, name="sum_stats")
    loss = 0.5 * jnp.sum(tot[34]) / float(D)

    def pack(ab, nm, nf, sk, fnm, fill):
        return jnp.concatenate([ab.reshape(24, D), nm, nf, jnp.pad(sk.reshape(1, 32), ((0, 0), (0, D - 32)), constant_values=fill),
                                fnm[None, :], jnp.full((STAT_ROWS - 34, D), fill, f32)], axis=0)

    sd, sm, sv = adamw(pack(ada_b, norm_mix, norm_ffn, a_sink, final_norm, 0.0),
                       pack(m_ada_b, m_norm_mix, m_norm_ffn, m_a_sink, m_final_norm, 0.0),
                       pack(v_ada_b, v_norm_mix, v_norm_ffn, v_a_sink, v_final_norm, 1.0), tot, name="adamw_small")

    def unpack(p):
        return p[0:24].reshape(DEPTH, 6 * D), p[24:28], p[28:32], p[32, :32].reshape(2, 16), p[33]

    small = {"grad": unpack(tot), "delta": unpack(sd), "m": unpack(sm), "v": unpack(sv)}

    dmod_all = stats_all[:, 0:24, :].reshape(N_DEV, DEPTH, 6 * D)
    dm_sh = jnp.moveaxis(lax.dynamic_slice_in_dim(dmod_all, chip * nsh, nsh, axis=2), 0, 1)
    g_ada = ada_grad(c_all.T, dm_sh, name="ada_grad")
    r_ada = (DEPTH * D, nsh)
    ada_res = adamw(ada_w.reshape(r_ada), m_ada_w.reshape(r_ada), v_ada_w.reshape(r_ada), g_ada.reshape(r_ada), name="adamw_ada")
    ada_out = [g_ada] + [t.reshape(ada_w.shape) for t in ada_res]

    mom = {"ffn_w_in": (m_ffn_w_in, v_ffn_w_in), "ffn_w_out": (m_ffn_w_out, v_ffn_w_out), "a_w_in": (m_a_w_in, v_a_w_in),
           "a_w_out": (m_a_w_out, v_a_w_out), "b_w_in": (m_b_w_in, v_b_w_in), "b_w_out": (m_b_w_out, v_b_w_out)}
    big_out = {k: None for k in names}
    for i in reversed(range(DEPTH)):
        own_, sib_ = exchanged[i]
        for (k, l), o_, s_ in zip(layer_keys(i), own_, sib_):
            w = big[k][0]
            r2 = (-1, w.shape[-1])
            big_out[k] = adamw_parts(w.reshape(r2), mom[k][0].reshape(r2), mom[k][1].reshape(r2), o_, s_, l, big_out[k],
                                     name=f"adamw_{k}{l}")
    big_out = {k: [t.reshape(big[k][0].shape) for t in big_out[k]] for k in names}

    def leaves(q):
        sm_ = small[("grad", "delta", "m", "v")[q]]
        return (ada_out[q], sm_[0], sm_[1], sm_[2], big_out["ffn_w_in"][q], big_out["ffn_w_out"][q], big_out["a_w_in"][q],
                big_out["a_w_out"][q], sm_[3], big_out["b_w_in"][q], big_out["b_w_out"][q], sm_[4])

    return (loss, dx[None], *leaves(0), *leaves(1), *leaves(2), *leaves(3))
```

```python
import functools
import math

import numpy as np
import jax
import jax.numpy as jnp
from jax import lax
from jax.experimental import pallas as pl
from jax.experimental.pallas import tpu as pltpu

f32 = jnp.float32
bf16 = jnp.bfloat16

D = 1024
DH = 64
GQ = 4
DEPTH = 4
F = 2816
A_QKV, A_OUT = 1536, 1024
B_QKV, B_OUT = 2304, 512
B_GROUPS = ((128, 1), (512, 4), (2048, 16))
RMS_EPS = 1e-6
NEG = -1e30
LR, B1, B2, ADAM_EPS, WD, STEP = 0.001, 0.9, 0.999, 1e-08, 0.01, 10
N_DEV = 8
STAT_ROWS = 40
MESH = pl.DeviceIdType.MESH
ANY = pl.BlockSpec(memory_space=pl.ANY)


def _pcall(body, **kw):
    return pl.pallas_call(body, **kw)


def _params(*sem):
    return pltpu.CompilerParams(dimension_semantics=sem, vmem_limit_bytes=56 * 1024 * 1024)


def _row_tile(s, want=1024):
    return want if s % want == 0 else s


ROW_CHUNKS = 4


def mm_norm(x, nw, sc, sh, w, *, name):
    s, d = x.shape
    n = w.shape[1]
    tm = _row_tile(s)
    rc = tm // ROW_CHUNKS

    def body(x_ref, nw_ref, sc_ref, sh_ref, w_ref, h_ref, y_ref):
        hs = []
        for c in range(ROW_CHUNKS):
            xv = x_ref[c * rc:(c + 1) * rc, :]
            r = lax.rsqrt(jnp.mean(xv * xv, axis=-1, keepdims=True) + RMS_EPS)
            hs.append(((xv * r * nw_ref[...]) * (1.0 + sc_ref[...]) + sh_ref[...]).astype(bf16))
        ys = [jnp.dot(h, w_ref[...], preferred_element_type=f32) for h in hs]
        for c in range(ROW_CHUNKS):
            h_ref[c * rc:(c + 1) * rc, :] = hs[c]
            y_ref[c * rc:(c + 1) * rc, :] = ys[c].astype(bf16)

    vec = pl.BlockSpec((1, d), lambda i: (0, 0))
    return _pcall(
        body, name=name, grid=(s // tm,),
        in_specs=[pl.BlockSpec((tm, d), lambda i: (i, 0)), vec, vec, vec, pl.BlockSpec((d, n), lambda i: (0, 0))],
        out_specs=[pl.BlockSpec((tm, d), lambda i: (i, 0)), pl.BlockSpec((tm, n), lambda i: (i, 0))],
        out_shape=[jax.ShapeDtypeStruct((s, d), bf16), jax.ShapeDtypeStruct((s, n), bf16)],
        compiler_params=_params("parallel"),
    )(x, nw, sc, sh, w)


def mm_resid(a, w, xres, g, *, name):
    s, k = a.shape
    n = w.shape[1]
    tm = _row_tile(s)
    rc = tm // ROW_CHUNKS

    def body(a_ref, w_ref, x_ref, g_ref, o_ref):
        ys = [jnp.dot(a_ref[c * rc:(c + 1) * rc, :], w_ref[...], preferred_element_type=f32) for c in range(ROW_CHUNKS)]
        for c, y in enumerate(ys):
            o_ref[c * rc:(c + 1) * rc, :] = x_ref[c * rc:(c + 1) * rc, :] + g_ref[...] * y

    big = pl.BlockSpec((tm, n), lambda i: (i, 0))
    return _pcall(
        body, name=name, grid=(s // tm,),
        in_specs=[pl.BlockSpec((tm, k), lambda i: (i, 0)), pl.BlockSpec((k, n), lambda i: (0, 0)), big,
                  pl.BlockSpec((1, n), lambda i: (0, 0))],
        out_specs=big, out_shape=jax.ShapeDtypeStruct((s, n), f32), compiler_params=_params("parallel"),
    )(a, w, xres, g)


def mm_nt_delta(dx, g, w, o, lse, sinkrow, *, name):
    s, d = dx.shape
    n = w.shape[0]
    wd = lse.shape[1]
    tm = _row_tile(s)
    rc = tm // ROW_CHUNKS
    ind = jnp.asarray(_head_indicator(n // DH), dtype=bf16)

    def body(dx_ref, g_ref, w_ref, o_ref, lse_ref, sink_ref, e_ref, do_ref, dl_ref, ds_ref):
        rows = [slice(c * rc, (c + 1) * rc) for c in range(ROW_CHUNKS)]
        as_ = [(dx_ref[rw, :] * g_ref[...]).astype(bf16) for rw in rows]
        dos = [lax.dot_general(a, w_ref[...], (((1,), (1,)), ((), ())), preferred_element_type=f32).astype(bf16) for a in as_]
        dls = [_dot_split(do.astype(f32) * o_ref[rw, :].astype(f32), e_ref[...]) for do, rw in zip(dos, rows)]
        part = None
        for rw, do, dl in zip(rows, dos, dls):
            do_ref[rw, :] = do
            dl_ref[rw, :] = dl
            p = -jnp.sum(jnp.exp(sink_ref[...] - lse_ref[rw, :]) * dl, axis=0, keepdims=True)
            part = p if part is None else part + p
        part = jnp.concatenate([part, jnp.zeros((7, wd), f32)], axis=0)

        @pl.when(pl.program_id(0) == 0)
        def _():
            ds_ref[...] = part

        @pl.when(pl.program_id(0) != 0)
        def _():
            ds_ref[...] += part

    return _pcall(
        body, name=name, grid=(s // tm,),
        in_specs=[pl.BlockSpec((tm, d), lambda i: (i, 0)), pl.BlockSpec((1, d), lambda i: (0, 0)),
                  pl.BlockSpec((n, d), lambda i: (0, 0)), pl.BlockSpec((tm, n), lambda i: (i, 0)),
                  pl.BlockSpec((tm, wd), lambda i: (i, 0)), pl.BlockSpec((1, wd), lambda i: (0, 0)),
                  pl.BlockSpec((n, wd), lambda i: (0, 0))],
        out_specs=[pl.BlockSpec((tm, n), lambda i: (i, 0)), pl.BlockSpec((tm, wd), lambda i: (i, 0)),
                   pl.BlockSpec((8, wd), lambda i: (0, 0))],
        out_shape=[jax.ShapeDtypeStruct((s, n), bf16), jax.ShapeDtypeStruct((s, wd), f32), jax.ShapeDtypeStruct((8, wd), f32)],
        compiler_params=_params("arbitrary"),
    )(dx, g, w, o, lse, sinkrow, ind)


def mm_nt_norm_bwd(a, w, x, dres, nw, sc, *, name):
    s, k = a.shape
    d = w.shape[0]
    tm = _row_tile(s)
    rc = tm // ROW_CHUNKS

    def body(a_ref, w_ref, x_ref, dr_ref, nw_ref, sc_ref, o_ref, st_ref):
        dhs = [lax.dot_general(a_ref[c * rc:(c + 1) * rc, :], w_ref[...], (((1,), (1,)), ((), ())), preferred_element_type=f32)
               for c in range(ROW_CHUNKS)]
        rows = None
        for c, dh in enumerate(dhs):
            xv = x_ref[c * rc:(c + 1) * rc, :]
            r = lax.rsqrt(jnp.mean(xv * xv, axis=-1, keepdims=True) + RMS_EPS)
            xh = xv * r
            dn = dh * (1.0 + sc_ref[...])
            dxh = dn * nw_ref[...]
            o_ref[c * rc:(c + 1) * rc, :] = dr_ref[c * rc:(c + 1) * rc, :] + r * (dxh - xh * jnp.mean(dxh * xh, axis=-1, keepdims=True))
            part = jnp.concatenate([
                jnp.sum(dn * xh, axis=0, keepdims=True),
                jnp.sum(dh * (xh * nw_ref[...]), axis=0, keepdims=True),
                jnp.sum(dh, axis=0, keepdims=True),
                jnp.zeros((5, d), f32)], axis=0)
            rows = part if rows is None else rows + part

        @pl.when(pl.program_id(0) == 0)
        def _():
            st_ref[...] = rows

        @pl.when(pl.program_id(0) != 0)
        def _():
            st_ref[...] += rows

    big = pl.BlockSpec((tm, d), lambda i: (i, 0))
    vec = pl.BlockSpec((1, d), lambda i: (0, 0))
    return _pcall(
        body, name=name, grid=(s // tm,),
        in_specs=[pl.BlockSpec((tm, k), lambda i: (i, 0)), pl.BlockSpec((d, k), lambda i: (0, 0)), big, big, vec, vec],
        out_specs=[big, pl.BlockSpec((8, d), lambda i: (0, 0))],
        out_shape=[jax.ShapeDtypeStruct((s, d), f32), jax.ShapeDtypeStruct((8, d), f32)],
        compiler_params=_params("arbitrary"),
    )(a, w, x, dres, nw, sc)


def mm_tn(a, b, scale=None, *, name):
    s, ka = a.shape
    nb = b.shape[1]
    ts = _row_tile(s)
    tn = 768 if nb % 768 == 0 and nb % 512 != 0 else 512
    tka = 1408 if ka % 1408 == 0 else min(ka, 1024)
    ns = s // ts

    def body(a_ref, b_ref, *rest):
        o_ref = rest[2] if scale is not None else rest[0]
        si = pl.program_id(2)
        part = lax.dot_general(a_ref[...], b_ref[...].astype(bf16), (((0,), (0,)), ((), ())), preferred_element_type=f32)

        @pl.when(si == 0)
        def _():
            o_ref[...] = part

        @pl.when(si != 0)
        def _():
            o_ref[...] += part

        if scale is not None:
            g_ref, wb_ref, dg_ref = rest[0], rest[1], rest[3]

            @pl.when(si == ns - 1)
            def _():
                gm = o_ref[...]
                dgp = jnp.sum(wb_ref[...].astype(f32) * gm, axis=0, keepdims=True)

                @pl.when(pl.program_id(1) == 0)
                def _():
                    dg_ref[...] = dgp

                @pl.when(pl.program_id(1) != 0)
                def _():
                    dg_ref[...] += dgp

                o_ref[...] = gm * g_ref[...]

    in_specs = [pl.BlockSpec((ts, tka), lambda j, i, k: (k, i)), pl.BlockSpec((ts, tn), lambda j, i, k: (k, j))]
    args = [a, b]
    out_specs = [pl.BlockSpec((tka, tn), lambda j, i, k: (i, j))]
    out_shape = [jax.ShapeDtypeStruct((ka, nb), f32)]
    if scale is not None:
        in_specs += [pl.BlockSpec((1, tn), lambda j, i, k: (0, j)), pl.BlockSpec((tka, tn), lambda j, i, k: (i, j))]
        args += list(scale)
        out_specs.append(pl.BlockSpec((1, tn), lambda j, i, k: (0, j)))
        out_shape.append(jax.ShapeDtypeStruct((1, nb), f32))
    res = _pcall(
        body, name=name, grid=(nb // tn, ka // tka, ns), in_specs=in_specs, out_specs=out_specs, out_shape=out_shape,
        compiler_params=_params("arbitrary", "arbitrary", "arbitrary"),
    )(*args)
    return res if scale is not None else res[0]


def loss_head(x, fn, tgt, *, name):
    s, d = x.shape
    tm = _row_tile(s, 512)

    def body(x_ref, fn_ref, t_ref, dx_ref, st_ref):
        xv = x_ref[...]
        r = lax.rsqrt(jnp.mean(xv * xv, axis=-1, keepdims=True) + RMS_EPS)
        xh = xv * r
        err = xh * fn_ref[...] - t_ref[...]
        dy = err / float(d)
        dxh = dy * fn_ref[...]
        dx_ref[...] = r * (dxh - xh * jnp.mean(dxh * xh, axis=-1, keepdims=True))
        rows = jnp.concatenate([
            jnp.sum(dy * xh, axis=0, keepdims=True),
            jnp.sum(err * err, axis=0, keepdims=True),
            jnp.zeros((6, d), f32)], axis=0)

        @pl.when(pl.program_id(0) == 0)
        def _():
            st_ref[...] = rows

        @pl.when(pl.program_id(0) != 0)
        def _():
            st_ref[...] += rows

    big = pl.BlockSpec((tm, d), lambda i: (i, 0))
    return _pcall(
        body, name=name, grid=(s // tm,), in_specs=[big, pl.BlockSpec((1, d), lambda i: (0, 0)), big],
        out_specs=[big, pl.BlockSpec((8, d), lambda i: (0, 0))],
        out_shape=[jax.ShapeDtypeStruct((s, d), f32), jax.ShapeDtypeStruct((8, d), f32)],
        compiler_params=_params("arbitrary"),
    )(x, fn, tgt)


FC = 2 * F // 4
FFN_ROWS = 256
FFN_CHUNKS = 1


def _resident(pairs, sems):
    @pl.when(pl.program_id(0) == 0)
    def _():
        cps = [pltpu.make_async_copy(h, v, sems.at[i]) for i, (h, v) in enumerate(pairs)]
        for cp in cps:
            cp.start()
        for cp in cps:
            cp.wait()


def ffn_fwd(x, nw, sc, sh, g, w_in, w_out, carry=(), *, name):
    s, d = x.shape
    tm = _row_tile(s, FFN_ROWS)
    nsteps = s // tm
    nc = len(carry)

    def body(*refs):
        x_ref, nw_ref, sc_ref, sh_ref, g_ref, win_hbm, wout_hbm = refs[:7]
        h_ref, gu_ref, a_ref, o_ref = refs[7 + nc:11 + nc]
        win_v, wout_v, sems = refs[11 + 2 * nc:14 + 2 * nc]
        if nc:
            start, finish = _gather_direct(refs[7:7 + nc], refs[11 + nc:11 + 2 * nc], *refs[14 + 2 * nc:])
            pl.when(pl.program_id(0) == 0)(start)
        _resident([(win_hbm, win_v), (wout_hbm, wout_v)], sems)
        rc = tm // FFN_CHUNKS
        rows = [slice(q * rc, (q + 1) * rc) for q in range(FFN_CHUNKS)]
        xs = [x_ref[rw, :] for rw in rows]
        hs = [((xv * lax.rsqrt(jnp.mean(xv * xv, axis=-1, keepdims=True) + RMS_EPS) * nw_ref[...]) * (1.0 + sc_ref[...])
               + sh_ref[...]).astype(bf16) for xv in xs]
        pairs = [(q, c) for q in range(FFN_CHUNKS) for c in range(2)]
        gts = [jnp.dot(hs[q], win_v[c], preferred_element_type=f32) for q, c in pairs]
        ups = [jnp.dot(hs[q], win_v[c + 2], preferred_element_type=f32) for q, c in pairs]
        acts = [(gt * jax.nn.sigmoid(gt) * up).astype(bf16) for gt, up in zip(gts, ups)]
        ys = [jnp.dot(act, wout_v[c * FC:(c + 1) * FC, :], preferred_element_type=f32) for act, (q, c) in zip(acts, pairs)]
        for (q, c), gt, up, act in zip(pairs, gts, ups, acts):
            cs = slice(c * FC, (c + 1) * FC)
            gu_ref[0, rows[q], cs] = gt.astype(bf16)
            gu_ref[1, rows[q], cs] = up.astype(bf16)
            a_ref[rows[q], cs] = act
        for q in range(FFN_CHUNKS):
            h_ref[rows[q], :] = hs[q]
            o_ref[rows[q], :] = xs[q] + g_ref[...] * (ys[2 * q] + ys[2 * q + 1])
        if nc:
            pl.when(pl.program_id(0) == nsteps - 1)(finish)

    big = pl.BlockSpec((tm, d), lambda i: (i, 0))
    vec = pl.BlockSpec((1, d), lambda i: (0, 0))
    return _pcall(
        body, name=name, grid=(nsteps,), in_specs=[big, vec, vec, vec, vec, ANY, ANY] + [ANY] * nc,
        out_specs=[big, pl.BlockSpec((2, tm, F), lambda i: (0, i, 0)), pl.BlockSpec((tm, F), lambda i: (i, 0)), big] + [ANY] * nc,
        out_shape=[jax.ShapeDtypeStruct((s, d), bf16), jax.ShapeDtypeStruct((2, s, F), bf16),
                   jax.ShapeDtypeStruct((s, F), bf16), jax.ShapeDtypeStruct((s, d), f32)]
        + [jax.ShapeDtypeStruct((4,) + tuple(sh_.shape), sh_.dtype) for sh_ in carry],
        scratch_shapes=[pltpu.VMEM((4, d, FC), bf16), pltpu.VMEM((F, d), bf16), pltpu.SemaphoreType.DMA((2,))]
        + (_gather_scratch(nc) if nc else []),
        compiler_params=_params("arbitrary"),
    )(x, nw, sc, sh, g, w_in, w_out, *carry)


def ffn_bwd_rows(dx, x, gu, g, nw, sc, w_in, w_out, carry=(), *, name):
    s, d = x.shape
    tm = _row_tile(s, FFN_ROWS)
    nsteps = s // tm
    nc = len(carry)
    nt_dims = (((1,), (1,)), ((), ()))

    def body(*refs):
        dx_ref, x_ref, gu_ref, g_ref, nw_ref, sc_ref, win_hbm, wout_hbm = refs[:8]
        dgu_ref, o_ref, st_ref = refs[8 + nc:11 + nc]
        win_v, wout_v, sems = refs[11 + 3 * nc:14 + 3 * nc]
        if nc:
            start, finish = _exchange(refs[8:8 + nc], refs[11 + nc:11 + 2 * nc], refs[11 + 2 * nc:11 + 3 * nc],
                                      *refs[14 + 3 * nc:])
            pl.when(pl.program_id(0) == 0)(start)
        _resident([(win_hbm, win_v), (wout_hbm, wout_v)], sems)
        dxv = dx_ref[...]
        a = (dxv * g_ref[...]).astype(bf16)
        dh = None
        for c in range(2):
            cs = slice(c * FC, (c + 1) * FC)
            da = lax.dot_general(a, wout_v[cs, :], nt_dims, preferred_element_type=f32)
            gt = gu_ref[0, :, cs].astype(f32)
            up = gu_ref[1, :, cs].astype(f32)
            sg = jax.nn.sigmoid(gt)
            dgate = (da * up * (sg * (1.0 + gt * (1.0 - sg)))).astype(bf16)
            dup = (da * (gt * sg)).astype(bf16)
            dgu_ref[0, :, cs] = dgate
            dgu_ref[1, :, cs] = dup
            part = (lax.dot_general(dgate, win_v[c], nt_dims, preferred_element_type=f32)
                    + lax.dot_general(dup, win_v[c + 2], nt_dims, preferred_element_type=f32))
            dh = part if dh is None else dh + part
        xv = x_ref[...]
        r = lax.rsqrt(jnp.mean(xv * xv, axis=-1, keepdims=True) + RMS_EPS)
        xh = xv * r
        dn = dh * (1.0 + sc_ref[...])
        dxh = dn * nw_ref[...]
        o_ref[...] = dxv + r * (dxh - xh * jnp.mean(dxh * xh, axis=-1, keepdims=True))
        rows = jnp.concatenate([
            jnp.sum(dn * xh, axis=0, keepdims=True),
            jnp.sum(dh * (xh * nw_ref[...]), axis=0, keepdims=True),
            jnp.sum(dh, axis=0, keepdims=True),
            jnp.zeros((5, d), f32)], axis=0)

        @pl.when(pl.program_id(0) == 0)
        def _():
            st_ref[...] = rows

        @pl.when(pl.program_id(0) != 0)
        def _():
            st_ref[...] += rows

        if nc:
            pl.when(pl.program_id(0) == nsteps - 1)(finish)

    big = pl.BlockSpec((tm, d), lambda i: (i, 0))
    vec = pl.BlockSpec((1, d), lambda i: (0, 0))
    gus = pl.BlockSpec((2, tm, F), lambda i: (0, i, 0))
    cshapes = [jax.ShapeDtypeStruct(p.shape, p.dtype) for p in carry]
    res = _pcall(
        body, name=name, grid=(nsteps,), in_specs=[big, big, gus, vec, vec, vec, ANY, ANY] + [ANY] * nc,
        out_specs=[gus, big, pl.BlockSpec((8, d), lambda i: (0, 0))] + [ANY] * (2 * nc),
        out_shape=[jax.ShapeDtypeStruct((2, s, F), bf16), jax.ShapeDtypeStruct((s, d), f32), jax.ShapeDtypeStruct((8, d), f32)]
        + cshapes + cshapes,
        scratch_shapes=[pltpu.VMEM((4, d, FC), bf16), pltpu.VMEM((F, d), bf16), pltpu.SemaphoreType.DMA((2,))]
        + (_exchange_scratch(nc) if nc else []),
        compiler_params=_params("arbitrary"),
    )(dx, x, gu, g, nw, sc, w_in, w_out, *carry)
    return res[0], res[1], res[2], res[3:3 + nc], res[3 + nc:]


def ffn_dw_in(h, dgu, *, name):
    s, d = h.shape
    ts = _row_tile(s)
    ns = s // ts
    tn_dims = (((0,), (0,)), ((), ()))

    def body(h_ref, dgu_ref, o_ref, acc):
        k = pl.program_id(1)

        @pl.when(k == 0)
        def _():
            acc[...] = jnp.zeros_like(acc)

        hv = h_ref[...]
        for c in range(2):
            acc[c] += lax.dot_general(hv, dgu_ref[:, c * FC:(c + 1) * FC], tn_dims, preferred_element_type=f32)

        @pl.when(k == ns - 1)
        def _():
            o_ref[...] = acc[...].astype(bf16)

    return _pcall(
        body, name=name, grid=(2, ns),
        in_specs=[pl.BlockSpec((ts, d), lambda hf, k: (k, 0)), pl.BlockSpec((None, ts, F), lambda hf, k: (hf, k, 0))],
        out_specs=pl.BlockSpec((2, d, FC), lambda hf, k: (hf, 0, 0)),
        out_shape=jax.ShapeDtypeStruct((4, d, FC), bf16), scratch_shapes=[pltpu.VMEM((2, d, FC), f32)],
        compiler_params=_params("arbitrary", "arbitrary"),
    )(h, dgu)


def ffn_dw_out(a, dx, g, wb, *, name):
    s, fdim = a.shape
    d = dx.shape[1]
    ts = _row_tile(s)
    ns = s // ts
    tn = d // 2
    tn_dims = (((0,), (0,)), ((), ()))

    def body(a_ref, dx_ref, g_ref, wb_ref, o_ref, dg_ref, acc):
        k = pl.program_id(1)

        @pl.when(k == 0)
        def _():
            acc[...] = jnp.zeros_like(acc)

        acc[...] += lax.dot_general(a_ref[...], dx_ref[...].astype(bf16), tn_dims, preferred_element_type=f32)

        @pl.when(k == ns - 1)
        def _():
            gm = acc[...]
            dg_ref[...] = jnp.concatenate([jnp.sum(wb_ref[...].astype(f32) * gm, axis=0, keepdims=True),
                                           jnp.zeros((7, tn), f32)], axis=0)
            o_ref[...] = (gm * g_ref[...]).astype(bf16)

    return _pcall(
        body, name=name, grid=(2, ns),
        in_specs=[pl.BlockSpec((ts, fdim), lambda j, k: (k, 0)), pl.BlockSpec((ts, tn), lambda j, k: (k, j)),
                  pl.BlockSpec((1, tn), lambda j, k: (0, j)), pl.BlockSpec((fdim, tn), lambda j, k: (0, j))],
        out_specs=[pl.BlockSpec((fdim, tn), lambda j, k: (0, j)), pl.BlockSpec((8, tn), lambda j, k: (0, j))],
        out_shape=[jax.ShapeDtypeStruct((fdim, d), bf16), jax.ShapeDtypeStruct((8, d), f32)],
        scratch_shapes=[pltpu.VMEM((fdim, tn), f32)], compiler_params=_params("arbitrary", "arbitrary"),
    )(a, dx, g, wb)


def _alibi(n):
    return np.asarray(2.0 ** (-8.0 * np.arange(1, n + 1) / n), dtype=np.float32)


class _Attn:
    def __init__(self, s, *, mixer, group=0):
        if mixer == "a":
            self.blk, self.dil, self.npairs = 128, 1, 2
            self.qb0, self.kb0, self.vb0 = 0, 8, 10
            slopes = _alibi(16).reshape(2, 2, GQ)
        else:
            window, dil = B_GROUPS[group]
            self.blk, self.dil, self.npairs = window // (2 * dil), dil, 1
            self.qb0, self.kb0, self.vb0 = (0, 12, 15) if dil == 1 else (0, 4, 5)
            slopes = _alibi(24).reshape(3, 1, 2, GQ)[group]
        self.l = s // self.dil
        self.t = min(512, self.l)
        self.nt = self.l // self.t
        self.nb = self.t // self.blk
        blk = self.blk
        qi = np.arange(blk)[:, None]
        rel = np.arange(3 * blk)[None, :] - blk - qi
        dist = (self.dil * np.abs(rel)).astype(np.float32)
        bias = -slopes[:, :, :, None, None] * dist[None, None, None]
        bias = np.where(np.abs(rel) <= blk, bias, np.float32(NEG)).astype(np.float32)
        self.bias = bias.reshape(self.npairs, 2, GQ * blk, 3 * blk)

    def grid(self):
        return (self.dil, self.npairs, self.nt)

    def tile(self, width, col):
        return pl.BlockSpec((None, self.t, width), lambda r, hp, i: (r, i, col(hp)))

    def halo(self, width, col):
        t, blk, nbl = self.t, self.blk, self.l // self.blk
        per = t // blk
        return [
            pl.BlockSpec((None, blk, width), lambda r, hp, i: (r, jnp.maximum(i * per - 1, 0), col(hp))),
            self.tile(width, col),
            pl.BlockSpec((None, blk, width), lambda r, hp, i: (r, jnp.minimum((i + 1) * per, nbl - 1), col(hp))),
        ]

    def qcol(self, e):
        return lambda hp: self.qb0 + 2 * hp + e

    def kcol(self, hp):
        return self.kb0 + hp

    def vcol(self, hp):
        return self.vb0 + hp

    def pcol(self, hp):
        return hp


def _stack_heads(x):
    return jnp.concatenate([x[:, g * DH:(g + 1) * DH] for g in range(GQ)], axis=0)


def _unstack_heads(x, rows):
    return jnp.concatenate([x[g * rows:(g + 1) * rows] for g in range(GQ)], axis=1)


def _head_cols(tile, hh, rows):
    return jnp.concatenate([tile[:, hh * GQ + g:hh * GQ + g + 1] for g in range(GQ)], axis=0)


def _carrying(body, n_in, n_out, n_scratch, carry, kind, grid):
    nc = len(carry)
    if not nc:
        return body, [], [], [], []
    n_res = nc if kind == "gather" else 2 * nc

    def wrapped(*refs):
        ins, src = refs[:n_in], refs[n_in:n_in + nc]
        outs = refs[n_in + nc:n_in + nc + n_out]
        res = refs[n_in + nc + n_out:n_in + nc + n_out + n_res]
        scr = refs[n_in + nc + n_out + n_res:n_in + nc + n_out + n_res + n_scratch]
        sems = refs[n_in + nc + n_out + n_res + n_scratch:]
        if kind == "gather":
            start, finish = _gather_direct(src, res, *sems)
        else:
            start, finish = _exchange(src, res[:nc], res[nc:], *sems)
        ids = [pl.program_id(a) for a in range(len(grid))]
        first = functools.reduce(jnp.logical_and, [i == 0 for i in ids])
        last = functools.reduce(jnp.logical_and, [i == g - 1 for i, g in zip(ids, grid)])
        pl.when(first)(start)
        body(*ins, *outs, *scr)
        pl.when(last)(finish)

    if kind == "gather":
        shapes = [jax.ShapeDtypeStruct((4,) + tuple(c.shape), c.dtype) for c in carry]
        sems = _gather_scratch(nc)
    else:
        shapes = [jax.ShapeDtypeStruct(c.shape, c.dtype) for c in carry] * 2
        sems = _exchange_scratch(nc)
    return wrapped, [ANY] * nc, [ANY] * n_res, shapes, sems


def attn_fwd(qkv, sinkcol, cfg, carry=(), *, out_dtype, name):
    blk, t, nb, nt, dil = cfg.blk, cfg.t, cfg.nb, cfg.nt, cfg.dil
    scale = DH ** -0.5

    def body(q0, q1, kp, km, kn, vp, vm, vn, bias_ref, sink_ref, o_ref, lse_ref, kx, vx):
        ti = pl.program_id(2)
        first, last = ti == 0, ti == nt - 1
        for hh in range(2):
            sl = slice(hh * DH, (hh + 1) * DH)
            for dst, (p_, m_, n_) in ((kx, (kp, km, kn)), (vx, (vp, vm, vn))):
                dst[hh, 0:blk] = p_[:, sl]
                dst[hh, blk:blk + t] = m_[:, sl]
                dst[hh, blk + t:] = n_[:, sl]
        col = lax.broadcasted_iota(jnp.int32, (GQ * blk, 3 * blk), 1)
        lane = lax.broadcasted_iota(jnp.int32, (blk, 128), 1)
        pairs = [(b, hh) for b in range(nb) for hh in range(2)]
        qs = [_stack_heads((q0, q1)[hh][b * blk:(b + 1) * blk, :]) * scale for b, hh in pairs]
        sc = [lax.dot_general(q_, kx[hh, b * blk:(b + 3) * blk, :], (((1,), (1,)), ((), ())), preferred_element_type=f32)
              for q_, (b, hh) in zip(qs, pairs)]
        sc = [s_ + bias_ref[0, hh] for s_, (b, hh) in zip(sc, pairs)]
        sc = [jnp.where(jnp.logical_and(first, col < blk), NEG, s_) if b == 0 else s_ for s_, (b, hh) in zip(sc, pairs)]
        sc = [jnp.where(jnp.logical_and(last, col >= 2 * blk), NEG, s_) if b == nb - 1 else s_ for s_, (b, hh) in zip(sc, pairs)]
        ms = [jnp.maximum(jnp.max(s_, axis=-1, keepdims=True), sink_ref[0, hh]) for s_, (b, hh) in zip(sc, pairs)]
        ps = [jnp.exp(s_ - m_) for s_, m_ in zip(sc, ms)]
        ls = [jnp.sum(p_, axis=-1, keepdims=True) + jnp.exp(sink_ref[0, hh] - m_) for p_, m_, (b, hh) in zip(ps, ms, pairs)]
        os_ = [jnp.dot(p_.astype(bf16), vx[hh, b * blk:(b + 3) * blk, :], preferred_element_type=f32)
               for p_, (b, hh) in zip(ps, pairs)]
        os_ = [o_ / l_ for o_, l_ in zip(os_, ls)]
        lses = [m_ + jnp.log(l_) for m_, l_ in zip(ms, ls)]
        for o_, (b, hh) in zip(os_, pairs):
            o_ref[b * blk:(b + 1) * blk, hh * 256:(hh + 1) * 256] = _unstack_heads(o_, blk).astype(out_dtype)
        for b in range(nb):
            lse_tile = jnp.zeros((blk, 128), f32)
            for hh in range(2):
                lse = lses[2 * b + hh]
                for g in range(GQ):
                    lse_tile = jnp.where(lane == hh * GQ + g, lse[g * blk:(g + 1) * blk], lse_tile)
            lse_ref[b * blk:(b + 1) * blk, :] = lse_tile

    in_specs = [cfg.tile(256, cfg.qcol(e)) for e in range(2)]
    in_specs += cfg.halo(128, cfg.kcol) + cfg.halo(128, cfg.vcol)
    in_specs += [pl.BlockSpec((1, 2, GQ * blk, 3 * blk), lambda r, hp, i: (hp, 0, 0, 0)),
                 pl.BlockSpec((1, 2, GQ * blk, 1), lambda r, hp, i: (hp, 0, 0, 0))]
    body, c_in, c_out, c_shape, c_sems = _carrying(body, 10, 2, 2, carry, "gather", cfg.grid())
    return _pcall(
        body, name=name, grid=cfg.grid(), in_specs=in_specs + c_in,
        out_specs=[cfg.tile(512, cfg.pcol), cfg.tile(128, cfg.pcol)] + c_out,
        out_shape=[jax.ShapeDtypeStruct((dil, cfg.l, cfg.npairs * 512), out_dtype),
                   jax.ShapeDtypeStruct((dil, cfg.l, cfg.npairs * 128), f32)] + c_shape,
        scratch_shapes=[pltpu.VMEM((2, t + 2 * blk, DH), bf16), pltpu.VMEM((2, t + 2 * blk, DH), bf16)] + c_sems,
        compiler_params=_params("arbitrary", "arbitrary", "arbitrary"),
    )(*([qkv] * 8), jnp.asarray(cfg.bias), sinkcol, *carry)


def attn_bwd(qkv, do, lse, delta, cfg, carry=(), *, name):
    blk, t, nb, nt, dil, npairs = cfg.blk, cfg.t, cfg.nb, cfg.nt, cfg.dil, cfg.npairs
    scale = DH ** -0.5
    nt_dims = (((1,), (1,)), ((), ()))
    tn_dims = (((0,), (0,)), ((), ()))

    def body(q0p, q0m, q0n, q1p, q1m, q1n, kp, km, kn, vp, vm, vn, dop, dom, don, lp, lm, ln, dp_, dm_, dn_,
             bias_ref, dq_ref, dk_ref, dv_ref, kx, vx, dkx, dvx):
        ti = pl.program_id(2)
        first, last = ti == 0, ti == nt - 1
        for hh in range(2):
            sl = slice(hh * DH, (hh + 1) * DH)
            for dst, (p_, m_, n_) in ((kx, (kp, km, kn)), (vx, (vp, vm, vn))):
                dst[hh, 0:blk] = p_[:, sl]
                dst[hh, blk:blk + t] = m_[:, sl]
                dst[hh, blk + t:] = n_[:, sl]
        dkx[...] = jnp.zeros_like(dkx)
        dvx[...] = jnp.zeros_like(dvx)

        def slab(prev, main, nxt, e):
            if e == 0:
                return prev[...]
            if e == nb + 1:
                return nxt[...]
            return main[(e - 1) * blk:e * blk, :]

        col3 = lax.broadcasted_iota(jnp.int32, (GQ * blk, 3 * blk), 1)

        def keys(e):
            if e == 0:
                return 1, 2, slice(2 * blk, 3 * blk)
            if e == nb + 1:
                return nb, nb + 1, slice(0, blk)
            return e - 1, e + 2, slice(0, 3 * blk)

        def edge(sc, e):
            if e == 0:
                return jnp.where(first, NEG, sc)
            if e == nb + 1:
                return jnp.where(last, NEG, sc)
            if e == 1:
                sc = jnp.where(jnp.logical_and(first, col3 < blk), NEG, sc)
            if e == nb:
                sc = jnp.where(jnp.logical_and(last, col3 >= 2 * blk), NEG, sc)
            return sc

        pairs = [(e, hh) for e in range(nb + 2) for hh in range(2)]
        qs = [_stack_heads(slab(*((q0p, q0m, q0n), (q1p, q1m, q1n))[hh], e)) * scale for e, hh in pairs]
        dos = [_stack_heads(slab(dop, dom, don, e)[:, hh * 256:(hh + 1) * 256]) for e, hh in pairs]
        lse_c = [_head_cols(slab(lp, lm, ln, e), hh, blk) for e, hh in pairs]
        dl_c = [_head_cols(slab(dp_, dm_, dn_, e), hh, blk) for e, hh in pairs]
        kw = [kx[hh, keys(e)[0] * blk:keys(e)[1] * blk, :] for e, hh in pairs]
        vw = [vx[hh, keys(e)[0] * blk:keys(e)[1] * blk, :] for e, hh in pairs]
        sc = [lax.dot_general(q_, k_, nt_dims, preferred_element_type=f32) for q_, k_ in zip(qs, kw)]
        dp = [lax.dot_general(d_, v_, nt_dims, preferred_element_type=f32) for d_, v_ in zip(dos, vw)]
        sc = [edge(s_ + bias_ref[0, hh, :, keys(e)[2]], e) for s_, (e, hh) in zip(sc, pairs)]
        ps = [jnp.exp(s_ - l_) for s_, l_ in zip(sc, lse_c)]
        ds = [(p_ * (d_ - c_)).astype(bf16) for p_, d_, c_ in zip(ps, dp, dl_c)]
        pb = [p_.astype(bf16) for p_ in ps]
        dks = [lax.dot_general(s_, q_, tn_dims, preferred_element_type=f32) for s_, q_ in zip(ds, qs)]
        dvs = [lax.dot_general(p_, d_, tn_dims, preferred_element_type=f32) for p_, d_ in zip(pb, dos)]
        dqs = [jnp.dot(s_, k_, preferred_element_type=f32) if 1 <= e <= nb else None for s_, k_, (e, hh) in zip(ds, kw, pairs)]
        for dk_, dv_, dq_, (e, hh) in zip(dks, dvs, dqs, pairs):
            k0, k1, _ = keys(e)
            dkx[hh, k0 * blk:k1 * blk, :] += dk_
            dvx[hh, k0 * blk:k1 * blk, :] += dv_
            if dq_ is not None:
                dq_ref[(e - 1) * blk:e * blk, hh * 256:(hh + 1) * 256] = (_unstack_heads(dq_, blk) * scale).astype(bf16)
        for hh in range(2):
            dk_ref[:, hh * DH:(hh + 1) * DH] = dkx[hh, blk:blk + t, :].astype(bf16)
            dv_ref[:, hh * DH:(hh + 1) * DH] = dvx[hh, blk:blk + t, :].astype(bf16)

    in_specs = cfg.halo(256, cfg.qcol(0)) + cfg.halo(256, cfg.qcol(1))
    in_specs += cfg.halo(128, cfg.kcol) + cfg.halo(128, cfg.vcol)
    in_specs += cfg.halo(512, cfg.pcol) + cfg.halo(128, cfg.pcol) + cfg.halo(128, cfg.pcol)
    in_specs += [pl.BlockSpec((1, 2, GQ * blk, 3 * blk), lambda r, hp, i: (hp, 0, 0, 0))]
    body, c_in, c_out, c_shape, c_sems = _carrying(body, 22, 3, 4, carry, "exchange", cfg.grid())
    res = _pcall(
        body, name=name, grid=cfg.grid(), in_specs=in_specs + c_in,
        out_specs=[cfg.tile(512, cfg.pcol), cfg.tile(128, cfg.pcol), cfg.tile(128, cfg.pcol)] + c_out,
        out_shape=[jax.ShapeDtypeStruct((dil, cfg.l, npairs * 512), bf16),
                   jax.ShapeDtypeStruct((dil, cfg.l, npairs * 128), bf16),
                   jax.ShapeDtypeStruct((dil, cfg.l, npairs * 128), bf16)] + c_shape,
        scratch_shapes=[pltpu.VMEM((2, t + 2 * blk, DH), bf16), pltpu.VMEM((2, t + 2 * blk, DH), bf16),
                        pltpu.VMEM((2, t + 2 * blk, DH), f32), pltpu.VMEM((2, t + 2 * blk, DH), f32)] + c_sems,
        compiler_params=_params("arbitrary", "arbitrary", "arbitrary"),
    )(*([qkv] * 12), do, do, do, lse, lse, lse, delta, delta, delta, jnp.asarray(cfg.bias), *carry)
    nc = len(carry)
    return (res[0], res[1], res[2], res[3:3 + nc], res[3 + nc:]) if nc else res


def _head_indicator(nheads):
    e = np.zeros((nheads * DH, (nheads // 8) * 128), np.float32)
    for c in range(nheads * DH):
        h = c // DH
        e[c, (h // 8) * 128 + h % 8] = 1.0
    return e


def _dot_split(x, e):
    hi = x.astype(bf16)
    lo = (x - hi.astype(f32)).astype(bf16)
    return jnp.dot(hi, e, preferred_element_type=f32) + jnp.dot(lo, e, preferred_element_type=f32)


def _spread(scr, x, d):
    tm, w = x.shape
    for j in range(w // 128):
        scr[j] = x[:, j * 128:(j + 1) * 128]
    return [jnp.concatenate([scr[j, pl.ds(r, tm // d, stride=d), :] for j in range(w // 128)], axis=1) for r in range(d)]


def _weave(scr, blocks, d):
    n, w = blocks[0].shape
    for r in range(d):
        for j in range(w // 128):
            scr[j, pl.ds(r, n, stride=d), :] = blocks[r][:, j * 128:(j + 1) * 128]
    return jnp.concatenate([scr[j] for j in range(w // 128)], axis=1)


def _res_spec(d, tm, w):
    return pl.BlockSpec((d, tm // d, w), lambda i: (0, i, 0))


DILATED = tuple(dil for _, dil in B_GROUPS[1:])


def b_to_strided(qkv, *, name):
    s = qkv.shape[0]
    tm = _row_tile(s)

    def body(x_ref, *rest):
        outs, scr = rest[:-1], rest[-1]
        for gi, (o_ref, d) in enumerate(zip(outs, DILATED), start=1):
            cols = jnp.concatenate([x_ref[:, gi * 512:(gi + 1) * 512], x_ref[:, 1536 + gi * 128:1536 + (gi + 1) * 128],
                                    x_ref[:, 1920 + gi * 128:1920 + (gi + 1) * 128]], axis=1).astype(f32)
            for r, blk_ in enumerate(_spread(scr, cols, d)):
                o_ref[r] = blk_.astype(bf16)

    return _pcall(
        body, name=name, grid=(s // tm,), in_specs=[pl.BlockSpec((tm, B_QKV), lambda i: (i, 0))],
        out_specs=[_res_spec(d, tm, 768) for d in DILATED],
        out_shape=[jax.ShapeDtypeStruct((d, s // d, 768), bf16) for d in DILATED],
        scratch_shapes=[pltpu.VMEM((6, tm, 128), f32)], compiler_params=_params("parallel"),
    )(qkv)


def b_bwd_to_strided(do, lse, delta, *, name):
    s = do.shape[0]
    tm = _row_tile(s)

    def body(do_ref, lse_ref, dl_ref, *rest):
        outs, scr = rest[:-1], rest[-1]
        allc = jnp.concatenate([do_ref[...].astype(f32), lse_ref[...], dl_ref[...]], axis=1)
        for gi, d in enumerate(DILATED):
            o_do, o_lse, o_dl = outs[3 * gi:3 * gi + 3]
            for r, blk_ in enumerate(_spread(scr, allc, d)):
                o_do[r] = blk_[:, :512].astype(bf16)
                o_lse[r] = blk_[:, 512:640]
                o_dl[r] = blk_[:, 640:768]

    out_specs, out_shape = [], []
    for d in DILATED:
        out_specs += [_res_spec(d, tm, 512), _res_spec(d, tm, 128), _res_spec(d, tm, 128)]
        out_shape += [jax.ShapeDtypeStruct((d, s // d, 512), bf16), jax.ShapeDtypeStruct((d, s // d, 128), f32),
                      jax.ShapeDtypeStruct((d, s // d, 128), f32)]
    return _pcall(
        body, name=name, grid=(s // tm,),
        in_specs=[pl.BlockSpec((tm, 512), lambda i: (i, 0)), pl.BlockSpec((tm, 128), lambda i: (i, 0)),
                  pl.BlockSpec((tm, 128), lambda i: (i, 0))],
        out_specs=out_specs, out_shape=out_shape, scratch_shapes=[pltpu.VMEM((6, tm, 128), f32)],
        compiler_params=_params("parallel"),
    )(do, lse, delta)


def b_from_strided(grads, *, name):
    s = grads[0][0].shape[1]
    tm = _row_tile(s)

    def body(*refs):
        ins, o_ref, scr = refs[:9], refs[9], refs[10]
        nat = [jnp.concatenate([ins[q][0].astype(f32) for q in range(3)], axis=1)]
        for gi, d in enumerate(DILATED, start=1):
            blocks = [jnp.concatenate([ins[3 * gi + q][r].astype(f32) for q in range(3)], axis=1) for r in range(d)]
            nat.append(_weave(scr, blocks, d))
        pieces = [nat[g][:, lo:hi] for lo, hi in ((0, 512), (512, 640), (640, 768)) for g in range(3)]
        o_ref[...] = jnp.concatenate(pieces, axis=1).astype(bf16)

    dils = (1,) + DILATED
    in_specs = [_res_spec(d, tm, w) for d in dils for w in (512, 128, 128)]
    return _pcall(
        body, name=name, grid=(s * 1 // tm,), in_specs=in_specs, out_specs=pl.BlockSpec((tm, B_QKV), lambda i: (i, 0)),
        out_shape=jax.ShapeDtypeStruct((s, B_QKV), bf16), scratch_shapes=[pltpu.VMEM((6, tm, 128), f32)],
        compiler_params=_params("parallel"),
    )(*[a for g in grads for a in g])


def attn_merge(os_, lses, *, name):
    s = os_[0].shape[1]
    tm = _row_tile(s)
    ind_t = jnp.asarray(_head_indicator(8).T, dtype=bf16)
    dils = (1,) + DILATED

    def body(o0, o1, o2, l0, l1, l2, e_ref, o_ref, lse_ref, scr):
        both = [jnp.concatenate([o0[0], l0[0]], axis=1)]
        for og, lg, d in ((o1, l1, dils[1]), (o2, l2, dils[2])):
            both.append(_weave(scr, [jnp.concatenate([og[r], lg[r]], axis=1) for r in range(d)], d))
        ls = [b[:, 512:640] for b in both]
        m = jnp.maximum(jnp.maximum(ls[0], ls[1]), ls[2])
        tot = m + jnp.log(jnp.exp(ls[0] - m) + jnp.exp(ls[1] - m) + jnp.exp(ls[2] - m))
        lse_ref[...] = tot
        acc = jnp.zeros((tm, B_OUT), f32)
        for b, lg in zip(both, ls):
            acc = acc + _dot_split(jnp.exp(lg - tot), e_ref[...]) * b[:, :512]
        o_ref[...] = acc.astype(bf16)

    return _pcall(
        body, name=name, grid=(s * 1 // tm,),
        in_specs=[_res_spec(d, tm, 512) for d in dils] + [_res_spec(d, tm, 128) for d in dils]
        + [pl.BlockSpec((128, B_OUT), lambda i: (0, 0))],
        out_specs=[pl.BlockSpec((tm, B_OUT), lambda i: (i, 0)), pl.BlockSpec((tm, 128), lambda i: (i, 0))],
        out_shape=[jax.ShapeDtypeStruct((s, B_OUT), bf16), jax.ShapeDtypeStruct((s, 128), f32)],
        scratch_shapes=[pltpu.VMEM((5, tm, 128), f32)], compiler_params=_params("parallel"),
    )(*os_, *lses, ind_t)


def ada_mod(c_all, w, b, *, name):
    n = w.shape[2]

    def body(c_ref, w_ref, b_ref, o_ref):
        cv = c_ref[...]
        cond = cv * jax.nn.sigmoid(cv)
        o_ref[0] = jnp.dot(cond, w_ref[0], preferred_element_type=f32, precision=lax.Precision.HIGHEST) + b_ref[0]

    return _pcall(
        body, name=name, grid=(DEPTH,),
        in_specs=[pl.BlockSpec((N_DEV, D), lambda i: (0, 0)), pl.BlockSpec((1, D, n), lambda i: (i, 0, 0)),
                  pl.BlockSpec((1, 1, n), lambda i: (i, 0, 0))],
        out_specs=pl.BlockSpec((1, N_DEV, n), lambda i: (i, 0, 0)),
        out_shape=jax.ShapeDtypeStruct((DEPTH, N_DEV, n), f32), compiler_params=_params("arbitrary"),
    )(c_all, w, b)


def ada_grad(c_t, dm, *, name):
    n = dm.shape[2]

    def body(c_ref, dm_ref, o_ref):
        cv = c_ref[...]
        cond = cv * jax.nn.sigmoid(cv)
        acc = cond[:, 0:1] * dm_ref[0, 0:1, :]
        for b in range(1, N_DEV):
            acc = acc + cond[:, b:b + 1] * dm_ref[0, b:b + 1, :]
        o_ref[0] = acc

    return _pcall(
        body, name=name, grid=(DEPTH,),
        in_specs=[pl.BlockSpec((D, N_DEV), lambda i: (0, 0)), pl.BlockSpec((1, N_DEV, n), lambda i: (i, 0, 0))],
        out_specs=pl.BlockSpec((1, D, n), lambda i: (i, 0, 0)),
        out_shape=jax.ShapeDtypeStruct((DEPTH, D, n), f32), compiler_params=_params("arbitrary"),
    )(c_t, dm)


def _adam_math(w, g, m, v):
    m2 = B1 * m + (1.0 - B1) * g
    v2 = B2 * v + (1.0 - B2) * (g * g)
    mh = m2 / (1.0 - B1 ** STEP)
    vh = v2 / (1.0 - B2 ** STEP)
    return -LR * (mh / (jnp.sqrt(vh) + ADAM_EPS) + WD * w), m2, v2


def adamw(w, m, v, g, *, name):
    r, c = w.shape
    tr = 256 if r % 256 == 0 else r

    def body(w_ref, m_ref, v_ref, g_ref, d_ref, m2_ref, v2_ref):
        d_ref[...], m2_ref[...], v2_ref[...] = _adam_math(w_ref[...], g_ref[...], m_ref[...], v_ref[...])

    spec = pl.BlockSpec((tr, c), lambda i: (i, 0))
    return _pcall(
        body, name=name, grid=(r // tr,), in_specs=[spec] * 4, out_specs=[spec] * 3,
        out_shape=[jax.ShapeDtypeStruct((r, c), f32)] * 3, compiler_params=_params("parallel"),
    )(w, m, v, g)


def adamw_parts(w, m, v, own, sib, layer, prev=None, *, name):
    c = w.shape[1]
    r = own.shape[1]
    tr = 256 if r % 256 == 0 else r // 2
    off = layer * (r // tr)

    def body(w_ref, m_ref, v_ref, own_ref, sib_ref, *rest):
        g_ref, d_ref, m2_ref, v2_ref = rest[-4:]

        def total(ref):
            return ((ref[0].astype(f32) + ref[1].astype(f32)) + ref[2].astype(f32)) + ref[3].astype(f32)

        g = total(own_ref) + total(sib_ref)
        g_ref[...] = g
        d_ref[...], m2_ref[...], v2_ref[...] = _adam_math(w_ref[...], g, m_ref[...], v_ref[...])

    spec = pl.BlockSpec((tr, c), lambda i: (off + i, 0))
    pspec = pl.BlockSpec((4, tr, c), lambda i: (0, i, 0))
    prev = () if prev is None else tuple(prev)
    return _pcall(
        body, name=name, grid=(r // tr,), in_specs=[spec] * 3 + [pspec] * 2 + [ANY] * len(prev), out_specs=[spec] * 4,
        out_shape=[jax.ShapeDtypeStruct(w.shape, f32)] * 4,
        input_output_aliases={5 + q: q for q in range(len(prev))}, compiler_params=_params("parallel"),
    )(w, m, v, own, sib, *prev)


def sum_devices(g, *, name):
    _, r, c = g.shape

    def body(g_ref, o_ref):
        acc = g_ref[0]
        for k in range(1, N_DEV):
            acc = acc + g_ref[k]
        o_ref[...] = acc

    return _pcall(body, name=name, out_shape=jax.ShapeDtypeStruct((r, c), f32))(g)


def _place():
    x, y, c = lax.axis_index("x"), lax.axis_index("y"), lax.axis_index("c")
    chips = [(1 - x, y), (x, 1 - y), (1 - x, 1 - y)]
    return x, y, c, chips


def allgather8(v, *, name):
    r, c_ = v.shape

    def body(v_ref, o_ref, send_sems, recv_sems, local_sem):
        x, y, c, _ = _place()
        me = 4 * x + 2 * y + c
        mine = pltpu.make_async_copy(v_ref, o_ref.at[me], local_sem)
        mine.start()
        flips = [(fx, fy, fc) for fx in (0, 1) for fy in (0, 1) for fc in (0, 1)][1:]

        def peer(f):
            return (x ^ f[0], y ^ f[1], c ^ f[2])

        def copy(k, slot, to):
            return pltpu.make_async_remote_copy(
                src_ref=v_ref, dst_ref=o_ref.at[slot], send_sem=send_sems.at[k], recv_sem=recv_sems.at[k],
                device_id=to, device_id_type=MESH)

        sends = [copy(k, me, peer(f)) for k, f in enumerate(flips)]
        for cp in sends:
            cp.start()
        for k, f in enumerate(flips):
            px, py, pc = peer(f)
            copy(k, 4 * px + 2 * py + pc, (x, y, c)).wait_recv()
        for cp in sends:
            cp.wait_send()
        mine.wait()

    return _pcall(
        body, name=name, in_specs=[ANY], out_specs=ANY, out_shape=jax.ShapeDtypeStruct((N_DEV, r, c_), v.dtype),
        scratch_shapes=[pltpu.SemaphoreType.DMA((7,)), pltpu.SemaphoreType.DMA((7,)), pltpu.SemaphoreType.DMA],
    )(v)


def gather_weights(shards, *, name):
    n = len(shards)

    def body(*refs):
        src, out = refs[:n], refs[n:2 * n]
        send_a, recv_a, send_f, recv_f, local_sems = refs[2 * n:]
        x, y, c, chips = _place()
        sib = (x, y, 1 - c)
        me = 2 * x + y
        locals_ = [pltpu.make_async_copy(src[a], out[a].at[me], local_sems.at[a]) for a in range(n)]
        for cp in locals_:
            cp.start()

        def half(a, which):
            rh = src[a].shape[0] // 2
            return pl.ds(which * rh, rh)

        def first(a, k, chip_from, to):
            slot = 2 * chip_from[0] + chip_from[1]
            s_ref = src[a].at[half(a, c)]
            return pltpu.make_async_remote_copy(
                src_ref=s_ref, dst_ref=out[a].at[slot, half(a, c)], send_sem=send_a.at[3 * a + k],
                recv_sem=recv_a.at[3 * a + k], device_id=to, device_id_type=MESH)

        def passed(a, k, chip_from, which, to):
            slot = 2 * chip_from[0] + chip_from[1]
            ref = out[a].at[slot, half(a, which)]
            return pltpu.make_async_remote_copy(
                src_ref=ref, dst_ref=ref, send_sem=send_f.at[3 * a + k], recv_sem=recv_f.at[3 * a + k],
                device_id=to, device_id_type=MESH)

        sends = [first(a, k, (x, y), (*chip, c)) for a in range(n) for k, chip in enumerate(chips)]
        for cp in sends:
            cp.start()
        fwd = []
        for a in range(n):
            for k, chip in enumerate(chips):
                first(a, k, chip, (x, y, c)).wait_recv()
                cp = passed(a, k, chip, c, sib)
                cp.start()
                fwd.append(cp)
        for a in range(n):
            for k, chip in enumerate(chips):
                passed(a, k, chip, 1 - c, (x, y, c)).wait_recv()
        for cp in sends + fwd:
            cp.wait_send()
        for cp in locals_:
            cp.wait()

    return _pcall(
        body, name=name, in_specs=[ANY] * n, out_specs=[ANY] * n,
        out_shape=[jax.ShapeDtypeStruct((4,) + tuple(sh.shape), sh.dtype) for sh in shards],
        scratch_shapes=[pltpu.SemaphoreType.DMA((3 * n,)) for _ in range(4)] + [pltpu.SemaphoreType.DMA((n,))],
    )(*shards)


def _gather_direct(src, out, send_sems, recv_sems, local_sems):
    n = len(src)
    x, y, c, chips = _place()
    me = 2 * x + y

    def copy(a, k, slot, to):
        return pltpu.make_async_remote_copy(
            src_ref=src[a], dst_ref=out[a].at[slot], send_sem=send_sems.at[3 * a + k], recv_sem=recv_sems.at[3 * a + k],
            device_id=to, device_id_type=MESH)

    def start():
        for a in range(n):
            pltpu.make_async_copy(src[a], out[a].at[me], local_sems.at[a]).start()
            for k, chip in enumerate(chips):
                copy(a, k, me, (*chip, c)).start()

    def finish():
        for a in range(n):
            for k, chip in enumerate(chips):
                copy(a, k, 2 * chip[0] + chip[1], (x, y, c)).wait_recv()
        for a in range(n):
            for k in range(3):
                copy(a, k, me, (x, y, c)).wait_send()
            pltpu.make_async_copy(src[a], out[a].at[me], local_sems.at[a]).wait()

    return start, finish


def _gather_scratch(n):
    return [pltpu.SemaphoreType.DMA((3 * n,)), pltpu.SemaphoreType.DMA((3 * n,)), pltpu.SemaphoreType.DMA((n,))]


def _exchange(src, own, sibo, send_sems, recv_sems, local_sems):
    n = len(src)
    x, y, c, chips = _place()
    sib = (x, y, 1 - c)
    me = 2 * x + y

    def slot(chip):
        return 2 * chip[0] + chip[1]

    def copy(a, k, s_ref, d_ref, to):
        return pltpu.make_async_remote_copy(
            src_ref=s_ref, dst_ref=d_ref, send_sem=send_sems.at[7 * a + k], recv_sem=recv_sems.at[7 * a + k],
            device_id=to, device_id_type=MESH)

    def start():
        for a in range(n):
            pltpu.make_async_copy(src[a].at[me], own[a].at[me], local_sems.at[a]).start()
            copy(a, 0, src[a].at[me], sibo[a].at[me], sib).start()
            for k, chip in enumerate(chips):
                copy(a, 1 + k, src[a].at[slot(chip)], own[a].at[me], (*chip, c)).start()

    def finish():
        for a in range(n):
            for k, chip in enumerate(chips):
                copy(a, 1 + k, src[a].at[me], own[a].at[slot(chip)], (x, y, c)).wait_recv()
                copy(a, 4 + k, own[a].at[slot(chip)], sibo[a].at[slot(chip)], sib).start()
        for a in range(n):
            copy(a, 0, src[a].at[me], sibo[a].at[me], (x, y, c)).wait_recv()
            for k, chip in enumerate(chips):
                copy(a, 4 + k, src[a].at[me], sibo[a].at[slot(chip)], (x, y, c)).wait_recv()
        for a in range(n):
            for k in range(7):
                copy(a, k, src[a].at[me], own[a].at[me], (x, y, c)).wait_send()
            pltpu.make_async_copy(src[a].at[me], own[a].at[me], local_sems.at[a]).wait()

    return start, finish


def _exchange_scratch(n):
    return [pltpu.SemaphoreType.DMA((7 * n,)), pltpu.SemaphoreType.DMA((7 * n,)), pltpu.SemaphoreType.DMA((n,))]


def exchange_grads(parts, *, name):
    n = len(parts)

    def body(*refs):
        start, finish = _exchange(refs[:n], refs[n:2 * n], refs[2 * n:3 * n], *refs[3 * n:])
        start()
        finish()

    shapes = [jax.ShapeDtypeStruct(p.shape, p.dtype) for p in parts]
    res = _pcall(body, name=name, in_specs=[ANY] * n, out_specs=[ANY] * (2 * n), out_shape=shapes + shapes,
                 scratch_shapes=_exchange_scratch(n))(*parts)
    return res[:n], res[n:]


def _natural(g, how):
    if how == "col":
        return jnp.moveaxis(g, 0, 1).reshape(g.shape[1], 4 * g.shape[2])
    return g.reshape(4 * g.shape[1], g.shape[2])


def _chunks(gw, how):
    k, n = gw.shape
    if how == "col":
        return jnp.moveaxis(gw.reshape(k, 4, n // 4), 1, 0).astype(bf16)
    return gw.reshape(4, k // 4, n).astype(bf16)


def kernel(x, c, ada_w, ada_b, norm_mix, norm_ffn, ffn_w_in, ffn_w_out, a_w_in, a_w_out, a_sink, b_w_in, b_w_out, final_norm, loss_target, m_ada_w, m_ada_b, m_norm_mix, m_norm_ffn, m_ffn_w_in, m_ffn_w_out, m_a_w_in, m_a_w_out, m_a_sink, m_b_w_in, m_b_w_out, m_final_norm, v_ada_w, v_ada_b, v_norm_mix, v_norm_ffn, v_ffn_w_in, v_ffn_w_out, v_a_w_in, v_a_w_out, v_a_sink, v_b_w_in, v_b_w_out, v_final_norm):
    s = x.shape[1]
    xi, yi, ci = lax.axis_index("x"), lax.axis_index("y"), lax.axis_index("c")
    chip = 2 * xi + yi
    dev = 2 * chip + ci
    x0 = x[0]
    tgt = loss_target[0]

    big = {"ffn_w_in": (ffn_w_in, "col"), "ffn_w_out": (ffn_w_out, "row"), "a_w_in": (a_w_in, "col"),
           "a_w_out": (a_w_out, "row"), "b_w_in": (b_w_in, "col"), "b_w_out": (b_w_out, "col")}
    names = list(big)

    def layer_keys(i):
        mix = "a" if i % 2 == 0 else "b"
        return [("ffn_w_in", i), ("ffn_w_out", i), (mix + "_w_in", i // 2), (mix + "_w_out", i // 2)]

    def shards_of(i):
        return [big[k][0][l].astype(bf16) for k, l in layer_keys(i)]

    def weights_of(i, gathered):
        return {k: (g if k == "ffn_w_in" else _natural(g, big[k][1])) for (k, _), g in zip(layer_keys(i), gathered)}

    mix0 = gather_weights(shards_of(0)[2:], name="gather_weights")

    c_all = allgather8(jnp.broadcast_to(c, (8, D)), name="gather_c")[:, 0, :]
    nsh = ada_w.shape[2]
    ada_b_sh = lax.dynamic_slice_in_dim(ada_b, chip * nsh, nsh, axis=1)[:, None, :]
    mod_part = ada_mod(c_all, ada_w, ada_b_sh, name="ada_mod")
    mod_all = allgather8(mod_part.reshape(DEPTH * N_DEV, nsh), name="gather_mod")
    mod_all = mod_all.reshape(4, 2, DEPTH, N_DEV, nsh)[:, 0]
    mod = lax.dynamic_index_in_dim(mod_all, dev, axis=2, keepdims=False)
    mod = jnp.moveaxis(mod, 0, 1).reshape(DEPTH, 6, 1, D)

    cfg_a = _Attn(s, mixer="a")
    cfg_b = [_Attn(s, mixer="b", group=g) for g in range(3)]
    no_sink = jnp.full((1, 2, GQ * 64, 1), NEG, f32)

    saved = []
    xc = x0
    for i in range(DEPTH):
        j = i // 2
        sh1, sc1, g1, sh2, sc2, g2 = (mod[i, q] for q in range(6))
        nmix, nffn = norm_mix[i][None, :], norm_ffn[i][None, :]
        mix = "a" if i % 2 == 0 else "b"
        if i == 0:
            h, qkv = mm_norm(xc, nmix, sc1, sh1, _natural(mix0[0], "col"), name="a_qkv")
            sinkcol = jnp.repeat(a_sink[j].reshape(2, 2, GQ), 128, axis=2)[..., None]
            o, lse, *ffn0 = attn_fwd(qkv[None], sinkcol, cfg_a, shards_of(0)[:2], out_dtype=bf16, name="a_attn_fwd_gather")
            o, lse = o[0], lse[0]
            wl = [weights_of(0, ffn0 + list(mix0))]
        elif i % 2 == 0:
            h, qkv = mm_norm(xc, nmix, sc1, sh1, wl[i]["a_w_in"], name="a_qkv")
            sinkcol = jnp.repeat(a_sink[j].reshape(2, 2, GQ), 128, axis=2)[..., None]
            o, lse = (t[0] for t in attn_fwd(qkv[None], sinkcol, cfg_a, out_dtype=bf16, name="a_attn_fwd"))
        else:
            h, qkv = mm_norm(xc, nmix, sc1, sh1, wl[i]["b_w_in"], name="b_qkv")
            qkv = [qkv[None]] + list(b_to_strided(qkv, name="b_to_strided"))
            outs = [attn_fwd(qkv[g], no_sink, cfg_b[g], out_dtype=f32, name=f"b_attn_fwd{g}") for g in range(3)]
            o, lse = attn_merge([t[0] for t in outs], [t[1] for t in outs], name="b_merge")
        x1 = mm_resid(o, wl[i][mix + "_w_out"], xc, g1, name=mix + "_out")
        nxt = shards_of(i + 1) if i + 1 < DEPTH else []
        h2, gu, act, x2, *got = ffn_fwd(x1, nffn, sc2, sh2, g2, wl[i]["ffn_w_in"], wl[i]["ffn_w_out"], nxt,
                                        name="ffn_fwd_gather" if nxt else "ffn_fwd")
        if nxt:
            wl.append(weights_of(i + 1, got))
        saved.append((xc, h, qkv, o, lse, x1, h2, gu, act))
        xc = x2

    dx, st_final = loss_head(xc, final_norm[None, :], tgt, name="loss_head")

    zero_row = jnp.zeros((1, D), f32)
    dmod_rows = [None] * DEPTH
    d_nmix, d_nffn = [None] * DEPTH, [None] * DEPTH
    d_sink = [None] * 2
    parts, exchanged = None, {}
    for i in reversed(range(DEPTH)):
        j = i // 2
        xin, h, qkv, o, lse, x1, h2, gu, act = saved[i]
        sh1, sc1, g1, sh2, sc2, g2 = (mod[i, q] for q in range(6))
        nmix, nffn = norm_mix[i][None, :], norm_ffn[i][None, :]
        mix = "a" if i % 2 == 0 else "b"
        w_fo, w_o, w_i = wl[i]["ffn_w_out"], wl[i][mix + "_w_out"], wl[i][mix + "_w_in"]
        dgu, dx1, st2, own_, sib_ = ffn_bwd_rows(dx, x1, gu, g2, nffn, sc2, wl[i]["ffn_w_in"], w_fo, parts or [],
                                                 name="ffn_bwd_rows_exchange" if parts else "ffn_bwd_rows")
        if parts:
            exchanged[i + 1] = (own_, sib_)
        gwo, dg2 = ffn_dw_out(act, dx, g2, w_fo, name="ffn_dw_out")
        dg2 = dg2[0:1]
        gwi = ffn_dw_in(h2, dgu, name="ffn_dw_in")
        gmo, dg1 = mm_tn(o, dx1, (g1, w_o), name=mix + "_dw_out")
        ffn_parts = [gwi, gwo.reshape(4, F // 4, D)]
        if i % 2 == 0:
            sinkrow = jnp.pad(a_sink[j].reshape(2, 8), ((0, 0), (0, 120))).reshape(1, 256)
            do, delta, dsk = mm_nt_delta(dx1, g1, w_o, o, lse, sinkrow, name="a_do")
            d_sink[j] = dsk[0].reshape(2, 128)[:, :8].reshape(16)
            dq, dk, dv, *ffn_x = attn_bwd(qkv[None], do[None], lse[None], delta[None], cfg_a, ffn_parts if i == 0 else [],
                                          name="a_attn_bwd_exchange" if i == 0 else "a_attn_bwd")
            dqkv = jnp.concatenate([dq[0], dk[0], dv[0]], axis=1)
        else:
            do, delta, _ = mm_nt_delta(dx1, g1, w_o, o, lse, jnp.zeros((1, 128), f32), name="b_do")
            st = [do[None], lse[None], delta[None]] + list(b_bwd_to_strided(do, lse, delta, name="b_bwd_to_strided"))
            gr = [attn_bwd(qkv[g], *st[3 * g:3 * g + 3], cfg_b[g], name=f"b_attn_bwd{g}") for g in range(3)]
            dqkv = b_from_strided(gr, name="b_from_strided")
        gmi = mm_tn(h, dqkv, name=mix + "_dw_in")
        dx, st1 = mm_nt_norm_bwd(dqkv, w_i, xin, dx1, nmix, sc1, name=mix + "_dh")
        dmod_rows[i] = jnp.concatenate([st1[2:3], st1[1:2], dg1, st2[2:3], st2[1:2], dg2], axis=0)
        d_nmix[i], d_nffn[i] = st1[0:1], st2[0:1]
        mix_parts = [_chunks(gmi, big[mix + "_w_in"][1]), _chunks(gmo, big[mix + "_w_out"][1])]
        parts = ffn_parts + mix_parts
    own_m, sib_m = exchange_grads(mix_parts, name="exchange_grads")
    exchanged[0] = (list(ffn_x[0]) + list(own_m), list(ffn_x[1]) + list(sib_m))

    sink_row = jnp.pad(jnp.concatenate(d_sink), (0, D - 32))[None, :]
    stats = jnp.concatenate(dmod_rows + d_nmix + d_nffn + [sink_row, st_final[0:1], st_final[1:2]]
                            + [zero_row] * (STAT_ROWS - 35), axis=0)
    stats_all = allgather8(stats, name="gather_stats")
    tot = sum_devices(stats_all, name="sum_stats")
    loss = 0.5 * jnp.sum(tot[34]) / float(D)

    def pack(ab, nm, nf, sk, fnm, fill):
        return jnp.concatenate([ab.reshape(24, D), nm, nf, jnp.pad(sk.reshape(1, 32), ((0, 0), (0, D - 32)), constant_values=fill),
                                fnm[None, :], jnp.full((STAT_ROWS - 34, D), fill, f32)], axis=0)

    sd, sm, sv = adamw(pack(ada_b, norm_mix, norm_ffn, a_sink, final_norm, 0.0),
                       pack(m_ada_b, m_norm_mix, m_norm_ffn, m_a_sink, m_final_norm, 0.0),
                       pack(v_ada_b, v_norm_mix, v_norm_ffn, v_a_sink, v_final_norm, 1.0), tot, name="adamw_small")

    def unpack(p):
        return p[0:24].reshape(DEPTH, 6 * D), p[24:28], p[28:32], p[32, :32].reshape(2, 16), p[33]

    small = {"grad": unpack(tot), "delta": unpack(sd), "m": unpack(sm), "v": unpack(sv)}

    dmod_all = stats_all[:, 0:24, :].reshape(N_DEV, DEPTH, 6 * D)
    dm_sh = jnp.moveaxis(lax.dynamic_slice_in_dim(dmod_all, chip * nsh, nsh, axis=2), 0, 1)
    g_ada = ada_grad(c_all.T, dm_sh, name="ada_grad")
    r_ada = (DEPTH * D, nsh)
    ada_res = adamw(ada_w.reshape(r_ada), m_ada_w.reshape(r_ada), v_ada_w.reshape(r_ada), g_ada.reshape(r_ada), name="adamw_ada")
    ada_out = [g_ada] + [t.reshape(ada_w.shape) for t in ada_res]

    mom = {"ffn_w_in": (m_ffn_w_in, v_ffn_w_in), "ffn_w_out": (m_ffn_w_out, v_ffn_w_out), "a_w_in": (m_a_w_in, v_a_w_in),
           "a_w_out": (m_a_w_out, v_a_w_out), "b_w_in": (m_b_w_in, v_b_w_in), "b_w_out": (m_b_w_out, v_b_w_out)}
    big_out = {k: None for k in names}
    for i in reversed(range(DEPTH)):
        own_, sib_ = exchanged[i]
        for (k, l), o_, s_ in zip(layer_keys(i), own_, sib_):
            w = big[k][0]
            r2 = (-1, w.shape[-1])
            big_out[k] = adamw_parts(w.reshape(r2), mom[k][0].reshape(r2), mom[k][1].reshape(r2), o_, s_, l, big_out[k],
                                     name=f"adamw_{k}{l}")
    big_out = {k: [t.reshape(big[k][0].shape) for t in big_out[k]] for k in names}

    def leaves(q):
        sm_ = small[("grad", "delta", "m", "v")[q]]
        return (ada_out[q], sm_[0], sm_[1], sm_[2], big_out["ffn_w_in"][q], big_out["ffn_w_out"][q], big_out["a_w_in"][q],
                big_out["a_w_out"][q], sm_[3], big_out["b_w_in"][q], big_out["b_w_out"][q], sm_[4])

    return (loss, dx[None], *leaves(0), *leaves(1), *leaves(2), *leaves(3))
```

```python
import functools
import math

import numpy as np
import jax
import jax.numpy as jnp
from jax import lax
from jax.experimental import pallas as pl
from jax.experimental.pallas import tpu as pltpu

f32 = jnp.float32
bf16 = jnp.bfloat16

D = 1024
DH = 64
GQ = 4
DEPTH = 4
F = 2816
A_QKV, A_OUT = 1536, 1024
B_QKV, B_OUT = 2304, 512
B_GROUPS = ((128, 1), (512, 4), (2048, 16))
RMS_EPS = 1e-6
NEG = -1e30
LR, B1, B2, ADAM_EPS, WD, STEP = 0.001, 0.9, 0.999, 1e-08, 0.01, 10
N_DEV = 8
STAT_ROWS = 40
MESH = pl.DeviceIdType.MESH
ANY = pl.BlockSpec(memory_space=pl.ANY)


def _pcall(body, **kw):
    return pl.pallas_call(body, **kw)


def _params(*sem):
    return pltpu.CompilerParams(dimension_semantics=sem, vmem_limit_bytes=56 * 1024 * 1024)


def _row_tile(s, want=1024):
    return want if s % want == 0 else s


ROW_CHUNKS = 4


def mm_norm(x, nw, sc, sh, w, *, name):
    s, d = x.shape
    n = w.shape[1]
    tm = _row_tile(s)
    rc = tm // ROW_CHUNKS

    def body(x_ref, nw_ref, sc_ref, sh_ref, w_ref, h_ref, y_ref):
        hs = []
        for c in range(ROW_CHUNKS):
            xv = x_ref[c * rc:(c + 1) * rc, :]
            r = lax.rsqrt(jnp.mean(xv * xv, axis=-1, keepdims=True) + RMS_EPS)
            hs.append(((xv * r * nw_ref[...]) * (1.0 + sc_ref[...]) + sh_ref[...]).astype(bf16))
        ys = [jnp.dot(h, w_ref[...], preferred_element_type=f32) for h in hs]
        for c in range(ROW_CHUNKS):
            h_ref[c * rc:(c + 1) * rc, :] = hs[c]
            y_ref[c * rc:(c + 1) * rc, :] = ys[c].astype(bf16)

    vec = pl.BlockSpec((1, d), lambda i: (0, 0))
    return _pcall(
        body, name=name, grid=(s // tm,),
        in_specs=[pl.BlockSpec((tm, d), lambda i: (i, 0)), vec, vec, vec, pl.BlockSpec((d, n), lambda i: (0, 0))],
        out_specs=[pl.BlockSpec((tm, d), lambda i: (i, 0)), pl.BlockSpec((tm, n), lambda i: (i, 0))],
        out_shape=[jax.ShapeDtypeStruct((s, d), bf16), jax.ShapeDtypeStruct((s, n), bf16)],
        compiler_params=_params("parallel"),
    )(x, nw, sc, sh, w)


def mm_resid(a, w, xres, g, *, name):
    s, k = a.shape
    n = w.shape[1]
    tm = _row_tile(s)
    rc = tm // ROW_CHUNKS

    def body(a_ref, w_ref, x_ref, g_ref, o_ref):
        ys = [jnp.dot(a_ref[c * rc:(c + 1) * rc, :], w_ref[...], preferred_element_type=f32) for c in range(ROW_CHUNKS)]
        for c, y in enumerate(ys):
            o_ref[c * rc:(c + 1) * rc, :] = x_ref[c * rc:(c + 1) * rc, :] + g_ref[...] * y

    big = pl.BlockSpec((tm, n), lambda i: (i, 0))
    return _pcall(
        body, name=name, grid=(s // tm,),
        in_specs=[pl.BlockSpec((tm, k), lambda i: (i, 0)), pl.BlockSpec((k, n), lambda i: (0, 0)), big,
                  pl.BlockSpec((1, n), lambda i: (0, 0))],
        out_specs=big, out_shape=jax.ShapeDtypeStruct((s, n), f32), compiler_params=_params("parallel"),
    )(a, w, xres, g)


def mm_nt_delta(dx, g, w, o, lse, sinkrow, *, name):
    s, d = dx.shape
    n = w.shape[0]
    wd = lse.shape[1]
    tm = _row_tile(s)
    rc = tm // ROW_CHUNKS
    ind = jnp.asarray(_head_indicator(n // DH), dtype=bf16)

    def body(dx_ref, g_ref, w_ref, o_ref, lse_ref, sink_ref, e_ref, do_ref, dl_ref, ds_ref):
        rows = [slice(c * rc, (c + 1) * rc) for c in range(ROW_CHUNKS)]
        as_ = [(dx_ref[rw, :] * g_ref[...]).astype(bf16) for rw in rows]
        dos = [lax.dot_general(a, w_ref[...], (((1,), (1,)), ((), ())), preferred_element_type=f32).astype(bf16) for a in as_]
        dls = [_dot_split(do.astype(f32) * o_ref[rw, :].astype(f32), e_ref[...]) for do, rw in zip(dos, rows)]
        part = None
        for rw, do, dl in zip(rows, dos, dls):
            do_ref[rw, :] = do
            dl_ref[rw, :] = dl
            p = -jnp.sum(jnp.exp(sink_ref[...] - lse_ref[rw, :]) * dl, axis=0, keepdims=True)
            part = p if part is None else part + p
        part = jnp.concatenate([part, jnp.zeros((7, wd), f32)], axis=0)

        @pl.when(pl.program_id(0) == 0)
        def _():
            ds_ref[...] = part

        @pl.when(pl.program_id(0) != 0)
        def _():
            ds_ref[...] += part

    return _pcall(
        body, name=name, grid=(s // tm,),
        in_specs=[pl.BlockSpec((tm, d), lambda i: (i, 0)), pl.BlockSpec((1, d), lambda i: (0, 0)),
                  pl.BlockSpec((n, d), lambda i: (0, 0)), pl.BlockSpec((tm, n), lambda i: (i, 0)),
                  pl.BlockSpec((tm, wd), lambda i: (i, 0)), pl.BlockSpec((1, wd), lambda i: (0, 0)),
                  pl.BlockSpec((n, wd), lambda i: (0, 0))],
        out_specs=[pl.BlockSpec((tm, n), lambda i: (i, 0)), pl.BlockSpec((tm, wd), lambda i: (i, 0)),
                   pl.BlockSpec((8, wd), lambda i: (0, 0))],
        out_shape=[jax.ShapeDtypeStruct((s, n), bf16), jax.ShapeDtypeStruct((s, wd), f32), jax.ShapeDtypeStruct((8, wd), f32)],
        compiler_params=_params("arbitrary"),
    )(dx, g, w, o, lse, sinkrow, ind)


def mm_nt_norm_bwd(a, w, x, dres, nw, sc, *, name):
    s, k = a.shape
    d = w.shape[0]
    tm = _row_tile(s)
    rc = tm // ROW_CHUNKS

    def body(a_ref, w_ref, x_ref, dr_ref, nw_ref, sc_ref, o_ref, st_ref):
        dhs = [lax.dot_general(a_ref[c * rc:(c + 1) * rc, :], w_ref[...], (((1,), (1,)), ((), ())), preferred_element_type=f32)
               for c in range(ROW_CHUNKS)]
        rows = None
        for c, dh in enumerate(dhs):
            xv = x_ref[c * rc:(c + 1) * rc, :]
            r = lax.rsqrt(jnp.mean(xv * xv, axis=-1, keepdims=True) + RMS_EPS)
            xh = xv * r
            dn = dh * (1.0 + sc_ref[...])
            dxh = dn * nw_ref[...]
            o_ref[c * rc:(c + 1) * rc, :] = dr_ref[c * rc:(c + 1) * rc, :] + r * (dxh - xh * jnp.mean(dxh * xh, axis=-1, keepdims=True))
            part = jnp.concatenate([
                jnp.sum(dn * xh, axis=0, keepdims=True),
                jnp.sum(dh * (xh * nw_ref[...]), axis=0, keepdims=True),
                jnp.sum(dh, axis=0, keepdims=True),
                jnp.zeros((5, d), f32)], axis=0)
            rows = part if rows is None else rows + part

        @pl.when(pl.program_id(0) == 0)
        def _():
            st_ref[...] = rows

        @pl.when(pl.program_id(0) != 0)
        def _():
            st_ref[...] += rows

    big = pl.BlockSpec((tm, d), lambda i: (i, 0))
    vec = pl.BlockSpec((1, d), lambda i: (0, 0))
    return _pcall(
        body, name=name, grid=(s // tm,),
        in_specs=[pl.BlockSpec((tm, k), lambda i: (i, 0)), pl.BlockSpec((d, k), lambda i: (0, 0)), big, big, vec, vec],
        out_specs=[big, pl.BlockSpec((8, d), lambda i: (0, 0))],
        out_shape=[jax.ShapeDtypeStruct((s, d), f32), jax.ShapeDtypeStruct((8, d), f32)],
        compiler_params=_params("arbitrary"),
    )(a, w, x, dres, nw, sc)


def mm_tn(a, b, scale=None, *, name):
    s, ka = a.shape
    nb = b.shape[1]
    ts = _row_tile(s)
    tn = 768 if nb % 768 == 0 and nb % 512 != 0 else 512
    tka = 1408 if ka % 1408 == 0 else min(ka, 1024)
    ns = s // ts

    def body(a_ref, b_ref, *rest):
        o_ref = rest[2] if scale is not None else rest[0]
        si = pl.program_id(2)
        part = lax.dot_general(a_ref[...], b_ref[...].astype(bf16), (((0,), (0,)), ((), ())), preferred_element_type=f32)

        @pl.when(si == 0)
        def _():
            o_ref[...] = part

        @pl.when(si != 0)
        def _():
            o_ref[...] += part

        if scale is not None:
            g_ref, wb_ref, dg_ref = rest[0], rest[1], rest[3]

            @pl.when(si == ns - 1)
            def _():
                gm = o_ref[...]
                dgp = jnp.sum(wb_ref[...].astype(f32) * gm, axis=0, keepdims=True)

                @pl.when(pl.program_id(1) == 0)
                def _():
                    dg_ref[...] = dgp

                @pl.when(pl.program_id(1) != 0)
                def _():
                    dg_ref[...] += dgp

                o_ref[...] = gm * g_ref[...]

    in_specs = [pl.BlockSpec((ts, tka), lambda j, i, k: (k, i)), pl.BlockSpec((ts, tn), lambda j, i, k: (k, j))]
    args = [a, b]
    out_specs = [pl.BlockSpec((tka, tn), lambda j, i, k: (i, j))]
    out_shape = [jax.ShapeDtypeStruct((ka, nb), f32)]
    if scale is not None:
        in_specs += [pl.BlockSpec((1, tn), lambda j, i, k: (0, j)), pl.BlockSpec((tka, tn), lambda j, i, k: (i, j))]
        args += list(scale)
        out_specs.append(pl.BlockSpec((1, tn), lambda j, i, k: (0, j)))
        out_shape.append(jax.ShapeDtypeStruct((1, nb), f32))
    res = _pcall(
        body, name=name, grid=(nb // tn, ka // tka, ns), in_specs=in_specs, out_specs=out_specs, out_shape=out_shape,
        compiler_params=_params("arbitrary", "arbitrary", "arbitrary"),
    )(*args)
    return res if scale is not None else res[0]


def loss_head(x, fn, tgt, *, name):
    s, d = x.shape
    tm = _row_tile(s, 512)

    def body(x_ref, fn_ref, t_ref, dx_ref, st_ref):
        xv = x_ref[...]
        r = lax.rsqrt(jnp.mean(xv * xv, axis=-1, keepdims=True) + RMS_EPS)
        xh = xv * r
        err = xh * fn_ref[...] - t_ref[...]
        dy = err / float(d)
        dxh = dy * fn_ref[...]
        dx_ref[...] = r * (dxh - xh * jnp.mean(dxh * xh, axis=-1, keepdims=True))
        rows = jnp.concatenate([
            jnp.sum(dy * xh, axis=0, keepdims=True),
            jnp.sum(err * err, axis=0, keepdims=True),
            jnp.zeros((6, d), f32)], axis=0)

        @pl.when(pl.program_id(0) == 0)
        def _():
            st_ref[...] = rows

        @pl.when(pl.program_id(0) != 0)
        def _():
            st_ref[...] += rows

    big = pl.BlockSpec((tm, d), lambda i: (i, 0))
    return _pcall(
        body, name=name, grid=(s // tm,), in_specs=[big, pl.BlockSpec((1, d), lambda i: (0, 0)), big],
        out_specs=[big, pl.BlockSpec((8, d), lambda i: (0, 0))],
        out_shape=[jax.ShapeDtypeStruct((s, d), f32), jax.ShapeDtypeStruct((8, d), f32)],
        compiler_params=_params("arbitrary"),
    )(x, fn, tgt)


FC = 2 * F // 4
FFN_ROWS = 256
FFN_CHUNKS = 1


def _resident(pairs, sems):
    @pl.when(pl.program_id(0) == 0)
    def _():
        cps = [pltpu.make_async_copy(h, v, sems.at[i]) for i, (h, v) in enumerate(pairs)]
        for cp in cps:
            cp.start()
        for cp in cps:
            cp.wait()


def ffn_fwd(x, nw, sc, sh, g, w_in, w_out, carry=(), *, name):
    s, d = x.shape
    tm = _row_tile(s, FFN_ROWS)
    nsteps = s // tm
    nc = len(carry)

    def body(*refs):
        x_ref, nw_ref, sc_ref, sh_ref, g_ref, win_hbm, wout_hbm = refs[:7]
        h_ref, gu_ref, a_ref, o_ref = refs[7 + nc:11 + nc]
        win_v, wout_v, sems = refs[11 + 2 * nc:14 + 2 * nc]
        if nc:
            start, finish = _gather_direct(refs[7:7 + nc], refs[11 + nc:11 + 2 * nc], *refs[14 + 2 * nc:])
            pl.when(pl.program_id(0) == 0)(start)
        _resident([(win_hbm, win_v), (wout_hbm, wout_v)], sems)
        rc = tm // FFN_CHUNKS
        rows = [slice(q * rc, (q + 1) * rc) for q in range(FFN_CHUNKS)]
        xs = [x_ref[rw, :] for rw in rows]
        hs = [((xv * lax.rsqrt(jnp.mean(xv * xv, axis=-1, keepdims=True) + RMS_EPS) * nw_ref[...]) * (1.0 + sc_ref[...])
               + sh_ref[...]).astype(bf16) for xv in xs]
        pairs = [(q, c) for q in range(FFN_CHUNKS) for c in range(2)]
        gts = [jnp.dot(hs[q], win_v[c], preferred_element_type=f32) for q, c in pairs]
        ups = [jnp.dot(hs[q], win_v[c + 2], preferred_element_type=f32) for q, c in pairs]
        acts = [(gt * jax.nn.sigmoid(gt) * up).astype(bf16) for gt, up in zip(gts, ups)]
        ys = [jnp.dot(act, wout_v[c * FC:(c + 1) * FC, :], preferred_element_type=f32) for act, (q, c) in zip(acts, pairs)]
        for (q, c), gt, up, act in zip(pairs, gts, ups, acts):
            cs = slice(c * FC, (c + 1) * FC)
            gu_ref[0, rows[q], cs] = gt.astype(bf16)
            gu_ref[1, rows[q], cs] = up.astype(bf16)
            a_ref[rows[q], cs] = act
        for q in range(FFN_CHUNKS):
            h_ref[rows[q], :] = hs[q]
            o_ref[rows[q], :] = xs[q] + g_ref[...] * (ys[2 * q] + ys[2 * q + 1])
        if nc:
            pl.when(pl.program_id(0) == nsteps - 1)(finish)

    big = pl.BlockSpec((tm, d), lambda i: (i, 0))
    vec = pl.BlockSpec((1, d), lambda i: (0, 0))
    return _pcall(
        body, name=name, grid=(nsteps,), in_specs=[big, vec, vec, vec, vec, ANY, ANY] + [ANY] * nc,
        out_specs=[big, pl.BlockSpec((2, tm, F), lambda i: (0, i, 0)), pl.BlockSpec((tm, F), lambda i: (i, 0)), big] + [ANY] * nc,
        out_shape=[jax.ShapeDtypeStruct((s, d), bf16), jax.ShapeDtypeStruct((2, s, F), bf16),
                   jax.ShapeDtypeStruct((s, F), bf16), jax.ShapeDtypeStruct((s, d), f32)]
        + [jax.ShapeDtypeStruct((4,) + tuple(sh_.shape), sh_.dtype) for sh_ in carry],
        scratch_shapes=[pltpu.VMEM((4, d, FC), bf16), pltpu.VMEM((F, d), bf16), pltpu.SemaphoreType.DMA((2,))]
        + (_gather_scratch(nc) if nc else []),
        compiler_params=_params("arbitrary"),
    )(x, nw, sc, sh, g, w_in, w_out, *carry)


def ffn_bwd_rows(dx, x, gu, g, nw, sc, w_in, w_out, carry=(), *, name):
    s, d = x.shape
    tm = _row_tile(s, FFN_ROWS)
    nsteps = s // tm
    nc = len(carry)
    nt_dims = (((1,), (1,)), ((), ()))

    def body(*refs):
        dx_ref, x_ref, gu_ref, g_ref, nw_ref, sc_ref, win_hbm, wout_hbm = refs[:8]
        dgu_ref, o_ref, st_ref = refs[8 + nc:11 + nc]
        win_v, wout_v, sems = refs[11 + 3 * nc:14 + 3 * nc]
        if nc:
            start, finish = _exchange(refs[8:8 + nc], refs[11 + nc:11 + 2 * nc], refs[11 + 2 * nc:11 + 3 * nc],
                                      *refs[14 + 3 * nc:])
            pl.when(pl.program_id(0) == 0)(start)
        _resident([(win_hbm, win_v), (wout_hbm, wout_v)], sems)
        dxv = dx_ref[...]
        a = (dxv * g_ref[...]).astype(bf16)
        dh = None
        for c in range(2):
            cs = slice(c * FC, (c + 1) * FC)
            da = lax.dot_general(a, wout_v[cs, :], nt_dims, preferred_element_type=f32)
            gt = gu_ref[0, :, cs].astype(f32)
            up = gu_ref[1, :, cs].astype(f32)
            sg = jax.nn.sigmoid(gt)
            dgate = (da * up * (sg * (1.0 + gt * (1.0 - sg)))).astype(bf16)
            dup = (da * (gt * sg)).astype(bf16)
            dgu_ref[0, :, cs] = dgate
            dgu_ref[1, :, cs] = dup
            part = (lax.dot_general(dgate, win_v[c], nt_dims, preferred_element_type=f32)
                    + lax.dot_general(dup, win_v[c + 2], nt_dims, preferred_element_type=f32))
            dh = part if dh is None else dh + part
        xv = x_ref[...]
        r = lax.rsqrt(jnp.mean(xv * xv, axis=-1, keepdims=True) + RMS_EPS)
        xh = xv * r
        dn = dh * (1.0 + sc_ref[...])
        dxh = dn * nw_ref[...]
        o_ref[...] = dxv + r * (dxh - xh * jnp.mean(dxh * xh, axis=-1, keepdims=True))
        rows = jnp.concatenate([
            jnp.sum(dn * xh, axis=0, keepdims=True),
            jnp.sum(dh * (xh * nw_ref[...]), axis=0, keepdims=True),
            jnp.sum(dh, axis=0, keepdims=True),
            jnp.zeros((5, d), f32)], axis=0)

        @pl.when(pl.program_id(0) == 0)
        def _():
            st_ref[...] = rows

        @pl.when(pl.program_id(0) != 0)
        def _():
            st_ref[...] += rows

        if nc:
            pl.when(pl.program_id(0) == nsteps - 1)(finish)

    big = pl.BlockSpec((tm, d), lambda i: (i, 0))
    vec = pl.BlockSpec((1, d), lambda i: (0, 0))
    gus = pl.BlockSpec((2, tm, F), lambda i: (0, i, 0))
    cshapes = [jax.ShapeDtypeStruct(p.shape, p.dtype) for p in carry]
    res = _pcall(
        body, name=name, grid=(nsteps,), in_specs=[big, big, gus, vec, vec, vec, ANY, ANY] + [ANY] * nc,
        out_specs=[gus, big, pl.BlockSpec((8, d), lambda i: (0, 0))] + [ANY] * (2 * nc),
        out_shape=[jax.ShapeDtypeStruct((2, s, F), bf16), jax.ShapeDtypeStruct((s, d), f32), jax.ShapeDtypeStruct((8, d), f32)]
        + cshapes + cshapes,
        scratch_shapes=[pltpu.VMEM((4, d, FC), bf16), pltpu.VMEM((F, d), bf16), pltpu.SemaphoreType.DMA((2,))]
        + (_exchange_scratch(nc) if nc else []),
        compiler_params=_params("arbitrary"),
    )(dx, x, gu, g, nw, sc, w_in, w_out, *carry)
    return res[0], res[1], res[2], res[3:3 + nc], res[3 + nc:]


def ffn_dw_in(h, dgu, *, name):
    s, d = h.shape
    ts = _row_tile(s)
    ns = s // ts
    tn_dims = (((0,), (0,)), ((), ()))

    def body(h_ref, dgu_ref, o_ref, acc):
        k = pl.program_id(1)

        @pl.when(k == 0)
        def _():
            acc[...] = jnp.zeros_like(acc)

        hv = h_ref[...]
        for c in range(2):
            acc[c] += lax.dot_general(hv, dgu_ref[:, c * FC:(c + 1) * FC], tn_dims, preferred_element_type=f32)

        @pl.when(k == ns - 1)
        def _():
            o_ref[...] = acc[...].astype(bf16)

    return _pcall(
        body, name=name, grid=(2, ns),
        in_specs=[pl.BlockSpec((ts, d), lambda hf, k: (k, 0)), pl.BlockSpec((None, ts, F), lambda hf, k: (hf, k, 0))],
        out_specs=pl.BlockSpec((2, d, FC), lambda hf, k: (hf, 0, 0)),
        out_shape=jax.ShapeDtypeStruct((4, d, FC), bf16), scratch_shapes=[pltpu.VMEM((2, d, FC), f32)],
        compiler_params=_params("arbitrary", "arbitrary"),
    )(h, dgu)


def ffn_dw_out(a, dx, g, wb, *, name):
    s, fdim = a.shape
    d = dx.shape[1]
    ts = _row_tile(s)
    ns = s // ts
    tn = d // 2
    tn_dims = (((0,), (0,)), ((), ()))

    def body(a_ref, dx_ref, g_ref, wb_ref, o_ref, dg_ref, acc):
        k = pl.program_id(1)

        @pl.when(k == 0)
        def _():
            acc[...] = jnp.zeros_like(acc)

        acc[...] += lax.dot_general(a_ref[...], dx_ref[...].astype(bf16), tn_dims, preferred_element_type=f32)

        @pl.when(k == ns - 1)
        def _():
            gm = acc[...]
            dg_ref[...] = jnp.concatenate([jnp.sum(wb_ref[...].astype(f32) * gm, axis=0, keepdims=True),
                                           jnp.zeros((7, tn), f32)], axis=0)
            o_ref[...] = (gm * g_ref[...]).astype(bf16)

    return _pcall(
        body, name=name, grid=(2, ns),
        in_specs=[pl.BlockSpec((ts, fdim), lambda j, k: (k, 0)), pl.BlockSpec((ts, tn), lambda j, k: (k, j)),
                  pl.BlockSpec((1, tn), lambda j, k: (0, j)), pl.BlockSpec((fdim, tn), lambda j, k: (0, j))],
        out_specs=[pl.BlockSpec((fdim, tn), lambda j, k: (0, j)), pl.BlockSpec((8, tn), lambda j, k: (0, j))],
        out_shape=[jax.ShapeDtypeStruct((fdim, d), bf16), jax.ShapeDtypeStruct((8, d), f32)],
        scratch_shapes=[pltpu.VMEM((fdim, tn), f32)], compiler_params=_params("arbitrary", "arbitrary"),
    )(a, dx, g, wb)


def _alibi(n):
    return np.asarray(2.0 ** (-8.0 * np.arange(1, n + 1) / n), dtype=np.float32)


class _Attn:
    def __init__(self, s, *, mixer, group=0):
        if mixer == "a":
            self.blk, self.dil, self.npairs = 128, 1, 2
            self.qb0, self.kb0, self.vb0 = 0, 8, 10
            slopes = _alibi(16).reshape(2, 2, GQ)
        else:
            window, dil = B_GROUPS[group]
            self.blk, self.dil, self.npairs = window // (2 * dil), dil, 1
            self.qb0, self.kb0, self.vb0 = (0, 12, 15) if dil == 1 else (0, 4, 5)
            slopes = _alibi(24).reshape(3, 1, 2, GQ)[group]
        self.l = s // self.dil
        self.t = min(512, self.l)
        self.nt = self.l // self.t
        self.nb = self.t // self.blk
        blk = self.blk
        qi = np.arange(blk)[:, None]
        rel = np.arange(3 * blk)[None, :] - blk - qi
        dist = (self.dil * np.abs(rel)).astype(np.float32)
        bias = -slopes[:, :, :, None, None] * dist[None, None, None]
        bias = np.where(np.abs(rel) <= blk, bias, np.float32(NEG)).astype(np.float32)
        self.bias = np.ascontiguousarray(np.swapaxes(bias.reshape(self.npairs, 2, GQ * blk, 3 * blk), -1, -2))

    def grid(self):
        return (self.dil, self.npairs, self.nt)

    def tile(self, width, col):
        return pl.BlockSpec((None, self.t, width), lambda r, hp, i: (r, i, col(hp)))

    def halo(self, width, col):
        t, blk, nbl = self.t, self.blk, self.l // self.blk
        per = t // blk
        return [
            pl.BlockSpec((None, blk, width), lambda r, hp, i: (r, jnp.maximum(i * per - 1, 0), col(hp))),
            self.tile(width, col),
            pl.BlockSpec((None, blk, width), lambda r, hp, i: (r, jnp.minimum((i + 1) * per, nbl - 1), col(hp))),
        ]

    def qcol(self, e):
        return lambda hp: self.qb0 + 2 * hp + e

    def kcol(self, hp):
        return self.kb0 + hp

    def vcol(self, hp):
        return self.vb0 + hp

    def pcol(self, hp):
        return hp


def _stack_heads(x):
    return jnp.concatenate([x[:, g * DH:(g + 1) * DH] for g in range(GQ)], axis=0)


def _unstack_heads(x, rows):
    return jnp.concatenate([x[g * rows:(g + 1) * rows] for g in range(GQ)], axis=1)


def _head_cols(tile, hh, rows):
    return jnp.concatenate([tile[:, hh * GQ + g:hh * GQ + g + 1] for g in range(GQ)], axis=0)


def _carrying(body, n_in, n_out, n_scratch, carry, kind, grid):
    nc = len(carry)
    if not nc:
        return body, [], [], [], []
    n_res = nc if kind == "gather" else 2 * nc

    def wrapped(*refs):
        ins, src = refs[:n_in], refs[n_in:n_in + nc]
        outs = refs[n_in + nc:n_in + nc + n_out]
        res = refs[n_in + nc + n_out:n_in + nc + n_out + n_res]
        scr = refs[n_in + nc + n_out + n_res:n_in + nc + n_out + n_res + n_scratch]
        sems = refs[n_in + nc + n_out + n_res + n_scratch:]
        if kind == "gather":
            start, finish = _gather_direct(src, res, *sems)
        else:
            start, finish = _exchange(src, res[:nc], res[nc:], *sems)
        ids = [pl.program_id(a) for a in range(len(grid))]
        first = functools.reduce(jnp.logical_and, [i == 0 for i in ids])
        last = functools.reduce(jnp.logical_and, [i == g - 1 for i, g in zip(ids, grid)])
        pl.when(first)(start)
        body(*ins, *outs, *scr)
        pl.when(last)(finish)

    if kind == "gather":
        shapes = [jax.ShapeDtypeStruct((4,) + tuple(c.shape), c.dtype) for c in carry]
        sems = _gather_scratch(nc)
    else:
        shapes = [jax.ShapeDtypeStruct(c.shape, c.dtype) for c in carry] * 2
        sems = _exchange_scratch(nc)
    return wrapped, [ANY] * nc, [ANY] * n_res, shapes, sems


def attn_fwd(qkv, sinkcol, cfg, carry=(), *, out_dtype, name):
    blk, t, nb, nt, dil = cfg.blk, cfg.t, cfg.nb, cfg.nt, cfg.dil
    scale = DH ** -0.5

    def body(q0, q1, kp, km, kn, vp, vm, vn, bias_ref, sink_ref, o_ref, lse_ref, kx, vx):
        ti = pl.program_id(2)
        first, last = ti == 0, ti == nt - 1
        for hh in range(2):
            sl = slice(hh * DH, (hh + 1) * DH)
            for dst, (p_, m_, n_) in ((kx, (kp, km, kn)), (vx, (vp, vm, vn))):
                dst[hh, 0:blk] = p_[:, sl]
                dst[hh, blk:blk + t] = m_[:, sl]
                dst[hh, blk + t:] = n_[:, sl]
        krow = lax.broadcasted_iota(jnp.int32, (3 * blk, GQ * blk), 0)
        pairs = [(b, hh) for b in range(nb) for hh in range(2)]
        qs = [_stack_heads((q0, q1)[hh][b * blk:(b + 1) * blk, :]) * scale for b, hh in pairs]
        sc = [lax.dot_general(kx[hh, b * blk:(b + 3) * blk, :], q_, (((1,), (1,)), ((), ())), preferred_element_type=f32)
              for q_, (b, hh) in zip(qs, pairs)]
        sc = [s_ + bias_ref[0, hh] for s_, (b, hh) in zip(sc, pairs)]
        sc = [jnp.where(jnp.logical_and(first, krow < blk), NEG, s_) if b == 0 else s_ for s_, (b, hh) in zip(sc, pairs)]
        sc = [jnp.where(jnp.logical_and(last, krow >= 2 * blk), NEG, s_) if b == nb - 1 else s_ for s_, (b, hh) in zip(sc, pairs)]
        ms = [jnp.maximum(jnp.max(s_, axis=0, keepdims=True), sink_ref[0, hh]) for s_, (b, hh) in zip(sc, pairs)]
        ps = [jnp.exp(s_ - m_) for s_, m_ in zip(sc, ms)]
        ls = [jnp.sum(p_, axis=0, keepdims=True) + jnp.exp(sink_ref[0, hh] - m_) for p_, m_, (b, hh) in zip(ps, ms, pairs)]
        os_ = [lax.dot_general(vx[hh, b * blk:(b + 3) * blk, :], p_.astype(bf16), (((0,), (0,)), ((), ())),
                               preferred_element_type=f32) for p_, (b, hh) in zip(ps, pairs)]
        os_ = [o_ / l_ for o_, l_ in zip(os_, ls)]
        lses = [m_ + jnp.log(l_) for m_, l_ in zip(ms, ls)]
        for o_, (b, hh) in zip(os_, pairs):
            o_ref[b * blk:(b + 1) * blk, hh * 256:(hh + 1) * 256] = _unstack_heads(o_.T, blk).astype(out_dtype)
        stat_rows = [jnp.concatenate([lses[2 * b + hh][:, g * blk:(g + 1) * blk] for b in range(nb)], axis=1)
                     for hh in range(2) for g in range(GQ)]
        lse_ref[...] = jnp.concatenate(stat_rows + [jnp.zeros((128 - 2 * GQ, t), f32)], axis=0).T

    in_specs = [cfg.tile(256, cfg.qcol(e)) for e in range(2)]
    in_specs += cfg.halo(128, cfg.kcol) + cfg.halo(128, cfg.vcol)
    in_specs += [pl.BlockSpec((1, 2, 3 * blk, GQ * blk), lambda r, hp, i: (hp, 0, 0, 0)),
                 pl.BlockSpec((1, 2, 1, GQ * blk), lambda r, hp, i: (hp, 0, 0, 0))]
    body, c_in, c_out, c_shape, c_sems = _carrying(body, 10, 2, 2, carry, "gather", cfg.grid())
    return _pcall(
        body, name=name, grid=cfg.grid(), in_specs=in_specs + c_in,
        out_specs=[cfg.tile(512, cfg.pcol), cfg.tile(128, cfg.pcol)] + c_out,
        out_shape=[jax.ShapeDtypeStruct((dil, cfg.l, cfg.npairs * 512), out_dtype),
                   jax.ShapeDtypeStruct((dil, cfg.l, cfg.npairs * 128), f32)] + c_shape,
        scratch_shapes=[pltpu.VMEM((2, t + 2 * blk, DH), bf16), pltpu.VMEM((2, t + 2 * blk, DH), bf16)] + c_sems,
        compiler_params=_params("arbitrary", "arbitrary", "arbitrary"),
    )(*([qkv] * 8), jnp.asarray(cfg.bias), sinkcol, *carry)


def attn_bwd(qkv, do, lse, delta, cfg, carry=(), *, name):
    blk, t, nb, nt, dil, npairs = cfg.blk, cfg.t, cfg.nb, cfg.nt, cfg.dil, cfg.npairs
    scale = DH ** -0.5
    nt_dims = (((1,), (1,)), ((), ()))
    tn_dims = (((0,), (0,)), ((), ()))

    def body(q0p, q0m, q0n, q1p, q1m, q1n, kp, km, kn, vp, vm, vn, dop, dom, don, lp, lm, ln, dp_, dm_, dn_,
             bias_ref, dq_ref, dk_ref, dv_ref, kx, vx, dkx, dvx):
        ti = pl.program_id(2)
        first, last = ti == 0, ti == nt - 1
        for hh in range(2):
            sl = slice(hh * DH, (hh + 1) * DH)
            for dst, (p_, m_, n_) in ((kx, (kp, km, kn)), (vx, (vp, vm, vn))):
                dst[hh, 0:blk] = p_[:, sl]
                dst[hh, blk:blk + t] = m_[:, sl]
                dst[hh, blk + t:] = n_[:, sl]
        dkx[...] = jnp.zeros_like(dkx)
        dvx[...] = jnp.zeros_like(dvx)

        def slab(prev, main, nxt, e):
            if e == 0:
                return prev[...]
            if e == nb + 1:
                return nxt[...]
            return main[(e - 1) * blk:e * blk, :]

        krow = lax.broadcasted_iota(jnp.int32, (3 * blk, GQ * blk), 0)

        def keys(e):
            if e == 0:
                return 1, 2, slice(2 * blk, 3 * blk)
            if e == nb + 1:
                return nb, nb + 1, slice(0, blk)
            return e - 1, e + 2, slice(0, 3 * blk)

        def edge(sc, e):
            if e == 0:
                return jnp.where(first, NEG, sc)
            if e == nb + 1:
                return jnp.where(last, NEG, sc)
            if e == 1:
                sc = jnp.where(jnp.logical_and(first, krow < blk), NEG, sc)
            if e == nb:
                sc = jnp.where(jnp.logical_and(last, krow >= 2 * blk), NEG, sc)
            return sc

        lse_t = [lp[...].T, lm[...].T, ln[...].T]
        dl_t = [dp_[...].T, dm_[...].T, dn_[...].T]

        def stat_row(parts, e, hh):
            src, lo = (parts[0], 0) if e == 0 else (parts[2], 0) if e == nb + 1 else (parts[1], (e - 1) * blk)
            return jnp.concatenate([src[hh * GQ + g:hh * GQ + g + 1, lo:lo + blk] for g in range(GQ)], axis=1)

        pairs = [(e, hh) for e in range(nb + 2) for hh in range(2)]
        qs = [_stack_heads(slab(*((q0p, q0m, q0n), (q1p, q1m, q1n))[hh], e)) * scale for e, hh in pairs]
        dos = [_stack_heads(slab(dop, dom, don, e)[:, hh * 256:(hh + 1) * 256]) for e, hh in pairs]
        lse_r = [stat_row(lse_t, e, hh) for e, hh in pairs]
        dl_r = [stat_row(dl_t, e, hh) for e, hh in pairs]
        kw = [kx[hh, keys(e)[0] * blk:keys(e)[1] * blk, :] for e, hh in pairs]
        vw = [vx[hh, keys(e)[0] * blk:keys(e)[1] * blk, :] for e, hh in pairs]
        sc = [lax.dot_general(k_, q_, nt_dims, preferred_element_type=f32) for q_, k_ in zip(qs, kw)]
        dp = [lax.dot_general(v_, d_, nt_dims, preferred_element_type=f32) for d_, v_ in zip(dos, vw)]
        sc = [edge(s_ + bias_ref[0, hh, keys(e)[2], :], e) for s_, (e, hh) in zip(sc, pairs)]
        ps = [jnp.exp(s_ - l_) for s_, l_ in zip(sc, lse_r)]
        ds = [(p_ * (d_ - c_)).astype(bf16) for p_, d_, c_ in zip(ps, dp, dl_r)]
        pb = [p_.astype(bf16) for p_ in ps]
        dks = [jnp.dot(s_, q_, preferred_element_type=f32) for s_, q_ in zip(ds, qs)]
        dvs = [jnp.dot(p_, d_, preferred_element_type=f32) for p_, d_ in zip(pb, dos)]
        dqs = [lax.dot_general(s_, k_, tn_dims, preferred_element_type=f32) if 1 <= e <= nb else None
               for s_, k_, (e, hh) in zip(ds, kw, pairs)]
        for dk_, dv_, dq_, (e, hh) in zip(dks, dvs, dqs, pairs):
            k0, k1, _ = keys(e)
            dkx[hh, k0 * blk:k1 * blk, :] += dk_
            dvx[hh, k0 * blk:k1 * blk, :] += dv_
            if dq_ is not None:
                dq_ref[(e - 1) * blk:e * blk, hh * 256:(hh + 1) * 256] = (_unstack_heads(dq_, blk) * scale).astype(bf16)
        for hh in range(2):
            dk_ref[:, hh * DH:(hh + 1) * DH] = dkx[hh, blk:blk + t, :].astype(bf16)
            dv_ref[:, hh * DH:(hh + 1) * DH] = dvx[hh, blk:blk + t, :].astype(bf16)

    in_specs = cfg.halo(256, cfg.qcol(0)) + cfg.halo(256, cfg.qcol(1))
    in_specs += cfg.halo(128, cfg.kcol) + cfg.halo(128, cfg.vcol)
    in_specs += cfg.halo(512, cfg.pcol) + cfg.halo(128, cfg.pcol) + cfg.halo(128, cfg.pcol)
    in_specs += [pl.BlockSpec((1, 2, 3 * blk, GQ * blk), lambda r, hp, i: (hp, 0, 0, 0))]
    body, c_in, c_out, c_shape, c_sems = _carrying(body, 22, 3, 4, carry, "exchange", cfg.grid())
    res = _pcall(
        body, name=name, grid=cfg.grid(), in_specs=in_specs + c_in,
        out_specs=[cfg.tile(512, cfg.pcol), cfg.tile(128, cfg.pcol), cfg.tile(128, cfg.pcol)] + c_out,
        out_shape=[jax.ShapeDtypeStruct((dil, cfg.l, npairs * 512), bf16),
                   jax.ShapeDtypeStruct((dil, cfg.l, npairs * 128), bf16),
                   jax.ShapeDtypeStruct((dil, cfg.l, npairs * 128), bf16)] + c_shape,
        scratch_shapes=[pltpu.VMEM((2, t + 2 * blk, DH), bf16), pltpu.VMEM((2, t + 2 * blk, DH), bf16),
                        pltpu.VMEM((2, t + 2 * blk, DH), f32), pltpu.VMEM((2, t + 2 * blk, DH), f32)] + c_sems,
        compiler_params=_params("arbitrary", "arbitrary", "arbitrary"),
    )(*([qkv] * 12), do, do, do, lse, lse, lse, delta, delta, delta, jnp.asarray(cfg.bias), *carry)
    nc = len(carry)
    return (res[0], res[1], res[2], res[3:3 + nc], res[3 + nc:]) if nc else res


def _head_indicator(nheads):
    e = np.zeros((nheads * DH, (nheads // 8) * 128), np.float32)
    for c in range(nheads * DH):
        h = c // DH
        e[c, (h // 8) * 128 + h % 8] = 1.0
    return e


def _dot_split(x, e):
    hi = x.astype(bf16)
    lo = (x - hi.astype(f32)).astype(bf16)
    return jnp.dot(hi, e, preferred_element_type=f32) + jnp.dot(lo, e, preferred_element_type=f32)


def _spread(scr, x, d):
    tm, w = x.shape
    for j in range(w // 128):
        scr[j] = x[:, j * 128:(j + 1) * 128]
    return [jnp.concatenate([scr[j, pl.ds(r, tm // d, stride=d), :] for j in range(w // 128)], axis=1) for r in range(d)]


def _weave(scr, blocks, d):
    n, w = blocks[0].shape
    for r in range(d):
        for j in range(w // 128):
            scr[j, pl.ds(r, n, stride=d), :] = blocks[r][:, j * 128:(j + 1) * 128]
    return jnp.concatenate([scr[j] for j in range(w // 128)], axis=1)


def _res_spec(d, tm, w):
    return pl.BlockSpec((d, tm // d, w), lambda i: (0, i, 0))


DILATED = tuple(dil for _, dil in B_GROUPS[1:])


def b_to_strided(qkv, *, name):
    s = qkv.shape[0]
    tm = _row_tile(s)

    def body(x_ref, *rest):
        outs, scr = rest[:-1], rest[-1]
        for gi, (o_ref, d) in enumerate(zip(outs, DILATED), start=1):
            cols = jnp.concatenate([x_ref[:, gi * 512:(gi + 1) * 512], x_ref[:, 1536 + gi * 128:1536 + (gi + 1) * 128],
                                    x_ref[:, 1920 + gi * 128:1920 + (gi + 1) * 128]], axis=1).astype(f32)
            for r, blk_ in enumerate(_spread(scr, cols, d)):
                o_ref[r] = blk_.astype(bf16)

    return _pcall(
        body, name=name, grid=(s // tm,), in_specs=[pl.BlockSpec((tm, B_QKV), lambda i: (i, 0))],
        out_specs=[_res_spec(d, tm, 768) for d in DILATED],
        out_shape=[jax.ShapeDtypeStruct((d, s // d, 768), bf16) for d in DILATED],
        scratch_shapes=[pltpu.VMEM((6, tm, 128), f32)], compiler_params=_params("parallel"),
    )(qkv)


def b_bwd_to_strided(do, lse, delta, *, name):
    s = do.shape[0]
    tm = _row_tile(s)

    def body(do_ref, lse_ref, dl_ref, *rest):
        outs, scr = rest[:-1], rest[-1]
        allc = jnp.concatenate([do_ref[...].astype(f32), lse_ref[...], dl_ref[...]], axis=1)
        for gi, d in enumerate(DILATED):
            o_do, o_lse, o_dl = outs[3 * gi:3 * gi + 3]
            for r, blk_ in enumerate(_spread(scr, allc, d)):
                o_do[r] = blk_[:, :512].astype(bf16)
                o_lse[r] = blk_[:, 512:640]
                o_dl[r] = blk_[:, 640:768]

    out_specs, out_shape = [], []
    for d in DILATED:
        out_specs += [_res_spec(d, tm, 512), _res_spec(d, tm, 128), _res_spec(d, tm, 128)]
        out_shape += [jax.ShapeDtypeStruct((d, s // d, 512), bf16), jax.ShapeDtypeStruct((d, s // d, 128), f32),
                      jax.ShapeDtypeStruct((d, s // d, 128), f32)]
    return _pcall(
        body, name=name, grid=(s // tm,),
        in_specs=[pl.BlockSpec((tm, 512), lambda i: (i, 0)), pl.BlockSpec((tm, 128), lambda i: (i, 0)),
                  pl.BlockSpec((tm, 128), lambda i: (i, 0))],
        out_specs=out_specs, out_shape=out_shape, scratch_shapes=[pltpu.VMEM((6, tm, 128), f32)],
        compiler_params=_params("parallel"),
    )(do, lse, delta)


def b_from_strided(grads, *, name):
    s = grads[0][0].shape[1]
    tm = _row_tile(s)

    def body(*refs):
        ins, o_ref, scr = refs[:9], refs[9], refs[10]
        nat = [jnp.concatenate([ins[q][0].astype(f32) for q in range(3)], axis=1)]
        for gi, d in enumerate(DILATED, start=1):
            blocks = [jnp.concatenate([ins[3 * gi + q][r].astype(f32) for q in range(3)], axis=1) for r in range(d)]
            nat.append(_weave(scr, blocks, d))
        pieces = [nat[g][:, lo:hi] for lo, hi in ((0, 512), (512, 640), (640, 768)) for g in range(3)]
        o_ref[...] = jnp.concatenate(pieces, axis=1).astype(bf16)

    dils = (1,) + DILATED
    in_specs = [_res_spec(d, tm, w) for d in dils for w in (512, 128, 128)]
    return _pcall(
        body, name=name, grid=(s * 1 // tm,), in_specs=in_specs, out_specs=pl.BlockSpec((tm, B_QKV), lambda i: (i, 0)),
        out_shape=jax.ShapeDtypeStruct((s, B_QKV), bf16), scratch_shapes=[pltpu.VMEM((6, tm, 128), f32)],
        compiler_params=_params("parallel"),
    )(*[a for g in grads for a in g])


def attn_merge(os_, lses, *, name):
    s = os_[0].shape[1]
    tm = _row_tile(s)
    ind_t = jnp.asarray(_head_indicator(8).T, dtype=bf16)
    dils = (1,) + DILATED

    def body(o0, o1, o2, l0, l1, l2, e_ref, o_ref, lse_ref, scr):
        both = [jnp.concatenate([o0[0], l0[0]], axis=1)]
        for og, lg, d in ((o1, l1, dils[1]), (o2, l2, dils[2])):
            both.append(_weave(scr, [jnp.concatenate([og[r], lg[r]], axis=1) for r in range(d)], d))
        ls = [b[:, 512:640] for b in both]
        m = jnp.maximum(jnp.maximum(ls[0], ls[1]), ls[2])
        tot = m + jnp.log(jnp.exp(ls[0] - m) + jnp.exp(ls[1] - m) + jnp.exp(ls[2] - m))
        lse_ref[...] = tot
        acc = jnp.zeros((tm, B_OUT), f32)
        for b, lg in zip(both, ls):
            acc = acc + _dot_split(jnp.exp(lg - tot), e_ref[...]) * b[:, :512]
        o_ref[...] = acc.astype(bf16)

    return _pcall(
        body, name=name, grid=(s * 1 // tm,),
        in_specs=[_res_spec(d, tm, 512) for d in dils] + [_res_spec(d, tm, 128) for d in dils]
        + [pl.BlockSpec((128, B_OUT), lambda i: (0, 0))],
        out_specs=[pl.BlockSpec((tm, B_OUT), lambda i: (i, 0)), pl.BlockSpec((tm, 128), lambda i: (i, 0))],
        out_shape=[jax.ShapeDtypeStruct((s, B_OUT), bf16), jax.ShapeDtypeStruct((s, 128), f32)],
        scratch_shapes=[pltpu.VMEM((5, tm, 128), f32)], compiler_params=_params("parallel"),
    )(*os_, *lses, ind_t)


def ada_mod(c_all, w, b, *, name):
    n = w.shape[2]

    def body(c_ref, w_ref, b_ref, o_ref):
        cv = c_ref[...]
        cond = cv * jax.nn.sigmoid(cv)
        o_ref[0] = jnp.dot(cond, w_ref[0], preferred_element_type=f32, precision=lax.Precision.HIGHEST) + b_ref[0]

    return _pcall(
        body, name=name, grid=(DEPTH,),
        in_specs=[pl.BlockSpec((N_DEV, D), lambda i: (0, 0)), pl.BlockSpec((1, D, n), lambda i: (i, 0, 0)),
                  pl.BlockSpec((1, 1, n), lambda i: (i, 0, 0))],
        out_specs=pl.BlockSpec((1, N_DEV, n), lambda i: (i, 0, 0)),
        out_shape=jax.ShapeDtypeStruct((DEPTH, N_DEV, n), f32), compiler_params=_params("arbitrary"),
    )(c_all, w, b)


def ada_grad(c_t, dm, *, name):
    n = dm.shape[2]

    def body(c_ref, dm_ref, o_ref):
        cv = c_ref[...]
        cond = cv * jax.nn.sigmoid(cv)
        acc = cond[:, 0:1] * dm_ref[0, 0:1, :]
        for b in range(1, N_DEV):
            acc = acc + cond[:, b:b + 1] * dm_ref[0, b:b + 1, :]
        o_ref[0] = acc

    return _pcall(
        body, name=name, grid=(DEPTH,),
        in_specs=[pl.BlockSpec((D, N_DEV), lambda i: (0, 0)), pl.BlockSpec((1, N_DEV, n), lambda i: (i, 0, 0))],
        out_specs=pl.BlockSpec((1, D, n), lambda i: (i, 0, 0)),
        out_shape=jax.ShapeDtypeStruct((DEPTH, D, n), f32), compiler_params=_params("arbitrary"),
    )(c_t, dm)


def _adam_math(w, g, m, v):
    m2 = B1 * m + (1.0 - B1) * g
    v2 = B2 * v + (1.0 - B2) * (g * g)
    mh = m2 / (1.0 - B1 ** STEP)
    vh = v2 / (1.0 - B2 ** STEP)
    return -LR * (mh / (jnp.sqrt(vh) + ADAM_EPS) + WD * w), m2, v2


def adamw(w, m, v, g, *, name):
    r, c = w.shape
    tr = 256 if r % 256 == 0 else r

    def body(w_ref, m_ref, v_ref, g_ref, d_ref, m2_ref, v2_ref):
        d_ref[...], m2_ref[...], v2_ref[...] = _adam_math(w_ref[...], g_ref[...], m_ref[...], v_ref[...])

    spec = pl.BlockSpec((tr, c), lambda i: (i, 0))
    return _pcall(
        body, name=name, grid=(r // tr,), in_specs=[spec] * 4, out_specs=[spec] * 3,
        out_shape=[jax.ShapeDtypeStruct((r, c), f32)] * 3, compiler_params=_params("parallel"),
    )(w, m, v, g)


def adamw_parts(w, m, v, own, sib, layer, prev=None, *, name):
    c = w.shape[1]
    r = own.shape[1]
    tr = 256 if r % 256 == 0 else r // 2
    off = layer * (r // tr)

    def body(w_ref, m_ref, v_ref, own_ref, sib_ref, *rest):
        g_ref, d_ref, m2_ref, v2_ref = rest[-4:]

        def total(ref):
            return ((ref[0].astype(f32) + ref[1].astype(f32)) + ref[2].astype(f32)) + ref[3].astype(f32)

        g = total(own_ref) + total(sib_ref)
        g_ref[...] = g
        d_ref[...], m2_ref[...], v2_ref[...] = _adam_math(w_ref[...], g, m_ref[...], v_ref[...])

    spec = pl.BlockSpec((tr, c), lambda i: (off + i, 0))
    pspec = pl.BlockSpec((4, tr, c), lambda i: (0, i, 0))
    prev = () if prev is None else tuple(prev)
    return _pcall(
        body, name=name, grid=(r // tr,), in_specs=[spec] * 3 + [pspec] * 2 + [ANY] * len(prev), out_specs=[spec] * 4,
        out_shape=[jax.ShapeDtypeStruct(w.shape, f32)] * 4,
        input_output_aliases={5 + q: q for q in range(len(prev))}, compiler_params=_params("parallel"),
    )(w, m, v, own, sib, *prev)


def sum_devices(g, *, name):
    _, r, c = g.shape

    def body(g_ref, o_ref):
        acc = g_ref[0]
        for k in range(1, N_DEV):
            acc = acc + g_ref[k]
        o_ref[...] = acc

    return _pcall(body, name=name, out_shape=jax.ShapeDtypeStruct((r, c), f32))(g)


def _place():
    x, y, c = lax.axis_index("x"), lax.axis_index("y"), lax.axis_index("c")
    chips = [(1 - x, y), (x, 1 - y), (1 - x, 1 - y)]
    return x, y, c, chips


def allgather8(v, *, name):
    r, c_ = v.shape

    def body(v_ref, o_ref, send_sems, recv_sems, local_sem):
        x, y, c, _ = _place()
        me = 4 * x + 2 * y + c
        mine = pltpu.make_async_copy(v_ref, o_ref.at[me], local_sem)
        mine.start()
        flips = [(fx, fy, fc) for fx in (0, 1) for fy in (0, 1) for fc in (0, 1)][1:]

        def peer(f):
            return (x ^ f[0], y ^ f[1], c ^ f[2])

        def copy(k, slot, to):
            return pltpu.make_async_remote_copy(
                src_ref=v_ref, dst_ref=o_ref.at[slot], send_sem=send_sems.at[k], recv_sem=recv_sems.at[k],
                device_id=to, device_id_type=MESH)

        sends = [copy(k, me, peer(f)) for k, f in enumerate(flips)]
        for cp in sends:
            cp.start()
        for k, f in enumerate(flips):
            px, py, pc = peer(f)
            copy(k, 4 * px + 2 * py + pc, (x, y, c)).wait_recv()
        for cp in sends:
            cp.wait_send()
        mine.wait()

    return _pcall(
        body, name=name, in_specs=[ANY], out_specs=ANY, out_shape=jax.ShapeDtypeStruct((N_DEV, r, c_), v.dtype),
        scratch_shapes=[pltpu.SemaphoreType.DMA((7,)), pltpu.SemaphoreType.DMA((7,)), pltpu.SemaphoreType.DMA],
    )(v)


def gather_weights(shards, *, name):
    n = len(shards)

    def body(*refs):
        src, out = refs[:n], refs[n:2 * n]
        send_a, recv_a, send_f, recv_f, local_sems = refs[2 * n:]
        x, y, c, chips = _place()
        sib = (x, y, 1 - c)
        me = 2 * x + y
        locals_ = [pltpu.make_async_copy(src[a], out[a].at[me], local_sems.at[a]) for a in range(n)]
        for cp in locals_:
            cp.start()

        def half(a, which):
            rh = src[a].shape[0] // 2
            return pl.ds(which * rh, rh)

        def first(a, k, chip_from, to):
            slot = 2 * chip_from[0] + chip_from[1]
            s_ref = src[a].at[half(a, c)]
            return pltpu.make_async_remote_copy(
                src_ref=s_ref, dst_ref=out[a].at[slot, half(a, c)], send_sem=send_a.at[3 * a + k],
                recv_sem=recv_a.at[3 * a + k], device_id=to, device_id_type=MESH)

        def passed(a, k, chip_from, which, to):
            slot = 2 * chip_from[0] + chip_from[1]
            ref = out[a].at[slot, half(a, which)]
            return pltpu.make_async_remote_copy(
                src_ref=ref, dst_ref=ref, send_sem=send_f.at[3 * a + k], recv_sem=recv_f.at[3 * a + k],
                device_id=to, device_id_type=MESH)

        sends = [first(a, k, (x, y), (*chip, c)) for a in range(n) for k, chip in enumerate(chips)]
        for cp in sends:
            cp.start()
        fwd = []
        for a in range(n):
            for k, chip in enumerate(chips):
                first(a, k, chip, (x, y, c)).wait_recv()
                cp = passed(a, k, chip, c, sib)
                cp.start()
                fwd.append(cp)
        for a in range(n):
            for k, chip in enumerate(chips):
                passed(a, k, chip, 1 - c, (x, y, c)).wait_recv()
        for cp in sends + fwd:
            cp.wait_send()
        for cp in locals_:
            cp.wait()

    return _pcall(
        body, name=name, in_specs=[ANY] * n, out_specs=[ANY] * n,
        out_shape=[jax.ShapeDtypeStruct((4,) + tuple(sh.shape), sh.dtype) for sh in shards],
        scratch_shapes=[pltpu.SemaphoreType.DMA((3 * n,)) for _ in range(4)] + [pltpu.SemaphoreType.DMA((n,))],
    )(*shards)


def _gather_direct(src, out, send_sems, recv_sems, local_sems):
    n = len(src)
    x, y, c, chips = _place()
    me = 2 * x + y

    def copy(a, k, slot, to):
        return pltpu.make_async_remote_copy(
            src_ref=src[a], dst_ref=out[a].at[slot], send_sem=send_sems.at[3 * a + k], recv_sem=recv_sems.at[3 * a + k],
            device_id=to, device_id_type=MESH)

    def start():
        for a in range(n):
            pltpu.make_async_copy(src[a], out[a].at[me], local_sems.at[a]).start()
            for k, chip in enumerate(chips):
                copy(a, k, me, (*chip, c)).start()

    def finish():
        for a in range(n):
            for k, chip in enumerate(chips):
                copy(a, k, 2 * chip[0] + chip[1], (x, y, c)).wait_recv()
        for a in range(n):
            for k in range(3):
                copy(a, k, me, (x, y, c)).wait_send()
            pltpu.make_async_copy(src[a], out[a].at[me], local_sems.at[a]).wait()

    return start, finish


def _gather_scratch(n):
    return [pltpu.SemaphoreType.DMA((3 * n,)), pltpu.SemaphoreType.DMA((3 * n,)), pltpu.SemaphoreType.DMA((n,))]


def _exchange(src, own, sibo, send_sems, recv_sems, local_sems):
    n = len(src)
    x, y, c, chips = _place()
    sib = (x, y, 1 - c)
    me = 2 * x + y

    def slot(chip):
        return 2 * chip[0] + chip[1]

    def copy(a, k, s_ref, d_ref, to):
        return pltpu.make_async_remote_copy(
            src_ref=s_ref, dst_ref=d_ref, send_sem=send_sems.at[7 * a + k], recv_sem=recv_sems.at[7 * a + k],
            device_id=to, device_id_type=MESH)

    def start():
        for a in range(n):
            pltpu.make_async_copy(src[a].at[me], own[a].at[me], local_sems.at[a]).start()
            copy(a, 0, src[a].at[me], sibo[a].at[me], sib).start()
            for k, chip in enumerate(chips):
                copy(a, 1 + k, src[a].at[slot(chip)], own[a].at[me], (*chip, c)).start()

    def finish():
        for a in range(n):
            for k, chip in enumerate(chips):
                copy(a, 1 + k, src[a].at[me], own[a].at[slot(chip)], (x, y, c)).wait_recv()
                copy(a, 4 + k, own[a].at[slot(chip)], sibo[a].at[slot(chip)], sib).start()
        for a in range(n):
            copy(a, 0, src[a].at[me], sibo[a].at[me], (x, y, c)).wait_recv()
            for k, chip in enumerate(chips):
                copy(a, 4 + k, src[a].at[me], sibo[a].at[slot(chip)], (x, y, c)).wait_recv()
        for a in range(n):
            for k in range(7):
                copy(a, k, src[a].at[me], own[a].at[me], (x, y, c)).wait_send()
            pltpu.make_async_copy(src[a].at[me], own[a].at[me], local_sems.at[a]).wait()

    return start, finish


def _exchange_scratch(n):
    return [pltpu.SemaphoreType.DMA((7 * n,)), pltpu.SemaphoreType.DMA((7 * n,)), pltpu.SemaphoreType.DMA((n,))]


def exchange_grads(parts, *, name):
    n = len(parts)

    def body(*refs):
        start, finish = _exchange(refs[:n], refs[n:2 * n], refs[2 * n:3 * n], *refs[3 * n:])
        start()
        finish()

    shapes = [jax.ShapeDtypeStruct(p.shape, p.dtype) for p in parts]
    res = _pcall(body, name=name, in_specs=[ANY] * n, out_specs=[ANY] * (2 * n), out_shape=shapes + shapes,
                 scratch_shapes=_exchange_scratch(n))(*parts)
    return res[:n], res[n:]


def _natural(g, how):
    if how == "col":
        return jnp.moveaxis(g, 0, 1).reshape(g.shape[1], 4 * g.shape[2])
    return g.reshape(4 * g.shape[1], g.shape[2])


def _chunks(gw, how):
    k, n = gw.shape
    if how == "col":
        return jnp.moveaxis(gw.reshape(k, 4, n // 4), 1, 0).astype(bf16)
    return gw.reshape(4, k // 4, n).astype(bf16)


def kernel(x, c, ada_w, ada_b, norm_mix, norm_ffn, ffn_w_in, ffn_w_out, a_w_in, a_w_out, a_sink, b_w_in, b_w_out, final_norm, loss_target, m_ada_w, m_ada_b, m_norm_mix, m_norm_ffn, m_ffn_w_in, m_ffn_w_out, m_a_w_in, m_a_w_out, m_a_sink, m_b_w_in, m_b_w_out, m_final_norm, v_ada_w, v_ada_b, v_norm_mix, v_norm_ffn, v_ffn_w_in, v_ffn_w_out, v_a_w_in, v_a_w_out, v_a_sink, v_b_w_in, v_b_w_out, v_final_norm):
    s = x.shape[1]
    xi, yi, ci = lax.axis_index("x"), lax.axis_index("y"), lax.axis_index("c")
    chip = 2 * xi + yi
    dev = 2 * chip + ci
    x0 = x[0]
    tgt = loss_target[0]

    big = {"ffn_w_in": (ffn_w_in, "col"), "ffn_w_out": (ffn_w_out, "row"), "a_w_in": (a_w_in, "col"),
           "a_w_out": (a_w_out, "row"), "b_w_in": (b_w_in, "col"), "b_w_out": (b_w_out, "col")}
    names = list(big)

    def layer_keys(i):
        mix = "a" if i % 2 == 0 else "b"
        return [("ffn_w_in", i), ("ffn_w_out", i), (mix + "_w_in", i // 2), (mix + "_w_out", i // 2)]

    def shards_of(i):
        return [big[k][0][l].astype(bf16) for k, l in layer_keys(i)]

    def weights_of(i, gathered):
        return {k: (g if k == "ffn_w_in" else _natural(g, big[k][1])) for (k, _), g in zip(layer_keys(i), gathered)}

    mix0 = gather_weights(shards_of(0)[2:], name="gather_weights")

    c_all = allgather8(jnp.broadcast_to(c, (8, D)), name="gather_c")[:, 0, :]
    nsh = ada_w.shape[2]
    ada_b_sh = lax.dynamic_slice_in_dim(ada_b, chip * nsh, nsh, axis=1)[:, None, :]
    mod_part = ada_mod(c_all, ada_w, ada_b_sh, name="ada_mod")
    mod_all = allgather8(mod_part.reshape(DEPTH * N_DEV, nsh), name="gather_mod")
    mod_all = mod_all.reshape(4, 2, DEPTH, N_DEV, nsh)[:, 0]
    mod = lax.dynamic_index_in_dim(mod_all, dev, axis=2, keepdims=False)
    mod = jnp.moveaxis(mod, 0, 1).reshape(DEPTH, 6, 1, D)

    cfg_a = _Attn(s, mixer="a")
    cfg_b = [_Attn(s, mixer="b", group=g) for g in range(3)]
    no_sink = jnp.full((1, 2, 1, GQ * 64), NEG, f32)

    saved = []
    xc = x0
    for i in range(DEPTH):
        j = i // 2
        sh1, sc1, g1, sh2, sc2, g2 = (mod[i, q] for q in range(6))
        nmix, nffn = norm_mix[i][None, :], norm_ffn[i][None, :]
        mix = "a" if i % 2 == 0 else "b"
        if i == 0:
            h, qkv = mm_norm(xc, nmix, sc1, sh1, _natural(mix0[0], "col"), name="a_qkv")
            sinkcol = jnp.repeat(a_sink[j].reshape(2, 2, GQ), 128, axis=2)[:, :, None, :]
            o, lse, *ffn0 = attn_fwd(qkv[None], sinkcol, cfg_a, shards_of(0)[:2], out_dtype=bf16, name="a_attn_fwd_gather")
            o, lse = o[0], lse[0]
            wl = [weights_of(0, ffn0 + list(mix0))]
        elif i % 2 == 0:
            h, qkv = mm_norm(xc, nmix, sc1, sh1, wl[i]["a_w_in"], name="a_qkv")
            sinkcol = jnp.repeat(a_sink[j].reshape(2, 2, GQ), 128, axis=2)[:, :, None, :]
            o, lse = (t[0] for t in attn_fwd(qkv[None], sinkcol, cfg_a, out_dtype=bf16, name="a_attn_fwd"))
        else:
            h, qkv = mm_norm(xc, nmix, sc1, sh1, wl[i]["b_w_in"], name="b_qkv")
            qkv = [qkv[None]] + list(b_to_strided(qkv, name="b_to_strided"))
            outs = [attn_fwd(qkv[g], no_sink, cfg_b[g], out_dtype=f32, name=f"b_attn_fwd{g}") for g in range(3)]
            o, lse = attn_merge([t[0] for t in outs], [t[1] for t in outs], name="b_merge")
        x1 = mm_resid(o, wl[i][mix + "_w_out"], xc, g1, name=mix + "_out")
        nxt = shards_of(i + 1) if i + 1 < DEPTH else []
        h2, gu, act, x2, *got = ffn_fwd(x1, nffn, sc2, sh2, g2, wl[i]["ffn_w_in"], wl[i]["ffn_w_out"], nxt,
                                        name="ffn_fwd_gather" if nxt else "ffn_fwd")
        if nxt:
            wl.append(weights_of(i + 1, got))
        saved.append((xc, h, qkv, o, lse, x1, h2, gu, act))
        xc = x2

    dx, st_final = loss_head(xc, final_norm[None, :], tgt, name="loss_head")

    zero_row = jnp.zeros((1, D), f32)
    dmod_rows = [None] * DEPTH
    d_nmix, d_nffn = [None] * DEPTH, [None] * DEPTH
    d_sink = [None] * 2
    parts, exchanged = None, {}
    for i in reversed(range(DEPTH)):
        j = i // 2
        xin, h, qkv, o, lse, x1, h2, gu, act = saved[i]
        sh1, sc1, g1, sh2, sc2, g2 = (mod[i, q] for q in range(6))
        nmix, nffn = norm_mix[i][None, :], norm_ffn[i][None, :]
        mix = "a" if i % 2 == 0 else "b"
        w_fo, w_o, w_i = wl[i]["ffn_w_out"], wl[i][mix + "_w_out"], wl[i][mix + "_w_in"]
        dgu, dx1, st2, own_, sib_ = ffn_bwd_rows(dx, x1, gu, g2, nffn, sc2, wl[i]["ffn_w_in"], w_fo, parts or [],
                                                 name="ffn_bwd_rows_exchange" if parts else "ffn_bwd_rows")
        if parts:
            exchanged[i + 1] = (own_, sib_)
        gwo, dg2 = ffn_dw_out(act, dx, g2, w_fo, name="ffn_dw_out")
        dg2 = dg2[0:1]
        gwi = ffn_dw_in(h2, dgu, name="ffn_dw_in")
        gmo, dg1 = mm_tn(o, dx1, (g1, w_o), name=mix + "_dw_out")
        ffn_parts = [gwi, gwo.reshape(4, F // 4, D)]
        if i % 2 == 0:
            sinkrow = jnp.pad(a_sink[j].reshape(2, 8), ((0, 0), (0, 120))).reshape(1, 256)
            do, delta, dsk = mm_nt_delta(dx1, g1, w_o, o, lse, sinkrow, name="a_do")
            d_sink[j] = dsk[0].reshape(2, 128)[:, :8].reshape(16)
            dq, dk, dv, *ffn_x = attn_bwd(qkv[None], do[None], lse[None], delta[None], cfg_a, ffn_parts if i == 0 else [],
                                          name="a_attn_bwd_exchange" if i == 0 else "a_attn_bwd")
            dqkv = jnp.concatenate([dq[0], dk[0], dv[0]], axis=1)
        else:
            do, delta, _ = mm_nt_delta(dx1, g1, w_o, o, lse, jnp.zeros((1, 128), f32), name="b_do")
            st = [do[None], lse[None], delta[None]] + list(b_bwd_to_strided(do, lse, delta, name="b_bwd_to_strided"))
            gr = [attn_bwd(qkv[g], *st[3 * g:3 * g + 3], cfg_b[g], name=f"b_attn_bwd{g}") for g in range(3)]
            dqkv = b_from_strided(gr, name="b_from_strided")
        gmi = mm_tn(h, dqkv, name=mix + "_dw_in")
        dx, st1 = mm_nt_norm_bwd(dqkv, w_i, xin, dx1, nmix, sc1, name=mix + "_dh")
        dmod_rows[i] = jnp.concatenate([st1[2:3], st1[1:2], dg1, st2[2:3], st2[1:2], dg2], axis=0)
        d_nmix[i], d_nffn[i] = st1[0:1], st2[0:1]
        mix_parts = [_chunks(gmi, big[mix + "_w_in"][1]), _chunks(gmo, big[mix + "_w_out"][1])]
        parts = ffn_parts + mix_parts
    own_m, sib_m = exchange_grads(mix_parts, name="exchange_grads")
    exchanged[0] = (list(ffn_x[0]) + list(own_m), list(ffn_x[1]) + list(sib_m))

    sink_row = jnp.pad(jnp.concatenate(d_sink), (0, D - 32))[None, :]
    stats = jnp.concatenate(dmod_rows + d_nmix + d_nffn + [sink_row, st_final[0:1], st_final[1:2]]
                            + [zero_row] * (STAT_ROWS - 35), axis=0)
    stats_all = allgather8(stats, name="gather_stats")
    tot = sum_devices(stats_all, name="sum_stats")
    loss = 0.5 * jnp.sum(tot[34]) / float(D)

    def pack(ab, nm, nf, sk, fnm, fill):
        return jnp.concatenate([ab.reshape(24, D), nm, nf, jnp.pad(sk.reshape(1, 32), ((0, 0), (0, D - 32)), constant_values=fill),
                                fnm[None, :], jnp.full((STAT_ROWS - 34, D), fill, f32)], axis=0)

    sd, sm, sv = adamw(pack(ada_b, norm_mix, norm_ffn, a_sink, final_norm, 0.0),
                       pack(m_ada_b, m_norm_mix, m_norm_ffn, m_a_sink, m_final_norm, 0.0),
                       pack(v_ada_b, v_norm_mix, v_norm_ffn, v_a_sink, v_final_norm, 1.0), tot, name="adamw_small")

    def unpack(p):
        return p[0:24].reshape(DEPTH, 6 * D), p[24:28], p[28:32], p[32, :32].reshape(2, 16), p[33]

    small = {"grad": unpack(tot), "delta": unpack(sd), "m": unpack(sm), "v": unpack(sv)}

    dmod_all = stats_all[:, 0:24, :].reshape(N_DEV, DEPTH, 6 * D)
    dm_sh = jnp.moveaxis(lax.dynamic_slice_in_dim(dmod_all, chip * nsh, nsh, axis=2), 0, 1)
    g_ada = ada_grad(c_all.T, dm_sh, name="ada_grad")
    r_ada = (DEPTH * D, nsh)
    ada_res = adamw(ada_w.reshape(r_ada), m_ada_w.reshape(r_ada), v_ada_w.reshape(r_ada), g_ada.reshape(r_ada), name="adamw_ada")
    ada_out = [g_ada] + [t.reshape(ada_w.shape) for t in ada_res]

    mom = {"ffn_w_in": (m_ffn_w_in, v_ffn_w_in), "ffn_w_out": (m_ffn_w_out, v_ffn_w_out), "a_w_in": (m_a_w_in, v_a_w_in),
           "a_w_out": (m_a_w_out, v_a_w_out), "b_w_in": (m_b_w_in, v_b_w_in), "b_w_out": (m_b_w_out, v_b_w_out)}
    big_out = {k: None for k in names}
    for i in reversed(range(DEPTH)):
        own_, sib_ = exchanged[i]
        for (k, l), o_, s_ in zip(layer_keys(i), own_, sib_):
            w = big[k][0]
            r2 = (-1, w.shape[-1])
            big_out[k] = adamw_parts(w.reshape(r2), mom[k][0].reshape(r2), mom[k][1].reshape(r2), o_, s_, l, big_out[k],
                                     name=f"adamw_{k}{l}")
    big_out = {k: [t.reshape(big[k][0].shape) for t in big_out[k]] for k in names}

    def leaves(q):
        sm_ = small[("grad", "delta", "m", "v")[q]]
        return (ada_out[q], sm_[0], sm_[1], sm_[2], big_out["ffn_w_in"][q], big_out["ffn_w_out"][q], big_out["a_w_in"][q],
                big_out["a_w_out"][q], sm_[3], big_out["b_w_in"][q], big_out["b_w_out"][q], sm_[4])

    return (loss, dx[None], *leaves(0), *leaves(1), *leaves(2), *leaves(3))
```

```python
import functools
import math

import numpy as np
import jax
import jax.numpy as jnp
from jax import lax
from jax.experimental import pallas as pl
from jax.experimental.pallas import tpu as pltpu

f32 = jnp.float32
bf16 = jnp.bfloat16

D = 1024
DH = 64
GQ = 4
DEPTH = 4
F = 2816
A_QKV, A_OUT = 1536, 1024
B_QKV, B_OUT = 2304, 512
B_GROUPS = ((128, 1), (512, 4), (2048, 16))
RMS_EPS = 1e-6
NEG = -1e30
LR, B1, B2, ADAM_EPS, WD, STEP = 0.001, 0.9, 0.999, 1e-08, 0.01, 10
N_DEV = 8
STAT_ROWS = 40
CARRY_LEAD = 6
MESH = pl.DeviceIdType.MESH
ANY = pl.BlockSpec(memory_space=pl.ANY)


def _pcall(body, **kw):
    return pl.pallas_call(body, **kw)


def _params(*sem):
    return pltpu.CompilerParams(dimension_semantics=sem, vmem_limit_bytes=56 * 1024 * 1024)


def _row_tile(s, want=1024):
    return want if s % want == 0 else s


ROW_CHUNKS = 4


def mm_norm(x, nw, sc, sh, w, *, name):
    s, d = x.shape
    n = w.shape[1]
    tm = _row_tile(s)
    rc = tm // ROW_CHUNKS

    def body(x_ref, nw_ref, sc_ref, sh_ref, w_ref, h_ref, y_ref):
        hs = []
        for c in range(ROW_CHUNKS):
            xv = x_ref[c * rc:(c + 1) * rc, :]
            r = lax.rsqrt(jnp.mean(xv * xv, axis=-1, keepdims=True) + RMS_EPS)
            hs.append(((xv * r * nw_ref[...]) * (1.0 + sc_ref[...]) + sh_ref[...]).astype(bf16))
        ys = [jnp.dot(h, w_ref[...], preferred_element_type=f32) for h in hs]
        for c in range(ROW_CHUNKS):
            h_ref[c * rc:(c + 1) * rc, :] = hs[c]
            y_ref[c * rc:(c + 1) * rc, :] = ys[c].astype(bf16)

    vec = pl.BlockSpec((1, d), lambda i: (0, 0))
    return _pcall(
        body, name=name, grid=(s // tm,),
        in_specs=[pl.BlockSpec((tm, d), lambda i: (i, 0)), vec, vec, vec, pl.BlockSpec((d, n), lambda i: (0, 0))],
        out_specs=[pl.BlockSpec((tm, d), lambda i: (i, 0)), pl.BlockSpec((tm, n), lambda i: (i, 0))],
        out_shape=[jax.ShapeDtypeStruct((s, d), bf16), jax.ShapeDtypeStruct((s, n), bf16)],
        compiler_params=_params("parallel"),
    )(x, nw, sc, sh, w)


def mm_resid(a, w, xres, g, *, name):
    s, k = a.shape
    n = w.shape[1]
    tm = _row_tile(s)
    rc = tm // ROW_CHUNKS

    def body(a_ref, w_ref, x_ref, g_ref, o_ref):
        ys = [jnp.dot(a_ref[c * rc:(c + 1) * rc, :], w_ref[...], preferred_element_type=f32) for c in range(ROW_CHUNKS)]
        for c, y in enumerate(ys):
            o_ref[c * rc:(c + 1) * rc, :] = x_ref[c * rc:(c + 1) * rc, :] + g_ref[...] * y

    big = pl.BlockSpec((tm, n), lambda i: (i, 0))
    return _pcall(
        body, name=name, grid=(s // tm,),
        in_specs=[pl.BlockSpec((tm, k), lambda i: (i, 0)), pl.BlockSpec((k, n), lambda i: (0, 0)), big,
                  pl.BlockSpec((1, n), lambda i: (0, 0))],
        out_specs=big, out_shape=jax.ShapeDtypeStruct((s, n), f32), compiler_params=_params("parallel"),
    )(a, w, xres, g)


def mm_nt_delta(dx, g, w, o, lse, sinkrow, *, name):
    s, d = dx.shape
    n = w.shape[0]
    wd = lse.shape[1]
    tm = _row_tile(s)
    rc = tm // ROW_CHUNKS
    ind = jnp.asarray(_head_indicator(n // DH), dtype=bf16)

    def body(dx_ref, g_ref, w_ref, o_ref, lse_ref, sink_ref, e_ref, do_ref, dl_ref, ds_ref):
        rows = [slice(c * rc, (c + 1) * rc) for c in range(ROW_CHUNKS)]
        as_ = [(dx_ref[rw, :] * g_ref[...]).astype(bf16) for rw in rows]
        dos = [lax.dot_general(a, w_ref[...], (((1,), (1,)), ((), ())), preferred_element_type=f32).astype(bf16) for a in as_]
        dls = [_dot_split(do.astype(f32) * o_ref[rw, :].astype(f32), e_ref[...]) for do, rw in zip(dos, rows)]
        part = None
        for rw, do, dl in zip(rows, dos, dls):
            do_ref[rw, :] = do
            dl_ref[rw, :] = dl
            p = -jnp.sum(jnp.exp(sink_ref[...] - lse_ref[rw, :]) * dl, axis=0, keepdims=True)
            part = p if part is None else part + p
        part = jnp.concatenate([part, jnp.zeros((7, wd), f32)], axis=0)

        @pl.when(pl.program_id(0) == 0)
        def _():
            ds_ref[...] = part

        @pl.when(pl.program_id(0) != 0)
        def _():
            ds_ref[...] += part

    return _pcall(
        body, name=name, grid=(s // tm,),
        in_specs=[pl.BlockSpec((tm, d), lambda i: (i, 0)), pl.BlockSpec((1, d), lambda i: (0, 0)),
                  pl.BlockSpec((n, d), lambda i: (0, 0)), pl.BlockSpec((tm, n), lambda i: (i, 0)),
                  pl.BlockSpec((tm, wd), lambda i: (i, 0)), pl.BlockSpec((1, wd), lambda i: (0, 0)),
                  pl.BlockSpec((n, wd), lambda i: (0, 0))],
        out_specs=[pl.BlockSpec((tm, n), lambda i: (i, 0)), pl.BlockSpec((tm, wd), lambda i: (i, 0)),
                   pl.BlockSpec((8, wd), lambda i: (0, 0))],
        out_shape=[jax.ShapeDtypeStruct((s, n), bf16), jax.ShapeDtypeStruct((s, wd), f32), jax.ShapeDtypeStruct((8, wd), f32)],
        compiler_params=_params("arbitrary"),
    )(dx, g, w, o, lse, sinkrow, ind)


def mm_nt_norm_bwd(a, w, x, dres, nw, sc, *, name):
    pieces = list(a) if isinstance(a, (list, tuple)) else [a]
    npc = len(pieces)
    s = pieces[0].shape[0]
    k = sum(p.shape[1] for p in pieces)
    d = w.shape[0]
    tm = _row_tile(s)
    rc = tm // ROW_CHUNKS

    def body(*refs):
        a_refs = refs[:npc]
        w_ref, x_ref, dr_ref, nw_ref, sc_ref, o_ref, st_ref = refs[npc:]

        def a_rows(c):
            got = [r[c * rc:(c + 1) * rc, :] for r in a_refs]
            return jnp.concatenate(got, axis=1) if npc > 1 else got[0]

        dhs = [lax.dot_general(a_rows(c), w_ref[...], (((1,), (1,)), ((), ())), preferred_element_type=f32)
               for c in range(ROW_CHUNKS)]
        rows = None
        for c, dh in enumerate(dhs):
            xv = x_ref[c * rc:(c + 1) * rc, :]
            r = lax.rsqrt(jnp.mean(xv * xv, axis=-1, keepdims=True) + RMS_EPS)
            xh = xv * r
            dn = dh * (1.0 + sc_ref[...])
            dxh = dn * nw_ref[...]
            o_ref[c * rc:(c + 1) * rc, :] = dr_ref[c * rc:(c + 1) * rc, :] + r * (dxh - xh * jnp.mean(dxh * xh, axis=-1, keepdims=True))
            part = jnp.concatenate([
                jnp.sum(dn * xh, axis=0, keepdims=True),
                jnp.sum(dh * (xh * nw_ref[...]), axis=0, keepdims=True),
                jnp.sum(dh, axis=0, keepdims=True),
                jnp.zeros((5, d), f32)], axis=0)
            rows = part if rows is None else rows + part

        @pl.when(pl.program_id(0) == 0)
        def _():
            st_ref[...] = rows

        @pl.when(pl.program_id(0) != 0)
        def _():
            st_ref[...] += rows

    big = pl.BlockSpec((tm, d), lambda i: (i, 0))
    vec = pl.BlockSpec((1, d), lambda i: (0, 0))
    return _pcall(
        body, name=name, grid=(s // tm,),
        in_specs=[pl.BlockSpec((tm, p.shape[1]), lambda i: (i, 0)) for p in pieces]
        + [pl.BlockSpec((d, k), lambda i: (0, 0)), big, big, vec, vec],
        out_specs=[big, pl.BlockSpec((8, d), lambda i: (0, 0))],
        out_shape=[jax.ShapeDtypeStruct((s, d), f32), jax.ShapeDtypeStruct((8, d), f32)],
        compiler_params=_params("arbitrary"),
    )(*pieces, w, x, dres, nw, sc)


def mm_tn(a, b, scale=None, *, name):
    pieces = list(b) if isinstance(b, (list, tuple)) else [b]
    s, ka = a.shape
    nb = sum(p.shape[1] for p in pieces)
    npc = len(pieces)
    ts = _row_tile(s)
    ns = s // ts

    def body(a_ref, *rest):
        b_refs, rest = rest[:npc], rest[npc:]
        o_ref = rest[2] if scale is not None else rest[0]
        si = pl.program_id(0)
        bv = jnp.concatenate([r[...].astype(bf16) for r in b_refs], axis=1) if npc > 1 else b_refs[0][...].astype(bf16)
        part = lax.dot_general(a_ref[...], bv, (((0,), (0,)), ((), ())), preferred_element_type=f32)

        @pl.when(si == 0)
        def _():
            o_ref[...] = part

        @pl.when(si != 0)
        def _():
            o_ref[...] += part

        if scale is not None:
            g_ref, wb_ref, dg_ref = rest[0], rest[1], rest[3]

            @pl.when(si == ns - 1)
            def _():
                gm = o_ref[...]
                dg_ref[...] = jnp.sum(wb_ref[...].astype(f32) * gm, axis=0, keepdims=True)
                o_ref[...] = gm * g_ref[...]

    in_specs = [pl.BlockSpec((ts, ka), lambda k: (k, 0))] + [pl.BlockSpec((ts, p.shape[1]), lambda k: (k, 0)) for p in pieces]
    args = [a] + pieces
    whole = pl.BlockSpec((ka, nb), lambda k: (0, 0))
    out_specs = [whole]
    out_shape = [jax.ShapeDtypeStruct((ka, nb), f32)]
    if scale is not None:
        in_specs += [pl.BlockSpec((1, nb), lambda k: (0, 0)), whole]
        args += list(scale)
        out_specs.append(pl.BlockSpec((1, nb), lambda k: (0, 0)))
        out_shape.append(jax.ShapeDtypeStruct((1, nb), f32))
    res = _pcall(body, name=name, grid=(ns,), in_specs=in_specs, out_specs=out_specs, out_shape=out_shape,
                 compiler_params=_params("arbitrary"))(*args)
    return res if scale is not None else res[0]


def loss_head(x, fn, tgt, *, name):
    s, d = x.shape
    tm = _row_tile(s, 512)

    def body(x_ref, fn_ref, t_ref, dx_ref, st_ref):
        xv = x_ref[...]
        r = lax.rsqrt(jnp.mean(xv * xv, axis=-1, keepdims=True) + RMS_EPS)
        xh = xv * r
        err = xh * fn_ref[...] - t_ref[...]
        dy = err / float(d)
        dxh = dy * fn_ref[...]
        dx_ref[...] = r * (dxh - xh * jnp.mean(dxh * xh, axis=-1, keepdims=True))
        rows = jnp.concatenate([
            jnp.sum(dy * xh, axis=0, keepdims=True),
            jnp.sum(err * err, axis=0, keepdims=True),
            jnp.zeros((6, d), f32)], axis=0)

        @pl.when(pl.program_id(0) == 0)
        def _():
            st_ref[...] = rows

        @pl.when(pl.program_id(0) != 0)
        def _():
            st_ref[...] += rows

    big = pl.BlockSpec((tm, d), lambda i: (i, 0))
    return _pcall(
        body, name=name, grid=(s // tm,), in_specs=[big, pl.BlockSpec((1, d), lambda i: (0, 0)), big],
        out_specs=[big, pl.BlockSpec((8, d), lambda i: (0, 0))],
        out_shape=[jax.ShapeDtypeStruct((s, d), f32), jax.ShapeDtypeStruct((8, d), f32)],
        compiler_params=_params("arbitrary"),
    )(x, fn, tgt)


FC = 2 * F // 4
FFN_ROWS = 256
FFN_CHUNKS = 1


def _resident(pairs, sems):
    @pl.when(pl.program_id(0) == 0)
    def _():
        cps = [pltpu.make_async_copy(h, v, sems.at[i]) for i, (h, v) in enumerate(pairs)]
        for cp in cps:
            cp.start()
        for cp in cps:
            cp.wait()


def ffn_fwd(x, nw, sc, sh, g, w_in, w_out, carry=(), *, name):
    s, d = x.shape
    tm = _row_tile(s, FFN_ROWS)
    nsteps = s // tm
    nc = len(carry)

    def body(*refs):
        x_ref, nw_ref, sc_ref, sh_ref, g_ref, win_hbm, wout_hbm = refs[:7]
        h_ref, gu_ref, a_ref, o_ref = refs[7 + nc:11 + nc]
        win_v, wout_v, sems = refs[11 + 2 * nc:14 + 2 * nc]
        if nc:
            start, finish = _gather_direct(refs[7:7 + nc], refs[11 + nc:11 + 2 * nc], *refs[14 + 2 * nc:])
            pl.when(pl.program_id(0) == 0)(start)
        _resident([(win_hbm, win_v), (wout_hbm, wout_v)], sems)
        rc = tm // FFN_CHUNKS
        rows = [slice(q * rc, (q + 1) * rc) for q in range(FFN_CHUNKS)]
        xs = [x_ref[rw, :] for rw in rows]
        hs = [((xv * lax.rsqrt(jnp.mean(xv * xv, axis=-1, keepdims=True) + RMS_EPS) * nw_ref[...]) * (1.0 + sc_ref[...])
               + sh_ref[...]).astype(bf16) for xv in xs]
        pairs = [(q, c) for q in range(FFN_CHUNKS) for c in range(2)]
        gts = [jnp.dot(hs[q], win_v[c], preferred_element_type=f32) for q, c in pairs]
        ups = [jnp.dot(hs[q], win_v[c + 2], preferred_element_type=f32) for q, c in pairs]
        acts = [(gt * jax.nn.sigmoid(gt) * up).astype(bf16) for gt, up in zip(gts, ups)]
        ys = [jnp.dot(act, wout_v[c * FC:(c + 1) * FC, :], preferred_element_type=f32) for act, (q, c) in zip(acts, pairs)]
        for (q, c), gt, up, act in zip(pairs, gts, ups, acts):
            cs = slice(c * FC, (c + 1) * FC)
            gu_ref[0, rows[q], cs] = gt.astype(bf16)
            gu_ref[1, rows[q], cs] = up.astype(bf16)
            a_ref[rows[q], cs] = act
        for q in range(FFN_CHUNKS):
            h_ref[rows[q], :] = hs[q]
            o_ref[rows[q], :] = xs[q] + g_ref[...] * (ys[2 * q] + ys[2 * q + 1])
        if nc:
            pl.when(pl.program_id(0) == nsteps - 1)(finish)

    big = pl.BlockSpec((tm, d), lambda i: (i, 0))
    vec = pl.BlockSpec((1, d), lambda i: (0, 0))
    return _pcall(
        body, name=name, grid=(nsteps,), in_specs=[big, vec, vec, vec, vec, ANY, ANY] + [ANY] * nc,
        out_specs=[big, pl.BlockSpec((2, tm, F), lambda i: (0, i, 0)), pl.BlockSpec((tm, F), lambda i: (i, 0)), big] + [ANY] * nc,
        out_shape=[jax.ShapeDtypeStruct((s, d), bf16), jax.ShapeDtypeStruct((2, s, F), bf16),
                   jax.ShapeDtypeStruct((s, F), bf16), jax.ShapeDtypeStruct((s, d), f32)]
        + [jax.ShapeDtypeStruct((4,) + tuple(sh_.shape), sh_.dtype) for sh_ in carry],
        scratch_shapes=[pltpu.VMEM((4, d, FC), bf16), pltpu.VMEM((F, d), bf16), pltpu.SemaphoreType.DMA((2,))]
        + (_gather_scratch(nc) if nc else []),
        compiler_params=_params("arbitrary"),
    )(x, nw, sc, sh, g, w_in, w_out, *carry)


def ffn_bwd_rows(dx, x, gu, g, nw, sc, w_in, w_out, carry=(), *, name):
    s, d = x.shape
    tm = _row_tile(s, FFN_ROWS)
    nsteps = s // tm
    nc = len(carry)
    nt_dims = (((1,), (1,)), ((), ()))

    def body(*refs):
        dx_ref, x_ref, gu_ref, g_ref, nw_ref, sc_ref, win_hbm, wout_hbm = refs[:8]
        dgu_ref, o_ref, st_ref = refs[8 + nc:11 + nc]
        win_v, wout_v, sems = refs[11 + 3 * nc:14 + 3 * nc]
        if nc:
            start, forward, finish = _exchange(refs[8:8 + nc], refs[11 + nc:11 + 2 * nc], refs[11 + 2 * nc:11 + 3 * nc],
                                               *refs[14 + 3 * nc:])
            pl.when(pl.program_id(0) == 0)(start)
            pl.when(pl.program_id(0) == max(nsteps - 1 - CARRY_LEAD, 0))(forward)
        _resident([(win_hbm, win_v), (wout_hbm, wout_v)], sems)
        dxv = dx_ref[...]
        a = (dxv * g_ref[...]).astype(bf16)
        dh = None
        for c in range(2):
            cs = slice(c * FC, (c + 1) * FC)
            da = lax.dot_general(a, wout_v[cs, :], nt_dims, preferred_element_type=f32)
            gt = gu_ref[0, :, cs].astype(f32)
            up = gu_ref[1, :, cs].astype(f32)
            sg = jax.nn.sigmoid(gt)
            dgate = (da * up * (sg * (1.0 + gt * (1.0 - sg)))).astype(bf16)
            dup = (da * (gt * sg)).astype(bf16)
            dgu_ref[0, :, cs] = dgate
            dgu_ref[1, :, cs] = dup
            part = (lax.dot_general(dgate, win_v[c], nt_dims, preferred_element_type=f32)
                    + lax.dot_general(dup, win_v[c + 2], nt_dims, preferred_element_type=f32))
            dh = part if dh is None else dh + part
        xv = x_ref[...]
        r = lax.rsqrt(jnp.mean(xv * xv, axis=-1, keepdims=True) + RMS_EPS)
        xh = xv * r
        dn = dh * (1.0 + sc_ref[...])
        dxh = dn * nw_ref[...]
        o_ref[...] = dxv + r * (dxh - xh * jnp.mean(dxh * xh, axis=-1, keepdims=True))
        rows = jnp.concatenate([
            jnp.sum(dn * xh, axis=0, keepdims=True),
            jnp.sum(dh * (xh * nw_ref[...]), axis=0, keepdims=True),
            jnp.sum(dh, axis=0, keepdims=True),
            jnp.zeros((5, d), f32)], axis=0)

        @pl.when(pl.program_id(0) == 0)
        def _():
            st_ref[...] = rows

        @pl.when(pl.program_id(0) != 0)
        def _():
            st_ref[...] += rows

        if nc:
            pl.when(pl.program_id(0) == nsteps - 1)(finish)

    big = pl.BlockSpec((tm, d), lambda i: (i, 0))
    vec = pl.BlockSpec((1, d), lambda i: (0, 0))
    gus = pl.BlockSpec((2, tm, F), lambda i: (0, i, 0))
    cshapes = [jax.ShapeDtypeStruct(p.shape, p.dtype) for p in carry]
    res = _pcall(
        body, name=name, grid=(nsteps,), in_specs=[big, big, gus, vec, vec, vec, ANY, ANY] + [ANY] * nc,
        out_specs=[gus, big, pl.BlockSpec((8, d), lambda i: (0, 0))] + [ANY] * (2 * nc),
        out_shape=[jax.ShapeDtypeStruct((2, s, F), bf16), jax.ShapeDtypeStruct((s, d), f32), jax.ShapeDtypeStruct((8, d), f32)]
        + cshapes + cshapes,
        scratch_shapes=[pltpu.VMEM((4, d, FC), bf16), pltpu.VMEM((F, d), bf16), pltpu.SemaphoreType.DMA((2,))]
        + (_exchange_scratch(nc) if nc else []),
        compiler_params=_params("arbitrary"),
    )(dx, x, gu, g, nw, sc, w_in, w_out, *carry)
    return res[0], res[1], res[2], res[3:3 + nc], res[3 + nc:]


def ffn_dw_in(h, dgu, *, name):
    s, d = h.shape
    ts = _row_tile(s)
    ns = s // ts
    tn_dims = (((0,), (0,)), ((), ()))

    def body(h_ref, dgu_ref, o_ref, acc):
        k = pl.program_id(1)

        @pl.when(k == 0)
        def _():
            acc[...] = jnp.zeros_like(acc)

        hv = h_ref[...]
        for c in range(2):
            acc[c] += lax.dot_general(hv, dgu_ref[:, c * FC:(c + 1) * FC], tn_dims, preferred_element_type=f32)

        @pl.when(k == ns - 1)
        def _():
            o_ref[...] = acc[...].astype(bf16)

    return _pcall(
        body, name=name, grid=(2, ns),
        in_specs=[pl.BlockSpec((ts, d), lambda hf, k: (k, 0)), pl.BlockSpec((None, ts, F), lambda hf, k: (hf, k, 0))],
        out_specs=pl.BlockSpec((2, d, FC), lambda hf, k: (hf, 0, 0)),
        out_shape=jax.ShapeDtypeStruct((4, d, FC), bf16), scratch_shapes=[pltpu.VMEM((2, d, FC), f32)],
        compiler_params=_params("arbitrary", "arbitrary"),
    )(h, dgu)


def ffn_dw_out(a, dx, g, wb, *, name):
    s, fdim = a.shape
    d = dx.shape[1]
    ts = _row_tile(s)
    ns = s // ts
    tn = d // 2
    tn_dims = (((0,), (0,)), ((), ()))

    def body(a_ref, dx_ref, g_ref, wb_ref, o_ref, dg_ref, acc):
        k = pl.program_id(1)

        @pl.when(k == 0)
        def _():
            acc[...] = jnp.zeros_like(acc)

        acc[...] += lax.dot_general(a_ref[...], dx_ref[...].astype(bf16), tn_dims, preferred_element_type=f32)

        @pl.when(k == ns - 1)
        def _():
            gm = acc[...]
            dg_ref[...] = jnp.concatenate([jnp.sum(wb_ref[...].astype(f32) * gm, axis=0, keepdims=True),
                                           jnp.zeros((7, tn), f32)], axis=0)
            o_ref[...] = (gm * g_ref[...]).astype(bf16)

    return _pcall(
        body, name=name, grid=(2, ns),
        in_specs=[pl.BlockSpec((ts, fdim), lambda j, k: (k, 0)), pl.BlockSpec((ts, tn), lambda j, k: (k, j)),
                  pl.BlockSpec((1, tn), lambda j, k: (0, j)), pl.BlockSpec((fdim, tn), lambda j, k: (0, j))],
        out_specs=[pl.BlockSpec((fdim, tn), lambda j, k: (0, j)), pl.BlockSpec((8, tn), lambda j, k: (0, j))],
        out_shape=[jax.ShapeDtypeStruct((fdim, d), bf16), jax.ShapeDtypeStruct((8, d), f32)],
        scratch_shapes=[pltpu.VMEM((fdim, tn), f32)], compiler_params=_params("arbitrary", "arbitrary"),
    )(a, dx, g, wb)


def _alibi(n):
    return np.asarray(2.0 ** (-8.0 * np.arange(1, n + 1) / n), dtype=np.float32)


class _Attn:
    def __init__(self, s, *, mixer, group=0):
        if mixer == "a":
            self.blk, self.dil, self.npairs = 128, 1, 2
            self.qb0, self.kb0, self.vb0 = 0, 8, 10
            slopes = _alibi(16).reshape(2, 2, GQ)
        else:
            window, dil = B_GROUPS[group]
            self.blk, self.dil, self.npairs = window // (2 * dil), dil, 1
            self.qb0, self.kb0, self.vb0 = (0, 12, 15) if dil == 1 else (0, 4, 5)
            slopes = _alibi(24).reshape(3, 1, 2, GQ)[group]
        self.l = s // self.dil
        self.t = min(512, self.l)
        self.nt = self.l // self.t
        self.nb = self.t // self.blk
        blk = self.blk
        qi = np.arange(blk)[:, None]
        rel = np.arange(3 * blk)[None, :] - blk - qi
        dist = (self.dil * np.abs(rel)).astype(np.float32)
        bias = -slopes[:, :, :, None, None] * dist[None, None, None]
        bias = np.where(np.abs(rel) <= blk, bias, np.float32(NEG)).astype(np.float32)
        self.bias = np.ascontiguousarray(np.swapaxes(bias.reshape(self.npairs, 2, GQ * blk, 3 * blk), -1, -2))

    def grid(self):
        return (self.dil, self.npairs, self.nt)

    def tile(self, width, col):
        return pl.BlockSpec((None, self.t, width), lambda r, hp, i: (r, i, col(hp)))

    def halo(self, width, col):
        t, blk, nbl = self.t, self.blk, self.l // self.blk
        per = t // blk
        return [
            pl.BlockSpec((None, blk, width), lambda r, hp, i: (r, jnp.maximum(i * per - 1, 0), col(hp))),
            self.tile(width, col),
            pl.BlockSpec((None, blk, width), lambda r, hp, i: (r, jnp.minimum((i + 1) * per, nbl - 1), col(hp))),
        ]

    def qcol(self, e):
        return lambda hp: self.qb0 + 2 * hp + e

    def kcol(self, hp):
        return self.kb0 + hp

    def vcol(self, hp):
        return self.vb0 + hp

    def pcol(self, hp):
        return hp


def _stack_heads(x):
    return jnp.concatenate([x[:, g * DH:(g + 1) * DH] for g in range(GQ)], axis=0)


def _unstack_heads(x, rows):
    return jnp.concatenate([x[g * rows:(g + 1) * rows] for g in range(GQ)], axis=1)


def _head_cols(tile, hh, rows):
    return jnp.concatenate([tile[:, hh * GQ + g:hh * GQ + g + 1] for g in range(GQ)], axis=0)


def _carrying(body, n_in, n_out, n_scratch, carry, kind, grid):
    nc = len(carry)
    if not nc:
        return body, [], [], [], []
    n_res = nc if kind == "gather" else 2 * nc

    def wrapped(*refs):
        ins, src = refs[:n_in], refs[n_in:n_in + nc]
        outs = refs[n_in + nc:n_in + nc + n_out]
        res = refs[n_in + nc + n_out:n_in + nc + n_out + n_res]
        scr = refs[n_in + nc + n_out + n_res:n_in + nc + n_out + n_res + n_scratch]
        sems = refs[n_in + nc + n_out + n_res + n_scratch:]
        ids = [pl.program_id(a) for a in range(len(grid))]
        first = functools.reduce(jnp.logical_and, [i == 0 for i in ids])
        last = functools.reduce(jnp.logical_and, [i == g - 1 for i, g in zip(ids, grid)])
        if kind == "gather":
            start, finish = _gather_direct(src, res, *sems)
            pl.when(first)(start)
        else:
            start, forward, finish = _exchange(src, res[:nc], res[nc:], *sems)
            pl.when(first)(start)
            early = [g - 1 for g in grid[:-1]] + [max(grid[-1] - 1 - CARRY_LEAD, 0)]
            pl.when(functools.reduce(jnp.logical_and, [i == g for i, g in zip(ids, early)]))(forward)
        body(*ins, *outs, *scr)
        pl.when(last)(finish)

    if kind == "gather":
        shapes = [jax.ShapeDtypeStruct((4,) + tuple(c.shape), c.dtype) for c in carry]
        sems = _gather_scratch(nc)
    else:
        shapes = [jax.ShapeDtypeStruct(c.shape, c.dtype) for c in carry] * 2
        sems = _exchange_scratch(nc)
    return wrapped, [ANY] * nc, [ANY] * n_res, shapes, sems


def attn_fwd(qkv, sinkcol, cfg, carry=(), *, out_dtype, name):
    blk, t, nb, nt, dil = cfg.blk, cfg.t, cfg.nb, cfg.nt, cfg.dil
    scale = DH ** -0.5

    def body(q0, q1, kp, km, kn, vp, vm, vn, bias_ref, sink_ref, o_ref, lse_ref, kx, vx):
        ti = pl.program_id(2)
        first, last = ti == 0, ti == nt - 1
        for hh in range(2):
            sl = slice(hh * DH, (hh + 1) * DH)
            for dst, (p_, m_, n_) in ((kx, (kp, km, kn)), (vx, (vp, vm, vn))):
                dst[hh, 0:blk] = p_[:, sl]
                dst[hh, blk:blk + t] = m_[:, sl]
                dst[hh, blk + t:] = n_[:, sl]
        krow = lax.broadcasted_iota(jnp.int32, (3 * blk, GQ * blk), 0)
        pairs = [(b, hh) for b in range(nb) for hh in range(2)]
        qs = [_stack_heads((q0, q1)[hh][b * blk:(b + 1) * blk, :]) * scale for b, hh in pairs]
        sc = [lax.dot_general(kx[hh, b * blk:(b + 3) * blk, :], q_, (((1,), (1,)), ((), ())), preferred_element_type=f32)
              for q_, (b, hh) in zip(qs, pairs)]
        sc = [s_ + bias_ref[0, hh] for s_, (b, hh) in zip(sc, pairs)]
        sc = [jnp.where(jnp.logical_and(first, krow < blk), NEG, s_) if b == 0 else s_ for s_, (b, hh) in zip(sc, pairs)]
        sc = [jnp.where(jnp.logical_and(last, krow >= 2 * blk), NEG, s_) if b == nb - 1 else s_ for s_, (b, hh) in zip(sc, pairs)]
        ms = [jnp.maximum(jnp.max(s_, axis=0, keepdims=True), sink_ref[0, hh]) for s_, (b, hh) in zip(sc, pairs)]
        ps = [jnp.exp(s_ - m_) for s_, m_ in zip(sc, ms)]
        ls = [jnp.sum(p_, axis=0, keepdims=True) + jnp.exp(sink_ref[0, hh] - m_) for p_, m_, (b, hh) in zip(ps, ms, pairs)]
        os_ = [lax.dot_general(vx[hh, b * blk:(b + 3) * blk, :], p_.astype(bf16), (((0,), (0,)), ((), ())),
                               preferred_element_type=f32) for p_, (b, hh) in zip(ps, pairs)]
        os_ = [o_ / l_ for o_, l_ in zip(os_, ls)]
        lses = [m_ + jnp.log(l_) for m_, l_ in zip(ms, ls)]
        for o_, (b, hh) in zip(os_, pairs):
            o_ref[b * blk:(b + 1) * blk, hh * 256:(hh + 1) * 256] = _unstack_heads(o_.T, blk).astype(out_dtype)
        stat_rows = [jnp.concatenate([lses[2 * b + hh][:, g * blk:(g + 1) * blk] for b in range(nb)], axis=1)
                     for hh in range(2) for g in range(GQ)]
        lse_ref[...] = jnp.concatenate(stat_rows + [jnp.zeros((128 - 2 * GQ, t), f32)], axis=0).T

    in_specs = [cfg.tile(256, cfg.qcol(e)) for e in range(2)]
    in_specs += cfg.halo(128, cfg.kcol) + cfg.halo(128, cfg.vcol)
    in_specs += [pl.BlockSpec((1, 2, 3 * blk, GQ * blk), lambda r, hp, i: (hp, 0, 0, 0)),
                 pl.BlockSpec((1, 2, 1, GQ * blk), lambda r, hp, i: (hp, 0, 0, 0))]
    body, c_in, c_out, c_shape, c_sems = _carrying(body, 10, 2, 2, carry, "gather", cfg.grid())
    return _pcall(
        body, name=name, grid=cfg.grid(), in_specs=in_specs + c_in,
        out_specs=[cfg.tile(512, cfg.pcol), cfg.tile(128, cfg.pcol)] + c_out,
        out_shape=[jax.ShapeDtypeStruct((dil, cfg.l, cfg.npairs * 512), out_dtype),
                   jax.ShapeDtypeStruct((dil, cfg.l, cfg.npairs * 128), f32)] + c_shape,
        scratch_shapes=[pltpu.VMEM((2, t + 2 * blk, DH), bf16), pltpu.VMEM((2, t + 2 * blk, DH), bf16)] + c_sems,
        compiler_params=_params("arbitrary", "arbitrary", "arbitrary"),
    )(*([qkv] * 8), jnp.asarray(cfg.bias), sinkcol, *carry)


def attn_bwd(qkv, do, lse, delta, cfg, carry=(), *, name):
    blk, t, nb, nt, dil, npairs = cfg.blk, cfg.t, cfg.nb, cfg.nt, cfg.dil, cfg.npairs
    scale = DH ** -0.5
    nt_dims = (((1,), (1,)), ((), ()))
    tn_dims = (((0,), (0,)), ((), ()))

    def body(q0p, q0m, q0n, q1p, q1m, q1n, kp, km, kn, vp, vm, vn, dop, dom, don, lp, lm, ln, dp_, dm_, dn_,
             bias_ref, dq_ref, dk_ref, dv_ref, kx, vx, dkx, dvx):
        ti = pl.program_id(2)
        first, last = ti == 0, ti == nt - 1
        for hh in range(2):
            sl = slice(hh * DH, (hh + 1) * DH)
            for dst, (p_, m_, n_) in ((kx, (kp, km, kn)), (vx, (vp, vm, vn))):
                dst[hh, 0:blk] = p_[:, sl]
                dst[hh, blk:blk + t] = m_[:, sl]
                dst[hh, blk + t:] = n_[:, sl]
        dkx[...] = jnp.zeros_like(dkx)
        dvx[...] = jnp.zeros_like(dvx)

        def slab(prev, main, nxt, e):
            if e == 0:
                return prev[...]
            if e == nb + 1:
                return nxt[...]
            return main[(e - 1) * blk:e * blk, :]

        krow = lax.broadcasted_iota(jnp.int32, (3 * blk, GQ * blk), 0)

        def keys(e):
            if e == 0:
                return 1, 2, slice(2 * blk, 3 * blk)
            if e == nb + 1:
                return nb, nb + 1, slice(0, blk)
            return e - 1, e + 2, slice(0, 3 * blk)

        def edge(sc, e):
            if e == 0:
                return jnp.where(first, NEG, sc)
            if e == nb + 1:
                return jnp.where(last, NEG, sc)
            if e == 1:
                sc = jnp.where(jnp.logical_and(first, krow < blk), NEG, sc)
            if e == nb:
                sc = jnp.where(jnp.logical_and(last, krow >= 2 * blk), NEG, sc)
            return sc

        lse_t = [lp[...].T, lm[...].T, ln[...].T]
        dl_t = [dp_[...].T, dm_[...].T, dn_[...].T]

        def stat_row(parts, e, hh):
            src, lo = (parts[0], 0) if e == 0 else (parts[2], 0) if e == nb + 1 else (parts[1], (e - 1) * blk)
            return jnp.concatenate([src[hh * GQ + g:hh * GQ + g + 1, lo:lo + blk] for g in range(GQ)], axis=1)

        pairs = [(e, hh) for e in range(nb + 2) for hh in range(2)]
        qs = [_stack_heads(slab(*((q0p, q0m, q0n), (q1p, q1m, q1n))[hh], e)) * scale for e, hh in pairs]
        dos = [_stack_heads(slab(dop, dom, don, e)[:, hh * 256:(hh + 1) * 256]) for e, hh in pairs]
        lse_r = [stat_row(lse_t, e, hh) for e, hh in pairs]
        dl_r = [stat_row(dl_t, e, hh) for e, hh in pairs]
        kw = [kx[hh, keys(e)[0] * blk:keys(e)[1] * blk, :] for e, hh in pairs]
        vw = [vx[hh, keys(e)[0] * blk:keys(e)[1] * blk, :] for e, hh in pairs]
        sc = [lax.dot_general(k_, q_, nt_dims, preferred_element_type=f32) for q_, k_ in zip(qs, kw)]
        dp = [lax.dot_general(v_, d_, nt_dims, preferred_element_type=f32) for d_, v_ in zip(dos, vw)]
        sc = [edge(s_ + bias_ref[0, hh, keys(e)[2], :], e) for s_, (e, hh) in zip(sc, pairs)]
        ps = [jnp.exp(s_ - l_) for s_, l_ in zip(sc, lse_r)]
        ds = [(p_ * (d_ - c_)).astype(bf16) for p_, d_, c_ in zip(ps, dp, dl_r)]
        pb = [p_.astype(bf16) for p_ in ps]
        dks = [jnp.dot(s_, q_, preferred_element_type=f32) for s_, q_ in zip(ds, qs)]
        dvs = [jnp.dot(p_, d_, preferred_element_type=f32) for p_, d_ in zip(pb, dos)]
        dqs = [lax.dot_general(s_, k_, tn_dims, preferred_element_type=f32) if 1 <= e <= nb else None
               for s_, k_, (e, hh) in zip(ds, kw, pairs)]
        for dk_, dv_, dq_, (e, hh) in zip(dks, dvs, dqs, pairs):
            k0, k1, _ = keys(e)
            dkx[hh, k0 * blk:k1 * blk, :] += dk_
            dvx[hh, k0 * blk:k1 * blk, :] += dv_
            if dq_ is not None:
                dq_ref[(e - 1) * blk:e * blk, hh * 256:(hh + 1) * 256] = (_unstack_heads(dq_, blk) * scale).astype(bf16)
        for hh in range(2):
            dk_ref[:, hh * DH:(hh + 1) * DH] = dkx[hh, blk:blk + t, :].astype(bf16)
            dv_ref[:, hh * DH:(hh + 1) * DH] = dvx[hh, blk:blk + t, :].astype(bf16)

    in_specs = cfg.halo(256, cfg.qcol(0)) + cfg.halo(256, cfg.qcol(1))
    in_specs += cfg.halo(128, cfg.kcol) + cfg.halo(128, cfg.vcol)
    in_specs += cfg.halo(512, cfg.pcol) + cfg.halo(128, cfg.pcol) + cfg.halo(128, cfg.pcol)
    in_specs += [pl.BlockSpec((1, 2, 3 * blk, GQ * blk), lambda r, hp, i: (hp, 0, 0, 0))]
    body, c_in, c_out, c_shape, c_sems = _carrying(body, 22, 3, 4, carry, "exchange", cfg.grid())
    res = _pcall(
        body, name=name, grid=cfg.grid(), in_specs=in_specs + c_in,
        out_specs=[cfg.tile(512, cfg.pcol), cfg.tile(128, cfg.pcol), cfg.tile(128, cfg.pcol)] + c_out,
        out_shape=[jax.ShapeDtypeStruct((dil, cfg.l, npairs * 512), bf16),
                   jax.ShapeDtypeStruct((dil, cfg.l, npairs * 128), bf16),
                   jax.ShapeDtypeStruct((dil, cfg.l, npairs * 128), bf16)] + c_shape,
        scratch_shapes=[pltpu.VMEM((2, t + 2 * blk, DH), bf16), pltpu.VMEM((2, t + 2 * blk, DH), bf16),
                        pltpu.VMEM((2, t + 2 * blk, DH), f32), pltpu.VMEM((2, t + 2 * blk, DH), f32)] + c_sems,
        compiler_params=_params("arbitrary", "arbitrary", "arbitrary"),
    )(*([qkv] * 12), do, do, do, lse, lse, lse, delta, delta, delta, jnp.asarray(cfg.bias), *carry)
    nc = len(carry)
    return (res[0], res[1], res[2], res[3:3 + nc], res[3 + nc:]) if nc else res


def _head_indicator(nheads):
    e = np.zeros((nheads * DH, (nheads // 8) * 128), np.float32)
    for c in range(nheads * DH):
        h = c // DH
        e[c, (h // 8) * 128 + h % 8] = 1.0
    return e


def _dot_split(x, e):
    hi = x.astype(bf16)
    lo = (x - hi.astype(f32)).astype(bf16)
    return jnp.dot(hi, e, preferred_element_type=f32) + jnp.dot(lo, e, preferred_element_type=f32)


def _spread(scr, x, d):
    tm, w = x.shape
    for j in range(w // 128):
        scr[j] = x[:, j * 128:(j + 1) * 128]
    return [jnp.concatenate([scr[j, pl.ds(r, tm // d, stride=d), :] for j in range(w // 128)], axis=1) for r in range(d)]


def _weave(scr, blocks, d):
    n, w = blocks[0].shape
    for r in range(d):
        for j in range(w // 128):
            scr[j, pl.ds(r, n, stride=d), :] = blocks[r][:, j * 128:(j + 1) * 128]
    return jnp.concatenate([scr[j] for j in range(w // 128)], axis=1)


def _res_spec(d, tm, w):
    return pl.BlockSpec((d, tm // d, w), lambda i: (0, i, 0))


DILATED = tuple(dil for _, dil in B_GROUPS[1:])


def b_to_strided(qkv, *, name):
    s = qkv.shape[0]
    tm = _row_tile(s)

    def body(x_ref, *rest):
        outs, scr = rest[:-1], rest[-1]
        for gi, (o_ref, d) in enumerate(zip(outs, DILATED), start=1):
            cols = jnp.concatenate([x_ref[:, gi * 512:(gi + 1) * 512], x_ref[:, 1536 + gi * 128:1536 + (gi + 1) * 128],
                                    x_ref[:, 1920 + gi * 128:1920 + (gi + 1) * 128]], axis=1).astype(f32)
            for r, blk_ in enumerate(_spread(scr, cols, d)):
                o_ref[r] = blk_.astype(bf16)

    return _pcall(
        body, name=name, grid=(s // tm,), in_specs=[pl.BlockSpec((tm, B_QKV), lambda i: (i, 0))],
        out_specs=[_res_spec(d, tm, 768) for d in DILATED],
        out_shape=[jax.ShapeDtypeStruct((d, s // d, 768), bf16) for d in DILATED],
        scratch_shapes=[pltpu.VMEM((6, tm, 128), f32)], compiler_params=_params("parallel"),
    )(qkv)


def b_bwd_to_strided(do, lse, delta, *, name):
    s = do.shape[0]
    tm = _row_tile(s)

    def body(do_ref, lse_ref, dl_ref, *rest):
        outs, scr = rest[:-1], rest[-1]
        allc = jnp.concatenate([do_ref[...].astype(f32), lse_ref[...], dl_ref[...]], axis=1)
        for gi, d in enumerate(DILATED):
            o_do, o_lse, o_dl = outs[3 * gi:3 * gi + 3]
            for r, blk_ in enumerate(_spread(scr, allc, d)):
                o_do[r] = blk_[:, :512].astype(bf16)
                o_lse[r] = blk_[:, 512:640]
                o_dl[r] = blk_[:, 640:768]

    out_specs, out_shape = [], []
    for d in DILATED:
        out_specs += [_res_spec(d, tm, 512), _res_spec(d, tm, 128), _res_spec(d, tm, 128)]
        out_shape += [jax.ShapeDtypeStruct((d, s // d, 512), bf16), jax.ShapeDtypeStruct((d, s // d, 128), f32),
                      jax.ShapeDtypeStruct((d, s // d, 128), f32)]
    return _pcall(
        body, name=name, grid=(s // tm,),
        in_specs=[pl.BlockSpec((tm, 512), lambda i: (i, 0)), pl.BlockSpec((tm, 128), lambda i: (i, 0)),
                  pl.BlockSpec((tm, 128), lambda i: (i, 0))],
        out_specs=out_specs, out_shape=out_shape, scratch_shapes=[pltpu.VMEM((6, tm, 128), f32)],
        compiler_params=_params("parallel"),
    )(do, lse, delta)


def b_from_strided(grads, *, name):
    s = grads[0][0].shape[1]
    tm = _row_tile(s)

    def body(*refs):
        ins, o_ref, scr = refs[:9], refs[9], refs[10]
        nat = [jnp.concatenate([ins[q][0].astype(f32) for q in range(3)], axis=1)]
        for gi, d in enumerate(DILATED, start=1):
            blocks = [jnp.concatenate([ins[3 * gi + q][r].astype(f32) for q in range(3)], axis=1) for r in range(d)]
            nat.append(_weave(scr, blocks, d))
        pieces = [nat[g][:, lo:hi] for lo, hi in ((0, 512), (512, 640), (640, 768)) for g in range(3)]
        o_ref[...] = jnp.concatenate(pieces, axis=1).astype(bf16)

    dils = (1,) + DILATED
    in_specs = [_res_spec(d, tm, w) for d in dils for w in (512, 128, 128)]
    return _pcall(
        body, name=name, grid=(s * 1 // tm,), in_specs=in_specs, out_specs=pl.BlockSpec((tm, B_QKV), lambda i: (i, 0)),
        out_shape=jax.ShapeDtypeStruct((s, B_QKV), bf16), scratch_shapes=[pltpu.VMEM((6, tm, 128), f32)],
        compiler_params=_params("parallel"),
    )(*[a for g in grads for a in g])


def attn_merge(os_, lses, *, name):
    s = os_[0].shape[1]
    tm = _row_tile(s)
    ind_t = jnp.asarray(_head_indicator(8).T, dtype=bf16)
    dils = (1,) + DILATED

    def body(o0, o1, o2, l0, l1, l2, e_ref, o_ref, lse_ref, scr):
        both = [jnp.concatenate([o0[0], l0[0]], axis=1)]
        for og, lg, d in ((o1, l1, dils[1]), (o2, l2, dils[2])):
            both.append(_weave(scr, [jnp.concatenate([og[r], lg[r]], axis=1) for r in range(d)], d))
        ls = [b[:, 512:640] for b in both]
        m = jnp.maximum(jnp.maximum(ls[0], ls[1]), ls[2])
        tot = m + jnp.log(jnp.exp(ls[0] - m) + jnp.exp(ls[1] - m) + jnp.exp(ls[2] - m))
        lse_ref[...] = tot
        acc = jnp.zeros((tm, B_OUT), f32)
        for b, lg in zip(both, ls):
            acc = acc + _dot_split(jnp.exp(lg - tot), e_ref[...]) * b[:, :512]
        o_ref[...] = acc.astype(bf16)

    return _pcall(
        body, name=name, grid=(s * 1 // tm,),
        in_specs=[_res_spec(d, tm, 512) for d in dils] + [_res_spec(d, tm, 128) for d in dils]
        + [pl.BlockSpec((128, B_OUT), lambda i: (0, 0))],
        out_specs=[pl.BlockSpec((tm, B_OUT), lambda i: (i, 0)), pl.BlockSpec((tm, 128), lambda i: (i, 0))],
        out_shape=[jax.ShapeDtypeStruct((s, B_OUT), bf16), jax.ShapeDtypeStruct((s, 128), f32)],
        scratch_shapes=[pltpu.VMEM((5, tm, 128), f32)], compiler_params=_params("parallel"),
    )(*os_, *lses, ind_t)


def ada_mod(c_all, w, b, *, name):
    n = w.shape[2]

    def body(c_ref, w_ref, b_ref, o_ref):
        cv = c_ref[...]
        cond = cv * jax.nn.sigmoid(cv)
        o_ref[0] = jnp.dot(cond, w_ref[0], preferred_element_type=f32, precision=lax.Precision.HIGHEST) + b_ref[0]

    return _pcall(
        body, name=name, grid=(DEPTH,),
        in_specs=[pl.BlockSpec((N_DEV, D), lambda i: (0, 0)), pl.BlockSpec((1, D, n), lambda i: (i, 0, 0)),
                  pl.BlockSpec((1, 1, n), lambda i: (i, 0, 0))],
        out_specs=pl.BlockSpec((1, N_DEV, n), lambda i: (i, 0, 0)),
        out_shape=jax.ShapeDtypeStruct((DEPTH, N_DEV, n), f32), compiler_params=_params("arbitrary"),
    )(c_all, w, b)


def ada_grad(c_t, dm, *, name):
    n = dm.shape[2]

    def body(c_ref, dm_ref, o_ref):
        cv = c_ref[...]
        cond = cv * jax.nn.sigmoid(cv)
        acc = cond[:, 0:1] * dm_ref[0, 0:1, :]
        for b in range(1, N_DEV):
            acc = acc + cond[:, b:b + 1] * dm_ref[0, b:b + 1, :]
        o_ref[0] = acc

    return _pcall(
        body, name=name, grid=(DEPTH,),
        in_specs=[pl.BlockSpec((D, N_DEV), lambda i: (0, 0)), pl.BlockSpec((1, N_DEV, n), lambda i: (i, 0, 0))],
        out_specs=pl.BlockSpec((1, D, n), lambda i: (i, 0, 0)),
        out_shape=jax.ShapeDtypeStruct((DEPTH, D, n), f32), compiler_params=_params("arbitrary"),
    )(c_t, dm)


def _adam_math(w, g, m, v):
    m2 = B1 * m + (1.0 - B1) * g
    v2 = B2 * v + (1.0 - B2) * (g * g)
    mh = m2 / (1.0 - B1 ** STEP)
    vh = v2 / (1.0 - B2 ** STEP)
    return -LR * (mh / (jnp.sqrt(vh) + ADAM_EPS) + WD * w), m2, v2


def adamw(w, m, v, g, *, name):
    r, c = w.shape
    tr = 256 if r % 256 == 0 else r

    def body(w_ref, m_ref, v_ref, g_ref, d_ref, m2_ref, v2_ref):
        d_ref[...], m2_ref[...], v2_ref[...] = _adam_math(w_ref[...], g_ref[...], m_ref[...], v_ref[...])

    spec = pl.BlockSpec((tr, c), lambda i: (i, 0))
    return _pcall(
        body, name=name, grid=(r // tr,), in_specs=[spec] * 4, out_specs=[spec] * 3,
        out_shape=[jax.ShapeDtypeStruct((r, c), f32)] * 3, compiler_params=_params("parallel"),
    )(w, m, v, g)


def adamw_parts(w, m, v, own, sib, layer, prev=None, *, name):
    c = w.shape[1]
    r = own.shape[1]
    tr = 256 if r % 256 == 0 else r // 2
    off = layer * (r // tr)

    def body(w_ref, m_ref, v_ref, own_ref, sib_ref, *rest):
        g_ref, d_ref, m2_ref, v2_ref = rest[-4:]

        def total(ref):
            return ((ref[0].astype(f32) + ref[1].astype(f32)) + ref[2].astype(f32)) + ref[3].astype(f32)

        g = total(own_ref) + total(sib_ref)
        g_ref[...] = g
        d_ref[...], m2_ref[...], v2_ref[...] = _adam_math(w_ref[...], g, m_ref[...], v_ref[...])

    spec = pl.BlockSpec((tr, c), lambda i: (off + i, 0))
    pspec = pl.BlockSpec((4, tr, c), lambda i: (0, i, 0))
    prev = () if prev is None else tuple(prev)
    return _pcall(
        body, name=name, grid=(r // tr,), in_specs=[spec] * 3 + [pspec] * 2 + [ANY] * len(prev), out_specs=[spec] * 4,
        out_shape=[jax.ShapeDtypeStruct(w.shape, f32)] * 4,
        input_output_aliases={5 + q: q for q in range(len(prev))}, compiler_params=_params("parallel"),
    )(w, m, v, own, sib, *prev)


def sum_devices(g, *, name):
    _, r, c = g.shape

    def body(g_ref, o_ref):
        acc = g_ref[0]
        for k in range(1, N_DEV):
            acc = acc + g_ref[k]
        o_ref[...] = acc

    return _pcall(body, name=name, out_shape=jax.ShapeDtypeStruct((r, c), f32))(g)


def _place():
    x, y, c = lax.axis_index("x"), lax.axis_index("y"), lax.axis_index("c")
    chips = [(1 - x, y), (x, 1 - y), (1 - x, 1 - y)]
    return x, y, c, chips


def allgather8(v, *, name):
    r, c_ = v.shape

    def body(v_ref, o_ref, send_sems, recv_sems, local_sem):
        x, y, c, _ = _place()
        me = 4 * x + 2 * y + c
        mine = pltpu.make_async_copy(v_ref, o_ref.at[me], local_sem)
        mine.start()
        flips = [(fx, fy, fc) for fx in (0, 1) for fy in (0, 1) for fc in (0, 1)][1:]

        def peer(f):
            return (x ^ f[0], y ^ f[1], c ^ f[2])

        def copy(k, slot, to):
            return pltpu.make_async_remote_copy(
                src_ref=v_ref, dst_ref=o_ref.at[slot], send_sem=send_sems.at[k], recv_sem=recv_sems.at[k],
                device_id=to, device_id_type=MESH)

        sends = [copy(k, me, peer(f)) for k, f in enumerate(flips)]
        for cp in sends:
            cp.start()
        for k, f in enumerate(flips):
            px, py, pc = peer(f)
            copy(k, 4 * px + 2 * py + pc, (x, y, c)).wait_recv()
        for cp in sends:
            cp.wait_send()
        mine.wait()

    return _pcall(
        body, name=name, in_specs=[ANY], out_specs=ANY, out_shape=jax.ShapeDtypeStruct((N_DEV, r, c_), v.dtype),
        scratch_shapes=[pltpu.SemaphoreType.DMA((7,)), pltpu.SemaphoreType.DMA((7,)), pltpu.SemaphoreType.DMA],
    )(v)


def gather_weights(shards, *, name):
    n = len(shards)

    def body(*refs):
        src, out = refs[:n], refs[n:2 * n]
        send_a, recv_a, send_f, recv_f, local_sems = refs[2 * n:]
        x, y, c, chips = _place()
        sib = (x, y, 1 - c)
        me = 2 * x + y
        locals_ = [pltpu.make_async_copy(src[a], out[a].at[me], local_sems.at[a]) for a in range(n)]
        for cp in locals_:
            cp.start()

        def half(a, which):
            rh = src[a].shape[0] // 2
            return pl.ds(which * rh, rh)

        def first(a, k, chip_from, to):
            slot = 2 * chip_from[0] + chip_from[1]
            s_ref = src[a].at[half(a, c)]
            return pltpu.make_async_remote_copy(
                src_ref=s_ref, dst_ref=out[a].at[slot, half(a, c)], send_sem=send_a.at[3 * a + k],
                recv_sem=recv_a.at[3 * a + k], device_id=to, device_id_type=MESH)

        def passed(a, k, chip_from, which, to):
            slot = 2 * chip_from[0] + chip_from[1]
            ref = out[a].at[slot, half(a, which)]
            return pltpu.make_async_remote_copy(
                src_ref=ref, dst_ref=ref, send_sem=send_f.at[3 * a + k], recv_sem=recv_f.at[3 * a + k],
                device_id=to, device_id_type=MESH)

        sends = [first(a, k, (x, y), (*chip, c)) for a in range(n) for k, chip in enumerate(chips)]
        for cp in sends:
            cp.start()
        fwd = []
        for a in range(n):
            for k, chip in enumerate(chips):
                first(a, k, chip, (x, y, c)).wait_recv()
                cp = passed(a, k, chip, c, sib)
                cp.start()
                fwd.append(cp)
        for a in range(n):
            for k, chip in enumerate(chips):
                passed(a, k, chip, 1 - c, (x, y, c)).wait_recv()
        for cp in sends + fwd:
            cp.wait_send()
        for cp in locals_:
            cp.wait()

    return _pcall(
        body, name=name, in_specs=[ANY] * n, out_specs=[ANY] * n,
        out_shape=[jax.ShapeDtypeStruct((4,) + tuple(sh.shape), sh.dtype) for sh in shards],
        scratch_shapes=[pltpu.SemaphoreType.DMA((3 * n,)) for _ in range(4)] + [pltpu.SemaphoreType.DMA((n,))],
    )(*shards)


def _gather_direct(src, out, send_sems, recv_sems, local_sems):
    n = len(src)
    x, y, c, chips = _place()
    me = 2 * x + y

    def copy(a, k, slot, to):
        return pltpu.make_async_remote_copy(
            src_ref=src[a], dst_ref=out[a].at[slot], send_sem=send_sems.at[3 * a + k], recv_sem=recv_sems.at[3 * a + k],
            device_id=to, device_id_type=MESH)

    def start():
        for a in range(n):
            pltpu.make_async_copy(src[a], out[a].at[me], local_sems.at[a]).start()
            for k, chip in enumerate(chips):
                copy(a, k, me, (*chip, c)).start()

    def finish():
        for a in range(n):
            for k, chip in enumerate(chips):
                copy(a, k, 2 * chip[0] + chip[1], (x, y, c)).wait_recv()
        for a in range(n):
            for k in range(3):
                copy(a, k, me, (x, y, c)).wait_send()
            pltpu.make_async_copy(src[a], out[a].at[me], local_sems.at[a]).wait()

    return start, finish


def _gather_scratch(n):
    return [pltpu.SemaphoreType.DMA((3 * n,)), pltpu.SemaphoreType.DMA((3 * n,)), pltpu.SemaphoreType.DMA((n,))]


def _exchange(src, own, sibo, send_sems, recv_sems, local_sems):
    n = len(src)
    x, y, c, chips = _place()
    sib = (x, y, 1 - c)
    me = 2 * x + y

    def slot(chip):
        return 2 * chip[0] + chip[1]

    def copy(a, k, s_ref, d_ref, to):
        return pltpu.make_async_remote_copy(
            src_ref=s_ref, dst_ref=d_ref, send_sem=send_sems.at[7 * a + k], recv_sem=recv_sems.at[7 * a + k],
            device_id=to, device_id_type=MESH)

    def start():
        for a in range(n):
            pltpu.make_async_copy(src[a].at[me], own[a].at[me], local_sems.at[a]).start()
            copy(a, 0, src[a].at[me], sibo[a].at[me], sib).start()
            for k, chip in enumerate(chips):
                copy(a, 1 + k, src[a].at[slot(chip)], own[a].at[me], (*chip, c)).start()

    def forward():
        for a in range(n):
            for k, chip in enumerate(chips):
                copy(a, 1 + k, src[a].at[me], own[a].at[slot(chip)], (x, y, c)).wait_recv()
                copy(a, 4 + k, own[a].at[slot(chip)], sibo[a].at[slot(chip)], sib).start()

    def finish():
        for a in range(n):
            copy(a, 0, src[a].at[me], sibo[a].at[me], (x, y, c)).wait_recv()
            for k, chip in enumerate(chips):
                copy(a, 4 + k, src[a].at[me], sibo[a].at[slot(chip)], (x, y, c)).wait_recv()
        for a in range(n):
            for k in range(7):
                copy(a, k, src[a].at[me], own[a].at[me], (x, y, c)).wait_send()
            pltpu.make_async_copy(src[a].at[me], own[a].at[me], local_sems.at[a]).wait()

    return start, forward, finish


def _exchange_scratch(n):
    return [pltpu.SemaphoreType.DMA((7 * n,)), pltpu.SemaphoreType.DMA((7 * n,)), pltpu.SemaphoreType.DMA((n,))]


def exchange_grads(parts, *, name):
    n = len(parts)

    def body(*refs):
        start, forward, finish = _exchange(refs[:n], refs[n:2 * n], refs[2 * n:3 * n], *refs[3 * n:])
        start()
        forward()
        finish()

    shapes = [jax.ShapeDtypeStruct(p.shape, p.dtype) for p in parts]
    res = _pcall(body, name=name, in_specs=[ANY] * n, out_specs=[ANY] * (2 * n), out_shape=shapes + shapes,
                 scratch_shapes=_exchange_scratch(n))(*parts)
    return res[:n], res[n:]


def _natural(g, how):
    if how == "col":
        return jnp.moveaxis(g, 0, 1).reshape(g.shape[1], 4 * g.shape[2])
    return g.reshape(4 * g.shape[1], g.shape[2])


def _chunks(gw, how):
    k, n = gw.shape
    if how == "col":
        return jnp.moveaxis(gw.reshape(k, 4, n // 4), 1, 0).astype(bf16)
    return gw.reshape(4, k // 4, n).astype(bf16)


def kernel(x, c, ada_w, ada_b, norm_mix, norm_ffn, ffn_w_in, ffn_w_out, a_w_in, a_w_out, a_sink, b_w_in, b_w_out, final_norm, loss_target, m_ada_w, m_ada_b, m_norm_mix, m_norm_ffn, m_ffn_w_in, m_ffn_w_out, m_a_w_in, m_a_w_out, m_a_sink, m_b_w_in, m_b_w_out, m_final_norm, v_ada_w, v_ada_b, v_norm_mix, v_norm_ffn, v_ffn_w_in, v_ffn_w_out, v_a_w_in, v_a_w_out, v_a_sink, v_b_w_in, v_b_w_out, v_final_norm):
    s = x.shape[1]
    xi, yi, ci = lax.axis_index("x"), lax.axis_index("y"), lax.axis_index("c")
    chip = 2 * xi + yi
    dev = 2 * chip + ci
    x0 = x[0]
    tgt = loss_target[0]

    big = {"ffn_w_in": (ffn_w_in, "col"), "ffn_w_out": (ffn_w_out, "row"), "a_w_in": (a_w_in, "col"),
           "a_w_out": (a_w_out, "row"), "b_w_in": (b_w_in, "col"), "b_w_out": (b_w_out, "col")}
    names = list(big)

    def layer_keys(i):
        mix = "a" if i % 2 == 0 else "b"
        return [("ffn_w_in", i), ("ffn_w_out", i), (mix + "_w_in", i // 2), (mix + "_w_out", i // 2)]

    def shards_of(i):
        return [big[k][0][l].astype(bf16) for k, l in layer_keys(i)]

    def weights_of(i, gathered):
        return {k: (g if k == "ffn_w_in" else _natural(g, big[k][1])) for (k, _), g in zip(layer_keys(i), gathered)}

    mix0 = gather_weights(shards_of(0)[2:], name="gather_weights")

    c_all = allgather8(jnp.broadcast_to(c, (8, D)), name="gather_c")[:, 0, :]
    nsh = ada_w.shape[2]
    ada_b_sh = lax.dynamic_slice_in_dim(ada_b, chip * nsh, nsh, axis=1)[:, None, :]
    mod_part = ada_mod(c_all, ada_w, ada_b_sh, name="ada_mod")
    mod_all = allgather8(mod_part.reshape(DEPTH * N_DEV, nsh), name="gather_mod")
    mod_all = mod_all.reshape(4, 2, DEPTH, N_DEV, nsh)[:, 0]
    mod = lax.dynamic_index_in_dim(mod_all, dev, axis=2, keepdims=False)
    mod = jnp.moveaxis(mod, 0, 1).reshape(DEPTH, 6, 1, D)

    cfg_a = _Attn(s, mixer="a")
    cfg_b = [_Attn(s, mixer="b", group=g) for g in range(3)]
    no_sink = jnp.full((1, 2, 1, GQ * 64), NEG, f32)

    saved = []
    xc = x0
    for i in range(DEPTH):
        j = i // 2
        sh1, sc1, g1, sh2, sc2, g2 = (mod[i, q] for q in range(6))
        nmix, nffn = norm_mix[i][None, :], norm_ffn[i][None, :]
        mix = "a" if i % 2 == 0 else "b"
        if i == 0:
            h, qkv = mm_norm(xc, nmix, sc1, sh1, _natural(mix0[0], "col"), name="a_qkv")
            sinkcol = jnp.repeat(a_sink[j].reshape(2, 2, GQ), 128, axis=2)[:, :, None, :]
            o, lse, *ffn0 = attn_fwd(qkv[None], sinkcol, cfg_a, shards_of(0)[:2], out_dtype=bf16, name="a_attn_fwd_gather")
            o, lse = o[0], lse[0]
            wl = [weights_of(0, ffn0 + list(mix0))]
        elif i % 2 == 0:
            h, qkv = mm_norm(xc, nmix, sc1, sh1, wl[i]["a_w_in"], name="a_qkv")
            sinkcol = jnp.repeat(a_sink[j].reshape(2, 2, GQ), 128, axis=2)[:, :, None, :]
            o, lse = (t[0] for t in attn_fwd(qkv[None], sinkcol, cfg_a, out_dtype=bf16, name="a_attn_fwd"))
        else:
            h, qkv = mm_norm(xc, nmix, sc1, sh1, wl[i]["b_w_in"], name="b_qkv")
            qkv = [qkv[None]] + list(b_to_strided(qkv, name="b_to_strided"))
            outs = [attn_fwd(qkv[g], no_sink, cfg_b[g], out_dtype=f32, name=f"b_attn_fwd{g}") for g in range(3)]
            o, lse = attn_merge([t[0] for t in outs], [t[1] for t in outs], name="b_merge")
        x1 = mm_resid(o, wl[i][mix + "_w_out"], xc, g1, name=mix + "_out")
        nxt = shards_of(i + 1) if i + 1 < DEPTH else []
        h2, gu, act, x2, *got = ffn_fwd(x1, nffn, sc2, sh2, g2, wl[i]["ffn_w_in"], wl[i]["ffn_w_out"], nxt,
                                        name="ffn_fwd_gather" if nxt else "ffn_fwd")
        if nxt:
            wl.append(weights_of(i + 1, got))
        saved.append((xc, h, qkv, o, lse, x1, h2, gu, act))
        xc = x2

    dx, st_final = loss_head(xc, final_norm[None, :], tgt, name="loss_head")

    zero_row = jnp.zeros((1, D), f32)
    dmod_rows = [None] * DEPTH
    d_nmix, d_nffn = [None] * DEPTH, [None] * DEPTH
    d_sink = [None] * 2
    parts, exchanged = None, {}
    for i in reversed(range(DEPTH)):
        j = i // 2
        xin, h, qkv, o, lse, x1, h2, gu, act = saved[i]
        sh1, sc1, g1, sh2, sc2, g2 = (mod[i, q] for q in range(6))
        nmix, nffn = norm_mix[i][None, :], norm_ffn[i][None, :]
        mix = "a" if i % 2 == 0 else "b"
        w_fo, w_o, w_i = wl[i]["ffn_w_out"], wl[i][mix + "_w_out"], wl[i][mix + "_w_in"]
        dgu, dx1, st2, own_, sib_ = ffn_bwd_rows(dx, x1, gu, g2, nffn, sc2, wl[i]["ffn_w_in"], w_fo, parts or [],
                                                 name="ffn_bwd_rows_exchange" if parts else "ffn_bwd_rows")
        if parts:
            exchanged[i + 1] = (own_, sib_)
        gwo, dg2 = ffn_dw_out(act, dx, g2, w_fo, name="ffn_dw_out")
        dg2 = dg2[0:1]
        gwi = ffn_dw_in(h2, dgu, name="ffn_dw_in")
        gmo, dg1 = mm_tn(o, dx1, (g1, w_o), name=mix + "_dw_out")
        ffn_parts = [gwi, gwo.reshape(4, F // 4, D)]
        if i % 2 == 0:
            sinkrow = jnp.pad(a_sink[j].reshape(2, 8), ((0, 0), (0, 120))).reshape(1, 256)
            do, delta, dsk = mm_nt_delta(dx1, g1, w_o, o, lse, sinkrow, name="a_do")
            d_sink[j] = dsk[0].reshape(2, 128)[:, :8].reshape(16)
            dq, dk, dv, *ffn_x = attn_bwd(qkv[None], do[None], lse[None], delta[None], cfg_a, ffn_parts if i == 0 else [],
                                          name="a_attn_bwd_exchange" if i == 0 else "a_attn_bwd")
            dqkv = [dq[0], dk[0], dv[0]]
        else:
            do, delta, _ = mm_nt_delta(dx1, g1, w_o, o, lse, jnp.zeros((1, 128), f32), name="b_do")
            st = [do[None], lse[None], delta[None]] + list(b_bwd_to_strided(do, lse, delta, name="b_bwd_to_strided"))
            gr = [attn_bwd(qkv[g], *st[3 * g:3 * g + 3], cfg_b[g], name=f"b_attn_bwd{g}") for g in range(3)]
            dqkv = b_from_strided(gr, name="b_from_strided")
        gmi = mm_tn(h, dqkv, name=mix + "_dw_in")
        dx, st1 = mm_nt_norm_bwd(dqkv, w_i, xin, dx1, nmix, sc1, name=mix + "_dh")
        dmod_rows[i] = jnp.concatenate([st1[2:3], st1[1:2], dg1, st2[2:3], st2[1:2], dg2], axis=0)
        d_nmix[i], d_nffn[i] = st1[0:1], st2[0:1]
        mix_parts = [_chunks(gmi, big[mix + "_w_in"][1]), _chunks(gmo, big[mix + "_w_out"][1])]
        parts = ffn_parts + mix_parts
    own_m, sib_m = exchange_grads(mix_parts, name="exchange_grads")
    exchanged[0] = (list(ffn_x[0]) + list(own_m), list(ffn_x[1]) + list(sib_m))

    sink_row = jnp.pad(jnp.concatenate(d_sink), (0, D - 32))[None, :]
    stats = jnp.concatenate(dmod_rows + d_nmix + d_nffn + [sink_row, st_final[0:1], st_final[1:2]]
                            + [zero_row] * (STAT_ROWS - 35), axis=0)
    stats_all = allgather8(stats, name="gather_stats")
    tot = sum_devices(stats_all, name="sum_stats")
    loss = 0.5 * jnp.sum(tot[34]) / float(D)

    def pack(ab, nm, nf, sk, fnm, fill):
        return jnp.concatenate([ab.reshape(24, D), nm, nf, jnp.pad(sk.reshape(1, 32), ((0, 0), (0, D - 32)), constant_values=fill),
                                fnm[None, :], jnp.full((STAT_ROWS - 34, D), fill, f32)], axis=0)

    sd, sm, sv = adamw(pack(ada_b, norm_mix, norm_ffn, a_sink, final_norm, 0.0),
                       pack(m_ada_b, m_norm_mix, m_norm_ffn, m_a_sink, m_final_norm, 0.0),
                       pack(v_ada_b, v_norm_mix, v_norm_ffn, v_a_sink, v_final_norm, 1.0), tot, name="adamw_small")

    def unpack(p):
        return p[0:24].reshape(DEPTH, 6 * D), p[24:28], p[28:32], p[32, :32].reshape(2, 16), p[33]

    small = {"grad": unpack(tot), "delta": unpack(sd), "m": unpack(sm), "v": unpack(sv)}

    dmod_all = stats_all[:, 0:24, :].reshape(N_DEV, DEPTH, 6 * D)
    dm_sh = jnp.moveaxis(lax.dynamic_slice_in_dim(dmod_all, chip * nsh, nsh, axis=2), 0, 1)
    g_ada = ada_grad(c_all.T, dm_sh, name="ada_grad")
    r_ada = (DEPTH * D, nsh)
    ada_res = adamw(ada_w.reshape(r_ada), m_ada_w.reshape(r_ada), v_ada_w.reshape(r_ada), g_ada.reshape(r_ada), name="adamw_ada")
    ada_out = [g_ada] + [t.reshape(ada_w.shape) for t in ada_res]

    mom = {"ffn_w_in": (m_ffn_w_in, v_ffn_w_in), "ffn_w_out": (m_ffn_w_out, v_ffn_w_out), "a_w_in": (m_a_w_in, v_a_w_in),
           "a_w_out": (m_a_w_out, v_a_w_out), "b_w_in": (m_b_w_in, v_b_w_in), "b_w_out": (m_b_w_out, v_b_w_out)}
    big_out = {k: None for k in names}
    for i in reversed(range(DEPTH)):
        own_, sib_ = exchanged[i]
        for (k, l), o_, s_ in zip(layer_keys(i), own_, sib_):
            w = big[k][0]
            r2 = (-1, w.shape[-1])
            big_out[k] = adamw_parts(w.reshape(r2), mom[k][0].reshape(r2), mom[k][1].reshape(r2), o_, s_, l, big_out[k],
                                     name=f"adamw_{k}{l}")
    big_out = {k: [t.reshape(big[k][0].shape) for t in big_out[k]] for k in names}

    def leaves(q):
        sm_ = small[("grad", "delta", "m", "v")[q]]
        return (ada_out[q], sm_[0], sm_[1], sm_[2], big_out["ffn_w_in"][q], big_out["ffn_w_out"][q], big_out["a_w_in"][q],
                big_out["a_w_out"][q], sm_[3], big_out["b_w_in"][q], big_out["b_w_out"][q], sm_[4])

    return (loss, dx[None], *leaves(0), *leaves(1), *leaves(2), *leaves(3))
```

```python
import functools
import math

import numpy as np
import jax
import jax.numpy as jnp
from jax import lax
from jax.experimental import pallas as pl
from jax.experimental.pallas import tpu as pltpu

f32 = jnp.float32
bf16 = jnp.bfloat16

D = 1024
DH = 64
GQ = 4
DEPTH = 4
F = 2816
A_QKV, A_OUT = 1536, 1024
B_QKV, B_OUT = 2304, 512
B_GROUPS = ((128, 1), (512, 4), (2048, 16))
RMS_EPS = 1e-6
NEG = -1e30
LR, B1, B2, ADAM_EPS, WD, STEP = 0.001, 0.9, 0.999, 1e-08, 0.01, 10
N_DEV = 8
STAT_ROWS = 40
CARRY_LEAD = 6
MESH = pl.DeviceIdType.MESH
ANY = pl.BlockSpec(memory_space=pl.ANY)


def _pcall(body, **kw):
    return pl.pallas_call(body, **kw)


def _params(*sem):
    return pltpu.CompilerParams(dimension_semantics=sem, vmem_limit_bytes=56 * 1024 * 1024)


def _row_tile(s, want=1024):
    return want if s % want == 0 else s


ROW_CHUNKS = 4


def mm_norm(x, nw, sc, sh, w, *, name):
    s, d = x.shape
    n = w.shape[1]
    tm = _row_tile(s)
    rc = tm // ROW_CHUNKS

    def body(x_ref, nw_ref, sc_ref, sh_ref, w_ref, h_ref, y_ref):
        hs = []
        for c in range(ROW_CHUNKS):
            xv = x_ref[c * rc:(c + 1) * rc, :]
            r = lax.rsqrt(jnp.mean(xv * xv, axis=-1, keepdims=True) + RMS_EPS)
            hs.append(((xv * r * nw_ref[...]) * (1.0 + sc_ref[...]) + sh_ref[...]).astype(bf16))
        ys = [jnp.dot(h, w_ref[...], preferred_element_type=f32) for h in hs]
        for c in range(ROW_CHUNKS):
            h_ref[c * rc:(c + 1) * rc, :] = hs[c]
            y_ref[c * rc:(c + 1) * rc, :] = ys[c].astype(bf16)

    vec = pl.BlockSpec((1, d), lambda i: (0, 0))
    return _pcall(
        body, name=name, grid=(s // tm,),
        in_specs=[pl.BlockSpec((tm, d), lambda i: (i, 0)), vec, vec, vec, pl.BlockSpec((d, n), lambda i: (0, 0))],
        out_specs=[pl.BlockSpec((tm, d), lambda i: (i, 0)), pl.BlockSpec((tm, n), lambda i: (i, 0))],
        out_shape=[jax.ShapeDtypeStruct((s, d), bf16), jax.ShapeDtypeStruct((s, n), bf16)],
        compiler_params=_params("parallel"),
    )(x, nw, sc, sh, w)


def mm_resid(a, w, xres, g, *, name):
    s, k = a.shape
    n = w.shape[1]
    tm = _row_tile(s)
    rc = tm // ROW_CHUNKS

    def body(a_ref, w_ref, x_ref, g_ref, o_ref):
        ys = [jnp.dot(a_ref[c * rc:(c + 1) * rc, :], w_ref[...], preferred_element_type=f32) for c in range(ROW_CHUNKS)]
        for c, y in enumerate(ys):
            o_ref[c * rc:(c + 1) * rc, :] = x_ref[c * rc:(c + 1) * rc, :] + g_ref[...] * y

    big = pl.BlockSpec((tm, n), lambda i: (i, 0))
    return _pcall(
        body, name=name, grid=(s // tm,),
        in_specs=[pl.BlockSpec((tm, k), lambda i: (i, 0)), pl.BlockSpec((k, n), lambda i: (0, 0)), big,
                  pl.BlockSpec((1, n), lambda i: (0, 0))],
        out_specs=big, out_shape=jax.ShapeDtypeStruct((s, n), f32), compiler_params=_params("parallel"),
    )(a, w, xres, g)


def mm_nt_delta(dx, g, w, o, lse, sinkrow, *, name):
    s, d = dx.shape
    n = w.shape[0]
    wd = lse.shape[1]
    tm = _row_tile(s)
    rc = tm // ROW_CHUNKS
    ind = jnp.asarray(_head_indicator(n // DH), dtype=bf16)

    def body(dx_ref, g_ref, w_ref, o_ref, lse_ref, sink_ref, e_ref, do_ref, dl_ref, ds_ref):
        rows = [slice(c * rc, (c + 1) * rc) for c in range(ROW_CHUNKS)]
        as_ = [(dx_ref[rw, :] * g_ref[...]).astype(bf16) for rw in rows]
        dos = [lax.dot_general(a, w_ref[...], (((1,), (1,)), ((), ())), preferred_element_type=f32).astype(bf16) for a in as_]
        dls = [_dot_split(do.astype(f32) * o_ref[rw, :].astype(f32), e_ref[...]) for do, rw in zip(dos, rows)]
        part = None
        for rw, do, dl in zip(rows, dos, dls):
            do_ref[rw, :] = do
            dl_ref[rw, :] = dl
            p = -jnp.sum(jnp.exp(sink_ref[...] - lse_ref[rw, :]) * dl, axis=0, keepdims=True)
            part = p if part is None else part + p
        part = jnp.concatenate([part, jnp.zeros((7, wd), f32)], axis=0)

        @pl.when(pl.program_id(0) == 0)
        def _():
            ds_ref[...] = part

        @pl.when(pl.program_id(0) != 0)
        def _():
            ds_ref[...] += part

    return _pcall(
        body, name=name, grid=(s // tm,),
        in_specs=[pl.BlockSpec((tm, d), lambda i: (i, 0)), pl.BlockSpec((1, d), lambda i: (0, 0)),
                  pl.BlockSpec((n, d), lambda i: (0, 0)), pl.BlockSpec((tm, n), lambda i: (i, 0)),
                  pl.BlockSpec((tm, wd), lambda i: (i, 0)), pl.BlockSpec((1, wd), lambda i: (0, 0)),
                  pl.BlockSpec((n, wd), lambda i: (0, 0))],
        out_specs=[pl.BlockSpec((tm, n), lambda i: (i, 0)), pl.BlockSpec((tm, wd), lambda i: (i, 0)),
                   pl.BlockSpec((8, wd), lambda i: (0, 0))],
        out_shape=[jax.ShapeDtypeStruct((s, n), bf16), jax.ShapeDtypeStruct((s, wd), f32), jax.ShapeDtypeStruct((8, wd), f32)],
        compiler_params=_params("arbitrary"),
    )(dx, g, w, o, lse, sinkrow, ind)


def mm_nt_norm_bwd(a, w, x, dres, nw, sc, *, name):
    pieces = list(a) if isinstance(a, (list, tuple)) else [a]
    npc = len(pieces)
    s = pieces[0].shape[0]
    k = sum(p.shape[1] for p in pieces)
    d = w.shape[0]
    tm = _row_tile(s)
    rc = tm // ROW_CHUNKS

    def body(*refs):
        a_refs = refs[:npc]
        w_ref, x_ref, dr_ref, nw_ref, sc_ref, o_ref, st_ref = refs[npc:]

        def a_rows(c):
            got = [r[c * rc:(c + 1) * rc, :] for r in a_refs]
            return jnp.concatenate(got, axis=1) if npc > 1 else got[0]

        dhs = [lax.dot_general(a_rows(c), w_ref[...], (((1,), (1,)), ((), ())), preferred_element_type=f32)
               for c in range(ROW_CHUNKS)]
        rows = None
        for c, dh in enumerate(dhs):
            xv = x_ref[c * rc:(c + 1) * rc, :]
            r = lax.rsqrt(jnp.mean(xv * xv, axis=-1, keepdims=True) + RMS_EPS)
            xh = xv * r
            dn = dh * (1.0 + sc_ref[...])
            dxh = dn * nw_ref[...]
            o_ref[c * rc:(c + 1) * rc, :] = dr_ref[c * rc:(c + 1) * rc, :] + r * (dxh - xh * jnp.mean(dxh * xh, axis=-1, keepdims=True))
            part = jnp.concatenate([
                jnp.sum(dn * xh, axis=0, keepdims=True),
                jnp.sum(dh * (xh * nw_ref[...]), axis=0, keepdims=True),
                jnp.sum(dh, axis=0, keepdims=True),
                jnp.zeros((5, d), f32)], axis=0)
            rows = part if rows is None else rows + part

        @pl.when(pl.program_id(0) == 0)
        def _():
            st_ref[...] = rows

        @pl.when(pl.program_id(0) != 0)
        def _():
            st_ref[...] += rows

    big = pl.BlockSpec((tm, d), lambda i: (i, 0))
    vec = pl.BlockSpec((1, d), lambda i: (0, 0))
    return _pcall(
        body, name=name, grid=(s // tm,),
        in_specs=[pl.BlockSpec((tm, p.shape[1]), lambda i: (i, 0)) for p in pieces]
        + [pl.BlockSpec((d, k), lambda i: (0, 0)), big, big, vec, vec],
        out_specs=[big, pl.BlockSpec((8, d), lambda i: (0, 0))],
        out_shape=[jax.ShapeDtypeStruct((s, d), f32), jax.ShapeDtypeStruct((8, d), f32)],
        compiler_params=_params("arbitrary"),
    )(*pieces, w, x, dres, nw, sc)


def mm_tn(a, b, scale=None, *, name):
    pieces = list(b) if isinstance(b, (list, tuple)) else [b]
    s, ka = a.shape
    nb = sum(p.shape[1] for p in pieces)
    npc = len(pieces)
    ts = _row_tile(s)
    ns = s // ts

    def body(a_ref, *rest):
        b_refs, rest = rest[:npc], rest[npc:]
        o_ref = rest[2] if scale is not None else rest[0]
        si = pl.program_id(0)
        bv = jnp.concatenate([r[...].astype(bf16) for r in b_refs], axis=1) if npc > 1 else b_refs[0][...].astype(bf16)
        part = lax.dot_general(a_ref[...], bv, (((0,), (0,)), ((), ())), preferred_element_type=f32)

        @pl.when(si == 0)
        def _():
            o_ref[...] = part

        @pl.when(si != 0)
        def _():
            o_ref[...] += part

        if scale is not None:
            g_ref, wb_ref, dg_ref = rest[0], rest[1], rest[3]

            @pl.when(si == ns - 1)
            def _():
                gm = o_ref[...]
                dg_ref[...] = jnp.sum(wb_ref[...].astype(f32) * gm, axis=0, keepdims=True)
                o_ref[...] = gm * g_ref[...]

    in_specs = [pl.BlockSpec((ts, ka), lambda k: (k, 0))] + [pl.BlockSpec((ts, p.shape[1]), lambda k: (k, 0)) for p in pieces]
    args = [a] + pieces
    whole = pl.BlockSpec((ka, nb), lambda k: (0, 0))
    out_specs = [whole]
    out_shape = [jax.ShapeDtypeStruct((ka, nb), f32)]
    if scale is not None:
        in_specs += [pl.BlockSpec((1, nb), lambda k: (0, 0)), whole]
        args += list(scale)
        out_specs.append(pl.BlockSpec((1, nb), lambda k: (0, 0)))
        out_shape.append(jax.ShapeDtypeStruct((1, nb), f32))
    res = _pcall(body, name=name, grid=(ns,), in_specs=in_specs, out_specs=out_specs, out_shape=out_shape,
                 compiler_params=_params("arbitrary"))(*args)
    return res if scale is not None else res[0]


def loss_head(x, fn, tgt, *, name):
    s, d = x.shape
    tm = _row_tile(s, 512)

    def body(x_ref, fn_ref, t_ref, dx_ref, st_ref):
        xv = x_ref[...]
        r = lax.rsqrt(jnp.mean(xv * xv, axis=-1, keepdims=True) + RMS_EPS)
        xh = xv * r
        err = xh * fn_ref[...] - t_ref[...]
        dy = err / float(d)
        dxh = dy * fn_ref[...]
        dx_ref[...] = r * (dxh - xh * jnp.mean(dxh * xh, axis=-1, keepdims=True))
        rows = jnp.concatenate([
            jnp.sum(dy * xh, axis=0, keepdims=True),
            jnp.sum(err * err, axis=0, keepdims=True),
            jnp.zeros((6, d), f32)], axis=0)

        @pl.when(pl.program_id(0) == 0)
        def _():
            st_ref[...] = rows

        @pl.when(pl.program_id(0) != 0)
        def _():
            st_ref[...] += rows

    big = pl.BlockSpec((tm, d), lambda i: (i, 0))
    return _pcall(
        body, name=name, grid=(s // tm,), in_specs=[big, pl.BlockSpec((1, d), lambda i: (0, 0)), big],
        out_specs=[big, pl.BlockSpec((8, d), lambda i: (0, 0))],
        out_shape=[jax.ShapeDtypeStruct((s, d), f32), jax.ShapeDtypeStruct((8, d), f32)],
        compiler_params=_params("arbitrary"),
    )(x, fn, tgt)


FC = 2 * F // 4
FFN_ROWS = 256
FFN_CHUNKS = 1


def _resident(pairs, sems):
    @pl.when(pl.program_id(0) == 0)
    def _():
        cps = [pltpu.make_async_copy(h, v, sems.at[i]) for i, (h, v) in enumerate(pairs)]
        for cp in cps:
            cp.start()
        for cp in cps:
            cp.wait()


def ffn_fwd(x, nw, sc, sh, g, w_in, w_out, carry=(), *, name):
    s, d = x.shape
    tm = _row_tile(s, FFN_ROWS)
    nsteps = s // tm
    nc = len(carry)

    def body(*refs):
        x_ref, nw_ref, sc_ref, sh_ref, g_ref, win_hbm, wout_hbm = refs[:7]
        h_ref, gu_ref, a_ref, o_ref = refs[7 + nc:11 + nc]
        win_v, wout_v, sems = refs[11 + 2 * nc:14 + 2 * nc]
        if nc:
            start, finish = _gather_direct(refs[7:7 + nc], refs[11 + nc:11 + 2 * nc], *refs[14 + 2 * nc:])
            pl.when(pl.program_id(0) == 0)(start)
        _resident([(win_hbm, win_v), (wout_hbm, wout_v)], sems)
        rc = tm // FFN_CHUNKS
        rows = [slice(q * rc, (q + 1) * rc) for q in range(FFN_CHUNKS)]
        xs = [x_ref[rw, :] for rw in rows]
        hs = [((xv * lax.rsqrt(jnp.mean(xv * xv, axis=-1, keepdims=True) + RMS_EPS) * nw_ref[...]) * (1.0 + sc_ref[...])
               + sh_ref[...]).astype(bf16) for xv in xs]
        pairs = [(q, c) for q in range(FFN_CHUNKS) for c in range(2)]
        gts = [jnp.dot(hs[q], win_v[c], preferred_element_type=f32) for q, c in pairs]
        ups = [jnp.dot(hs[q], win_v[c + 2], preferred_element_type=f32) for q, c in pairs]
        acts = [(gt * jax.nn.sigmoid(gt) * up).astype(bf16) for gt, up in zip(gts, ups)]
        ys = [jnp.dot(act, wout_v[c * FC:(c + 1) * FC, :], preferred_element_type=f32) for act, (q, c) in zip(acts, pairs)]
        for (q, c), gt, up, act in zip(pairs, gts, ups, acts):
            cs = slice(c * FC, (c + 1) * FC)
            gu_ref[0, rows[q], cs] = gt.astype(bf16)
            gu_ref[1, rows[q], cs] = up.astype(bf16)
            a_ref[rows[q], cs] = act
        for q in range(FFN_CHUNKS):
            h_ref[rows[q], :] = hs[q]
            o_ref[rows[q], :] = xs[q] + g_ref[...] * (ys[2 * q] + ys[2 * q + 1])
        if nc:
            pl.when(pl.program_id(0) == nsteps - 1)(finish)

    big = pl.BlockSpec((tm, d), lambda i: (i, 0))
    vec = pl.BlockSpec((1, d), lambda i: (0, 0))
    return _pcall(
        body, name=name, grid=(nsteps,), in_specs=[big, vec, vec, vec, vec, ANY, ANY] + [ANY] * nc,
        out_specs=[big, pl.BlockSpec((2, tm, F), lambda i: (0, i, 0)), pl.BlockSpec((tm, F), lambda i: (i, 0)), big] + [ANY] * nc,
        out_shape=[jax.ShapeDtypeStruct((s, d), bf16), jax.ShapeDtypeStruct((2, s, F), bf16),
                   jax.ShapeDtypeStruct((s, F), bf16), jax.ShapeDtypeStruct((s, d), f32)]
        + [jax.ShapeDtypeStruct((4,) + tuple(sh_.shape), sh_.dtype) for sh_ in carry],
        scratch_shapes=[pltpu.VMEM((4, d, FC), bf16), pltpu.VMEM((F, d), bf16), pltpu.SemaphoreType.DMA((2,))]
        + (_gather_scratch(nc) if nc else []),
        compiler_params=_params("arbitrary"),
    )(x, nw, sc, sh, g, w_in, w_out, *carry)


def ffn_bwd_rows(dx, x, gu, g, nw, sc, w_in, w_out, carry=(), *, name):
    s, d = x.shape
    tm = _row_tile(s, FFN_ROWS)
    nsteps = s // tm
    nc = len(carry)
    nt_dims = (((1,), (1,)), ((), ()))

    def body(*refs):
        dx_ref, x_ref, gu_ref, g_ref, nw_ref, sc_ref, win_hbm, wout_hbm = refs[:8]
        dgu_ref, o_ref, st_ref = refs[8 + nc:11 + nc]
        win_v, wout_v, sems = refs[11 + 3 * nc:14 + 3 * nc]
        if nc:
            start, forward, finish = _exchange(refs[8:8 + nc], refs[11 + nc:11 + 2 * nc], refs[11 + 2 * nc:11 + 3 * nc],
                                               *refs[14 + 3 * nc:])
            pl.when(pl.program_id(0) == 0)(start)
            pl.when(pl.program_id(0) == max(nsteps - 1 - CARRY_LEAD, 0))(forward)
        _resident([(win_hbm, win_v), (wout_hbm, wout_v)], sems)
        dxv = dx_ref[...]
        a = (dxv * g_ref[...]).astype(bf16)
        halves = [slice(c * FC, (c + 1) * FC) for c in range(2)]
        das = [lax.dot_general(a, wout_v[cs, :], nt_dims, preferred_element_type=f32) for cs in halves]
        gts = [gu_ref[0, :, cs].astype(f32) for cs in halves]
        ups = [gu_ref[1, :, cs].astype(f32) for cs in halves]
        sgs = [jax.nn.sigmoid(gt) for gt in gts]
        dgates = [(da * up * (sg * (1.0 + gt * (1.0 - sg)))).astype(bf16) for da, gt, up, sg in zip(das, gts, ups, sgs)]
        dups = [(da * (gt * sg)).astype(bf16) for da, gt, sg in zip(das, gts, sgs)]
        for cs, dgate, dup in zip(halves, dgates, dups):
            dgu_ref[0, :, cs] = dgate
            dgu_ref[1, :, cs] = dup
        parts = [lax.dot_general(dgates[c], win_v[c], nt_dims, preferred_element_type=f32) for c in range(2)]
        parts += [lax.dot_general(dups[c], win_v[c + 2], nt_dims, preferred_element_type=f32) for c in range(2)]
        dh = (parts[0] + parts[1]) + (parts[2] + parts[3])
        xv = x_ref[...]
        r = lax.rsqrt(jnp.mean(xv * xv, axis=-1, keepdims=True) + RMS_EPS)
        xh = xv * r
        dn = dh * (1.0 + sc_ref[...])
        dxh = dn * nw_ref[...]
        o_ref[...] = dxv + r * (dxh - xh * jnp.mean(dxh * xh, axis=-1, keepdims=True))
        rows = jnp.concatenate([
            jnp.sum(dn * xh, axis=0, keepdims=True),
            jnp.sum(dh * (xh * nw_ref[...]), axis=0, keepdims=True),
            jnp.sum(dh, axis=0, keepdims=True),
            jnp.zeros((5, d), f32)], axis=0)

        @pl.when(pl.program_id(0) == 0)
        def _():
            st_ref[...] = rows

        @pl.when(pl.program_id(0) != 0)
        def _():
            st_ref[...] += rows

        if nc:
            pl.when(pl.program_id(0) == nsteps - 1)(finish)

    big = pl.BlockSpec((tm, d), lambda i: (i, 0))
    vec = pl.BlockSpec((1, d), lambda i: (0, 0))
    gus = pl.BlockSpec((2, tm, F), lambda i: (0, i, 0))
    cshapes = [jax.ShapeDtypeStruct(p.shape, p.dtype) for p in carry]
    res = _pcall(
        body, name=name, grid=(nsteps,), in_specs=[big, big, gus, vec, vec, vec, ANY, ANY] + [ANY] * nc,
        out_specs=[gus, big, pl.BlockSpec((8, d), lambda i: (0, 0))] + [ANY] * (2 * nc),
        out_shape=[jax.ShapeDtypeStruct((2, s, F), bf16), jax.ShapeDtypeStruct((s, d), f32), jax.ShapeDtypeStruct((8, d), f32)]
        + cshapes + cshapes,
        scratch_shapes=[pltpu.VMEM((4, d, FC), bf16), pltpu.VMEM((F, d), bf16), pltpu.SemaphoreType.DMA((2,))]
        + (_exchange_scratch(nc) if nc else []),
        compiler_params=_params("arbitrary"),
    )(dx, x, gu, g, nw, sc, w_in, w_out, *carry)
    return res[0], res[1], res[2], res[3:3 + nc], res[3 + nc:]


def ffn_dw_in(h, dgu, *, name):
    s, d = h.shape
    ts = _row_tile(s)
    ns = s // ts
    tn_dims = (((0,), (0,)), ((), ()))

    def body(h_ref, dgu_ref, o_ref, acc):
        k = pl.program_id(1)

        @pl.when(k == 0)
        def _():
            acc[...] = jnp.zeros_like(acc)

        hv = h_ref[...]
        for c in range(2):
            acc[c] += lax.dot_general(hv, dgu_ref[:, c * FC:(c + 1) * FC], tn_dims, preferred_element_type=f32)

        @pl.when(k == ns - 1)
        def _():
            o_ref[...] = acc[...].astype(bf16)

    return _pcall(
        body, name=name, grid=(2, ns),
        in_specs=[pl.BlockSpec((ts, d), lambda hf, k: (k, 0)), pl.BlockSpec((None, ts, F), lambda hf, k: (hf, k, 0))],
        out_specs=pl.BlockSpec((2, d, FC), lambda hf, k: (hf, 0, 0)),
        out_shape=jax.ShapeDtypeStruct((4, d, FC), bf16), scratch_shapes=[pltpu.VMEM((2, d, FC), f32)],
        compiler_params=_params("arbitrary", "arbitrary"),
    )(h, dgu)


def ffn_dw_out(a, dx, g, wb, *, name):
    s, fdim = a.shape
    d = dx.shape[1]
    ts = _row_tile(s)
    ns = s // ts
    tn = d // 2
    tn_dims = (((0,), (0,)), ((), ()))

    def body(a_ref, dx_ref, g_ref, wb_ref, o_ref, dg_ref, acc):
        k = pl.program_id(1)

        @pl.when(k == 0)
        def _():
            acc[...] = jnp.zeros_like(acc)

        acc[...] += lax.dot_general(a_ref[...], dx_ref[...].astype(bf16), tn_dims, preferred_element_type=f32)

        @pl.when(k == ns - 1)
        def _():
            gm = acc[...]
            dg_ref[...] = jnp.concatenate([jnp.sum(wb_ref[...].astype(f32) * gm, axis=0, keepdims=True),
                                           jnp.zeros((7, tn), f32)], axis=0)
            o_ref[...] = (gm * g_ref[...]).astype(bf16)

    return _pcall(
        body, name=name, grid=(2, ns),
        in_specs=[pl.BlockSpec((ts, fdim), lambda j, k: (k, 0)), pl.BlockSpec((ts, tn), lambda j, k: (k, j)),
                  pl.BlockSpec((1, tn), lambda j, k: (0, j)), pl.BlockSpec((fdim, tn), lambda j, k: (0, j))],
        out_specs=[pl.BlockSpec((fdim, tn), lambda j, k: (0, j)), pl.BlockSpec((8, tn), lambda j, k: (0, j))],
        out_shape=[jax.ShapeDtypeStruct((fdim, d), bf16), jax.ShapeDtypeStruct((8, d), f32)],
        scratch_shapes=[pltpu.VMEM((fdim, tn), f32)], compiler_params=_params("arbitrary", "arbitrary"),
    )(a, dx, g, wb)


def _alibi(n):
    return np.asarray(2.0 ** (-8.0 * np.arange(1, n + 1) / n), dtype=np.float32)


class _Attn:
    def __init__(self, s, *, mixer, group=0):
        if mixer == "a":
            self.blk, self.dil, self.npairs = 128, 1, 2
            self.qb0, self.kb0, self.vb0 = 0, 8, 10
            slopes = _alibi(16).reshape(2, 2, GQ)
        else:
            window, dil = B_GROUPS[group]
            self.blk, self.dil, self.npairs = window // (2 * dil), dil, 1
            self.qb0, self.kb0, self.vb0 = (0, 12, 15) if dil == 1 else (0, 4, 5)
            slopes = _alibi(24).reshape(3, 1, 2, GQ)[group]
        self.l = s // self.dil
        self.t = min(1024, self.l)
        self.nt = self.l // self.t
        self.nb = self.t // self.blk
        blk = self.blk
        qi = np.arange(blk)[:, None]
        rel = np.arange(3 * blk)[None, :] - blk - qi
        dist = (self.dil * np.abs(rel)).astype(np.float32)
        bias = -slopes[:, :, :, None, None] * dist[None, None, None]
        bias = np.where(np.abs(rel) <= blk, bias, np.float32(NEG)).astype(np.float32)
        self.bias = np.ascontiguousarray(np.swapaxes(bias.reshape(self.npairs, 2, GQ * blk, 3 * blk), -1, -2))

    def grid(self):
        return (self.dil, self.npairs, self.nt)

    def tile(self, width, col):
        return pl.BlockSpec((None, self.t, width), lambda r, hp, i: (r, i, col(hp)))

    def halo(self, width, col):
        t, blk, nbl = self.t, self.blk, self.l // self.blk
        per = t // blk
        return [
            pl.BlockSpec((None, blk, width), lambda r, hp, i: (r, jnp.maximum(i * per - 1, 0), col(hp))),
            self.tile(width, col),
            pl.BlockSpec((None, blk, width), lambda r, hp, i: (r, jnp.minimum((i + 1) * per, nbl - 1), col(hp))),
        ]

    def qcol(self, e):
        return lambda hp: self.qb0 + 2 * hp + e

    def kcol(self, hp):
        return self.kb0 + hp

    def vcol(self, hp):
        return self.vb0 + hp

    def pcol(self, hp):
        return hp


def _stack_heads(x):
    return jnp.concatenate([x[:, g * DH:(g + 1) * DH] for g in range(GQ)], axis=0)


def _unstack_heads(x, rows):
    return jnp.concatenate([x[g * rows:(g + 1) * rows] for g in range(GQ)], axis=1)


def _head_cols(tile, hh, rows):
    return jnp.concatenate([tile[:, hh * GQ + g:hh * GQ + g + 1] for g in range(GQ)], axis=0)


def _carrying(body, n_in, n_out, n_scratch, carry, kind, grid):
    nc = len(carry)
    if not nc:
        return body, [], [], [], []
    n_res = nc if kind == "gather" else 2 * nc

    def wrapped(*refs):
        ins, src = refs[:n_in], refs[n_in:n_in + nc]
        outs = refs[n_in + nc:n_in + nc + n_out]
        res = refs[n_in + nc + n_out:n_in + nc + n_out + n_res]
        scr = refs[n_in + nc + n_out + n_res:n_in + nc + n_out + n_res + n_scratch]
        sems = refs[n_in + nc + n_out + n_res + n_scratch:]
        ids = [pl.program_id(a) for a in range(len(grid))]
        first = functools.reduce(jnp.logical_and, [i == 0 for i in ids])
        last = functools.reduce(jnp.logical_and, [i == g - 1 for i, g in zip(ids, grid)])
        if kind == "gather":
            start, finish = _gather_direct(src, res, *sems)
            pl.when(first)(start)
        else:
            start, forward, finish = _exchange(src, res[:nc], res[nc:], *sems)
            pl.when(first)(start)
            early = [g - 1 for g in grid[:-1]] + [max(grid[-1] - 1 - CARRY_LEAD, 0)]
            pl.when(functools.reduce(jnp.logical_and, [i == g for i, g in zip(ids, early)]))(forward)
        body(*ins, *outs, *scr)
        pl.when(last)(finish)

    if kind == "gather":
        shapes = [jax.ShapeDtypeStruct((4,) + tuple(c.shape), c.dtype) for c in carry]
        sems = _gather_scratch(nc)
    else:
        shapes = [jax.ShapeDtypeStruct(c.shape, c.dtype) for c in carry] * 2
        sems = _exchange_scratch(nc)
    return wrapped, [ANY] * nc, [ANY] * n_res, shapes, sems


def attn_fwd(qkv, sinkcol, cfg, carry=(), *, out_dtype, name):
    blk, t, nb, nt, dil = cfg.blk, cfg.t, cfg.nb, cfg.nt, cfg.dil
    scale = DH ** -0.5

    def body(q0, q1, kp, km, kn, vp, vm, vn, bias_ref, sink_ref, o_ref, lse_ref, kx, vx):
        ti = pl.program_id(2)
        first, last = ti == 0, ti == nt - 1
        for hh in range(2):
            sl = slice(hh * DH, (hh + 1) * DH)
            for dst, (p_, m_, n_) in ((kx, (kp, km, kn)), (vx, (vp, vm, vn))):
                dst[hh, 0:blk] = p_[:, sl]
                dst[hh, blk:blk + t] = m_[:, sl]
                dst[hh, blk + t:] = n_[:, sl]
        krow = lax.broadcasted_iota(jnp.int32, (3 * blk, GQ * blk), 0)
        pairs = [(b, hh) for b in range(nb) for hh in range(2)]
        qs = [_stack_heads((q0, q1)[hh][b * blk:(b + 1) * blk, :]) * scale for b, hh in pairs]
        sc = [lax.dot_general(kx[hh, b * blk:(b + 3) * blk, :], q_, (((1,), (1,)), ((), ())), preferred_element_type=f32)
              for q_, (b, hh) in zip(qs, pairs)]
        sc = [s_ + bias_ref[0, hh] for s_, (b, hh) in zip(sc, pairs)]
        sc = [jnp.where(jnp.logical_and(first, krow < blk), NEG, s_) if b == 0 else s_ for s_, (b, hh) in zip(sc, pairs)]
        sc = [jnp.where(jnp.logical_and(last, krow >= 2 * blk), NEG, s_) if b == nb - 1 else s_ for s_, (b, hh) in zip(sc, pairs)]
        ms = [jnp.maximum(jnp.max(s_, axis=0, keepdims=True), sink_ref[0, hh]) for s_, (b, hh) in zip(sc, pairs)]
        ps = [jnp.exp(s_ - m_) for s_, m_ in zip(sc, ms)]
        ls = [jnp.sum(p_, axis=0, keepdims=True) + jnp.exp(sink_ref[0, hh] - m_) for p_, m_, (b, hh) in zip(ps, ms, pairs)]
        os_ = [lax.dot_general(vx[hh, b * blk:(b + 3) * blk, :], p_.astype(bf16), (((0,), (0,)), ((), ())),
                               preferred_element_type=f32) for p_, (b, hh) in zip(ps, pairs)]
        os_ = [o_ / l_ for o_, l_ in zip(os_, ls)]
        lses = [m_ + jnp.log(l_) for m_, l_ in zip(ms, ls)]
        for o_, (b, hh) in zip(os_, pairs):
            o_ref[b * blk:(b + 1) * blk, hh * 256:(hh + 1) * 256] = _unstack_heads(o_.T, blk).astype(out_dtype)
        stat_rows = [jnp.concatenate([lses[2 * b + hh][:, g * blk:(g + 1) * blk] for b in range(nb)], axis=1)
                     for hh in range(2) for g in range(GQ)]
        lse_ref[...] = jnp.concatenate(stat_rows + [jnp.zeros((128 - 2 * GQ, t), f32)], axis=0).T

    in_specs = [cfg.tile(256, cfg.qcol(e)) for e in range(2)]
    in_specs += cfg.halo(128, cfg.kcol) + cfg.halo(128, cfg.vcol)
    in_specs += [pl.BlockSpec((1, 2, 3 * blk, GQ * blk), lambda r, hp, i: (hp, 0, 0, 0)),
                 pl.BlockSpec((1, 2, 1, GQ * blk), lambda r, hp, i: (hp, 0, 0, 0))]
    body, c_in, c_out, c_shape, c_sems = _carrying(body, 10, 2, 2, carry, "gather", cfg.grid())
    return _pcall(
        body, name=name, grid=cfg.grid(), in_specs=in_specs + c_in,
        out_specs=[cfg.tile(512, cfg.pcol), cfg.tile(128, cfg.pcol)] + c_out,
        out_shape=[jax.ShapeDtypeStruct((dil, cfg.l, cfg.npairs * 512), out_dtype),
                   jax.ShapeDtypeStruct((dil, cfg.l, cfg.npairs * 128), f32)] + c_shape,
        scratch_shapes=[pltpu.VMEM((2, t + 2 * blk, DH), bf16), pltpu.VMEM((2, t + 2 * blk, DH), bf16)] + c_sems,
        compiler_params=_params("arbitrary", "arbitrary", "arbitrary"),
    )(*([qkv] * 8), jnp.asarray(cfg.bias), sinkcol, *carry)


def attn_bwd(qkv, do, lse, delta, cfg, carry=(), *, name):
    blk, t, nb, nt, dil, npairs = cfg.blk, cfg.t, cfg.nb, cfg.nt, cfg.dil, cfg.npairs
    scale = DH ** -0.5
    nt_dims = (((1,), (1,)), ((), ()))
    tn_dims = (((0,), (0,)), ((), ()))

    def body(q0p, q0m, q0n, q1p, q1m, q1n, kp, km, kn, vp, vm, vn, dop, dom, don, lp, lm, ln, dp_, dm_, dn_,
             bias_ref, dq_ref, dk_ref, dv_ref, kx, vx, dkx, dvx):
        ti = pl.program_id(2)
        first, last = ti == 0, ti == nt - 1
        for hh in range(2):
            sl = slice(hh * DH, (hh + 1) * DH)
            for dst, (p_, m_, n_) in ((kx, (kp, km, kn)), (vx, (vp, vm, vn))):
                dst[hh, 0:blk] = p_[:, sl]
                dst[hh, blk:blk + t] = m_[:, sl]
                dst[hh, blk + t:] = n_[:, sl]
        dkx[...] = jnp.zeros_like(dkx)
        dvx[...] = jnp.zeros_like(dvx)

        def slab(prev, main, nxt, e):
            if e == 0:
                return prev[...]
            if e == nb + 1:
                return nxt[...]
            return main[(e - 1) * blk:e * blk, :]

        krow = lax.broadcasted_iota(jnp.int32, (3 * blk, GQ * blk), 0)

        def keys(e):
            if e == 0:
                return 1, 2, slice(2 * blk, 3 * blk)
            if e == nb + 1:
                return nb, nb + 1, slice(0, blk)
            return e - 1, e + 2, slice(0, 3 * blk)

        def edge(sc, e):
            if e == 0:
                return jnp.where(first, NEG, sc)
            if e == nb + 1:
                return jnp.where(last, NEG, sc)
            if e == 1:
                sc = jnp.where(jnp.logical_and(first, krow < blk), NEG, sc)
            if e == nb:
                sc = jnp.where(jnp.logical_and(last, krow >= 2 * blk), NEG, sc)
            return sc

        lse_t = [lp[...].T, lm[...].T, ln[...].T]
        dl_t = [dp_[...].T, dm_[...].T, dn_[...].T]

        def stat_row(parts, e, hh):
            src, lo = (parts[0], 0) if e == 0 else (parts[2], 0) if e == nb + 1 else (parts[1], (e - 1) * blk)
            return jnp.concatenate([src[hh * GQ + g:hh * GQ + g + 1, lo:lo + blk] for g in range(GQ)], axis=1)

        pairs = [(e, hh) for e in range(nb + 2) for hh in range(2)]
        qs = [_stack_heads(slab(*((q0p, q0m, q0n), (q1p, q1m, q1n))[hh], e)) * scale for e, hh in pairs]
        dos = [_stack_heads(slab(dop, dom, don, e)[:, hh * 256:(hh + 1) * 256]) for e, hh in pairs]
        lse_r = [stat_row(lse_t, e, hh) for e, hh in pairs]
        dl_r = [stat_row(dl_t, e, hh) for e, hh in pairs]
        kw = [kx[hh, keys(e)[0] * blk:keys(e)[1] * blk, :] for e, hh in pairs]
        vw = [vx[hh, keys(e)[0] * blk:keys(e)[1] * blk, :] for e, hh in pairs]
        sc = [lax.dot_general(k_, q_, nt_dims, preferred_element_type=f32) for q_, k_ in zip(qs, kw)]
        dp = [lax.dot_general(v_, d_, nt_dims, preferred_element_type=f32) for d_, v_ in zip(dos, vw)]
        sc = [edge(s_ + bias_ref[0, hh, keys(e)[2], :], e) for s_, (e, hh) in zip(sc, pairs)]
        ps = [jnp.exp(s_ - l_) for s_, l_ in zip(sc, lse_r)]
        ds = [(p_ * (d_ - c_)).astype(bf16) for p_, d_, c_ in zip(ps, dp, dl_r)]
        pb = [p_.astype(bf16) for p_ in ps]
        dks = [jnp.dot(s_, q_, preferred_element_type=f32) for s_, q_ in zip(ds, qs)]
        dvs = [jnp.dot(p_, d_, preferred_element_type=f32) for p_, d_ in zip(pb, dos)]
        dqs = [lax.dot_general(s_, k_, tn_dims, preferred_element_type=f32) if 1 <= e <= nb else None
               for s_, k_, (e, hh) in zip(ds, kw, pairs)]
        for dk_, dv_, dq_, (e, hh) in zip(dks, dvs, dqs, pairs):
            k0, k1, _ = keys(e)
            dkx[hh, k0 * blk:k1 * blk, :] += dk_
            dvx[hh, k0 * blk:k1 * blk, :] += dv_
            if dq_ is not None:
                dq_ref[(e - 1) * blk:e * blk, hh * 256:(hh + 1) * 256] = (_unstack_heads(dq_, blk) * scale).astype(bf16)
        for hh in range(2):
            dk_ref[:, hh * DH:(hh + 1) * DH] = dkx[hh, blk:blk + t, :].astype(bf16)
            dv_ref[:, hh * DH:(hh + 1) * DH] = dvx[hh, blk:blk + t, :].astype(bf16)

    in_specs = cfg.halo(256, cfg.qcol(0)) + cfg.halo(256, cfg.qcol(1))
    in_specs += cfg.halo(128, cfg.kcol) + cfg.halo(128, cfg.vcol)
    in_specs += cfg.halo(512, cfg.pcol) + cfg.halo(128, cfg.pcol) + cfg.halo(128, cfg.pcol)
    in_specs += [pl.BlockSpec((1, 2, 3 * blk, GQ * blk), lambda r, hp, i: (hp, 0, 0, 0))]
    body, c_in, c_out, c_shape, c_sems = _carrying(body, 22, 3, 4, carry, "exchange", cfg.grid())
    res = _pcall(
        body, name=name, grid=cfg.grid(), in_specs=in_specs + c_in,
        out_specs=[cfg.tile(512, cfg.pcol), cfg.tile(128, cfg.pcol), cfg.tile(128, cfg.pcol)] + c_out,
        out_shape=[jax.ShapeDtypeStruct((dil, cfg.l, npairs * 512), bf16),
                   jax.ShapeDtypeStruct((dil, cfg.l, npairs * 128), bf16),
                   jax.ShapeDtypeStruct((dil, cfg.l, npairs * 128), bf16)] + c_shape,
        scratch_shapes=[pltpu.VMEM((2, t + 2 * blk, DH), bf16), pltpu.VMEM((2, t + 2 * blk, DH), bf16),
                        pltpu.VMEM((2, t + 2 * blk, DH), f32), pltpu.VMEM((2, t + 2 * blk, DH), f32)] + c_sems,
        compiler_params=_params("arbitrary", "arbitrary", "arbitrary"),
    )(*([qkv] * 12), do, do, do, lse, lse, lse, delta, delta, delta, jnp.asarray(cfg.bias), *carry)
    nc = len(carry)
    return (res[0], res[1], res[2], res[3:3 + nc], res[3 + nc:]) if nc else res


def _head_indicator(nheads):
    e = np.zeros((nheads * DH, (nheads // 8) * 128), np.float32)
    for c in range(nheads * DH):
        h = c // DH
        e[c, (h // 8) * 128 + h % 8] = 1.0
    return e


def _dot_split(x, e):
    hi = x.astype(bf16)
    lo = (x - hi.astype(f32)).astype(bf16)
    return jnp.dot(hi, e, preferred_element_type=f32) + jnp.dot(lo, e, preferred_element_type=f32)


def _spread(scr, x, d):
    tm, w = x.shape
    for j in range(w // 128):
        scr[j] = x[:, j * 128:(j + 1) * 128]
    return [jnp.concatenate([scr[j, pl.ds(r, tm // d, stride=d), :] for j in range(w // 128)], axis=1) for r in range(d)]


def _weave(scr, blocks, d):
    n, w = blocks[0].shape
    for r in range(d):
        for j in range(w // 128):
            scr[j, pl.ds(r, n, stride=d), :] = blocks[r][:, j * 128:(j + 1) * 128]
    return jnp.concatenate([scr[j] for j in range(w // 128)], axis=1)


def _res_spec(d, tm, w):
    return pl.BlockSpec((d, tm // d, w), lambda i: (0, i, 0))


DILATED = tuple(dil for _, dil in B_GROUPS[1:])


def b_to_strided(qkv, *, name):
    s = qkv.shape[0]
    tm = _row_tile(s)

    def body(x_ref, *rest):
        outs, scr = rest[:-1], rest[-1]
        for gi, (o_ref, d) in enumerate(zip(outs, DILATED), start=1):
            cols = jnp.concatenate([x_ref[:, gi * 512:(gi + 1) * 512], x_ref[:, 1536 + gi * 128:1536 + (gi + 1) * 128],
                                    x_ref[:, 1920 + gi * 128:1920 + (gi + 1) * 128]], axis=1).astype(f32)
            for r, blk_ in enumerate(_spread(scr, cols, d)):
                o_ref[r] = blk_.astype(bf16)

    return _pcall(
        body, name=name, grid=(s // tm,), in_specs=[pl.BlockSpec((tm, B_QKV), lambda i: (i, 0))],
        out_specs=[_res_spec(d, tm, 768) for d in DILATED],
        out_shape=[jax.ShapeDtypeStruct((d, s // d, 768), bf16) for d in DILATED],
        scratch_shapes=[pltpu.VMEM((6, tm, 128), f32)], compiler_params=_params("parallel"),
    )(qkv)


def b_bwd_to_strided(do, lse, delta, *, name):
    s = do.shape[0]
    tm = _row_tile(s)

    def body(do_ref, lse_ref, dl_ref, *rest):
        outs, scr = rest[:-1], rest[-1]
        allc = jnp.concatenate([do_ref[...].astype(f32), lse_ref[...], dl_ref[...]], axis=1)
        for gi, d in enumerate(DILATED):
            o_do, o_lse, o_dl = outs[3 * gi:3 * gi + 3]
            for r, blk_ in enumerate(_spread(scr, allc, d)):
                o_do[r] = blk_[:, :512].astype(bf16)
                o_lse[r] = blk_[:, 512:640]
                o_dl[r] = blk_[:, 640:768]

    out_specs, out_shape = [], []
    for d in DILATED:
        out_specs += [_res_spec(d, tm, 512), _res_spec(d, tm, 128), _res_spec(d, tm, 128)]
        out_shape += [jax.ShapeDtypeStruct((d, s // d, 512), bf16), jax.ShapeDtypeStruct((d, s // d, 128), f32),
                      jax.ShapeDtypeStruct((d, s // d, 128), f32)]
    return _pcall(
        body, name=name, grid=(s // tm,),
        in_specs=[pl.BlockSpec((tm, 512), lambda i: (i, 0)), pl.BlockSpec((tm, 128), lambda i: (i, 0)),
                  pl.BlockSpec((tm, 128), lambda i: (i, 0))],
        out_specs=out_specs, out_shape=out_shape, scratch_shapes=[pltpu.VMEM((6, tm, 128), f32)],
        compiler_params=_params("parallel"),
    )(do, lse, delta)


def b_from_strided(grads, *, name):
    s = grads[0][0].shape[1]
    tm = _row_tile(s)

    def body(*refs):
        ins, o_ref, scr = refs[:9], refs[9], refs[10]
        nat = [jnp.concatenate([ins[q][0].astype(f32) for q in range(3)], axis=1)]
        for gi, d in enumerate(DILATED, start=1):
            blocks = [jnp.concatenate([ins[3 * gi + q][r].astype(f32) for q in range(3)], axis=1) for r in range(d)]
            nat.append(_weave(scr, blocks, d))
        pieces = [nat[g][:, lo:hi] for lo, hi in ((0, 512), (512, 640), (640, 768)) for g in range(3)]
        o_ref[...] = jnp.concatenate(pieces, axis=1).astype(bf16)

    dils = (1,) + DILATED
    in_specs = [_res_spec(d, tm, w) for d in dils for w in (512, 128, 128)]
    return _pcall(
        body, name=name, grid=(s * 1 // tm,), in_specs=in_specs, out_specs=pl.BlockSpec((tm, B_QKV), lambda i: (i, 0)),
        out_shape=jax.ShapeDtypeStruct((s, B_QKV), bf16), scratch_shapes=[pltpu.VMEM((6, tm, 128), f32)],
        compiler_params=_params("parallel"),
    )(*[a for g in grads for a in g])


def attn_merge(os_, lses, *, name):
    s = os_[0].shape[1]
    tm = _row_tile(s)
    ind_t = jnp.asarray(_head_indicator(8).T, dtype=bf16)
    dils = (1,) + DILATED

    def body(o0, o1, o2, l0, l1, l2, e_ref, o_ref, lse_ref, scr):
        both = [jnp.concatenate([o0[0], l0[0]], axis=1)]
        for og, lg, d in ((o1, l1, dils[1]), (o2, l2, dils[2])):
            both.append(_weave(scr, [jnp.concatenate([og[r], lg[r]], axis=1) for r in range(d)], d))
        ls = [b[:, 512:640] for b in both]
        m = jnp.maximum(jnp.maximum(ls[0], ls[1]), ls[2])
        tot = m + jnp.log(jnp.exp(ls[0] - m) + jnp.exp(ls[1] - m) + jnp.exp(ls[2] - m))
        lse_ref[...] = tot
        acc = jnp.zeros((tm, B_OUT), f32)
        for b, lg in zip(both, ls):
            acc = acc + _dot_split(jnp.exp(lg - tot), e_ref[...]) * b[:, :512]
        o_ref[...] = acc.astype(bf16)

    return _pcall(
        body, name=name, grid=(s * 1 // tm,),
        in_specs=[_res_spec(d, tm, 512) for d in dils] + [_res_spec(d, tm, 128) for d in dils]
        + [pl.BlockSpec((128, B_OUT), lambda i: (0, 0))],
        out_specs=[pl.BlockSpec((tm, B_OUT), lambda i: (i, 0)), pl.BlockSpec((tm, 128), lambda i: (i, 0))],
        out_shape=[jax.ShapeDtypeStruct((s, B_OUT), bf16), jax.ShapeDtypeStruct((s, 128), f32)],
        scratch_shapes=[pltpu.VMEM((5, tm, 128), f32)], compiler_params=_params("parallel"),
    )(*os_, *lses, ind_t)


def ada_mod(c_all, w, b, *, name):
    n = w.shape[2]

    def body(c_ref, w_ref, b_ref, o_ref):
        cv = c_ref[...]
        cond = cv * jax.nn.sigmoid(cv)
        o_ref[0] = jnp.dot(cond, w_ref[0], preferred_element_type=f32, precision=lax.Precision.HIGHEST) + b_ref[0]

    return _pcall(
        body, name=name, grid=(DEPTH,),
        in_specs=[pl.BlockSpec((N_DEV, D), lambda i: (0, 0)), pl.BlockSpec((1, D, n), lambda i: (i, 0, 0)),
                  pl.BlockSpec((1, 1, n), lambda i: (i, 0, 0))],
        out_specs=pl.BlockSpec((1, N_DEV, n), lambda i: (i, 0, 0)),
        out_shape=jax.ShapeDtypeStruct((DEPTH, N_DEV, n), f32), compiler_params=_params("arbitrary"),
    )(c_all, w, b)


def ada_grad(c_t, dm, *, name):
    n = dm.shape[2]

    def body(c_ref, dm_ref, o_ref):
        cv = c_ref[...]
        cond = cv * jax.nn.sigmoid(cv)
        acc = cond[:, 0:1] * dm_ref[0, 0:1, :]
        for b in range(1, N_DEV):
            acc = acc + cond[:, b:b + 1] * dm_ref[0, b:b + 1, :]
        o_ref[0] = acc

    return _pcall(
        body, name=name, grid=(DEPTH,),
        in_specs=[pl.BlockSpec((D, N_DEV), lambda i: (0, 0)), pl.BlockSpec((1, N_DEV, n), lambda i: (i, 0, 0))],
        out_specs=pl.BlockSpec((1, D, n), lambda i: (i, 0, 0)),
        out_shape=jax.ShapeDtypeStruct((DEPTH, D, n), f32), compiler_params=_params("arbitrary"),
    )(c_t, dm)


def _adam_math(w, g, m, v):
    m2 = B1 * m + (1.0 - B1) * g
    v2 = B2 * v + (1.0 - B2) * (g * g)
    mh = m2 / (1.0 - B1 ** STEP)
    vh = v2 / (1.0 - B2 ** STEP)
    return -LR * (mh / (jnp.sqrt(vh) + ADAM_EPS) + WD * w), m2, v2


def adamw(w, m, v, g, *, name):
    r, c = w.shape
    tr = 256 if r % 256 == 0 else r

    def body(w_ref, m_ref, v_ref, g_ref, d_ref, m2_ref, v2_ref):
        d_ref[...], m2_ref[...], v2_ref[...] = _adam_math(w_ref[...], g_ref[...], m_ref[...], v_ref[...])

    spec = pl.BlockSpec((tr, c), lambda i: (i, 0))
    return _pcall(
        body, name=name, grid=(r // tr,), in_specs=[spec] * 4, out_specs=[spec] * 3,
        out_shape=[jax.ShapeDtypeStruct((r, c), f32)] * 3, compiler_params=_params("parallel"),
    )(w, m, v, g)


def adamw_parts(w, m, v, own, sib, layer, prev=None, *, name):
    c = w.shape[1]
    r = own.shape[1]
    tr = 256 if r % 256 == 0 else r // 2
    off = layer * (r // tr)

    def body(w_ref, m_ref, v_ref, own_ref, sib_ref, *rest):
        g_ref, d_ref, m2_ref, v2_ref = rest[-4:]

        def total(ref):
            return ((ref[0].astype(f32) + ref[1].astype(f32)) + ref[2].astype(f32)) + ref[3].astype(f32)

        g = total(own_ref) + total(sib_ref)
        g_ref[...] = g
        d_ref[...], m2_ref[...], v2_ref[...] = _adam_math(w_ref[...], g, m_ref[...], v_ref[...])

    spec = pl.BlockSpec((tr, c), lambda i: (off + i, 0))
    pspec = pl.BlockSpec((4, tr, c), lambda i: (0, i, 0))
    prev = () if prev is None else tuple(prev)
    return _pcall(
        body, name=name, grid=(r // tr,), in_specs=[spec] * 3 + [pspec] * 2 + [ANY] * len(prev), out_specs=[spec] * 4,
        out_shape=[jax.ShapeDtypeStruct(w.shape, f32)] * 4,
        input_output_aliases={5 + q: q for q in range(len(prev))}, compiler_params=_params("parallel"),
    )(w, m, v, own, sib, *prev)


def sum_devices(g, *, name):
    _, r, c = g.shape

    def body(g_ref, o_ref):
        acc = g_ref[0]
        for k in range(1, N_DEV):
            acc = acc + g_ref[k]
        o_ref[...] = acc

    return _pcall(body, name=name, out_shape=jax.ShapeDtypeStruct((r, c), f32))(g)


def _place():
    x, y, c = lax.axis_index("x"), lax.axis_index("y"), lax.axis_index("c")
    chips = [(1 - x, y), (x, 1 - y), (1 - x, 1 - y)]
    return x, y, c, chips


def allgather8(v, *, name):
    r, c_ = v.shape

    def body(v_ref, o_ref, send_sems, recv_sems, local_sem):
        x, y, c, _ = _place()
        me = 4 * x + 2 * y + c
        mine = pltpu.make_async_copy(v_ref, o_ref.at[me], local_sem)
        mine.start()
        flips = [(fx, fy, fc) for fx in (0, 1) for fy in (0, 1) for fc in (0, 1)][1:]

        def peer(f):
            return (x ^ f[0], y ^ f[1], c ^ f[2])

        def copy(k, slot, to):
            return pltpu.make_async_remote_copy(
                src_ref=v_ref, dst_ref=o_ref.at[slot], send_sem=send_sems.at[k], recv_sem=recv_sems.at[k],
                device_id=to, device_id_type=MESH)

        sends = [copy(k, me, peer(f)) for k, f in enumerate(flips)]
        for cp in sends:
            cp.start()
        for k, f in enumerate(flips):
            px, py, pc = peer(f)
            copy(k, 4 * px + 2 * py + pc, (x, y, c)).wait_recv()
        for cp in sends:
            cp.wait_send()
        mine.wait()

    return _pcall(
        body, name=name, in_specs=[ANY], out_specs=ANY, out_shape=jax.ShapeDtypeStruct((N_DEV, r, c_), v.dtype),
        scratch_shapes=[pltpu.SemaphoreType.DMA((7,)), pltpu.SemaphoreType.DMA((7,)), pltpu.SemaphoreType.DMA],
    )(v)


def gather_weights(shards, *, name):
    n = len(shards)

    def body(*refs):
        src, out = refs[:n], refs[n:2 * n]
        send_a, recv_a, send_f, recv_f, local_sems = refs[2 * n:]
        x, y, c, chips = _place()
        sib = (x, y, 1 - c)
        me = 2 * x + y
        locals_ = [pltpu.make_async_copy(src[a], out[a].at[me], local_sems.at[a]) for a in range(n)]
        for cp in locals_:
            cp.start()

        def half(a, which):
            rh = src[a].shape[0] // 2
            return pl.ds(which * rh, rh)

        def first(a, k, chip_from, to):
            slot = 2 * chip_from[0] + chip_from[1]
            s_ref = src[a].at[half(a, c)]
            return pltpu.make_async_remote_copy(
                src_ref=s_ref, dst_ref=out[a].at[slot, half(a, c)], send_sem=send_a.at[3 * a + k],
                recv_sem=recv_a.at[3 * a + k], device_id=to, device_id_type=MESH)

        def passed(a, k, chip_from, which, to):
            slot = 2 * chip_from[0] + chip_from[1]
            ref = out[a].at[slot, half(a, which)]
            return pltpu.make_async_remote_copy(
                src_ref=ref, dst_ref=ref, send_sem=send_f.at[3 * a + k], recv_sem=recv_f.at[3 * a + k],
                device_id=to, device_id_type=MESH)

        sends = [first(a, k, (x, y), (*chip, c)) for a in range(n) for k, chip in enumerate(chips)]
        for cp in sends:
            cp.start()
        fwd = []
        for a in range(n):
            for k, chip in enumerate(chips):
                first(a, k, chip, (x, y, c)).wait_recv()
                cp = passed(a, k, chip, c, sib)
                cp.start()
                fwd.append(cp)
        for a in range(n):
            for k, chip in enumerate(chips):
                passed(a, k, chip, 1 - c, (x, y, c)).wait_recv()
        for cp in sends + fwd:
            cp.wait_send()
        for cp in locals_:
            cp.wait()

    return _pcall(
        body, name=name, in_specs=[ANY] * n, out_specs=[ANY] * n,
        out_shape=[jax.ShapeDtypeStruct((4,) + tuple(sh.shape), sh.dtype) for sh in shards],
        scratch_shapes=[pltpu.SemaphoreType.DMA((3 * n,)) for _ in range(4)] + [pltpu.SemaphoreType.DMA((n,))],
    )(*shards)


def _gather_direct(src, out, send_sems, recv_sems, local_sems):
    n = len(src)
    x, y, c, chips = _place()
    me = 2 * x + y

    def copy(a, k, slot, to):
        return pltpu.make_async_remote_copy(
            src_ref=src[a], dst_ref=out[a].at[slot], send_sem=send_sems.at[3 * a + k], recv_sem=recv_sems.at[3 * a + k],
            device_id=to, device_id_type=MESH)

    def start():
        for a in range(n):
            pltpu.make_async_copy(src[a], out[a].at[me], local_sems.at[a]).start()
            for k, chip in enumerate(chips):
                copy(a, k, me, (*chip, c)).start()

    def finish():
        for a in range(n):
            for k, chip in enumerate(chips):
                copy(a, k, 2 * chip[0] + chip[1], (x, y, c)).wait_recv()
        for a in range(n):
            for k in range(3):
                copy(a, k, me, (x, y, c)).wait_send()
            pltpu.make_async_copy(src[a], out[a].at[me], local_sems.at[a]).wait()

    return start, finish


def _gather_scratch(n):
    return [pltpu.SemaphoreType.DMA((3 * n,)), pltpu.SemaphoreType.DMA((3 * n,)), pltpu.SemaphoreType.DMA((n,))]


def _exchange(src, own, sibo, send_sems, recv_sems, local_sems):
    n = len(src)
    x, y, c, chips = _place()
    sib = (x, y, 1 - c)
    me = 2 * x + y

    def slot(chip):
        return 2 * chip[0] + chip[1]

    def copy(a, k, s_ref, d_ref, to):
        return pltpu.make_async_remote_copy(
            src_ref=s_ref, dst_ref=d_ref, send_sem=send_sems.at[7 * a + k], recv_sem=recv_sems.at[7 * a + k],
            device_id=to, device_id_type=MESH)

    def start():
        for a in range(n):
            pltpu.make_async_copy(src[a].at[me], own[a].at[me], local_sems.at[a]).start()
            copy(a, 0, src[a].at[me], sibo[a].at[me], sib).start()
            for k, chip in enumerate(chips):
                copy(a, 1 + k, src[a].at[slot(chip)], own[a].at[me], (*chip, c)).start()

    def forward():
        for a in range(n):
            for k, chip in enumerate(chips):
                copy(a, 1 + k, src[a].at[me], own[a].at[slot(chip)], (x, y, c)).wait_recv()
                copy(a, 4 + k, own[a].at[slot(chip)], sibo[a].at[slot(chip)], sib).start()

    def finish():
        for a in range(n):
            copy(a, 0, src[a].at[me], sibo[a].at[me], (x, y, c)).wait_recv()
            for k, chip in enumerate(chips):
                copy(a, 4 + k, src[a].at[me], sibo[a].at[slot(chip)], (x, y, c)).wait_recv()
        for a in range(n):
            for k in range(7):
                copy(a, k, src[a].at[me], own[a].at[me], (x, y, c)).wait_send()
            pltpu.make_async_copy(src[a].at[me], own[a].at[me], local_sems.at[a]).wait()

    return start, forward, finish


def _exchange_scratch(n):
    return [pltpu.SemaphoreType.DMA((7 * n,)), pltpu.SemaphoreType.DMA((7 * n,)), pltpu.SemaphoreType.DMA((n,))]


def exchange_grads(parts, *, name):
    n = len(parts)

    def body(*refs):
        start, forward, finish = _exchange(refs[:n], refs[n:2 * n], refs[2 * n:3 * n], *refs[3 * n:])
        start()
        forward()
        finish()

    shapes = [jax.ShapeDtypeStruct(p.shape, p.dtype) for p in parts]
    res = _pcall(body, name=name, in_specs=[ANY] * n, out_specs=[ANY] * (2 * n), out_shape=shapes + shapes,
                 scratch_shapes=_exchange_scratch(n))(*parts)
    return res[:n], res[n:]


def _natural(g, how):
    if how == "col":
        return jnp.moveaxis(g, 0, 1).reshape(g.shape[1], 4 * g.shape[2])
    return g.reshape(4 * g.shape[1], g.shape[2])


def _chunks(gw, how):
    k, n = gw.shape
    if how == "col":
        return jnp.moveaxis(gw.reshape(k, 4, n // 4), 1, 0).astype(bf16)
    return gw.reshape(4, k // 4, n).astype(bf16)


def kernel(x, c, ada_w, ada_b, norm_mix, norm_ffn, ffn_w_in, ffn_w_out, a_w_in, a_w_out, a_sink, b_w_in, b_w_out, final_norm, loss_target, m_ada_w, m_ada_b, m_norm_mix, m_norm_ffn, m_ffn_w_in, m_ffn_w_out, m_a_w_in, m_a_w_out, m_a_sink, m_b_w_in, m_b_w_out, m_final_norm, v_ada_w, v_ada_b, v_norm_mix, v_norm_ffn, v_ffn_w_in, v_ffn_w_out, v_a_w_in, v_a_w_out, v_a_sink, v_b_w_in, v_b_w_out, v_final_norm):
    s = x.shape[1]
    xi, yi, ci = lax.axis_index("x"), lax.axis_index("y"), lax.axis_index("c")
    chip = 2 * xi + yi
    dev = 2 * chip + ci
    x0 = x[0]
    tgt = loss_target[0]

    big = {"ffn_w_in": (ffn_w_in, "col"), "ffn_w_out": (ffn_w_out, "row"), "a_w_in": (a_w_in, "col"),
           "a_w_out": (a_w_out, "row"), "b_w_in": (b_w_in, "col"), "b_w_out": (b_w_out, "col")}
    names = list(big)

    def layer_keys(i):
        mix = "a" if i % 2 == 0 else "b"
        return [("ffn_w_in", i), ("ffn_w_out", i), (mix + "_w_in", i // 2), (mix + "_w_out", i // 2)]

    def shards_of(i):
        return [big[k][0][l].astype(bf16) for k, l in layer_keys(i)]

    def weights_of(i, gathered):
        return {k: (g if k == "ffn_w_in" else _natural(g, big[k][1])) for (k, _), g in zip(layer_keys(i), gathered)}

    mix0 = gather_weights(shards_of(0)[2:], name="gather_weights")

    c_all = allgather8(jnp.broadcast_to(c, (8, D)), name="gather_c")[:, 0, :]
    nsh = ada_w.shape[2]
    ada_b_sh = lax.dynamic_slice_in_dim(ada_b, chip * nsh, nsh, axis=1)[:, None, :]
    mod_part = ada_mod(c_all, ada_w, ada_b_sh, name="ada_mod")
    mod_all = allgather8(mod_part.reshape(DEPTH * N_DEV, nsh), name="gather_mod")
    mod_all = mod_all.reshape(4, 2, DEPTH, N_DEV, nsh)[:, 0]
    mod = lax.dynamic_index_in_dim(mod_all, dev, axis=2, keepdims=False)
    mod = jnp.moveaxis(mod, 0, 1).reshape(DEPTH, 6, 1, D)

    cfg_a = _Attn(s, mixer="a")
    cfg_b = [_Attn(s, mixer="b", group=g) for g in range(3)]
    no_sink = jnp.full((1, 2, 1, GQ * 64), NEG, f32)

    saved = []
    xc = x0
    for i in range(DEPTH):
        j = i // 2
        sh1, sc1, g1, sh2, sc2, g2 = (mod[i, q] for q in range(6))
        nmix, nffn = norm_mix[i][None, :], norm_ffn[i][None, :]
        mix = "a" if i % 2 == 0 else "b"
        if i == 0:
            h, qkv = mm_norm(xc, nmix, sc1, sh1, _natural(mix0[0], "col"), name="a_qkv")
            sinkcol = jnp.repeat(a_sink[j].reshape(2, 2, GQ), 128, axis=2)[:, :, None, :]
            o, lse, *ffn0 = attn_fwd(qkv[None], sinkcol, cfg_a, shards_of(0)[:2], out_dtype=bf16, name="a_attn_fwd_gather")
            o, lse = o[0], lse[0]
            wl = [weights_of(0, ffn0 + list(mix0))]
        elif i % 2 == 0:
            h, qkv = mm_norm(xc, nmix, sc1, sh1, wl[i]["a_w_in"], name="a_qkv")
            sinkcol = jnp.repeat(a_sink[j].reshape(2, 2, GQ), 128, axis=2)[:, :, None, :]
            o, lse = (t[0] for t in attn_fwd(qkv[None], sinkcol, cfg_a, out_dtype=bf16, name="a_attn_fwd"))
        else:
            h, qkv = mm_norm(xc, nmix, sc1, sh1, wl[i]["b_w_in"], name="b_qkv")
            qkv = [qkv[None]] + list(b_to_strided(qkv, name="b_to_strided"))
            outs = [attn_fwd(qkv[g], no_sink, cfg_b[g], out_dtype=f32, name=f"b_attn_fwd{g}") for g in range(3)]
            o, lse = attn_merge([t[0] for t in outs], [t[1] for t in outs], name="b_merge")
        x1 = mm_resid(o, wl[i][mix + "_w_out"], xc, g1, name=mix + "_out")
        nxt = shards_of(i + 1) if i + 1 < DEPTH else []
        h2, gu, act, x2, *got = ffn_fwd(x1, nffn, sc2, sh2, g2, wl[i]["ffn_w_in"], wl[i]["ffn_w_out"], nxt,
                                        name="ffn_fwd_gather" if nxt else "ffn_fwd")
        if nxt:
            wl.append(weights_of(i + 1, got))
        saved.append((xc, h, qkv, o, lse, x1, h2, gu, act))
        xc = x2

    dx, st_final = loss_head(xc, final_norm[None, :], tgt, name="loss_head")

    zero_row = jnp.zeros((1, D), f32)
    dmod_rows = [None] * DEPTH
    d_nmix, d_nffn = [None] * DEPTH, [None] * DEPTH
    d_sink = [None] * 2
    parts, exchanged = None, {}
    for i in reversed(range(DEPTH)):
        j = i // 2
        xin, h, qkv, o, lse, x1, h2, gu, act = saved[i]
        sh1, sc1, g1, sh2, sc2, g2 = (mod[i, q] for q in range(6))
        nmix, nffn = norm_mix[i][None, :], norm_ffn[i][None, :]
        mix = "a" if i % 2 == 0 else "b"
        w_fo, w_o, w_i = wl[i]["ffn_w_out"], wl[i][mix + "_w_out"], wl[i][mix + "_w_in"]
        dgu, dx1, st2, own_, sib_ = ffn_bwd_rows(dx, x1, gu, g2, nffn, sc2, wl[i]["ffn_w_in"], w_fo, parts or [],
                                                 name="ffn_bwd_rows_exchange" if parts else "ffn_bwd_rows")
        if parts:
            exchanged[i + 1] = (own_, sib_)
        gwo, dg2 = ffn_dw_out(act, dx, g2, w_fo, name="ffn_dw_out")
        dg2 = dg2[0:1]
        gwi = ffn_dw_in(h2, dgu, name="ffn_dw_in")
        gmo, dg1 = mm_tn(o, dx1, (g1, w_o), name=mix + "_dw_out")
        ffn_parts = [gwi, gwo.reshape(4, F // 4, D)]
        if i % 2 == 0:
            sinkrow = jnp.pad(a_sink[j].reshape(2, 8), ((0, 0), (0, 120))).reshape(1, 256)
            do, delta, dsk = mm_nt_delta(dx1, g1, w_o, o, lse, sinkrow, name="a_do")
            d_sink[j] = dsk[0].reshape(2, 128)[:, :8].reshape(16)
            dq, dk, dv, *ffn_x = attn_bwd(qkv[None], do[None], lse[None], delta[None], cfg_a, ffn_parts if i == 0 else [],
                                          name="a_attn_bwd_exchange" if i == 0 else "a_attn_bwd")
            dqkv = [dq[0], dk[0], dv[0]]
        else:
            do, delta, _ = mm_nt_delta(dx1, g1, w_o, o, lse, jnp.zeros((1, 128), f32), name="b_do")
            st = [do[None], lse[None], delta[None]] + list(b_bwd_to_strided(do, lse, delta, name="b_bwd_to_strided"))
            gr = [attn_bwd(qkv[g], *st[3 * g:3 * g + 3], cfg_b[g], name=f"b_attn_bwd{g}") for g in range(3)]
            dqkv = b_from_strided(gr, name="b_from_strided")
        gmi = mm_tn(h, dqkv, name=mix + "_dw_in")
        dx, st1 = mm_nt_norm_bwd(dqkv, w_i, xin, dx1, nmix, sc1, name=mix + "_dh")
        dmod_rows[i] = jnp.concatenate([st1[2:3], st1[1:2], dg1, st2[2:3], st2[1:2], dg2], axis=0)
        d_nmix[i], d_nffn[i] = st1[0:1], st2[0:1]
        mix_parts = [_chunks(gmi, big[mix + "_w_in"][1]), _chunks(gmo, big[mix + "_w_out"][1])]
        parts = ffn_parts + mix_parts
    own_m, sib_m = exchange_grads(mix_parts, name="exchange_grads")
    exchanged[0] = (list(ffn_x[0]) + list(own_m), list(ffn_x[1]) + list(sib_m))

    sink_row = jnp.pad(jnp.concatenate(d_sink), (0, D - 32))[None, :]
    stats = jnp.concatenate(dmod_rows + d_nmix + d_nffn + [sink_row, st_final[0:1], st_final[1:2]]
                            + [zero_row] * (STAT_ROWS - 35), axis=0)
    stats_all = allgather8(stats, name="gather_stats")
    tot = sum_devices(stats_all, name="sum_stats")
    loss = 0.5 * jnp.sum(tot[34]) / float(D)

    def pack(ab, nm, nf, sk, fnm, fill):
        return jnp.concatenate([ab.reshape(24, D), nm, nf, jnp.pad(sk.reshape(1, 32), ((0, 0), (0, D - 32)), constant_values=fill),
                                fnm[None, :], jnp.full((STAT_ROWS - 34, D), fill, f32)], axis=0)

    sd, sm, sv = adamw(pack(ada_b, norm_mix, norm_ffn, a_sink, final_norm, 0.0),
                       pack(m_ada_b, m_norm_mix, m_norm_ffn, m_a_sink, m_final_norm, 0.0),
                       pack(v_ada_b, v_norm_mix, v_norm_ffn, v_a_sink, v_final_norm, 1.0), tot, name="adamw_small")

    def unpack(p):
        return p[0:24].reshape(DEPTH, 6 * D), p[24:28], p[28:32], p[32, :32].reshape(2, 16), p[33]

    small = {"grad": unpack(tot), "delta": unpack(sd), "m": unpack(sm), "v": unpack(sv)}

    dmod_all = stats_all[:, 0:24, :].reshape(N_DEV, DEPTH, 6 * D)
    dm_sh = jnp.moveaxis(lax.dynamic_slice_in_dim(dmod_all, chip * nsh, nsh, axis=2), 0, 1)
    g_ada = ada_grad(c_all.T, dm_sh, name="ada_grad")
    r_ada = (DEPTH * D, nsh)
    ada_res = adamw(ada_w.reshape(r_ada), m_ada_w.reshape(r_ada), v_ada_w.reshape(r_ada), g_ada.reshape(r_ada), name="adamw_ada")
    ada_out = [g_ada] + [t.reshape(ada_w.shape) for t in ada_res]

    mom = {"ffn_w_in": (m_ffn_w_in, v_ffn_w_in), "ffn_w_out": (m_ffn_w_out, v_ffn_w_out), "a_w_in": (m_a_w_in, v_a_w_in),
           "a_w_out": (m_a_w_out, v_a_w_out), "b_w_in": (m_b_w_in, v_b_w_in), "b_w_out": (m_b_w_out, v_b_w_out)}
    big_out = {k: None for k in names}
    for i in reversed(range(DEPTH)):
        own_, sib_ = exchanged[i]
        for (k, l), o_, s_ in zip(layer_keys(i), own_, sib_):
            w = big[k][0]
            r2 = (-1, w.shape[-1])
            big_out[k] = adamw_parts(w.reshape(r2), mom[k][0].reshape(r2), mom[k][1].reshape(r2), o_, s_, l, big_out[k],
                                     name=f"adamw_{k}{l}")
    big_out = {k: [t.reshape(big[k][0].shape) for t in big_out[k]] for k in names}

    def leaves(q):
        sm_ = small[("grad", "delta", "m", "v")[q]]
        return (ada_out[q], sm_[0], sm_[1], sm_[2], big_out["ffn_w_in"][q], big_out["ffn_w_out"][q], big_out["a_w_in"][q],
                big_out["a_w_out"][q], sm_[3], big_out["b_w_in"][q], big_out["b_w_out"][q], sm_[4])

    return (loss, dx[None], *leaves(0), *leaves(1), *leaves(2), *leaves(3))
```

```python
import functools

import numpy as np
import jax
import jax.numpy as jnp
from jax import lax
from jax.experimental import pallas as pl
from jax.experimental.pallas import tpu as pltpu

f32 = jnp.float32
bf16 = jnp.bfloat16

D = 1024
DH = 64
GQ = 4
DEPTH = 4
F = 2816
A_QKV, A_OUT = 1536, 1024
B_QKV, B_OUT = 2304, 512
B_GROUPS = ((128, 1), (512, 4), (2048, 16))
RMS_EPS = 1e-6
NEG = -1e30
LR, B1, B2, ADAM_EPS, WD, STEP = 0.001, 0.9, 0.999, 1e-08, 0.01, 10
N_DEV = 8
STAT_ROWS = 40
CARRY_LEAD = 6
MESH = pl.DeviceIdType.MESH
ANY = pl.BlockSpec(memory_space=pl.ANY)


def _pcall(body, **kw):
    return pl.pallas_call(body, **kw)


def _params(*sem):
    return pltpu.CompilerParams(dimension_semantics=sem, vmem_limit_bytes=56 * 1024 * 1024)


def _row_tile(s, want=1024):
    return want if s % want == 0 else s


ROW_CHUNKS = 4


def mm_norm(x, nw, sc, sh, w, *, name):
    s, d = x.shape
    n = w.shape[1]
    tm = _row_tile(s)
    rc = tm // ROW_CHUNKS

    def body(x_ref, nw_ref, sc_ref, sh_ref, w_ref, h_ref, y_ref):
        hs = []
        for c in range(ROW_CHUNKS):
            xv = x_ref[c * rc:(c + 1) * rc, :]
            r = lax.rsqrt(jnp.mean(xv * xv, axis=-1, keepdims=True) + RMS_EPS)
            hs.append(((xv * r * nw_ref[...]) * (1.0 + sc_ref[...]) + sh_ref[...]).astype(bf16))
        ys = [jnp.dot(h, w_ref[...], preferred_element_type=f32) for h in hs]
        for c in range(ROW_CHUNKS):
            h_ref[c * rc:(c + 1) * rc, :] = hs[c]
            y_ref[c * rc:(c + 1) * rc, :] = ys[c].astype(bf16)

    vec = pl.BlockSpec((1, d), lambda i: (0, 0))
    return _pcall(
        body, name=name, grid=(s // tm,),
        in_specs=[pl.BlockSpec((tm, d), lambda i: (i, 0)), vec, vec, vec, pl.BlockSpec((d, n), lambda i: (0, 0))],
        out_specs=[pl.BlockSpec((tm, d), lambda i: (i, 0)), pl.BlockSpec((tm, n), lambda i: (i, 0))],
        out_shape=[jax.ShapeDtypeStruct((s, d), bf16), jax.ShapeDtypeStruct((s, n), bf16)],
        compiler_params=_params("parallel"),
    )(x, nw, sc, sh, w)


def mm_resid(a, w, xres, g, *, name):
    s, k = a.shape
    n = w.shape[1]
    tm = _row_tile(s)
    rc = tm // ROW_CHUNKS

    def body(a_ref, w_ref, x_ref, g_ref, o_ref):
        ys = [jnp.dot(a_ref[c * rc:(c + 1) * rc, :], w_ref[...], preferred_element_type=f32) for c in range(ROW_CHUNKS)]
        for c, y in enumerate(ys):
            o_ref[c * rc:(c + 1) * rc, :] = x_ref[c * rc:(c + 1) * rc, :] + g_ref[...] * y

    big = pl.BlockSpec((tm, n), lambda i: (i, 0))
    return _pcall(
        body, name=name, grid=(s // tm,),
        in_specs=[pl.BlockSpec((tm, k), lambda i: (i, 0)), pl.BlockSpec((k, n), lambda i: (0, 0)), big,
                  pl.BlockSpec((1, n), lambda i: (0, 0))],
        out_specs=big, out_shape=jax.ShapeDtypeStruct((s, n), f32), compiler_params=_params("parallel"),
    )(a, w, xres, g)


def mm_nt_delta(dx, g, w, o, lse, sinkrow, *, name):
    s, d = dx.shape
    n = w.shape[0]
    wd = lse.shape[1]
    tm = _row_tile(s)
    rc = tm // ROW_CHUNKS
    ind = jnp.asarray(_head_indicator(n // DH), dtype=bf16)

    def body(dx_ref, g_ref, w_ref, o_ref, lse_ref, sink_ref, e_ref, do_ref, dl_ref, ds_ref):
        rows = [slice(c * rc, (c + 1) * rc) for c in range(ROW_CHUNKS)]
        as_ = [(dx_ref[rw, :] * g_ref[...]).astype(bf16) for rw in rows]
        dos = [lax.dot_general(a, w_ref[...], (((1,), (1,)), ((), ())), preferred_element_type=f32).astype(bf16) for a in as_]
        dls = [_dot_split(do.astype(f32) * o_ref[rw, :].astype(f32), e_ref[...]) for do, rw in zip(dos, rows)]
        part = None
        for rw, do, dl in zip(rows, dos, dls):
            do_ref[rw, :] = do
            dl_ref[rw, :] = dl
            p = -jnp.sum(jnp.exp(sink_ref[...] - lse_ref[rw, :]) * dl, axis=0, keepdims=True)
            part = p if part is None else part + p
        part = jnp.concatenate([part, jnp.zeros((7, wd), f32)], axis=0)

        @pl.when(pl.program_id(0) == 0)
        def _():
            ds_ref[...] = part

        @pl.when(pl.program_id(0) != 0)
        def _():
            ds_ref[...] += part

    return _pcall(
        body, name=name, grid=(s // tm,),
        in_specs=[pl.BlockSpec((tm, d), lambda i: (i, 0)), pl.BlockSpec((1, d), lambda i: (0, 0)),
                  pl.BlockSpec((n, d), lambda i: (0, 0)), pl.BlockSpec((tm, n), lambda i: (i, 0)),
                  pl.BlockSpec((tm, wd), lambda i: (i, 0)), pl.BlockSpec((1, wd), lambda i: (0, 0)),
                  pl.BlockSpec((n, wd), lambda i: (0, 0))],
        out_specs=[pl.BlockSpec((tm, n), lambda i: (i, 0)), pl.BlockSpec((tm, wd), lambda i: (i, 0)),
                   pl.BlockSpec((8, wd), lambda i: (0, 0))],
        out_shape=[jax.ShapeDtypeStruct((s, n), bf16), jax.ShapeDtypeStruct((s, wd), f32), jax.ShapeDtypeStruct((8, wd), f32)],
        compiler_params=_params("arbitrary"),
    )(dx, g, w, o, lse, sinkrow, ind)


def mm_nt_norm_bwd(a, w, x, dres, nw, sc, *, name):
    pieces = list(a) if isinstance(a, (list, tuple)) else [a]
    npc = len(pieces)
    s = pieces[0].shape[0]
    k = sum(p.shape[1] for p in pieces)
    d = w.shape[0]
    tm = _row_tile(s)
    rc = tm // ROW_CHUNKS

    def body(*refs):
        a_refs = refs[:npc]
        w_ref, x_ref, dr_ref, nw_ref, sc_ref, o_ref, st_ref = refs[npc:]

        def a_rows(c):
            got = [r[c * rc:(c + 1) * rc, :] for r in a_refs]
            return jnp.concatenate(got, axis=1) if npc > 1 else got[0]

        dhs = [lax.dot_general(a_rows(c), w_ref[...], (((1,), (1,)), ((), ())), preferred_element_type=f32)
               for c in range(ROW_CHUNKS)]
        rows = None
        for c, dh in enumerate(dhs):
            xv = x_ref[c * rc:(c + 1) * rc, :]
            r = lax.rsqrt(jnp.mean(xv * xv, axis=-1, keepdims=True) + RMS_EPS)
            xh = xv * r
            dn = dh * (1.0 + sc_ref[...])
            dxh = dn * nw_ref[...]
            o_ref[c * rc:(c + 1) * rc, :] = dr_ref[c * rc:(c + 1) * rc, :] + r * (dxh - xh * jnp.mean(dxh * xh, axis=-1, keepdims=True))
            part = jnp.concatenate([
                jnp.sum(dn * xh, axis=0, keepdims=True),
                jnp.sum(dh * (xh * nw_ref[...]), axis=0, keepdims=True),
                jnp.sum(dh, axis=0, keepdims=True),
                jnp.zeros((5, d), f32)], axis=0)
            rows = part if rows is None else rows + part

        @pl.when(pl.program_id(0) == 0)
        def _():
            st_ref[...] = rows

        @pl.when(pl.program_id(0) != 0)
        def _():
            st_ref[...] += rows

    big = pl.BlockSpec((tm, d), lambda i: (i, 0))
    vec = pl.BlockSpec((1, d), lambda i: (0, 0))
    return _pcall(
        body, name=name, grid=(s // tm,),
        in_specs=[pl.BlockSpec((tm, p.shape[1]), lambda i: (i, 0)) for p in pieces]
        + [pl.BlockSpec((d, k), lambda i: (0, 0)), big, big, vec, vec],
        out_specs=[big, pl.BlockSpec((8, d), lambda i: (0, 0))],
        out_shape=[jax.ShapeDtypeStruct((s, d), f32), jax.ShapeDtypeStruct((8, d), f32)],
        compiler_params=_params("arbitrary"),
    )(*pieces, w, x, dres, nw, sc)


def mm_tn(a, b, scale=None, *, name):
    pieces = list(b) if isinstance(b, (list, tuple)) else [b]
    s, ka = a.shape
    nb = sum(p.shape[1] for p in pieces)
    npc = len(pieces)
    ts = _row_tile(s)
    ns = s // ts

    def body(a_ref, *rest):
        b_refs, rest = rest[:npc], rest[npc:]
        o_ref = rest[2] if scale is not None else rest[0]
        si = pl.program_id(0)
        bv = jnp.concatenate([r[...].astype(bf16) for r in b_refs], axis=1) if npc > 1 else b_refs[0][...].astype(bf16)
        part = lax.dot_general(a_ref[...], bv, (((0,), (0,)), ((), ())), preferred_element_type=f32)

        @pl.when(si == 0)
        def _():
            o_ref[...] = part

        @pl.when(si != 0)
        def _():
            o_ref[...] += part

        if scale is not None:
            g_ref, wb_ref, dg_ref = rest[0], rest[1], rest[3]

            @pl.when(si == ns - 1)
            def _():
                gm = o_ref[...]
                dg_ref[...] = jnp.sum(wb_ref[...].astype(f32) * gm, axis=0, keepdims=True)
                o_ref[...] = gm * g_ref[...]

    in_specs = [pl.BlockSpec((ts, ka), lambda k: (k, 0))] + [pl.BlockSpec((ts, p.shape[1]), lambda k: (k, 0)) for p in pieces]
    args = [a] + pieces
    whole = pl.BlockSpec((ka, nb), lambda k: (0, 0))
    out_specs = [whole]
    out_shape = [jax.ShapeDtypeStruct((ka, nb), f32)]
    if scale is not None:
        in_specs += [pl.BlockSpec((1, nb), lambda k: (0, 0)), whole]
        args += list(scale)
        out_specs.append(pl.BlockSpec((1, nb), lambda k: (0, 0)))
        out_shape.append(jax.ShapeDtypeStruct((1, nb), f32))
    res = _pcall(body, name=name, grid=(ns,), in_specs=in_specs, out_specs=out_specs, out_shape=out_shape,
                 compiler_params=_params("arbitrary"))(*args)
    return res if scale is not None else res[0]


def loss_head(x, fn, tgt, *, name):
    s, d = x.shape
    tm = _row_tile(s, 512)

    def body(x_ref, fn_ref, t_ref, dx_ref, st_ref):
        xv = x_ref[...]
        r = lax.rsqrt(jnp.mean(xv * xv, axis=-1, keepdims=True) + RMS_EPS)
        xh = xv * r
        err = xh * fn_ref[...] - t_ref[...]
        dy = err / float(d)
        dxh = dy * fn_ref[...]
        dx_ref[...] = r * (dxh - xh * jnp.mean(dxh * xh, axis=-1, keepdims=True))
        rows = jnp.concatenate([
            jnp.sum(dy * xh, axis=0, keepdims=True),
            jnp.sum(err * err, axis=0, keepdims=True),
            jnp.zeros((6, d), f32)], axis=0)

        @pl.when(pl.program_id(0) == 0)
        def _():
            st_ref[...] = rows

        @pl.when(pl.program_id(0) != 0)
        def _():
            st_ref[...] += rows

    big = pl.BlockSpec((tm, d), lambda i: (i, 0))
    return _pcall(
        body, name=name, grid=(s // tm,), in_specs=[big, pl.BlockSpec((1, d), lambda i: (0, 0)), big],
        out_specs=[big, pl.BlockSpec((8, d), lambda i: (0, 0))],
        out_shape=[jax.ShapeDtypeStruct((s, d), f32), jax.ShapeDtypeStruct((8, d), f32)],
        compiler_params=_params("arbitrary"),
    )(x, fn, tgt)


FC = 2 * F // 4
FFN_ROWS = 256


def _resident(pairs, sems):
    @pl.when(pl.program_id(0) == 0)
    def _():
        cps = [pltpu.make_async_copy(h, v, sems.at[i]) for i, (h, v) in enumerate(pairs)]
        for cp in cps:
            cp.start()
        for cp in cps:
            cp.wait()


def ffn_fwd(x, nw, sc, sh, g, w_in, w_out, carry=(), *, name):
    s, d = x.shape
    tm = _row_tile(s, FFN_ROWS)
    nsteps = s // tm
    nc = len(carry)

    def body(*refs):
        x_ref, nw_ref, sc_ref, sh_ref, g_ref, win_hbm, wout_hbm = refs[:7]
        h_ref, gu_ref, a_ref, o_ref = refs[7 + nc:11 + nc]
        win_v, wout_v, sems = refs[11 + 2 * nc:14 + 2 * nc]
        if nc:
            start, finish = _gather_direct(refs[7:7 + nc], refs[11 + nc:11 + 2 * nc], *refs[14 + 2 * nc:])
            pl.when(pl.program_id(0) == 0)(start)
        _resident([(win_hbm, win_v), (wout_hbm, wout_v)], sems)
        xv = x_ref[...]
        r = lax.rsqrt(jnp.mean(xv * xv, axis=-1, keepdims=True) + RMS_EPS)
        h = ((xv * r * nw_ref[...]) * (1.0 + sc_ref[...]) + sh_ref[...]).astype(bf16)
        h_ref[...] = h
        halves = [slice(c * FC, (c + 1) * FC) for c in range(2)]
        gts = [jnp.dot(h, win_v[c], preferred_element_type=f32) for c in range(2)]
        ups = [jnp.dot(h, win_v[c + 2], preferred_element_type=f32) for c in range(2)]
        acts = [(gt * jax.nn.sigmoid(gt) * up).astype(bf16) for gt, up in zip(gts, ups)]
        ys = [jnp.dot(act, wout_v[cs, :], preferred_element_type=f32) for act, cs in zip(acts, halves)]
        for cs, gt, up, act in zip(halves, gts, ups, acts):
            gu_ref[0, :, cs] = gt.astype(bf16)
            gu_ref[1, :, cs] = up.astype(bf16)
            a_ref[:, cs] = act
        o_ref[...] = xv + g_ref[...] * (ys[0] + ys[1])
        if nc:
            pl.when(pl.program_id(0) == nsteps - 1)(finish)

    big = pl.BlockSpec((tm, d), lambda i: (i, 0))
    vec = pl.BlockSpec((1, d), lambda i: (0, 0))
    return _pcall(
        body, name=name, grid=(nsteps,), in_specs=[big, vec, vec, vec, vec, ANY, ANY] + [ANY] * nc,
        out_specs=[big, pl.BlockSpec((2, tm, F), lambda i: (0, i, 0)), pl.BlockSpec((tm, F), lambda i: (i, 0)), big] + [ANY] * nc,
        out_shape=[jax.ShapeDtypeStruct((s, d), bf16), jax.ShapeDtypeStruct((2, s, F), bf16),
                   jax.ShapeDtypeStruct((s, F), bf16), jax.ShapeDtypeStruct((s, d), f32)]
        + [jax.ShapeDtypeStruct((4,) + tuple(sh_.shape), sh_.dtype) for sh_ in carry],
        scratch_shapes=[pltpu.VMEM((4, d, FC), bf16), pltpu.VMEM((F, d), bf16), pltpu.SemaphoreType.DMA((2,))]
        + (_gather_scratch(nc) if nc else []),
        compiler_params=_params("arbitrary"),
    )(x, nw, sc, sh, g, w_in, w_out, *carry)


def ffn_bwd_rows(dx, x, gu, g, nw, sc, w_in, w_out, carry=(), *, name):
    s, d = x.shape
    tm = _row_tile(s, FFN_ROWS)
    nsteps = s // tm
    nc = len(carry)
    nt_dims = (((1,), (1,)), ((), ()))

    def body(*refs):
        dx_ref, x_ref, gu_ref, g_ref, nw_ref, sc_ref, win_hbm, wout_hbm = refs[:8]
        dgu_ref, o_ref, st_ref = refs[8 + nc:11 + nc]
        win_v, wout_v, sems = refs[11 + 3 * nc:14 + 3 * nc]
        if nc:
            start, forward, finish = _exchange(refs[8:8 + nc], refs[11 + nc:11 + 2 * nc], refs[11 + 2 * nc:11 + 3 * nc],
                                               *refs[14 + 3 * nc:])
            pl.when(pl.program_id(0) == 0)(start)
            pl.when(pl.program_id(0) == max(nsteps - 1 - CARRY_LEAD, 0))(forward)
        _resident([(win_hbm, win_v), (wout_hbm, wout_v)], sems)
        dxv = dx_ref[...]
        a = (dxv * g_ref[...]).astype(bf16)
        halves = [slice(c * FC, (c + 1) * FC) for c in range(2)]
        das = [lax.dot_general(a, wout_v[cs, :], nt_dims, preferred_element_type=f32) for cs in halves]
        gts = [gu_ref[0, :, cs].astype(f32) for cs in halves]
        ups = [gu_ref[1, :, cs].astype(f32) for cs in halves]
        sgs = [jax.nn.sigmoid(gt) for gt in gts]
        dgates = [(da * up * (sg * (1.0 + gt * (1.0 - sg)))).astype(bf16) for da, gt, up, sg in zip(das, gts, ups, sgs)]
        dups = [(da * (gt * sg)).astype(bf16) for da, gt, sg in zip(das, gts, sgs)]
        for cs, dgate, dup in zip(halves, dgates, dups):
            dgu_ref[0, :, cs] = dgate
            dgu_ref[1, :, cs] = dup
        parts = [lax.dot_general(dgates[c], win_v[c], nt_dims, preferred_element_type=f32) for c in range(2)]
        parts += [lax.dot_general(dups[c], win_v[c + 2], nt_dims, preferred_element_type=f32) for c in range(2)]
        dh = (parts[0] + parts[1]) + (parts[2] + parts[3])
        xv = x_ref[...]
        r = lax.rsqrt(jnp.mean(xv * xv, axis=-1, keepdims=True) + RMS_EPS)
        xh = xv * r
        dn = dh * (1.0 + sc_ref[...])
        dxh = dn * nw_ref[...]
        o_ref[...] = dxv + r * (dxh - xh * jnp.mean(dxh * xh, axis=-1, keepdims=True))
        rows = jnp.concatenate([
            jnp.sum(dn * xh, axis=0, keepdims=True),
            jnp.sum(dh * (xh * nw_ref[...]), axis=0, keepdims=True),
            jnp.sum(dh, axis=0, keepdims=True),
            jnp.zeros((5, d), f32)], axis=0)

        @pl.when(pl.program_id(0) == 0)
        def _():
            st_ref[...] = rows

        @pl.when(pl.program_id(0) != 0)
        def _():
            st_ref[...] += rows

        if nc:
            pl.when(pl.program_id(0) == nsteps - 1)(finish)

    big = pl.BlockSpec((tm, d), lambda i: (i, 0))
    vec = pl.BlockSpec((1, d), lambda i: (0, 0))
    gus = pl.BlockSpec((2, tm, F), lambda i: (0, i, 0))
    cshapes = [jax.ShapeDtypeStruct(p.shape, p.dtype) for p in carry]
    res = _pcall(
        body, name=name, grid=(nsteps,), in_specs=[big, big, gus, vec, vec, vec, ANY, ANY] + [ANY] * nc,
        out_specs=[gus, big, pl.BlockSpec((8, d), lambda i: (0, 0))] + [ANY] * (2 * nc),
        out_shape=[jax.ShapeDtypeStruct((2, s, F), bf16), jax.ShapeDtypeStruct((s, d), f32), jax.ShapeDtypeStruct((8, d), f32)]
        + cshapes + cshapes,
        scratch_shapes=[pltpu.VMEM((4, d, FC), bf16), pltpu.VMEM((F, d), bf16), pltpu.SemaphoreType.DMA((2,))]
        + (_exchange_scratch(nc) if nc else []),
        compiler_params=_params("arbitrary"),
    )(dx, x, gu, g, nw, sc, w_in, w_out, *carry)
    return res[0], res[1], res[2], res[3:3 + nc], res[3 + nc:]


def ffn_dw_in(h, dgu, *, name):
    s, d = h.shape
    ts = _row_tile(s)
    ns = s // ts
    tn_dims = (((0,), (0,)), ((), ()))

    def body(h_ref, dgu_ref, o_ref, acc):
        k = pl.program_id(1)

        @pl.when(k == 0)
        def _():
            acc[...] = jnp.zeros_like(acc)

        hv = h_ref[...]
        for c in range(2):
            acc[c] += lax.dot_general(hv, dgu_ref[:, c * FC:(c + 1) * FC], tn_dims, preferred_element_type=f32)

        @pl.when(k == ns - 1)
        def _():
            o_ref[...] = acc[...].astype(bf16)

    return _pcall(
        body, name=name, grid=(2, ns),
        in_specs=[pl.BlockSpec((ts, d), lambda hf, k: (k, 0)), pl.BlockSpec((None, ts, F), lambda hf, k: (hf, k, 0))],
        out_specs=pl.BlockSpec((2, d, FC), lambda hf, k: (hf, 0, 0)),
        out_shape=jax.ShapeDtypeStruct((4, d, FC), bf16), scratch_shapes=[pltpu.VMEM((2, d, FC), f32)],
        compiler_params=_params("arbitrary", "arbitrary"),
    )(h, dgu)


def ffn_dw_out(a, dx, g, wb, *, name):
    s, fdim = a.shape
    d = dx.shape[1]
    ts = _row_tile(s)
    ns = s // ts
    tn = d // 2
    tn_dims = (((0,), (0,)), ((), ()))

    def body(a_ref, dx_ref, g_ref, wb_ref, o_ref, dg_ref, acc):
        k = pl.program_id(1)

        @pl.when(k == 0)
        def _():
            acc[...] = jnp.zeros_like(acc)

        acc[...] += lax.dot_general(a_ref[...], dx_ref[...].astype(bf16), tn_dims, preferred_element_type=f32)

        @pl.when(k == ns - 1)
        def _():
            gm = acc[...]
            dg_ref[...] = jnp.concatenate([jnp.sum(wb_ref[...].astype(f32) * gm, axis=0, keepdims=True),
                                           jnp.zeros((7, tn), f32)], axis=0)
            o_ref[...] = (gm * g_ref[...]).astype(bf16)

    return _pcall(
        body, name=name, grid=(2, ns),
        in_specs=[pl.BlockSpec((ts, fdim), lambda j, k: (k, 0)), pl.BlockSpec((ts, tn), lambda j, k: (k, j)),
                  pl.BlockSpec((1, tn), lambda j, k: (0, j)), pl.BlockSpec((fdim, tn), lambda j, k: (0, j))],
        out_specs=[pl.BlockSpec((fdim, tn), lambda j, k: (0, j)), pl.BlockSpec((8, tn), lambda j, k: (0, j))],
        out_shape=[jax.ShapeDtypeStruct((fdim, d), bf16), jax.ShapeDtypeStruct((8, d), f32)],
        scratch_shapes=[pltpu.VMEM((fdim, tn), f32)], compiler_params=_params("arbitrary", "arbitrary"),
    )(a, dx, g, wb)


def _alibi(n):
    return np.asarray(2.0 ** (-8.0 * np.arange(1, n + 1) / n), dtype=np.float32)


class _Attn:
    def __init__(self, s, *, mixer, group=0):
        if mixer == "a":
            self.blk, self.dil, self.npairs = 128, 1, 2
            self.qb0, self.kb0, self.vb0 = 0, 8, 10
            slopes = _alibi(16).reshape(2, 2, GQ)
        else:
            window, dil = B_GROUPS[group]
            self.blk, self.dil, self.npairs = window // (2 * dil), dil, 1
            self.qb0, self.kb0, self.vb0 = (0, 12, 15) if dil == 1 else (0, 4, 5)
            slopes = _alibi(24).reshape(3, 1, 2, GQ)[group]
        self.l = s // self.dil
        self.t = min(1024, self.l)
        self.nt = self.l // self.t
        self.nb = self.t // self.blk
        blk = self.blk
        qi = np.arange(blk)[:, None]
        rel = np.arange(3 * blk)[None, :] - blk - qi
        dist = (self.dil * np.abs(rel)).astype(np.float32)
        bias = -slopes[:, :, :, None, None] * dist[None, None, None]
        bias = np.where(np.abs(rel) <= blk, bias, np.float32(NEG)).astype(np.float32)
        self.bias = np.ascontiguousarray(np.swapaxes(bias.reshape(self.npairs, 2, GQ * blk, 3 * blk), -1, -2))

    def grid(self):
        return (self.dil, self.npairs, self.nt)

    def tile(self, width, col):
        return pl.BlockSpec((None, self.t, width), lambda r, hp, i: (r, i, col(hp)))

    def halo(self, width, col):
        t, blk, nbl = self.t, self.blk, self.l // self.blk
        per = t // blk
        return [
            pl.BlockSpec((None, blk, width), lambda r, hp, i: (r, jnp.maximum(i * per - 1, 0), col(hp))),
            self.tile(width, col),
            pl.BlockSpec((None, blk, width), lambda r, hp, i: (r, jnp.minimum((i + 1) * per, nbl - 1), col(hp))),
        ]

    def qcol(self, e):
        return lambda hp: self.qb0 + 2 * hp + e

    def kcol(self, hp):
        return self.kb0 + hp

    def vcol(self, hp):
        return self.vb0 + hp

    def pcol(self, hp):
        return hp


def _stack_heads(x):
    return jnp.concatenate([x[:, g * DH:(g + 1) * DH] for g in range(GQ)], axis=0)


def _unstack_heads(x, rows):
    return jnp.concatenate([x[g * rows:(g + 1) * rows] for g in range(GQ)], axis=1)


def _carrying(body, n_in, n_out, n_scratch, carry, kind, grid):
    nc = len(carry)
    if not nc:
        return body, [], [], [], []
    n_res = nc if kind == "gather" else 2 * nc

    def wrapped(*refs):
        ins, src = refs[:n_in], refs[n_in:n_in + nc]
        outs = refs[n_in + nc:n_in + nc + n_out]
        res = refs[n_in + nc + n_out:n_in + nc + n_out + n_res]
        scr = refs[n_in + nc + n_out + n_res:n_in + nc + n_out + n_res + n_scratch]
        sems = refs[n_in + nc + n_out + n_res + n_scratch:]
        ids = [pl.program_id(a) for a in range(len(grid))]
        first = functools.reduce(jnp.logical_and, [i == 0 for i in ids])
        last = functools.reduce(jnp.logical_and, [i == g - 1 for i, g in zip(ids, grid)])
        if kind == "gather":
            start, finish = _gather_direct(src, res, *sems)
            pl.when(first)(start)
        else:
            start, forward, finish = _exchange(src, res[:nc], res[nc:], *sems)
            pl.when(first)(start)
            early = [g - 1 for g in grid[:-1]] + [max(grid[-1] - 1 - CARRY_LEAD, 0)]
            pl.when(functools.reduce(jnp.logical_and, [i == g for i, g in zip(ids, early)]))(forward)
        body(*ins, *outs, *scr)
        pl.when(last)(finish)

    if kind == "gather":
        shapes = [jax.ShapeDtypeStruct((4,) + tuple(c.shape), c.dtype) for c in carry]
        sems = _gather_scratch(nc)
    else:
        shapes = [jax.ShapeDtypeStruct(c.shape, c.dtype) for c in carry] * 2
        sems = _exchange_scratch(nc)
    return wrapped, [ANY] * nc, [ANY] * n_res, shapes, sems


def attn_fwd(qkv, sinkcol, cfg, carry=(), *, out_dtype, name):
    blk, t, nb, nt, dil = cfg.blk, cfg.t, cfg.nb, cfg.nt, cfg.dil
    scale = DH ** -0.5

    def body(q0, q1, kp, km, kn, vp, vm, vn, bias_ref, sink_ref, o_ref, lse_ref, kx, vx):
        ti = pl.program_id(2)
        first, last = ti == 0, ti == nt - 1
        for hh in range(2):
            sl = slice(hh * DH, (hh + 1) * DH)
            for dst, (p_, m_, n_) in ((kx, (kp, km, kn)), (vx, (vp, vm, vn))):
                dst[hh, 0:blk] = p_[:, sl]
                dst[hh, blk:blk + t] = m_[:, sl]
                dst[hh, blk + t:] = n_[:, sl]
        krow = lax.broadcasted_iota(jnp.int32, (3 * blk, GQ * blk), 0)
        pairs = [(b, hh) for b in range(nb) for hh in range(2)]
        qs = [_stack_heads((q0, q1)[hh][b * blk:(b + 1) * blk, :]) * scale for b, hh in pairs]
        sc = [lax.dot_general(kx[hh, b * blk:(b + 3) * blk, :], q_, (((1,), (1,)), ((), ())), preferred_element_type=f32)
              for q_, (b, hh) in zip(qs, pairs)]
        sc = [s_ + bias_ref[0, hh] for s_, (b, hh) in zip(sc, pairs)]
        sc = [jnp.where(jnp.logical_and(first, krow < blk), NEG, s_) if b == 0 else s_ for s_, (b, hh) in zip(sc, pairs)]
        sc = [jnp.where(jnp.logical_and(last, krow >= 2 * blk), NEG, s_) if b == nb - 1 else s_ for s_, (b, hh) in zip(sc, pairs)]
        ms = [jnp.maximum(jnp.max(s_, axis=0, keepdims=True), sink_ref[0, hh]) for s_, (b, hh) in zip(sc, pairs)]
        ps = [jnp.exp(s_ - m_) for s_, m_ in zip(sc, ms)]
        ls = [jnp.sum(p_, axis=0, keepdims=True) + jnp.exp(sink_ref[0, hh] - m_) for p_, m_, (b, hh) in zip(ps, ms, pairs)]
        os_ = [lax.dot_general(vx[hh, b * blk:(b + 3) * blk, :], p_.astype(bf16), (((0,), (0,)), ((), ())),
                               preferred_element_type=f32) for p_, (b, hh) in zip(ps, pairs)]
        os_ = [o_ / l_ for o_, l_ in zip(os_, ls)]
        lses = [m_ + jnp.log(l_) for m_, l_ in zip(ms, ls)]
        for o_, (b, hh) in zip(os_, pairs):
            o_ref[b * blk:(b + 1) * blk, hh * 256:(hh + 1) * 256] = _unstack_heads(o_.T, blk).astype(out_dtype)
        stat_rows = [jnp.concatenate([lses[2 * b + hh][:, g * blk:(g + 1) * blk] for b in range(nb)], axis=1)
                     for hh in range(2) for g in range(GQ)]
        lse_ref[...] = jnp.concatenate(stat_rows + [jnp.zeros((128 - 2 * GQ, t), f32)], axis=0).T

    in_specs = [cfg.tile(256, cfg.qcol(e)) for e in range(2)]
    in_specs += cfg.halo(128, cfg.kcol) + cfg.halo(128, cfg.vcol)
    in_specs += [pl.BlockSpec((1, 2, 3 * blk, GQ * blk), lambda r, hp, i: (hp, 0, 0, 0)),
                 pl.BlockSpec((1, 2, 1, GQ * blk), lambda r, hp, i: (hp, 0, 0, 0))]
    body, c_in, c_out, c_shape, c_sems = _carrying(body, 10, 2, 2, carry, "gather", cfg.grid())
    return _pcall(
        body, name=name, grid=cfg.grid(), in_specs=in_specs + c_in,
        out_specs=[cfg.tile(512, cfg.pcol), cfg.tile(128, cfg.pcol)] + c_out,
        out_shape=[jax.ShapeDtypeStruct((dil, cfg.l, cfg.npairs * 512), out_dtype),
                   jax.ShapeDtypeStruct((dil, cfg.l, cfg.npairs * 128), f32)] + c_shape,
        scratch_shapes=[pltpu.VMEM((2, t + 2 * blk, DH), bf16), pltpu.VMEM((2, t + 2 * blk, DH), bf16)] + c_sems,
        compiler_params=_params("arbitrary", "arbitrary", "arbitrary"),
    )(*([qkv] * 8), jnp.asarray(cfg.bias), sinkcol, *carry)


def attn_bwd(qkv, do, lse, delta, cfg, carry=(), *, name):
    blk, t, nb, nt, dil, npairs = cfg.blk, cfg.t, cfg.nb, cfg.nt, cfg.dil, cfg.npairs
    scale = DH ** -0.5
    nt_dims = (((1,), (1,)), ((), ()))
    tn_dims = (((0,), (0,)), ((), ()))

    def body(q0p, q0m, q0n, q1p, q1m, q1n, kp, km, kn, vp, vm, vn, dop, dom, don, lp, lm, ln, dp_, dm_, dn_,
             bias_ref, dq_ref, dk_ref, dv_ref, kx, vx, dkx, dvx):
        ti = pl.program_id(2)
        first, last = ti == 0, ti == nt - 1
        for hh in range(2):
            sl = slice(hh * DH, (hh + 1) * DH)
            for dst, (p_, m_, n_) in ((kx, (kp, km, kn)), (vx, (vp, vm, vn))):
                dst[hh, 0:blk] = p_[:, sl]
                dst[hh, blk:blk + t] = m_[:, sl]
                dst[hh, blk + t:] = n_[:, sl]
        dkx[...] = jnp.zeros_like(dkx)
        dvx[...] = jnp.zeros_like(dvx)

        def slab(prev, main, nxt, e):
            if e == 0:
                return prev[...]
            if e == nb + 1:
                return nxt[...]
            return main[(e - 1) * blk:e * blk, :]

        krow = lax.broadcasted_iota(jnp.int32, (3 * blk, GQ * blk), 0)

        def keys(e):
            if e == 0:
                return 1, 2, slice(2 * blk, 3 * blk)
            if e == nb + 1:
                return nb, nb + 1, slice(0, blk)
            return e - 1, e + 2, slice(0, 3 * blk)

        def edge(sc, e):
            if e == 0:
                return jnp.where(first, NEG, sc)
            if e == nb + 1:
                return jnp.where(last, NEG, sc)
            if e == 1:
                sc = jnp.where(jnp.logical_and(first, krow < blk), NEG, sc)
            if e == nb:
                sc = jnp.where(jnp.logical_and(last, krow >= 2 * blk), NEG, sc)
            return sc

        lse_t = [lp[...].T, lm[...].T, ln[...].T]
        dl_t = [dp_[...].T, dm_[...].T, dn_[...].T]

        def stat_row(parts, e, hh):
            src, lo = (parts[0], 0) if e == 0 else (parts[2], 0) if e == nb + 1 else (parts[1], (e - 1) * blk)
            return jnp.concatenate([src[hh * GQ + g:hh * GQ + g + 1, lo:lo + blk] for g in range(GQ)], axis=1)

        pairs = [(e, hh) for e in range(nb + 2) for hh in range(2)]
        qs = [_stack_heads(slab(*((q0p, q0m, q0n), (q1p, q1m, q1n))[hh], e)) * scale for e, hh in pairs]
        dos = [_stack_heads(slab(dop, dom, don, e)[:, hh * 256:(hh + 1) * 256]) for e, hh in pairs]
        lse_r = [stat_row(lse_t, e, hh) for e, hh in pairs]
        dl_r = [stat_row(dl_t, e, hh) for e, hh in pairs]
        kw = [kx[hh, keys(e)[0] * blk:keys(e)[1] * blk, :] for e, hh in pairs]
        vw = [vx[hh, keys(e)[0] * blk:keys(e)[1] * blk, :] for e, hh in pairs]
        sc = [lax.dot_general(k_, q_, nt_dims, preferred_element_type=f32) for q_, k_ in zip(qs, kw)]
        dp = [lax.dot_general(v_, d_, nt_dims, preferred_element_type=f32) for d_, v_ in zip(dos, vw)]
        sc = [edge(s_ + bias_ref[0, hh, keys(e)[2], :], e) for s_, (e, hh) in zip(sc, pairs)]
        ps = [jnp.exp(s_ - l_) for s_, l_ in zip(sc, lse_r)]
        ds = [(p_ * (d_ - c_)).astype(bf16) for p_, d_, c_ in zip(ps, dp, dl_r)]
        pb = [p_.astype(bf16) for p_ in ps]
        dks = [jnp.dot(s_, q_, preferred_element_type=f32) for s_, q_ in zip(ds, qs)]
        dvs = [jnp.dot(p_, d_, preferred_element_type=f32) for p_, d_ in zip(pb, dos)]
        dqs = [lax.dot_general(s_, k_, tn_dims, preferred_element_type=f32) if 1 <= e <= nb else None
               for s_, k_, (e, hh) in zip(ds, kw, pairs)]
        for dk_, dv_, dq_, (e, hh) in zip(dks, dvs, dqs, pairs):
            k0, k1, _ = keys(e)
            dkx[hh, k0 * blk:k1 * blk, :] += dk_
            dvx[hh, k0 * blk:k1 * blk, :] += dv_
            if dq_ is not None:
                dq_ref[(e - 1) * blk:e * blk, hh * 256:(hh + 1) * 256] = (_unstack_heads(dq_, blk) * scale).astype(bf16)
        for hh in range(2):
            dk_ref[:, hh * DH:(hh + 1) * DH] = dkx[hh, blk:blk + t, :].astype(bf16)
            dv_ref[:, hh * DH:(hh + 1) * DH] = dvx[hh, blk:blk + t, :].astype(bf16)

    in_specs = cfg.halo(256, cfg.qcol(0)) + cfg.halo(256, cfg.qcol(1))
    in_specs += cfg.halo(128, cfg.kcol) + cfg.halo(128, cfg.vcol)
    in_specs += cfg.halo(512, cfg.pcol) + cfg.halo(128, cfg.pcol) + cfg.halo(128, cfg.pcol)
    in_specs += [pl.BlockSpec((1, 2, 3 * blk, GQ * blk), lambda r, hp, i: (hp, 0, 0, 0))]
    body, c_in, c_out, c_shape, c_sems = _carrying(body, 22, 3, 4, carry, "exchange", cfg.grid())
    res = _pcall(
        body, name=name, grid=cfg.grid(), in_specs=in_specs + c_in,
        out_specs=[cfg.tile(512, cfg.pcol), cfg.tile(128, cfg.pcol), cfg.tile(128, cfg.pcol)] + c_out,
        out_shape=[jax.ShapeDtypeStruct((dil, cfg.l, npairs * 512), bf16),
                   jax.ShapeDtypeStruct((dil, cfg.l, npairs * 128), bf16),
                   jax.ShapeDtypeStruct((dil, cfg.l, npairs * 128), bf16)] + c_shape,
        scratch_shapes=[pltpu.VMEM((2, t + 2 * blk, DH), bf16), pltpu.VMEM((2, t + 2 * blk, DH), bf16),
                        pltpu.VMEM((2, t + 2 * blk, DH), f32), pltpu.VMEM((2, t + 2 * blk, DH), f32)] + c_sems,
        compiler_params=_params("arbitrary", "arbitrary", "arbitrary"),
    )(*([qkv] * 12), do, do, do, lse, lse, lse, delta, delta, delta, jnp.asarray(cfg.bias), *carry)
    nc = len(carry)
    return (res[0], res[1], res[2], res[3:3 + nc], res[3 + nc:]) if nc else res


def _head_indicator(nheads):
    e = np.zeros((nheads * DH, (nheads // 8) * 128), np.float32)
    for c in range(nheads * DH):
        h = c // DH
        e[c, (h // 8) * 128 + h % 8] = 1.0
    return e


def _dot_split(x, e):
    hi = x.astype(bf16)
    lo = (x - hi.astype(f32)).astype(bf16)
    return jnp.dot(hi, e, preferred_element_type=f32) + jnp.dot(lo, e, preferred_element_type=f32)


def _spread(scr, x, d):
    tm, w = x.shape
    for j in range(w // 128):
        scr[j] = x[:, j * 128:(j + 1) * 128]
    return [jnp.concatenate([scr[j, pl.ds(r, tm // d, stride=d), :] for j in range(w // 128)], axis=1) for r in range(d)]


def _weave(scr, blocks, d):
    n, w = blocks[0].shape
    for r in range(d):
        for j in range(w // 128):
            scr[j, pl.ds(r, n, stride=d), :] = blocks[r][:, j * 128:(j + 1) * 128]
    return jnp.concatenate([scr[j] for j in range(w // 128)], axis=1)


def _res_spec(d, tm, w):
    return pl.BlockSpec((d, tm // d, w), lambda i: (0, i, 0))


DILATED = tuple(dil for _, dil in B_GROUPS[1:])


def b_to_strided(qkv, *, name):
    s = qkv.shape[0]
    tm = _row_tile(s)

    def body(x_ref, *rest):
        outs, scr = rest[:-1], rest[-1]
        for gi, (o_ref, d) in enumerate(zip(outs, DILATED), start=1):
            cols = jnp.concatenate([x_ref[:, gi * 512:(gi + 1) * 512], x_ref[:, 1536 + gi * 128:1536 + (gi + 1) * 128],
                                    x_ref[:, 1920 + gi * 128:1920 + (gi + 1) * 128]], axis=1).astype(f32)
            for r, blk_ in enumerate(_spread(scr, cols, d)):
                o_ref[r] = blk_.astype(bf16)

    return _pcall(
        body, name=name, grid=(s // tm,), in_specs=[pl.BlockSpec((tm, B_QKV), lambda i: (i, 0))],
        out_specs=[_res_spec(d, tm, 768) for d in DILATED],
        out_shape=[jax.ShapeDtypeStruct((d, s // d, 768), bf16) for d in DILATED],
        scratch_shapes=[pltpu.VMEM((6, tm, 128), f32)], compiler_params=_params("parallel"),
    )(qkv)


def b_bwd_to_strided(do, lse, delta, *, name):
    s = do.shape[0]
    tm = _row_tile(s)

    def body(do_ref, lse_ref, dl_ref, *rest):
        outs, scr = rest[:-1], rest[-1]
        allc = jnp.concatenate([do_ref[...].astype(f32), lse_ref[...], dl_ref[...]], axis=1)
        for gi, d in enumerate(DILATED):
            o_do, o_lse, o_dl = outs[3 * gi:3 * gi + 3]
            for r, blk_ in enumerate(_spread(scr, allc, d)):
                o_do[r] = blk_[:, :512].astype(bf16)
                o_lse[r] = blk_[:, 512:640]
                o_dl[r] = blk_[:, 640:768]

    out_specs, out_shape = [], []
    for d in DILATED:
        out_specs += [_res_spec(d, tm, 512), _res_spec(d, tm, 128), _res_spec(d, tm, 128)]
        out_shape += [jax.ShapeDtypeStruct((d, s // d, 512), bf16), jax.ShapeDtypeStruct((d, s // d, 128), f32),
                      jax.ShapeDtypeStruct((d, s // d, 128), f32)]
    return _pcall(
        body, name=name, grid=(s // tm,),
        in_specs=[pl.BlockSpec((tm, 512), lambda i: (i, 0)), pl.BlockSpec((tm, 128), lambda i: (i, 0)),
                  pl.BlockSpec((tm, 128), lambda i: (i, 0))],
        out_specs=out_specs, out_shape=out_shape, scratch_shapes=[pltpu.VMEM((6, tm, 128), f32)],
        compiler_params=_params("parallel"),
    )(do, lse, delta)


def b_from_strided(grads, *, name):
    s = grads[0][0].shape[1]
    tm = _row_tile(s)

    def body(*refs):
        ins, o_ref, scr = refs[:9], refs[9], refs[10]
        nat = [jnp.concatenate([ins[q][0].astype(f32) for q in range(3)], axis=1)]
        for gi, d in enumerate(DILATED, start=1):
            blocks = [jnp.concatenate([ins[3 * gi + q][r].astype(f32) for q in range(3)], axis=1) for r in range(d)]
            nat.append(_weave(scr, blocks, d))
        pieces = [nat[g][:, lo:hi] for lo, hi in ((0, 512), (512, 640), (640, 768)) for g in range(3)]
        o_ref[...] = jnp.concatenate(pieces, axis=1).astype(bf16)

    dils = (1,) + DILATED
    in_specs = [_res_spec(d, tm, w) for d in dils for w in (512, 128, 128)]
    return _pcall(
        body, name=name, grid=(s * 1 // tm,), in_specs=in_specs, out_specs=pl.BlockSpec((tm, B_QKV), lambda i: (i, 0)),
        out_shape=jax.ShapeDtypeStruct((s, B_QKV), bf16), scratch_shapes=[pltpu.VMEM((6, tm, 128), f32)],
        compiler_params=_params("parallel"),
    )(*[a for g in grads for a in g])


def attn_merge(os_, lses, *, name):
    s = os_[0].shape[1]
    tm = _row_tile(s)
    ind_t = jnp.asarray(_head_indicator(8).T, dtype=bf16)
    dils = (1,) + DILATED

    def body(o0, o1, o2, l0, l1, l2, e_ref, o_ref, lse_ref, scr):
        both = [jnp.concatenate([o0[0], l0[0]], axis=1)]
        for og, lg, d in ((o1, l1, dils[1]), (o2, l2, dils[2])):
            both.append(_weave(scr, [jnp.concatenate([og[r], lg[r]], axis=1) for r in range(d)], d))
        ls = [b[:, 512:640] for b in both]
        m = jnp.maximum(jnp.maximum(ls[0], ls[1]), ls[2])
        tot = m + jnp.log(jnp.exp(ls[0] - m) + jnp.exp(ls[1] - m) + jnp.exp(ls[2] - m))
        lse_ref[...] = tot
        acc = jnp.zeros((tm, B_OUT), f32)
        for b, lg in zip(both, ls):
            acc = acc + _dot_split(jnp.exp(lg - tot), e_ref[...]) * b[:, :512]
        o_ref[...] = acc.astype(bf16)

    return _pcall(
        body, name=name, grid=(s * 1 // tm,),
        in_specs=[_res_spec(d, tm, 512) for d in dils] + [_res_spec(d, tm, 128) for d in dils]
        + [pl.BlockSpec((128, B_OUT), lambda i: (0, 0))],
        out_specs=[pl.BlockSpec((tm, B_OUT), lambda i: (i, 0)), pl.BlockSpec((tm, 128), lambda i: (i, 0))],
        out_shape=[jax.ShapeDtypeStruct((s, B_OUT), bf16), jax.ShapeDtypeStruct((s, 128), f32)],
        scratch_shapes=[pltpu.VMEM((5, tm, 128), f32)], compiler_params=_params("parallel"),
    )(*os_, *lses, ind_t)


def ada_mod(c_all, w, b, *, name):
    n = w.shape[2]

    def body(c_ref, w_ref, b_ref, o_ref):
        cv = c_ref[...]
        cond = cv * jax.nn.sigmoid(cv)
        o_ref[0] = jnp.dot(cond, w_ref[0], preferred_element_type=f32, precision=lax.Precision.HIGHEST) + b_ref[0]

    return _pcall(
        body, name=name, grid=(DEPTH,),
        in_specs=[pl.BlockSpec((N_DEV, D), lambda i: (0, 0)), pl.BlockSpec((1, D, n), lambda i: (i, 0, 0)),
                  pl.BlockSpec((1, 1, n), lambda i: (i, 0, 0))],
        out_specs=pl.BlockSpec((1, N_DEV, n), lambda i: (i, 0, 0)),
        out_shape=jax.ShapeDtypeStruct((DEPTH, N_DEV, n), f32), compiler_params=_params("arbitrary"),
    )(c_all, w, b)


def ada_grad(c_t, dm, *, name):
    n = dm.shape[2]

    def body(c_ref, dm_ref, o_ref):
        cv = c_ref[...]
        cond = cv * jax.nn.sigmoid(cv)
        acc = cond[:, 0:1] * dm_ref[0, 0:1, :]
        for b in range(1, N_DEV):
            acc = acc + cond[:, b:b + 1] * dm_ref[0, b:b + 1, :]
        o_ref[0] = acc

    return _pcall(
        body, name=name, grid=(DEPTH,),
        in_specs=[pl.BlockSpec((D, N_DEV), lambda i: (0, 0)), pl.BlockSpec((1, N_DEV, n), lambda i: (i, 0, 0))],
        out_specs=pl.BlockSpec((1, D, n), lambda i: (i, 0, 0)),
        out_shape=jax.ShapeDtypeStruct((DEPTH, D, n), f32), compiler_params=_params("arbitrary"),
    )(c_t, dm)


def _adam_math(w, g, m, v):
    m2 = B1 * m + (1.0 - B1) * g
    v2 = B2 * v + (1.0 - B2) * (g * g)
    mh = m2 / (1.0 - B1 ** STEP)
    vh = v2 / (1.0 - B2 ** STEP)
    return -LR * (mh / (jnp.sqrt(vh) + ADAM_EPS) + WD * w), m2, v2


def adamw(w, m, v, g, *, name):
    r, c = w.shape
    tr = 256 if r % 256 == 0 else r

    def body(w_ref, m_ref, v_ref, g_ref, d_ref, m2_ref, v2_ref):
        d_ref[...], m2_ref[...], v2_ref[...] = _adam_math(w_ref[...], g_ref[...], m_ref[...], v_ref[...])

    spec = pl.BlockSpec((tr, c), lambda i: (i, 0))
    return _pcall(
        body, name=name, grid=(r // tr,), in_specs=[spec] * 4, out_specs=[spec] * 3,
        out_shape=[jax.ShapeDtypeStruct((r, c), f32)] * 3, compiler_params=_params("parallel"),
    )(w, m, v, g)


def adamw_parts(w, m, v, own, sib, layer, prev=None, *, name):
    c = w.shape[1]
    r = own.shape[1]
    tr = 256 if r % 256 == 0 else r // 2
    off = layer * (r // tr)

    def body(w_ref, m_ref, v_ref, own_ref, sib_ref, *rest):
        g_ref, d_ref, m2_ref, v2_ref = rest[-4:]

        def total(ref):
            return ((ref[0].astype(f32) + ref[1].astype(f32)) + ref[2].astype(f32)) + ref[3].astype(f32)

        g = total(own_ref) + total(sib_ref)
        g_ref[...] = g
        d_ref[...], m2_ref[...], v2_ref[...] = _adam_math(w_ref[...], g, m_ref[...], v_ref[...])

    spec = pl.BlockSpec((tr, c), lambda i: (off + i, 0))
    pspec = pl.BlockSpec((4, tr, c), lambda i: (0, i, 0))
    prev = () if prev is None else tuple(prev)
    return _pcall(
        body, name=name, grid=(r // tr,), in_specs=[spec] * 3 + [pspec] * 2 + [ANY] * len(prev), out_specs=[spec] * 4,
        out_shape=[jax.ShapeDtypeStruct(w.shape, f32)] * 4,
        input_output_aliases={5 + q: q for q in range(len(prev))}, compiler_params=_params("parallel"),
    )(w, m, v, own, sib, *prev)


def sum_devices(g, *, name):
    _, r, c = g.shape

    def body(g_ref, o_ref):
        acc = g_ref[0]
        for k in range(1, N_DEV):
            acc = acc + g_ref[k]
        o_ref[...] = acc

    return _pcall(body, name=name, out_shape=jax.ShapeDtypeStruct((r, c), f32))(g)


def _place():
    x, y, c = lax.axis_index("x"), lax.axis_index("y"), lax.axis_index("c")
    chips = [(1 - x, y), (x, 1 - y), (1 - x, 1 - y)]
    return x, y, c, chips


def allgather8(v, *, name):
    r, c_ = v.shape

    def body(v_ref, o_ref, send_sems, recv_sems, local_sem):
        x, y, c, _ = _place()
        me = 4 * x + 2 * y + c
        mine = pltpu.make_async_copy(v_ref, o_ref.at[me], local_sem)
        mine.start()
        flips = [(fx, fy, fc) for fx in (0, 1) for fy in (0, 1) for fc in (0, 1)][1:]

        def peer(f):
            return (x ^ f[0], y ^ f[1], c ^ f[2])

        def copy(k, slot, to):
            return pltpu.make_async_remote_copy(
                src_ref=v_ref, dst_ref=o_ref.at[slot], send_sem=send_sems.at[k], recv_sem=recv_sems.at[k],
                device_id=to, device_id_type=MESH)

        sends = [copy(k, me, peer(f)) for k, f in enumerate(flips)]
        for cp in sends:
            cp.start()
        for k, f in enumerate(flips):
            px, py, pc = peer(f)
            copy(k, 4 * px + 2 * py + pc, (x, y, c)).wait_recv()
        for cp in sends:
            cp.wait_send()
        mine.wait()

    return _pcall(
        body, name=name, in_specs=[ANY], out_specs=ANY, out_shape=jax.ShapeDtypeStruct((N_DEV, r, c_), v.dtype),
        scratch_shapes=[pltpu.SemaphoreType.DMA((7,)), pltpu.SemaphoreType.DMA((7,)), pltpu.SemaphoreType.DMA],
    )(v)


def gather_weights(shards, *, name):
    n = len(shards)

    def body(*refs):
        src, out = refs[:n], refs[n:2 * n]
        send_a, recv_a, send_f, recv_f, local_sems = refs[2 * n:]
        x, y, c, chips = _place()
        sib = (x, y, 1 - c)
        me = 2 * x + y
        locals_ = [pltpu.make_async_copy(src[a], out[a].at[me], local_sems.at[a]) for a in range(n)]
        for cp in locals_:
            cp.start()

        def half(a, which):
            rh = src[a].shape[0] // 2
            return pl.ds(which * rh, rh)

        def first(a, k, chip_from, to):
            slot = 2 * chip_from[0] + chip_from[1]
            s_ref = src[a].at[half(a, c)]
            return pltpu.make_async_remote_copy(
                src_ref=s_ref, dst_ref=out[a].at[slot, half(a, c)], send_sem=send_a.at[3 * a + k],
                recv_sem=recv_a.at[3 * a + k], device_id=to, device_id_type=MESH)

        def passed(a, k, chip_from, which, to):
            slot = 2 * chip_from[0] + chip_from[1]
            ref = out[a].at[slot, half(a, which)]
            return pltpu.make_async_remote_copy(
                src_ref=ref, dst_ref=ref, send_sem=send_f.at[3 * a + k], recv_sem=recv_f.at[3 * a + k],
                device_id=to, device_id_type=MESH)

        sends = [first(a, k, (x, y), (*chip, c)) for a in range(n) for k, chip in enumerate(chips)]
        for cp in sends:
            cp.start()
        fwd = []
        for a in range(n):
            for k, chip in enumerate(chips):
                first(a, k, chip, (x, y, c)).wait_recv()
                cp = passed(a, k, chip, c, sib)
                cp.start()
                fwd.append(cp)
        for a in range(n):
            for k, chip in enumerate(chips):
                passed(a, k, chip, 1 - c, (x, y, c)).wait_recv()
        for cp in sends + fwd:
            cp.wait_send()
        for cp in locals_:
            cp.wait()

    return _pcall(
        body, name=name, in_specs=[ANY] * n, out_specs=[ANY] * n,
        out_shape=[jax.ShapeDtypeStruct((4,) + tuple(sh.shape), sh.dtype) for sh in shards],
        scratch_shapes=[pltpu.SemaphoreType.DMA((3 * n,)) for _ in range(4)] + [pltpu.SemaphoreType.DMA((n,))],
    )(*shards)


def _gather_direct(src, out, send_sems, recv_sems, local_sems):
    n = len(src)
    x, y, c, chips = _place()
    me = 2 * x + y

    def copy(a, k, slot, to):
        return pltpu.make_async_remote_copy(
            src_ref=src[a], dst_ref=out[a].at[slot], send_sem=send_sems.at[3 * a + k], recv_sem=recv_sems.at[3 * a + k],
            device_id=to, device_id_type=MESH)

    def start():
        for a in range(n):
            pltpu.make_async_copy(src[a], out[a].at[me], local_sems.at[a]).start()
            for k, chip in enumerate(chips):
                copy(a, k, me, (*chip, c)).start()

    def finish():
        for a in range(n):
            for k, chip in enumerate(chips):
                copy(a, k, 2 * chip[0] + chip[1], (x, y, c)).wait_recv()
        for a in range(n):
            for k in range(3):
                copy(a, k, me, (x, y, c)).wait_send()
            pltpu.make_async_copy(src[a], out[a].at[me], local_sems.at[a]).wait()

    return start, finish


def _gather_scratch(n):
    return [pltpu.SemaphoreType.DMA((3 * n,)), pltpu.SemaphoreType.DMA((3 * n,)), pltpu.SemaphoreType.DMA((n,))]


def _exchange(src, own, sibo, send_sems, recv_sems, local_sems):
    n = len(src)
    x, y, c, chips = _place()
    sib = (x, y, 1 - c)
    me = 2 * x + y

    def slot(chip):
        return 2 * chip[0] + chip[1]

    def copy(a, k, s_ref, d_ref, to):
        return pltpu.make_async_remote_copy(
            src_ref=s_ref, dst_ref=d_ref, send_sem=send_sems.at[7 * a + k], recv_sem=recv_sems.at[7 * a + k],
            device_id=to, device_id_type=MESH)

    def start():
        for a in range(n):
            pltpu.make_async_copy(src[a].at[me], own[a].at[me], local_sems.at[a]).start()
            copy(a, 0, src[a].at[me], sibo[a].at[me], sib).start()
            for k, chip in enumerate(chips):
                copy(a, 1 + k, src[a].at[slot(chip)], own[a].at[me], (*chip, c)).start()

    def forward():
        for a in range(n):
            for k, chip in enumerate(chips):
                copy(a, 1 + k, src[a].at[me], own[a].at[slot(chip)], (x, y, c)).wait_recv()
                copy(a, 4 + k, own[a].at[slot(chip)], sibo[a].at[slot(chip)], sib).start()

    def finish():
        for a in range(n):
            copy(a, 0, src[a].at[me], sibo[a].at[me], (x, y, c)).wait_recv()
            for k, chip in enumerate(chips):
                copy(a, 4 + k, src[a].at[me], sibo[a].at[slot(chip)], (x, y, c)).wait_recv()
        for a in range(n):
            for k in range(7):
                copy(a, k, src[a].at[me], own[a].at[me], (x, y, c)).wait_send()
            pltpu.make_async_copy(src[a].at[me], own[a].at[me], local_sems.at[a]).wait()

    return start, forward, finish


def _exchange_scratch(n):
    return [pltpu.SemaphoreType.DMA((7 * n,)), pltpu.SemaphoreType.DMA((7 * n,)), pltpu.SemaphoreType.DMA((n,))]


def exchange_grads(parts, *, name):
    n = len(parts)

    def body(*refs):
        start, forward, finish = _exchange(refs[:n], refs[n:2 * n], refs[2 * n:3 * n], *refs[3 * n:])
        start()
        forward()
        finish()

    shapes = [jax.ShapeDtypeStruct(p.shape, p.dtype) for p in parts]
    res = _pcall(body, name=name, in_specs=[ANY] * n, out_specs=[ANY] * (2 * n), out_shape=shapes + shapes,
                 scratch_shapes=_exchange_scratch(n))(*parts)
    return res[:n], res[n:]


def _natural(g, how):
    if how == "col":
        return jnp.moveaxis(g, 0, 1).reshape(g.shape[1], 4 * g.shape[2])
    return g.reshape(4 * g.shape[1], g.shape[2])


def _chunks(gw, how):
    k, n = gw.shape
    if how == "col":
        return jnp.moveaxis(gw.reshape(k, 4, n // 4), 1, 0).astype(bf16)
    return gw.reshape(4, k // 4, n).astype(bf16)


def kernel(x, c, ada_w, ada_b, norm_mix, norm_ffn, ffn_w_in, ffn_w_out, a_w_in, a_w_out, a_sink, b_w_in, b_w_out, final_norm, loss_target, m_ada_w, m_ada_b, m_norm_mix, m_norm_ffn, m_ffn_w_in, m_ffn_w_out, m_a_w_in, m_a_w_out, m_a_sink, m_b_w_in, m_b_w_out, m_final_norm, v_ada_w, v_ada_b, v_norm_mix, v_norm_ffn, v_ffn_w_in, v_ffn_w_out, v_a_w_in, v_a_w_out, v_a_sink, v_b_w_in, v_b_w_out, v_final_norm):
    s = x.shape[1]
    xi, yi, ci = lax.axis_index("x"), lax.axis_index("y"), lax.axis_index("c")
    chip = 2 * xi + yi
    dev = 2 * chip + ci
    x0 = x[0]
    tgt = loss_target[0]

    big = {"ffn_w_in": (ffn_w_in, "col"), "ffn_w_out": (ffn_w_out, "row"), "a_w_in": (a_w_in, "col"),
           "a_w_out": (a_w_out, "row"), "b_w_in": (b_w_in, "col"), "b_w_out": (b_w_out, "col")}
    names = list(big)

    def layer_keys(i):
        mix = "a" if i % 2 == 0 else "b"
        return [("ffn_w_in", i), ("ffn_w_out", i), (mix + "_w_in", i // 2), (mix + "_w_out", i // 2)]

    def shards_of(i):
        return [big[k][0][l].astype(bf16) for k, l in layer_keys(i)]

    def weights_of(i, gathered):
        return {k: (g if k == "ffn_w_in" else _natural(g, big[k][1])) for (k, _), g in zip(layer_keys(i), gathered)}

    mix0 = gather_weights(shards_of(0)[2:], name="gather_weights")

    c_all = allgather8(jnp.broadcast_to(c, (8, D)), name="gather_c")[:, 0, :]
    nsh = ada_w.shape[2]
    ada_b_sh = lax.dynamic_slice_in_dim(ada_b, chip * nsh, nsh, axis=1)[:, None, :]
    mod_part = ada_mod(c_all, ada_w, ada_b_sh, name="ada_mod")
    mod_all = allgather8(mod_part.reshape(DEPTH * N_DEV, nsh), name="gather_mod")
    mod_all = mod_all.reshape(4, 2, DEPTH, N_DEV, nsh)[:, 0]
    mod = lax.dynamic_index_in_dim(mod_all, dev, axis=2, keepdims=False)
    mod = jnp.moveaxis(mod, 0, 1).reshape(DEPTH, 6, 1, D)

    cfg_a = _Attn(s, mixer="a")
    cfg_b = [_Attn(s, mixer="b", group=g) for g in range(3)]
    no_sink = jnp.full((1, 2, 1, GQ * 64), NEG, f32)

    saved = []
    xc = x0
    for i in range(DEPTH):
        j = i // 2
        sh1, sc1, g1, sh2, sc2, g2 = (mod[i, q] for q in range(6))
        nmix, nffn = norm_mix[i][None, :], norm_ffn[i][None, :]
        mix = "a" if i % 2 == 0 else "b"
        if i == 0:
            h, qkv = mm_norm(xc, nmix, sc1, sh1, _natural(mix0[0], "col"), name="a_qkv")
            sinkcol = jnp.repeat(a_sink[j].reshape(2, 2, GQ), 128, axis=2)[:, :, None, :]
            o, lse, *ffn0 = attn_fwd(qkv[None], sinkcol, cfg_a, shards_of(0)[:2], out_dtype=bf16, name="a_attn_fwd_gather")
            o, lse = o[0], lse[0]
            wl = [weights_of(0, ffn0 + list(mix0))]
        elif i % 2 == 0:
            h, qkv = mm_norm(xc, nmix, sc1, sh1, wl[i]["a_w_in"], name="a_qkv")
            sinkcol = jnp.repeat(a_sink[j].reshape(2, 2, GQ), 128, axis=2)[:, :, None, :]
            o, lse = (t[0] for t in attn_fwd(qkv[None], sinkcol, cfg_a, out_dtype=bf16, name="a_attn_fwd"))
        else:
            h, qkv = mm_norm(xc, nmix, sc1, sh1, wl[i]["b_w_in"], name="b_qkv")
            qkv = [qkv[None]] + list(b_to_strided(qkv, name="b_to_strided"))
            outs = [attn_fwd(qkv[g], no_sink, cfg_b[g], out_dtype=f32, name=f"b_attn_fwd{g}") for g in range(3)]
            o, lse = attn_merge([t[0] for t in outs], [t[1] for t in outs], name="b_merge")
        x1 = mm_resid(o, wl[i][mix + "_w_out"], xc, g1, name=mix + "_out")
        nxt = shards_of(i + 1) if i + 1 < DEPTH else []
        h2, gu, act, x2, *got = ffn_fwd(x1, nffn, sc2, sh2, g2, wl[i]["ffn_w_in"], wl[i]["ffn_w_out"], nxt,
                                        name="ffn_fwd_gather" if nxt else "ffn_fwd")
        if nxt:
            wl.append(weights_of(i + 1, got))
        saved.append((xc, h, qkv, o, lse, x1, h2, gu, act))
        xc = x2

    dx, st_final = loss_head(xc, final_norm[None, :], tgt, name="loss_head")

    zero_row = jnp.zeros((1, D), f32)
    dmod_rows = [None] * DEPTH
    d_nmix, d_nffn = [None] * DEPTH, [None] * DEPTH
    d_sink = [None] * 2
    parts, exchanged = None, {}
    for i in reversed(range(DEPTH)):
        j = i // 2
        xin, h, qkv, o, lse, x1, h2, gu, act = saved[i]
        sh1, sc1, g1, sh2, sc2, g2 = (mod[i, q] for q in range(6))
        nmix, nffn = norm_mix[i][None, :], norm_ffn[i][None, :]
        mix = "a" if i % 2 == 0 else "b"
        w_fo, w_o, w_i = wl[i]["ffn_w_out"], wl[i][mix + "_w_out"], wl[i][mix + "_w_in"]
        dgu, dx1, st2, own_, sib_ = ffn_bwd_rows(dx, x1, gu, g2, nffn, sc2, wl[i]["ffn_w_in"], w_fo, parts or [],
                                                 name="ffn_bwd_rows_exchange" if parts else "ffn_bwd_rows")
        if parts:
            exchanged[i + 1] = (own_, sib_)
        gwo, dg2 = ffn_dw_out(act, dx, g2, w_fo, name="ffn_dw_out")
        dg2 = dg2[0:1]
        gwi = ffn_dw_in(h2, dgu, name="ffn_dw_in")
        gmo, dg1 = mm_tn(o, dx1, (g1, w_o), name=mix + "_dw_out")
        ffn_parts = [gwi, gwo.reshape(4, F // 4, D)]
        if i % 2 == 0:
            sinkrow = jnp.pad(a_sink[j].reshape(2, 8), ((0, 0), (0, 120))).reshape(1, 256)
            do, delta, dsk = mm_nt_delta(dx1, g1, w_o, o, lse, sinkrow, name="a_do")
            d_sink[j] = dsk[0].reshape(2, 128)[:, :8].reshape(16)
            dq, dk, dv, *ffn_x = attn_bwd(qkv[None], do[None], lse[None], delta[None], cfg_a, ffn_parts if i == 0 else [],
                                          name="a_attn_bwd_exchange" if i == 0 else "a_attn_bwd")
            dqkv = [dq[0], dk[0], dv[0]]
        else:
            do, delta, _ = mm_nt_delta(dx1, g1, w_o, o, lse, jnp.zeros((1, 128), f32), name="b_do")
            st = [do[None], lse[None], delta[None]] + list(b_bwd_to_strided(do, lse, delta, name="b_bwd_to_strided"))
            gr = [attn_bwd(qkv[g], *st[3 * g:3 * g + 3], cfg_b[g], name=f"b_attn_bwd{g}") for g in range(3)]
            dqkv = b_from_strided(gr, name="b_from_strided")
        gmi = mm_tn(h, dqkv, name=mix + "_dw_in")
        dx, st1 = mm_nt_norm_bwd(dqkv, w_i, xin, dx1, nmix, sc1, name=mix + "_dh")
        dmod_rows[i] = jnp.concatenate([st1[2:3], st1[1:2], dg1, st2[2:3], st2[1:2], dg2], axis=0)
        d_nmix[i], d_nffn[i] = st1[0:1], st2[0:1]
        mix_parts = [_chunks(gmi, big[mix + "_w_in"][1]), _chunks(gmo, big[mix + "_w_out"][1])]
        parts = ffn_parts + mix_parts
    own_m, sib_m = exchange_grads(mix_parts, name="exchange_grads")
    exchanged[0] = (list(ffn_x[0]) + list(own_m), list(ffn_x[1]) + list(sib_m))

    sink_row = jnp.pad(jnp.concatenate(d_sink), (0, D - 32))[None, :]
    stats = jnp.concatenate(dmod_rows + d_nmix + d_nffn + [sink_row, st_final[0:1], st_final[1:2]]
                            + [zero_row] * (STAT_ROWS - 35), axis=0)
    stats_all = allgather8(stats, name="gather_stats")
    tot = sum_devices(stats_all, name="sum_stats")
    loss = 0.5 * jnp.sum(tot[34]) / float(D)

    def pack(ab, nm, nf, sk, fnm, fill):
        return jnp.concatenate([ab.reshape(24, D), nm, nf, jnp.pad(sk.reshape(1, 32), ((0, 0), (0, D - 32)), constant_values=fill),
                                fnm[None, :], jnp.full((STAT_ROWS - 34, D), fill, f32)], axis=0)

    sd, sm, sv = adamw(pack(ada_b, norm_mix, norm_ffn, a_sink, final_norm, 0.0),
                       pack(m_ada_b, m_norm_mix, m_norm_ffn, m_a_sink, m_final_norm, 0.0),
                       pack(v_ada_b, v_norm_mix, v_norm_ffn, v_a_sink, v_final_norm, 1.0), tot, name="adamw_small")

    def unpack(p):
        return p[0:24].reshape(DEPTH, 6 * D), p[24:28], p[28:32], p[32, :32].reshape(2, 16), p[33]

    small = {"grad": unpack(tot), "delta": unpack(sd), "m": unpack(sm), "v": unpack(sv)}

    dmod_all = stats_all[:, 0:24, :].reshape(N_DEV, DEPTH, 6 * D)
    dm_sh = jnp.moveaxis(lax.dynamic_slice_in_dim(dmod_all, chip * nsh, nsh, axis=2), 0, 1)
    g_ada = ada_grad(c_all.T, dm_sh, name="ada_grad")
    r_ada = (DEPTH * D, nsh)
    ada_res = adamw(ada_w.reshape(r_ada), m_ada_w.reshape(r_ada), v_ada_w.reshape(r_ada), g_ada.reshape(r_ada), name="adamw_ada")
    ada_out = [g_ada] + [t.reshape(ada_w.shape) for t in ada_res]

    mom = {"ffn_w_in": (m_ffn_w_in, v_ffn_w_in), "ffn_w_out": (m_ffn_w_out, v_ffn_w_out), "a_w_in": (m_a_w_in, v_a_w_in),
           "a_w_out": (m_a_w_out, v_a_w_out), "b_w_in": (m_b_w_in, v_b_w_in), "b_w_out": (m_b_w_out, v_b_w_out)}
    big_out = {k: None for k in names}
    for i in reversed(range(DEPTH)):
        own_, sib_ = exchanged[i]
        for (k, l), o_, s_ in zip(layer_keys(i), own_, sib_):
            w = big[k][0]
            r2 = (-1, w.shape[-1])
            big_out[k] = adamw_parts(w.reshape(r2), mom[k][0].reshape(r2), mom[k][1].reshape(r2), o_, s_, l, big_out[k],
                                     name=f"adamw_{k}{l}")
    big_out = {k: [t.reshape(big[k][0].shape) for t in big_out[k]] for k in names}

    def leaves(q):
        sm_ = small[("grad", "delta", "m", "v")[q]]
        return (ada_out[q], sm_[0], sm_[1], sm_[2], big_out["ffn_w_in"][q], big_out["ffn_w_out"][q], big_out["a_w_in"][q],
                big_out["a_w_out"][q], sm_[3], big_out["b_w_in"][q], big_out["b_w_out"][q], sm_[4])

    return (loss, dx[None], *leaves(0), *leaves(1), *leaves(2), *leaves(3))
```

```python
import functools

import numpy as np
import jax
import jax.numpy as jnp
from jax import lax
from jax.experimental import pallas as pl
from jax.experimental.pallas import tpu as pltpu

f32 = jnp.float32
bf16 = jnp.bfloat16

D = 1024
DH = 64
GQ = 4
DEPTH = 4
F = 2816
A_QKV, A_OUT = 1536, 1024
B_QKV, B_OUT = 2304, 512
B_GROUPS = ((128, 1), (512, 4), (2048, 16))
RMS_EPS = 1e-6
NEG = -1e30
LR, B1, B2, ADAM_EPS, WD, STEP = 0.001, 0.9, 0.999, 1e-08, 0.01, 10
N_DEV = 8
STAT_ROWS = 40
CARRY_LEAD = 6
MESH = pl.DeviceIdType.MESH
ANY = pl.BlockSpec(memory_space=pl.ANY)


def _pcall(body, **kw):
    return pl.pallas_call(body, **kw)


def _params(*sem):
    return pltpu.CompilerParams(dimension_semantics=sem, vmem_limit_bytes=56 * 1024 * 1024)


def _row_tile(s, want=1024):
    return want if s % want == 0 else s


ROW_CHUNKS = 4


def mm_norm(x, nw, sc, sh, w, *, name):
    s, d = x.shape
    n = w.shape[1]
    tm = _row_tile(s)
    rc = tm // ROW_CHUNKS

    def body(x_ref, nw_ref, sc_ref, sh_ref, w_ref, h_ref, y_ref):
        hs = []
        for c in range(ROW_CHUNKS):
            xv = x_ref[c * rc:(c + 1) * rc, :]
            r = lax.rsqrt(jnp.mean(xv * xv, axis=-1, keepdims=True) + RMS_EPS)
            hs.append(((xv * r * nw_ref[...]) * (1.0 + sc_ref[...]) + sh_ref[...]).astype(bf16))
        ys = [jnp.dot(h, w_ref[...], preferred_element_type=f32) for h in hs]
        for c in range(ROW_CHUNKS):
            h_ref[c * rc:(c + 1) * rc, :] = hs[c]
            y_ref[c * rc:(c + 1) * rc, :] = ys[c].astype(bf16)

    vec = pl.BlockSpec((1, d), lambda i: (0, 0))
    return _pcall(
        body, name=name, grid=(s // tm,),
        in_specs=[pl.BlockSpec((tm, d), lambda i: (i, 0)), vec, vec, vec, pl.BlockSpec((d, n), lambda i: (0, 0))],
        out_specs=[pl.BlockSpec((tm, d), lambda i: (i, 0)), pl.BlockSpec((tm, n), lambda i: (i, 0))],
        out_shape=[jax.ShapeDtypeStruct((s, d), bf16), jax.ShapeDtypeStruct((s, n), bf16)],
        compiler_params=_params("parallel"),
    )(x, nw, sc, sh, w)


def mm_resid(a, w, xres, g, *, name):
    s, k = a.shape
    n = w.shape[1]
    tm = _row_tile(s)
    rc = tm // ROW_CHUNKS

    def body(a_ref, w_ref, x_ref, g_ref, o_ref):
        ys = [jnp.dot(a_ref[c * rc:(c + 1) * rc, :], w_ref[...], preferred_element_type=f32) for c in range(ROW_CHUNKS)]
        for c, y in enumerate(ys):
            o_ref[c * rc:(c + 1) * rc, :] = x_ref[c * rc:(c + 1) * rc, :] + g_ref[...] * y

    big = pl.BlockSpec((tm, n), lambda i: (i, 0))
    return _pcall(
        body, name=name, grid=(s // tm,),
        in_specs=[pl.BlockSpec((tm, k), lambda i: (i, 0)), pl.BlockSpec((k, n), lambda i: (0, 0)), big,
                  pl.BlockSpec((1, n), lambda i: (0, 0))],
        out_specs=big, out_shape=jax.ShapeDtypeStruct((s, n), f32), compiler_params=_params("parallel"),
    )(a, w, xres, g)


def mm_nt_delta(dx, g, w, o, lse, sinkrow, *, name):
    s, d = dx.shape
    n = w.shape[0]
    wd = lse.shape[1]
    tm = _row_tile(s)
    rc = tm // ROW_CHUNKS
    ind = jnp.asarray(_head_indicator(n // DH), dtype=bf16)

    def body(dx_ref, g_ref, w_ref, o_ref, lse_ref, sink_ref, e_ref, do_ref, dl_ref, ds_ref):
        rows = [slice(c * rc, (c + 1) * rc) for c in range(ROW_CHUNKS)]
        as_ = [(dx_ref[rw, :] * g_ref[...]).astype(bf16) for rw in rows]
        dos = [lax.dot_general(a, w_ref[...], (((1,), (1,)), ((), ())), preferred_element_type=f32).astype(bf16) for a in as_]
        dls = [_dot_split(do.astype(f32) * o_ref[rw, :].astype(f32), e_ref[...]) for do, rw in zip(dos, rows)]
        part = None
        for rw, do, dl in zip(rows, dos, dls):
            do_ref[rw, :] = do
            dl_ref[rw, :] = dl
            p = -jnp.sum(jnp.exp(sink_ref[...] - lse_ref[rw, :]) * dl, axis=0, keepdims=True)
            part = p if part is None else part + p
        part = jnp.concatenate([part, jnp.zeros((7, wd), f32)], axis=0)

        @pl.when(pl.program_id(0) == 0)
        def _():
            ds_ref[...] = part

        @pl.when(pl.program_id(0) != 0)
        def _():
            ds_ref[...] += part

    return _pcall(
        body, name=name, grid=(s // tm,),
        in_specs=[pl.BlockSpec((tm, d), lambda i: (i, 0)), pl.BlockSpec((1, d), lambda i: (0, 0)),
                  pl.BlockSpec((n, d), lambda i: (0, 0)), pl.BlockSpec((tm, n), lambda i: (i, 0)),
                  pl.BlockSpec((tm, wd), lambda i: (i, 0)), pl.BlockSpec((1, wd), lambda i: (0, 0)),
                  pl.BlockSpec((n, wd), lambda i: (0, 0))],
        out_specs=[pl.BlockSpec((tm, n), lambda i: (i, 0)), pl.BlockSpec((tm, wd), lambda i: (i, 0)),
                   pl.BlockSpec((8, wd), lambda i: (0, 0))],
        out_shape=[jax.ShapeDtypeStruct((s, n), bf16), jax.ShapeDtypeStruct((s, wd), f32), jax.ShapeDtypeStruct((8, wd), f32)],
        compiler_params=_params("arbitrary"),
    )(dx, g, w, o, lse, sinkrow, ind)


def mm_nt_norm_bwd(a, w, x, dres, nw, sc, *, name):
    pieces = list(a) if isinstance(a, (list, tuple)) else [a]
    npc = len(pieces)
    s = pieces[0].shape[0]
    k = sum(p.shape[1] for p in pieces)
    d = w.shape[0]
    tm = _row_tile(s)
    rc = tm // ROW_CHUNKS

    def body(*refs):
        a_refs = refs[:npc]
        w_ref, x_ref, dr_ref, nw_ref, sc_ref, o_ref, st_ref = refs[npc:]

        def a_rows(c):
            got = [r[c * rc:(c + 1) * rc, :] for r in a_refs]
            return jnp.concatenate(got, axis=1) if npc > 1 else got[0]

        dhs = [lax.dot_general(a_rows(c), w_ref[...], (((1,), (1,)), ((), ())), preferred_element_type=f32)
               for c in range(ROW_CHUNKS)]
        rows = None
        for c, dh in enumerate(dhs):
            xv = x_ref[c * rc:(c + 1) * rc, :]
            r = lax.rsqrt(jnp.mean(xv * xv, axis=-1, keepdims=True) + RMS_EPS)
            xh = xv * r
            dn = dh * (1.0 + sc_ref[...])
            dxh = dn * nw_ref[...]
            o_ref[c * rc:(c + 1) * rc, :] = dr_ref[c * rc:(c + 1) * rc, :] + r * (dxh - xh * jnp.mean(dxh * xh, axis=-1, keepdims=True))
            part = jnp.concatenate([
                jnp.sum(dn * xh, axis=0, keepdims=True),
                jnp.sum(dh * (xh * nw_ref[...]), axis=0, keepdims=True),
                jnp.sum(dh, axis=0, keepdims=True),
                jnp.zeros((5, d), f32)], axis=0)
            rows = part if rows is None else rows + part

        @pl.when(pl.program_id(0) == 0)
        def _():
            st_ref[...] = rows

        @pl.when(pl.program_id(0) != 0)
        def _():
            st_ref[...] += rows

    big = pl.BlockSpec((tm, d), lambda i: (i, 0))
    vec = pl.BlockSpec((1, d), lambda i: (0, 0))
    return _pcall(
        body, name=name, grid=(s // tm,),
        in_specs=[pl.BlockSpec((tm, p.shape[1]), lambda i: (i, 0)) for p in pieces]
        + [pl.BlockSpec((d, k), lambda i: (0, 0)), big, big, vec, vec],
        out_specs=[big, pl.BlockSpec((8, d), lambda i: (0, 0))],
        out_shape=[jax.ShapeDtypeStruct((s, d), f32), jax.ShapeDtypeStruct((8, d), f32)],
        compiler_params=_params("arbitrary"),
    )(*pieces, w, x, dres, nw, sc)


def mm_tn(a, b, scale=None, *, name):
    pieces = list(b) if isinstance(b, (list, tuple)) else [b]
    s, ka = a.shape
    nb = sum(p.shape[1] for p in pieces)
    npc = len(pieces)
    ts = _row_tile(s)
    ns = s // ts

    def body(a_ref, *rest):
        b_refs, rest = rest[:npc], rest[npc:]
        o_ref = rest[2] if scale is not None else rest[0]
        si = pl.program_id(0)
        bv = jnp.concatenate([r[...].astype(bf16) for r in b_refs], axis=1) if npc > 1 else b_refs[0][...].astype(bf16)
        part = lax.dot_general(a_ref[...], bv, (((0,), (0,)), ((), ())), preferred_element_type=f32)

        @pl.when(si == 0)
        def _():
            o_ref[...] = part

        @pl.when(si != 0)
        def _():
            o_ref[...] += part

        if scale is not None:
            g_ref, wb_ref, dg_ref = rest[0], rest[1], rest[3]

            @pl.when(si == ns - 1)
            def _():
                gm = o_ref[...]
                dg_ref[...] = jnp.sum(wb_ref[...].astype(f32) * gm, axis=0, keepdims=True)
                o_ref[...] = gm * g_ref[...]

    in_specs = [pl.BlockSpec((ts, ka), lambda k: (k, 0))] + [pl.BlockSpec((ts, p.shape[1]), lambda k: (k, 0)) for p in pieces]
    args = [a] + pieces
    whole = pl.BlockSpec((ka, nb), lambda k: (0, 0))
    out_specs = [whole]
    out_shape = [jax.ShapeDtypeStruct((ka, nb), f32)]
    if scale is not None:
        in_specs += [pl.BlockSpec((1, nb), lambda k: (0, 0)), whole]
        args += list(scale)
        out_specs.append(pl.BlockSpec((1, nb), lambda k: (0, 0)))
        out_shape.append(jax.ShapeDtypeStruct((1, nb), f32))
    res = _pcall(body, name=name, grid=(ns,), in_specs=in_specs, out_specs=out_specs, out_shape=out_shape,
                 compiler_params=_params("arbitrary"))(*args)
    return res if scale is not None else res[0]


def loss_head(x, fn, tgt, *, name):
    s, d = x.shape
    tm = _row_tile(s, 512)

    def body(x_ref, fn_ref, t_ref, dx_ref, st_ref):
        xv = x_ref[...]
        r = lax.rsqrt(jnp.mean(xv * xv, axis=-1, keepdims=True) + RMS_EPS)
        xh = xv * r
        err = xh * fn_ref[...] - t_ref[...]
        dy = err / float(d)
        dxh = dy * fn_ref[...]
        dx_ref[...] = r * (dxh - xh * jnp.mean(dxh * xh, axis=-1, keepdims=True))
        rows = jnp.concatenate([
            jnp.sum(dy * xh, axis=0, keepdims=True),
            jnp.sum(err * err, axis=0, keepdims=True),
            jnp.zeros((6, d), f32)], axis=0)

        @pl.when(pl.program_id(0) == 0)
        def _():
            st_ref[...] = rows

        @pl.when(pl.program_id(0) != 0)
        def _():
            st_ref[...] += rows

    big = pl.BlockSpec((tm, d), lambda i: (i, 0))
    return _pcall(
        body, name=name, grid=(s // tm,), in_specs=[big, pl.BlockSpec((1, d), lambda i: (0, 0)), big],
        out_specs=[big, pl.BlockSpec((8, d), lambda i: (0, 0))],
        out_shape=[jax.ShapeDtypeStruct((s, d), f32), jax.ShapeDtypeStruct((8, d), f32)],
        compiler_params=_params("arbitrary"),
    )(x, fn, tgt)


FC = 2 * F // 4
FFN_ROWS = 256


def _resident(pairs, sems):
    @pl.when(pl.program_id(0) == 0)
    def _():
        cps = [pltpu.make_async_copy(h, v, sems.at[i]) for i, (h, v) in enumerate(pairs)]
        for cp in cps:
            cp.start()
        for cp in cps:
            cp.wait()


def ffn_fwd(x, nw, sc, sh, g, w_in, w_out, carry=(), *, name):
    s, d = x.shape
    tm = _row_tile(s, FFN_ROWS)
    nsteps = s // tm
    nc = len(carry)

    def body(*refs):
        x_ref, nw_ref, sc_ref, sh_ref, g_ref, win_hbm, wout_hbm = refs[:7]
        h_ref, gu_ref, a_ref, o_ref = refs[7 + nc:11 + nc]
        win_v, wout_v, sems = refs[11 + 2 * nc:14 + 2 * nc]
        if nc:
            start, finish = _gather_direct(refs[7:7 + nc], refs[11 + nc:11 + 2 * nc], *refs[14 + 2 * nc:])
            pl.when(pl.program_id(0) == 0)(start)
        _resident([(win_hbm, win_v), (wout_hbm, wout_v)], sems)
        xv = x_ref[...]
        r = lax.rsqrt(jnp.mean(xv * xv, axis=-1, keepdims=True) + RMS_EPS)
        h = ((xv * r * nw_ref[...]) * (1.0 + sc_ref[...]) + sh_ref[...]).astype(bf16)
        h_ref[...] = h
        halves = [slice(c * FC, (c + 1) * FC) for c in range(2)]
        gts = [jnp.dot(h, win_v[c], preferred_element_type=f32) for c in range(2)]
        ups = [jnp.dot(h, win_v[c + 2], preferred_element_type=f32) for c in range(2)]
        acts = [(gt * jax.nn.sigmoid(gt) * up).astype(bf16) for gt, up in zip(gts, ups)]
        ys = [jnp.dot(act, wout_v[cs, :], preferred_element_type=f32) for act, cs in zip(acts, halves)]
        for cs, gt, up, act in zip(halves, gts, ups, acts):
            gu_ref[0, :, cs] = gt.astype(bf16)
            gu_ref[1, :, cs] = up.astype(bf16)
            a_ref[:, cs] = act
        o_ref[...] = xv + g_ref[...] * (ys[0] + ys[1])
        if nc:
            pl.when(pl.program_id(0) == nsteps - 1)(finish)

    big = pl.BlockSpec((tm, d), lambda i: (i, 0))
    vec = pl.BlockSpec((1, d), lambda i: (0, 0))
    return _pcall(
        body, name=name, grid=(nsteps,), in_specs=[big, vec, vec, vec, vec, ANY, ANY] + [ANY] * nc,
        out_specs=[big, pl.BlockSpec((2, tm, F), lambda i: (0, i, 0)), pl.BlockSpec((tm, F), lambda i: (i, 0)), big] + [ANY] * nc,
        out_shape=[jax.ShapeDtypeStruct((s, d), bf16), jax.ShapeDtypeStruct((2, s, F), bf16),
                   jax.ShapeDtypeStruct((s, F), bf16), jax.ShapeDtypeStruct((s, d), f32)]
        + [jax.ShapeDtypeStruct((4,) + tuple(sh_.shape), sh_.dtype) for sh_ in carry],
        scratch_shapes=[pltpu.VMEM((4, d, FC), bf16), pltpu.VMEM((F, d), bf16), pltpu.SemaphoreType.DMA((2,))]
        + (_gather_scratch(nc) if nc else []),
        compiler_params=_params("arbitrary"),
    )(x, nw, sc, sh, g, w_in, w_out, *carry)


def ffn_bwd_rows(dx, x, gu, g, nw, sc, w_in, w_out, carry=(), *, name):
    s, d = x.shape
    tm = _row_tile(s, FFN_ROWS)
    nsteps = s // tm
    nc = len(carry)
    nt_dims = (((1,), (1,)), ((), ()))

    def body(*refs):
        dx_ref, x_ref, gu_ref, g_ref, nw_ref, sc_ref, win_hbm, wout_hbm = refs[:8]
        dgu_ref, o_ref, st_ref = refs[8 + nc:11 + nc]
        win_v, wout_v, sems = refs[11 + 3 * nc:14 + 3 * nc]
        if nc:
            start, forward, finish = _exchange(refs[8:8 + nc], refs[11 + nc:11 + 2 * nc], refs[11 + 2 * nc:11 + 3 * nc],
                                               *refs[14 + 3 * nc:])
            pl.when(pl.program_id(0) == 0)(start)
            pl.when(pl.program_id(0) == max(nsteps - 1 - CARRY_LEAD, 0))(forward)
        _resident([(win_hbm, win_v), (wout_hbm, wout_v)], sems)
        dxv = dx_ref[...]
        a = (dxv * g_ref[...]).astype(bf16)
        halves = [slice(c * FC, (c + 1) * FC) for c in range(2)]
        das = [lax.dot_general(a, wout_v[cs, :], nt_dims, preferred_element_type=f32) for cs in halves]
        gts = [gu_ref[0, :, cs].astype(f32) for cs in halves]
        ups = [gu_ref[1, :, cs].astype(f32) for cs in halves]
        sgs = [jax.nn.sigmoid(gt) for gt in gts]
        dgates = [(da * up * (sg * (1.0 + gt * (1.0 - sg)))).astype(bf16) for da, gt, up, sg in zip(das, gts, ups, sgs)]
        dups = [(da * (gt * sg)).astype(bf16) for da, gt, sg in zip(das, gts, sgs)]
        for cs, dgate, dup in zip(halves, dgates, dups):
            dgu_ref[0, :, cs] = dgate
            dgu_ref[1, :, cs] = dup
        parts = [lax.dot_general(dgates[c], win_v[c], nt_dims, preferred_element_type=f32) for c in range(2)]
        parts += [lax.dot_general(dups[c], win_v[c + 2], nt_dims, preferred_element_type=f32) for c in range(2)]
        dh = (parts[0] + parts[1]) + (parts[2] + parts[3])
        xv = x_ref[...]
        r = lax.rsqrt(jnp.mean(xv * xv, axis=-1, keepdims=True) + RMS_EPS)
        xh = xv * r
        dn = dh * (1.0 + sc_ref[...])
        dxh = dn * nw_ref[...]
        o_ref[...] = dxv + r * (dxh - xh * jnp.mean(dxh * xh, axis=-1, keepdims=True))
        rows = jnp.concatenate([
            jnp.sum(dn * xh, axis=0, keepdims=True),
            jnp.sum(dh * (xh * nw_ref[...]), axis=0, keepdims=True),
            jnp.sum(dh, axis=0, keepdims=True),
            jnp.zeros((5, d), f32)], axis=0)

        @pl.when(pl.program_id(0) == 0)
        def _():
            st_ref[...] = rows

        @pl.when(pl.program_id(0) != 0)
        def _():
            st_ref[...] += rows

        if nc:
            pl.when(pl.program_id(0) == nsteps - 1)(finish)

    big = pl.BlockSpec((tm, d), lambda i: (i, 0))
    vec = pl.BlockSpec((1, d), lambda i: (0, 0))
    gus = pl.BlockSpec((2, tm, F), lambda i: (0, i, 0))
    cshapes = [jax.ShapeDtypeStruct(p.shape, p.dtype) for p in carry]
    res = _pcall(
        body, name=name, grid=(nsteps,), in_specs=[big, big, gus, vec, vec, vec, ANY, ANY] + [ANY] * nc,
        out_specs=[gus, big, pl.BlockSpec((8, d), lambda i: (0, 0))] + [ANY] * (2 * nc),
        out_shape=[jax.ShapeDtypeStruct((2, s, F), bf16), jax.ShapeDtypeStruct((s, d), f32), jax.ShapeDtypeStruct((8, d), f32)]
        + cshapes + cshapes,
        scratch_shapes=[pltpu.VMEM((4, d, FC), bf16), pltpu.VMEM((F, d), bf16), pltpu.SemaphoreType.DMA((2,))]
        + (_exchange_scratch(nc) if nc else []),
        compiler_params=_params("arbitrary"),
    )(dx, x, gu, g, nw, sc, w_in, w_out, *carry)
    return res[0], res[1], res[2], res[3:3 + nc], res[3 + nc:]


def ffn_dw_in(h, dgu, *, name):
    s, d = h.shape
    ts = _row_tile(s, 2048)
    ns = s // ts
    tn_dims = (((0,), (0,)), ((), ()))

    def body(h_ref, dgu_ref, o_ref, acc):
        k = pl.program_id(1)
        part = lax.dot_general(h_ref[...], dgu_ref[...], tn_dims, preferred_element_type=f32)

        @pl.when(k == 0)
        def _():
            acc[...] = part

        @pl.when(k != 0)
        def _():
            acc[...] += part

        @pl.when(k == ns - 1)
        def _():
            o_ref[...] = acc[...].astype(bf16)

    return _pcall(
        body, name=name, grid=(4, ns),
        in_specs=[pl.BlockSpec((ts, d), lambda q, k: (k, 0)), pl.BlockSpec((None, ts, FC), lambda q, k: (q // 2, k, q % 2))],
        out_specs=pl.BlockSpec((None, d, FC), lambda q, k: (q, 0, 0)),
        out_shape=jax.ShapeDtypeStruct((4, d, FC), bf16), scratch_shapes=[pltpu.VMEM((d, FC), f32)],
        compiler_params=_params("arbitrary", "arbitrary"),
    )(h, dgu)


def ffn_dw_out(a, dx, g, wb, *, name):
    s, fdim = a.shape
    d = dx.shape[1]
    ts = _row_tile(s)
    ns = s // ts
    tn = d // 2
    tn_dims = (((0,), (0,)), ((), ()))

    def body(a_ref, dx_ref, g_ref, wb_ref, o_ref, dg_ref, acc):
        k = pl.program_id(1)

        @pl.when(k == 0)
        def _():
            acc[...] = jnp.zeros_like(acc)

        acc[...] += lax.dot_general(a_ref[...], dx_ref[...].astype(bf16), tn_dims, preferred_element_type=f32)

        @pl.when(k == ns - 1)
        def _():
            gm = acc[...]
            dg_ref[...] = jnp.concatenate([jnp.sum(wb_ref[...].astype(f32) * gm, axis=0, keepdims=True),
                                           jnp.zeros((7, tn), f32)], axis=0)
            o_ref[...] = (gm * g_ref[...]).astype(bf16)

    return _pcall(
        body, name=name, grid=(2, ns),
        in_specs=[pl.BlockSpec((ts, fdim), lambda j, k: (k, 0)), pl.BlockSpec((ts, tn), lambda j, k: (k, j)),
                  pl.BlockSpec((1, tn), lambda j, k: (0, j)), pl.BlockSpec((fdim, tn), lambda j, k: (0, j))],
        out_specs=[pl.BlockSpec((fdim, tn), lambda j, k: (0, j)), pl.BlockSpec((8, tn), lambda j, k: (0, j))],
        out_shape=[jax.ShapeDtypeStruct((fdim, d), bf16), jax.ShapeDtypeStruct((8, d), f32)],
        scratch_shapes=[pltpu.VMEM((fdim, tn), f32)], compiler_params=_params("arbitrary", "arbitrary"),
    )(a, dx, g, wb)


def _alibi(n):
    return np.asarray(2.0 ** (-8.0 * np.arange(1, n + 1) / n), dtype=np.float32)


class _Attn:
    def __init__(self, s, *, mixer, group=0):
        if mixer == "a":
            self.blk, self.dil, self.npairs = 128, 1, 2
            self.qb0, self.kb0, self.vb0 = 0, 8, 10
            slopes = _alibi(16).reshape(2, 2, GQ)
        else:
            window, dil = B_GROUPS[group]
            self.blk, self.dil, self.npairs = window // (2 * dil), dil, 1
            self.qb0, self.kb0, self.vb0 = (0, 12, 15) if dil == 1 else (0, 4, 5)
            slopes = _alibi(24).reshape(3, 1, 2, GQ)[group]
        self.l = s // self.dil
        self.t = min(1024, self.l)
        self.nt = self.l // self.t
        self.nb = self.t // self.blk
        blk = self.blk
        qi = np.arange(blk)[:, None]
        rel = np.arange(3 * blk)[None, :] - blk - qi
        dist = (self.dil * np.abs(rel)).astype(np.float32)
        bias = -slopes[:, :, :, None, None] * dist[None, None, None]
        bias = np.where(np.abs(rel) <= blk, bias, np.float32(NEG)).astype(np.float32)
        self.bias = np.ascontiguousarray(np.swapaxes(bias.reshape(self.npairs, 2, GQ * blk, 3 * blk), -1, -2))

    def grid(self):
        return (self.dil, self.npairs, self.nt)

    def tile(self, width, col):
        return pl.BlockSpec((None, self.t, width), lambda r, hp, i: (r, i, col(hp)))

    def halo(self, width, col):
        t, blk, nbl = self.t, self.blk, self.l // self.blk
        per = t // blk
        return [
            pl.BlockSpec((None, blk, width), lambda r, hp, i: (r, jnp.maximum(i * per - 1, 0), col(hp))),
            self.tile(width, col),
            pl.BlockSpec((None, blk, width), lambda r, hp, i: (r, jnp.minimum((i + 1) * per, nbl - 1), col(hp))),
        ]

    def qcol(self, e):
        return lambda hp: self.qb0 + 2 * hp + e

    def kcol(self, hp):
        return self.kb0 + hp

    def vcol(self, hp):
        return self.vb0 + hp

    def pcol(self, hp):
        return hp


def _stack_heads(x):
    return jnp.concatenate([x[:, g * DH:(g + 1) * DH] for g in range(GQ)], axis=0)


def _unstack_heads(x, rows):
    return jnp.concatenate([x[g * rows:(g + 1) * rows] for g in range(GQ)], axis=1)


def _carrying(body, n_in, n_out, n_scratch, carry, kind, grid):
    nc = len(carry)
    if not nc:
        return body, [], [], [], []
    n_res = nc if kind == "gather" else 2 * nc

    def wrapped(*refs):
        ins, src = refs[:n_in], refs[n_in:n_in + nc]
        outs = refs[n_in + nc:n_in + nc + n_out]
        res = refs[n_in + nc + n_out:n_in + nc + n_out + n_res]
        scr = refs[n_in + nc + n_out + n_res:n_in + nc + n_out + n_res + n_scratch]
        sems = refs[n_in + nc + n_out + n_res + n_scratch:]
        ids = [pl.program_id(a) for a in range(len(grid))]
        first = functools.reduce(jnp.logical_and, [i == 0 for i in ids])
        last = functools.reduce(jnp.logical_and, [i == g - 1 for i, g in zip(ids, grid)])
        if kind == "gather":
            start, finish = _gather_direct(src, res, *sems)
            pl.when(first)(start)
        else:
            start, forward, finish = _exchange(src, res[:nc], res[nc:], *sems)
            pl.when(first)(start)
            early = [g - 1 for g in grid[:-1]] + [max(grid[-1] - 1 - CARRY_LEAD, 0)]
            pl.when(functools.reduce(jnp.logical_and, [i == g for i, g in zip(ids, early)]))(forward)
        body(*ins, *outs, *scr)
        pl.when(last)(finish)

    if kind == "gather":
        shapes = [jax.ShapeDtypeStruct((4,) + tuple(c.shape), c.dtype) for c in carry]
        sems = _gather_scratch(nc)
    else:
        shapes = [jax.ShapeDtypeStruct(c.shape, c.dtype) for c in carry] * 2
        sems = _exchange_scratch(nc)
    return wrapped, [ANY] * nc, [ANY] * n_res, shapes, sems


def attn_fwd(qkv, sinkcol, cfg, carry=(), *, out_dtype, name):
    blk, t, nb, nt, dil = cfg.blk, cfg.t, cfg.nb, cfg.nt, cfg.dil
    scale = DH ** -0.5

    def body(q0, q1, kp, km, kn, vp, vm, vn, bias_ref, sink_ref, o_ref, lse_ref, kx, vx):
        ti = pl.program_id(2)
        first, last = ti == 0, ti == nt - 1
        for hh in range(2):
            sl = slice(hh * DH, (hh + 1) * DH)
            for dst, (p_, m_, n_) in ((kx, (kp, km, kn)), (vx, (vp, vm, vn))):
                dst[hh, 0:blk] = p_[:, sl]
                dst[hh, blk:blk + t] = m_[:, sl]
                dst[hh, blk + t:] = n_[:, sl]
        krow = lax.broadcasted_iota(jnp.int32, (3 * blk, GQ * blk), 0)
        pairs = [(b, hh) for b in range(nb) for hh in range(2)]
        qs = [_stack_heads((q0, q1)[hh][b * blk:(b + 1) * blk, :]) * scale for b, hh in pairs]
        sc = [lax.dot_general(kx[hh, b * blk:(b + 3) * blk, :], q_, (((1,), (1,)), ((), ())), preferred_element_type=f32)
              for q_, (b, hh) in zip(qs, pairs)]
        sc = [s_ + bias_ref[0, hh] for s_, (b, hh) in zip(sc, pairs)]
        sc = [jnp.where(jnp.logical_and(first, krow < blk), NEG, s_) if b == 0 else s_ for s_, (b, hh) in zip(sc, pairs)]
        sc = [jnp.where(jnp.logical_and(last, krow >= 2 * blk), NEG, s_) if b == nb - 1 else s_ for s_, (b, hh) in zip(sc, pairs)]
        ms = [jnp.maximum(jnp.max(s_, axis=0, keepdims=True), sink_ref[0, hh]) for s_, (b, hh) in zip(sc, pairs)]
        ps = [jnp.exp(s_ - m_) for s_, m_ in zip(sc, ms)]
        ls = [jnp.sum(p_, axis=0, keepdims=True) + jnp.exp(sink_ref[0, hh] - m_) for p_, m_, (b, hh) in zip(ps, ms, pairs)]
        os_ = [lax.dot_general(vx[hh, b * blk:(b + 3) * blk, :], p_.astype(bf16), (((0,), (0,)), ((), ())),
                               preferred_element_type=f32) for p_, (b, hh) in zip(ps, pairs)]
        os_ = [o_ / l_ for o_, l_ in zip(os_, ls)]
        lses = [m_ + jnp.log(l_) for m_, l_ in zip(ms, ls)]
        for o_, (b, hh) in zip(os_, pairs):
            o_ref[b * blk:(b + 1) * blk, hh * 256:(hh + 1) * 256] = _unstack_heads(o_.T, blk).astype(out_dtype)
        stat_rows = [jnp.concatenate([lses[2 * b + hh][:, g * blk:(g + 1) * blk] for b in range(nb)], axis=1)
                     for hh in range(2) for g in range(GQ)]
        lse_ref[...] = jnp.concatenate(stat_rows + [jnp.zeros((128 - 2 * GQ, t), f32)], axis=0).T

    in_specs = [cfg.tile(256, cfg.qcol(e)) for e in range(2)]
    in_specs += cfg.halo(128, cfg.kcol) + cfg.halo(128, cfg.vcol)
    in_specs += [pl.BlockSpec((1, 2, 3 * blk, GQ * blk), lambda r, hp, i: (hp, 0, 0, 0)),
                 pl.BlockSpec((1, 2, 1, GQ * blk), lambda r, hp, i: (hp, 0, 0, 0))]
    body, c_in, c_out, c_shape, c_sems = _carrying(body, 10, 2, 2, carry, "gather", cfg.grid())
    return _pcall(
        body, name=name, grid=cfg.grid(), in_specs=in_specs + c_in,
        out_specs=[cfg.tile(512, cfg.pcol), cfg.tile(128, cfg.pcol)] + c_out,
        out_shape=[jax.ShapeDtypeStruct((dil, cfg.l, cfg.npairs * 512), out_dtype),
                   jax.ShapeDtypeStruct((dil, cfg.l, cfg.npairs * 128), f32)] + c_shape,
        scratch_shapes=[pltpu.VMEM((2, t + 2 * blk, DH), bf16), pltpu.VMEM((2, t + 2 * blk, DH), bf16)] + c_sems,
        compiler_params=_params("arbitrary", "arbitrary", "arbitrary"),
    )(*([qkv] * 8), jnp.asarray(cfg.bias), sinkcol, *carry)


def attn_bwd(qkv, do, lse, delta, cfg, carry=(), *, name):
    blk, t, nb, nt, dil, npairs = cfg.blk, cfg.t, cfg.nb, cfg.nt, cfg.dil, cfg.npairs
    scale = DH ** -0.5
    nt_dims = (((1,), (1,)), ((), ()))
    tn_dims = (((0,), (0,)), ((), ()))

    def body(q0p, q0m, q0n, q1p, q1m, q1n, kp, km, kn, vp, vm, vn, dop, dom, don, lp, lm, ln, dp_, dm_, dn_,
             bias_ref, dq_ref, dk_ref, dv_ref, kx, vx, dkx, dvx):
        ti = pl.program_id(2)
        first, last = ti == 0, ti == nt - 1
        for hh in range(2):
            sl = slice(hh * DH, (hh + 1) * DH)
            for dst, (p_, m_, n_) in ((kx, (kp, km, kn)), (vx, (vp, vm, vn))):
                dst[hh, 0:blk] = p_[:, sl]
                dst[hh, blk:blk + t] = m_[:, sl]
                dst[hh, blk + t:] = n_[:, sl]
        dkx[...] = jnp.zeros_like(dkx)
        dvx[...] = jnp.zeros_like(dvx)

        def slab(prev, main, nxt, e):
            if e == 0:
                return prev[...]
            if e == nb + 1:
                return nxt[...]
            return main[(e - 1) * blk:e * blk, :]

        krow = lax.broadcasted_iota(jnp.int32, (3 * blk, GQ * blk), 0)

        def keys(e):
            if e == 0:
                return 1, 2, slice(2 * blk, 3 * blk)
            if e == nb + 1:
                return nb, nb + 1, slice(0, blk)
            return e - 1, e + 2, slice(0, 3 * blk)

        def edge(sc, e):
            if e == 0:
                return jnp.where(first, NEG, sc)
            if e == nb + 1:
                return jnp.where(last, NEG, sc)
            if e == 1:
                sc = jnp.where(jnp.logical_and(first, krow < blk), NEG, sc)
            if e == nb:
                sc = jnp.where(jnp.logical_and(last, krow >= 2 * blk), NEG, sc)
            return sc

        lse_t = [lp[...].T, lm[...].T, ln[...].T]
        dl_t = [dp_[...].T, dm_[...].T, dn_[...].T]

        def stat_row(parts, e, hh):
            src, lo = (parts[0], 0) if e == 0 else (parts[2], 0) if e == nb + 1 else (parts[1], (e - 1) * blk)
            return jnp.concatenate([src[hh * GQ + g:hh * GQ + g + 1, lo:lo + blk] for g in range(GQ)], axis=1)

        pairs = [(e, hh) for e in range(nb + 2) for hh in range(2)]
        qs = [_stack_heads(slab(*((q0p, q0m, q0n), (q1p, q1m, q1n))[hh], e)) * scale for e, hh in pairs]
        dos = [_stack_heads(slab(dop, dom, don, e)[:, hh * 256:(hh + 1) * 256]) for e, hh in pairs]
        lse_r = [stat_row(lse_t, e, hh) for e, hh in pairs]
        dl_r = [stat_row(dl_t, e, hh) for e, hh in pairs]
        kw = [kx[hh, keys(e)[0] * blk:keys(e)[1] * blk, :] for e, hh in pairs]
        vw = [vx[hh, keys(e)[0] * blk:keys(e)[1] * blk, :] for e, hh in pairs]
        sc = [lax.dot_general(k_, q_, nt_dims, preferred_element_type=f32) for q_, k_ in zip(qs, kw)]
        dp = [lax.dot_general(v_, d_, nt_dims, preferred_element_type=f32) for d_, v_ in zip(dos, vw)]
        sc = [edge(s_ + bias_ref[0, hh, keys(e)[2], :], e) for s_, (e, hh) in zip(sc, pairs)]
        ps = [jnp.exp(s_ - l_) for s_, l_ in zip(sc, lse_r)]
        ds = [(p_ * (d_ - c_)).astype(bf16) for p_, d_, c_ in zip(ps, dp, dl_r)]
        pb = [p_.astype(bf16) for p_ in ps]
        dks = [jnp.dot(s_, q_, preferred_element_type=f32) for s_, q_ in zip(ds, qs)]
        dvs = [jnp.dot(p_, d_, preferred_element_type=f32) for p_, d_ in zip(pb, dos)]
        dqs = [lax.dot_general(s_, k_, tn_dims, preferred_element_type=f32) if 1 <= e <= nb else None
               for s_, k_, (e, hh) in zip(ds, kw, pairs)]
        for dk_, dv_, dq_, (e, hh) in zip(dks, dvs, dqs, pairs):
            k0, k1, _ = keys(e)
            dkx[hh, k0 * blk:k1 * blk, :] += dk_
            dvx[hh, k0 * blk:k1 * blk, :] += dv_
            if dq_ is not None:
                dq_ref[(e - 1) * blk:e * blk, hh * 256:(hh + 1) * 256] = (_unstack_heads(dq_, blk) * scale).astype(bf16)
        for hh in range(2):
            dk_ref[:, hh * DH:(hh + 1) * DH] = dkx[hh, blk:blk + t, :].astype(bf16)
            dv_ref[:, hh * DH:(hh + 1) * DH] = dvx[hh, blk:blk + t, :].astype(bf16)

    in_specs = cfg.halo(256, cfg.qcol(0)) + cfg.halo(256, cfg.qcol(1))
    in_specs += cfg.halo(128, cfg.kcol) + cfg.halo(128, cfg.vcol)
    in_specs += cfg.halo(512, cfg.pcol) + cfg.halo(128, cfg.pcol) + cfg.halo(128, cfg.pcol)
    in_specs += [pl.BlockSpec((1, 2, 3 * blk, GQ * blk), lambda r, hp, i: (hp, 0, 0, 0))]
    body, c_in, c_out, c_shape, c_sems = _carrying(body, 22, 3, 4, carry, "exchange", cfg.grid())
    res = _pcall(
        body, name=name, grid=cfg.grid(), in_specs=in_specs + c_in,
        out_specs=[cfg.tile(512, cfg.pcol), cfg.tile(128, cfg.pcol), cfg.tile(128, cfg.pcol)] + c_out,
        out_shape=[jax.ShapeDtypeStruct((dil, cfg.l, npairs * 512), bf16),
                   jax.ShapeDtypeStruct((dil, cfg.l, npairs * 128), bf16),
                   jax.ShapeDtypeStruct((dil, cfg.l, npairs * 128), bf16)] + c_shape,
        scratch_shapes=[pltpu.VMEM((2, t + 2 * blk, DH), bf16), pltpu.VMEM((2, t + 2 * blk, DH), bf16),
                        pltpu.VMEM((2, t + 2 * blk, DH), f32), pltpu.VMEM((2, t + 2 * blk, DH), f32)] + c_sems,
        compiler_params=_params("arbitrary", "arbitrary", "arbitrary"),
    )(*([qkv] * 12), do, do, do, lse, lse, lse, delta, delta, delta, jnp.asarray(cfg.bias), *carry)
    nc = len(carry)
    return (res[0], res[1], res[2], res[3:3 + nc], res[3 + nc:]) if nc else res


def _head_indicator(nheads):
    e = np.zeros((nheads * DH, (nheads // 8) * 128), np.float32)
    for c in range(nheads * DH):
        h = c // DH
        e[c, (h // 8) * 128 + h % 8] = 1.0
    return e


def _dot_split(x, e):
    hi = x.astype(bf16)
    lo = (x - hi.astype(f32)).astype(bf16)
    return jnp.dot(hi, e, preferred_element_type=f32) + jnp.dot(lo, e, preferred_element_type=f32)


def _spread(scr, x, d):
    tm, w = x.shape
    for j in range(w // 128):
        scr[j] = x[:, j * 128:(j + 1) * 128]
    return [jnp.concatenate([scr[j, pl.ds(r, tm // d, stride=d), :] for j in range(w // 128)], axis=1) for r in range(d)]


def _weave(scr, blocks, d):
    n, w = blocks[0].shape
    for r in range(d):
        for j in range(w // 128):
            scr[j, pl.ds(r, n, stride=d), :] = blocks[r][:, j * 128:(j + 1) * 128]
    return jnp.concatenate([scr[j] for j in range(w // 128)], axis=1)


def _res_spec(d, tm, w):
    return pl.BlockSpec((d, tm // d, w), lambda i: (0, i, 0))


DILATED = tuple(dil for _, dil in B_GROUPS[1:])


def b_to_strided(qkv, *, name):
    s = qkv.shape[0]
    tm = _row_tile(s)

    def body(x_ref, *rest):
        outs, scr = rest[:-1], rest[-1]
        for gi, (o_ref, d) in enumerate(zip(outs, DILATED), start=1):
            cols = jnp.concatenate([x_ref[:, gi * 512:(gi + 1) * 512], x_ref[:, 1536 + gi * 128:1536 + (gi + 1) * 128],
                                    x_ref[:, 1920 + gi * 128:1920 + (gi + 1) * 128]], axis=1).astype(f32)
            for r, blk_ in enumerate(_spread(scr, cols, d)):
                o_ref[r] = blk_.astype(bf16)

    return _pcall(
        body, name=name, grid=(s // tm,), in_specs=[pl.BlockSpec((tm, B_QKV), lambda i: (i, 0))],
        out_specs=[_res_spec(d, tm, 768) for d in DILATED],
        out_shape=[jax.ShapeDtypeStruct((d, s // d, 768), bf16) for d in DILATED],
        scratch_shapes=[pltpu.VMEM((6, tm, 128), f32)], compiler_params=_params("parallel"),
    )(qkv)


def b_bwd_to_strided(do, lse, delta, *, name):
    s = do.shape[0]
    tm = _row_tile(s)

    def body(do_ref, lse_ref, dl_ref, *rest):
        outs, scr = rest[:-1], rest[-1]
        allc = jnp.concatenate([do_ref[...].astype(f32), lse_ref[...], dl_ref[...]], axis=1)
        for gi, d in enumerate(DILATED):
            o_do, o_lse, o_dl = outs[3 * gi:3 * gi + 3]
            for r, blk_ in enumerate(_spread(scr, allc, d)):
                o_do[r] = blk_[:, :512].astype(bf16)
                o_lse[r] = blk_[:, 512:640]
                o_dl[r] = blk_[:, 640:768]

    out_specs, out_shape = [], []
    for d in DILATED:
        out_specs += [_res_spec(d, tm, 512), _res_spec(d, tm, 128), _res_spec(d, tm, 128)]
        out_shape += [jax.ShapeDtypeStruct((d, s // d, 512), bf16), jax.ShapeDtypeStruct((d, s // d, 128), f32),
                      jax.ShapeDtypeStruct((d, s // d, 128), f32)]
    return _pcall(
        body, name=name, grid=(s // tm,),
        in_specs=[pl.BlockSpec((tm, 512), lambda i: (i, 0)), pl.BlockSpec((tm, 128), lambda i: (i, 0)),
                  pl.BlockSpec((tm, 128), lambda i: (i, 0))],
        out_specs=out_specs, out_shape=out_shape, scratch_shapes=[pltpu.VMEM((6, tm, 128), f32)],
        compiler_params=_params("parallel"),
    )(do, lse, delta)


def b_from_strided(grads, *, name):
    s = grads[0][0].shape[1]
    tm = _row_tile(s)

    def body(*refs):
        ins, o_ref, scr = refs[:9], refs[9], refs[10]
        nat = [jnp.concatenate([ins[q][0].astype(f32) for q in range(3)], axis=1)]
        for gi, d in enumerate(DILATED, start=1):
            blocks = [jnp.concatenate([ins[3 * gi + q][r].astype(f32) for q in range(3)], axis=1) for r in range(d)]
            nat.append(_weave(scr, blocks, d))
        pieces = [nat[g][:, lo:hi] for lo, hi in ((0, 512), (512, 640), (640, 768)) for g in range(3)]
        o_ref[...] = jnp.concatenate(pieces, axis=1).astype(bf16)

    dils = (1,) + DILATED
    in_specs = [_res_spec(d, tm, w) for d in dils for w in (512, 128, 128)]
    return _pcall(
        body, name=name, grid=(s * 1 // tm,), in_specs=in_specs, out_specs=pl.BlockSpec((tm, B_QKV), lambda i: (i, 0)),
        out_shape=jax.ShapeDtypeStruct((s, B_QKV), bf16), scratch_shapes=[pltpu.VMEM((6, tm, 128), f32)],
        compiler_params=_params("parallel"),
    )(*[a for g in grads for a in g])


def attn_merge(os_, lses, *, name):
    s = os_[0].shape[1]
    tm = _row_tile(s)
    ind_t = jnp.asarray(_head_indicator(8).T, dtype=bf16)
    dils = (1,) + DILATED

    def body(o0, o1, o2, l0, l1, l2, e_ref, o_ref, lse_ref, scr):
        both = [jnp.concatenate([o0[0], l0[0]], axis=1)]
        for og, lg, d in ((o1, l1, dils[1]), (o2, l2, dils[2])):
            both.append(_weave(scr, [jnp.concatenate([og[r], lg[r]], axis=1) for r in range(d)], d))
        ls = [b[:, 512:640] for b in both]
        m = jnp.maximum(jnp.maximum(ls[0], ls[1]), ls[2])
        tot = m + jnp.log(jnp.exp(ls[0] - m) + jnp.exp(ls[1] - m) + jnp.exp(ls[2] - m))
        lse_ref[...] = tot
        acc = jnp.zeros((tm, B_OUT), f32)
        for b, lg in zip(both, ls):
            acc = acc + _dot_split(jnp.exp(lg - tot), e_ref[...]) * b[:, :512]
        o_ref[...] = acc.astype(bf16)

    return _pcall(
        body, name=name, grid=(s * 1 // tm,),
        in_specs=[_res_spec(d, tm, 512) for d in dils] + [_res_spec(d, tm, 128) for d in dils]
        + [pl.BlockSpec((128, B_OUT), lambda i: (0, 0))],
        out_specs=[pl.BlockSpec((tm, B_OUT), lambda i: (i, 0)), pl.BlockSpec((tm, 128), lambda i: (i, 0))],
        out_shape=[jax.ShapeDtypeStruct((s, B_OUT), bf16), jax.ShapeDtypeStruct((s, 128), f32)],
        scratch_shapes=[pltpu.VMEM((5, tm, 128), f32)], compiler_params=_params("parallel"),
    )(*os_, *lses, ind_t)


def ada_mod(c_all, w, b, *, name):
    n = w.shape[2]

    def body(c_ref, w_ref, b_ref, o_ref):
        cv = c_ref[...]
        cond = cv * jax.nn.sigmoid(cv)
        o_ref[0] = jnp.dot(cond, w_ref[0], preferred_element_type=f32, precision=lax.Precision.HIGHEST) + b_ref[0]

    return _pcall(
        body, name=name, grid=(DEPTH,),
        in_specs=[pl.BlockSpec((N_DEV, D), lambda i: (0, 0)), pl.BlockSpec((1, D, n), lambda i: (i, 0, 0)),
                  pl.BlockSpec((1, 1, n), lambda i: (i, 0, 0))],
        out_specs=pl.BlockSpec((1, N_DEV, n), lambda i: (i, 0, 0)),
        out_shape=jax.ShapeDtypeStruct((DEPTH, N_DEV, n), f32), compiler_params=_params("arbitrary"),
    )(c_all, w, b)


def ada_grad(c_t, dm, *, name):
    n = dm.shape[2]

    def body(c_ref, dm_ref, o_ref):
        cv = c_ref[...]
        cond = cv * jax.nn.sigmoid(cv)
        acc = cond[:, 0:1] * dm_ref[0, 0:1, :]
        for b in range(1, N_DEV):
            acc = acc + cond[:, b:b + 1] * dm_ref[0, b:b + 1, :]
        o_ref[0] = acc

    return _pcall(
        body, name=name, grid=(DEPTH,),
        in_specs=[pl.BlockSpec((D, N_DEV), lambda i: (0, 0)), pl.BlockSpec((1, N_DEV, n), lambda i: (i, 0, 0))],
        out_specs=pl.BlockSpec((1, D, n), lambda i: (i, 0, 0)),
        out_shape=jax.ShapeDtypeStruct((DEPTH, D, n), f32), compiler_params=_params("arbitrary"),
    )(c_t, dm)


def _adam_math(w, g, m, v):
    m2 = B1 * m + (1.0 - B1) * g
    v2 = B2 * v + (1.0 - B2) * (g * g)
    mh = m2 / (1.0 - B1 ** STEP)
    vh = v2 / (1.0 - B2 ** STEP)
    return -LR * (mh / (jnp.sqrt(vh) + ADAM_EPS) + WD * w), m2, v2


def adamw(w, m, v, g, *, name):
    r, c = w.shape
    tr = 256 if r % 256 == 0 else r

    def body(w_ref, m_ref, v_ref, g_ref, d_ref, m2_ref, v2_ref):
        d_ref[...], m2_ref[...], v2_ref[...] = _adam_math(w_ref[...], g_ref[...], m_ref[...], v_ref[...])

    spec = pl.BlockSpec((tr, c), lambda i: (i, 0))
    return _pcall(
        body, name=name, grid=(r // tr,), in_specs=[spec] * 4, out_specs=[spec] * 3,
        out_shape=[jax.ShapeDtypeStruct((r, c), f32)] * 3, compiler_params=_params("parallel"),
    )(w, m, v, g)


def adamw_parts(w, m, v, own, sib, layer, prev=None, *, name):
    c = w.shape[1]
    r = own.shape[1]
    tr = 256 if r % 256 == 0 else r // 2
    off = layer * (r // tr)

    def body(w_ref, m_ref, v_ref, own_ref, sib_ref, *rest):
        g_ref, d_ref, m2_ref, v2_ref = rest[-4:]

        def total(ref):
            return ((ref[0].astype(f32) + ref[1].astype(f32)) + ref[2].astype(f32)) + ref[3].astype(f32)

        g = total(own_ref) + total(sib_ref)
        g_ref[...] = g
        d_ref[...], m2_ref[...], v2_ref[...] = _adam_math(w_ref[...], g, m_ref[...], v_ref[...])

    spec = pl.BlockSpec((tr, c), lambda i: (off + i, 0))
    pspec = pl.BlockSpec((4, tr, c), lambda i: (0, i, 0))
    prev = () if prev is None else tuple(prev)
    return _pcall(
        body, name=name, grid=(r // tr,), in_specs=[spec] * 3 + [pspec] * 2 + [ANY] * len(prev), out_specs=[spec] * 4,
        out_shape=[jax.ShapeDtypeStruct(w.shape, f32)] * 4,
        input_output_aliases={5 + q: q for q in range(len(prev))}, compiler_params=_params("parallel"),
    )(w, m, v, own, sib, *prev)


def sum_devices(g, *, name):
    _, r, c = g.shape

    def body(g_ref, o_ref):
        acc = g_ref[0]
        for k in range(1, N_DEV):
            acc = acc + g_ref[k]
        o_ref[...] = acc

    return _pcall(body, name=name, out_shape=jax.ShapeDtypeStruct((r, c), f32))(g)


def _place():
    x, y, c = lax.axis_index("x"), lax.axis_index("y"), lax.axis_index("c")
    chips = [(1 - x, y), (x, 1 - y), (1 - x, 1 - y)]
    return x, y, c, chips


def allgather8(v, *, name):
    r, c_ = v.shape

    def body(v_ref, o_ref, send_sems, recv_sems, local_sem):
        x, y, c, _ = _place()
        me = 4 * x + 2 * y + c
        mine = pltpu.make_async_copy(v_ref, o_ref.at[me], local_sem)
        mine.start()
        flips = [(fx, fy, fc) for fx in (0, 1) for fy in (0, 1) for fc in (0, 1)][1:]

        def peer(f):
            return (x ^ f[0], y ^ f[1], c ^ f[2])

        def copy(k, slot, to):
            return pltpu.make_async_remote_copy(
                src_ref=v_ref, dst_ref=o_ref.at[slot], send_sem=send_sems.at[k], recv_sem=recv_sems.at[k],
                device_id=to, device_id_type=MESH)

        sends = [copy(k, me, peer(f)) for k, f in enumerate(flips)]
        for cp in sends:
            cp.start()
        for k, f in enumerate(flips):
            px, py, pc = peer(f)
            copy(k, 4 * px + 2 * py + pc, (x, y, c)).wait_recv()
        for cp in sends:
            cp.wait_send()
        mine.wait()

    return _pcall(
        body, name=name, in_specs=[ANY], out_specs=ANY, out_shape=jax.ShapeDtypeStruct((N_DEV, r, c_), v.dtype),
        scratch_shapes=[pltpu.SemaphoreType.DMA((7,)), pltpu.SemaphoreType.DMA((7,)), pltpu.SemaphoreType.DMA],
    )(v)


def gather_weights(shards, *, name):
    n = len(shards)

    def body(*refs):
        src, out = refs[:n], refs[n:2 * n]
        send_a, recv_a, send_f, recv_f, local_sems = refs[2 * n:]
        x, y, c, chips = _place()
        sib = (x, y, 1 - c)
        me = 2 * x + y
        locals_ = [pltpu.make_async_copy(src[a], out[a].at[me], local_sems.at[a]) for a in range(n)]
        for cp in locals_:
            cp.start()

        def half(a, which):
            rh = src[a].shape[0] // 2
            return pl.ds(which * rh, rh)

        def first(a, k, chip_from, to):
            slot = 2 * chip_from[0] + chip_from[1]
            s_ref = src[a].at[half(a, c)]
            return pltpu.make_async_remote_copy(
                src_ref=s_ref, dst_ref=out[a].at[slot, half(a, c)], send_sem=send_a.at[3 * a + k],
                recv_sem=recv_a.at[3 * a + k], device_id=to, device_id_type=MESH)

        def passed(a, k, chip_from, which, to):
            slot = 2 * chip_from[0] + chip_from[1]
            ref = out[a].at[slot, half(a, which)]
            return pltpu.make_async_remote_copy(
                src_ref=ref, dst_ref=ref, send_sem=send_f.at[3 * a + k], recv_sem=recv_f.at[3 * a + k],
                device_id=to, device_id_type=MESH)

        sends = [first(a, k, (x, y), (*chip, c)) for a in range(n) for k, chip in enumerate(chips)]
        for cp in sends:
            cp.start()
        fwd = []
        for a in range(n):
            for k, chip in enumerate(chips):
                first(a, k, chip, (x, y, c)).wait_recv()
                cp = passed(a, k, chip, c, sib)
                cp.start()
                fwd.append(cp)
        for a in range(n):
            for k, chip in enumerate(chips):
                passed(a, k, chip, 1 - c, (x, y, c)).wait_recv()
        for cp in sends + fwd:
            cp.wait_send()
        for cp in locals_:
            cp.wait()

    return _pcall(
        body, name=name, in_specs=[ANY] * n, out_specs=[ANY] * n,
        out_shape=[jax.ShapeDtypeStruct((4,) + tuple(sh.shape), sh.dtype) for sh in shards],
        scratch_shapes=[pltpu.SemaphoreType.DMA((3 * n,)) for _ in range(4)] + [pltpu.SemaphoreType.DMA((n,))],
    )(*shards)


def _gather_direct(src, out, send_sems, recv_sems, local_sems):
    n = len(src)
    x, y, c, chips = _place()
    me = 2 * x + y

    def copy(a, k, slot, to):
        return pltpu.make_async_remote_copy(
            src_ref=src[a], dst_ref=out[a].at[slot], send_sem=send_sems.at[3 * a + k], recv_sem=recv_sems.at[3 * a + k],
            device_id=to, device_id_type=MESH)

    def start():
        for a in range(n):
            pltpu.make_async_copy(src[a], out[a].at[me], local_sems.at[a]).start()
            for k, chip in enumerate(chips):
                copy(a, k, me, (*chip, c)).start()

    def finish():
        for a in range(n):
            for k, chip in enumerate(chips):
                copy(a, k, 2 * chip[0] + chip[1], (x, y, c)).wait_recv()
        for a in range(n):
            for k in range(3):
                copy(a, k, me, (x, y, c)).wait_send()
            pltpu.make_async_copy(src[a], out[a].at[me], local_sems.at[a]).wait()

    return start, finish


def _gather_scratch(n):
    return [pltpu.SemaphoreType.DMA((3 * n,)), pltpu.SemaphoreType.DMA((3 * n,)), pltpu.SemaphoreType.DMA((n,))]


def _exchange(src, own, sibo, send_sems, recv_sems, local_sems):
    n = len(src)
    x, y, c, chips = _place()
    sib = (x, y, 1 - c)
    me = 2 * x + y

    def slot(chip):
        return 2 * chip[0] + chip[1]

    def copy(a, k, s_ref, d_ref, to):
        return pltpu.make_async_remote_copy(
            src_ref=s_ref, dst_ref=d_ref, send_sem=send_sems.at[7 * a + k], recv_sem=recv_sems.at[7 * a + k],
            device_id=to, device_id_type=MESH)

    def start():
        for a in range(n):
            pltpu.make_async_copy(src[a].at[me], own[a].at[me], local_sems.at[a]).start()
            copy(a, 0, src[a].at[me], sibo[a].at[me], sib).start()
            for k, chip in enumerate(chips):
                copy(a, 1 + k, src[a].at[slot(chip)], own[a].at[me], (*chip, c)).start()

    def forward():
        for a in range(n):
            for k, chip in enumerate(chips):
                copy(a, 1 + k, src[a].at[me], own[a].at[slot(chip)], (x, y, c)).wait_recv()
                copy(a, 4 + k, own[a].at[slot(chip)], sibo[a].at[slot(chip)], sib).start()

    def finish():
        for a in range(n):
            copy(a, 0, src[a].at[me], sibo[a].at[me], (x, y, c)).wait_recv()
            for k, chip in enumerate(chips):
                copy(a, 4 + k, src[a].at[me], sibo[a].at[slot(chip)], (x, y, c)).wait_recv()
        for a in range(n):
            for k in range(7):
                copy(a, k, src[a].at[me], own[a].at[me], (x, y, c)).wait_send()
            pltpu.make_async_copy(src[a].at[me], own[a].at[me], local_sems.at[a]).wait()

    return start, forward, finish


def _exchange_scratch(n):
    return [pltpu.SemaphoreType.DMA((7 * n,)), pltpu.SemaphoreType.DMA((7 * n,)), pltpu.SemaphoreType.DMA((n,))]


def exchange_grads(parts, *, name):
    n = len(parts)

    def body(*refs):
        start, forward, finish = _exchange(refs[:n], refs[n:2 * n], refs[2 * n:3 * n], *refs[3 * n:])
        start()
        forward()
        finish()

    shapes = [jax.ShapeDtypeStruct(p.shape, p.dtype) for p in parts]
    res = _pcall(body, name=name, in_specs=[ANY] * n, out_specs=[ANY] * (2 * n), out_shape=shapes + shapes,
                 scratch_shapes=_exchange_scratch(n))(*parts)
    return res[:n], res[n:]


def _natural(g, how):
    if how == "col":
        return jnp.moveaxis(g, 0, 1).reshape(g.shape[1], 4 * g.shape[2])
    return g.reshape(4 * g.shape[1], g.shape[2])


def _chunks(gw, how):
    k, n = gw.shape
    if how == "col":
        return jnp.moveaxis(gw.reshape(k, 4, n // 4), 1, 0).astype(bf16)
    return gw.reshape(4, k // 4, n).astype(bf16)


def kernel(x, c, ada_w, ada_b, norm_mix, norm_ffn, ffn_w_in, ffn_w_out, a_w_in, a_w_out, a_sink, b_w_in, b_w_out, final_norm, loss_target, m_ada_w, m_ada_b, m_norm_mix, m_norm_ffn, m_ffn_w_in, m_ffn_w_out, m_a_w_in, m_a_w_out, m_a_sink, m_b_w_in, m_b_w_out, m_final_norm, v_ada_w, v_ada_b, v_norm_mix, v_norm_ffn, v_ffn_w_in, v_ffn_w_out, v_a_w_in, v_a_w_out, v_a_sink, v_b_w_in, v_b_w_out, v_final_norm):
    s = x.shape[1]
    xi, yi, ci = lax.axis_index("x"), lax.axis_index("y"), lax.axis_index("c")
    chip = 2 * xi + yi
    dev = 2 * chip + ci
    x0 = x[0]
    tgt = loss_target[0]

    big = {"ffn_w_in": (ffn_w_in, "col"), "ffn_w_out": (ffn_w_out, "row"), "a_w_in": (a_w_in, "col"),
           "a_w_out": (a_w_out, "row"), "b_w_in": (b_w_in, "col"), "b_w_out": (b_w_out, "col")}
    names = list(big)

    def layer_keys(i):
        mix = "a" if i % 2 == 0 else "b"
        return [("ffn_w_in", i), ("ffn_w_out", i), (mix + "_w_in", i // 2), (mix + "_w_out", i // 2)]

    def shards_of(i):
        return [big[k][0][l].astype(bf16) for k, l in layer_keys(i)]

    def weights_of(i, gathered):
        return {k: (g if k == "ffn_w_in" else _natural(g, big[k][1])) for (k, _), g in zip(layer_keys(i), gathered)}

    mix0 = gather_weights(shards_of(0)[2:], name="gather_weights")

    c_all = allgather8(jnp.broadcast_to(c, (8, D)), name="gather_c")[:, 0, :]
    nsh = ada_w.shape[2]
    ada_b_sh = lax.dynamic_slice_in_dim(ada_b, chip * nsh, nsh, axis=1)[:, None, :]
    mod_part = ada_mod(c_all, ada_w, ada_b_sh, name="ada_mod")
    mod_all = allgather8(mod_part.reshape(DEPTH * N_DEV, nsh), name="gather_mod")
    mod_all = mod_all.reshape(4, 2, DEPTH, N_DEV, nsh)[:, 0]
    mod = lax.dynamic_index_in_dim(mod_all, dev, axis=2, keepdims=False)
    mod = jnp.moveaxis(mod, 0, 1).reshape(DEPTH, 6, 1, D)

    cfg_a = _Attn(s, mixer="a")
    cfg_b = [_Attn(s, mixer="b", group=g) for g in range(3)]
    no_sink = jnp.full((1, 2, 1, GQ * 64), NEG, f32)

    saved = []
    xc = x0
    for i in range(DEPTH):
        j = i // 2
        sh1, sc1, g1, sh2, sc2, g2 = (mod[i, q] for q in range(6))
        nmix, nffn = norm_mix[i][None, :], norm_ffn[i][None, :]
        mix = "a" if i % 2 == 0 else "b"
        if i == 0:
            h, qkv = mm_norm(xc, nmix, sc1, sh1, _natural(mix0[0], "col"), name="a_qkv")
            sinkcol = jnp.repeat(a_sink[j].reshape(2, 2, GQ), 128, axis=2)[:, :, None, :]
            o, lse, *ffn0 = attn_fwd(qkv[None], sinkcol, cfg_a, shards_of(0)[:2], out_dtype=bf16, name="a_attn_fwd_gather")
            o, lse = o[0], lse[0]
            wl = [weights_of(0, ffn0 + list(mix0))]
        elif i % 2 == 0:
            h, qkv = mm_norm(xc, nmix, sc1, sh1, wl[i]["a_w_in"], name="a_qkv")
            sinkcol = jnp.repeat(a_sink[j].reshape(2, 2, GQ), 128, axis=2)[:, :, None, :]
            o, lse = (t[0] for t in attn_fwd(qkv[None], sinkcol, cfg_a, out_dtype=bf16, name="a_attn_fwd"))
        else:
            h, qkv = mm_norm(xc, nmix, sc1, sh1, wl[i]["b_w_in"], name="b_qkv")
            qkv = [qkv[None]] + list(b_to_strided(qkv, name="b_to_strided"))
            outs = [attn_fwd(qkv[g], no_sink, cfg_b[g], out_dtype=f32, name=f"b_attn_fwd{g}") for g in range(3)]
            o, lse = attn_merge([t[0] for t in outs], [t[1] for t in outs], name="b_merge")
        x1 = mm_resid(o, wl[i][mix + "_w_out"], xc, g1, name=mix + "_out")
        nxt = shards_of(i + 1) if i + 1 < DEPTH else []
        h2, gu, act, x2, *got = ffn_fwd(x1, nffn, sc2, sh2, g2, wl[i]["ffn_w_in"], wl[i]["ffn_w_out"], nxt,
                                        name="ffn_fwd_gather" if nxt else "ffn_fwd")
        if nxt:
            wl.append(weights_of(i + 1, got))
        saved.append((xc, h, qkv, o, lse, x1, h2, gu, act))
        xc = x2

    dx, st_final = loss_head(xc, final_norm[None, :], tgt, name="loss_head")

    zero_row = jnp.zeros((1, D), f32)
    dmod_rows = [None] * DEPTH
    d_nmix, d_nffn = [None] * DEPTH, [None] * DEPTH
    d_sink = [None] * 2
    parts, exchanged = None, {}
    for i in reversed(range(DEPTH)):
        j = i // 2
        xin, h, qkv, o, lse, x1, h2, gu, act = saved[i]
        sh1, sc1, g1, sh2, sc2, g2 = (mod[i, q] for q in range(6))
        nmix, nffn = norm_mix[i][None, :], norm_ffn[i][None, :]
        mix = "a" if i % 2 == 0 else "b"
        w_fo, w_o, w_i = wl[i]["ffn_w_out"], wl[i][mix + "_w_out"], wl[i][mix + "_w_in"]
        dgu, dx1, st2, own_, sib_ = ffn_bwd_rows(dx, x1, gu, g2, nffn, sc2, wl[i]["ffn_w_in"], w_fo, parts or [],
                                                 name="ffn_bwd_rows_exchange" if parts else "ffn_bwd_rows")
        if parts:
            exchanged[i + 1] = (own_, sib_)
        gwo, dg2 = ffn_dw_out(act, dx, g2, w_fo, name="ffn_dw_out")
        dg2 = dg2[0:1]
        gwi = ffn_dw_in(h2, dgu, name="ffn_dw_in")
        gmo, dg1 = mm_tn(o, dx1, (g1, w_o), name=mix + "_dw_out")
        ffn_parts = [gwi, gwo.reshape(4, F // 4, D)]
        if i % 2 == 0:
            sinkrow = jnp.pad(a_sink[j].reshape(2, 8), ((0, 0), (0, 120))).reshape(1, 256)
            do, delta, dsk = mm_nt_delta(dx1, g1, w_o, o, lse, sinkrow, name="a_do")
            d_sink[j] = dsk[0].reshape(2, 128)[:, :8].reshape(16)
            dq, dk, dv, *ffn_x = attn_bwd(qkv[None], do[None], lse[None], delta[None], cfg_a, ffn_parts if i == 0 else [],
                                          name="a_attn_bwd_exchange" if i == 0 else "a_attn_bwd")
            dqkv = [dq[0], dk[0], dv[0]]
        else:
            do, delta, _ = mm_nt_delta(dx1, g1, w_o, o, lse, jnp.zeros((1, 128), f32), name="b_do")
            st = [do[None], lse[None], delta[None]] + list(b_bwd_to_strided(do, lse, delta, name="b_bwd_to_strided"))
            gr = [attn_bwd(qkv[g], *st[3 * g:3 * g + 3], cfg_b[g], name=f"b_attn_bwd{g}") for g in range(3)]
            dqkv = b_from_strided(gr, name="b_from_strided")
        gmi = mm_tn(h, dqkv, name=mix + "_dw_in")
        dx, st1 = mm_nt_norm_bwd(dqkv, w_i, xin, dx1, nmix, sc1, name=mix + "_dh")
        dmod_rows[i] = jnp.concatenate([st1[2:3], st1[1:2], dg1, st2[2:3], st2[1:2], dg2], axis=0)
        d_nmix[i], d_nffn[i] = st1[0:1], st2[0:1]
        mix_parts = [_chunks(gmi, big[mix + "_w_in"][1]), _chunks(gmo, big[mix + "_w_out"][1])]
        parts = ffn_parts + mix_parts
    own_m, sib_m = exchange_grads(mix_parts, name="exchange_grads")
    exchanged[0] = (list(ffn_x[0]) + list(own_m), list(ffn_x[1]) + list(sib_m))

    sink_row = jnp.pad(jnp.concatenate(d_sink), (0, D - 32))[None, :]
    stats = jnp.concatenate(dmod_rows + d_nmix + d_nffn + [sink_row, st_final[0:1], st_final[1:2]]
                            + [zero_row] * (STAT_ROWS - 35), axis=0)
    stats_all = allgather8(stats, name="gather_stats")
    tot = sum_devices(stats_all, name="sum_stats")
    loss = 0.5 * jnp.sum(tot[34]) / float(D)

    def pack(ab, nm, nf, sk, fnm, fill):
        return jnp.concatenate([ab.reshape(24, D), nm, nf, jnp.pad(sk.reshape(1, 32), ((0, 0), (0, D - 32)), constant_values=fill),
                                fnm[None, :], jnp.full((STAT_ROWS - 34, D), fill, f32)], axis=0)

    sd, sm, sv = adamw(pack(ada_b, norm_mix, norm_ffn, a_sink, final_norm, 0.0),
                       pack(m_ada_b, m_norm_mix, m_norm_ffn, m_a_sink, m_final_norm, 0.0),
                       pack(v_ada_b, v_norm_mix, v_norm_ffn, v_a_sink, v_final_norm, 1.0), tot, name="adamw_small")

    def unpack(p):
        return p[0:24].reshape(DEPTH, 6 * D), p[24:28], p[28:32], p[32, :32].reshape(2, 16), p[33]

    small = {"grad": unpack(tot), "delta": unpack(sd), "m": unpack(sm), "v": unpack(sv)}

    dmod_all = stats_all[:, 0:24, :].reshape(N_DEV, DEPTH, 6 * D)
    dm_sh = jnp.moveaxis(lax.dynamic_slice_in_dim(dmod_all, chip * nsh, nsh, axis=2), 0, 1)
    g_ada = ada_grad(c_all.T, dm_sh, name="ada_grad")
    r_ada = (DEPTH * D, nsh)
    ada_res = adamw(ada_w.reshape(r_ada), m_ada_w.reshape(r_ada), v_ada_w.reshape(r_ada), g_ada.reshape(r_ada), name="adamw_ada")
    ada_out = [g_ada] + [t.reshape(ada_w.shape) for t in ada_res]

    mom = {"ffn_w_in": (m_ffn_w_in, v_ffn_w_in), "ffn_w_out": (m_ffn_w_out, v_ffn_w_out), "a_w_in": (m_a_w_in, v_a_w_in),
           "a_w_out": (m_a_w_out, v_a_w_out), "b_w_in": (m_b_w_in, v_b_w_in), "b_w_out": (m_b_w_out, v_b_w_out)}
    big_out = {k: None for k in names}
    for i in reversed(range(DEPTH)):
        own_, sib_ = exchanged[i]
        for (k, l), o_, s_ in zip(layer_keys(i), own_, sib_):
            w = big[k][0]
            r2 = (-1, w.shape[-1])
            big_out[k] = adamw_parts(w.reshape(r2), mom[k][0].reshape(r2), mom[k][1].reshape(r2), o_, s_, l, big_out[k],
                                     name=f"adamw_{k}{l}")
    big_out = {k: [t.reshape(big[k][0].shape) for t in big_out[k]] for k in names}

    def leaves(q):
        sm_ = small[("grad", "delta", "m", "v")[q]]
        return (ada_out[q], sm_[0], sm_[1], sm_[2], big_out["ffn_w_in"][q], big_out["ffn_w_out"][q], big_out["a_w_in"][q],
                big_out["a_w_out"][q], sm_[3], big_out["b_w_in"][q], big_out["b_w_out"][q], sm_[4])

    return (loss, dx[None], *leaves(0), *leaves(1), *leaves(2), *leaves(3))
```

```python
import functools

import numpy as np
import jax
import jax.numpy as jnp
from jax import lax
from jax.experimental import pallas as pl
from jax.experimental.pallas import tpu as pltpu

f32 = jnp.float32
bf16 = jnp.bfloat16

D = 1024
DH = 64
GQ = 4
DEPTH = 4
F = 2816
A_QKV, A_OUT = 1536, 1024
B_QKV, B_OUT = 2304, 512
B_GROUPS = ((128, 1), (512, 4), (2048, 16))
RMS_EPS = 1e-6
NEG = -1e30
LR, B1, B2, ADAM_EPS, WD, STEP = 0.001, 0.9, 0.999, 1e-08, 0.01, 10
N_DEV = 8
STAT_ROWS = 40
CARRY_LEAD = 6
MESH = pl.DeviceIdType.MESH
ANY = pl.BlockSpec(memory_space=pl.ANY)


def _pcall(body, **kw):
    return pl.pallas_call(body, **kw)


def _params(*sem):
    return pltpu.CompilerParams(dimension_semantics=sem, vmem_limit_bytes=56 * 1024 * 1024)


def _row_tile(s, want=1024):
    return want if s % want == 0 else s


ROW_CHUNKS = 4


def mm_norm(x, nw, sc, sh, w, *, name):
    s, d = x.shape
    n = w.shape[1]
    tm = _row_tile(s)
    rc = tm // ROW_CHUNKS

    def body(x_ref, nw_ref, sc_ref, sh_ref, w_ref, h_ref, y_ref):
        hs = []
        for c in range(ROW_CHUNKS):
            xv = x_ref[c * rc:(c + 1) * rc, :]
            r = lax.rsqrt(jnp.mean(xv * xv, axis=-1, keepdims=True) + RMS_EPS)
            hs.append(((xv * r * nw_ref[...]) * (1.0 + sc_ref[...]) + sh_ref[...]).astype(bf16))
        ys = [jnp.dot(h, w_ref[...], preferred_element_type=f32) for h in hs]
        for c in range(ROW_CHUNKS):
            h_ref[c * rc:(c + 1) * rc, :] = hs[c]
            y_ref[c * rc:(c + 1) * rc, :] = ys[c].astype(bf16)

    vec = pl.BlockSpec((1, d), lambda i: (0, 0))
    return _pcall(
        body, name=name, grid=(s // tm,),
        in_specs=[pl.BlockSpec((tm, d), lambda i: (i, 0)), vec, vec, vec, pl.BlockSpec((d, n), lambda i: (0, 0))],
        out_specs=[pl.BlockSpec((tm, d), lambda i: (i, 0)), pl.BlockSpec((tm, n), lambda i: (i, 0))],
        out_shape=[jax.ShapeDtypeStruct((s, d), bf16), jax.ShapeDtypeStruct((s, n), bf16)],
        compiler_params=_params("parallel"),
    )(x, nw, sc, sh, w)


def mm_resid(a, w, xres, g, *, name):
    s, k = a.shape
    n = w.shape[1]
    tm = _row_tile(s)
    rc = tm // ROW_CHUNKS

    def body(a_ref, w_ref, x_ref, g_ref, o_ref):
        ys = [jnp.dot(a_ref[c * rc:(c + 1) * rc, :], w_ref[...], preferred_element_type=f32) for c in range(ROW_CHUNKS)]
        for c, y in enumerate(ys):
            o_ref[c * rc:(c + 1) * rc, :] = x_ref[c * rc:(c + 1) * rc, :] + g_ref[...] * y

    big = pl.BlockSpec((tm, n), lambda i: (i, 0))
    return _pcall(
        body, name=name, grid=(s // tm,),
        in_specs=[pl.BlockSpec((tm, k), lambda i: (i, 0)), pl.BlockSpec((k, n), lambda i: (0, 0)), big,
                  pl.BlockSpec((1, n), lambda i: (0, 0))],
        out_specs=big, out_shape=jax.ShapeDtypeStruct((s, n), f32), compiler_params=_params("parallel"),
    )(a, w, xres, g)


def mm_nt_delta(dx, g, w, o, lse, sinkrow, *, name):
    s, d = dx.shape
    n = w.shape[0]
    wd = lse.shape[1]
    tm = _row_tile(s)
    rc = tm // ROW_CHUNKS
    ind = jnp.asarray(_head_indicator(n // DH), dtype=bf16)

    def body(dx_ref, g_ref, w_ref, o_ref, lse_ref, sink_ref, e_ref, do_ref, dl_ref, ds_ref):
        rows = [slice(c * rc, (c + 1) * rc) for c in range(ROW_CHUNKS)]
        as_ = [(dx_ref[rw, :] * g_ref[...]).astype(bf16) for rw in rows]
        dos = [lax.dot_general(a, w_ref[...], (((1,), (1,)), ((), ())), preferred_element_type=f32).astype(bf16) for a in as_]
        dls = [_dot_split(do.astype(f32) * o_ref[rw, :].astype(f32), e_ref[...]) for do, rw in zip(dos, rows)]
        part = None
        for rw, do, dl in zip(rows, dos, dls):
            do_ref[rw, :] = do
            dl_ref[rw, :] = dl
            p = -jnp.sum(jnp.exp(sink_ref[...] - lse_ref[rw, :]) * dl, axis=0, keepdims=True)
            part = p if part is None else part + p
        part = jnp.concatenate([part, jnp.zeros((7, wd), f32)], axis=0)

        @pl.when(pl.program_id(0) == 0)
        def _():
            ds_ref[...] = part

        @pl.when(pl.program_id(0) != 0)
        def _():
            ds_ref[...] += part

    return _pcall(
        body, name=name, grid=(s // tm,),
        in_specs=[pl.BlockSpec((tm, d), lambda i: (i, 0)), pl.BlockSpec((1, d), lambda i: (0, 0)),
                  pl.BlockSpec((n, d), lambda i: (0, 0)), pl.BlockSpec((tm, n), lambda i: (i, 0)),
                  pl.BlockSpec((tm, wd), lambda i: (i, 0)), pl.BlockSpec((1, wd), lambda i: (0, 0)),
                  pl.BlockSpec((n, wd), lambda i: (0, 0))],
        out_specs=[pl.BlockSpec((tm, n), lambda i: (i, 0)), pl.BlockSpec((tm, wd), lambda i: (i, 0)),
                   pl.BlockSpec((8, wd), lambda i: (0, 0))],
        out_shape=[jax.ShapeDtypeStruct((s, n), bf16), jax.ShapeDtypeStruct((s, wd), f32), jax.ShapeDtypeStruct((8, wd), f32)],
        compiler_params=_params("arbitrary"),
    )(dx, g, w, o, lse, sinkrow, ind)


def mm_nt_norm_bwd(a, w, x, dres, nw, sc, *, name):
    pieces = list(a) if isinstance(a, (list, tuple)) else [a]
    npc = len(pieces)
    s = pieces[0].shape[0]
    k = sum(p.shape[1] for p in pieces)
    d = w.shape[0]
    tm = _row_tile(s)
    rc = tm // ROW_CHUNKS

    def body(*refs):
        a_refs = refs[:npc]
        w_ref, x_ref, dr_ref, nw_ref, sc_ref, o_ref, st_ref = refs[npc:]

        def a_rows(c):
            got = [r[c * rc:(c + 1) * rc, :] for r in a_refs]
            return jnp.concatenate(got, axis=1) if npc > 1 else got[0]

        dhs = [lax.dot_general(a_rows(c), w_ref[...], (((1,), (1,)), ((), ())), preferred_element_type=f32)
               for c in range(ROW_CHUNKS)]
        rows = None
        for c, dh in enumerate(dhs):
            xv = x_ref[c * rc:(c + 1) * rc, :]
            r = lax.rsqrt(jnp.mean(xv * xv, axis=-1, keepdims=True) + RMS_EPS)
            xh = xv * r
            dn = dh * (1.0 + sc_ref[...])
            dxh = dn * nw_ref[...]
            o_ref[c * rc:(c + 1) * rc, :] = dr_ref[c * rc:(c + 1) * rc, :] + r * (dxh - xh * jnp.mean(dxh * xh, axis=-1, keepdims=True))
            part = jnp.concatenate([
                jnp.sum(dn * xh, axis=0, keepdims=True),
                jnp.sum(dh * (xh * nw_ref[...]), axis=0, keepdims=True),
                jnp.sum(dh, axis=0, keepdims=True),
                jnp.zeros((5, d), f32)], axis=0)
            rows = part if rows is None else rows + part

        @pl.when(pl.program_id(0) == 0)
        def _():
            st_ref[...] = rows

        @pl.when(pl.program_id(0) != 0)
        def _():
            st_ref[...] += rows

    big = pl.BlockSpec((tm, d), lambda i: (i, 0))
    vec = pl.BlockSpec((1, d), lambda i: (0, 0))
    return _pcall(
        body, name=name, grid=(s // tm,),
        in_specs=[pl.BlockSpec((tm, p.shape[1]), lambda i: (i, 0)) for p in pieces]
        + [pl.BlockSpec((d, k), lambda i: (0, 0)), big, big, vec, vec],
        out_specs=[big, pl.BlockSpec((8, d), lambda i: (0, 0))],
        out_shape=[jax.ShapeDtypeStruct((s, d), f32), jax.ShapeDtypeStruct((8, d), f32)],
        compiler_params=_params("arbitrary"),
    )(*pieces, w, x, dres, nw, sc)


def mm_tn(a, b, scale=None, *, name):
    pieces = list(b) if isinstance(b, (list, tuple)) else [b]
    s, ka = a.shape
    nb = sum(p.shape[1] for p in pieces)
    npc = len(pieces)
    ts = _row_tile(s)
    ns = s // ts

    def body(a_ref, *rest):
        b_refs, rest = rest[:npc], rest[npc:]
        o_ref = rest[2] if scale is not None else rest[0]
        si = pl.program_id(0)
        bv = jnp.concatenate([r[...].astype(bf16) for r in b_refs], axis=1) if npc > 1 else b_refs[0][...].astype(bf16)
        part = lax.dot_general(a_ref[...], bv, (((0,), (0,)), ((), ())), preferred_element_type=f32)

        @pl.when(si == 0)
        def _():
            o_ref[...] = part

        @pl.when(si != 0)
        def _():
            o_ref[...] += part

        if scale is not None:
            g_ref, wb_ref, dg_ref = rest[0], rest[1], rest[3]

            @pl.when(si == ns - 1)
            def _():
                gm = o_ref[...]
                dg_ref[...] = jnp.sum(wb_ref[...].astype(f32) * gm, axis=0, keepdims=True)
                o_ref[...] = gm * g_ref[...]

    in_specs = [pl.BlockSpec((ts, ka), lambda k: (k, 0))] + [pl.BlockSpec((ts, p.shape[1]), lambda k: (k, 0)) for p in pieces]
    args = [a] + pieces
    whole = pl.BlockSpec((ka, nb), lambda k: (0, 0))
    out_specs = [whole]
    out_shape = [jax.ShapeDtypeStruct((ka, nb), f32)]
    if scale is not None:
        in_specs += [pl.BlockSpec((1, nb), lambda k: (0, 0)), whole]
        args += list(scale)
        out_specs.append(pl.BlockSpec((1, nb), lambda k: (0, 0)))
        out_shape.append(jax.ShapeDtypeStruct((1, nb), f32))
    res = _pcall(body, name=name, grid=(ns,), in_specs=in_specs, out_specs=out_specs, out_shape=out_shape,
                 compiler_params=_params("arbitrary"))(*args)
    return res if scale is not None else res[0]


def loss_head(x, fn, tgt, *, name):
    s, d = x.shape
    tm = _row_tile(s, 512)

    def body(x_ref, fn_ref, t_ref, dx_ref, st_ref):
        xv = x_ref[...]
        r = lax.rsqrt(jnp.mean(xv * xv, axis=-1, keepdims=True) + RMS_EPS)
        xh = xv * r
        err = xh * fn_ref[...] - t_ref[...]
        dy = err / float(d)
        dxh = dy * fn_ref[...]
        dx_ref[...] = r * (dxh - xh * jnp.mean(dxh * xh, axis=-1, keepdims=True))
        rows = jnp.concatenate([
            jnp.sum(dy * xh, axis=0, keepdims=True),
            jnp.sum(err * err, axis=0, keepdims=True),
            jnp.zeros((6, d), f32)], axis=0)

        @pl.when(pl.program_id(0) == 0)
        def _():
            st_ref[...] = rows

        @pl.when(pl.program_id(0) != 0)
        def _():
            st_ref[...] += rows

    big = pl.BlockSpec((tm, d), lambda i: (i, 0))
    return _pcall(
        body, name=name, grid=(s // tm,), in_specs=[big, pl.BlockSpec((1, d), lambda i: (0, 0)), big],
        out_specs=[big, pl.BlockSpec((8, d), lambda i: (0, 0))],
        out_shape=[jax.ShapeDtypeStruct((s, d), f32), jax.ShapeDtypeStruct((8, d), f32)],
        compiler_params=_params("arbitrary"),
    )(x, fn, tgt)


FC = 2 * F // 4
FFN_ROWS = 256


def _resident(pairs, sems):
    @pl.when(pl.program_id(0) == 0)
    def _():
        cps = [pltpu.make_async_copy(h, v, sems.at[i]) for i, (h, v) in enumerate(pairs)]
        for cp in cps:
            cp.start()
        for cp in cps:
            cp.wait()


def ffn_fwd(x, nw, sc, sh, g, w_in, w_out, carry=(), *, name):
    s, d = x.shape
    tm = _row_tile(s, 2 * FFN_ROWS)
    nsteps = s // tm
    nc = len(carry)

    def body(*refs):
        x_ref, nw_ref, sc_ref, sh_ref, g_ref, win_hbm, wout_hbm = refs[:7]
        h_ref, gu_ref, a_ref, o_ref = refs[7 + nc:11 + nc]
        win_v, wout_v, sems = refs[11 + 2 * nc:14 + 2 * nc]
        if nc:
            start, finish = _gather_direct(refs[7:7 + nc], refs[11 + nc:11 + 2 * nc], *refs[14 + 2 * nc:])
            pl.when(pl.program_id(0) == 0)(start)
        _resident([(win_hbm, win_v), (wout_hbm, wout_v)], sems)
        xv = x_ref[...]
        r = lax.rsqrt(jnp.mean(xv * xv, axis=-1, keepdims=True) + RMS_EPS)
        h = ((xv * r * nw_ref[...]) * (1.0 + sc_ref[...]) + sh_ref[...]).astype(bf16)
        h_ref[...] = h
        halves = [slice(c * FC, (c + 1) * FC) for c in range(2)]
        gts = [jnp.dot(h, win_v[c], preferred_element_type=f32) for c in range(2)]
        ups = [jnp.dot(h, win_v[c + 2], preferred_element_type=f32) for c in range(2)]
        acts = [(gt * jax.nn.sigmoid(gt) * up).astype(bf16) for gt, up in zip(gts, ups)]
        ys = [jnp.dot(act, wout_v[cs, :], preferred_element_type=f32) for act, cs in zip(acts, halves)]
        for cs, gt, up, act in zip(halves, gts, ups, acts):
            gu_ref[0, :, cs] = gt.astype(bf16)
            gu_ref[1, :, cs] = up.astype(bf16)
            a_ref[:, cs] = act
        o_ref[...] = xv + g_ref[...] * (ys[0] + ys[1])
        if nc:
            pl.when(pl.program_id(0) == nsteps - 1)(finish)

    big = pl.BlockSpec((tm, d), lambda i: (i, 0))
    vec = pl.BlockSpec((1, d), lambda i: (0, 0))
    return _pcall(
        body, name=name, grid=(nsteps,), in_specs=[big, vec, vec, vec, vec, ANY, ANY] + [ANY] * nc,
        out_specs=[big, pl.BlockSpec((2, tm, F), lambda i: (0, i, 0)), pl.BlockSpec((tm, F), lambda i: (i, 0)), big] + [ANY] * nc,
        out_shape=[jax.ShapeDtypeStruct((s, d), bf16), jax.ShapeDtypeStruct((2, s, F), bf16),
                   jax.ShapeDtypeStruct((s, F), bf16), jax.ShapeDtypeStruct((s, d), f32)]
        + [jax.ShapeDtypeStruct((4,) + tuple(sh_.shape), sh_.dtype) for sh_ in carry],
        scratch_shapes=[pltpu.VMEM((4, d, FC), bf16), pltpu.VMEM((F, d), bf16), pltpu.SemaphoreType.DMA((2,))]
        + (_gather_scratch(nc) if nc else []),
        compiler_params=_params("arbitrary"),
    )(x, nw, sc, sh, g, w_in, w_out, *carry)


def ffn_bwd_rows(dx, x, gu, g, nw, sc, w_in, w_out, carry=(), *, name):
    s, d = x.shape
    tm = _row_tile(s, FFN_ROWS)
    nsteps = s // tm
    nc = len(carry)
    nt_dims = (((1,), (1,)), ((), ()))

    def body(*refs):
        dx_ref, x_ref, gu_ref, g_ref, nw_ref, sc_ref, win_hbm, wout_hbm = refs[:8]
        dgu_ref, o_ref, st_ref = refs[8 + nc:11 + nc]
        win_v, wout_v, sems = refs[11 + 3 * nc:14 + 3 * nc]
        if nc:
            start, forward, finish = _exchange(refs[8:8 + nc], refs[11 + nc:11 + 2 * nc], refs[11 + 2 * nc:11 + 3 * nc],
                                               *refs[14 + 3 * nc:])
            pl.when(pl.program_id(0) == 0)(start)
            pl.when(pl.program_id(0) == max(nsteps - 1 - CARRY_LEAD, 0))(forward)
        _resident([(win_hbm, win_v), (wout_hbm, wout_v)], sems)
        dxv = dx_ref[...]
        a = (dxv * g_ref[...]).astype(bf16)
        halves = [slice(c * FC, (c + 1) * FC) for c in range(2)]
        das = [lax.dot_general(a, wout_v[cs, :], nt_dims, preferred_element_type=f32) for cs in halves]
        gts = [gu_ref[0, :, cs].astype(f32) for cs in halves]
        ups = [gu_ref[1, :, cs].astype(f32) for cs in halves]
        sgs = [jax.nn.sigmoid(gt) for gt in gts]
        dgates = [(da * up * (sg * (1.0 + gt * (1.0 - sg)))).astype(bf16) for da, gt, up, sg in zip(das, gts, ups, sgs)]
        dups = [(da * (gt * sg)).astype(bf16) for da, gt, sg in zip(das, gts, sgs)]
        for cs, dgate, dup in zip(halves, dgates, dups):
            dgu_ref[0, :, cs] = dgate
            dgu_ref[1, :, cs] = dup
        parts = [lax.dot_general(dgates[c], win_v[c], nt_dims, preferred_element_type=f32) for c in range(2)]
        parts += [lax.dot_general(dups[c], win_v[c + 2], nt_dims, preferred_element_type=f32) for c in range(2)]
        dh = (parts[0] + parts[1]) + (parts[2] + parts[3])
        xv = x_ref[...]
        r = lax.rsqrt(jnp.mean(xv * xv, axis=-1, keepdims=True) + RMS_EPS)
        xh = xv * r
        dn = dh * (1.0 + sc_ref[...])
        dxh = dn * nw_ref[...]
        o_ref[...] = dxv + r * (dxh - xh * jnp.mean(dxh * xh, axis=-1, keepdims=True))
        rows = jnp.concatenate([
            jnp.sum(dn * xh, axis=0, keepdims=True),
            jnp.sum(dh * (xh * nw_ref[...]), axis=0, keepdims=True),
            jnp.sum(dh, axis=0, keepdims=True),
            jnp.zeros((5, d), f32)], axis=0)

        @pl.when(pl.program_id(0) == 0)
        def _():
            st_ref[...] = rows

        @pl.when(pl.program_id(0) != 0)
        def _():
            st_ref[...] += rows

        if nc:
            pl.when(pl.program_id(0) == nsteps - 1)(finish)

    big = pl.BlockSpec((tm, d), lambda i: (i, 0))
    vec = pl.BlockSpec((1, d), lambda i: (0, 0))
    gus = pl.BlockSpec((2, tm, F), lambda i: (0, i, 0))
    cshapes = [jax.ShapeDtypeStruct(p.shape, p.dtype) for p in carry]
    res = _pcall(
        body, name=name, grid=(nsteps,), in_specs=[big, big, gus, vec, vec, vec, ANY, ANY] + [ANY] * nc,
        out_specs=[gus, big, pl.BlockSpec((8, d), lambda i: (0, 0))] + [ANY] * (2 * nc),
        out_shape=[jax.ShapeDtypeStruct((2, s, F), bf16), jax.ShapeDtypeStruct((s, d), f32), jax.ShapeDtypeStruct((8, d), f32)]
        + cshapes + cshapes,
        scratch_shapes=[pltpu.VMEM((4, d, FC), bf16), pltpu.VMEM((F, d), bf16), pltpu.SemaphoreType.DMA((2,))]
        + (_exchange_scratch(nc) if nc else []),
        compiler_params=_params("arbitrary"),
    )(dx, x, gu, g, nw, sc, w_in, w_out, *carry)
    return res[0], res[1], res[2], res[3:3 + nc], res[3 + nc:]


def ffn_dw_in(h, dgu, *, name):
    s, d = h.shape
    ts = _row_tile(s)
    ns = s // ts
    tn_dims = (((0,), (0,)), ((), ()))

    def body(h_ref, dgu_ref, o_ref, acc):
        k = pl.program_id(1)

        @pl.when(k == 0)
        def _():
            acc[...] = jnp.zeros_like(acc)

        hv = h_ref[...]
        for c in range(2):
            acc[c] += lax.dot_general(hv, dgu_ref[:, c * FC:(c + 1) * FC], tn_dims, preferred_element_type=f32)

        @pl.when(k == ns - 1)
        def _():
            o_ref[...] = acc[...].astype(bf16)

    return _pcall(
        body, name=name, grid=(2, ns),
        in_specs=[pl.BlockSpec((ts, d), lambda hf, k: (k, 0)), pl.BlockSpec((None, ts, F), lambda hf, k: (hf, k, 0))],
        out_specs=pl.BlockSpec((2, d, FC), lambda hf, k: (hf, 0, 0)),
        out_shape=jax.ShapeDtypeStruct((4, d, FC), bf16), scratch_shapes=[pltpu.VMEM((2, d, FC), f32)],
        compiler_params=_params("arbitrary", "arbitrary"),
    )(h, dgu)


def ffn_dw_out(a, dx, g, wb, *, name):
    s, fdim = a.shape
    d = dx.shape[1]
    ts = _row_tile(s)
    ns = s // ts
    tn = d // 2
    tn_dims = (((0,), (0,)), ((), ()))

    def body(a_ref, dx_ref, g_ref, wb_ref, o_ref, dg_ref, acc):
        k = pl.program_id(1)

        @pl.when(k == 0)
        def _():
            acc[...] = jnp.zeros_like(acc)

        acc[...] += lax.dot_general(a_ref[...], dx_ref[...].astype(bf16), tn_dims, preferred_element_type=f32)

        @pl.when(k == ns - 1)
        def _():
            gm = acc[...]
            dg_ref[...] = jnp.concatenate([jnp.sum(wb_ref[...].astype(f32) * gm, axis=0, keepdims=True),
                                           jnp.zeros((7, tn), f32)], axis=0)
            o_ref[...] = (gm * g_ref[...]).astype(bf16)

    return _pcall(
        body, name=name, grid=(2, ns),
        in_specs=[pl.BlockSpec((ts, fdim), lambda j, k: (k, 0)), pl.BlockSpec((ts, tn), lambda j, k: (k, j)),
                  pl.BlockSpec((1, tn), lambda j, k: (0, j)), pl.BlockSpec((fdim, tn), lambda j, k: (0, j))],
        out_specs=[pl.BlockSpec((fdim, tn), lambda j, k: (0, j)), pl.BlockSpec((8, tn), lambda j, k: (0, j))],
        out_shape=[jax.ShapeDtypeStruct((fdim, d), bf16), jax.ShapeDtypeStruct((8, d), f32)],
        scratch_shapes=[pltpu.VMEM((fdim, tn), f32)], compiler_params=_params("arbitrary", "arbitrary"),
    )(a, dx, g, wb)


def _alibi(n):
    return np.asarray(2.0 ** (-8.0 * np.arange(1, n + 1) / n), dtype=np.float32)


class _Attn:
    def __init__(self, s, *, mixer, group=0):
        if mixer == "a":
            self.blk, self.dil, self.npairs = 128, 1, 2
            self.qb0, self.kb0, self.vb0 = 0, 8, 10
            slopes = _alibi(16).reshape(2, 2, GQ)
        else:
            window, dil = B_GROUPS[group]
            self.blk, self.dil, self.npairs = window // (2 * dil), dil, 1
            self.qb0, self.kb0, self.vb0 = (0, 12, 15) if dil == 1 else (0, 4, 5)
            slopes = _alibi(24).reshape(3, 1, 2, GQ)[group]
        self.l = s // self.dil
        self.t = min(1024, self.l)
        self.nt = self.l // self.t
        self.nb = self.t // self.blk
        blk = self.blk
        qi = np.arange(blk)[:, None]
        rel = np.arange(3 * blk)[None, :] - blk - qi
        dist = (self.dil * np.abs(rel)).astype(np.float32)
        bias = -slopes[:, :, :, None, None] * dist[None, None, None]
        bias = np.where(np.abs(rel) <= blk, bias, np.float32(NEG)).astype(np.float32)
        self.bias = np.ascontiguousarray(np.swapaxes(bias.reshape(self.npairs, 2, GQ * blk, 3 * blk), -1, -2))

    def grid(self):
        return (self.dil, self.npairs, self.nt)

    def tile(self, width, col):
        return pl.BlockSpec((None, self.t, width), lambda r, hp, i: (r, i, col(hp)))

    def halo(self, width, col):
        t, blk, nbl = self.t, self.blk, self.l // self.blk
        per = t // blk
        return [
            pl.BlockSpec((None, blk, width), lambda r, hp, i: (r, jnp.maximum(i * per - 1, 0), col(hp))),
            self.tile(width, col),
            pl.BlockSpec((None, blk, width), lambda r, hp, i: (r, jnp.minimum((i + 1) * per, nbl - 1), col(hp))),
        ]

    def qcol(self, e):
        return lambda hp: self.qb0 + 2 * hp + e

    def kcol(self, hp):
        return self.kb0 + hp

    def vcol(self, hp):
        return self.vb0 + hp

    def pcol(self, hp):
        return hp


def _stack_heads(x):
    return jnp.concatenate([x[:, g * DH:(g + 1) * DH] for g in range(GQ)], axis=0)


def _unstack_heads(x, rows):
    return jnp.concatenate([x[g * rows:(g + 1) * rows] for g in range(GQ)], axis=1)


def _carrying(body, n_in, n_out, n_scratch, carry, kind, grid):
    nc = len(carry)
    if not nc:
        return body, [], [], [], []
    n_res = nc if kind == "gather" else 2 * nc

    def wrapped(*refs):
        ins, src = refs[:n_in], refs[n_in:n_in + nc]
        outs = refs[n_in + nc:n_in + nc + n_out]
        res = refs[n_in + nc + n_out:n_in + nc + n_out + n_res]
        scr = refs[n_in + nc + n_out + n_res:n_in + nc + n_out + n_res + n_scratch]
        sems = refs[n_in + nc + n_out + n_res + n_scratch:]
        ids = [pl.program_id(a) for a in range(len(grid))]
        first = functools.reduce(jnp.logical_and, [i == 0 for i in ids])
        last = functools.reduce(jnp.logical_and, [i == g - 1 for i, g in zip(ids, grid)])
        if kind == "gather":
            start, finish = _gather_direct(src, res, *sems)
            pl.when(first)(start)
        else:
            start, forward, finish = _exchange(src, res[:nc], res[nc:], *sems)
            pl.when(first)(start)
            early = [g - 1 for g in grid[:-1]] + [max(grid[-1] - 1 - CARRY_LEAD, 0)]
            pl.when(functools.reduce(jnp.logical_and, [i == g for i, g in zip(ids, early)]))(forward)
        body(*ins, *outs, *scr)
        pl.when(last)(finish)

    if kind == "gather":
        shapes = [jax.ShapeDtypeStruct((4,) + tuple(c.shape), c.dtype) for c in carry]
        sems = _gather_scratch(nc)
    else:
        shapes = [jax.ShapeDtypeStruct(c.shape, c.dtype) for c in carry] * 2
        sems = _exchange_scratch(nc)
    return wrapped, [ANY] * nc, [ANY] * n_res, shapes, sems


def attn_fwd(qkv, sinkcol, cfg, carry=(), *, out_dtype, name):
    blk, t, nb, nt, dil = cfg.blk, cfg.t, cfg.nb, cfg.nt, cfg.dil
    scale = DH ** -0.5

    def body(q0, q1, kp, km, kn, vp, vm, vn, bias_ref, sink_ref, o_ref, lse_ref, kx, vx):
        ti = pl.program_id(2)
        first, last = ti == 0, ti == nt - 1
        for hh in range(2):
            sl = slice(hh * DH, (hh + 1) * DH)
            for dst, (p_, m_, n_) in ((kx, (kp, km, kn)), (vx, (vp, vm, vn))):
                dst[hh, 0:blk] = p_[:, sl]
                dst[hh, blk:blk + t] = m_[:, sl]
                dst[hh, blk + t:] = n_[:, sl]
        krow = lax.broadcasted_iota(jnp.int32, (3 * blk, GQ * blk), 0)
        pairs = [(b, hh) for b in range(nb) for hh in range(2)]
        qs = [_stack_heads((q0, q1)[hh][b * blk:(b + 1) * blk, :]) * scale for b, hh in pairs]
        sc = [lax.dot_general(kx[hh, b * blk:(b + 3) * blk, :], q_, (((1,), (1,)), ((), ())), preferred_element_type=f32)
              for q_, (b, hh) in zip(qs, pairs)]
        sc = [s_ + bias_ref[0, hh] for s_, (b, hh) in zip(sc, pairs)]
        sc = [jnp.where(jnp.logical_and(first, krow < blk), NEG, s_) if b == 0 else s_ for s_, (b, hh) in zip(sc, pairs)]
        sc = [jnp.where(jnp.logical_and(last, krow >= 2 * blk), NEG, s_) if b == nb - 1 else s_ for s_, (b, hh) in zip(sc, pairs)]
        ms = [jnp.maximum(jnp.max(s_, axis=0, keepdims=True), sink_ref[0, hh]) for s_, (b, hh) in zip(sc, pairs)]
        ps = [jnp.exp(s_ - m_) for s_, m_ in zip(sc, ms)]
        ls = [jnp.sum(p_, axis=0, keepdims=True) + jnp.exp(sink_ref[0, hh] - m_) for p_, m_, (b, hh) in zip(ps, ms, pairs)]
        os_ = [lax.dot_general(vx[hh, b * blk:(b + 3) * blk, :], p_.astype(bf16), (((0,), (0,)), ((), ())),
                               preferred_element_type=f32) for p_, (b, hh) in zip(ps, pairs)]
        os_ = [o_ / l_ for o_, l_ in zip(os_, ls)]
        lses = [m_ + jnp.log(l_) for m_, l_ in zip(ms, ls)]
        for o_, (b, hh) in zip(os_, pairs):
            o_ref[b * blk:(b + 1) * blk, hh * 256:(hh + 1) * 256] = _unstack_heads(o_.T, blk).astype(out_dtype)
        stat_rows = [jnp.concatenate([lses[2 * b + hh][:, g * blk:(g + 1) * blk] for b in range(nb)], axis=1)
                     for hh in range(2) for g in range(GQ)]
        lse_ref[...] = jnp.concatenate(stat_rows + [jnp.zeros((128 - 2 * GQ, t), f32)], axis=0).T

    in_specs = [cfg.tile(256, cfg.qcol(e)) for e in range(2)]
    in_specs += cfg.halo(128, cfg.kcol) + cfg.halo(128, cfg.vcol)
    in_specs += [pl.BlockSpec((1, 2, 3 * blk, GQ * blk), lambda r, hp, i: (hp, 0, 0, 0)),
                 pl.BlockSpec((1, 2, 1, GQ * blk), lambda r, hp, i: (hp, 0, 0, 0))]
    body, c_in, c_out, c_shape, c_sems = _carrying(body, 10, 2, 2, carry, "gather", cfg.grid())
    return _pcall(
        body, name=name, grid=cfg.grid(), in_specs=in_specs + c_in,
        out_specs=[cfg.tile(512, cfg.pcol), cfg.tile(128, cfg.pcol)] + c_out,
        out_shape=[jax.ShapeDtypeStruct((dil, cfg.l, cfg.npairs * 512), out_dtype),
                   jax.ShapeDtypeStruct((dil, cfg.l, cfg.npairs * 128), f32)] + c_shape,
        scratch_shapes=[pltpu.VMEM((2, t + 2 * blk, DH), bf16), pltpu.VMEM((2, t + 2 * blk, DH), bf16)] + c_sems,
        compiler_params=_params("arbitrary", "arbitrary", "arbitrary"),
    )(*([qkv] * 8), jnp.asarray(cfg.bias), sinkcol, *carry)


def attn_bwd(qkv, do, lse, delta, cfg, carry=(), *, name):
    blk, t, nb, nt, dil, npairs = cfg.blk, cfg.t, cfg.nb, cfg.nt, cfg.dil, cfg.npairs
    scale = DH ** -0.5
    nt_dims = (((1,), (1,)), ((), ()))
    tn_dims = (((0,), (0,)), ((), ()))

    def body(q0p, q0m, q0n, q1p, q1m, q1n, kp, km, kn, vp, vm, vn, dop, dom, don, lp, lm, ln, dp_, dm_, dn_,
             bias_ref, dq_ref, dk_ref, dv_ref, kx, vx, dkx, dvx):
        ti = pl.program_id(2)
        first, last = ti == 0, ti == nt - 1
        for hh in range(2):
            sl = slice(hh * DH, (hh + 1) * DH)
            for dst, (p_, m_, n_) in ((kx, (kp, km, kn)), (vx, (vp, vm, vn))):
                dst[hh, 0:blk] = p_[:, sl]
                dst[hh, blk:blk + t] = m_[:, sl]
                dst[hh, blk + t:] = n_[:, sl]
        dkx[...] = jnp.zeros_like(dkx)
        dvx[...] = jnp.zeros_like(dvx)

        def slab(prev, main, nxt, e):
            if e == 0:
                return prev[...]
            if e == nb + 1:
                return nxt[...]
            return main[(e - 1) * blk:e * blk, :]

        krow = lax.broadcasted_iota(jnp.int32, (3 * blk, GQ * blk), 0)

        def keys(e):
            if e == 0:
                return 1, 2, slice(2 * blk, 3 * blk)
            if e == nb + 1:
                return nb, nb + 1, slice(0, blk)
            return e - 1, e + 2, slice(0, 3 * blk)

        def edge(sc, e):
            if e == 0:
                return jnp.where(first, NEG, sc)
            if e == nb + 1:
                return jnp.where(last, NEG, sc)
            if e == 1:
                sc = jnp.where(jnp.logical_and(first, krow < blk), NEG, sc)
            if e == nb:
                sc = jnp.where(jnp.logical_and(last, krow >= 2 * blk), NEG, sc)
            return sc

        lse_t = [lp[...].T, lm[...].T, ln[...].T]
        dl_t = [dp_[...].T, dm_[...].T, dn_[...].T]

        def stat_row(parts, e, hh):
            src, lo = (parts[0], 0) if e == 0 else (parts[2], 0) if e == nb + 1 else (parts[1], (e - 1) * blk)
            return jnp.concatenate([src[hh * GQ + g:hh * GQ + g + 1, lo:lo + blk] for g in range(GQ)], axis=1)

        pairs = [(e, hh) for e in range(nb + 2) for hh in range(2)]
        qs = [_stack_heads(slab(*((q0p, q0m, q0n), (q1p, q1m, q1n))[hh], e)) * scale for e, hh in pairs]
        dos = [_stack_heads(slab(dop, dom, don, e)[:, hh * 256:(hh + 1) * 256]) for e, hh in pairs]
        lse_r = [stat_row(lse_t, e, hh) for e, hh in pairs]
        dl_r = [stat_row(dl_t, e, hh) for e, hh in pairs]
        kw = [kx[hh, keys(e)[0] * blk:keys(e)[1] * blk, :] for e, hh in pairs]
        vw = [vx[hh, keys(e)[0] * blk:keys(e)[1] * blk, :] for e, hh in pairs]
        sc = [lax.dot_general(k_, q_, nt_dims, preferred_element_type=f32) for q_, k_ in zip(qs, kw)]
        dp = [lax.dot_general(v_, d_, nt_dims, preferred_element_type=f32) for d_, v_ in zip(dos, vw)]
        sc = [edge(s_ + bias_ref[0, hh, keys(e)[2], :], e) for s_, (e, hh) in zip(sc, pairs)]
        ps = [jnp.exp(s_ - l_) for s_, l_ in zip(sc, lse_r)]
        ds = [(p_ * (d_ - c_)).astype(bf16) for p_, d_, c_ in zip(ps, dp, dl_r)]
        pb = [p_.astype(bf16) for p_ in ps]
        dks = [jnp.dot(s_, q_, preferred_element_type=f32) for s_, q_ in zip(ds, qs)]
        dvs = [jnp.dot(p_, d_, preferred_element_type=f32) for p_, d_ in zip(pb, dos)]
        dqs = [lax.dot_general(s_, k_, tn_dims, preferred_element_type=f32) if 1 <= e <= nb else None
               for s_, k_, (e, hh) in zip(ds, kw, pairs)]
        for dk_, dv_, dq_, (e, hh) in zip(dks, dvs, dqs, pairs):
            k0, k1, _ = keys(e)
            dkx[hh, k0 * blk:k1 * blk, :] += dk_
            dvx[hh, k0 * blk:k1 * blk, :] += dv_
            if dq_ is not None:
                dq_ref[(e - 1) * blk:e * blk, hh * 256:(hh + 1) * 256] = (_unstack_heads(dq_, blk) * scale).astype(bf16)
        for hh in range(2):
            dk_ref[:, hh * DH:(hh + 1) * DH] = dkx[hh, blk:blk + t, :].astype(bf16)
            dv_ref[:, hh * DH:(hh + 1) * DH] = dvx[hh, blk:blk + t, :].astype(bf16)

    in_specs = cfg.halo(256, cfg.qcol(0)) + cfg.halo(256, cfg.qcol(1))
    in_specs += cfg.halo(128, cfg.kcol) + cfg.halo(128, cfg.vcol)
    in_specs += cfg.halo(512, cfg.pcol) + cfg.halo(128, cfg.pcol) + cfg.halo(128, cfg.pcol)
    in_specs += [pl.BlockSpec((1, 2, 3 * blk, GQ * blk), lambda r, hp, i: (hp, 0, 0, 0))]
    body, c_in, c_out, c_shape, c_sems = _carrying(body, 22, 3, 4, carry, "exchange", cfg.grid())
    res = _pcall(
        body, name=name, grid=cfg.grid(), in_specs=in_specs + c_in,
        out_specs=[cfg.tile(512, cfg.pcol), cfg.tile(128, cfg.pcol), cfg.tile(128, cfg.pcol)] + c_out,
        out_shape=[jax.ShapeDtypeStruct((dil, cfg.l, npairs * 512), bf16),
                   jax.ShapeDtypeStruct((dil, cfg.l, npairs * 128), bf16),
                   jax.ShapeDtypeStruct((dil, cfg.l, npairs * 128), bf16)] + c_shape,
        scratch_shapes=[pltpu.VMEM((2, t + 2 * blk, DH), bf16), pltpu.VMEM((2, t + 2 * blk, DH), bf16),
                        pltpu.VMEM((2, t + 2 * blk, DH), f32), pltpu.VMEM((2, t + 2 * blk, DH), f32)] + c_sems,
        compiler_params=_params("arbitrary", "arbitrary", "arbitrary"),
    )(*([qkv] * 12), do, do, do, lse, lse, lse, delta, delta, delta, jnp.asarray(cfg.bias), *carry)
    nc = len(carry)
    return (res[0], res[1], res[2], res[3:3 + nc], res[3 + nc:]) if nc else res


def _head_indicator(nheads):
    e = np.zeros((nheads * DH, (nheads // 8) * 128), np.float32)
    for c in range(nheads * DH):
        h = c // DH
        e[c, (h // 8) * 128 + h % 8] = 1.0
    return e


def _dot_split(x, e):
    hi = x.astype(bf16)
    lo = (x - hi.astype(f32)).astype(bf16)
    return jnp.dot(hi, e, preferred_element_type=f32) + jnp.dot(lo, e, preferred_element_type=f32)


def _spread(scr, x, d):
    tm, w = x.shape
    for j in range(w // 128):
        scr[j] = x[:, j * 128:(j + 1) * 128]
    return [jnp.concatenate([scr[j, pl.ds(r, tm // d, stride=d), :] for j in range(w // 128)], axis=1) for r in range(d)]


def _weave(scr, blocks, d):
    n, w = blocks[0].shape
    for r in range(d):
        for j in range(w // 128):
            scr[j, pl.ds(r, n, stride=d), :] = blocks[r][:, j * 128:(j + 1) * 128]
    return jnp.concatenate([scr[j] for j in range(w // 128)], axis=1)


def _res_spec(d, tm, w):
    return pl.BlockSpec((d, tm // d, w), lambda i: (0, i, 0))


DILATED = tuple(dil for _, dil in B_GROUPS[1:])


def b_to_strided(qkv, *, name):
    s = qkv.shape[0]
    tm = _row_tile(s)

    def body(x_ref, *rest):
        outs, scr = rest[:-1], rest[-1]
        for gi, (o_ref, d) in enumerate(zip(outs, DILATED), start=1):
            cols = jnp.concatenate([x_ref[:, gi * 512:(gi + 1) * 512], x_ref[:, 1536 + gi * 128:1536 + (gi + 1) * 128],
                                    x_ref[:, 1920 + gi * 128:1920 + (gi + 1) * 128]], axis=1).astype(f32)
            for r, blk_ in enumerate(_spread(scr, cols, d)):
                o_ref[r] = blk_.astype(bf16)

    return _pcall(
        body, name=name, grid=(s // tm,), in_specs=[pl.BlockSpec((tm, B_QKV), lambda i: (i, 0))],
        out_specs=[_res_spec(d, tm, 768) for d in DILATED],
        out_shape=[jax.ShapeDtypeStruct((d, s // d, 768), bf16) for d in DILATED],
        scratch_shapes=[pltpu.VMEM((6, tm, 128), f32)], compiler_params=_params("parallel"),
    )(qkv)


def b_bwd_to_strided(do, lse, delta, *, name):
    s = do.shape[0]
    tm = _row_tile(s)

    def body(do_ref, lse_ref, dl_ref, *rest):
        outs, scr = rest[:-1], rest[-1]
        allc = jnp.concatenate([do_ref[...].astype(f32), lse_ref[...], dl_ref[...]], axis=1)
        for gi, d in enumerate(DILATED):
            o_do, o_lse, o_dl = outs[3 * gi:3 * gi + 3]
            for r, blk_ in enumerate(_spread(scr, allc, d)):
                o_do[r] = blk_[:, :512].astype(bf16)
                o_lse[r] = blk_[:, 512:640]
                o_dl[r] = blk_[:, 640:768]

    out_specs, out_shape = [], []
    for d in DILATED:
        out_specs += [_res_spec(d, tm, 512), _res_spec(d, tm, 128), _res_spec(d, tm, 128)]
        out_shape += [jax.ShapeDtypeStruct((d, s // d, 512), bf16), jax.ShapeDtypeStruct((d, s // d, 128), f32),
                      jax.ShapeDtypeStruct((d, s // d, 128), f32)]
    return _pcall(
        body, name=name, grid=(s // tm,),
        in_specs=[pl.BlockSpec((tm, 512), lambda i: (i, 0)), pl.BlockSpec((tm, 128), lambda i: (i, 0)),
                  pl.BlockSpec((tm, 128), lambda i: (i, 0))],
        out_specs=out_specs, out_shape=out_shape, scratch_shapes=[pltpu.VMEM((6, tm, 128), f32)],
        compiler_params=_params("parallel"),
    )(do, lse, delta)


def b_from_strided(grads, *, name):
    s = grads[0][0].shape[1]
    tm = _row_tile(s)

    def body(*refs):
        ins, o_ref, scr = refs[:9], refs[9], refs[10]
        nat = [jnp.concatenate([ins[q][0].astype(f32) for q in range(3)], axis=1)]
        for gi, d in enumerate(DILATED, start=1):
            blocks = [jnp.concatenate([ins[3 * gi + q][r].astype(f32) for q in range(3)], axis=1) for r in range(d)]
            nat.append(_weave(scr, blocks, d))
        pieces = [nat[g][:, lo:hi] for lo, hi in ((0, 512), (512, 640), (640, 768)) for g in range(3)]
        o_ref[...] = jnp.concatenate(pieces, axis=1).astype(bf16)

    dils = (1,) + DILATED
    in_specs = [_res_spec(d, tm, w) for d in dils for w in (512, 128, 128)]
    return _pcall(
        body, name=name, grid=(s * 1 // tm,), in_specs=in_specs, out_specs=pl.BlockSpec((tm, B_QKV), lambda i: (i, 0)),
        out_shape=jax.ShapeDtypeStruct((s, B_QKV), bf16), scratch_shapes=[pltpu.VMEM((6, tm, 128), f32)],
        compiler_params=_params("parallel"),
    )(*[a for g in grads for a in g])


def attn_merge(os_, lses, *, name):
    s = os_[0].shape[1]
    tm = _row_tile(s)
    ind_t = jnp.asarray(_head_indicator(8).T, dtype=bf16)
    dils = (1,) + DILATED

    def body(o0, o1, o2, l0, l1, l2, e_ref, o_ref, lse_ref, scr):
        both = [jnp.concatenate([o0[0], l0[0]], axis=1)]
        for og, lg, d in ((o1, l1, dils[1]), (o2, l2, dils[2])):
            both.append(_weave(scr, [jnp.concatenate([og[r], lg[r]], axis=1) for r in range(d)], d))
        ls = [b[:, 512:640] for b in both]
        m = jnp.maximum(jnp.maximum(ls[0], ls[1]), ls[2])
        tot = m + jnp.log(jnp.exp(ls[0] - m) + jnp.exp(ls[1] - m) + jnp.exp(ls[2] - m))
        lse_ref[...] = tot
        acc = jnp.zeros((tm, B_OUT), f32)
        for b, lg in zip(both, ls):
            acc = acc + _dot_split(jnp.exp(lg - tot), e_ref[...]) * b[:, :512]
        o_ref[...] = acc.astype(bf16)

    return _pcall(
        body, name=name, grid=(s * 1 // tm,),
        in_specs=[_res_spec(d, tm, 512) for d in dils] + [_res_spec(d, tm, 128) for d in dils]
        + [pl.BlockSpec((128, B_OUT), lambda i: (0, 0))],
        out_specs=[pl.BlockSpec((tm, B_OUT), lambda i: (i, 0)), pl.BlockSpec((tm, 128), lambda i: (i, 0))],
        out_shape=[jax.ShapeDtypeStruct((s, B_OUT), bf16), jax.ShapeDtypeStruct((s, 128), f32)],
        scratch_shapes=[pltpu.VMEM((5, tm, 128), f32)], compiler_params=_params("parallel"),
    )(*os_, *lses, ind_t)


def ada_mod(c_all, w, b, *, name):
    n = w.shape[2]

    def body(c_ref, w_ref, b_ref, o_ref):
        cv = c_ref[...]
        cond = cv * jax.nn.sigmoid(cv)
        o_ref[0] = jnp.dot(cond, w_ref[0], preferred_element_type=f32, precision=lax.Precision.HIGHEST) + b_ref[0]

    return _pcall(
        body, name=name, grid=(DEPTH,),
        in_specs=[pl.BlockSpec((N_DEV, D), lambda i: (0, 0)), pl.BlockSpec((1, D, n), lambda i: (i, 0, 0)),
                  pl.BlockSpec((1, 1, n), lambda i: (i, 0, 0))],
        out_specs=pl.BlockSpec((1, N_DEV, n), lambda i: (i, 0, 0)),
        out_shape=jax.ShapeDtypeStruct((DEPTH, N_DEV, n), f32), compiler_params=_params("arbitrary"),
    )(c_all, w, b)


def ada_grad(c_t, dm, *, name):
    n = dm.shape[2]

    def body(c_ref, dm_ref, o_ref):
        cv = c_ref[...]
        cond = cv * jax.nn.sigmoid(cv)
        acc = cond[:, 0:1] * dm_ref[0, 0:1, :]
        for b in range(1, N_DEV):
            acc = acc + cond[:, b:b + 1] * dm_ref[0, b:b + 1, :]
        o_ref[0] = acc

    return _pcall(
        body, name=name, grid=(DEPTH,),
        in_specs=[pl.BlockSpec((D, N_DEV), lambda i: (0, 0)), pl.BlockSpec((1, N_DEV, n), lambda i: (i, 0, 0))],
        out_specs=pl.BlockSpec((1, D, n), lambda i: (i, 0, 0)),
        out_shape=jax.ShapeDtypeStruct((DEPTH, D, n), f32), compiler_params=_params("arbitrary"),
    )(c_t, dm)


def _adam_math(w, g, m, v):
    m2 = B1 * m + (1.0 - B1) * g
    v2 = B2 * v + (1.0 - B2) * (g * g)
    mh = m2 / (1.0 - B1 ** STEP)
    vh = v2 / (1.0 - B2 ** STEP)
    return -LR * (mh / (jnp.sqrt(vh) + ADAM_EPS) + WD * w), m2, v2


def adamw(w, m, v, g, *, name):
    r, c = w.shape
    tr = 256 if r % 256 == 0 else r

    def body(w_ref, m_ref, v_ref, g_ref, d_ref, m2_ref, v2_ref):
        d_ref[...], m2_ref[...], v2_ref[...] = _adam_math(w_ref[...], g_ref[...], m_ref[...], v_ref[...])

    spec = pl.BlockSpec((tr, c), lambda i: (i, 0))
    return _pcall(
        body, name=name, grid=(r // tr,), in_specs=[spec] * 4, out_specs=[spec] * 3,
        out_shape=[jax.ShapeDtypeStruct((r, c), f32)] * 3, compiler_params=_params("parallel"),
    )(w, m, v, g)


def adamw_parts(w, m, v, own, sib, layer, prev=None, *, name):
    c = w.shape[1]
    r = own.shape[1]
    tr = 256 if r % 256 == 0 else r // 2
    off = layer * (r // tr)

    def body(w_ref, m_ref, v_ref, own_ref, sib_ref, *rest):
        g_ref, d_ref, m2_ref, v2_ref = rest[-4:]

        def total(ref):
            return ((ref[0].astype(f32) + ref[1].astype(f32)) + ref[2].astype(f32)) + ref[3].astype(f32)

        g = total(own_ref) + total(sib_ref)
        g_ref[...] = g
        d_ref[...], m2_ref[...], v2_ref[...] = _adam_math(w_ref[...], g, m_ref[...], v_ref[...])

    spec = pl.BlockSpec((tr, c), lambda i: (off + i, 0))
    pspec = pl.BlockSpec((4, tr, c), lambda i: (0, i, 0))
    prev = () if prev is None else tuple(prev)
    return _pcall(
        body, name=name, grid=(r // tr,), in_specs=[spec] * 3 + [pspec] * 2 + [ANY] * len(prev), out_specs=[spec] * 4,
        out_shape=[jax.ShapeDtypeStruct(w.shape, f32)] * 4,
        input_output_aliases={5 + q: q for q in range(len(prev))}, compiler_params=_params("parallel"),
    )(w, m, v, own, sib, *prev)


def sum_devices(g, *, name):
    _, r, c = g.shape

    def body(g_ref, o_ref):
        acc = g_ref[0]
        for k in range(1, N_DEV):
            acc = acc + g_ref[k]
        o_ref[...] = acc

    return _pcall(body, name=name, out_shape=jax.ShapeDtypeStruct((r, c), f32))(g)


def _place():
    x, y, c = lax.axis_index("x"), lax.axis_index("y"), lax.axis_index("c")
    chips = [(1 - x, y), (x, 1 - y), (1 - x, 1 - y)]
    return x, y, c, chips


def allgather8(v, *, name):
    r, c_ = v.shape

    def body(v_ref, o_ref, send_sems, recv_sems, local_sem):
        x, y, c, _ = _place()
        me = 4 * x + 2 * y + c
        mine = pltpu.make_async_copy(v_ref, o_ref.at[me], local_sem)
        mine.start()
        flips = [(fx, fy, fc) for fx in (0, 1) for fy in (0, 1) for fc in (0, 1)][1:]

        def peer(f):
            return (x ^ f[0], y ^ f[1], c ^ f[2])

        def copy(k, slot, to):
            return pltpu.make_async_remote_copy(
                src_ref=v_ref, dst_ref=o_ref.at[slot], send_sem=send_sems.at[k], recv_sem=recv_sems.at[k],
                device_id=to, device_id_type=MESH)

        sends = [copy(k, me, peer(f)) for k, f in enumerate(flips)]
        for cp in sends:
            cp.start()
        for k, f in enumerate(flips):
            px, py, pc = peer(f)
            copy(k, 4 * px + 2 * py + pc, (x, y, c)).wait_recv()
        for cp in sends:
            cp.wait_send()
        mine.wait()

    return _pcall(
        body, name=name, in_specs=[ANY], out_specs=ANY, out_shape=jax.ShapeDtypeStruct((N_DEV, r, c_), v.dtype),
        scratch_shapes=[pltpu.SemaphoreType.DMA((7,)), pltpu.SemaphoreType.DMA((7,)), pltpu.SemaphoreType.DMA],
    )(v)


def gather_weights(shards, *, name):
    n = len(shards)

    def body(*refs):
        src, out = refs[:n], refs[n:2 * n]
        send_a, recv_a, send_f, recv_f, local_sems = refs[2 * n:]
        x, y, c, chips = _place()
        sib = (x, y, 1 - c)
        me = 2 * x + y
        locals_ = [pltpu.make_async_copy(src[a], out[a].at[me], local_sems.at[a]) for a in range(n)]
        for cp in locals_:
            cp.start()

        def half(a, which):
            rh = src[a].shape[0] // 2
            return pl.ds(which * rh, rh)

        def first(a, k, chip_from, to):
            slot = 2 * chip_from[0] + chip_from[1]
            s_ref = src[a].at[half(a, c)]
            return pltpu.make_async_remote_copy(
                src_ref=s_ref, dst_ref=out[a].at[slot, half(a, c)], send_sem=send_a.at[3 * a + k],
                recv_sem=recv_a.at[3 * a + k], device_id=to, device_id_type=MESH)

        def passed(a, k, chip_from, which, to):
            slot = 2 * chip_from[0] + chip_from[1]
            ref = out[a].at[slot, half(a, which)]
            return pltpu.make_async_remote_copy(
                src_ref=ref, dst_ref=ref, send_sem=send_f.at[3 * a + k], recv_sem=recv_f.at[3 * a + k],
                device_id=to, device_id_type=MESH)

        sends = [first(a, k, (x, y), (*chip, c)) for a in range(n) for k, chip in enumerate(chips)]
        for cp in sends:
            cp.start()
        fwd = []
        for a in range(n):
            for k, chip in enumerate(chips):
                first(a, k, chip, (x, y, c)).wait_recv()
                cp = passed(a, k, chip, c, sib)
                cp.start()
                fwd.append(cp)
        for a in range(n):
            for k, chip in enumerate(chips):
                passed(a, k, chip, 1 - c, (x, y, c)).wait_recv()
        for cp in sends + fwd:
            cp.wait_send()
        for cp in locals_:
            cp.wait()

    return _pcall(
        body, name=name, in_specs=[ANY] * n, out_specs=[ANY] * n,
        out_shape=[jax.ShapeDtypeStruct((4,) + tuple(sh.shape), sh.dtype) for sh in shards],
        scratch_shapes=[pltpu.SemaphoreType.DMA((3 * n,)) for _ in range(4)] + [pltpu.SemaphoreType.DMA((n,))],
    )(*shards)


def _gather_direct(src, out, send_sems, recv_sems, local_sems):
    n = len(src)
    x, y, c, chips = _place()
    me = 2 * x + y

    def copy(a, k, slot, to):
        return pltpu.make_async_remote_copy(
            src_ref=src[a], dst_ref=out[a].at[slot], send_sem=send_sems.at[3 * a + k], recv_sem=recv_sems.at[3 * a + k],
            device_id=to, device_id_type=MESH)

    def start():
        for a in range(n):
            pltpu.make_async_copy(src[a], out[a].at[me], local_sems.at[a]).start()
            for k, chip in enumerate(chips):
                copy(a, k, me, (*chip, c)).start()

    def finish():
        for a in range(n):
            for k, chip in enumerate(chips):
                copy(a, k, 2 * chip[0] + chip[1], (x, y, c)).wait_recv()
        for a in range(n):
            for k in range(3):
                copy(a, k, me, (x, y, c)).wait_send()
            pltpu.make_async_copy(src[a], out[a].at[me], local_sems.at[a]).wait()

    return start, finish


def _gather_scratch(n):
    return [pltpu.SemaphoreType.DMA((3 * n,)), pltpu.SemaphoreType.DMA((3 * n,)), pltpu.SemaphoreType.DMA((n,))]


def _exchange(src, own, sibo, send_sems, recv_sems, local_sems):
    n = len(src)
    x, y, c, chips = _place()
    sib = (x, y, 1 - c)
    me = 2 * x + y

    def slot(chip):
        return 2 * chip[0] + chip[1]

    def copy(a, k, s_ref, d_ref, to):
        return pltpu.make_async_remote_copy(
            src_ref=s_ref, dst_ref=d_ref, send_sem=send_sems.at[7 * a + k], recv_sem=recv_sems.at[7 * a + k],
            device_id=to, device_id_type=MESH)

    def start():
        for a in range(n):
            pltpu.make_async_copy(src[a].at[me], own[a].at[me], local_sems.at[a]).start()
            copy(a, 0, src[a].at[me], sibo[a].at[me], sib).start()
            for k, chip in enumerate(chips):
                copy(a, 1 + k, src[a].at[slot(chip)], own[a].at[me], (*chip, c)).start()

    def forward():
        for a in range(n):
            for k, chip in enumerate(chips):
                copy(a, 1 + k, src[a].at[me], own[a].at[slot(chip)], (x, y, c)).wait_recv()
                copy(a, 4 + k, own[a].at[slot(chip)], sibo[a].at[slot(chip)], sib).start()

    def finish():
        for a in range(n):
            copy(a, 0, src[a].at[me], sibo[a].at[me], (x, y, c)).wait_recv()
            for k, chip in enumerate(chips):
                copy(a, 4 + k, src[a].at[me], sibo[a].at[slot(chip)], (x, y, c)).wait_recv()
        for a in range(n):
            for k in range(7):
                copy(a, k, src[a].at[me], own[a].at[me], (x, y, c)).wait_send()
            pltpu.make_async_copy(src[a].at[me], own[a].at[me], local_sems.at[a]).wait()

    return start, forward, finish


def _exchange_scratch(n):
    return [pltpu.SemaphoreType.DMA((7 * n,)), pltpu.SemaphoreType.DMA((7 * n,)), pltpu.SemaphoreType.DMA((n,))]


def exchange_grads(parts, *, name):
    n = len(parts)

    def body(*refs):
        start, forward, finish = _exchange(refs[:n], refs[n:2 * n], refs[2 * n:3 * n], *refs[3 * n:])
        start()
        forward()
        finish()

    shapes = [jax.ShapeDtypeStruct(p.shape, p.dtype) for p in parts]
    res = _pcall(body, name=name, in_specs=[ANY] * n, out_specs=[ANY] * (2 * n), out_shape=shapes + shapes,
                 scratch_shapes=_exchange_scratch(n))(*parts)
    return res[:n], res[n:]


def _natural(g, how):
    if how == "col":
        return jnp.moveaxis(g, 0, 1).reshape(g.shape[1], 4 * g.shape[2])
    return g.reshape(4 * g.shape[1], g.shape[2])


def _chunks(gw, how):
    k, n = gw.shape
    if how == "col":
        return jnp.moveaxis(gw.reshape(k, 4, n // 4), 1, 0).astype(bf16)
    return gw.reshape(4, k // 4, n).astype(bf16)


def kernel(x, c, ada_w, ada_b, norm_mix, norm_ffn, ffn_w_in, ffn_w_out, a_w_in, a_w_out, a_sink, b_w_in, b_w_out, final_norm, loss_target, m_ada_w, m_ada_b, m_norm_mix, m_norm_ffn, m_ffn_w_in, m_ffn_w_out, m_a_w_in, m_a_w_out, m_a_sink, m_b_w_in, m_b_w_out, m_final_norm, v_ada_w, v_ada_b, v_norm_mix, v_norm_ffn, v_ffn_w_in, v_ffn_w_out, v_a_w_in, v_a_w_out, v_a_sink, v_b_w_in, v_b_w_out, v_final_norm):
    s = x.shape[1]
    xi, yi, ci = lax.axis_index("x"), lax.axis_index("y"), lax.axis_index("c")
    chip = 2 * xi + yi
    dev = 2 * chip + ci
    x0 = x[0]
    tgt = loss_target[0]

    big = {"ffn_w_in": (ffn_w_in, "col"), "ffn_w_out": (ffn_w_out, "row"), "a_w_in": (a_w_in, "col"),
           "a_w_out": (a_w_out, "row"), "b_w_in": (b_w_in, "col"), "b_w_out": (b_w_out, "col")}
    names = list(big)

    def layer_keys(i):
        mix = "a" if i % 2 == 0 else "b"
        return [("ffn_w_in", i), ("ffn_w_out", i), (mix + "_w_in", i // 2), (mix + "_w_out", i // 2)]

    def shards_of(i):
        return [big[k][0][l].astype(bf16) for k, l in layer_keys(i)]

    def weights_of(i, gathered):
        return {k: (g if k == "ffn_w_in" else _natural(g, big[k][1])) for (k, _), g in zip(layer_keys(i), gathered)}

    mix0 = gather_weights(shards_of(0)[2:], name="gather_weights")

    c_all = allgather8(jnp.broadcast_to(c, (8, D)), name="gather_c")[:, 0, :]
    nsh = ada_w.shape[2]
    ada_b_sh = lax.dynamic_slice_in_dim(ada_b, chip * nsh, nsh, axis=1)[:, None, :]
    mod_part = ada_mod(c_all, ada_w, ada_b_sh, name="ada_mod")
    mod_all = allgather8(mod_part.reshape(DEPTH * N_DEV, nsh), name="gather_mod")
    mod_all = mod_all.reshape(4, 2, DEPTH, N_DEV, nsh)[:, 0]
    mod = lax.dynamic_index_in_dim(mod_all, dev, axis=2, keepdims=False)
    mod = jnp.moveaxis(mod, 0, 1).reshape(DEPTH, 6, 1, D)

    cfg_a = _Attn(s, mixer="a")
    cfg_b = [_Attn(s, mixer="b", group=g) for g in range(3)]
    no_sink = jnp.full((1, 2, 1, GQ * 64), NEG, f32)

    saved = []
    xc = x0
    for i in range(DEPTH):
        j = i // 2
        sh1, sc1, g1, sh2, sc2, g2 = (mod[i, q] for q in range(6))
        nmix, nffn = norm_mix[i][None, :], norm_ffn[i][None, :]
        mix = "a" if i % 2 == 0 else "b"
        if i == 0:
            h, qkv = mm_norm(xc, nmix, sc1, sh1, _natural(mix0[0], "col"), name="a_qkv")
            sinkcol = jnp.repeat(a_sink[j].reshape(2, 2, GQ), 128, axis=2)[:, :, None, :]
            o, lse, *ffn0 = attn_fwd(qkv[None], sinkcol, cfg_a, shards_of(0)[:2], out_dtype=bf16, name="a_attn_fwd_gather")
            o, lse = o[0], lse[0]
            wl = [weights_of(0, ffn0 + list(mix0))]
        elif i % 2 == 0:
            h, qkv = mm_norm(xc, nmix, sc1, sh1, wl[i]["a_w_in"], name="a_qkv")
            sinkcol = jnp.repeat(a_sink[j].reshape(2, 2, GQ), 128, axis=2)[:, :, None, :]
            o, lse = (t[0] for t in attn_fwd(qkv[None], sinkcol, cfg_a, out_dtype=bf16, name="a_attn_fwd"))
        else:
            h, qkv = mm_norm(xc, nmix, sc1, sh1, wl[i]["b_w_in"], name="b_qkv")
            qkv = [qkv[None]] + list(b_to_strided(qkv, name="b_to_strided"))
            outs = [attn_fwd(qkv[g], no_sink, cfg_b[g], out_dtype=f32, name=f"b_attn_fwd{g}") for g in range(3)]
            o, lse = attn_merge([t[0] for t in outs], [t[1] for t in outs], name="b_merge")
        x1 = mm_resid(o, wl[i][mix + "_w_out"], xc, g1, name=mix + "_out")
        nxt = shards_of(i + 1) if i + 1 < DEPTH else []
        h2, gu, act, x2, *got = ffn_fwd(x1, nffn, sc2, sh2, g2, wl[i]["ffn_w_in"], wl[i]["ffn_w_out"], nxt,
                                        name="ffn_fwd_gather" if nxt else "ffn_fwd")
        if nxt:
            wl.append(weights_of(i + 1, got))
        saved.append((xc, h, qkv, o, lse, x1, h2, gu, act))
        xc = x2

    dx, st_final = loss_head(xc, final_norm[None, :], tgt, name="loss_head")

    zero_row = jnp.zeros((1, D), f32)
    dmod_rows = [None] * DEPTH
    d_nmix, d_nffn = [None] * DEPTH, [None] * DEPTH
    d_sink = [None] * 2
    parts, exchanged = None, {}
    for i in reversed(range(DEPTH)):
        j = i // 2
        xin, h, qkv, o, lse, x1, h2, gu, act = saved[i]
        sh1, sc1, g1, sh2, sc2, g2 = (mod[i, q] for q in range(6))
        nmix, nffn = norm_mix[i][None, :], norm_ffn[i][None, :]
        mix = "a" if i % 2 == 0 else "b"
        w_fo, w_o, w_i = wl[i]["ffn_w_out"], wl[i][mix + "_w_out"], wl[i][mix + "_w_in"]
        dgu, dx1, st2, own_, sib_ = ffn_bwd_rows(dx, x1, gu, g2, nffn, sc2, wl[i]["ffn_w_in"], w_fo, parts or [],
                                                 name="ffn_bwd_rows_exchange" if parts else "ffn_bwd_rows")
        if parts:
            exchanged[i + 1] = (own_, sib_)
        gwo, dg2 = ffn_dw_out(act, dx, g2, w_fo, name="ffn_dw_out")
        dg2 = dg2[0:1]
        gwi = ffn_dw_in(h2, dgu, name="ffn_dw_in")
        gmo, dg1 = mm_tn(o, dx1, (g1, w_o), name=mix + "_dw_out")
        ffn_parts = [gwi, gwo.reshape(4, F // 4, D)]
        if i % 2 == 0:
            sinkrow = jnp.pad(a_sink[j].reshape(2, 8), ((0, 0), (0, 120))).reshape(1, 256)
            do, delta, dsk = mm_nt_delta(dx1, g1, w_o, o, lse, sinkrow, name="a_do")
            d_sink[j] = dsk[0].reshape(2, 128)[:, :8].reshape(16)
            dq, dk, dv, *ffn_x = attn_bwd(qkv[None], do[None], lse[None], delta[None], cfg_a, ffn_parts if i == 0 else [],
                                          name="a_attn_bwd_exchange" if i == 0 else "a_attn_bwd")
            dqkv = [dq[0], dk[0], dv[0]]
        else:
            do, delta, _ = mm_nt_delta(dx1, g1, w_o, o, lse, jnp.zeros((1, 128), f32), name="b_do")
            st = [do[None], lse[None], delta[None]] + list(b_bwd_to_strided(do, lse, delta, name="b_bwd_to_strided"))
            gr = [attn_bwd(qkv[g], *st[3 * g:3 * g + 3], cfg_b[g], name=f"b_attn_bwd{g}") for g in range(3)]
            dqkv = b_from_strided(gr, name="b_from_strided")
        gmi = mm_tn(h, dqkv, name=mix + "_dw_in")
        dx, st1 = mm_nt_norm_bwd(dqkv, w_i, xin, dx1, nmix, sc1, name=mix + "_dh")
        dmod_rows[i] = jnp.concatenate([st1[2:3], st1[1:2], dg1, st2[2:3], st2[1:2], dg2], axis=0)
        d_nmix[i], d_nffn[i] = st1[0:1], st2[0:1]
        mix_parts = [_chunks(gmi, big[mix + "_w_in"][1]), _chunks(gmo, big[mix + "_w_out"][1])]
        parts = ffn_parts + mix_parts
    own_m, sib_m = exchange_grads(mix_parts, name="exchange_grads")
    exchanged[0] = (list(ffn_x[0]) + list(own_m), list(ffn_x[1]) + list(sib_m))

    sink_row = jnp.pad(jnp.concatenate(d_sink), (0, D - 32))[None, :]
    stats = jnp.concatenate(dmod_rows + d_nmix + d_nffn + [sink_row, st_final[0:1], st_final[1:2]]
                            + [zero_row] * (STAT_ROWS - 35), axis=0)
    stats_all = allgather8(stats, name="gather_stats")
    tot = sum_devices(stats_all, name="sum_stats")
    loss = 0.5 * jnp.sum(tot[34]) / float(D)

    def pack(ab, nm, nf, sk, fnm, fill):
        return jnp.concatenate([ab.reshape(24, D), nm, nf, jnp.pad(sk.reshape(1, 32), ((0, 0), (0, D - 32)), constant_values=fill),
                                fnm[None, :], jnp.full((STAT_ROWS - 34, D), fill, f32)], axis=0)

    sd, sm, sv = adamw(pack(ada_b, norm_mix, norm_ffn, a_sink, final_norm, 0.0),
                       pack(m_ada_b, m_norm_mix, m_norm_ffn, m_a_sink, m_final_norm, 0.0),
                       pack(v_ada_b, v_norm_mix, v_norm_ffn, v_a_sink, v_final_norm, 1.0), tot, name="adamw_small")

    def unpack(p):
        return p[0:24].reshape(DEPTH, 6 * D), p[24:28], p[28:32], p[32, :32].reshape(2, 16), p[33]

    small = {"grad": unpack(tot), "delta": unpack(sd), "m": unpack(sm), "v": unpack(sv)}

    dmod_all = stats_all[:, 0:24, :].reshape(N_DEV, DEPTH, 6 * D)
    dm_sh = jnp.moveaxis(lax.dynamic_slice_in_dim(dmod_all, chip * nsh, nsh, axis=2), 0, 1)
    g_ada = ada_grad(c_all.T, dm_sh, name="ada_grad")
    r_ada = (DEPTH * D, nsh)
    ada_res = adamw(ada_w.reshape(r_ada), m_ada_w.reshape(r_ada), v_ada_w.reshape(r_ada), g_ada.reshape(r_ada), name="adamw_ada")
    ada_out = [g_ada] + [t.reshape(ada_w.shape) for t in ada_res]

    mom = {"ffn_w_in": (m_ffn_w_in, v_ffn_w_in), "ffn_w_out": (m_ffn_w_out, v_ffn_w_out), "a_w_in": (m_a_w_in, v_a_w_in),
           "a_w_out": (m_a_w_out, v_a_w_out), "b_w_in": (m_b_w_in, v_b_w_in), "b_w_out": (m_b_w_out, v_b_w_out)}
    big_out = {k: None for k in names}
    for i in reversed(range(DEPTH)):
        own_, sib_ = exchanged[i]
        for (k, l), o_, s_ in zip(layer_keys(i), own_, sib_):
            w = big[k][0]
            r2 = (-1, w.shape[-1])
            big_out[k] = adamw_parts(w.reshape(r2), mom[k][0].reshape(r2), mom[k][1].reshape(r2), o_, s_, l, big_out[k],
                                     name=f"adamw_{k}{l}")
    big_out = {k: [t.reshape(big[k][0].shape) for t in big_out[k]] for k in names}

    def leaves(q):
        sm_ = small[("grad", "delta", "m", "v")[q]]
        return (ada_out[q], sm_[0], sm_[1], sm_[2], big_out["ffn_w_in"][q], big_out["ffn_w_out"][q], big_out["a_w_in"][q],
                big_out["a_w_out"][q], sm_[3], big_out["b_w_in"][q], big_out["b_w_out"][q], sm_[4])

    return (loss, dx[None], *leaves(0), *leaves(1), *leaves(2), *leaves(3))
```
